```python
import math
import jax, jax.numpy as jnp
from jax import lax
import numpy as np

D_MODEL = 2048
BATCH = 2
SEQ = 4096
DEPTH = 4
DEC_BATCH = 32
DEC_SEQ = 8
PAST_LEN = 16384
PAGE_SIZE = 128

N_EVEN = (DEPTH + 1) // 2
N_ODD = DEPTH // 2

A_WIDTH = D_MODEL // 2
A_HEADS = 8
A_KDIM = 128
A_VDIM = A_WIDTH // A_HEADS
A_QK = A_HEADS * A_KDIM
A_CHUNK = 64
B_HEADS = 16
B_HEAD_DIM = 64
B_KV_HEADS = 4
B_GROUP = B_HEADS // B_KV_HEADS
B_WIDTH = B_HEADS * B_HEAD_DIM
B_KV_WIDTH = B_KV_HEADS * B_HEAD_DIM
WINDOW = 128
N_BUCKETS = 32
MAX_DISTANCE = 128
MASK_VALUE = -1e30
IN_A = 2 * A_QK + 2 * A_WIDTH
IN_EVEN = IN_A + B_WIDTH + 2 * B_KV_WIDTH
EVEN_SPLITS = (A_QK, 2 * A_QK, 2 * A_QK + A_WIDTH, IN_A, IN_A + B_WIDTH, IN_A + B_WIDTH + B_KV_WIDTH)
MIX_WIDTH = A_WIDTH + B_WIDTH
C_HEAD = 64
C_HEADS = D_MODEL // C_HEAD
LORA_DECAY = 96
LORA_AAA = 96
LORA_MV = 64
LORA_GATE = 256
GN_EPS = 64e-5
D_FF = 4 * D_MODEL
NORM_EPS = 1e-6

kernel_name = "hgrn2_swa_sink_rwkv7_hybrid_step"

F32 = jnp.float32


def rmsnorm(x, g):
    xf = x.astype(F32)
    y = xf * lax.rsqrt(jnp.mean(xf * xf, axis=-1, keepdims=True) + NORM_EPS)
    return (y * g.astype(F32)).astype(x.dtype)


def hgrn_lower_bounds(lb_raw):
    p = jax.nn.softmax(lb_raw.astype(F32), axis=0)
    return jnp.cumsum(p, axis=0) - p[0]


def gla_chunked(q, k, v, logf, s0):
    B, L, H, K = q.shape
    C = math.gcd(L, A_CHUNK)
    N = L // C
    def to_chunks(t):
        return t.reshape(B, N, C, H, t.shape[-1]).transpose(1, 0, 3, 2, 4)
    causal = jnp.tril(jnp.ones((C, C), bool))[:, :, None]
    def step(S, inp):
        qi, ki, vi, gi = inp
        G = jnp.cumsum(gi, axis=2)
        diff = G[:, :, :, None, :] - G[:, :, None, :, :]
        decay = jnp.where(causal, jnp.exp(jnp.minimum(diff, 0.0)), 0.0)
        attn = jnp.einsum("bhik,bhjk,bhijk->bhij", qi, ki, decay)
        o = jnp.einsum("bhij,bhjv->bhiv", attn, vi) + jnp.einsum("bhik,bhkv->bhiv", qi * jnp.exp(G), S)
        g_last = G[:, :, -1, :]
        S_new = jnp.exp(g_last)[..., None] * S + jnp.einsum("bhjk,bhjv->bhkv", ki * jnp.exp(g_last[:, :, None, :] - G), vi)
        return S_new, o
    S, o = lax.scan(step, s0, (to_chunks(q), to_chunks(k), to_chunks(v), to_chunks(logf)))
    o = o.transpose(1, 0, 3, 2, 4).reshape(B, L, H, v.shape[-1])
    return o, S


def t5_bucket(dist):
    max_exact = N_BUCKETS // 2
    d = np.maximum(dist, 0)
    large = max_exact + (np.log(np.maximum(d, max_exact).astype(np.float32) / max_exact)
                         / math.log(MAX_DISTANCE / max_exact) * (N_BUCKETS - max_exact)).astype(np.int32)
    large = np.minimum(large, N_BUCKETS - 1)
    return np.where(d < max_exact, d, large).astype(np.int32)


def swa_sinks(q, k_all, v_all, pos0, rel_bias, sinks):
    B, L, H, D = q.shape
    P = k_all.shape[1] - L
    pad = WINDOW - P
    k_pad = jnp.pad(k_all, ((0, 0), (pad, 0), (0, 0), (0, 0)))
    v_pad = jnp.pad(v_all, ((0, 0), (pad, 0), (0, 0), (0, 0)))
    QB = math.gcd(L, WINDOW)
    NB = L // QB
    span = WINDOW + QB
    idx = np.arange(NB)[:, None] * QB + np.arange(span)[None, :]
    kb = k_pad[:, idx]
    vb = v_pad[:, idx]
    qb = q.reshape(B, NB, QB, B_KV_HEADS, B_GROUP, D)
    dist = np.arange(QB)[:, None] + WINDOW - np.arange(span)[None, :]
    key_pos = pos0 - WINDOW + idx
    valid = (dist >= 0)[None] & (dist < WINDOW)[None] & (key_pos[:, None, :] >= 0)
    bias = rel_bias.astype(F32)[t5_bucket(dist)]
    bias = bias.transpose(2, 0, 1).reshape(B_KV_HEADS, B_GROUP, QB, span)
    s = jnp.einsum("bnqkgd,bnskd->bnkgqs", qb, kb).astype(F32) * (D ** -0.5) + bias
    s = jnp.where(valid[None, :, None, None], s, MASK_VALUE)
    sink = sinks.astype(F32).reshape(B_KV_HEADS, B_GROUP)[None, None, :, :, None, None]
    m = jnp.maximum(jnp.max(s, axis=-1, keepdims=True), sink)
    p = jnp.exp(s - m)
    p = p / (jnp.sum(p, axis=-1, keepdims=True) + jnp.exp(sink - m))
    o = jnp.einsum("bnkgqs,bnskd->bnqkgd", p.astype(vb.dtype), vb)
    return o.reshape(B, L, H * D)


def even_mixer(h, lb, s0, k_past, v_past, pos0, w_in, a_norm_g, rel_bias, sinks, w_out):
    B, L, _ = h.shape
    proj = h @ w_in
    q_a, f_a, i_a, g_a, q_b, k_b, v_b = jnp.split(proj, EVEN_SPLITS, axis=-1)
    fq = f_a.astype(F32).reshape(B, L, A_HEADS, A_KDIM)
    lbh = lb.reshape(A_HEADS, A_KDIM)
    f = lbh + (1.0 - lbh) * jax.nn.sigmoid(fq)
    logf = jnp.log(f)
    kk = (1.0 - lbh) * jax.nn.sigmoid(-fq)
    qq = jax.nn.silu(q_a.astype(F32)).reshape(B, L, A_HEADS, A_KDIM) * (A_KDIM ** -0.5)
    vv = i_a.astype(F32).reshape(B, L, A_HEADS, A_VDIM)
    o_a, s_new = gla_chunked(qq, kk, vv, logf, s0.astype(F32))
    o_a = (rmsnorm(o_a.reshape(B, L, A_WIDTH), a_norm_g) * jax.nn.silu(g_a.astype(F32))).astype(h.dtype)
    qb = q_b.reshape(B, L, B_HEADS, B_HEAD_DIM)
    k_all = jnp.concatenate([k_past.astype(h.dtype), k_b.reshape(B, L, B_KV_HEADS, B_HEAD_DIM)], axis=1)
    v_all = jnp.concatenate([v_past.astype(h.dtype), v_b.reshape(B, L, B_KV_HEADS, B_HEAD_DIM)], axis=1)
    o_b = swa_sinks(qb, k_all, v_all, pos0, rel_bias, sinks).astype(h.dtype)
    out = jnp.concatenate([o_a, o_b], axis=-1) @ w_out
    return out, s_new, k_all[:, -WINDOW:], v_all[:, -WINDOW:]


def rwkv7_mixer(h, shift0, S0, v_first, vres, mu, wr, wk, wv, wo, w0, w1, w2, a0, a1, a2, g1, g2, k_k, k_a, r_k, lnx_g, lnx_b):
    B, L, D = h.shape
    H, N = C_HEADS, C_HEAD
    x_prev = jnp.concatenate([shift0[:, None, :].astype(h.dtype), h[:, :-1]], axis=1)
    xx = x_prev - h
    xr, xw, xk, xv, xa, xg = [h + xx * mu[i] for i in range(6)]
    r = xr @ wr
    k = xk @ wk
    v = xv @ wv
    v_layer = v
    if vres is not None:
        v0, v1, v2 = vres
        v = v + (v_first - v) * jax.nn.sigmoid(v0 + (xv @ v1) @ v2)
    w_log = -jax.nn.softplus(-(w0 + jnp.tanh(xw @ w1) @ w2).astype(F32)) - 0.5
    a = jax.nn.sigmoid((a0 + (xa @ a1) @ a2).astype(F32))
    g = jax.nn.sigmoid(xg @ g1) @ g2
    heads = lambda t: t.astype(F32).reshape(B, L, H, N)
    rh, vh, ah = heads(r), heads(v), heads(a)
    kkh = heads(k * k_k)
    kkh = kkh * lax.rsqrt(jnp.maximum(jnp.sum(kkh * kkh, axis=-1, keepdims=True), 1e-24))
    kh = heads(k) * (1.0 + (ah - 1.0) * k_a.astype(F32).reshape(H, N))
    decay = jnp.exp(-jnp.exp(w_log)).reshape(B, L, H, N)
    def step(S, inp):
        r_t, w_t, k_t, v_t, kk_t, a_t = inp
        sa = jnp.einsum("bhij,bhj->bhi", S, -kk_t)
        S = S * w_t[:, :, None, :] + sa[..., None] * (kk_t * a_t)[:, :, None, :] + v_t[..., None] * k_t[:, :, None, :]
        return S, jnp.einsum("bhij,bhj->bhi", S, r_t)
    seq = lambda t: jnp.moveaxis(t, 1, 0)
    S_new, y = lax.scan(step, S0.astype(F32), (seq(rh), seq(decay), seq(kh), seq(vh), seq(kkh), seq(ah)))
    y = jnp.moveaxis(y, 0, 1)
    mean = jnp.mean(y, axis=-1, keepdims=True)
    var = jnp.mean(jnp.square(y - mean), axis=-1, keepdims=True)
    y = ((y - mean) * lax.rsqrt(var + GN_EPS)).reshape(B, L, D) * lnx_g.astype(F32) + lnx_b.astype(F32)
    y = y + (jnp.sum(rh * kh * r_k.astype(F32), axis=-1, keepdims=True) * vh).reshape(B, L, D)
    out = (y.astype(h.dtype) * g) @ wo
    return out, S_new, h[:, -1], v_layer


def trunk(x, st_hgrn, k_cache, v_cache, st_rwkv, st_shift, pos0, p):
    lbs = hgrn_lower_bounds(p["hgrn_lb_raw"])
    hgrn_out, k_out, v_out, rwkv_out, shift_out = [], [], [], [], []
    v_first = None
    for layer in range(DEPTH):
        h = rmsnorm(x, p["norm_mix_pre"][layer])
        if layer % 2 == 0:
            e = layer // 2
            mix, s_new, k_new, v_new = even_mixer(h, lbs[e], st_hgrn[e], k_cache[e], v_cache[e], pos0,
                                                  p["w_in_even"][e], p["hgrn_norm_g"][e], p["rel_bias"],
                                                  p["attn_sinks"][e], p["w_out_even"][e])
            hgrn_out.append(s_new.astype(st_hgrn.dtype))
            k_out.append(k_new.astype(k_cache.dtype))
            v_out.append(v_new.astype(v_cache.dtype))
        else:
            o = layer // 2
            vres = None if o == 0 else (p["rw_v0"][o - 1], p["rw_v1"][o - 1], p["rw_v2"][o - 1])
            mix, S_new, sh_new, v_layer = rwkv7_mixer(
                h, st_shift[o], st_rwkv[o], v_first, vres, p["rw_mu"][o], p["rw_wr"][o], p["rw_wk"][o],
                p["rw_wv"][o], p["rw_wo"][o], p["rw_w0"][o], p["rw_w1"][o], p["rw_w2"][o], p["rw_a0"][o],
                p["rw_a1"][o], p["rw_a2"][o], p["rw_g1"][o], p["rw_g2"][o], p["rw_kk"][o], p["rw_ka"][o],
                p["rw_rk"][o], p["rw_lnx_g"][o], p["rw_lnx_b"][o])
            if o == 0:
                v_first = v_layer
            rwkv_out.append(S_new.astype(st_rwkv.dtype))
            shift_out.append(sh_new.astype(st_shift.dtype))
        x = x + rmsnorm(mix, p["norm_mix_post"][layer])
        u = rmsnorm(x, p["norm_ffn_pre"][layer])
        u = jnp.square(jax.nn.relu(u @ p["w_up"][layer])) @ p["w_down"][layer]
        x = x + rmsnorm(u, p["norm_ffn_post"][layer])
    return x, jnp.stack(hgrn_out), jnp.stack(k_out), jnp.stack(v_out), jnp.stack(rwkv_out), jnp.stack(shift_out)


def setup_inputs(seed: int = 0) -> dict:
    key = jax.random.key(seed)
    keys = jax.random.split(key, 64)
    counter = [0]
    def nk():
        k = keys[counter[0]]
        counter[0] += 1
        return k
    def nrm(shape, scale=1.0):
        return jax.random.normal(nk(), shape, F32) * scale
    def gain(shape):
        return 1.0 + nrm(shape, 0.05)
    D = D_MODEL
    return {
        "x_prompt": nrm((BATCH, SEQ, D)),
        "x_sample": nrm((DEC_BATCH, DEC_SEQ, D)),
        "state_hgrn": nrm((N_EVEN, DEC_BATCH, A_HEADS, A_KDIM, A_VDIM), 0.5),
        "cache_swa_k": nrm((N_EVEN, DEC_BATCH, WINDOW, B_KV_HEADS, B_HEAD_DIM)),
        "cache_swa_v": nrm((N_EVEN, DEC_BATCH, WINDOW, B_KV_HEADS, B_HEAD_DIM)),
        "state_rwkv": nrm((N_ODD, DEC_BATCH, C_HEADS, C_HEAD, C_HEAD), 0.5),
        "state_shift": nrm((N_ODD, DEC_BATCH, D)),
        "norm_mix_pre": gain((DEPTH, D)),
        "norm_mix_post": gain((DEPTH, D)),
        "norm_ffn_pre": gain((DEPTH, D)),
        "norm_ffn_post": gain((DEPTH, D)),
        "w_in_even": nrm((N_EVEN, D, IN_EVEN), D ** -0.5),
        "w_out_even": nrm((N_EVEN, MIX_WIDTH, D), MIX_WIDTH ** -0.5),
        "hgrn_lb_raw": nrm((N_EVEN, A_QK), 0.5),
        "hgrn_norm_g": gain((N_EVEN, A_WIDTH)),
        "rel_bias": nrm((N_BUCKETS, B_HEADS), 0.5),
        "attn_sinks": nrm((N_EVEN, B_HEADS), 0.5),
        "rw_mu": jax.random.uniform(nk(), (N_ODD, 6, D), F32),
        "rw_wr": nrm((N_ODD, D, D), D ** -0.5),
        "rw_wk": nrm((N_ODD, D, D), D ** -0.5),
        "rw_wv": nrm((N_ODD, D, D), D ** -0.5),
        "rw_wo": nrm((N_ODD, D, D), D ** -0.5),
        "rw_w0": jax.random.uniform(nk(), (N_ODD, D), F32, -6.5, -1.5),
        "rw_w1": nrm((N_ODD, D, LORA_DECAY), D ** -0.5),
        "rw_w2": nrm((N_ODD, LORA_DECAY, D), 0.1 * LORA_DECAY ** -0.5),
        "rw_a0": nrm((N_ODD, D), 0.1),
        "rw_a1": nrm((N_ODD, D, LORA_AAA), D ** -0.5),
        "rw_a2": nrm((N_ODD, LORA_AAA, D), 0.1 * LORA_AAA ** -0.5),
        "rw_v0": nrm((N_ODD - 1, D), 0.1) + 1.0,
        "rw_v1": nrm((N_ODD - 1, D, LORA_MV), D ** -0.5),
        "rw_v2": nrm((N_ODD - 1, LORA_MV, D), 0.1 * LORA_MV ** -0.5),
        "rw_g1": nrm((N_ODD, D, LORA_GATE), D ** -0.5),
        "rw_g2": nrm((N_ODD, LORA_GATE, D), LORA_GATE ** -0.5),
        "rw_kk": 0.85 + nrm((N_ODD, D), 0.05),
        "rw_ka": 1.0 + nrm((N_ODD, D), 0.05),
        "rw_rk": nrm((N_ODD, C_HEADS, C_HEAD), 0.1),
        "rw_lnx_g": gain((N_ODD, D)),
        "rw_lnx_b": nrm((N_ODD, D), 0.02),
        "w_up": nrm((DEPTH, D, D_FF), D ** -0.5),
        "w_down": nrm((DEPTH, D_FF, D), D_FF ** -0.5),
    }


def reference(x_prompt, x_sample, state_hgrn, cache_swa_k, cache_swa_v, state_rwkv, state_shift,
              norm_mix_pre, norm_mix_post, norm_ffn_pre, norm_ffn_post,
              w_in_even, w_out_even, hgrn_lb_raw, hgrn_norm_g, rel_bias, attn_sinks,
              rw_mu, rw_wr, rw_wk, rw_wv, rw_wo, rw_w0, rw_w1, rw_w2, rw_a0, rw_a1, rw_a2,
              rw_v0, rw_v1, rw_v2, rw_g1, rw_g2, rw_kk, rw_ka, rw_rk, rw_lnx_g, rw_lnx_b,
              w_up, w_down):
    p = {
        "norm_mix_pre": norm_mix_pre, "norm_mix_post": norm_mix_post,
        "norm_ffn_pre": norm_ffn_pre, "norm_ffn_post": norm_ffn_post,
        "w_in_even": w_in_even, "w_out_even": w_out_even, "hgrn_lb_raw": hgrn_lb_raw,
        "hgrn_norm_g": hgrn_norm_g, "rel_bias": rel_bias, "attn_sinks": attn_sinks,
        "rw_mu": rw_mu, "rw_wr": rw_wr, "rw_wk": rw_wk, "rw_wv": rw_wv, "rw_wo": rw_wo,
        "rw_w0": rw_w0, "rw_w1": rw_w1, "rw_w2": rw_w2, "rw_a0": rw_a0, "rw_a1": rw_a1, "rw_a2": rw_a2,
        "rw_v0": rw_v0, "rw_v1": rw_v1, "rw_v2": rw_v2, "rw_g1": rw_g1, "rw_g2": rw_g2,
        "rw_kk": rw_kk, "rw_ka": rw_ka, "rw_rk": rw_rk, "rw_lnx_g": rw_lnx_g, "rw_lnx_b": rw_lnx_b,
        "w_up": w_up, "w_down": w_down,
    }
    Bp = x_prompt.shape[0]
    dt = x_prompt.dtype
    zero_hgrn = jnp.zeros((N_EVEN, Bp, A_HEADS, A_KDIM, A_VDIM), dt)
    zero_kv = jnp.zeros((N_EVEN, Bp, 0, B_KV_HEADS, B_HEAD_DIM), dt)
    zero_rwkv = jnp.zeros((N_ODD, Bp, C_HEADS, C_HEAD, C_HEAD), dt)
    zero_shift = jnp.zeros((N_ODD, Bp, D_MODEL), dt)
    y_prompt, hgrn_p, k_p, v_p, rwkv_p, shift_p = trunk(
        x_prompt, zero_hgrn, zero_kv, zero_kv, zero_rwkv, zero_shift, 0, p)
    y_sample, hgrn_s, k_s, v_s, rwkv_s, shift_s = trunk(
        x_sample, state_hgrn, cache_swa_k, cache_swa_v, state_rwkv, state_shift, PAST_LEN, p)
    return (y_prompt, y_sample, hgrn_p, hgrn_s, k_p, k_s, v_p, v_s, rwkv_p, rwkv_s, shift_p, shift_s)
```

```python
import functools
import math

import numpy as np
import jax
import jax.numpy as jnp
from jax import lax
from jax.experimental import pallas as pl
from jax.experimental.pallas import tpu as pltpu

F32 = jnp.float32
BF16 = jnp.bfloat16

D_MODEL = 2048
DEPTH = 4
N_EVEN = 2
N_ODD = 2
A_HEADS = 8
A_KDIM = 128
A_VDIM = 128
A_WIDTH = 1024
A_QK = 1024
B_HEADS = 16
B_HEAD_DIM = 64
B_KV_HEADS = 4
B_GROUP = 4
B_WIDTH = 1024
B_KV_WIDTH = 256
WINDOW = 128
N_BUCKETS = 32
MAX_DISTANCE = 128
MASK_VALUE = -1e30
IN_A = 4096
IN_EVEN = 5632
C_HEAD = 64
C_HEADS = 32
GN_EPS = 64e-5
D_FF = 8192
NORM_EPS = 1e-6

LANES = 128
VMEM_LIMIT = 56 * 1024 * 1024

HGRN_CHUNK = 64
RWKV_CHUNK = 64


def _cparams(sem):
    return pltpu.CompilerParams(dimension_semantics=sem, vmem_limit_bytes=VMEM_LIMIT)


def _rms(x, g):
    return x * lax.rsqrt(jnp.mean(x * x, axis=-1, keepdims=True) + NORM_EPS) * g


def _dot(a, b):
    return jnp.dot(a.astype(BF16), b.astype(BF16), preferred_element_type=F32)


def _dot_nt(a, b):
    return lax.dot_general(a.astype(BF16), b.astype(BF16), (((1,), (1,)), ((), ())),
                           preferred_element_type=F32)


def _dot_tn(a, b):
    return lax.dot_general(a.astype(BF16), b.astype(BF16), (((0,), (0,)), ((), ())),
                           preferred_element_type=F32)


def _dot3(a, b):
    ah = a.astype(BF16)
    al = (a - ah.astype(F32)).astype(BF16)
    bh = b.astype(BF16)
    bl = (b - bh.astype(F32)).astype(BF16)
    return (jnp.dot(ah, bh, preferred_element_type=F32)
            + jnp.dot(ah, bl, preferred_element_type=F32)
            + jnp.dot(al, bh, preferred_element_type=F32))


def _dot_exact01(sel, x):
    hi = x.astype(BF16)
    r1 = x - hi.astype(F32)
    mid = r1.astype(BF16)
    lo = (r1 - mid.astype(F32)).astype(BF16)
    s = sel.astype(BF16)
    return (jnp.dot(s, hi, preferred_element_type=F32)
            + jnp.dot(s, mid, preferred_element_type=F32)
            + jnp.dot(s, lo, preferred_element_type=F32))


def _pick_tile(m, cands):
    for c in cands:
        if m % c == 0:
            return c
    return m


def _norm_matmul_kernel(x_ref, g_ref, w_ref, o_ref, xn_ref):
    @pl.when(pl.program_id(1) == 0)
    def _():
        xn_ref[...] = _rms(x_ref[...], g_ref[...]).astype(BF16)

    o_ref[...] = jnp.dot(xn_ref[...], w_ref[...], preferred_element_type=F32)


def norm_matmul(x, g, w_bf16, tn=512):
    m, d = x.shape
    n = w_bf16.shape[1]
    tm = _pick_tile(m, (768, 512, 256, 128, 64, 32, 16, 8))
    return pl.pallas_call(
        _norm_matmul_kernel,
        grid=(m // tm, n // tn),
        in_specs=[pl.BlockSpec((tm, d), lambda i, j: (i, 0)),
                  pl.BlockSpec((1, d), lambda i, j: (0, 0)),
                  pl.BlockSpec((d, tn), lambda i, j: (0, j))],
        out_specs=pl.BlockSpec((tm, tn), lambda i, j: (i, j)),
        out_shape=jax.ShapeDtypeStruct((m, n), F32),
        scratch_shapes=[pltpu.VMEM((tm, d), BF16)],
        compiler_params=_cparams(("parallel", "arbitrary")),
        name="norm_matmul",
    )(x, g.reshape(1, d), w_bf16)


def _ffn_kernel(x_ref, gpre_ref, wup_ref, wdn_ref, gpost_ref, o_ref, xn_ref, acc_ref):
    f = pl.program_id(1)

    @pl.when(f == 0)
    def _():
        xn_ref[...] = _rms(x_ref[...], gpre_ref[...]).astype(BF16)
        acc_ref[...] = jnp.zeros_like(acc_ref)

    h = jnp.dot(xn_ref[...], wup_ref[...], preferred_element_type=F32)
    h = jnp.square(jnp.maximum(h, 0.0)).astype(BF16)
    acc_ref[...] += jnp.dot(h, wdn_ref[...], preferred_element_type=F32)

    @pl.when(f == pl.num_programs(1) - 1)
    def _():
        o_ref[...] = x_ref[...] + _rms(acc_ref[...], gpost_ref[...])


def ffn(x, gpre, wup_bf16, wdn_bf16, gpost, tf=512):
    m, d = x.shape
    dff = wup_bf16.shape[1]
    tm = _pick_tile(m, (768, 512, 256, 128, 64, 32, 16, 8))
    return pl.pallas_call(
        _ffn_kernel,
        grid=(m // tm, dff // tf),
        in_specs=[pl.BlockSpec((tm, d), lambda i, f: (i, 0)),
                  pl.BlockSpec((1, d), lambda i, f: (0, 0)),
                  pl.BlockSpec((d, tf), lambda i, f: (0, f)),
                  pl.BlockSpec((tf, d), lambda i, f: (f, 0)),
                  pl.BlockSpec((1, d), lambda i, f: (0, 0))],
        out_specs=pl.BlockSpec((tm, d), lambda i, f: (i, 0)),
        out_shape=jax.ShapeDtypeStruct((m, d), F32),
        scratch_shapes=[pltpu.VMEM((tm, d), BF16), pltpu.VMEM((tm, d), F32)],
        compiler_params=_cparams(("parallel", "arbitrary")),
        name="ffn",
    )(x, gpre.reshape(1, d), wup_bf16, wdn_bf16, gpost.reshape(1, d))


def _even_out_kernel(oa_ref, ga_ref, ob_ref, x_ref, ag_ref, w_ref, gpost_ref, o_ref):
    ga = ga_ref[...]
    oan = _rms(oa_ref[...], ag_ref[...]) * (ga * jax.nn.sigmoid(ga))
    mix = (jnp.dot(oan.astype(BF16), w_ref[:A_WIDTH, :], preferred_element_type=F32)
           + jnp.dot(ob_ref[...].astype(BF16), w_ref[A_WIDTH:, :], preferred_element_type=F32))
    o_ref[...] = x_ref[...] + _rms(mix, gpost_ref[...])


def even_out(o_a, proj, o_b, x, a_norm_g, w_out_bf16, gpost):
    m, d = x.shape
    tm = _pick_tile(m, (384, 256, 128, 64, 32, 16, 8))
    ga_blk = (3 * A_WIDTH) // A_WIDTH
    return pl.pallas_call(
        _even_out_kernel,
        grid=(m // tm,),
        in_specs=[pl.BlockSpec((tm, A_WIDTH), lambda i: (i, 0)),
                  pl.BlockSpec((tm, A_WIDTH), lambda i: (i, ga_blk)),
                  pl.BlockSpec((tm, B_WIDTH), lambda i: (i, 0)),
                  pl.BlockSpec((tm, d), lambda i: (i, 0)),
                  pl.BlockSpec((1, A_WIDTH), lambda i: (0, 0)),
                  pl.BlockSpec((A_WIDTH + B_WIDTH, d), lambda i: (0, 0)),
                  pl.BlockSpec((1, d), lambda i: (0, 0))],
        out_specs=pl.BlockSpec((tm, d), lambda i: (i, 0)),
        out_shape=jax.ShapeDtypeStruct((m, d), F32),
        compiler_params=_cparams(("parallel",)),
        name="even_out",
    )(o_a, proj, o_b, x, a_norm_g.reshape(1, A_WIDTH), w_out_bf16, gpost.reshape(1, d))


def _odd_out_kernel(y_ref, x_ref, w_ref, gpost_ref, o_ref):
    mix = jnp.dot(y_ref[...].astype(BF16), w_ref[...], preferred_element_type=F32)
    o_ref[...] = x_ref[...] + _rms(mix, gpost_ref[...])


def odd_out(yg, x, wo_bf16, gpost):
    m, d = x.shape
    tm = _pick_tile(m, (384, 256, 128, 64, 32, 16, 8))
    return pl.pallas_call(
        _odd_out_kernel,
        grid=(m // tm,),
        in_specs=[pl.BlockSpec((tm, d), lambda i: (i, 0)),
                  pl.BlockSpec((tm, d), lambda i: (i, 0)),
                  pl.BlockSpec((d, d), lambda i: (0, 0)),
                  pl.BlockSpec((1, d), lambda i: (0, 0))],
        out_specs=pl.BlockSpec((tm, d), lambda i: (i, 0)),
        out_shape=jax.ShapeDtypeStruct((m, d), F32),
        compiler_params=_cparams(("parallel",)),
        name="odd_out",
    )(yg, x, wo_bf16, gpost.reshape(1, d))


def _level_consts(c):
    levels = []
    s = c // 2
    while s >= 1:
        levels.append(s)
        s //= 2
    nl = len(levels)
    sel = np.zeros((nl, c, c), np.float32)
    mask = np.zeros((nl, c, c), np.float32)
    idx = np.arange(c)
    for l, s in enumerate(levels):
        ref_row = (idx // (2 * s)) * (2 * s) + s - 1
        sel[l, idx, ref_row] = 1.0
        same = (idx[:, None] // (2 * s)) == (idx[None, :] // (2 * s))
        upper = (idx[:, None] % (2 * s)) >= s
        lower = (idx[None, :] % (2 * s)) < s
        mask[l] = (same & upper & lower).astype(np.float32)
    tril = np.tril(np.ones((c, c), np.float32))
    return nl, sel.reshape(nl * c, c), mask, tril


def _hgrn_kernel(*refs, layer, chunk, nchunks, nlevels, has_state):
    if has_state:
        (q_ref, f_ref, i_ref, lb_ref, sel_ref, mask_ref, tril_ref, s0_ref,
         o_ref, s_ref, st_ref) = refs
    else:
        (q_ref, f_ref, i_ref, lb_ref, sel_ref, mask_ref, tril_ref,
         o_ref, s_ref, st_ref) = refs
    c = chunk
    l_idx = pl.program_id(2)

    @pl.when(l_idx == 0)
    def _():
        if has_state:
            st_ref[...] = s0_ref[0, 0].T
        else:
            st_ref[...] = jnp.zeros_like(st_ref)

    lbr = lb_ref[...]
    e = jnp.exp(lbr - jnp.max(lbr, axis=0, keepdims=True))
    p = e / jnp.sum(e, axis=0, keepdims=True)
    lb = jnp.zeros((1, LANES), F32)
    for i in range(1, layer + 1):
        lb = lb + p[i:i + 1, :]
    one_m_lb = 1.0 - lb

    def body(ci, carry):
        rows = pl.ds(pl.multiple_of(ci * c, c), c)
        fq = f_ref[rows, :]
        qr = q_ref[rows, :]
        v = i_ref[rows, :]
        logf = jnp.log(lb + one_m_lb * jax.nn.sigmoid(fq))
        k = one_m_lb * jax.nn.sigmoid(-fq)
        q = qr * jax.nn.sigmoid(qr) * (A_KDIM ** -0.5)

        g = _dot_exact01(tril_ref[...], logf)
        refs_g = _dot_exact01(sel_ref[...], g)
        glast = g[c - 1:c, :]
        st = st_ref[...]

        o = _dot_nt(q * jnp.exp(g), st)
        attn = jnp.zeros((c, c), F32)
        for l in range(nlevels):
            gr = refs_g[l * c:(l + 1) * c, :]
            qs = q * jnp.exp(jnp.minimum(g - gr, 0.0))
            ks = k * jnp.exp(jnp.minimum(gr - g, 0.0))
            attn = attn + mask_ref[l] * _dot_nt(qs, ks)
        o = o + _dot(attn, v) + jnp.sum(q * k, axis=-1, keepdims=True) * v
        o_ref[rows, :] = o

        kd = k * jnp.exp(glast - g)
        st_ref[...] = st * jnp.exp(glast) + _dot_tn(v, kd)
        return carry

    lax.fori_loop(0, nchunks, body, 0)

    @pl.when(l_idx == pl.num_programs(2) - 1)
    def _():
        s_ref[0, 0] = st_ref[...].T


def hgrn(proj, lb_raw, layer, batch, seqlen, s0=None):
    m = proj.shape[0]
    c = math.gcd(seqlen, HGRN_CHUNK)
    lblk = _pick_tile(seqlen, (512, 256, 128, 64, 32, 16, 8))
    nl_blocks = seqlen // lblk
    nlevels, sel, mask, tril = _level_consts(c)
    has_state = s0 is not None
    kern = functools.partial(_hgrn_kernel, layer=layer, chunk=c, nchunks=lblk // c,
                             nlevels=nlevels, has_state=has_state)
    row = lambda b, h, l: b * nl_blocks + l
    in_specs = [pl.BlockSpec((lblk, LANES), lambda b, h, l: (row(b, h, l), h)),
                pl.BlockSpec((lblk, LANES), lambda b, h, l: (row(b, h, l), A_HEADS + h)),
                pl.BlockSpec((lblk, LANES), lambda b, h, l: (row(b, h, l), 2 * A_HEADS + h)),
                pl.BlockSpec((N_EVEN, LANES), lambda b, h, l: (0, h)),
                pl.BlockSpec((nlevels * c, c), lambda b, h, l: (0, 0)),
                pl.BlockSpec((nlevels, c, c), lambda b, h, l: (0, 0, 0)),
                pl.BlockSpec((c, c), lambda b, h, l: (0, 0))]
    args = [proj, proj, proj, lb_raw, jnp.asarray(sel), jnp.asarray(mask), jnp.asarray(tril)]
    if has_state:
        in_specs.append(pl.BlockSpec((1, 1, A_KDIM, A_VDIM), lambda b, h, l: (b, h, 0, 0)))
        args.append(s0)
    return pl.pallas_call(
        kern,
        grid=(batch, A_HEADS, nl_blocks),
        in_specs=in_specs,
        out_specs=[pl.BlockSpec((lblk, LANES), lambda b, h, l: (row(b, h, l), h)),
                   pl.BlockSpec((1, 1, A_KDIM, A_VDIM), lambda b, h, l: (b, h, 0, 0))],
        out_shape=[jax.ShapeDtypeStruct((m, A_WIDTH), F32),
                   jax.ShapeDtypeStruct((batch, A_HEADS, A_KDIM, A_VDIM), F32)],
        scratch_shapes=[pltpu.VMEM((A_VDIM, A_KDIM), F32)],
        compiler_params=_cparams(("parallel", "parallel", "arbitrary")),
        name="hgrn",
    )(*args)


def _t5_bucket(dist):
    max_exact = N_BUCKETS // 2
    d = np.maximum(dist, 0)
    large = max_exact + (np.log(np.maximum(d, max_exact).astype(np.float32) / max_exact)
                         / math.log(MAX_DISTANCE / max_exact) * (N_BUCKETS - max_exact)).astype(np.int32)
    large = np.minimum(large, N_BUCKETS - 1)
    return np.where(d < max_exact, d, large).astype(np.int32)


def _swa_kernel(q_ref, kp_ref, kc_ref, vp_ref, vc_ref, bucket_ref, band_ref, rb_ref, sink_ref,
                o_ref, bias_ref, *, qb, span, prev_always_valid):
    first = (pl.program_id(0) == 0) & (pl.program_id(1) == 0)

    @pl.when(first)
    def _():
        bk = bucket_ref[...]
        band = band_ref[...]

        def per_head(h, carry):
            def per_bucket(bi, acc):
                return jnp.where(bk == bi, rb_ref[bi, h], acc)
            acc = lax.fori_loop(0, N_BUCKETS, per_bucket, jnp.zeros((qb, span), F32))
            bias_ref[h] = jnp.where(band > 0, acc, MASK_VALUE)
            return carry

        lax.fori_loop(0, B_HEADS, per_head, 0)

    scale = B_HEAD_DIM ** -0.5
    q = q_ref[...]
    kall = jnp.concatenate([kp_ref[...], kc_ref[...]], axis=0)
    vall = jnp.concatenate([vp_ref[...], vc_ref[...]], axis=0)
    if not prev_always_valid:
        col = lax.broadcasted_iota(jnp.int32, (qb, span), 1)
        no_prev = (col < WINDOW) & (pl.program_id(1) == 0)
    outs = []
    for kh in range(B_KV_HEADS):
        kk = kall[:, kh * B_HEAD_DIM:(kh + 1) * B_HEAD_DIM]
        vv = vall[:, kh * B_HEAD_DIM:(kh + 1) * B_HEAD_DIM]
        for g in range(B_GROUP):
            h = kh * B_GROUP + g
            qh = q[:, h * B_HEAD_DIM:(h + 1) * B_HEAD_DIM] * scale
            s = _dot_nt(qh, kk) + bias_ref[h]
            if not prev_always_valid:
                s = jnp.where(no_prev, MASK_VALUE, s)
            sink = sink_ref[h]
            m = jnp.maximum(jnp.max(s, axis=-1, keepdims=True), sink)
            p = jnp.exp(s - m)
            denom = jnp.sum(p, axis=-1, keepdims=True) + jnp.exp(sink - m)
            outs.append(_dot(p, vv) / denom)
    o_ref[...] = jnp.concatenate(outs, axis=1)


def swa(proj, batch, seqlen, rel_bias, sinks, k_past=None, v_past=None):
    m = proj.shape[0]
    has_cache = k_past is not None
    qb = math.gcd(seqlen, WINDOW)
    nb = seqlen // qb
    span = WINDOW + qb
    dist = np.arange(qb)[:, None] + WINDOW - np.arange(span)[None, :]
    band = ((dist >= 0) & (dist < WINDOW)).astype(np.float32)
    bucket = _t5_bucket(dist)
    q_col = IN_A // B_WIDTH
    k_col = (IN_A + B_WIDTH) // B_KV_WIDTH
    v_col = k_col + 1
    cur = lambda c: (lambda b, n: (b * nb + n, c))
    if has_cache:
        assert nb == 1
        prev_k = pl.BlockSpec((WINDOW, B_KV_WIDTH), lambda b, n: (b, 0))
        prev_v = pl.BlockSpec((WINDOW, B_KV_WIDTH), lambda b, n: (b, 0))
        kp_arr, vp_arr = k_past, v_past
    else:
        assert qb == WINDOW
        prev = lambda c: (lambda b, n: (b * nb + jnp.maximum(n - 1, 0), c))
        prev_k = pl.BlockSpec((WINDOW, B_KV_WIDTH), prev(k_col))
        prev_v = pl.BlockSpec((WINDOW, B_KV_WIDTH), prev(v_col))
        kp_arr, vp_arr = proj, proj
    kern = functools.partial(_swa_kernel, qb=qb, span=span, prev_always_valid=has_cache)
    return pl.pallas_call(
        kern,
        grid=(batch, nb),
        in_specs=[pl.BlockSpec((qb, B_WIDTH), cur(q_col)),
                  prev_k,
                  pl.BlockSpec((qb, B_KV_WIDTH), cur(k_col)),
                  prev_v,
                  pl.BlockSpec((qb, B_KV_WIDTH), cur(v_col)),
                  pl.BlockSpec((qb, span), lambda b, n: (0, 0)),
                  pl.BlockSpec((qb, span), lambda b, n: (0, 0)),
                  pl.BlockSpec(memory_space=pltpu.SMEM),
                  pl.BlockSpec(memory_space=pltpu.SMEM)],
        out_specs=pl.BlockSpec((qb, B_WIDTH), lambda b, n: (b * nb + n, 0)),
        out_shape=jax.ShapeDtypeStruct((m, B_WIDTH), F32),
        scratch_shapes=[pltpu.VMEM((B_HEADS, qb, span), F32)],
        compiler_params=_cparams(("arbitrary", "arbitrary")),
        name="swa",
    )(proj, kp_arr, proj, vp_arr, proj, jnp.asarray(bucket), jnp.asarray(band), rel_bias, sinks)


def _rmsnorm_kernel(x_ref, g_ref, o_ref):
    o_ref[...] = _rms(x_ref[...], g_ref[...])


def rmsnorm_rows(x, g):
    m, d = x.shape
    tm = _pick_tile(m, (512, 256, 128, 64, 32, 16, 8))
    return pl.pallas_call(
        _rmsnorm_kernel,
        grid=(m // tm,),
        in_specs=[pl.BlockSpec((tm, d), lambda i: (i, 0)), pl.BlockSpec((1, d), lambda i: (0, 0))],
        out_specs=pl.BlockSpec((tm, d), lambda i: (i, 0)),
        out_shape=jax.ShapeDtypeStruct((m, d), F32),
        compiler_params=_cparams(("parallel",)),
        name="rmsnorm",
    )(x, g.reshape(1, d))


def _rwkv_proj_kernel(h_ref, hp_ref, mu_ref, w_ref, o_ref, xm_ref):
    @pl.when(pl.program_id(2) == 0)
    def _():
        h = h_ref[...]
        xm_ref[...] = (h + (hp_ref[...] - h) * mu_ref[0]).astype(BF16)

    o_ref[0] = jnp.dot(xm_ref[...], w_ref[0], preferred_element_type=F32)


def rwkv_proj(h, hp, mu3, w3_bf16, tn=512):
    m, d = h.shape
    tm = _pick_tile(m, (512, 256, 128, 64, 32, 16, 8))
    return pl.pallas_call(
        _rwkv_proj_kernel,
        grid=(m // tm, 3, d // tn),
        in_specs=[pl.BlockSpec((tm, d), lambda i, p, j: (i, 0)),
                  pl.BlockSpec((tm, d), lambda i, p, j: (i, 0)),
                  pl.BlockSpec((1, 1, d), lambda i, p, j: (p, 0, 0)),
                  pl.BlockSpec((1, d, tn), lambda i, p, j: (p, 0, j))],
        out_specs=pl.BlockSpec((1, tm, tn), lambda i, p, j: (p, i, j)),
        out_shape=jax.ShapeDtypeStruct((3, m, d), F32),
        scratch_shapes=[pltpu.VMEM((tm, d), BF16)],
        compiler_params=_cparams(("parallel", "arbitrary", "arbitrary")),
        name="rwkv_proj",
    )(h, hp, mu3, w3_bf16)


def _softplus(z):
    return jnp.maximum(z, 0.0) + jnp.log(1.0 + jnp.exp(-jnp.abs(z)))


def _rwkv_lora_kernel(*refs, has_vres):
    if has_vres:
        (h_ref, hp_ref, mu_ref, w0_ref, w1_ref, w2_ref, a0_ref, a1_ref, a2_ref, g1_ref, g2_ref,
         v0_ref, v1_ref, v2_ref, wl_ref, a_ref, g_ref, vg_ref) = refs
    else:
        (h_ref, hp_ref, mu_ref, w0_ref, w1_ref, w2_ref, a0_ref, a1_ref, a2_ref, g1_ref, g2_ref,
         wl_ref, a_ref, g_ref) = refs
    h = h_ref[...]
    xx = hp_ref[...] - h
    mix = lambda i: (h + xx * mu_ref[i:i + 1, :]).astype(BF16)
    xw, xa, xg = mix(0), mix(1), mix(2)
    wpre = w0_ref[...] + _dot(jnp.tanh(jnp.dot(xw, w1_ref[...], preferred_element_type=F32)), w2_ref[...])
    w_log = -_softplus(-wpre) - 0.5
    wl_ref[...] = -jnp.exp(w_log)
    apre = a0_ref[...] + _dot(jnp.dot(xa, a1_ref[...], preferred_element_type=F32), a2_ref[...])
    a_ref[...] = jax.nn.sigmoid(apre)
    g_ref[...] = _dot(jax.nn.sigmoid(jnp.dot(xg, g1_ref[...], preferred_element_type=F32)), g2_ref[...])
    if has_vres:
        xv = mix(3)
        vpre = v0_ref[...] + _dot(jnp.dot(xv, v1_ref[...], preferred_element_type=F32), v2_ref[...])
        vg_ref[...] = jax.nn.sigmoid(vpre)


def _pad_lora(w1, w2):
    r = w1.shape[1]
    rp = -(-r // LANES) * LANES
    return (jnp.pad(w1, ((0, 0), (0, rp - r))).astype(BF16),
            jnp.pad(w2, ((0, rp - r), (0, 0))).astype(BF16))


def rwkv_lora(h, hp, mu_wagv, w0, w1, w2, a0, a1, a2, g1, g2, vres=None):
    m, d = h.shape
    tm = _pick_tile(m, (256, 128, 64, 32, 16, 8))
    has_vres = vres is not None
    row = lambda i: (i, 0)
    const = lambda i: (0, 0)
    w1p, w2p = _pad_lora(w1, w2)
    a1p, a2p = _pad_lora(a1, a2)
    g1p, g2p = _pad_lora(g1, g2)
    args = [h, hp, mu_wagv, w0.reshape(1, d), w1p, w2p, a0.reshape(1, d), a1p, a2p, g1p, g2p]
    if has_vres:
        v1p, v2p = _pad_lora(vres[1], vres[2])
        args += [vres[0].reshape(1, d), v1p, v2p]
    in_specs = [pl.BlockSpec((tm, d), row), pl.BlockSpec((tm, d), row)]
    in_specs += [pl.BlockSpec(a.shape, const) for a in args[2:]]
    n_out = 4 if has_vres else 3
    return pl.pallas_call(
        functools.partial(_rwkv_lora_kernel, has_vres=has_vres),
        grid=(m // tm,),
        in_specs=in_specs,
        out_specs=[pl.BlockSpec((tm, d), row)] * n_out,
        out_shape=[jax.ShapeDtypeStruct((m, d), F32)] * n_out,
        compiler_params=_cparams(("parallel",)),
        name="rwkv_lora",
    )(*args)


def _rwkv_consts(c):
    i = np.arange(2 * c)
    same = (i[:, None] // c) == (i[None, :] // c)
    strict = (same & ((i[:, None] % c) > (i[None, :] % c))).astype(np.float32)
    incl = (same & ((i[:, None] % c) >= (i[None, :] % c))).astype(np.float32)
    eye = np.eye(2 * c, dtype=np.float32)
    tril = np.tril(np.ones((c, c), np.float32))
    l = np.arange(LANES)
    headones = ((l[:, None] // C_HEAD) == (l[None, :] // C_HEAD)).astype(np.float32)
    return strict, incl, eye, tril, headones


def _dot2_ones(x, ones_bf16):
    hi = x.astype(BF16)
    lo = (x - hi.astype(F32)).astype(BF16)
    return (jnp.dot(hi, ones_bf16, preferred_element_type=F32)
            + jnp.dot(lo, ones_bf16, preferred_element_type=F32))


def _rwkv_core_kernel(*refs, chunk, nchunks, npairs, has_state, has_vres):
    it = iter(refs)
    r_ref, k_ref, v_ref, wl_ref, a_ref, g_ref = (next(it) for _ in range(6))
    if has_vres:
        vf_ref, vg_ref = next(it), next(it)
    kk_ref, ka_ref, rk_ref, lg_ref, lb_ref = (next(it) for _ in range(5))
    strict_ref, incl_ref, eye_ref, tril_ref, hones_ref = (next(it) for _ in range(5))
    if has_state:
        s0_ref = next(it)
    y_ref, s_ref, st_ref = next(it), next(it), next(it)
    c = chunk
    l_idx = pl.program_id(2)
    lane = lax.broadcasted_iota(jnp.int32, (1, LANES), 1)
    m0 = (lane < C_HEAD).astype(F32)
    m1 = 1.0 - m0

    @pl.when(l_idx == 0)
    def _():
        for pi in range(npairs):
            if has_state:
                z = jnp.zeros((C_HEAD, C_HEAD), F32)
                top = jnp.concatenate([s0_ref[0, 2 * pi], z], axis=1)
                bot = jnp.concatenate([z, s0_ref[0, 2 * pi + 1]], axis=1)
                st_ref[pi] = jnp.concatenate([top, bot], axis=0)
            else:
                st_ref[pi] = jnp.zeros((LANES, LANES), F32)

    hones = hones_ref[...].astype(BF16)
    strict = strict_ref[...]
    incl = incl_ref[...]
    eye = eye_ref[...]
    tril = tril_ref[...]

    def stack(x):
        return jnp.concatenate([x * m0, x * m1], axis=0)

    def one_pair(pi, rows):
        cols = slice(pi * LANES, (pi + 1) * LANES)
        r = r_ref[0, rows, cols]
        k = k_ref[0, rows, cols]
        v = v_ref[0, rows, cols]
        wl = wl_ref[rows, cols]
        a = a_ref[rows, cols]
        if has_vres:
            v = v + (vf_ref[0, rows, cols] - v) * vg_ref[rows, cols]
        kr = k * kk_ref[:, cols]
        kk = kr * lax.rsqrt(jnp.maximum(_dot2_ones(kr * kr, hones), 1e-24))
        kh = k * (1.0 + (a - 1.0) * ka_ref[:, cols])
        al = -kk
        b = kk * a

        gc = _dot_exact01(tril, wl)
        gp = gc - wl
        gl = gc[c - 1:c, :]
        e_neg = jnp.exp(-gc)
        e_out = jnp.exp(gl - gc)
        ab = al * jnp.exp(gp)
        rb = r * jnp.exp(gc)
        bt = b * e_neg
        kt = kh * e_neg
        bh = b * e_out
        khat = kh * e_out

        lhs = jnp.concatenate([stack(ab), stack(rb)], axis=0)
        with_b = _dot_nt(lhs, jnp.concatenate([bt, bt], axis=0))
        with_k = _dot_nt(lhs, jnp.concatenate([kt, kt], axis=0))
        a_ab = with_b[:2 * c] * strict
        a_rb = with_b[2 * c:] * incl
        a_ak = with_k[:2 * c] * strict
        a_rk = with_k[2 * c:] * incl

        t = eye + a_ab
        pw = a_ab
        n = 1
        while 2 * n < c:
            pw = _dot3(pw, pw)
            t = t + _dot3(t, pw)
            n *= 2

        st = st_ref[pi]
        from_state = _dot_nt(jnp.concatenate([ab, rb], axis=0), st)
        vs = stack(v)
        rhs = stack(from_state[:c]) + _dot(a_ak, vs)
        us = _dot(t, rhs)
        ys = _dot(a_rb, us) + _dot(a_rk, vs)
        u = us[:c] + us[c:]
        y = from_state[c:] + ys[:c] + ys[c:]

        upd = _dot_tn(jnp.concatenate([u, v], axis=0), jnp.concatenate([bh, khat], axis=0))
        st_ref[pi] = st * jnp.exp(gl) + upd * hones_ref[...]

        inv_n = 1.0 / C_HEAD
        mean = _dot2_ones(y, hones) * inv_n
        dlt = y - mean
        var = _dot2_ones(dlt * dlt, hones) * inv_n
        yn = dlt * lax.rsqrt(var + GN_EPS) * lg_ref[:, cols] + lb_ref[:, cols]
        bonus = _dot2_ones(r * kh * rk_ref[:, cols], hones) * v
        y_ref[rows, cols] = ((yn + bonus) * g_ref[rows, cols]).astype(y_ref.dtype)

    def body(ci, carry):
        rows = pl.ds(pl.multiple_of(ci * c, c), c)
        for pi in range(npairs):
            one_pair(pi, rows)
        return carry

    lax.fori_loop(0, nchunks, body, 0)

    @pl.when(l_idx == pl.num_programs(2) - 1)
    def _():
        for pi in range(npairs):
            st = st_ref[pi]
            s_ref[0, 2 * pi] = st[:C_HEAD, :C_HEAD]
            s_ref[0, 2 * pi + 1] = st[C_HEAD:, C_HEAD:]


def rwkv_core(rkv, wl, a, g, kk_p, ka_p, rk_p, lnx_g, lnx_b, batch, seqlen,
              s0=None, v_first=None, vgate=None, npairs=4):
    _, m, d = rkv.shape
    c = math.gcd(seqlen, RWKV_CHUNK)
    lblk = _pick_tile(seqlen, (256, 128, 64, 32, 16, 8))
    nl_blocks = seqlen // lblk
    has_state = s0 is not None
    has_vres = v_first is not None
    w = npairs * LANES
    ngroups = d // w
    strict, incl, eye, tril, hones = (jnp.asarray(x) for x in _rwkv_consts(c))
    row = lambda b, p, l: b * nl_blocks + l
    blk3 = lambda which: pl.BlockSpec((1, lblk, w), lambda b, p, l: (which, row(b, p, l), p))
    blk2 = pl.BlockSpec((lblk, w), lambda b, p, l: (row(b, p, l), p))
    par = pl.BlockSpec((1, w), lambda b, p, l: (0, p))
    full = lambda arr: pl.BlockSpec(arr.shape, lambda b, p, l: (0,) * arr.ndim)
    in_specs = [blk3(0), blk3(1), blk3(2), blk2, blk2, blk2]
    args = [rkv, rkv, rkv, wl, a, g]
    if has_vres:
        in_specs += [blk3(2), blk2]
        args += [v_first, vgate]
    in_specs += [par] * 5
    args += [x.reshape(1, d) for x in (kk_p, ka_p, rk_p, lnx_g, lnx_b)]
    in_specs += [full(x) for x in (strict, incl, eye, tril, hones)]
    args += [strict, incl, eye, tril, hones]
    st_blk = pl.BlockSpec((1, 2 * npairs, C_HEAD, C_HEAD), lambda b, p, l: (b, p, 0, 0))
    if has_state:
        in_specs.append(st_blk)
        args.append(s0)
    kern = functools.partial(_rwkv_core_kernel, chunk=c, nchunks=lblk // c, npairs=npairs,
                             has_state=has_state, has_vres=has_vres)
    return pl.pallas_call(
        kern,
        grid=(batch, ngroups, nl_blocks),
        in_specs=in_specs,
        out_specs=[blk2, st_blk],
        out_shape=[jax.ShapeDtypeStruct((m, d), F32),
                   jax.ShapeDtypeStruct((batch, C_HEADS, C_HEAD, C_HEAD), F32)],
        scratch_shapes=[pltpu.VMEM((npairs, LANES, LANES), F32)],
        compiler_params=_cparams(("parallel", "parallel", "arbitrary")),
        name="rwkv_core",
    )(*args)


def _even_layer(x, batch, seqlen, e, layer, p, st_hgrn, k_cache, v_cache):
    proj = norm_matmul(x, p["norm_mix_pre"][layer], p["w_in_even"][e])
    o_a, s_new = hgrn(proj, p["hgrn_lb_raw"], e, batch, seqlen, st_hgrn)
    k_lo = IN_A + B_WIDTH
    kb = proj[:, k_lo:k_lo + B_KV_WIDTH].reshape(batch, seqlen, B_KV_HEADS, B_HEAD_DIM)
    vb = proj[:, k_lo + B_KV_WIDTH:].reshape(batch, seqlen, B_KV_HEADS, B_HEAD_DIM)
    if k_cache is None:
        o_b = swa(proj, batch, seqlen, p["rel_bias"], p["attn_sinks"][e])
        k_new, v_new = kb[:, -WINDOW:], vb[:, -WINDOW:]
    else:
        o_b = swa(proj, batch, seqlen, p["rel_bias"], p["attn_sinks"][e],
                  k_cache.reshape(batch * WINDOW, B_KV_WIDTH), v_cache.reshape(batch * WINDOW, B_KV_WIDTH))
        k_new = jnp.concatenate([k_cache, kb], axis=1)[:, -WINDOW:]
        v_new = jnp.concatenate([v_cache, vb], axis=1)[:, -WINDOW:]
    x = even_out(o_a, proj, o_b, x, p["hgrn_norm_g"][e], p["w_out_even"][e], p["norm_mix_post"][layer])
    return x, s_new, k_new, v_new


def _odd_layer(x, batch, seqlen, o, layer, p, shift0, s0, v_first):
    m, d = x.shape
    h = rmsnorm_rows(x, p["norm_mix_pre"][layer])
    h3 = h.reshape(batch, seqlen, d)
    first = jnp.zeros((batch, 1, d), F32) if shift0 is None else shift0[:, None, :]
    hp = jnp.concatenate([first, h3[:, :-1]], axis=1).reshape(m, d)
    mu = p["rw_mu"][o]
    mu_rkv = jnp.stack([mu[0], mu[2], mu[3]])[:, None, :]
    mu_wagv = jnp.stack([mu[1], mu[4], mu[5], mu[3]])
    rkv = rwkv_proj(h, hp, mu_rkv, p["w_rkv"][o])
    vres = None if o == 0 else (p["rw_v0"][o - 1], p["rw_v1"][o - 1], p["rw_v2"][o - 1])
    outs = rwkv_lora(h, hp, mu_wagv, p["rw_w0"][o], p["rw_w1"][o], p["rw_w2"][o],
                     p["rw_a0"][o], p["rw_a1"][o], p["rw_a2"][o], p["rw_g1"][o], p["rw_g2"][o], vres)
    wl, a, g = outs[:3]
    vgate = outs[3] if o > 0 else None
    yg, s_new = rwkv_core(rkv, wl, a, g, p["rw_kk"][o], p["rw_ka"][o], p["rw_rk"][o],
                          p["rw_lnx_g"][o], p["rw_lnx_b"][o], batch, seqlen, s0,
                          v_first if o > 0 else None, vgate)
    x = odd_out(yg, x, p["rw_wo"][o], p["norm_mix_post"][layer])
    return x, s_new, h3[:, -1], rkv


def _trunk(x3, st_hgrn, k_cache, v_cache, st_rwkv, st_shift, p):
    batch, seqlen, d = x3.shape
    x = x3.reshape(batch * seqlen, d)
    has_state = st_hgrn is not None
    hgrn_out, k_out, v_out, rwkv_out, shift_out = [], [], [], [], []
    v_first = None
    for layer in range(DEPTH):
        if layer % 2 == 0:
            e = layer // 2
            x, s_new, k_new, v_new = _even_layer(
                x, batch, seqlen, e, layer, p,
                st_hgrn[e] if has_state else None,
                k_cache[e] if has_state else None,
                v_cache[e] if has_state else None)
            hgrn_out.append(s_new)
            k_out.append(k_new)
            v_out.append(v_new)
        else:
            o = layer // 2
            x, s_new, sh_new, rkv = _odd_layer(
                x, batch, seqlen, o, layer, p,
                st_shift[o] if has_state else None,
                st_rwkv[o] if has_state else None,
                v_first)
            if o == 0:
                v_first = rkv
            rwkv_out.append(s_new)
            shift_out.append(sh_new)
        x = ffn(x, p["norm_ffn_pre"][layer], p["w_up"][layer], p["w_down"][layer], p["norm_ffn_post"][layer])
    return (x.reshape(batch, seqlen, d), jnp.stack(hgrn_out), jnp.stack(k_out), jnp.stack(v_out),
            jnp.stack(rwkv_out), jnp.stack(shift_out))


def kernel(x_prompt, x_sample, state_hgrn, cache_swa_k, cache_swa_v, state_rwkv, state_shift,
           norm_mix_pre, norm_mix_post, norm_ffn_pre, norm_ffn_post,
           w_in_even, w_out_even, hgrn_lb_raw, hgrn_norm_g, rel_bias, attn_sinks,
           rw_mu, rw_wr, rw_wk, rw_wv, rw_wo, rw_w0, rw_w1, rw_w2, rw_a0, rw_a1, rw_a2,
           rw_v0, rw_v1, rw_v2, rw_g1, rw_g2, rw_kk, rw_ka, rw_rk, rw_lnx_g, rw_lnx_b,
           w_up, w_down):
    p = {
        "norm_mix_pre": norm_mix_pre, "norm_mix_post": norm_mix_post,
        "norm_ffn_pre": norm_ffn_pre, "norm_ffn_post": norm_ffn_post,
        "w_in_even": w_in_even.astype(BF16), "w_out_even": w_out_even.astype(BF16),
        "hgrn_lb_raw": hgrn_lb_raw, "hgrn_norm_g": hgrn_norm_g,
        "rel_bias": rel_bias, "attn_sinks": attn_sinks,
        "rw_mu": rw_mu, "w_rkv": jnp.stack([rw_wr, rw_wk, rw_wv], axis=1).astype(BF16),
        "rw_wo": rw_wo.astype(BF16),
        "rw_w0": rw_w0, "rw_w1": rw_w1, "rw_w2": rw_w2, "rw_a0": rw_a0, "rw_a1": rw_a1, "rw_a2": rw_a2,
        "rw_v0": rw_v0, "rw_v1": rw_v1, "rw_v2": rw_v2, "rw_g1": rw_g1, "rw_g2": rw_g2,
        "rw_kk": rw_kk, "rw_ka": rw_ka, "rw_rk": rw_rk, "rw_lnx_g": rw_lnx_g, "rw_lnx_b": rw_lnx_b,
        "w_up": w_up.astype(BF16), "w_down": w_down.astype(BF16),
    }
    y_p, hgrn_p, k_p, v_p, rwkv_p, shift_p = _trunk(x_prompt, None, None, None, None, None, p)
    y_s, hgrn_s, k_s, v_s, rwkv_s, shift_s = _trunk(
        x_sample, state_hgrn, cache_swa_k, cache_swa_v, state_rwkv, state_shift, p)
    return (y_p, y_s, hgrn_p, hgrn_s, k_p, k_s, v_p, v_s, rwkv_p, rwkv_s, shift_p, shift_s)
```

```python
import functools
import math

import numpy as np
import jax
import jax.numpy as jnp
from jax import lax
from jax.experimental import pallas as pl
from jax.experimental.pallas import tpu as pltpu

F32 = jnp.float32
BF16 = jnp.bfloat16

D_MODEL = 2048
DEPTH = 4
N_EVEN = 2
N_ODD = 2
A_HEADS = 8
A_KDIM = 128
A_VDIM = 128
A_WIDTH = 1024
A_QK = 1024
B_HEADS = 16
B_HEAD_DIM = 64
B_KV_HEADS = 4
B_GROUP = 4
B_WIDTH = 1024
B_KV_WIDTH = 256
WINDOW = 128
N_BUCKETS = 32
MAX_DISTANCE = 128
MASK_VALUE = -1e30
IN_A = 4096
IN_EVEN = 5632
C_HEAD = 64
C_HEADS = 32
GN_EPS = 64e-5
D_FF = 8192
NORM_EPS = 1e-6

LANES = 128
VMEM_LIMIT = 56 * 1024 * 1024

HGRN_CHUNK = 64
RWKV_CHUNK = 64


def _cparams(sem):
    return pltpu.CompilerParams(dimension_semantics=sem, vmem_limit_bytes=VMEM_LIMIT)


def _rms(x, g):
    return x * lax.rsqrt(jnp.mean(x * x, axis=-1, keepdims=True) + NORM_EPS) * g


def _dot(a, b):
    return jnp.dot(a.astype(BF16), b.astype(BF16), preferred_element_type=F32)


def _dot_nt(a, b):
    return lax.dot_general(a.astype(BF16), b.astype(BF16), (((1,), (1,)), ((), ())),
                           preferred_element_type=F32)


def _dot_tn(a, b):
    return lax.dot_general(a.astype(BF16), b.astype(BF16), (((0,), (0,)), ((), ())),
                           preferred_element_type=F32)


def _split2(x):
    hi = x.astype(BF16)
    return hi, (x - hi.astype(F32)).astype(BF16)


def _dot3_shared(lhs_list, b):
    bh, bl = _split2(b)
    parts = [_split2(a) for a in lhs_list]
    his = [p[0] for p in parts]
    los = [p[1] for p in parts]
    by_hi = jnp.dot(jnp.concatenate(his + los, axis=0), bh, preferred_element_type=F32)
    by_lo = jnp.dot(jnp.concatenate(his, axis=0), bl, preferred_element_type=F32)
    n = sum(a.shape[0] for a in lhs_list)
    out, off = [], 0
    for a in lhs_list:
        m = a.shape[0]
        out.append(by_hi[off:off + m] + by_hi[n + off:n + off + m] + by_lo[off:off + m])
        off += m
    return out


def _dot_exact01(sel, x):
    hi = x.astype(BF16)
    r1 = x - hi.astype(F32)
    mid = r1.astype(BF16)
    lo = (r1 - mid.astype(F32)).astype(BF16)
    s = sel.astype(BF16)
    return (jnp.dot(s, hi, preferred_element_type=F32)
            + jnp.dot(s, mid, preferred_element_type=F32)
            + jnp.dot(s, lo, preferred_element_type=F32))


def _pick_tile(m, cands):
    for c in cands:
        if m % c == 0:
            return c
    return m


def _norm_matmul_kernel(x_ref, g_ref, w_ref, o_ref, xn_ref):
    @pl.when(pl.program_id(1) == 0)
    def _():
        xn_ref[...] = _rms(x_ref[...], g_ref[...]).astype(BF16)

    o_ref[...] = jnp.dot(xn_ref[...], w_ref[...], preferred_element_type=F32)


def norm_matmul(x, g, w_bf16, tn=512):
    m, d = x.shape
    n = w_bf16.shape[1]
    tm = _pick_tile(m, (768, 512, 256, 128, 64, 32, 16, 8))
    return pl.pallas_call(
        _norm_matmul_kernel,
        grid=(m // tm, n // tn),
        in_specs=[pl.BlockSpec((tm, d), lambda i, j: (i, 0)),
                  pl.BlockSpec((1, d), lambda i, j: (0, 0)),
                  pl.BlockSpec((d, tn), lambda i, j: (0, j))],
        out_specs=pl.BlockSpec((tm, tn), lambda i, j: (i, j)),
        out_shape=jax.ShapeDtypeStruct((m, n), F32),
        scratch_shapes=[pltpu.VMEM((tm, d), BF16)],
        compiler_params=_cparams(("parallel", "arbitrary")),
        name="norm_matmul",
    )(x, g.reshape(1, d), w_bf16)


def _ffn_kernel(x_ref, gpre_ref, wup_ref, wdn_ref, gpost_ref, o_ref, xn_ref, acc_ref):
    f = pl.program_id(1)

    @pl.when(f == 0)
    def _():
        xn_ref[...] = _rms(x_ref[...], gpre_ref[...]).astype(BF16)
        acc_ref[...] = jnp.zeros_like(acc_ref)

    h = jnp.dot(xn_ref[...], wup_ref[...], preferred_element_type=F32)
    h = jnp.square(jnp.maximum(h, 0.0)).astype(BF16)
    acc_ref[...] += jnp.dot(h, wdn_ref[...], preferred_element_type=F32)

    @pl.when(f == pl.num_programs(1) - 1)
    def _():
        o_ref[...] = x_ref[...] + _rms(acc_ref[...], gpost_ref[...])


def ffn(x, gpre, wup_bf16, wdn_bf16, gpost, tf=512):
    m, d = x.shape
    dff = wup_bf16.shape[1]
    tm = _pick_tile(m, (768, 512, 256, 128, 64, 32, 16, 8))
    return pl.pallas_call(
        _ffn_kernel,
        grid=(m // tm, dff // tf),
        in_specs=[pl.BlockSpec((tm, d), lambda i, f: (i, 0)),
                  pl.BlockSpec((1, d), lambda i, f: (0, 0)),
                  pl.BlockSpec((d, tf), lambda i, f: (0, f)),
                  pl.BlockSpec((tf, d), lambda i, f: (f, 0)),
                  pl.BlockSpec((1, d), lambda i, f: (0, 0))],
        out_specs=pl.BlockSpec((tm, d), lambda i, f: (i, 0)),
        out_shape=jax.ShapeDtypeStruct((m, d), F32),
        scratch_shapes=[pltpu.VMEM((tm, d), BF16), pltpu.VMEM((tm, d), F32)],
        compiler_params=_cparams(("parallel", "arbitrary")),
        name="ffn",
    )(x, gpre.reshape(1, d), wup_bf16, wdn_bf16, gpost.reshape(1, d))


def _even_out_kernel(oa_ref, ga_ref, ob_ref, x_ref, ag_ref, w_ref, gpost_ref, o_ref):
    ga = ga_ref[...]
    oan = _rms(oa_ref[...], ag_ref[...]) * (ga * jax.nn.sigmoid(ga))
    mix = (jnp.dot(oan.astype(BF16), w_ref[:A_WIDTH, :], preferred_element_type=F32)
           + jnp.dot(ob_ref[...].astype(BF16), w_ref[A_WIDTH:, :], preferred_element_type=F32))
    o_ref[...] = x_ref[...] + _rms(mix, gpost_ref[...])


def even_out(o_a, proj, o_b, x, a_norm_g, w_out_bf16, gpost):
    m, d = x.shape
    tm = _pick_tile(m, (384, 256, 128, 64, 32, 16, 8))
    ga_blk = (3 * A_WIDTH) // A_WIDTH
    return pl.pallas_call(
        _even_out_kernel,
        grid=(m // tm,),
        in_specs=[pl.BlockSpec((tm, A_WIDTH), lambda i: (i, 0)),
                  pl.BlockSpec((tm, A_WIDTH), lambda i: (i, ga_blk)),
                  pl.BlockSpec((tm, B_WIDTH), lambda i: (i, 0)),
                  pl.BlockSpec((tm, d), lambda i: (i, 0)),
                  pl.BlockSpec((1, A_WIDTH), lambda i: (0, 0)),
                  pl.BlockSpec((A_WIDTH + B_WIDTH, d), lambda i: (0, 0)),
                  pl.BlockSpec((1, d), lambda i: (0, 0))],
        out_specs=pl.BlockSpec((tm, d), lambda i: (i, 0)),
        out_shape=jax.ShapeDtypeStruct((m, d), F32),
        compiler_params=_cparams(("parallel",)),
        name="even_out",
    )(o_a, proj, o_b, x, a_norm_g.reshape(1, A_WIDTH), w_out_bf16, gpost.reshape(1, d))


def _odd_out_kernel(y_ref, x_ref, w_ref, gpost_ref, o_ref):
    mix = jnp.dot(y_ref[...].astype(BF16), w_ref[...], preferred_element_type=F32)
    o_ref[...] = x_ref[...] + _rms(mix, gpost_ref[...])


def odd_out(yg, x, wo_bf16, gpost):
    m, d = x.shape
    tm = _pick_tile(m, (384, 256, 128, 64, 32, 16, 8))
    return pl.pallas_call(
        _odd_out_kernel,
        grid=(m // tm,),
        in_specs=[pl.BlockSpec((tm, d), lambda i: (i, 0)),
                  pl.BlockSpec((tm, d), lambda i: (i, 0)),
                  pl.BlockSpec((d, d), lambda i: (0, 0)),
                  pl.BlockSpec((1, d), lambda i: (0, 0))],
        out_specs=pl.BlockSpec((tm, d), lambda i: (i, 0)),
        out_shape=jax.ShapeDtypeStruct((m, d), F32),
        compiler_params=_cparams(("parallel",)),
        name="odd_out",
    )(yg, x, wo_bf16, gpost.reshape(1, d))


def _level_consts(c):
    levels = []
    s = c // 2
    while s >= 1:
        levels.append(s)
        s //= 2
    nl = len(levels)
    sel = np.zeros((nl, c, c), np.float32)
    mask = np.zeros((nl, c, c), np.float32)
    idx = np.arange(c)
    for l, s in enumerate(levels):
        ref_row = (idx // (2 * s)) * (2 * s) + s - 1
        sel[l, idx, ref_row] = 1.0
        same = (idx[:, None] // (2 * s)) == (idx[None, :] // (2 * s))
        upper = (idx[:, None] % (2 * s)) >= s
        lower = (idx[None, :] % (2 * s)) < s
        mask[l] = (same & upper & lower).astype(np.float32)
    tril = np.tril(np.ones((c, c), np.float32))
    return nl, sel.reshape(nl * c, c), mask, tril


def _hgrn_kernel(*refs, layer, chunk, nchunks, nlevels, has_state):
    if has_state:
        (q_ref, f_ref, i_ref, lb_ref, sel_ref, mask_ref, tril_ref, s0_ref,
         o_ref, s_ref, st_ref) = refs
    else:
        (q_ref, f_ref, i_ref, lb_ref, sel_ref, mask_ref, tril_ref,
         o_ref, s_ref, st_ref) = refs
    c = chunk
    l_idx = pl.program_id(2)

    @pl.when(l_idx == 0)
    def _():
        if has_state:
            st_ref[...] = s0_ref[0, 0].T
        else:
            st_ref[...] = jnp.zeros_like(st_ref)

    lbr = lb_ref[...]
    e = jnp.exp(lbr - jnp.max(lbr, axis=0, keepdims=True))
    p = e / jnp.sum(e, axis=0, keepdims=True)
    lb = jnp.zeros((1, LANES), F32)
    for i in range(1, layer + 1):
        lb = lb + p[i:i + 1, :]
    one_m_lb = 1.0 - lb

    def body(ci, carry):
        rows = pl.ds(pl.multiple_of(ci * c, c), c)
        fq = f_ref[rows, :]
        qr = q_ref[rows, :]
        v = i_ref[rows, :]
        logf = jnp.log(lb + one_m_lb * jax.nn.sigmoid(fq))
        k = one_m_lb * jax.nn.sigmoid(-fq)
        q = qr * jax.nn.sigmoid(qr) * (A_KDIM ** -0.5)

        g = _dot_exact01(tril_ref[...], logf)
        refs_g = _dot_exact01(sel_ref[...], g)
        glast = g[c - 1:c, :]
        st = st_ref[...]

        o = _dot_nt(q * jnp.exp(g), st)
        attn = jnp.zeros((c, c), F32)
        for l in range(nlevels):
            gr = refs_g[l * c:(l + 1) * c, :]
            qs = q * jnp.exp(jnp.minimum(g - gr, 0.0))
            ks = k * jnp.exp(jnp.minimum(gr - g, 0.0))
            attn = attn + mask_ref[l] * _dot_nt(qs, ks)
        o = o + _dot(attn, v) + jnp.sum(q * k, axis=-1, keepdims=True) * v
        o_ref[rows, :] = o

        kd = k * jnp.exp(glast - g)
        st_ref[...] = st * jnp.exp(glast) + _dot_tn(v, kd)
        return carry

    lax.fori_loop(0, nchunks, body, 0)

    @pl.when(l_idx == pl.num_programs(2) - 1)
    def _():
        s_ref[0, 0] = st_ref[...].T


def hgrn(proj, lb_raw, layer, batch, seqlen, s0=None):
    m = proj.shape[0]
    c = math.gcd(seqlen, HGRN_CHUNK)
    lblk = _pick_tile(seqlen, (512, 256, 128, 64, 32, 16, 8))
    nl_blocks = seqlen // lblk
    nlevels, sel, mask, tril = _level_consts(c)
    has_state = s0 is not None
    kern = functools.partial(_hgrn_kernel, layer=layer, chunk=c, nchunks=lblk // c,
                             nlevels=nlevels, has_state=has_state)
    row = lambda b, h, l: b * nl_blocks + l
    in_specs = [pl.BlockSpec((lblk, LANES), lambda b, h, l: (row(b, h, l), h)),
                pl.BlockSpec((lblk, LANES), lambda b, h, l: (row(b, h, l), A_HEADS + h)),
                pl.BlockSpec((lblk, LANES), lambda b, h, l: (row(b, h, l), 2 * A_HEADS + h)),
                pl.BlockSpec((N_EVEN, LANES), lambda b, h, l: (0, h)),
                pl.BlockSpec((nlevels * c, c), lambda b, h, l: (0, 0)),
                pl.BlockSpec((nlevels, c, c), lambda b, h, l: (0, 0, 0)),
                pl.BlockSpec((c, c), lambda b, h, l: (0, 0))]
    args = [proj, proj, proj, lb_raw, jnp.asarray(sel), jnp.asarray(mask), jnp.asarray(tril)]
    if has_state:
        in_specs.append(pl.BlockSpec((1, 1, A_KDIM, A_VDIM), lambda b, h, l: (b, h, 0, 0)))
        args.append(s0)
    return pl.pallas_call(
        kern,
        grid=(batch, A_HEADS, nl_blocks),
        in_specs=in_specs,
        out_specs=[pl.BlockSpec((lblk, LANES), lambda b, h, l: (row(b, h, l), h)),
                   pl.BlockSpec((1, 1, A_KDIM, A_VDIM), lambda b, h, l: (b, h, 0, 0))],
        out_shape=[jax.ShapeDtypeStruct((m, A_WIDTH), F32),
                   jax.ShapeDtypeStruct((batch, A_HEADS, A_KDIM, A_VDIM), F32)],
        scratch_shapes=[pltpu.VMEM((A_VDIM, A_KDIM), F32)],
        compiler_params=_cparams(("parallel", "parallel", "arbitrary")),
        name="hgrn",
    )(*args)


def _t5_bucket(dist):
    max_exact = N_BUCKETS // 2
    d = np.maximum(dist, 0)
    large = max_exact + (np.log(np.maximum(d, max_exact).astype(np.float32) / max_exact)
                         / math.log(MAX_DISTANCE / max_exact) * (N_BUCKETS - max_exact)).astype(np.int32)
    large = np.minimum(large, N_BUCKETS - 1)
    return np.where(d < max_exact, d, large).astype(np.int32)


def _swa_kernel(q_ref, kp_ref, kc_ref, vp_ref, vc_ref, bucket_ref, band_ref, rb_ref, sink_ref,
                o_ref, bias_ref, *, qb, span, prev_always_valid):
    first = (pl.program_id(0) == 0) & (pl.program_id(1) == 0)

    @pl.when(first)
    def _():
        bk = bucket_ref[...]
        band = band_ref[...]

        def per_head(h, carry):
            def per_bucket(bi, acc):
                return jnp.where(bk == bi, rb_ref[bi, h], acc)
            acc = lax.fori_loop(0, N_BUCKETS, per_bucket, jnp.zeros((qb, span), F32))
            bias_ref[h] = jnp.where(band > 0, acc, MASK_VALUE)
            return carry

        lax.fori_loop(0, B_HEADS, per_head, 0)

    scale = B_HEAD_DIM ** -0.5
    q = q_ref[...]
    kall = jnp.concatenate([kp_ref[...], kc_ref[...]], axis=0)
    vall = jnp.concatenate([vp_ref[...], vc_ref[...]], axis=0)
    if not prev_always_valid:
        col = lax.broadcasted_iota(jnp.int32, (qb, span), 1)
        no_prev = (col < WINDOW) & (pl.program_id(1) == 0)
    outs = []
    for kh in range(B_KV_HEADS):
        kk = kall[:, kh * B_HEAD_DIM:(kh + 1) * B_HEAD_DIM]
        vv = vall[:, kh * B_HEAD_DIM:(kh + 1) * B_HEAD_DIM]
        for g in range(B_GROUP):
            h = kh * B_GROUP + g
            qh = q[:, h * B_HEAD_DIM:(h + 1) * B_HEAD_DIM] * scale
            s = _dot_nt(qh, kk) + bias_ref[h]
            if not prev_always_valid:
                s = jnp.where(no_prev, MASK_VALUE, s)
            sink = sink_ref[h]
            m = jnp.maximum(jnp.max(s, axis=-1, keepdims=True), sink)
            p = jnp.exp(s - m)
            denom = jnp.sum(p, axis=-1, keepdims=True) + jnp.exp(sink - m)
            outs.append(_dot(p, vv) / denom)
    o_ref[...] = jnp.concatenate(outs, axis=1)


def swa(proj, batch, seqlen, rel_bias, sinks, k_past=None, v_past=None):
    m = proj.shape[0]
    has_cache = k_past is not None
    qb = math.gcd(seqlen, WINDOW)
    nb = seqlen // qb
    span = WINDOW + qb
    dist = np.arange(qb)[:, None] + WINDOW - np.arange(span)[None, :]
    band = ((dist >= 0) & (dist < WINDOW)).astype(np.float32)
    bucket = _t5_bucket(dist)
    q_col = IN_A // B_WIDTH
    k_col = (IN_A + B_WIDTH) // B_KV_WIDTH
    v_col = k_col + 1
    cur = lambda c: (lambda b, n: (b * nb + n, c))
    if has_cache:
        assert nb == 1
        prev_k = pl.BlockSpec((WINDOW, B_KV_WIDTH), lambda b, n: (b, 0))
        prev_v = pl.BlockSpec((WINDOW, B_KV_WIDTH), lambda b, n: (b, 0))
        kp_arr, vp_arr = k_past, v_past
    else:
        assert qb == WINDOW
        prev = lambda c: (lambda b, n: (b * nb + jnp.maximum(n - 1, 0), c))
        prev_k = pl.BlockSpec((WINDOW, B_KV_WIDTH), prev(k_col))
        prev_v = pl.BlockSpec((WINDOW, B_KV_WIDTH), prev(v_col))
        kp_arr, vp_arr = proj, proj
    kern = functools.partial(_swa_kernel, qb=qb, span=span, prev_always_valid=has_cache)
    return pl.pallas_call(
        kern,
        grid=(batch, nb),
        in_specs=[pl.BlockSpec((qb, B_WIDTH), cur(q_col)),
                  prev_k,
                  pl.BlockSpec((qb, B_KV_WIDTH), cur(k_col)),
                  prev_v,
                  pl.BlockSpec((qb, B_KV_WIDTH), cur(v_col)),
                  pl.BlockSpec((qb, span), lambda b, n: (0, 0)),
                  pl.BlockSpec((qb, span), lambda b, n: (0, 0)),
                  pl.BlockSpec(memory_space=pltpu.SMEM),
                  pl.BlockSpec(memory_space=pltpu.SMEM)],
        out_specs=pl.BlockSpec((qb, B_WIDTH), lambda b, n: (b * nb + n, 0)),
        out_shape=jax.ShapeDtypeStruct((m, B_WIDTH), F32),
        scratch_shapes=[pltpu.VMEM((B_HEADS, qb, span), F32)],
        compiler_params=_cparams(("arbitrary", "arbitrary")),
        name="swa",
    )(proj, kp_arr, proj, vp_arr, proj, jnp.asarray(bucket), jnp.asarray(band), rel_bias, sinks)


def _rmsnorm_kernel(x_ref, g_ref, o_ref):
    o_ref[...] = _rms(x_ref[...], g_ref[...])


def rmsnorm_rows(x, g):
    m, d = x.shape
    tm = _pick_tile(m, (512, 256, 128, 64, 32, 16, 8))
    return pl.pallas_call(
        _rmsnorm_kernel,
        grid=(m // tm,),
        in_specs=[pl.BlockSpec((tm, d), lambda i: (i, 0)), pl.BlockSpec((1, d), lambda i: (0, 0))],
        out_specs=pl.BlockSpec((tm, d), lambda i: (i, 0)),
        out_shape=jax.ShapeDtypeStruct((m, d), F32),
        compiler_params=_cparams(("parallel",)),
        name="rmsnorm",
    )(x, g.reshape(1, d))


def _rwkv_proj_kernel(h_ref, hp_ref, mu_ref, w_ref, o_ref, xm_ref):
    @pl.when(pl.program_id(2) == 0)
    def _():
        h = h_ref[...]
        xm_ref[...] = (h + (hp_ref[...] - h) * mu_ref[0]).astype(BF16)

    o_ref[0] = jnp.dot(xm_ref[...], w_ref[0], preferred_element_type=F32)


def rwkv_proj(h, hp, mu3, w3_bf16, tn=512):
    m, d = h.shape
    tm = _pick_tile(m, (512, 256, 128, 64, 32, 16, 8))
    return pl.pallas_call(
        _rwkv_proj_kernel,
        grid=(m // tm, 3, d // tn),
        in_specs=[pl.BlockSpec((tm, d), lambda i, p, j: (i, 0)),
                  pl.BlockSpec((tm, d), lambda i, p, j: (i, 0)),
                  pl.BlockSpec((1, 1, d), lambda i, p, j: (p, 0, 0)),
                  pl.BlockSpec((1, d, tn), lambda i, p, j: (p, 0, j))],
        out_specs=pl.BlockSpec((1, tm, tn), lambda i, p, j: (p, i, j)),
        out_shape=jax.ShapeDtypeStruct((3, m, d), F32),
        scratch_shapes=[pltpu.VMEM((tm, d), BF16)],
        compiler_params=_cparams(("parallel", "arbitrary", "arbitrary")),
        name="rwkv_proj",
    )(h, hp, mu3, w3_bf16)


def _softplus(z):
    return jnp.maximum(z, 0.0) + jnp.log(1.0 + jnp.exp(-jnp.abs(z)))


def _rwkv_lora_kernel(*refs, has_vres):
    if has_vres:
        (h_ref, hp_ref, mu_ref, w0_ref, w1_ref, w2_ref, a0_ref, a1_ref, a2_ref, g1_ref, g2_ref,
         v0_ref, v1_ref, v2_ref, wl_ref, a_ref, g_ref, vg_ref) = refs
    else:
        (h_ref, hp_ref, mu_ref, w0_ref, w1_ref, w2_ref, a0_ref, a1_ref, a2_ref, g1_ref, g2_ref,
         wl_ref, a_ref, g_ref) = refs
    h = h_ref[...]
    xx = hp_ref[...] - h
    mix = lambda i: (h + xx * mu_ref[i:i + 1, :]).astype(BF16)
    xw, xa, xg = mix(0), mix(1), mix(2)
    wpre = w0_ref[...] + _dot(jnp.tanh(jnp.dot(xw, w1_ref[...], preferred_element_type=F32)), w2_ref[...])
    w_log = -_softplus(-wpre) - 0.5
    wl_ref[...] = -jnp.exp(w_log)
    apre = a0_ref[...] + _dot(jnp.dot(xa, a1_ref[...], preferred_element_type=F32), a2_ref[...])
    a_ref[...] = jax.nn.sigmoid(apre)
    g_ref[...] = _dot(jax.nn.sigmoid(jnp.dot(xg, g1_ref[...], preferred_element_type=F32)), g2_ref[...])
    if has_vres:
        xv = mix(3)
        vpre = v0_ref[...] + _dot(jnp.dot(xv, v1_ref[...], preferred_element_type=F32), v2_ref[...])
        vg_ref[...] = jax.nn.sigmoid(vpre)


def _pad_lora(w1, w2):
    r = w1.shape[1]
    rp = -(-r // LANES) * LANES
    return (jnp.pad(w1, ((0, 0), (0, rp - r))).astype(BF16),
            jnp.pad(w2, ((0, rp - r), (0, 0))).astype(BF16))


def rwkv_lora(h, hp, mu_wagv, w0, w1, w2, a0, a1, a2, g1, g2, vres=None):
    m, d = h.shape
    tm = _pick_tile(m, (256, 128, 64, 32, 16, 8))
    has_vres = vres is not None
    row = lambda i: (i, 0)
    const = lambda i: (0, 0)
    w1p, w2p = _pad_lora(w1, w2)
    a1p, a2p = _pad_lora(a1, a2)
    g1p, g2p = _pad_lora(g1, g2)
    args = [h, hp, mu_wagv, w0.reshape(1, d), w1p, w2p, a0.reshape(1, d), a1p, a2p, g1p, g2p]
    if has_vres:
        v1p, v2p = _pad_lora(vres[1], vres[2])
        args += [vres[0].reshape(1, d), v1p, v2p]
    in_specs = [pl.BlockSpec((tm, d), row), pl.BlockSpec((tm, d), row)]
    in_specs += [pl.BlockSpec(a.shape, const) for a in args[2:]]
    n_out = 4 if has_vres else 3
    return pl.pallas_call(
        functools.partial(_rwkv_lora_kernel, has_vres=has_vres),
        grid=(m // tm,),
        in_specs=in_specs,
        out_specs=[pl.BlockSpec((tm, d), row)] * n_out,
        out_shape=[jax.ShapeDtypeStruct((m, d), F32)] * n_out,
        compiler_params=_cparams(("parallel",)),
        name="rwkv_lora",
    )(*args)


def _rwkv_consts(c):
    i = np.arange(2 * c)
    same = (i[:, None] // c) == (i[None, :] // c)
    strict = (same & ((i[:, None] % c) > (i[None, :] % c))).astype(np.float32)
    incl = (same & ((i[:, None] % c) >= (i[None, :] % c))).astype(np.float32)
    eye = np.eye(2 * c, dtype=np.float32)
    tril = np.tril(np.ones((c, c), np.float32))
    l = np.arange(LANES)
    headones = ((l[:, None] // C_HEAD) == (l[None, :] // C_HEAD)).astype(np.float32)
    return strict, incl, eye, tril, headones


def _dot2_ones(x, ones_bf16):
    hi = x.astype(BF16)
    lo = (x - hi.astype(F32)).astype(BF16)
    return (jnp.dot(hi, ones_bf16, preferred_element_type=F32)
            + jnp.dot(lo, ones_bf16, preferred_element_type=F32))


def _rwkv_core_kernel(*refs, chunk, nchunks, npairs, has_state, has_vres):
    it = iter(refs)
    r_ref, k_ref, v_ref, wl_ref, a_ref, g_ref = (next(it) for _ in range(6))
    if has_vres:
        vf_ref, vg_ref = next(it), next(it)
    kk_ref, ka_ref, rk_ref, lg_ref, lb_ref = (next(it) for _ in range(5))
    strict_ref, incl_ref, eye_ref, tril_ref, hones_ref = (next(it) for _ in range(5))
    if has_state:
        s0_ref = next(it)
    y_ref, s_ref, st_ref = next(it), next(it), next(it)
    c = chunk
    l_idx = pl.program_id(2)
    lane = lax.broadcasted_iota(jnp.int32, (1, LANES), 1)
    m0 = (lane < C_HEAD).astype(F32)
    m1 = 1.0 - m0

    @pl.when(l_idx == 0)
    def _():
        for pi in range(npairs):
            if has_state:
                z = jnp.zeros((C_HEAD, C_HEAD), F32)
                top = jnp.concatenate([s0_ref[0, 2 * pi], z], axis=1)
                bot = jnp.concatenate([z, s0_ref[0, 2 * pi + 1]], axis=1)
                st_ref[pi] = jnp.concatenate([top, bot], axis=0)
            else:
                st_ref[pi] = jnp.zeros((LANES, LANES), F32)

    hones = hones_ref[...].astype(BF16)
    strict = strict_ref[...]
    incl = incl_ref[...]
    eye = eye_ref[...]
    tril = tril_ref[...]

    def stack(x):
        return jnp.concatenate([x * m0, x * m1], axis=0)

    def rowsums(xs):
        tot = _dot2_ones(jnp.concatenate(xs, axis=0), hones)
        return [tot[i * c:(i + 1) * c] for i in range(len(xs))]

    def load(pi, rows):
        cols = slice(pi * LANES, (pi + 1) * LANES)
        k = k_ref[0, rows, cols]
        v = v_ref[0, rows, cols]
        a = a_ref[rows, cols]
        if has_vres:
            v = v + (vf_ref[0, rows, cols] - v) * vg_ref[rows, cols]
        return dict(cols=cols, r=r_ref[0, rows, cols], v=v, a=a, wl=wl_ref[rows, cols],
                    kr=k * kk_ref[:, cols], kh=k * (1.0 + (a - 1.0) * ka_ref[:, cols]))

    def decays(p, ss):
        kk = p["kr"] * lax.rsqrt(jnp.maximum(ss, 1e-24))
        b = kk * p["a"]
        gc = _dot_exact01(tril, p["wl"])
        gl = gc[c - 1:c, :]
        e_neg = jnp.exp(-gc)
        e_out = jnp.exp(gl - gc)
        p.update(gl=gl, ab=-kk * jnp.exp(gc - p["wl"]), rb=p["r"] * jnp.exp(gc),
                 bt=b * e_neg, kt=p["kh"] * e_neg, bh=b * e_out, khat=p["kh"] * e_out)

    def intra(p):
        lhs = jnp.concatenate([stack(p["ab"]), stack(p["rb"])], axis=0)
        with_b = _dot_nt(lhs, jnp.concatenate([p["bt"], p["bt"]], axis=0))
        with_k = _dot_nt(lhs, jnp.concatenate([p["kt"], p["kt"]], axis=0))
        p.update(a_ab=with_b[:2 * c] * strict, a_rb=with_b[2 * c:] * incl,
                 a_ak=with_k[:2 * c] * strict, a_rk=with_k[2 * c:] * incl)

    def body(ci, carry):
        rows = pl.ds(pl.multiple_of(ci * c, c), c)
        ps = [load(pi, rows) for pi in range(npairs)]
        for p, ss in zip(ps, rowsums([p["kr"] * p["kr"] for p in ps])):
            decays(p, ss)
        for p in ps:
            intra(p)

        ts = [eye + p["a_ab"] for p in ps]
        pws = [p["a_ab"] for p in ps]
        if c > 2:
            pws = [_dot3_shared([pw], pw)[0] for pw in pws]
            n = 2
            while 2 * n < c:
                res = [_dot3_shared([t, pw], pw) for t, pw in zip(ts, pws)]
                ts = [t + r[0] for t, r in zip(ts, res)]
                pws = [r[1] for r in res]
                n *= 2
            ts = [t + _dot3_shared([t], pw)[0] for t, pw in zip(ts, pws)]

        sts = [st_ref[pi] for pi in range(npairs)]
        fss = [_dot_nt(jnp.concatenate([p["ab"], p["rb"]], axis=0), st) for p, st in zip(ps, sts)]
        vss = [stack(p["v"]) for p in ps]
        rhss = [stack(fs[:c]) + _dot(p["a_ak"], vs) for p, fs, vs in zip(ps, fss, vss)]
        uss = [_dot(t, rhs) for t, rhs in zip(ts, rhss)]
        yss = [_dot(p["a_rb"], us) + _dot(p["a_rk"], vs) for p, us, vs in zip(ps, uss, vss)]
        for pi, (p, st, us) in enumerate(zip(ps, sts, uss)):
            u = us[:c] + us[c:]
            upd = _dot_tn(jnp.concatenate([u, p["v"]], axis=0),
                          jnp.concatenate([p["bh"], p["khat"]], axis=0))
            st_ref[pi] = st * jnp.exp(p["gl"]) + upd * hones_ref[...]
        inv_n = 1.0 / C_HEAD
        ys_ = [fs[c:] + ys[:c] + ys[c:] for fs, ys in zip(fss, yss)]
        sums = rowsums(ys_ + [p["r"] * p["kh"] * rk_ref[:, p["cols"]] for p in ps])
        dlts = [y - m * inv_n for y, m in zip(ys_, sums[:npairs])]
        vars_ = rowsums([d * d for d in dlts])
        for p, dlt, var, bsum in zip(ps, dlts, vars_, sums[npairs:]):
            cols = p["cols"]
            yn = dlt * lax.rsqrt(var * inv_n + GN_EPS) * lg_ref[:, cols] + lb_ref[:, cols]
            y_ref[rows, cols] = ((yn + bsum * p["v"]) * g_ref[rows, cols]).astype(y_ref.dtype)
        return carry

    lax.fori_loop(0, nchunks, body, 0)

    @pl.when(l_idx == pl.num_programs(2) - 1)
    def _():
        for pi in range(npairs):
            st = st_ref[pi]
            s_ref[0, 2 * pi] = st[:C_HEAD, :C_HEAD]
            s_ref[0, 2 * pi + 1] = st[C_HEAD:, C_HEAD:]


def rwkv_core(rkv, wl, a, g, kk_p, ka_p, rk_p, lnx_g, lnx_b, batch, seqlen,
              s0=None, v_first=None, vgate=None, npairs=8):
    _, m, d = rkv.shape
    c = math.gcd(seqlen, RWKV_CHUNK)
    lblk = _pick_tile(seqlen, (256, 128, 64, 32, 16, 8))
    nl_blocks = seqlen // lblk
    has_state = s0 is not None
    has_vres = v_first is not None
    w = npairs * LANES
    ngroups = d // w
    strict, incl, eye, tril, hones = (jnp.asarray(x) for x in _rwkv_consts(c))
    row = lambda b, p, l: b * nl_blocks + l
    blk3 = lambda which: pl.BlockSpec((1, lblk, w), lambda b, p, l: (which, row(b, p, l), p))
    blk2 = pl.BlockSpec((lblk, w), lambda b, p, l: (row(b, p, l), p))
    par = pl.BlockSpec((1, w), lambda b, p, l: (0, p))
    full = lambda arr: pl.BlockSpec(arr.shape, lambda b, p, l: (0,) * arr.ndim)
    in_specs = [blk3(0), blk3(1), blk3(2), blk2, blk2, blk2]
    args = [rkv, rkv, rkv, wl, a, g]
    if has_vres:
        in_specs += [blk3(2), blk2]
        args += [v_first, vgate]
    in_specs += [par] * 5
    args += [x.reshape(1, d) for x in (kk_p, ka_p, rk_p, lnx_g, lnx_b)]
    in_specs += [full(x) for x in (strict, incl, eye, tril, hones)]
    args += [strict, incl, eye, tril, hones]
    st_blk = pl.BlockSpec((1, 2 * npairs, C_HEAD, C_HEAD), lambda b, p, l: (b, p, 0, 0))
    if has_state:
        in_specs.append(st_blk)
        args.append(s0)
    kern = functools.partial(_rwkv_core_kernel, chunk=c, nchunks=lblk // c, npairs=npairs,
                             has_state=has_state, has_vres=has_vres)
    return pl.pallas_call(
        kern,
        grid=(batch, ngroups, nl_blocks),
        in_specs=in_specs,
        out_specs=[blk2, st_blk],
        out_shape=[jax.ShapeDtypeStruct((m, d), F32),
                   jax.ShapeDtypeStruct((batch, C_HEADS, C_HEAD, C_HEAD), F32)],
        scratch_shapes=[pltpu.VMEM((npairs, LANES, LANES), F32)],
        compiler_params=_cparams(("parallel", "parallel", "arbitrary")),
        name="rwkv_core",
    )(*args)


def _even_layer(x, batch, seqlen, e, layer, p, st_hgrn, k_cache, v_cache):
    proj = norm_matmul(x, p["norm_mix_pre"][layer], p["w_in_even"][e])
    o_a, s_new = hgrn(proj, p["hgrn_lb_raw"], e, batch, seqlen, st_hgrn)
    k_lo = IN_A + B_WIDTH
    kb = proj[:, k_lo:k_lo + B_KV_WIDTH].reshape(batch, seqlen, B_KV_HEADS, B_HEAD_DIM)
    vb = proj[:, k_lo + B_KV_WIDTH:].reshape(batch, seqlen, B_KV_HEADS, B_HEAD_DIM)
    if k_cache is None:
        o_b = swa(proj, batch, seqlen, p["rel_bias"], p["attn_sinks"][e])
        k_new, v_new = kb[:, -WINDOW:], vb[:, -WINDOW:]
    else:
        o_b = swa(proj, batch, seqlen, p["rel_bias"], p["attn_sinks"][e],
                  k_cache.reshape(batch * WINDOW, B_KV_WIDTH), v_cache.reshape(batch * WINDOW, B_KV_WIDTH))
        k_new = jnp.concatenate([k_cache, kb], axis=1)[:, -WINDOW:]
        v_new = jnp.concatenate([v_cache, vb], axis=1)[:, -WINDOW:]
    x = even_out(o_a, proj, o_b, x, p["hgrn_norm_g"][e], p["w_out_even"][e], p["norm_mix_post"][layer])
    return x, s_new, k_new, v_new


def _odd_layer(x, batch, seqlen, o, layer, p, shift0, s0, v_first):
    m, d = x.shape
    h = rmsnorm_rows(x, p["norm_mix_pre"][layer])
    h3 = h.reshape(batch, seqlen, d)
    first = jnp.zeros((batch, 1, d), F32) if shift0 is None else shift0[:, None, :]
    hp = jnp.concatenate([first, h3[:, :-1]], axis=1).reshape(m, d)
    mu = p["rw_mu"][o]
    mu_rkv = jnp.stack([mu[0], mu[2], mu[3]])[:, None, :]
    mu_wagv = jnp.stack([mu[1], mu[4], mu[5], mu[3]])
    rkv = rwkv_proj(h, hp, mu_rkv, p["w_rkv"][o])
    vres = None if o == 0 else (p["rw_v0"][o - 1], p["rw_v1"][o - 1], p["rw_v2"][o - 1])
    outs = rwkv_lora(h, hp, mu_wagv, p["rw_w0"][o], p["rw_w1"][o], p["rw_w2"][o],
                     p["rw_a0"][o], p["rw_a1"][o], p["rw_a2"][o], p["rw_g1"][o], p["rw_g2"][o], vres)
    wl, a, g = outs[:3]
    vgate = outs[3] if o > 0 else None
    yg, s_new = rwkv_core(rkv, wl, a, g, p["rw_kk"][o], p["rw_ka"][o], p["rw_rk"][o],
                          p["rw_lnx_g"][o], p["rw_lnx_b"][o], batch, seqlen, s0,
                          v_first if o > 0 else None, vgate)
    x = odd_out(yg, x, p["rw_wo"][o], p["norm_mix_post"][layer])
    return x, s_new, h3[:, -1], rkv


def _trunk(x3, st_hgrn, k_cache, v_cache, st_rwkv, st_shift, p):
    batch, seqlen, d = x3.shape
    x = x3.reshape(batch * seqlen, d)
    has_state = st_hgrn is not None
    hgrn_out, k_out, v_out, rwkv_out, shift_out = [], [], [], [], []
    v_first = None
    for layer in range(DEPTH):
        if layer % 2 == 0:
            e = layer // 2
            x, s_new, k_new, v_new = _even_layer(
                x, batch, seqlen, e, layer, p,
                st_hgrn[e] if has_state else None,
                k_cache[e] if has_state else None,
                v_cache[e] if has_state else None)
            hgrn_out.append(s_new)
            k_out.append(k_new)
            v_out.append(v_new)
        else:
            o = layer // 2
            x, s_new, sh_new, rkv = _odd_layer(
                x, batch, seqlen, o, layer, p,
                st_shift[o] if has_state else None,
                st_rwkv[o] if has_state else None,
                v_first)
            if o == 0:
                v_first = rkv
            rwkv_out.append(s_new)
            shift_out.append(sh_new)
        x = ffn(x, p["norm_ffn_pre"][layer], p["w_up"][layer], p["w_down"][layer], p["norm_ffn_post"][layer])
    return (x.reshape(batch, seqlen, d), jnp.stack(hgrn_out), jnp.stack(k_out), jnp.stack(v_out),
            jnp.stack(rwkv_out), jnp.stack(shift_out))


def kernel(x_prompt, x_sample, state_hgrn, cache_swa_k, cache_swa_v, state_rwkv, state_shift,
           norm_mix_pre, norm_mix_post, norm_ffn_pre, norm_ffn_post,
           w_in_even, w_out_even, hgrn_lb_raw, hgrn_norm_g, rel_bias, attn_sinks,
           rw_mu, rw_wr, rw_wk, rw_wv, rw_wo, rw_w0, rw_w1, rw_w2, rw_a0, rw_a1, rw_a2,
           rw_v0, rw_v1, rw_v2, rw_g1, rw_g2, rw_kk, rw_ka, rw_rk, rw_lnx_g, rw_lnx_b,
           w_up, w_down):
    p = {
        "norm_mix_pre": norm_mix_pre, "norm_mix_post": norm_mix_post,
        "norm_ffn_pre": norm_ffn_pre, "norm_ffn_post": norm_ffn_post,
        "w_in_even": w_in_even.astype(BF16), "w_out_even": w_out_even.astype(BF16),
        "hgrn_lb_raw": hgrn_lb_raw, "hgrn_norm_g": hgrn_norm_g,
        "rel_bias": rel_bias, "attn_sinks": attn_sinks,
        "rw_mu": rw_mu, "w_rkv": jnp.stack([rw_wr, rw_wk, rw_wv], axis=1).astype(BF16),
        "rw_wo": rw_wo.astype(BF16),
        "rw_w0": rw_w0, "rw_w1": rw_w1, "rw_w2": rw_w2, "rw_a0": rw_a0, "rw_a1": rw_a1, "rw_a2": rw_a2,
        "rw_v0": rw_v0, "rw_v1": rw_v1, "rw_v2": rw_v2, "rw_g1": rw_g1, "rw_g2": rw_g2,
        "rw_kk": rw_kk, "rw_ka": rw_ka, "rw_rk": rw_rk, "rw_lnx_g": rw_lnx_g, "rw_lnx_b": rw_lnx_b,
        "w_up": w_up.astype(BF16), "w_down": w_down.astype(BF16),
    }
    y_p, hgrn_p, k_p, v_p, rwkv_p, shift_p = _trunk(x_prompt, None, None, None, None, None, p)
    y_s, hgrn_s, k_s, v_s, rwkv_s, shift_s = _trunk(
        x_sample, state_hgrn, cache_swa_k, cache_swa_v, state_rwkv, state_shift, p)
    return (y_p, y_s, hgrn_p, hgrn_s, k_p, k_s, v_p, v_s, rwkv_p, rwkv_s, shift_p, shift_s)
```

```python
import functools
import math

import numpy as np
import jax
import jax.numpy as jnp
from jax import lax
from jax.experimental import pallas as pl
from jax.experimental.pallas import tpu as pltpu

F32 = jnp.float32
BF16 = jnp.bfloat16

D_MODEL = 2048
DEPTH = 4
N_EVEN = 2
N_ODD = 2
A_HEADS = 8
A_KDIM = 128
A_VDIM = 128
A_WIDTH = 1024
A_QK = 1024
B_HEADS = 16
B_HEAD_DIM = 64
B_KV_HEADS = 4
B_GROUP = 4
B_WIDTH = 1024
B_KV_WIDTH = 256
WINDOW = 128
N_BUCKETS = 32
MAX_DISTANCE = 128
MASK_VALUE = -1e30
IN_A = 4096
IN_EVEN = 5632
C_HEAD = 64
C_HEADS = 32
GN_EPS = 64e-5
D_FF = 8192
NORM_EPS = 1e-6

LANES = 128
VMEM_LIMIT = 56 * 1024 * 1024

HGRN_CHUNK = 64
RWKV_CHUNK = 64


def _cparams(sem):
    return pltpu.CompilerParams(dimension_semantics=sem, vmem_limit_bytes=VMEM_LIMIT)


def _rms(x, g):
    return x * lax.rsqrt(jnp.mean(x * x, axis=-1, keepdims=True) + NORM_EPS) * g


def _dot(a, b):
    return jnp.dot(a.astype(BF16), b.astype(BF16), preferred_element_type=F32)


def _dot_nt(a, b):
    return lax.dot_general(a.astype(BF16), b.astype(BF16), (((1,), (1,)), ((), ())),
                           preferred_element_type=F32)


def _dot_tn(a, b):
    return lax.dot_general(a.astype(BF16), b.astype(BF16), (((0,), (0,)), ((), ())),
                           preferred_element_type=F32)


def _split2(x):
    hi = x.astype(BF16)
    return hi, (x - hi.astype(F32)).astype(BF16)


def _dot3_shared(lhs_list, b):
    bh, bl = _split2(b)
    parts = [_split2(a) for a in lhs_list]
    his = [p[0] for p in parts]
    los = [p[1] for p in parts]
    by_hi = jnp.dot(jnp.concatenate(his + los, axis=0), bh, preferred_element_type=F32)
    by_lo = jnp.dot(jnp.concatenate(his, axis=0), bl, preferred_element_type=F32)
    n = sum(a.shape[0] for a in lhs_list)
    out, off = [], 0
    for a in lhs_list:
        m = a.shape[0]
        out.append(by_hi[off:off + m] + by_hi[n + off:n + off + m] + by_lo[off:off + m])
        off += m
    return out


def _dot_exact01(sel, x):
    hi = x.astype(BF16)
    r1 = x - hi.astype(F32)
    mid = r1.astype(BF16)
    lo = (r1 - mid.astype(F32)).astype(BF16)
    s = sel.astype(BF16)
    return (jnp.dot(s, hi, preferred_element_type=F32)
            + jnp.dot(s, mid, preferred_element_type=F32)
            + jnp.dot(s, lo, preferred_element_type=F32))


def _pick_tile(m, cands):
    for c in cands:
        if m % c == 0:
            return c
    return m


def _norm_matmul_kernel(x_ref, g_ref, w_ref, o_ref, xn_ref):
    @pl.when(pl.program_id(1) == 0)
    def _():
        xn_ref[...] = _rms(x_ref[...], g_ref[...]).astype(BF16)

    o_ref[...] = jnp.dot(xn_ref[...], w_ref[...], preferred_element_type=F32)


def norm_matmul(x, g, w_bf16, tn=1408):
    m, d = x.shape
    n = w_bf16.shape[1]
    tm = _pick_tile(m, (768, 512, 256, 128, 64, 32, 16, 8))
    return pl.pallas_call(
        _norm_matmul_kernel,
        grid=(m // tm, n // tn),
        in_specs=[pl.BlockSpec((tm, d), lambda i, j: (i, 0)),
                  pl.BlockSpec((1, d), lambda i, j: (0, 0)),
                  pl.BlockSpec((d, tn), lambda i, j: (0, j))],
        out_specs=pl.BlockSpec((tm, tn), lambda i, j: (i, j)),
        out_shape=jax.ShapeDtypeStruct((m, n), F32),
        scratch_shapes=[pltpu.VMEM((tm, d), BF16)],
        compiler_params=_cparams(("parallel", "arbitrary")),
        name="norm_matmul",
    )(x, g.reshape(1, d), w_bf16)


def _ffn_kernel(x_ref, gpre_ref, wup_ref, wdn_ref, gpost_ref, o_ref, xn_ref, acc_ref):
    f = pl.program_id(1)

    @pl.when(f == 0)
    def _():
        xn_ref[...] = _rms(x_ref[...], gpre_ref[...]).astype(BF16)
        acc_ref[...] = jnp.zeros_like(acc_ref)

    h = jnp.dot(xn_ref[...], wup_ref[...], preferred_element_type=F32)
    h = jnp.square(jnp.maximum(h, 0.0)).astype(BF16)
    acc_ref[...] += jnp.dot(h, wdn_ref[...], preferred_element_type=F32)

    @pl.when(f == pl.num_programs(1) - 1)
    def _():
        o_ref[...] = x_ref[...] + _rms(acc_ref[...], gpost_ref[...])


def ffn(x, gpre, wup_bf16, wdn_bf16, gpost, tf=1024):
    m, d = x.shape
    dff = wup_bf16.shape[1]
    tm = _pick_tile(m, (512, 256, 128, 64, 32, 16, 8))
    return pl.pallas_call(
        _ffn_kernel,
        grid=(m // tm, dff // tf),
        in_specs=[pl.BlockSpec((tm, d), lambda i, f: (i, 0)),
                  pl.BlockSpec((1, d), lambda i, f: (0, 0)),
                  pl.BlockSpec((d, tf), lambda i, f: (0, f)),
                  pl.BlockSpec((tf, d), lambda i, f: (f, 0)),
                  pl.BlockSpec((1, d), lambda i, f: (0, 0))],
        out_specs=pl.BlockSpec((tm, d), lambda i, f: (i, 0)),
        out_shape=jax.ShapeDtypeStruct((m, d), F32),
        scratch_shapes=[pltpu.VMEM((tm, d), BF16), pltpu.VMEM((tm, d), F32)],
        compiler_params=_cparams(("parallel", "arbitrary")),
        name="ffn",
    )(x, gpre.reshape(1, d), wup_bf16, wdn_bf16, gpost.reshape(1, d))


def _even_out_kernel(oa_ref, ga_ref, ob_ref, x_ref, ag_ref, w_ref, gpost_ref, o_ref):
    ga = ga_ref[...]
    oan = _rms(oa_ref[...], ag_ref[...]) * (ga * jax.nn.sigmoid(ga))
    mix = (jnp.dot(oan.astype(BF16), w_ref[:A_WIDTH, :], preferred_element_type=F32)
           + jnp.dot(ob_ref[...].astype(BF16), w_ref[A_WIDTH:, :], preferred_element_type=F32))
    o_ref[...] = x_ref[...] + _rms(mix, gpost_ref[...])


def even_out(o_a, proj, o_b, x, a_norm_g, w_out_bf16, gpost):
    m, d = x.shape
    tm = _pick_tile(m, (384, 256, 128, 64, 32, 16, 8))
    ga_blk = (3 * A_WIDTH) // A_WIDTH
    return pl.pallas_call(
        _even_out_kernel,
        grid=(m // tm,),
        in_specs=[pl.BlockSpec((tm, A_WIDTH), lambda i: (i, 0)),
                  pl.BlockSpec((tm, A_WIDTH), lambda i: (i, ga_blk)),
                  pl.BlockSpec((tm, B_WIDTH), lambda i: (i, 0)),
                  pl.BlockSpec((tm, d), lambda i: (i, 0)),
                  pl.BlockSpec((1, A_WIDTH), lambda i: (0, 0)),
                  pl.BlockSpec((A_WIDTH + B_WIDTH, d), lambda i: (0, 0)),
                  pl.BlockSpec((1, d), lambda i: (0, 0))],
        out_specs=pl.BlockSpec((tm, d), lambda i: (i, 0)),
        out_shape=jax.ShapeDtypeStruct((m, d), F32),
        compiler_params=_cparams(("parallel",)),
        name="even_out",
    )(o_a, proj, o_b, x, a_norm_g.reshape(1, A_WIDTH), w_out_bf16, gpost.reshape(1, d))


def _odd_out_kernel(y_ref, x_ref, w_ref, gpost_ref, o_ref):
    mix = jnp.dot(y_ref[...].astype(BF16), w_ref[...], preferred_element_type=F32)
    o_ref[...] = x_ref[...] + _rms(mix, gpost_ref[...])


def odd_out(yg, x, wo_bf16, gpost):
    m, d = x.shape
    tm = _pick_tile(m, (384, 256, 128, 64, 32, 16, 8))
    return pl.pallas_call(
        _odd_out_kernel,
        grid=(m // tm,),
        in_specs=[pl.BlockSpec((tm, d), lambda i: (i, 0)),
                  pl.BlockSpec((tm, d), lambda i: (i, 0)),
                  pl.BlockSpec((d, d), lambda i: (0, 0)),
                  pl.BlockSpec((1, d), lambda i: (0, 0))],
        out_specs=pl.BlockSpec((tm, d), lambda i: (i, 0)),
        out_shape=jax.ShapeDtypeStruct((m, d), F32),
        compiler_params=_cparams(("parallel",)),
        name="odd_out",
    )(yg, x, wo_bf16, gpost.reshape(1, d))


def _level_consts(c):
    levels = []
    s = c // 2
    while s >= 1:
        levels.append(s)
        s //= 2
    nl = len(levels)
    sel = np.zeros((nl, c, c), np.float32)
    mask = np.zeros((nl, c, c), np.float32)
    idx = np.arange(c)
    for l, s in enumerate(levels):
        ref_row = (idx // (2 * s)) * (2 * s) + s - 1
        sel[l, idx, ref_row] = 1.0
        same = (idx[:, None] // (2 * s)) == (idx[None, :] // (2 * s))
        upper = (idx[:, None] % (2 * s)) >= s
        lower = (idx[None, :] % (2 * s)) < s
        mask[l] = (same & upper & lower).astype(np.float32)
    tril = np.tril(np.ones((c, c), np.float32))
    return nl, sel.reshape(nl * c, c), mask, tril


def _hgrn_kernel(*refs, layer, chunk, nchunks, nlevels, has_state, nheads):
    if has_state:
        (q_ref, f_ref, i_ref, lb_ref, sel_ref, mask_ref, tril_ref, s0_ref,
         o_ref, s_ref, st_ref) = refs
    else:
        (q_ref, f_ref, i_ref, lb_ref, sel_ref, mask_ref, tril_ref,
         o_ref, s_ref, st_ref) = refs
    c = chunk
    nh = nheads
    l_idx = pl.program_id(2)

    @pl.when(l_idx == 0)
    def _():
        for hi in range(nh):
            if has_state:
                st_ref[hi] = s0_ref[0, hi].T
            else:
                st_ref[hi] = jnp.zeros((A_VDIM, A_KDIM), F32)

    lbr = lb_ref[...]
    e = jnp.exp(lbr - jnp.max(lbr, axis=0, keepdims=True))
    p = e / jnp.sum(e, axis=0, keepdims=True)
    lb = jnp.zeros((1, nh * LANES), F32)
    for i in range(1, layer + 1):
        lb = lb + p[i:i + 1, :]
    one_m_lb = 1.0 - lb
    head = lambda x, hi: x[:, hi * LANES:(hi + 1) * LANES]

    def body(ci, carry):
        rows = pl.ds(pl.multiple_of(ci * c, c), c)
        fq = f_ref[rows, :]
        qr = q_ref[rows, :]
        v = i_ref[rows, :]
        logf = jnp.log(lb + one_m_lb * jax.nn.sigmoid(fq))
        k = one_m_lb * jax.nn.sigmoid(-fq)
        q = qr * jax.nn.sigmoid(qr) * (A_KDIM ** -0.5)

        g = _dot_exact01(tril_ref[...], logf)
        refs_g = _dot_exact01(sel_ref[...], g)
        glast = g[c - 1:c, :]
        q_in = q * jnp.exp(g)
        kd = k * jnp.exp(glast - g)
        dec = jnp.exp(glast)
        diag = q * k

        sts = [st_ref[hi] for hi in range(nh)]
        os_ = [_dot_nt(head(q_in, hi), sts[hi]) for hi in range(nh)]
        attns = [jnp.zeros((c, c), F32) for _ in range(nh)]
        for l in range(nlevels):
            gr = refs_g[l * c:(l + 1) * c, :]
            qs = q * jnp.exp(jnp.minimum(g - gr, 0.0))
            ks = k * jnp.exp(jnp.minimum(gr - g, 0.0))
            ml = mask_ref[l]
            attns = [at + ml * _dot_nt(head(qs, hi), head(ks, hi)) for hi, at in enumerate(attns)]
        for hi in range(nh):
            vh = head(v, hi)
            o = os_[hi] + _dot(attns[hi], vh) + jnp.sum(head(diag, hi), axis=-1, keepdims=True) * vh
            o_ref[rows, hi * LANES:(hi + 1) * LANES] = o
        for hi in range(nh):
            st_ref[hi] = sts[hi] * head(dec, hi) + _dot_tn(head(v, hi), head(kd, hi))
        return carry

    lax.fori_loop(0, nchunks, body, 0, unroll=2 if nchunks % 2 == 0 else 1)

    @pl.when(l_idx == pl.num_programs(2) - 1)
    def _():
        for hi in range(nh):
            s_ref[0, hi] = st_ref[hi].T


def hgrn(proj, lb_raw, layer, batch, seqlen, s0=None, nheads=4):
    m = proj.shape[0]
    c = math.gcd(seqlen, HGRN_CHUNK)
    lblk = _pick_tile(seqlen, (512, 256, 128, 64, 32, 16, 8))
    nl_blocks = seqlen // lblk
    nlevels, sel, mask, tril = _level_consts(c)
    has_state = s0 is not None
    w = nheads * LANES
    ngroups = A_HEADS // nheads
    kern = functools.partial(_hgrn_kernel, layer=layer, chunk=c, nchunks=lblk // c,
                             nlevels=nlevels, has_state=has_state, nheads=nheads)
    row = lambda b, h, l: b * nl_blocks + l
    in_specs = [pl.BlockSpec((lblk, w), lambda b, h, l: (row(b, h, l), h)),
                pl.BlockSpec((lblk, w), lambda b, h, l: (row(b, h, l), ngroups + h)),
                pl.BlockSpec((lblk, w), lambda b, h, l: (row(b, h, l), 2 * ngroups + h)),
                pl.BlockSpec((N_EVEN, w), lambda b, h, l: (0, h)),
                pl.BlockSpec((nlevels * c, c), lambda b, h, l: (0, 0)),
                pl.BlockSpec((nlevels, c, c), lambda b, h, l: (0, 0, 0)),
                pl.BlockSpec((c, c), lambda b, h, l: (0, 0))]
    args = [proj, proj, proj, lb_raw, jnp.asarray(sel), jnp.asarray(mask), jnp.asarray(tril)]
    st_blk = pl.BlockSpec((1, nheads, A_KDIM, A_VDIM), lambda b, h, l: (b, h, 0, 0))
    if has_state:
        in_specs.append(st_blk)
        args.append(s0)
    return pl.pallas_call(
        kern,
        grid=(batch, ngroups, nl_blocks),
        in_specs=in_specs,
        out_specs=[pl.BlockSpec((lblk, w), lambda b, h, l: (row(b, h, l), h)), st_blk],
        out_shape=[jax.ShapeDtypeStruct((m, A_WIDTH), F32),
                   jax.ShapeDtypeStruct((batch, A_HEADS, A_KDIM, A_VDIM), F32)],
        scratch_shapes=[pltpu.VMEM((nheads, A_VDIM, A_KDIM), F32)],
        compiler_params=_cparams(("parallel", "parallel", "arbitrary")),
        name="hgrn",
    )(*args)


def _t5_bucket(dist):
    max_exact = N_BUCKETS // 2
    d = np.maximum(dist, 0)
    large = max_exact + (np.log(np.maximum(d, max_exact).astype(np.float32) / max_exact)
                         / math.log(MAX_DISTANCE / max_exact) * (N_BUCKETS - max_exact)).astype(np.int32)
    large = np.minimum(large, N_BUCKETS - 1)
    return np.where(d < max_exact, d, large).astype(np.int32)


def _swa_kernel(q_ref, kp_ref, kc_ref, vp_ref, vc_ref, bucket_ref, band_ref, rb_ref, sink_ref,
                o_ref, bias_ref, *, qb, span, prev_always_valid):
    first = (pl.program_id(0) == 0) & (pl.program_id(1) == 0)

    @pl.when(first)
    def _():
        bk = bucket_ref[...]
        band = band_ref[...]

        def per_head(h, carry):
            def per_bucket(bi, acc):
                return jnp.where(bk == bi, rb_ref[bi, h], acc)
            acc = lax.fori_loop(0, N_BUCKETS, per_bucket, jnp.zeros((qb, span), F32))
            bias_ref[h] = jnp.where(band > 0, acc, MASK_VALUE)
            return carry

        lax.fori_loop(0, B_HEADS, per_head, 0)

    scale = B_HEAD_DIM ** -0.5
    q = q_ref[...]
    kall = jnp.concatenate([kp_ref[...], kc_ref[...]], axis=0)
    vall = jnp.concatenate([vp_ref[...], vc_ref[...]], axis=0)
    if not prev_always_valid:
        col = lax.broadcasted_iota(jnp.int32, (qb, span), 1)
        no_prev = (col < WINDOW) & (pl.program_id(1) == 0)
    outs = []
    for kh in range(B_KV_HEADS):
        kk = kall[:, kh * B_HEAD_DIM:(kh + 1) * B_HEAD_DIM]
        vv = vall[:, kh * B_HEAD_DIM:(kh + 1) * B_HEAD_DIM]
        for g in range(B_GROUP):
            h = kh * B_GROUP + g
            qh = q[:, h * B_HEAD_DIM:(h + 1) * B_HEAD_DIM] * scale
            s = _dot_nt(qh, kk) + bias_ref[h]
            if not prev_always_valid:
                s = jnp.where(no_prev, MASK_VALUE, s)
            sink = sink_ref[h]
            m = jnp.maximum(jnp.max(s, axis=-1, keepdims=True), sink)
            p = jnp.exp(s - m)
            denom = jnp.sum(p, axis=-1, keepdims=True) + jnp.exp(sink - m)
            outs.append(_dot(p, vv) / denom)
    o_ref[...] = jnp.concatenate(outs, axis=1)


def swa(proj, batch, seqlen, rel_bias, sinks, k_past=None, v_past=None):
    m = proj.shape[0]
    has_cache = k_past is not None
    qb = math.gcd(seqlen, WINDOW)
    nb = seqlen // qb
    span = WINDOW + qb
    dist = np.arange(qb)[:, None] + WINDOW - np.arange(span)[None, :]
    band = ((dist >= 0) & (dist < WINDOW)).astype(np.float32)
    bucket = _t5_bucket(dist)
    q_col = IN_A // B_WIDTH
    k_col = (IN_A + B_WIDTH) // B_KV_WIDTH
    v_col = k_col + 1
    cur = lambda c: (lambda b, n: (b * nb + n, c))
    if has_cache:
        assert nb == 1
        prev_k = pl.BlockSpec((WINDOW, B_KV_WIDTH), lambda b, n: (b, 0))
        prev_v = pl.BlockSpec((WINDOW, B_KV_WIDTH), lambda b, n: (b, 0))
        kp_arr, vp_arr = k_past, v_past
    else:
        assert qb == WINDOW
        prev = lambda c: (lambda b, n: (b * nb + jnp.maximum(n - 1, 0), c))
        prev_k = pl.BlockSpec((WINDOW, B_KV_WIDTH), prev(k_col))
        prev_v = pl.BlockSpec((WINDOW, B_KV_WIDTH), prev(v_col))
        kp_arr, vp_arr = proj, proj
    kern = functools.partial(_swa_kernel, qb=qb, span=span, prev_always_valid=has_cache)
    return pl.pallas_call(
        kern,
        grid=(batch, nb),
        in_specs=[pl.BlockSpec((qb, B_WIDTH), cur(q_col)),
                  prev_k,
                  pl.BlockSpec((qb, B_KV_WIDTH), cur(k_col)),
                  prev_v,
                  pl.BlockSpec((qb, B_KV_WIDTH), cur(v_col)),
                  pl.BlockSpec((qb, span), lambda b, n: (0, 0)),
                  pl.BlockSpec((qb, span), lambda b, n: (0, 0)),
                  pl.BlockSpec(memory_space=pltpu.SMEM),
                  pl.BlockSpec(memory_space=pltpu.SMEM)],
        out_specs=pl.BlockSpec((qb, B_WIDTH), lambda b, n: (b * nb + n, 0)),
        out_shape=jax.ShapeDtypeStruct((m, B_WIDTH), F32),
        scratch_shapes=[pltpu.VMEM((B_HEADS, qb, span), F32)],
        compiler_params=_cparams(("arbitrary", "arbitrary")),
        name="swa",
    )(proj, kp_arr, proj, vp_arr, proj, jnp.asarray(bucket), jnp.asarray(band), rel_bias, sinks)


def _rmsnorm_kernel(x_ref, g_ref, o_ref):
    o_ref[...] = _rms(x_ref[...], g_ref[...])


def rmsnorm_rows(x, g):
    m, d = x.shape
    tm = _pick_tile(m, (512, 256, 128, 64, 32, 16, 8))
    return pl.pallas_call(
        _rmsnorm_kernel,
        grid=(m // tm,),
        in_specs=[pl.BlockSpec((tm, d), lambda i: (i, 0)), pl.BlockSpec((1, d), lambda i: (0, 0))],
        out_specs=pl.BlockSpec((tm, d), lambda i: (i, 0)),
        out_shape=jax.ShapeDtypeStruct((m, d), F32),
        compiler_params=_cparams(("parallel",)),
        name="rmsnorm",
    )(x, g.reshape(1, d))


def _rwkv_proj_kernel(h_ref, hp_ref, mu_ref, w_ref, o_ref, xm_ref):
    @pl.when(pl.program_id(2) == 0)
    def _():
        h = h_ref[...]
        xm_ref[...] = (h + (hp_ref[...] - h) * mu_ref[0]).astype(BF16)

    o_ref[0] = jnp.dot(xm_ref[...], w_ref[0], preferred_element_type=F32)


def rwkv_proj(h, hp, mu3, w3_bf16, tn=2048):
    m, d = h.shape
    tm = _pick_tile(m, (512, 256, 128, 64, 32, 16, 8))
    return pl.pallas_call(
        _rwkv_proj_kernel,
        grid=(m // tm, 3, d // tn),
        in_specs=[pl.BlockSpec((tm, d), lambda i, p, j: (i, 0)),
                  pl.BlockSpec((tm, d), lambda i, p, j: (i, 0)),
                  pl.BlockSpec((1, 1, d), lambda i, p, j: (p, 0, 0)),
                  pl.BlockSpec((1, d, tn), lambda i, p, j: (p, 0, j))],
        out_specs=pl.BlockSpec((1, tm, tn), lambda i, p, j: (p, i, j)),
        out_shape=jax.ShapeDtypeStruct((3, m, d), F32),
        scratch_shapes=[pltpu.VMEM((tm, d), BF16)],
        compiler_params=_cparams(("parallel", "arbitrary", "arbitrary")),
        name="rwkv_proj",
    )(h, hp, mu3, w3_bf16)


def _softplus(z):
    return jnp.maximum(z, 0.0) + jnp.log(1.0 + jnp.exp(-jnp.abs(z)))


def _rwkv_lora_kernel(*refs, has_vres):
    if has_vres:
        (h_ref, hp_ref, mu_ref, w0_ref, w1_ref, w2_ref, a0_ref, a1_ref, a2_ref, g1_ref, g2_ref,
         v0_ref, v1_ref, v2_ref, wl_ref, a_ref, g_ref, vg_ref) = refs
    else:
        (h_ref, hp_ref, mu_ref, w0_ref, w1_ref, w2_ref, a0_ref, a1_ref, a2_ref, g1_ref, g2_ref,
         wl_ref, a_ref, g_ref) = refs
    h = h_ref[...]
    xx = hp_ref[...] - h
    mix = lambda i: (h + xx * mu_ref[i:i + 1, :]).astype(BF16)
    xw, xa, xg = mix(0), mix(1), mix(2)
    wpre = w0_ref[...] + _dot(jnp.tanh(jnp.dot(xw, w1_ref[...], preferred_element_type=F32)), w2_ref[...])
    w_log = -_softplus(-wpre) - 0.5
    wl_ref[...] = -jnp.exp(w_log)
    apre = a0_ref[...] + _dot(jnp.dot(xa, a1_ref[...], preferred_element_type=F32), a2_ref[...])
    a_ref[...] = jax.nn.sigmoid(apre)
    g_ref[...] = _dot(jax.nn.sigmoid(jnp.dot(xg, g1_ref[...], preferred_element_type=F32)), g2_ref[...])
    if has_vres:
        xv = mix(3)
        vpre = v0_ref[...] + _dot(jnp.dot(xv, v1_ref[...], preferred_element_type=F32), v2_ref[...])
        vg_ref[...] = jax.nn.sigmoid(vpre)


def _pad_lora(w1, w2):
    r = w1.shape[1]
    rp = -(-r // LANES) * LANES
    return (jnp.pad(w1, ((0, 0), (0, rp - r))).astype(BF16),
            jnp.pad(w2, ((0, rp - r), (0, 0))).astype(BF16))


def rwkv_lora(h, hp, mu_wagv, w0, w1, w2, a0, a1, a2, g1, g2, vres=None):
    m, d = h.shape
    tm = _pick_tile(m, (256, 128, 64, 32, 16, 8))
    has_vres = vres is not None
    row = lambda i: (i, 0)
    const = lambda i: (0, 0)
    w1p, w2p = _pad_lora(w1, w2)
    a1p, a2p = _pad_lora(a1, a2)
    g1p, g2p = _pad_lora(g1, g2)
    args = [h, hp, mu_wagv, w0.reshape(1, d), w1p, w2p, a0.reshape(1, d), a1p, a2p, g1p, g2p]
    if has_vres:
        v1p, v2p = _pad_lora(vres[1], vres[2])
        args += [vres[0].reshape(1, d), v1p, v2p]
    in_specs = [pl.BlockSpec((tm, d), row), pl.BlockSpec((tm, d), row)]
    in_specs += [pl.BlockSpec(a.shape, const) for a in args[2:]]
    n_out = 4 if has_vres else 3
    return pl.pallas_call(
        functools.partial(_rwkv_lora_kernel, has_vres=has_vres),
        grid=(m // tm,),
        in_specs=in_specs,
        out_specs=[pl.BlockSpec((tm, d), row)] * n_out,
        out_shape=[jax.ShapeDtypeStruct((m, d), F32)] * n_out,
        compiler_params=_cparams(("parallel",)),
        name="rwkv_lora",
    )(*args)


def _rwkv_consts(c):
    i = np.arange(2 * c)
    same = (i[:, None] // c) == (i[None, :] // c)
    strict = (same & ((i[:, None] % c) > (i[None, :] % c))).astype(np.float32)
    incl = (same & ((i[:, None] % c) >= (i[None, :] % c))).astype(np.float32)
    eye = np.eye(2 * c, dtype=np.float32)
    tril = np.tril(np.ones((c, c), np.float32))
    l = np.arange(LANES)
    headones = ((l[:, None] // C_HEAD) == (l[None, :] // C_HEAD)).astype(np.float32)
    return strict, incl, eye, tril, headones


def _dot2_ones(x, ones_bf16):
    hi = x.astype(BF16)
    lo = (x - hi.astype(F32)).astype(BF16)
    return (jnp.dot(hi, ones_bf16, preferred_element_type=F32)
            + jnp.dot(lo, ones_bf16, preferred_element_type=F32))


def _rwkv_core_kernel(*refs, chunk, nchunks, npairs, has_state, has_vres):
    it = iter(refs)
    r_ref, k_ref, v_ref, wl_ref, a_ref, g_ref = (next(it) for _ in range(6))
    if has_vres:
        vf_ref, vg_ref = next(it), next(it)
    kk_ref, ka_ref, rk_ref, lg_ref, lb_ref = (next(it) for _ in range(5))
    strict_ref, incl_ref, eye_ref, tril_ref, hones_ref = (next(it) for _ in range(5))
    if has_state:
        s0_ref = next(it)
    y_ref, s_ref, st_ref = next(it), next(it), next(it)
    c = chunk
    l_idx = pl.program_id(2)
    lane = lax.broadcasted_iota(jnp.int32, (1, LANES), 1)
    m0 = (lane < C_HEAD).astype(F32)
    m1 = 1.0 - m0

    @pl.when(l_idx == 0)
    def _():
        for pi in range(npairs):
            if has_state:
                z = jnp.zeros((C_HEAD, C_HEAD), F32)
                top = jnp.concatenate([s0_ref[0, 2 * pi], z], axis=1)
                bot = jnp.concatenate([z, s0_ref[0, 2 * pi + 1]], axis=1)
                st_ref[pi] = jnp.concatenate([top, bot], axis=0)
            else:
                st_ref[pi] = jnp.zeros((LANES, LANES), F32)

    hones = hones_ref[...].astype(BF16)
    strict = strict_ref[...]
    incl = incl_ref[...]
    eye = eye_ref[...]
    tril = tril_ref[...]

    def stack(x):
        return jnp.concatenate([x * m0, x * m1], axis=0)

    def rowsums(xs):
        tot = _dot2_ones(jnp.concatenate(xs, axis=0), hones)
        return [tot[i * c:(i + 1) * c] for i in range(len(xs))]

    def load(pi, rows):
        cols = slice(pi * LANES, (pi + 1) * LANES)
        k = k_ref[0, rows, cols]
        v = v_ref[0, rows, cols]
        a = a_ref[rows, cols]
        if has_vres:
            v = v + (vf_ref[0, rows, cols] - v) * vg_ref[rows, cols]
        return dict(cols=cols, r=r_ref[0, rows, cols], v=v, a=a, wl=wl_ref[rows, cols],
                    kr=k * kk_ref[:, cols], kh=k * (1.0 + (a - 1.0) * ka_ref[:, cols]))

    def decays(p, ss):
        kk = p["kr"] * lax.rsqrt(jnp.maximum(ss, 1e-24))
        b = kk * p["a"]
        gc = _dot_exact01(tril, p["wl"])
        gl = gc[c - 1:c, :]
        e_neg = jnp.exp(-gc)
        e_out = jnp.exp(gl - gc)
        p.update(gl=gl, ab=-kk * jnp.exp(gc - p["wl"]), rb=p["r"] * jnp.exp(gc),
                 bt=b * e_neg, kt=p["kh"] * e_neg, bh=b * e_out, khat=p["kh"] * e_out)

    def intra(p):
        lhs = jnp.concatenate([stack(p["ab"]), stack(p["rb"])], axis=0)
        with_b = _dot_nt(lhs, jnp.concatenate([p["bt"], p["bt"]], axis=0))
        with_k = _dot_nt(lhs, jnp.concatenate([p["kt"], p["kt"]], axis=0))
        p.update(a_ab=with_b[:2 * c] * strict, a_rb=with_b[2 * c:] * incl,
                 a_ak=with_k[:2 * c] * strict, a_rk=with_k[2 * c:] * incl)

    def body(ci, carry):
        rows = pl.ds(pl.multiple_of(ci * c, c), c)
        ps = [load(pi, rows) for pi in range(npairs)]
        for p, ss in zip(ps, rowsums([p["kr"] * p["kr"] for p in ps])):
            decays(p, ss)
        for p in ps:
            intra(p)

        ts = [eye + p["a_ab"] for p in ps]
        pws = [p["a_ab"] for p in ps]
        if c > 2:
            pws = [_dot3_shared([pw], pw)[0] for pw in pws]
            n = 2
            while 2 * n < c:
                res = [_dot3_shared([t, pw], pw) for t, pw in zip(ts, pws)]
                ts = [t + r[0] for t, r in zip(ts, res)]
                pws = [r[1] for r in res]
                n *= 2
            ts = [t + _dot3_shared([t], pw)[0] for t, pw in zip(ts, pws)]

        sts = [st_ref[pi] for pi in range(npairs)]
        fss = [_dot_nt(jnp.concatenate([p["ab"], p["rb"]], axis=0), st) for p, st in zip(ps, sts)]
        vss = [stack(p["v"]) for p in ps]
        rhss = [stack(fs[:c]) + _dot(p["a_ak"], vs) for p, fs, vs in zip(ps, fss, vss)]
        uss = [_dot(t, rhs) for t, rhs in zip(ts, rhss)]
        yss = [_dot(p["a_rb"], us) + _dot(p["a_rk"], vs) for p, us, vs in zip(ps, uss, vss)]
        for pi, (p, st, us) in enumerate(zip(ps, sts, uss)):
            u = us[:c] + us[c:]
            upd = _dot_tn(jnp.concatenate([u, p["v"]], axis=0),
                          jnp.concatenate([p["bh"], p["khat"]], axis=0))
            st_ref[pi] = st * jnp.exp(p["gl"]) + upd * hones_ref[...]
        inv_n = 1.0 / C_HEAD
        ys_ = [fs[c:] + ys[:c] + ys[c:] for fs, ys in zip(fss, yss)]
        sums = rowsums(ys_ + [p["r"] * p["kh"] * rk_ref[:, p["cols"]] for p in ps])
        dlts = [y - m * inv_n for y, m in zip(ys_, sums[:npairs])]
        vars_ = rowsums([d * d for d in dlts])
        for p, dlt, var, bsum in zip(ps, dlts, vars_, sums[npairs:]):
            cols = p["cols"]
            yn = dlt * lax.rsqrt(var * inv_n + GN_EPS) * lg_ref[:, cols] + lb_ref[:, cols]
            y_ref[rows, cols] = ((yn + bsum * p["v"]) * g_ref[rows, cols]).astype(y_ref.dtype)
        return carry

    lax.fori_loop(0, nchunks, body, 0)

    @pl.when(l_idx == pl.num_programs(2) - 1)
    def _():
        for pi in range(npairs):
            st = st_ref[pi]
            s_ref[0, 2 * pi] = st[:C_HEAD, :C_HEAD]
            s_ref[0, 2 * pi + 1] = st[C_HEAD:, C_HEAD:]


def rwkv_core(rkv, wl, a, g, kk_p, ka_p, rk_p, lnx_g, lnx_b, batch, seqlen,
              s0=None, v_first=None, vgate=None, npairs=8):
    _, m, d = rkv.shape
    c = math.gcd(seqlen, RWKV_CHUNK)
    lblk = _pick_tile(seqlen, (256, 128, 64, 32, 16, 8))
    nl_blocks = seqlen // lblk
    has_state = s0 is not None
    has_vres = v_first is not None
    w = npairs * LANES
    ngroups = d // w
    strict, incl, eye, tril, hones = (jnp.asarray(x) for x in _rwkv_consts(c))
    row = lambda b, p, l: b * nl_blocks + l
    blk3 = lambda which: pl.BlockSpec((1, lblk, w), lambda b, p, l: (which, row(b, p, l), p))
    blk2 = pl.BlockSpec((lblk, w), lambda b, p, l: (row(b, p, l), p))
    par = pl.BlockSpec((1, w), lambda b, p, l: (0, p))
    full = lambda arr: pl.BlockSpec(arr.shape, lambda b, p, l: (0,) * arr.ndim)
    in_specs = [blk3(0), blk3(1), blk3(2), blk2, blk2, blk2]
    args = [rkv, rkv, rkv, wl, a, g]
    if has_vres:
        in_specs += [blk3(2), blk2]
        args += [v_first, vgate]
    in_specs += [par] * 5
    args += [x.reshape(1, d) for x in (kk_p, ka_p, rk_p, lnx_g, lnx_b)]
    in_specs += [full(x) for x in (strict, incl, eye, tril, hones)]
    args += [strict, incl, eye, tril, hones]
    st_blk = pl.BlockSpec((1, 2 * npairs, C_HEAD, C_HEAD), lambda b, p, l: (b, p, 0, 0))
    if has_state:
        in_specs.append(st_blk)
        args.append(s0)
    kern = functools.partial(_rwkv_core_kernel, chunk=c, nchunks=lblk // c, npairs=npairs,
                             has_state=has_state, has_vres=has_vres)
    return pl.pallas_call(
        kern,
        grid=(batch, ngroups, nl_blocks),
        in_specs=in_specs,
        out_specs=[blk2, st_blk],
        out_shape=[jax.ShapeDtypeStruct((m, d), F32),
                   jax.ShapeDtypeStruct((batch, C_HEADS, C_HEAD, C_HEAD), F32)],
        scratch_shapes=[pltpu.VMEM((npairs, LANES, LANES), F32)],
        compiler_params=_cparams(("parallel", "parallel", "arbitrary")),
        name="rwkv_core",
    )(*args)


def _even_layer(x, batch, seqlen, e, layer, p, st_hgrn, k_cache, v_cache):
    proj = norm_matmul(x, p["norm_mix_pre"][layer], p["w_in_even"][e])
    o_a, s_new = hgrn(proj, p["hgrn_lb_raw"], e, batch, seqlen, st_hgrn)
    k_lo = IN_A + B_WIDTH
    kb = proj[:, k_lo:k_lo + B_KV_WIDTH].reshape(batch, seqlen, B_KV_HEADS, B_HEAD_DIM)
    vb = proj[:, k_lo + B_KV_WIDTH:].reshape(batch, seqlen, B_KV_HEADS, B_HEAD_DIM)
    if k_cache is None:
        o_b = swa(proj, batch, seqlen, p["rel_bias"], p["attn_sinks"][e])
        k_new, v_new = kb[:, -WINDOW:], vb[:, -WINDOW:]
    else:
        o_b = swa(proj, batch, seqlen, p["rel_bias"], p["attn_sinks"][e],
                  k_cache.reshape(batch * WINDOW, B_KV_WIDTH), v_cache.reshape(batch * WINDOW, B_KV_WIDTH))
        k_new = jnp.concatenate([k_cache, kb], axis=1)[:, -WINDOW:]
        v_new = jnp.concatenate([v_cache, vb], axis=1)[:, -WINDOW:]
    x = even_out(o_a, proj, o_b, x, p["hgrn_norm_g"][e], p["w_out_even"][e], p["norm_mix_post"][layer])
    return x, s_new, k_new, v_new


def _odd_layer(x, batch, seqlen, o, layer, p, shift0, s0, v_first):
    m, d = x.shape
    h = rmsnorm_rows(x, p["norm_mix_pre"][layer])
    h3 = h.reshape(batch, seqlen, d)
    first = jnp.zeros((batch, 1, d), F32) if shift0 is None else shift0[:, None, :]
    hp = jnp.concatenate([first, h3[:, :-1]], axis=1).reshape(m, d)
    mu = p["rw_mu"][o]
    mu_rkv = jnp.stack([mu[0], mu[2], mu[3]])[:, None, :]
    mu_wagv = jnp.stack([mu[1], mu[4], mu[5], mu[3]])
    rkv = rwkv_proj(h, hp, mu_rkv, p["w_rkv"][o])
    vres = None if o == 0 else (p["rw_v0"][o - 1], p["rw_v1"][o - 1], p["rw_v2"][o - 1])
    outs = rwkv_lora(h, hp, mu_wagv, p["rw_w0"][o], p["rw_w1"][o], p["rw_w2"][o],
                     p["rw_a0"][o], p["rw_a1"][o], p["rw_a2"][o], p["rw_g1"][o], p["rw_g2"][o], vres)
    wl, a, g = outs[:3]
    vgate = outs[3] if o > 0 else None
    yg, s_new = rwkv_core(rkv, wl, a, g, p["rw_kk"][o], p["rw_ka"][o], p["rw_rk"][o],
                          p["rw_lnx_g"][o], p["rw_lnx_b"][o], batch, seqlen, s0,
                          v_first if o > 0 else None, vgate)
    x = odd_out(yg, x, p["rw_wo"][o], p["norm_mix_post"][layer])
    return x, s_new, h3[:, -1], rkv


def _trunk(x3, st_hgrn, k_cache, v_cache, st_rwkv, st_shift, p):
    batch, seqlen, d = x3.shape
    x = x3.reshape(batch * seqlen, d)
    has_state = st_hgrn is not None
    hgrn_out, k_out, v_out, rwkv_out, shift_out = [], [], [], [], []
    v_first = None
    for layer in range(DEPTH):
        if layer % 2 == 0:
            e = layer // 2
            x, s_new, k_new, v_new = _even_layer(
                x, batch, seqlen, e, layer, p,
                st_hgrn[e] if has_state else None,
                k_cache[e] if has_state else None,
                v_cache[e] if has_state else None)
            hgrn_out.append(s_new)
            k_out.append(k_new)
            v_out.append(v_new)
        else:
            o = layer // 2
            x, s_new, sh_new, rkv = _odd_layer(
                x, batch, seqlen, o, layer, p,
                st_shift[o] if has_state else None,
                st_rwkv[o] if has_state else None,
                v_first)
            if o == 0:
                v_first = rkv
            rwkv_out.append(s_new)
            shift_out.append(sh_new)
        x = ffn(x, p["norm_ffn_pre"][layer], p["w_up"][layer], p["w_down"][layer], p["norm_ffn_post"][layer])
    return (x.reshape(batch, seqlen, d), jnp.stack(hgrn_out), jnp.stack(k_out), jnp.stack(v_out),
            jnp.stack(rwkv_out), jnp.stack(shift_out))


def kernel(x_prompt, x_sample, state_hgrn, cache_swa_k, cache_swa_v, state_rwkv, state_shift,
           norm_mix_pre, norm_mix_post, norm_ffn_pre, norm_ffn_post,
           w_in_even, w_out_even, hgrn_lb_raw, hgrn_norm_g, rel_bias, attn_sinks,
           rw_mu, rw_wr, rw_wk, rw_wv, rw_wo, rw_w0, rw_w1, rw_w2, rw_a0, rw_a1, rw_a2,
           rw_v0, rw_v1, rw_v2, rw_g1, rw_g2, rw_kk, rw_ka, rw_rk, rw_lnx_g, rw_lnx_b,
           w_up, w_down):
    p = {
        "norm_mix_pre": norm_mix_pre, "norm_mix_post": norm_mix_post,
        "norm_ffn_pre": norm_ffn_pre, "norm_ffn_post": norm_ffn_post,
        "w_in_even": w_in_even.astype(BF16), "w_out_even": w_out_even.astype(BF16),
        "hgrn_lb_raw": hgrn_lb_raw, "hgrn_norm_g": hgrn_norm_g,
        "rel_bias": rel_bias, "attn_sinks": attn_sinks,
        "rw_mu": rw_mu, "w_rkv": jnp.stack([rw_wr, rw_wk, rw_wv], axis=1).astype(BF16),
        "rw_wo": rw_wo.astype(BF16),
        "rw_w0": rw_w0, "rw_w1": rw_w1, "rw_w2": rw_w2, "rw_a0": rw_a0, "rw_a1": rw_a1, "rw_a2": rw_a2,
        "rw_v0": rw_v0, "rw_v1": rw_v1, "rw_v2": rw_v2, "rw_g1": rw_g1, "rw_g2": rw_g2,
        "rw_kk": rw_kk, "rw_ka": rw_ka, "rw_rk": rw_rk, "rw_lnx_g": rw_lnx_g, "rw_lnx_b": rw_lnx_b,
        "w_up": w_up.astype(BF16), "w_down": w_down.astype(BF16),
    }
    y_p, hgrn_p, k_p, v_p, rwkv_p, shift_p = _trunk(x_prompt, None, None, None, None, None, p)
    y_s, hgrn_s, k_s, v_s, rwkv_s, shift_s = _trunk(
        x_sample, state_hgrn, cache_swa_k, cache_swa_v, state_rwkv, state_shift, p)
    return (y_p, y_s, hgrn_p, hgrn_s, k_p, k_s, v_p, v_s, rwkv_p, rwkv_s, shift_p, shift_s)
```

```python
import functools
import math

import numpy as np
import jax
import jax.numpy as jnp
from jax import lax
from jax.experimental import pallas as pl
from jax.experimental.pallas import tpu as pltpu

F32 = jnp.float32
BF16 = jnp.bfloat16

D_MODEL = 2048
DEPTH = 4
N_EVEN = 2
N_ODD = 2
A_HEADS = 8
A_KDIM = 128
A_VDIM = 128
A_WIDTH = 1024
A_QK = 1024
B_HEADS = 16
B_HEAD_DIM = 64
B_KV_HEADS = 4
B_GROUP = 4
B_WIDTH = 1024
B_KV_WIDTH = 256
WINDOW = 128
N_BUCKETS = 32
MAX_DISTANCE = 128
MASK_VALUE = -1e30
IN_A = 4096
IN_EVEN = 5632
C_HEAD = 64
C_HEADS = 32
GN_EPS = 64e-5
D_FF = 8192
NORM_EPS = 1e-6

LANES = 128
VMEM_LIMIT = 56 * 1024 * 1024

HGRN_CHUNK = 64
RWKV_CHUNK = 64


def _cparams(sem):
    return pltpu.CompilerParams(dimension_semantics=sem, vmem_limit_bytes=VMEM_LIMIT)


def _rms(x, g):
    return x * lax.rsqrt(jnp.mean(x * x, axis=-1, keepdims=True) + NORM_EPS) * g


def _dot(a, b):
    return jnp.dot(a.astype(BF16), b.astype(BF16), preferred_element_type=F32)


def _dot_nt(a, b):
    return lax.dot_general(a.astype(BF16), b.astype(BF16), (((1,), (1,)), ((), ())),
                           preferred_element_type=F32)


def _dot_tn(a, b):
    return lax.dot_general(a.astype(BF16), b.astype(BF16), (((0,), (0,)), ((), ())),
                           preferred_element_type=F32)


def _split2(x):
    hi = x.astype(BF16)
    return hi, (x - hi.astype(F32)).astype(BF16)


def _dot3_shared(lhs_list, b):
    bh, bl = _split2(b)
    parts = [_split2(a) for a in lhs_list]
    his = [p[0] for p in parts]
    los = [p[1] for p in parts]
    by_hi = jnp.dot(jnp.concatenate(his + los, axis=0), bh, preferred_element_type=F32)
    by_lo = jnp.dot(jnp.concatenate(his, axis=0), bl, preferred_element_type=F32)
    n = sum(a.shape[0] for a in lhs_list)
    out, off = [], 0
    for a in lhs_list:
        m = a.shape[0]
        out.append(by_hi[off:off + m] + by_hi[n + off:n + off + m] + by_lo[off:off + m])
        off += m
    return out


def _dot_exact01(sel, x):
    hi = x.astype(BF16)
    r1 = x - hi.astype(F32)
    mid = r1.astype(BF16)
    lo = (r1 - mid.astype(F32)).astype(BF16)
    s = sel.astype(BF16)
    return (jnp.dot(s, hi, preferred_element_type=F32)
            + jnp.dot(s, mid, preferred_element_type=F32)
            + jnp.dot(s, lo, preferred_element_type=F32))


def _pick_tile(m, cands):
    for c in cands:
        if m % c == 0:
            return c
    return m


def _norm_matmul_kernel(x_ref, g_ref, w_ref, o_ref, xn_ref):
    @pl.when(pl.program_id(1) == 0)
    def _():
        xn_ref[...] = _rms(x_ref[...], g_ref[...]).astype(BF16)

    o_ref[...] = jnp.dot(xn_ref[...], w_ref[...], preferred_element_type=F32)


def norm_matmul(x, g, w_stack_bf16, li, tn=1408):
    m, d = x.shape
    n = w_stack_bf16.shape[2]
    tm = _pick_tile(m, (512, 256, 128, 64, 32, 16, 8))
    return pl.pallas_call(
        _norm_matmul_kernel,
        grid=(m // tm, n // tn),
        in_specs=[pl.BlockSpec((tm, d), lambda i, j: (i, 0)),
                  pl.BlockSpec((1, d), lambda i, j: (0, 0)),
                  pl.BlockSpec((None, d, tn), lambda i, j: (li, 0, j))],
        out_specs=pl.BlockSpec((tm, tn), lambda i, j: (i, j)),
        out_shape=jax.ShapeDtypeStruct((m, n), F32),
        scratch_shapes=[pltpu.VMEM((tm, d), BF16)],
        compiler_params=_cparams(("parallel", "arbitrary")),
        name="norm_matmul",
    )(x, g.reshape(1, d), w_stack_bf16)


def _ffn_kernel(x_ref, gpre_ref, wup_ref, wdn_ref, gpost_ref, o_ref, xn_ref, acc_ref):
    f = pl.program_id(1)

    @pl.when(f == 0)
    def _():
        xn_ref[...] = _rms(x_ref[...], gpre_ref[...]).astype(BF16)
        acc_ref[...] = jnp.zeros_like(acc_ref)

    h = jnp.dot(xn_ref[...], wup_ref[...], preferred_element_type=F32)
    h = jnp.square(jnp.maximum(h, 0.0)).astype(BF16)
    acc_ref[...] += jnp.dot(h, wdn_ref[...], preferred_element_type=F32)

    @pl.when(f == pl.num_programs(1) - 1)
    def _():
        o_ref[...] = x_ref[...] + _rms(acc_ref[...], gpost_ref[...])


def ffn(x, gpre, wup_bf16, wdn_bf16, gpost, li, tf=1024):
    m, d = x.shape
    dff = wup_bf16.shape[2]
    tm = _pick_tile(m, (512, 256, 128, 64, 32, 16, 8))
    return pl.pallas_call(
        _ffn_kernel,
        grid=(m // tm, dff // tf),
        in_specs=[pl.BlockSpec((tm, d), lambda i, f: (i, 0)),
                  pl.BlockSpec((1, d), lambda i, f: (0, 0)),
                  pl.BlockSpec((None, d, tf), lambda i, f: (li, 0, f)),
                  pl.BlockSpec((None, tf, d), lambda i, f: (li, f, 0)),
                  pl.BlockSpec((1, d), lambda i, f: (0, 0))],
        out_specs=pl.BlockSpec((tm, d), lambda i, f: (i, 0)),
        out_shape=jax.ShapeDtypeStruct((m, d), F32),
        scratch_shapes=[pltpu.VMEM((tm, d), BF16), pltpu.VMEM((tm, d), F32)],
        compiler_params=_cparams(("parallel", "arbitrary")),
        name="ffn",
    )(x, gpre.reshape(1, d), wup_bf16, wdn_bf16, gpost.reshape(1, d))


def _even_out_kernel(oa_ref, ga_ref, ob_ref, x_ref, ag_ref, w_ref, gpost_ref, o_ref):
    ga = ga_ref[...]
    oan = _rms(oa_ref[...], ag_ref[...]) * (ga * jax.nn.sigmoid(ga))
    mix = (jnp.dot(oan.astype(BF16), w_ref[:A_WIDTH, :], preferred_element_type=F32)
           + jnp.dot(ob_ref[...].astype(BF16), w_ref[A_WIDTH:, :], preferred_element_type=F32))
    o_ref[...] = x_ref[...] + _rms(mix, gpost_ref[...])


def even_out(o_a, proj, o_b, x, a_norm_g, w_out_bf16, gpost, li):
    m, d = x.shape
    tm = _pick_tile(m, (384, 256, 128, 64, 32, 16, 8))
    ga_blk = (3 * A_WIDTH) // A_WIDTH
    return pl.pallas_call(
        _even_out_kernel,
        grid=(m // tm,),
        in_specs=[pl.BlockSpec((tm, A_WIDTH), lambda i: (i, 0)),
                  pl.BlockSpec((tm, A_WIDTH), lambda i: (i, ga_blk)),
                  pl.BlockSpec((tm, B_WIDTH), lambda i: (i, 0)),
                  pl.BlockSpec((tm, d), lambda i: (i, 0)),
                  pl.BlockSpec((1, A_WIDTH), lambda i: (0, 0)),
                  pl.BlockSpec((None, A_WIDTH + B_WIDTH, d), lambda i: (li, 0, 0)),
                  pl.BlockSpec((1, d), lambda i: (0, 0))],
        out_specs=pl.BlockSpec((tm, d), lambda i: (i, 0)),
        out_shape=jax.ShapeDtypeStruct((m, d), F32),
        compiler_params=_cparams(("parallel",)),
        name="even_out",
    )(o_a, proj, o_b, x, a_norm_g.reshape(1, A_WIDTH), w_out_bf16, gpost.reshape(1, d))


def _odd_out_kernel(y_ref, x_ref, w_ref, gpost_ref, o_ref):
    mix = jnp.dot(y_ref[...].astype(BF16), w_ref[...], preferred_element_type=F32)
    o_ref[...] = x_ref[...] + _rms(mix, gpost_ref[...])


def odd_out(yg, x, wo_bf16, gpost, li):
    m, d = x.shape
    tm = _pick_tile(m, (384, 256, 128, 64, 32, 16, 8))
    return pl.pallas_call(
        _odd_out_kernel,
        grid=(m // tm,),
        in_specs=[pl.BlockSpec((tm, d), lambda i: (i, 0)),
                  pl.BlockSpec((tm, d), lambda i: (i, 0)),
                  pl.BlockSpec((None, d, d), lambda i: (li, 0, 0)),
                  pl.BlockSpec((1, d), lambda i: (0, 0))],
        out_specs=pl.BlockSpec((tm, d), lambda i: (i, 0)),
        out_shape=jax.ShapeDtypeStruct((m, d), F32),
        compiler_params=_cparams(("parallel",)),
        name="odd_out",
    )(yg, x, wo_bf16, gpost.reshape(1, d))


def _level_consts(c):
    levels = []
    s = c // 2
    while s >= 1:
        levels.append(s)
        s //= 2
    nl = len(levels)
    sel = np.zeros((nl, c, c), np.float32)
    mask = np.zeros((nl, c, c), np.float32)
    idx = np.arange(c)
    for l, s in enumerate(levels):
        ref_row = (idx // (2 * s)) * (2 * s) + s - 1
        sel[l, idx, ref_row] = 1.0
        same = (idx[:, None] // (2 * s)) == (idx[None, :] // (2 * s))
        upper = (idx[:, None] % (2 * s)) >= s
        lower = (idx[None, :] % (2 * s)) < s
        mask[l] = (same & upper & lower).astype(np.float32)
    tril = np.tril(np.ones((c, c), np.float32))
    return nl, sel.reshape(nl * c, c), mask, tril


def _hgrn_kernel(*refs, layer, chunk, nchunks, nlevels, has_state, nheads):
    if has_state:
        (q_ref, f_ref, i_ref, lb_ref, sel_ref, mask_ref, tril_ref, s0_ref,
         o_ref, s_ref, st_ref) = refs
    else:
        (q_ref, f_ref, i_ref, lb_ref, sel_ref, mask_ref, tril_ref,
         o_ref, s_ref, st_ref) = refs
    c = chunk
    nh = nheads
    l_idx = pl.program_id(2)

    @pl.when(l_idx == 0)
    def _():
        for hi in range(nh):
            if has_state:
                st_ref[hi] = s0_ref[0, hi].T
            else:
                st_ref[hi] = jnp.zeros((A_VDIM, A_KDIM), F32)

    lbr = lb_ref[...]
    e = jnp.exp(lbr - jnp.max(lbr, axis=0, keepdims=True))
    p = e / jnp.sum(e, axis=0, keepdims=True)
    lb = jnp.zeros((1, nh * LANES), F32)
    for i in range(1, layer + 1):
        lb = lb + p[i:i + 1, :]
    one_m_lb = 1.0 - lb
    head = lambda x, hi: x[:, hi * LANES:(hi + 1) * LANES]

    def body(ci, carry):
        rows = pl.ds(pl.multiple_of(ci * c, c), c)
        fq = f_ref[rows, :]
        qr = q_ref[rows, :]
        v = i_ref[rows, :]
        logf = jnp.log(lb + one_m_lb * jax.nn.sigmoid(fq))
        k = one_m_lb * jax.nn.sigmoid(-fq)
        q = qr * jax.nn.sigmoid(qr) * (A_KDIM ** -0.5)

        g = _dot_exact01(tril_ref[...], logf)
        refs_g = _dot_exact01(sel_ref[...], g)
        glast = g[c - 1:c, :]
        q_in = q * jnp.exp(g)
        kd = k * jnp.exp(glast - g)
        dec = jnp.exp(glast)
        diag = q * k

        sts = [st_ref[hi] for hi in range(nh)]
        os_ = [_dot_nt(head(q_in, hi), sts[hi]) for hi in range(nh)]
        attns = [jnp.zeros((c, c), F32) for _ in range(nh)]
        for l in range(nlevels):
            gr = refs_g[l * c:(l + 1) * c, :]
            qs = q * jnp.exp(jnp.minimum(g - gr, 0.0))
            ks = k * jnp.exp(jnp.minimum(gr - g, 0.0))
            ml = mask_ref[l]
            attns = [at + ml * _dot_nt(head(qs, hi), head(ks, hi)) for hi, at in enumerate(attns)]
        for hi in range(nh):
            vh = head(v, hi)
            o = os_[hi] + _dot(attns[hi], vh) + jnp.sum(head(diag, hi), axis=-1, keepdims=True) * vh
            o_ref[rows, hi * LANES:(hi + 1) * LANES] = o
        for hi in range(nh):
            st_ref[hi] = sts[hi] * head(dec, hi) + _dot_tn(head(v, hi), head(kd, hi))
        return carry

    lax.fori_loop(0, nchunks, body, 0, unroll=2 if nchunks % 2 == 0 else 1)

    @pl.when(l_idx == pl.num_programs(2) - 1)
    def _():
        for hi in range(nh):
            s_ref[0, hi] = st_ref[hi].T


def hgrn(proj, lb_raw, layer, batch, seqlen, s0=None, nheads=4):
    m = proj.shape[0]
    c = math.gcd(seqlen, HGRN_CHUNK)
    lblk = _pick_tile(seqlen, (512, 256, 128, 64, 32, 16, 8))
    nl_blocks = seqlen // lblk
    nlevels, sel, mask, tril = _level_consts(c)
    has_state = s0 is not None
    w = nheads * LANES
    ngroups = A_HEADS // nheads
    kern = functools.partial(_hgrn_kernel, layer=layer, chunk=c, nchunks=lblk // c,
                             nlevels=nlevels, has_state=has_state, nheads=nheads)
    row = lambda b, h, l: b * nl_blocks + l
    in_specs = [pl.BlockSpec((lblk, w), lambda b, h, l: (row(b, h, l), h)),
                pl.BlockSpec((lblk, w), lambda b, h, l: (row(b, h, l), ngroups + h)),
                pl.BlockSpec((lblk, w), lambda b, h, l: (row(b, h, l), 2 * ngroups + h)),
                pl.BlockSpec((N_EVEN, w), lambda b, h, l: (0, h)),
                pl.BlockSpec((nlevels * c, c), lambda b, h, l: (0, 0)),
                pl.BlockSpec((nlevels, c, c), lambda b, h, l: (0, 0, 0)),
                pl.BlockSpec((c, c), lambda b, h, l: (0, 0))]
    args = [proj, proj, proj, lb_raw, jnp.asarray(sel), jnp.asarray(mask), jnp.asarray(tril)]
    st_blk = pl.BlockSpec((1, nheads, A_KDIM, A_VDIM), lambda b, h, l: (b, h, 0, 0))
    if has_state:
        in_specs.append(pl.BlockSpec((None, 1, nheads, A_KDIM, A_VDIM), lambda b, h, l: (layer, b, h, 0, 0)))
        args.append(s0)
    return pl.pallas_call(
        kern,
        grid=(batch, ngroups, nl_blocks),
        in_specs=in_specs,
        out_specs=[pl.BlockSpec((lblk, w), lambda b, h, l: (row(b, h, l), h)), st_blk],
        out_shape=[jax.ShapeDtypeStruct((m, A_WIDTH), F32),
                   jax.ShapeDtypeStruct((batch, A_HEADS, A_KDIM, A_VDIM), F32)],
        scratch_shapes=[pltpu.VMEM((nheads, A_VDIM, A_KDIM), F32)],
        compiler_params=_cparams(("parallel", "parallel", "arbitrary")),
        name="hgrn",
    )(*args)


def _t5_bucket(dist):
    max_exact = N_BUCKETS // 2
    d = np.maximum(dist, 0)
    large = max_exact + (np.log(np.maximum(d, max_exact).astype(np.float32) / max_exact)
                         / math.log(MAX_DISTANCE / max_exact) * (N_BUCKETS - max_exact)).astype(np.int32)
    large = np.minimum(large, N_BUCKETS - 1)
    return np.where(d < max_exact, d, large).astype(np.int32)


def _swa_kernel(q_ref, kp_ref, kc_ref, vp_ref, vc_ref, bucket_ref, band_ref, rb_ref, sink_ref,
                o_ref, bias_ref, *, qb, span, prev_always_valid):
    first = (pl.program_id(0) == 0) & (pl.program_id(1) == 0)

    @pl.when(first)
    def _():
        bk = bucket_ref[...]
        band = band_ref[...]

        def per_head(h, carry):
            def per_bucket(bi, acc):
                return jnp.where(bk == bi, rb_ref[bi, h], acc)
            acc = lax.fori_loop(0, N_BUCKETS, per_bucket, jnp.zeros((qb, span), F32))
            bias_ref[h] = jnp.where(band > 0, acc, MASK_VALUE)
            return carry

        lax.fori_loop(0, B_HEADS, per_head, 0)

    scale = B_HEAD_DIM ** -0.5
    q = q_ref[...]
    kall = jnp.concatenate([kp_ref[...], kc_ref[...]], axis=0)
    vall = jnp.concatenate([vp_ref[...], vc_ref[...]], axis=0)
    if not prev_always_valid:
        col = lax.broadcasted_iota(jnp.int32, (qb, span), 1)
        no_prev = (col < WINDOW) & (pl.program_id(1) == 0)
    heads = range(B_HEADS)
    ks = [kall[:, kh * B_HEAD_DIM:(kh + 1) * B_HEAD_DIM].astype(BF16) for kh in range(B_KV_HEADS)]
    vs = [vall[:, kh * B_HEAD_DIM:(kh + 1) * B_HEAD_DIM].astype(BF16) for kh in range(B_KV_HEADS)]
    qs = [(q[:, h * B_HEAD_DIM:(h + 1) * B_HEAD_DIM] * scale).astype(BF16) for h in heads]
    ss = [_dot_nt(qs[h], ks[h // B_GROUP]) + bias_ref[h] for h in heads]
    if not prev_always_valid:
        ss = [jnp.where(no_prev, MASK_VALUE, s) for s in ss]
    ms = [jnp.maximum(jnp.max(ss[h], axis=-1, keepdims=True), sink_ref[h]) for h in heads]
    ps = [jnp.exp(s - m) for s, m in zip(ss, ms)]
    denoms = [jnp.sum(ps[h], axis=-1, keepdims=True) + jnp.exp(sink_ref[h] - ms[h]) for h in heads]
    outs = [_dot(ps[h], vs[h // B_GROUP]) / denoms[h] for h in heads]
    o_ref[...] = jnp.concatenate(outs, axis=1)


def swa(proj, batch, seqlen, rel_bias, sinks, layer, k_past=None, v_past=None):
    m = proj.shape[0]
    has_cache = k_past is not None
    qb = math.gcd(seqlen, WINDOW)
    nb = seqlen // qb
    span = WINDOW + qb
    dist = np.arange(qb)[:, None] + WINDOW - np.arange(span)[None, :]
    band = ((dist >= 0) & (dist < WINDOW)).astype(np.float32)
    bucket = _t5_bucket(dist)
    q_col = IN_A // B_WIDTH
    k_col = (IN_A + B_WIDTH) // B_KV_WIDTH
    v_col = k_col + 1
    cur = lambda c: (lambda b, n: (b * nb + n, c))
    if has_cache:
        assert nb == 1
        prev_k = pl.BlockSpec((None, WINDOW, B_KV_WIDTH), lambda b, n: (layer, b, 0))
        prev_v = pl.BlockSpec((None, WINDOW, B_KV_WIDTH), lambda b, n: (layer, b, 0))
        kp_arr = k_past.reshape(k_past.shape[0], batch * WINDOW, B_KV_WIDTH)
        vp_arr = v_past.reshape(v_past.shape[0], batch * WINDOW, B_KV_WIDTH)
    else:
        assert qb == WINDOW
        prev = lambda c: (lambda b, n: (b * nb + jnp.maximum(n - 1, 0), c))
        prev_k = pl.BlockSpec((WINDOW, B_KV_WIDTH), prev(k_col))
        prev_v = pl.BlockSpec((WINDOW, B_KV_WIDTH), prev(v_col))
        kp_arr, vp_arr = proj, proj
    kern = functools.partial(_swa_kernel, qb=qb, span=span, prev_always_valid=has_cache)
    return pl.pallas_call(
        kern,
        grid=(batch, nb),
        in_specs=[pl.BlockSpec((qb, B_WIDTH), cur(q_col)),
                  prev_k,
                  pl.BlockSpec((qb, B_KV_WIDTH), cur(k_col)),
                  prev_v,
                  pl.BlockSpec((qb, B_KV_WIDTH), cur(v_col)),
                  pl.BlockSpec((qb, span), lambda b, n: (0, 0)),
                  pl.BlockSpec((qb, span), lambda b, n: (0, 0)),
                  pl.BlockSpec(memory_space=pltpu.SMEM),
                  pl.BlockSpec(memory_space=pltpu.SMEM)],
        out_specs=pl.BlockSpec((qb, B_WIDTH), lambda b, n: (b * nb + n, 0)),
        out_shape=jax.ShapeDtypeStruct((m, B_WIDTH), F32),
        scratch_shapes=[pltpu.VMEM((B_HEADS, qb, span), F32)],
        compiler_params=_cparams(("arbitrary", "arbitrary")),
        name="swa",
    )(proj, kp_arr, proj, vp_arr, proj, jnp.asarray(bucket), jnp.asarray(band), rel_bias, sinks)


def _rmsnorm_kernel(x_ref, g_ref, o_ref):
    o_ref[...] = _rms(x_ref[...], g_ref[...])


def rmsnorm_rows(x, g):
    m, d = x.shape
    tm = _pick_tile(m, (512, 256, 128, 64, 32, 16, 8))
    return pl.pallas_call(
        _rmsnorm_kernel,
        grid=(m // tm,),
        in_specs=[pl.BlockSpec((tm, d), lambda i: (i, 0)), pl.BlockSpec((1, d), lambda i: (0, 0))],
        out_specs=pl.BlockSpec((tm, d), lambda i: (i, 0)),
        out_shape=jax.ShapeDtypeStruct((m, d), F32),
        compiler_params=_cparams(("parallel",)),
        name="rmsnorm",
    )(x, g.reshape(1, d))


def _rwkv_proj_kernel(h_ref, hp_ref, mu_ref, w_ref, o_ref, xm_ref):
    @pl.when(pl.program_id(2) == 0)
    def _():
        h = h_ref[...]
        xm_ref[...] = (h + (hp_ref[...] - h) * mu_ref[0]).astype(BF16)

    o_ref[0] = jnp.dot(xm_ref[...], w_ref[...], preferred_element_type=F32)


def rwkv_proj(h, hp, mu3, w3_bf16, li, tn=2048):
    m, d = h.shape
    tm = _pick_tile(m, (512, 256, 128, 64, 32, 16, 8))
    return pl.pallas_call(
        _rwkv_proj_kernel,
        grid=(m // tm, 3, d // tn),
        in_specs=[pl.BlockSpec((tm, d), lambda i, p, j: (i, 0)),
                  pl.BlockSpec((tm, d), lambda i, p, j: (i, 0)),
                  pl.BlockSpec((1, 1, d), lambda i, p, j: (p, 0, 0)),
                  pl.BlockSpec((None, None, d, tn), lambda i, p, j: (li, p, 0, j))],
        out_specs=pl.BlockSpec((1, tm, tn), lambda i, p, j: (p, i, j)),
        out_shape=jax.ShapeDtypeStruct((3, m, d), F32),
        scratch_shapes=[pltpu.VMEM((tm, d), BF16)],
        compiler_params=_cparams(("parallel", "arbitrary", "arbitrary")),
        name="rwkv_proj",
    )(h, hp, mu3, w3_bf16)


def _softplus(z):
    return jnp.maximum(z, 0.0) + jnp.log(1.0 + jnp.exp(-jnp.abs(z)))


def _rwkv_lora_kernel(*refs, has_vres):
    if has_vres:
        (h_ref, hp_ref, mu_ref, w0_ref, w1_ref, w2_ref, a0_ref, a1_ref, a2_ref, g1_ref, g2_ref,
         v0_ref, v1_ref, v2_ref, wl_ref, a_ref, g_ref, vg_ref) = refs
    else:
        (h_ref, hp_ref, mu_ref, w0_ref, w1_ref, w2_ref, a0_ref, a1_ref, a2_ref, g1_ref, g2_ref,
         wl_ref, a_ref, g_ref) = refs
    h = h_ref[...]
    xx = hp_ref[...] - h
    mix = lambda i: (h + xx * mu_ref[i:i + 1, :]).astype(BF16)
    xw, xa, xg = mix(0), mix(1), mix(2)
    wpre = w0_ref[...] + _dot(jnp.tanh(jnp.dot(xw, w1_ref[...], preferred_element_type=F32)), w2_ref[...])
    w_log = -_softplus(-wpre) - 0.5
    wl_ref[...] = -jnp.exp(w_log)
    apre = a0_ref[...] + _dot(jnp.dot(xa, a1_ref[...], preferred_element_type=F32), a2_ref[...])
    a_ref[...] = jax.nn.sigmoid(apre)
    g_ref[...] = _dot(jax.nn.sigmoid(jnp.dot(xg, g1_ref[...], preferred_element_type=F32)), g2_ref[...])
    if has_vres:
        xv = mix(3)
        vpre = v0_ref[...] + _dot(jnp.dot(xv, v1_ref[...], preferred_element_type=F32), v2_ref[...])
        vg_ref[...] = jax.nn.sigmoid(vpre)


def _pad_lora(w1, w2):
    r = w1.shape[1]
    rp = -(-r // LANES) * LANES
    return (jnp.pad(w1, ((0, 0), (0, rp - r))).astype(BF16),
            jnp.pad(w2, ((0, rp - r), (0, 0))).astype(BF16))


def rwkv_lora(h, hp, mu_wagv, w0, w1, w2, a0, a1, a2, g1, g2, vres=None):
    m, d = h.shape
    tm = _pick_tile(m, (256, 128, 64, 32, 16, 8))
    has_vres = vres is not None
    row = lambda i: (i, 0)
    const = lambda i: (0, 0)
    w1p, w2p = _pad_lora(w1, w2)
    a1p, a2p = _pad_lora(a1, a2)
    g1p, g2p = _pad_lora(g1, g2)
    args = [h, hp, mu_wagv, w0.reshape(1, d), w1p, w2p, a0.reshape(1, d), a1p, a2p, g1p, g2p]
    if has_vres:
        v1p, v2p = _pad_lora(vres[1], vres[2])
        args += [vres[0].reshape(1, d), v1p, v2p]
    in_specs = [pl.BlockSpec((tm, d), row), pl.BlockSpec((tm, d), row)]
    in_specs += [pl.BlockSpec(a.shape, const) for a in args[2:]]
    n_out = 4 if has_vres else 3
    return pl.pallas_call(
        functools.partial(_rwkv_lora_kernel, has_vres=has_vres),
        grid=(m // tm,),
        in_specs=in_specs,
        out_specs=[pl.BlockSpec((tm, d), row)] * n_out,
        out_shape=[jax.ShapeDtypeStruct((m, d), F32)] * n_out,
        compiler_params=_cparams(("parallel",)),
        name="rwkv_lora",
    )(*args)


def _rwkv_consts(c):
    i = np.arange(2 * c)
    same = (i[:, None] // c) == (i[None, :] // c)
    strict = (same & ((i[:, None] % c) > (i[None, :] % c))).astype(np.float32)
    incl = (same & ((i[:, None] % c) >= (i[None, :] % c))).astype(np.float32)
    eye = np.eye(2 * c, dtype=np.float32)
    tril = np.tril(np.ones((c, c), np.float32))
    l = np.arange(LANES)
    headones = ((l[:, None] // C_HEAD) == (l[None, :] // C_HEAD)).astype(np.float32)
    return strict, incl, eye, tril, headones


def _dot2_ones(x, ones_bf16):
    hi = x.astype(BF16)
    lo = (x - hi.astype(F32)).astype(BF16)
    return (jnp.dot(hi, ones_bf16, preferred_element_type=F32)
            + jnp.dot(lo, ones_bf16, preferred_element_type=F32))


def _rwkv_core_kernel(*refs, chunk, nchunks, npairs, has_state, has_vres):
    it = iter(refs)
    r_ref, k_ref, v_ref, wl_ref, a_ref, g_ref = (next(it) for _ in range(6))
    if has_vres:
        vf_ref, vg_ref = next(it), next(it)
    kk_ref, ka_ref, rk_ref, lg_ref, lb_ref = (next(it) for _ in range(5))
    strict_ref, incl_ref, eye_ref, tril_ref, hones_ref = (next(it) for _ in range(5))
    if has_state:
        s0_ref = next(it)
    y_ref, s_ref, st_ref = next(it), next(it), next(it)
    c = chunk
    l_idx = pl.program_id(2)
    lane = lax.broadcasted_iota(jnp.int32, (1, LANES), 1)
    m0 = (lane < C_HEAD).astype(F32)
    m1 = 1.0 - m0

    @pl.when(l_idx == 0)
    def _():
        for pi in range(npairs):
            if has_state:
                z = jnp.zeros((C_HEAD, C_HEAD), F32)
                top = jnp.concatenate([s0_ref[0, 2 * pi], z], axis=1)
                bot = jnp.concatenate([z, s0_ref[0, 2 * pi + 1]], axis=1)
                st_ref[pi] = jnp.concatenate([top, bot], axis=0)
            else:
                st_ref[pi] = jnp.zeros((LANES, LANES), F32)

    hones = hones_ref[...].astype(BF16)
    strict = strict_ref[...]
    incl = incl_ref[...]
    eye = eye_ref[...]
    tril = tril_ref[...]

    def stack(x):
        return jnp.concatenate([x * m0, x * m1], axis=0)

    def rowsums(xs):
        tot = _dot2_ones(jnp.concatenate(xs, axis=0), hones)
        return [tot[i * c:(i + 1) * c] for i in range(len(xs))]

    def load(pi, rows):
        cols = slice(pi * LANES, (pi + 1) * LANES)
        k = k_ref[0, rows, cols]
        v = v_ref[0, rows, cols]
        a = a_ref[rows, cols]
        if has_vres:
            v = v + (vf_ref[0, rows, cols] - v) * vg_ref[rows, cols]
        return dict(cols=cols, r=r_ref[0, rows, cols], v=v, a=a, wl=wl_ref[rows, cols],
                    kr=k * kk_ref[:, cols], kh=k * (1.0 + (a - 1.0) * ka_ref[:, cols]))

    def decays(p, ss):
        kk = p["kr"] * lax.rsqrt(jnp.maximum(ss, 1e-24))
        b = kk * p["a"]
        gc = _dot_exact01(tril, p["wl"])
        gl = gc[c - 1:c, :]
        e_neg = jnp.exp(-gc)
        e_out = jnp.exp(gl - gc)
        p.update(gl=gl, ab=-kk * jnp.exp(gc - p["wl"]), rb=p["r"] * jnp.exp(gc),
                 bt=b * e_neg, kt=p["kh"] * e_neg, bh=b * e_out, khat=p["kh"] * e_out)

    def intra(p):
        lhs = jnp.concatenate([stack(p["ab"]), stack(p["rb"])], axis=0)
        with_b = _dot_nt(lhs, jnp.concatenate([p["bt"], p["bt"]], axis=0))
        with_k = _dot_nt(lhs, jnp.concatenate([p["kt"], p["kt"]], axis=0))
        p.update(a_ab=with_b[:2 * c] * strict, a_rb=with_b[2 * c:] * incl,
                 a_ak=with_k[:2 * c] * strict, a_rk=with_k[2 * c:] * incl)

    def body(ci, carry):
        rows = pl.ds(pl.multiple_of(ci * c, c), c)
        ps = [load(pi, rows) for pi in range(npairs)]
        for p, ss in zip(ps, rowsums([p["kr"] * p["kr"] for p in ps])):
            decays(p, ss)
        for p in ps:
            intra(p)

        ts = [eye + p["a_ab"] for p in ps]
        pws = [p["a_ab"] for p in ps]
        if c > 2:
            pws = [_dot3_shared([pw], pw)[0] for pw in pws]
            n = 2
            while 2 * n < c:
                res = [_dot3_shared([t, pw], pw) for t, pw in zip(ts, pws)]
                ts = [t + r[0] for t, r in zip(ts, res)]
                pws = [r[1] for r in res]
                n *= 2
            ts = [t + _dot3_shared([t], pw)[0] for t, pw in zip(ts, pws)]

        sts = [st_ref[pi] for pi in range(npairs)]
        fss = [_dot_nt(jnp.concatenate([p["ab"], p["rb"]], axis=0), st) for p, st in zip(ps, sts)]
        vss = [stack(p["v"]) for p in ps]
        rhss = [stack(fs[:c]) + _dot(p["a_ak"], vs) for p, fs, vs in zip(ps, fss, vss)]
        uss = [_dot(t, rhs) for t, rhs in zip(ts, rhss)]
        yss = [_dot(p["a_rb"], us) + _dot(p["a_rk"], vs) for p, us, vs in zip(ps, uss, vss)]
        for pi, (p, st, us) in enumerate(zip(ps, sts, uss)):
            u = us[:c] + us[c:]
            upd = _dot_tn(jnp.concatenate([u, p["v"]], axis=0),
                          jnp.concatenate([p["bh"], p["khat"]], axis=0))
            st_ref[pi] = st * jnp.exp(p["gl"]) + upd * hones_ref[...]
        inv_n = 1.0 / C_HEAD
        ys_ = [fs[c:] + ys[:c] + ys[c:] for fs, ys in zip(fss, yss)]
        sums = rowsums(ys_ + [p["r"] * p["kh"] * rk_ref[:, p["cols"]] for p in ps])
        dlts = [y - m * inv_n for y, m in zip(ys_, sums[:npairs])]
        vars_ = rowsums([d * d for d in dlts])
        for p, dlt, var, bsum in zip(ps, dlts, vars_, sums[npairs:]):
            cols = p["cols"]
            yn = dlt * lax.rsqrt(var * inv_n + GN_EPS) * lg_ref[:, cols] + lb_ref[:, cols]
            y_ref[rows, cols] = ((yn + bsum * p["v"]) * g_ref[rows, cols]).astype(y_ref.dtype)
        return carry

    lax.fori_loop(0, nchunks, body, 0)

    @pl.when(l_idx == pl.num_programs(2) - 1)
    def _():
        for pi in range(npairs):
            st = st_ref[pi]
            s_ref[0, 2 * pi] = st[:C_HEAD, :C_HEAD]
            s_ref[0, 2 * pi + 1] = st[C_HEAD:, C_HEAD:]


def rwkv_core(rkv, wl, a, g, kk_p, ka_p, rk_p, lnx_g, lnx_b, batch, seqlen, layer,
              s0=None, v_first=None, vgate=None, npairs=8):
    _, m, d = rkv.shape
    c = math.gcd(seqlen, RWKV_CHUNK)
    lblk = _pick_tile(seqlen, (256, 128, 64, 32, 16, 8))
    nl_blocks = seqlen // lblk
    has_state = s0 is not None
    has_vres = v_first is not None
    w = npairs * LANES
    ngroups = d // w
    strict, incl, eye, tril, hones = (jnp.asarray(x) for x in _rwkv_consts(c))
    row = lambda b, p, l: b * nl_blocks + l
    blk3 = lambda which: pl.BlockSpec((1, lblk, w), lambda b, p, l: (which, row(b, p, l), p))
    blk2 = pl.BlockSpec((lblk, w), lambda b, p, l: (row(b, p, l), p))
    par = pl.BlockSpec((1, w), lambda b, p, l: (0, p))
    full = lambda arr: pl.BlockSpec(arr.shape, lambda b, p, l: (0,) * arr.ndim)
    in_specs = [blk3(0), blk3(1), blk3(2), blk2, blk2, blk2]
    args = [rkv, rkv, rkv, wl, a, g]
    if has_vres:
        in_specs += [blk3(2), blk2]
        args += [v_first, vgate]
    in_specs += [par] * 5
    args += [x.reshape(1, d) for x in (kk_p, ka_p, rk_p, lnx_g, lnx_b)]
    in_specs += [full(x) for x in (strict, incl, eye, tril, hones)]
    args += [strict, incl, eye, tril, hones]
    st_blk = pl.BlockSpec((1, 2 * npairs, C_HEAD, C_HEAD), lambda b, p, l: (b, p, 0, 0))
    if has_state:
        in_specs.append(pl.BlockSpec((None, 1, 2 * npairs, C_HEAD, C_HEAD),
                                     lambda b, p, l: (layer, b, p, 0, 0)))
        args.append(s0)
    kern = functools.partial(_rwkv_core_kernel, chunk=c, nchunks=lblk // c, npairs=npairs,
                             has_state=has_state, has_vres=has_vres)
    return pl.pallas_call(
        kern,
        grid=(batch, ngroups, nl_blocks),
        in_specs=in_specs,
        out_specs=[blk2, st_blk],
        out_shape=[jax.ShapeDtypeStruct((m, d), F32),
                   jax.ShapeDtypeStruct((batch, C_HEADS, C_HEAD, C_HEAD), F32)],
        scratch_shapes=[pltpu.VMEM((npairs, LANES, LANES), F32)],
        compiler_params=_cparams(("parallel", "parallel", "arbitrary")),
        name="rwkv_core",
    )(*args)


def _even_layer(x, batch, seqlen, e, layer, p, st_hgrn, k_cache, v_cache):
    proj = norm_matmul(x, p["norm_mix_pre"][layer], p["w_in_even"], e)
    o_a, s_new = hgrn(proj, p["hgrn_lb_raw"], e, batch, seqlen, st_hgrn)
    k_lo = IN_A + B_WIDTH
    new_rows = min(seqlen, WINDOW)
    p3 = proj.reshape(batch, seqlen, IN_EVEN)[:, seqlen - new_rows:]
    kb = p3[:, :, k_lo:k_lo + B_KV_WIDTH].reshape(batch, new_rows, B_KV_HEADS, B_HEAD_DIM)
    vb = p3[:, :, k_lo + B_KV_WIDTH:].reshape(batch, new_rows, B_KV_HEADS, B_HEAD_DIM)
    o_b = swa(proj, batch, seqlen, p["rel_bias"], p["attn_sinks"][e], e, k_cache, v_cache)
    if k_cache is None:
        k_new, v_new = kb, vb
    else:
        k_new = jnp.concatenate([k_cache[e, :, new_rows:], kb], axis=1)
        v_new = jnp.concatenate([v_cache[e, :, new_rows:], vb], axis=1)
    x = even_out(o_a, proj, o_b, x, p["hgrn_norm_g"][e], p["w_out_even"], p["norm_mix_post"][layer], e)
    return x, s_new, k_new, v_new


def _odd_layer(x, batch, seqlen, o, layer, p, shift0, s0, v_first):
    m, d = x.shape
    h = rmsnorm_rows(x, p["norm_mix_pre"][layer])
    h3 = h.reshape(batch, seqlen, d)
    first = jnp.zeros((batch, 1, d), F32) if shift0 is None else shift0[:, None, :]
    hp = jnp.concatenate([first, h3[:, :-1]], axis=1).reshape(m, d)
    mu = p["rw_mu"][o]
    mu_rkv = jnp.stack([mu[0], mu[2], mu[3]])[:, None, :]
    mu_wagv = jnp.stack([mu[1], mu[4], mu[5], mu[3]])
    rkv = rwkv_proj(h, hp, mu_rkv, p["w_rkv"], o)
    vres = None if o == 0 else (p["rw_v0"][o - 1], p["rw_v1"][o - 1], p["rw_v2"][o - 1])
    outs = rwkv_lora(h, hp, mu_wagv, p["rw_w0"][o], p["rw_w1"][o], p["rw_w2"][o],
                     p["rw_a0"][o], p["rw_a1"][o], p["rw_a2"][o], p["rw_g1"][o], p["rw_g2"][o], vres)
    wl, a, g = outs[:3]
    vgate = outs[3] if o > 0 else None
    yg, s_new = rwkv_core(rkv, wl, a, g, p["rw_kk"][o], p["rw_ka"][o], p["rw_rk"][o],
                          p["rw_lnx_g"][o], p["rw_lnx_b"][o], batch, seqlen, o, s0,
                          v_first if o > 0 else None, vgate)
    x = odd_out(yg, x, p["rw_wo"], p["norm_mix_post"][layer], o)
    return x, s_new, h3[:, -1], rkv


def _trunk(x3, st_hgrn, k_cache, v_cache, st_rwkv, st_shift, p):
    batch, seqlen, d = x3.shape
    x = x3.reshape(batch * seqlen, d)
    has_state = st_hgrn is not None
    hgrn_out, k_out, v_out, rwkv_out, shift_out = [], [], [], [], []
    v_first = None
    for layer in range(DEPTH):
        if layer % 2 == 0:
            e = layer // 2
            x, s_new, k_new, v_new = _even_layer(x, batch, seqlen, e, layer, p, st_hgrn, k_cache, v_cache)
            hgrn_out.append(s_new)
            k_out.append(k_new)
            v_out.append(v_new)
        else:
            o = layer // 2
            x, s_new, sh_new, rkv = _odd_layer(
                x, batch, seqlen, o, layer, p,
                st_shift[o] if has_state else None,
                st_rwkv,
                v_first)
            if o == 0:
                v_first = rkv
            rwkv_out.append(s_new)
            shift_out.append(sh_new)
        x = ffn(x, p["norm_ffn_pre"][layer], p["w_up"], p["w_down"], p["norm_ffn_post"][layer], layer)
    return (x.reshape(batch, seqlen, d), jnp.stack(hgrn_out), jnp.stack(k_out), jnp.stack(v_out),
            jnp.stack(rwkv_out), jnp.stack(shift_out))


def kernel(x_prompt, x_sample, state_hgrn, cache_swa_k, cache_swa_v, state_rwkv, state_shift,
           norm_mix_pre, norm_mix_post, norm_ffn_pre, norm_ffn_post,
           w_in_even, w_out_even, hgrn_lb_raw, hgrn_norm_g, rel_bias, attn_sinks,
           rw_mu, rw_wr, rw_wk, rw_wv, rw_wo, rw_w0, rw_w1, rw_w2, rw_a0, rw_a1, rw_a2,
           rw_v0, rw_v1, rw_v2, rw_g1, rw_g2, rw_kk, rw_ka, rw_rk, rw_lnx_g, rw_lnx_b,
           w_up, w_down):
    p = {
        "norm_mix_pre": norm_mix_pre, "norm_mix_post": norm_mix_post,
        "norm_ffn_pre": norm_ffn_pre, "norm_ffn_post": norm_ffn_post,
        "w_in_even": w_in_even.astype(BF16), "w_out_even": w_out_even.astype(BF16),
        "hgrn_lb_raw": hgrn_lb_raw, "hgrn_norm_g": hgrn_norm_g,
        "rel_bias": rel_bias, "attn_sinks": attn_sinks,
        "rw_mu": rw_mu, "w_rkv": jnp.stack([rw_wr, rw_wk, rw_wv], axis=1).astype(BF16),
        "rw_wo": rw_wo.astype(BF16),
        "rw_w0": rw_w0, "rw_w1": rw_w1, "rw_w2": rw_w2, "rw_a0": rw_a0, "rw_a1": rw_a1, "rw_a2": rw_a2,
        "rw_v0": rw_v0, "rw_v1": rw_v1, "rw_v2": rw_v2, "rw_g1": rw_g1, "rw_g2": rw_g2,
        "rw_kk": rw_kk, "rw_ka": rw_ka, "rw_rk": rw_rk, "rw_lnx_g": rw_lnx_g, "rw_lnx_b": rw_lnx_b,
        "w_up": w_up.astype(BF16), "w_down": w_down.astype(BF16),
    }
    y_p, hgrn_p, k_p, v_p, rwkv_p, shift_p = _trunk(x_prompt, None, None, None, None, None, p)
    y_s, hgrn_s, k_s, v_s, rwkv_s, shift_s = _trunk(
        x_sample, state_hgrn, cache_swa_k, cache_swa_v, state_rwkv, state_shift, p)
    return (y_p, y_s, hgrn_p, hgrn_s, k_p, k_s, v_p, v_s, rwkv_p, rwkv_s, shift_p, shift_s)
```

```python
import functools
import math

import numpy as np
import jax
import jax.numpy as jnp
from jax import lax
from jax.experimental import pallas as pl
from jax.experimental.pallas import tpu as pltpu

F32 = jnp.float32
BF16 = jnp.bfloat16

D_MODEL = 2048
DEPTH = 4
N_EVEN = 2
N_ODD = 2
A_HEADS = 8
A_KDIM = 128
A_VDIM = 128
A_WIDTH = 1024
A_QK = 1024
B_HEADS = 16
B_HEAD_DIM = 64
B_KV_HEADS = 4
B_GROUP = 4
B_WIDTH = 1024
B_KV_WIDTH = 256
WINDOW = 128
N_BUCKETS = 32
MAX_DISTANCE = 128
MASK_VALUE = -1e30
IN_A = 4096
IN_EVEN = 5632
C_HEAD = 64
C_HEADS = 32
GN_EPS = 64e-5
D_FF = 8192
NORM_EPS = 1e-6

LANES = 128
VMEM_LIMIT = 56 * 1024 * 1024

HGRN_CHUNK = 64
RWKV_CHUNK = 64


def _cparams(sem):
    return pltpu.CompilerParams(dimension_semantics=sem, vmem_limit_bytes=VMEM_LIMIT)


def _rms(x, g):
    return x * lax.rsqrt(jnp.mean(x * x, axis=-1, keepdims=True) + NORM_EPS) * g


def _dot(a, b):
    return jnp.dot(a.astype(BF16), b.astype(BF16), preferred_element_type=F32)


def _dot_nt(a, b):
    return lax.dot_general(a.astype(BF16), b.astype(BF16), (((1,), (1,)), ((), ())),
                           preferred_element_type=F32)


def _dot_tn(a, b):
    return lax.dot_general(a.astype(BF16), b.astype(BF16), (((0,), (0,)), ((), ())),
                           preferred_element_type=F32)


def _split2(x):
    hi = x.astype(BF16)
    return hi, (x - hi.astype(F32)).astype(BF16)


def _dot3_shared(lhs_list, b):
    bh, bl = _split2(b)
    parts = [_split2(a) for a in lhs_list]
    his = [p[0] for p in parts]
    los = [p[1] for p in parts]
    by_hi = jnp.dot(jnp.concatenate(his + los, axis=0), bh, preferred_element_type=F32)
    by_lo = jnp.dot(jnp.concatenate(his, axis=0), bl, preferred_element_type=F32)
    n = sum(a.shape[0] for a in lhs_list)
    out, off = [], 0
    for a in lhs_list:
        m = a.shape[0]
        out.append(by_hi[off:off + m] + by_hi[n + off:n + off + m] + by_lo[off:off + m])
        off += m
    return out


def _dot_exact01(sel, x):
    hi = x.astype(BF16)
    r1 = x - hi.astype(F32)
    mid = r1.astype(BF16)
    lo = (r1 - mid.astype(F32)).astype(BF16)
    s = sel.astype(BF16)
    return (jnp.dot(s, hi, preferred_element_type=F32)
            + jnp.dot(s, mid, preferred_element_type=F32)
            + jnp.dot(s, lo, preferred_element_type=F32))


def _pick_tile(m, cands):
    for c in cands:
        if m % c == 0:
            return c
    return m


def _norm_matmul_kernel(x_ref, g_ref, w_ref, o_ref, xn_ref):
    @pl.when(pl.program_id(1) == 0)
    def _():
        xn_ref[...] = _rms(x_ref[...], g_ref[...]).astype(BF16)

    o_ref[...] = jnp.dot(xn_ref[...], w_ref[...], preferred_element_type=F32)


def norm_matmul(x, g, w_stack_bf16, li, tn=1408):
    m, d = x.shape
    n = w_stack_bf16.shape[2]
    tm = _pick_tile(m, (512, 256, 128, 64, 32, 16, 8))
    return pl.pallas_call(
        _norm_matmul_kernel,
        grid=(m // tm, n // tn),
        in_specs=[pl.BlockSpec((tm, d), lambda i, j: (i, 0)),
                  pl.BlockSpec((1, d), lambda i, j: (0, 0)),
                  pl.BlockSpec((None, d, tn), lambda i, j: (li, 0, j))],
        out_specs=pl.BlockSpec((tm, tn), lambda i, j: (i, j)),
        out_shape=jax.ShapeDtypeStruct((m, n), F32),
        scratch_shapes=[pltpu.VMEM((tm, d), BF16)],
        compiler_params=_cparams(("parallel", "arbitrary")),
        name="norm_matmul",
    )(x, g.reshape(1, d), w_stack_bf16)


def _ffn_kernel(x_ref, gpre_ref, wup_ref, wdn_ref, gpost_ref, o_ref, xn_ref, acc_ref):
    f = pl.program_id(1)

    @pl.when(f == 0)
    def _():
        xn_ref[...] = _rms(x_ref[...], gpre_ref[...]).astype(BF16)
        acc_ref[...] = jnp.zeros_like(acc_ref)

    h = jnp.dot(xn_ref[...], wup_ref[...], preferred_element_type=F32)
    h = jnp.square(jnp.maximum(h, 0.0)).astype(BF16)
    acc_ref[...] += jnp.dot(h, wdn_ref[...], preferred_element_type=F32)

    @pl.when(f == pl.num_programs(1) - 1)
    def _():
        o_ref[...] = x_ref[...] + _rms(acc_ref[...], gpost_ref[...])


def ffn(x, gpre, wup_bf16, wdn_bf16, gpost, li, tf=1024):
    m, d = x.shape
    dff = wup_bf16.shape[2]
    tm = _pick_tile(m, (512, 256, 128, 64, 32, 16, 8))
    return pl.pallas_call(
        _ffn_kernel,
        grid=(m // tm, dff // tf),
        in_specs=[pl.BlockSpec((tm, d), lambda i, f: (i, 0)),
                  pl.BlockSpec((1, d), lambda i, f: (0, 0)),
                  pl.BlockSpec((None, d, tf), lambda i, f: (li, 0, f)),
                  pl.BlockSpec((None, tf, d), lambda i, f: (li, f, 0)),
                  pl.BlockSpec((1, d), lambda i, f: (0, 0))],
        out_specs=pl.BlockSpec((tm, d), lambda i, f: (i, 0)),
        out_shape=jax.ShapeDtypeStruct((m, d), F32),
        scratch_shapes=[pltpu.VMEM((tm, d), BF16), pltpu.VMEM((tm, d), F32)],
        compiler_params=_cparams(("parallel", "arbitrary")),
        name="ffn",
    )(x, gpre.reshape(1, d), wup_bf16, wdn_bf16, gpost.reshape(1, d))


def _even_out_kernel(oa_ref, ga_ref, ob_ref, x_ref, ag_ref, w_ref, gpost_ref, o_ref):
    ga = ga_ref[...]
    oan = _rms(oa_ref[...], ag_ref[...]) * (ga * jax.nn.sigmoid(ga))
    mix = (jnp.dot(oan.astype(BF16), w_ref[:A_WIDTH, :], preferred_element_type=F32)
           + jnp.dot(ob_ref[...].astype(BF16), w_ref[A_WIDTH:, :], preferred_element_type=F32))
    o_ref[...] = x_ref[...] + _rms(mix, gpost_ref[...])


def even_out(o_a, proj, o_b, x, a_norm_g, w_out_bf16, gpost, li):
    m, d = x.shape
    tm = _pick_tile(m, (384, 256, 128, 64, 32, 16, 8))
    ga_blk = (3 * A_WIDTH) // A_WIDTH
    return pl.pallas_call(
        _even_out_kernel,
        grid=(m // tm,),
        in_specs=[pl.BlockSpec((tm, A_WIDTH), lambda i: (i, 0)),
                  pl.BlockSpec((tm, A_WIDTH), lambda i: (i, ga_blk)),
                  pl.BlockSpec((tm, B_WIDTH), lambda i: (i, 0)),
                  pl.BlockSpec((tm, d), lambda i: (i, 0)),
                  pl.BlockSpec((1, A_WIDTH), lambda i: (0, 0)),
                  pl.BlockSpec((None, A_WIDTH + B_WIDTH, d), lambda i: (li, 0, 0)),
                  pl.BlockSpec((1, d), lambda i: (0, 0))],
        out_specs=pl.BlockSpec((tm, d), lambda i: (i, 0)),
        out_shape=jax.ShapeDtypeStruct((m, d), F32),
        compiler_params=_cparams(("parallel",)),
        name="even_out",
    )(o_a, proj, o_b, x, a_norm_g.reshape(1, A_WIDTH), w_out_bf16, gpost.reshape(1, d))


def _odd_out_kernel(y_ref, x_ref, w_ref, gpost_ref, o_ref):
    mix = jnp.dot(y_ref[...].astype(BF16), w_ref[...], preferred_element_type=F32)
    o_ref[...] = x_ref[...] + _rms(mix, gpost_ref[...])


def odd_out(yg, x, wo_bf16, gpost, li):
    m, d = x.shape
    tm = _pick_tile(m, (384, 256, 128, 64, 32, 16, 8))
    return pl.pallas_call(
        _odd_out_kernel,
        grid=(m // tm,),
        in_specs=[pl.BlockSpec((tm, d), lambda i: (i, 0)),
                  pl.BlockSpec((tm, d), lambda i: (i, 0)),
                  pl.BlockSpec((None, d, d), lambda i: (li, 0, 0)),
                  pl.BlockSpec((1, d), lambda i: (0, 0))],
        out_specs=pl.BlockSpec((tm, d), lambda i: (i, 0)),
        out_shape=jax.ShapeDtypeStruct((m, d), F32),
        compiler_params=_cparams(("parallel",)),
        name="odd_out",
    )(yg, x, wo_bf16, gpost.reshape(1, d))


def _level_consts(c):
    levels = []
    s = c // 2
    while s >= 1:
        levels.append(s)
        s //= 2
    nl = len(levels)
    sel = np.zeros((nl, c, c), np.float32)
    mask = np.zeros((nl, c, c), np.float32)
    idx = np.arange(c)
    for l, s in enumerate(levels):
        ref_row = (idx // (2 * s)) * (2 * s) + s - 1
        sel[l, idx, ref_row] = 1.0
        same = (idx[:, None] // (2 * s)) == (idx[None, :] // (2 * s))
        upper = (idx[:, None] % (2 * s)) >= s
        lower = (idx[None, :] % (2 * s)) < s
        mask[l] = (same & upper & lower).astype(np.float32)
    tril = np.tril(np.ones((c, c), np.float32))
    return nl, sel.reshape(nl * c, c), mask, tril


def _hgrn_kernel(*refs, layer, chunk, nchunks, nlevels, has_state, nheads):
    if has_state:
        (q_ref, f_ref, i_ref, lb_ref, sel_ref, mask_ref, tril_ref, s0_ref,
         o_ref, s_ref, st_ref) = refs
    else:
        (q_ref, f_ref, i_ref, lb_ref, sel_ref, mask_ref, tril_ref,
         o_ref, s_ref, st_ref) = refs
    c = chunk
    nh = nheads
    l_idx = pl.program_id(2)

    @pl.when(l_idx == 0)
    def _():
        for hi in range(nh):
            if has_state:
                st_ref[hi] = s0_ref[0, hi].T
            else:
                st_ref[hi] = jnp.zeros((A_VDIM, A_KDIM), F32)

    lbr = lb_ref[...]
    e = jnp.exp(lbr - jnp.max(lbr, axis=0, keepdims=True))
    p = e / jnp.sum(e, axis=0, keepdims=True)
    lb = jnp.zeros((1, nh * LANES), F32)
    for i in range(1, layer + 1):
        lb = lb + p[i:i + 1, :]
    one_m_lb = 1.0 - lb
    head = lambda x, hi: x[:, hi * LANES:(hi + 1) * LANES]

    def body(ci, carry):
        rows = pl.ds(pl.multiple_of(ci * c, c), c)
        fq = f_ref[rows, :]
        qr = q_ref[rows, :]
        v = i_ref[rows, :]
        logf = jnp.log(lb + one_m_lb * jax.nn.sigmoid(fq))
        k = one_m_lb * jax.nn.sigmoid(-fq)
        q = qr * jax.nn.sigmoid(qr) * (A_KDIM ** -0.5)

        g = _dot_exact01(tril_ref[...], logf)
        refs_g = _dot_exact01(sel_ref[...], g)
        glast = g[c - 1:c, :]
        q_in = q * jnp.exp(g)
        kd = k * jnp.exp(glast - g)
        dec = jnp.exp(glast)
        diag = q * k

        sts = [st_ref[hi] for hi in range(nh)]
        os_ = [_dot_nt(head(q_in, hi), sts[hi]) for hi in range(nh)]
        attns = [jnp.zeros((c, c), F32) for _ in range(nh)]
        for l in range(nlevels):
            gr = refs_g[l * c:(l + 1) * c, :]
            qs = q * jnp.exp(jnp.minimum(g - gr, 0.0))
            ks = k * jnp.exp(jnp.minimum(gr - g, 0.0))
            ml = mask_ref[l]
            attns = [at + ml * _dot_nt(head(qs, hi), head(ks, hi)) for hi, at in enumerate(attns)]
        for hi in range(nh):
            vh = head(v, hi)
            o = os_[hi] + _dot(attns[hi], vh) + jnp.sum(head(diag, hi), axis=-1, keepdims=True) * vh
            o_ref[rows, hi * LANES:(hi + 1) * LANES] = o
        for hi in range(nh):
            st_ref[hi] = sts[hi] * head(dec, hi) + _dot_tn(head(v, hi), head(kd, hi))
        return carry

    lax.fori_loop(0, nchunks, body, 0, unroll=2 if nchunks % 2 == 0 else 1)

    @pl.when(l_idx == pl.num_programs(2) - 1)
    def _():
        for hi in range(nh):
            s_ref[0, hi] = st_ref[hi].T


def hgrn(proj, lb_raw, layer, batch, seqlen, s0=None, nheads=4):
    m = proj.shape[0]
    c = math.gcd(seqlen, HGRN_CHUNK)
    lblk = _pick_tile(seqlen, (512, 256, 128, 64, 32, 16, 8))
    nl_blocks = seqlen // lblk
    nlevels, sel, mask, tril = _level_consts(c)
    has_state = s0 is not None
    w = nheads * LANES
    ngroups = A_HEADS // nheads
    kern = functools.partial(_hgrn_kernel, layer=layer, chunk=c, nchunks=lblk // c,
                             nlevels=nlevels, has_state=has_state, nheads=nheads)
    row = lambda b, h, l: b * nl_blocks + l
    in_specs = [pl.BlockSpec((lblk, w), lambda b, h, l: (row(b, h, l), h)),
                pl.BlockSpec((lblk, w), lambda b, h, l: (row(b, h, l), ngroups + h)),
                pl.BlockSpec((lblk, w), lambda b, h, l: (row(b, h, l), 2 * ngroups + h)),
                pl.BlockSpec((N_EVEN, w), lambda b, h, l: (0, h)),
                pl.BlockSpec((nlevels * c, c), lambda b, h, l: (0, 0)),
                pl.BlockSpec((nlevels, c, c), lambda b, h, l: (0, 0, 0)),
                pl.BlockSpec((c, c), lambda b, h, l: (0, 0))]
    args = [proj, proj, proj, lb_raw, jnp.asarray(sel), jnp.asarray(mask), jnp.asarray(tril)]
    st_blk = pl.BlockSpec((1, nheads, A_KDIM, A_VDIM), lambda b, h, l: (b, h, 0, 0))
    if has_state:
        in_specs.append(pl.BlockSpec((None, 1, nheads, A_KDIM, A_VDIM), lambda b, h, l: (layer, b, h, 0, 0)))
        args.append(s0)
    return pl.pallas_call(
        kern,
        grid=(batch, ngroups, nl_blocks),
        in_specs=in_specs,
        out_specs=[pl.BlockSpec((lblk, w), lambda b, h, l: (row(b, h, l), h)), st_blk],
        out_shape=[jax.ShapeDtypeStruct((m, A_WIDTH), F32),
                   jax.ShapeDtypeStruct((batch, A_HEADS, A_KDIM, A_VDIM), F32)],
        scratch_shapes=[pltpu.VMEM((nheads, A_VDIM, A_KDIM), F32)],
        compiler_params=_cparams(("parallel", "parallel", "arbitrary")),
        name="hgrn",
    )(*args)


def _t5_bucket(dist):
    max_exact = N_BUCKETS // 2
    d = np.maximum(dist, 0)
    large = max_exact + (np.log(np.maximum(d, max_exact).astype(np.float32) / max_exact)
                         / math.log(MAX_DISTANCE / max_exact) * (N_BUCKETS - max_exact)).astype(np.int32)
    large = np.minimum(large, N_BUCKETS - 1)
    return np.where(d < max_exact, d, large).astype(np.int32)


def _swa_kernel(q_ref, kp_ref, kc_ref, vp_ref, vc_ref, bucket_ref, band_ref, rb_ref, sink_ref,
                o_ref, bias_ref, *, qb, span, prev_always_valid):
    first = (pl.program_id(0) == 0) & (pl.program_id(1) == 0)

    @pl.when(first)
    def _():
        bk = bucket_ref[...]
        band = band_ref[...]

        def per_head(h, carry):
            def per_bucket(bi, acc):
                return jnp.where(bk == bi, rb_ref[bi, h], acc)
            acc = lax.fori_loop(0, N_BUCKETS, per_bucket, jnp.zeros((qb, span), F32))
            bias_ref[h] = jnp.where(band > 0, acc, MASK_VALUE)
            return carry

        lax.fori_loop(0, B_HEADS, per_head, 0)

    scale = B_HEAD_DIM ** -0.5
    q = q_ref[...]
    kall = jnp.concatenate([kp_ref[...], kc_ref[...]], axis=0)
    vall = jnp.concatenate([vp_ref[...], vc_ref[...]], axis=0)
    if not prev_always_valid:
        col = lax.broadcasted_iota(jnp.int32, (qb, span), 1)
        no_prev = (col < WINDOW) & (pl.program_id(1) == 0)
    heads = range(B_HEADS)
    ks = [kall[:, kh * B_HEAD_DIM:(kh + 1) * B_HEAD_DIM].astype(BF16) for kh in range(B_KV_HEADS)]
    vs = [vall[:, kh * B_HEAD_DIM:(kh + 1) * B_HEAD_DIM].astype(BF16) for kh in range(B_KV_HEADS)]
    qs = [(q[:, h * B_HEAD_DIM:(h + 1) * B_HEAD_DIM] * scale).astype(BF16) for h in heads]
    ss = [_dot_nt(qs[h], ks[h // B_GROUP]) + bias_ref[h] for h in heads]
    if not prev_always_valid:
        ss = [jnp.where(no_prev, MASK_VALUE, s) for s in ss]
    ms = [jnp.maximum(jnp.max(ss[h], axis=-1, keepdims=True), sink_ref[h]) for h in heads]
    ps = [jnp.exp(s - m) for s, m in zip(ss, ms)]
    denoms = [jnp.sum(ps[h], axis=-1, keepdims=True) + jnp.exp(sink_ref[h] - ms[h]) for h in heads]
    outs = [_dot(ps[h], vs[h // B_GROUP]) / denoms[h] for h in heads]
    o_ref[...] = jnp.concatenate(outs, axis=1)


def swa(proj, batch, seqlen, rel_bias, sinks, layer, k_past=None, v_past=None):
    m = proj.shape[0]
    has_cache = k_past is not None
    qb = math.gcd(seqlen, WINDOW)
    nb = seqlen // qb
    span = WINDOW + qb
    dist = np.arange(qb)[:, None] + WINDOW - np.arange(span)[None, :]
    band = ((dist >= 0) & (dist < WINDOW)).astype(np.float32)
    bucket = _t5_bucket(dist)
    q_col = IN_A // B_WIDTH
    k_col = (IN_A + B_WIDTH) // B_KV_WIDTH
    v_col = k_col + 1
    cur = lambda c: (lambda b, n: (b * nb + n, c))
    if has_cache:
        assert nb == 1
        prev_k = pl.BlockSpec((None, WINDOW, B_KV_WIDTH), lambda b, n: (layer, b, 0))
        prev_v = pl.BlockSpec((None, WINDOW, B_KV_WIDTH), lambda b, n: (layer, b, 0))
        kp_arr = k_past.reshape(k_past.shape[0], batch * WINDOW, B_KV_WIDTH)
        vp_arr = v_past.reshape(v_past.shape[0], batch * WINDOW, B_KV_WIDTH)
    else:
        assert qb == WINDOW
        prev = lambda c: (lambda b, n: (b * nb + jnp.maximum(n - 1, 0), c))
        prev_k = pl.BlockSpec((WINDOW, B_KV_WIDTH), prev(k_col))
        prev_v = pl.BlockSpec((WINDOW, B_KV_WIDTH), prev(v_col))
        kp_arr, vp_arr = proj, proj
    kern = functools.partial(_swa_kernel, qb=qb, span=span, prev_always_valid=has_cache)
    return pl.pallas_call(
        kern,
        grid=(batch, nb),
        in_specs=[pl.BlockSpec((qb, B_WIDTH), cur(q_col)),
                  prev_k,
                  pl.BlockSpec((qb, B_KV_WIDTH), cur(k_col)),
                  prev_v,
                  pl.BlockSpec((qb, B_KV_WIDTH), cur(v_col)),
                  pl.BlockSpec((qb, span), lambda b, n: (0, 0)),
                  pl.BlockSpec((qb, span), lambda b, n: (0, 0)),
                  pl.BlockSpec(memory_space=pltpu.SMEM),
                  pl.BlockSpec(memory_space=pltpu.SMEM)],
        out_specs=pl.BlockSpec((qb, B_WIDTH), lambda b, n: (b * nb + n, 0)),
        out_shape=jax.ShapeDtypeStruct((m, B_WIDTH), F32),
        scratch_shapes=[pltpu.VMEM((B_HEADS, qb, span), F32)],
        compiler_params=_cparams(("arbitrary", "arbitrary")),
        name="swa",
    )(proj, kp_arr, proj, vp_arr, proj, jnp.asarray(bucket), jnp.asarray(band), rel_bias, sinks)


def _rmsnorm_kernel(x_ref, g_ref, o_ref):
    o_ref[...] = _rms(x_ref[...], g_ref[...])


def rmsnorm_rows(x, g):
    m, d = x.shape
    tm = _pick_tile(m, (512, 256, 128, 64, 32, 16, 8))
    return pl.pallas_call(
        _rmsnorm_kernel,
        grid=(m // tm,),
        in_specs=[pl.BlockSpec((tm, d), lambda i: (i, 0)), pl.BlockSpec((1, d), lambda i: (0, 0))],
        out_specs=pl.BlockSpec((tm, d), lambda i: (i, 0)),
        out_shape=jax.ShapeDtypeStruct((m, d), F32),
        compiler_params=_cparams(("parallel",)),
        name="rmsnorm",
    )(x, g.reshape(1, d))


LORA_W = (0, 128)
LORA_A = (128, 256)
LORA_G = (256, 512)
LORA_V = (512, 640)


def _rwkv_in_kernel(x_ref, xp_ref, s_ref, g_ref, mu3_ref, mul_ref, w_ref, w1_ref,
                    rkv_ref, mid_ref, h_s, hp_s, *, tm, seqlen, has_vres):
    ph = pl.program_id(1)

    @pl.when(ph == 0)
    def _():
        g = g_ref[...]
        h = _rms(x_ref[...], g)
        rowid = lax.broadcasted_iota(jnp.int32, h.shape, 0)
        rolled = pltpu.roll(h, 1, 0)
        if seqlen % tm == 0:
            prev_last = _rms(xp_ref[...], g)[7:8, :]
            at_start = pl.program_id(0) % (seqlen // tm) == 0
            first = jnp.where(at_start, s_ref[...], prev_last)
            hp = jnp.where(rowid == 0, first, rolled)
        else:
            hp = jnp.where(rowid % seqlen == 0, s_ref[...], rolled)
        h_s[...] = h
        hp_s[...] = hp

    @pl.when(ph < 3)
    def _():
        h = h_s[...]
        xm = (h + (hp_s[...] - h) * mu3_ref[0]).astype(BF16)
        rkv_ref[0] = jnp.dot(xm, w_ref[...], preferred_element_type=F32)

    @pl.when(ph == 3)
    def _():
        h = h_s[...]
        xx = hp_s[...] - h
        mix = lambda i: (h + xx * mul_ref[i:i + 1, :]).astype(BF16)
        low = lambda i, rng: jnp.dot(mix(i), w1_ref[:, rng[0]:rng[1]], preferred_element_type=F32)
        parts = [jnp.tanh(low(0, LORA_W)), low(1, LORA_A), jax.nn.sigmoid(low(2, LORA_G))]
        if has_vres:
            parts.append(low(3, LORA_V))
        mid_ref[...] = jnp.concatenate(parts, axis=1).astype(BF16)


def rwkv_in(x, g, shift0, batch, seqlen, mu, w3_bf16, li, w1cat_bf16, has_vres):
    m, d = x.shape
    tm = next(t for t in (512, 256, 128, 64, 32, 16, 8)
              if m % t == 0 and (seqlen % t == 0 or t % seqlen == 0))
    midw = w1cat_bf16.shape[1]
    mu_rkv = jnp.stack([mu[0], mu[2], mu[3]])[:, None, :]
    mu_low = jnp.stack([mu[1], mu[4], mu[5], mu[3]])
    if shift0 is None:
        shift0 = jnp.zeros((batch, d), F32)
    if seqlen % tm == 0:
        srow = shift0[:, None, :]
        tps = seqlen // tm
        s_spec = pl.BlockSpec((None, 1, d), lambda i, p: (i // tps, 0, 0))
    else:
        srow = jnp.repeat(shift0, seqlen, axis=0)
        s_spec = pl.BlockSpec((tm, d), lambda i, p: (i, 0))
    sub = tm // 8
    kern = functools.partial(_rwkv_in_kernel, tm=tm, seqlen=seqlen, has_vres=has_vres)
    return pl.pallas_call(
        kern,
        grid=(m // tm, 4),
        in_specs=[pl.BlockSpec((tm, d), lambda i, p: (i, 0)),
                  pl.BlockSpec((8, d), lambda i, p: (jnp.maximum(i * sub - 1, 0), 0)),
                  s_spec,
                  pl.BlockSpec((1, d), lambda i, p: (0, 0)),
                  pl.BlockSpec((1, 1, d), lambda i, p: (jnp.minimum(p, 2), 0, 0)),
                  pl.BlockSpec((4, d), lambda i, p: (0, 0)),
                  pl.BlockSpec((None, None, d, d), lambda i, p: (li, jnp.minimum(p, 2), 0, 0)),
                  pl.BlockSpec((d, midw), lambda i, p: (0, 0))],
        out_specs=[pl.BlockSpec((1, tm, d), lambda i, p: (jnp.minimum(p, 2), i, 0)),
                   pl.BlockSpec((tm, midw), lambda i, p: (i, 0))],
        out_shape=[jax.ShapeDtypeStruct((3, m, d), F32),
                   jax.ShapeDtypeStruct((m, midw), BF16)],
        scratch_shapes=[pltpu.VMEM((tm, d), F32), pltpu.VMEM((tm, d), F32)],
        compiler_params=_cparams(("parallel", "arbitrary")),
        name="rwkv_in",
    )(x, x, srow, g.reshape(1, d), mu_rkv, mu_low, w3_bf16, w1cat_bf16)


def _softplus(z):
    return jnp.maximum(z, 0.0) + jnp.log(1.0 + jnp.exp(-jnp.abs(z)))


def _pad_lora(w1, w2, width):
    r = w1.shape[1]
    return (jnp.pad(w1, ((0, 0), (0, width - r))).astype(BF16),
            jnp.pad(w2, ((0, width - r), (0, 0))).astype(BF16))


def _rwkv_consts(c):
    i = np.arange(2 * c)
    same = (i[:, None] // c) == (i[None, :] // c)
    strict = (same & ((i[:, None] % c) > (i[None, :] % c))).astype(np.float32)
    incl = (same & ((i[:, None] % c) >= (i[None, :] % c))).astype(np.float32)
    eye = np.eye(2 * c, dtype=np.float32)
    tril = np.tril(np.ones((c, c), np.float32))
    l = np.arange(LANES)
    headones = ((l[:, None] // C_HEAD) == (l[None, :] // C_HEAD)).astype(np.float32)
    return strict, incl, eye, tril, headones


def _dot2_ones(x, ones_bf16):
    hi = x.astype(BF16)
    lo = (x - hi.astype(F32)).astype(BF16)
    return (jnp.dot(hi, ones_bf16, preferred_element_type=F32)
            + jnp.dot(lo, ones_bf16, preferred_element_type=F32))


def _rwkv_core_kernel(*refs, chunk, nchunks, npairs, has_state, has_vres):
    it = iter(refs)
    r_ref, k_ref, v_ref, mid_ref = (next(it) for _ in range(4))
    w0_ref, w2_ref, a0_ref, a2_ref, g2_ref = (next(it) for _ in range(5))
    if has_vres:
        vf_ref, v0_ref, v2_ref = next(it), next(it), next(it)
    kk_ref, ka_ref, rk_ref, lg_ref, lb_ref = (next(it) for _ in range(5))
    strict_ref, incl_ref, eye_ref, tril_ref, hones_ref = (next(it) for _ in range(5))
    if has_state:
        s0_ref = next(it)
    y_ref, s_ref, st_ref, wl_ref, a_ref, g_ref = (next(it) for _ in range(6))
    if has_vres:
        vg_ref = next(it)
    c = chunk
    l_idx = pl.program_id(2)
    lane = lax.broadcasted_iota(jnp.int32, (1, LANES), 1)
    m0 = (lane < C_HEAD).astype(F32)
    m1 = 1.0 - m0

    low = lambda rng, w_ref: jnp.dot(mid_ref[:, rng[0]:rng[1]], w_ref[...], preferred_element_type=F32)
    w_log = -_softplus(-(w0_ref[...] + low(LORA_W, w2_ref))) - 0.5
    wl_ref[...] = -jnp.exp(w_log)
    a_ref[...] = jax.nn.sigmoid(a0_ref[...] + low(LORA_A, a2_ref))
    g_ref[...] = low(LORA_G, g2_ref)
    if has_vres:
        vg_ref[...] = jax.nn.sigmoid(v0_ref[...] + low(LORA_V, v2_ref))

    @pl.when(l_idx == 0)
    def _():
        for pi in range(npairs):
            if has_state:
                z = jnp.zeros((C_HEAD, C_HEAD), F32)
                top = jnp.concatenate([s0_ref[0, 2 * pi], z], axis=1)
                bot = jnp.concatenate([z, s0_ref[0, 2 * pi + 1]], axis=1)
                st_ref[pi] = jnp.concatenate([top, bot], axis=0)
            else:
                st_ref[pi] = jnp.zeros((LANES, LANES), F32)

    hones = hones_ref[...].astype(BF16)
    strict = strict_ref[...]
    incl = incl_ref[...]
    eye = eye_ref[...]
    tril = tril_ref[...]

    def stack(x):
        return jnp.concatenate([x * m0, x * m1], axis=0)

    def rowsums(xs):
        tot = _dot2_ones(jnp.concatenate(xs, axis=0), hones)
        return [tot[i * c:(i + 1) * c] for i in range(len(xs))]

    def load(pi, rows):
        cols = slice(pi * LANES, (pi + 1) * LANES)
        k = k_ref[0, rows, cols]
        v = v_ref[0, rows, cols]
        a = a_ref[rows, cols]
        if has_vres:
            v = v + (vf_ref[0, rows, cols] - v) * vg_ref[rows, cols]
        return dict(cols=cols, r=r_ref[0, rows, cols], v=v, a=a, wl=wl_ref[rows, cols],
                    kr=k * kk_ref[:, cols], kh=k * (1.0 + (a - 1.0) * ka_ref[:, cols]))

    def decays(p, ss):
        kk = p["kr"] * lax.rsqrt(jnp.maximum(ss, 1e-24))
        b = kk * p["a"]
        gc = _dot_exact01(tril, p["wl"])
        gl = gc[c - 1:c, :]
        e_neg = jnp.exp(-gc)
        e_out = jnp.exp(gl - gc)
        p.update(gl=gl, ab=-kk * jnp.exp(gc - p["wl"]), rb=p["r"] * jnp.exp(gc),
                 bt=b * e_neg, kt=p["kh"] * e_neg, bh=b * e_out, khat=p["kh"] * e_out)

    def intra(p):
        lhs = jnp.concatenate([stack(p["ab"]), stack(p["rb"])], axis=0)
        with_b = _dot_nt(lhs, jnp.concatenate([p["bt"], p["bt"]], axis=0))
        with_k = _dot_nt(lhs, jnp.concatenate([p["kt"], p["kt"]], axis=0))
        p.update(a_ab=with_b[:2 * c] * strict, a_rb=with_b[2 * c:] * incl,
                 a_ak=with_k[:2 * c] * strict, a_rk=with_k[2 * c:] * incl)

    def body(ci, carry):
        rows = pl.ds(pl.multiple_of(ci * c, c), c)
        ps = [load(pi, rows) for pi in range(npairs)]
        for p, ss in zip(ps, rowsums([p["kr"] * p["kr"] for p in ps])):
            decays(p, ss)
        for p in ps:
            intra(p)

        ts = [eye + p["a_ab"] for p in ps]
        pws = [p["a_ab"] for p in ps]
        if c > 2:
            pws = [_dot3_shared([pw], pw)[0] for pw in pws]
            n = 2
            while 2 * n < c:
                res = [_dot3_shared([t, pw], pw) for t, pw in zip(ts, pws)]
                ts = [t + r[0] for t, r in zip(ts, res)]
                pws = [r[1] for r in res]
                n *= 2
            ts = [t + _dot3_shared([t], pw)[0] for t, pw in zip(ts, pws)]

        sts = [st_ref[pi] for pi in range(npairs)]
        fss = [_dot_nt(jnp.concatenate([p["ab"], p["rb"]], axis=0), st) for p, st in zip(ps, sts)]
        vss = [stack(p["v"]) for p in ps]
        rhss = [stack(fs[:c]) + _dot(p["a_ak"], vs) for p, fs, vs in zip(ps, fss, vss)]
        uss = [_dot(t, rhs) for t, rhs in zip(ts, rhss)]
        yss = [_dot(p["a_rb"], us) + _dot(p["a_rk"], vs) for p, us, vs in zip(ps, uss, vss)]
        for pi, (p, st, us) in enumerate(zip(ps, sts, uss)):
            u = us[:c] + us[c:]
            upd = _dot_tn(jnp.concatenate([u, p["v"]], axis=0),
                          jnp.concatenate([p["bh"], p["khat"]], axis=0))
            st_ref[pi] = st * jnp.exp(p["gl"]) + upd * hones_ref[...]
        inv_n = 1.0 / C_HEAD
        ys_ = [fs[c:] + ys[:c] + ys[c:] for fs, ys in zip(fss, yss)]
        sums = rowsums(ys_ + [p["r"] * p["kh"] * rk_ref[:, p["cols"]] for p in ps])
        dlts = [y - m * inv_n for y, m in zip(ys_, sums[:npairs])]
        vars_ = rowsums([d * d for d in dlts])
        for p, dlt, var, bsum in zip(ps, dlts, vars_, sums[npairs:]):
            cols = p["cols"]
            yn = dlt * lax.rsqrt(var * inv_n + GN_EPS) * lg_ref[:, cols] + lb_ref[:, cols]
            y_ref[rows, cols] = ((yn + bsum * p["v"]) * g_ref[rows, cols]).astype(y_ref.dtype)
        return carry

    lax.fori_loop(0, nchunks, body, 0)

    @pl.when(l_idx == pl.num_programs(2) - 1)
    def _():
        for pi in range(npairs):
            st = st_ref[pi]
            s_ref[0, 2 * pi] = st[:C_HEAD, :C_HEAD]
            s_ref[0, 2 * pi + 1] = st[C_HEAD:, C_HEAD:]


def rwkv_core(rkv, mid, low2, kk_p, ka_p, rk_p, lnx_g, lnx_b, batch, seqlen, layer,
              s0=None, v_first=None, npairs=8):
    _, m, d = rkv.shape
    c = math.gcd(seqlen, RWKV_CHUNK)
    lblk = _pick_tile(seqlen, (256, 128, 64, 32, 16, 8))
    nl_blocks = seqlen // lblk
    has_state = s0 is not None
    has_vres = v_first is not None
    w = npairs * LANES
    ngroups = d // w
    midw = mid.shape[1]
    strict, incl, eye, tril, hones = (jnp.asarray(x) for x in _rwkv_consts(c))
    row = lambda b, p, l: b * nl_blocks + l
    blk3 = lambda which: pl.BlockSpec((1, lblk, w), lambda b, p, l: (which, row(b, p, l), p))
    blk2 = pl.BlockSpec((lblk, w), lambda b, p, l: (row(b, p, l), p))
    par = pl.BlockSpec((1, w), lambda b, p, l: (0, p))
    cols = lambda arr: pl.BlockSpec((arr.shape[0], w), lambda b, p, l: (0, p))
    full = lambda arr: pl.BlockSpec(arr.shape, lambda b, p, l: (0,) * arr.ndim)
    w0, w2, a0, a2, g2 = low2[:5]
    in_specs = [blk3(0), blk3(1), blk3(2), pl.BlockSpec((lblk, midw), lambda b, p, l: (row(b, p, l), 0)),
                par, cols(w2), par, cols(a2), cols(g2)]
    args = [rkv, rkv, rkv, mid, w0.reshape(1, d), w2, a0.reshape(1, d), a2, g2]
    if has_vres:
        v0, v2 = low2[5:]
        in_specs += [blk3(2), par, cols(v2)]
        args += [v_first, v0.reshape(1, d), v2]
    in_specs += [par] * 5
    args += [x.reshape(1, d) for x in (kk_p, ka_p, rk_p, lnx_g, lnx_b)]
    in_specs += [full(x) for x in (strict, incl, eye, tril, hones)]
    args += [strict, incl, eye, tril, hones]
    st_blk = pl.BlockSpec((1, 2 * npairs, C_HEAD, C_HEAD), lambda b, p, l: (b, p, 0, 0))
    if has_state:
        in_specs.append(pl.BlockSpec((None, 1, 2 * npairs, C_HEAD, C_HEAD),
                                     lambda b, p, l: (layer, b, p, 0, 0)))
        args.append(s0)
    kern = functools.partial(_rwkv_core_kernel, chunk=c, nchunks=lblk // c, npairs=npairs,
                             has_state=has_state, has_vres=has_vres)
    return pl.pallas_call(
        kern,
        grid=(batch, ngroups, nl_blocks),
        in_specs=in_specs,
        out_specs=[blk2, st_blk],
        out_shape=[jax.ShapeDtypeStruct((m, d), F32),
                   jax.ShapeDtypeStruct((batch, C_HEADS, C_HEAD, C_HEAD), F32)],
        scratch_shapes=[pltpu.VMEM((npairs, LANES, LANES), F32)]
        + [pltpu.VMEM((lblk, w), F32)] * (4 if has_vres else 3),
        compiler_params=_cparams(("parallel", "parallel", "arbitrary")),
        name="rwkv_core",
    )(*args)


def _even_layer(x, batch, seqlen, e, layer, p, st_hgrn, k_cache, v_cache):
    proj = norm_matmul(x, p["norm_mix_pre"][layer], p["w_in_even"], e)
    o_a, s_new = hgrn(proj, p["hgrn_lb_raw"], e, batch, seqlen, st_hgrn)
    k_lo = IN_A + B_WIDTH
    new_rows = min(seqlen, WINDOW)
    p3 = proj.reshape(batch, seqlen, IN_EVEN)[:, seqlen - new_rows:]
    kb = p3[:, :, k_lo:k_lo + B_KV_WIDTH].reshape(batch, new_rows, B_KV_HEADS, B_HEAD_DIM)
    vb = p3[:, :, k_lo + B_KV_WIDTH:].reshape(batch, new_rows, B_KV_HEADS, B_HEAD_DIM)
    o_b = swa(proj, batch, seqlen, p["rel_bias"], p["attn_sinks"][e], e, k_cache, v_cache)
    if k_cache is None:
        k_new, v_new = kb, vb
    else:
        k_new = jnp.concatenate([k_cache[e, :, new_rows:], kb], axis=1)
        v_new = jnp.concatenate([v_cache[e, :, new_rows:], vb], axis=1)
    x = even_out(o_a, proj, o_b, x, p["hgrn_norm_g"][e], p["w_out_even"], p["norm_mix_post"][layer], e)
    return x, s_new, k_new, v_new


def _odd_layer(x, batch, seqlen, o, layer, p, shift0, s0, v_first):
    m, d = x.shape
    g_pre = p["norm_mix_pre"][layer]
    has_vres = o > 0
    width = lambda rng: rng[1] - rng[0]
    w1p, w2p = _pad_lora(p["rw_w1"][o], p["rw_w2"][o], width(LORA_W))
    a1p, a2p = _pad_lora(p["rw_a1"][o], p["rw_a2"][o], width(LORA_A))
    g1p, g2p = _pad_lora(p["rw_g1"][o], p["rw_g2"][o], width(LORA_G))
    first, low2 = [w1p, a1p, g1p], [p["rw_w0"][o], w2p, p["rw_a0"][o], a2p, g2p]
    if has_vres:
        v1p, v2p = _pad_lora(p["rw_v1"][o - 1], p["rw_v2"][o - 1], width(LORA_V))
        first.append(v1p)
        low2 += [p["rw_v0"][o - 1], v2p]
    rkv, mid = rwkv_in(x, g_pre, shift0, batch, seqlen, p["rw_mu"][o], p["w_rkv"], o,
                       jnp.concatenate(first, axis=1), has_vres)
    yg, s_new = rwkv_core(rkv, mid, low2, p["rw_kk"][o], p["rw_ka"][o], p["rw_rk"][o],
                          p["rw_lnx_g"][o], p["rw_lnx_b"][o], batch, seqlen, o, s0,
                          v_first if has_vres else None)
    shift_new = rmsnorm_rows(x.reshape(batch, seqlen, d)[:, -1], g_pre)
    x = odd_out(yg, x, p["rw_wo"], p["norm_mix_post"][layer], o)
    return x, s_new, shift_new, rkv


def _trunk(x3, st_hgrn, k_cache, v_cache, st_rwkv, st_shift, p):
    batch, seqlen, d = x3.shape
    x = x3.reshape(batch * seqlen, d)
    has_state = st_hgrn is not None
    hgrn_out, k_out, v_out, rwkv_out, shift_out = [], [], [], [], []
    v_first = None
    for layer in range(DEPTH):
        if layer % 2 == 0:
            e = layer // 2
            x, s_new, k_new, v_new = _even_layer(x, batch, seqlen, e, layer, p, st_hgrn, k_cache, v_cache)
            hgrn_out.append(s_new)
            k_out.append(k_new)
            v_out.append(v_new)
        else:
            o = layer // 2
            x, s_new, sh_new, rkv = _odd_layer(
                x, batch, seqlen, o, layer, p,
                st_shift[o] if has_state else None,
                st_rwkv,
                v_first)
            if o == 0:
                v_first = rkv
            rwkv_out.append(s_new)
            shift_out.append(sh_new)
        x = ffn(x, p["norm_ffn_pre"][layer], p["w_up"], p["w_down"], p["norm_ffn_post"][layer], layer)
    return (x.reshape(batch, seqlen, d), jnp.stack(hgrn_out), jnp.stack(k_out), jnp.stack(v_out),
            jnp.stack(rwkv_out), jnp.stack(shift_out))


def kernel(x_prompt, x_sample, state_hgrn, cache_swa_k, cache_swa_v, state_rwkv, state_shift,
           norm_mix_pre, norm_mix_post, norm_ffn_pre, norm_ffn_post,
           w_in_even, w_out_even, hgrn_lb_raw, hgrn_norm_g, rel_bias, attn_sinks,
           rw_mu, rw_wr, rw_wk, rw_wv, rw_wo, rw_w0, rw_w1, rw_w2, rw_a0, rw_a1, rw_a2,
           rw_v0, rw_v1, rw_v2, rw_g1, rw_g2, rw_kk, rw_ka, rw_rk, rw_lnx_g, rw_lnx_b,
           w_up, w_down):
    p = {
        "norm_mix_pre": norm_mix_pre, "norm_mix_post": norm_mix_post,
        "norm_ffn_pre": norm_ffn_pre, "norm_ffn_post": norm_ffn_post,
        "w_in_even": w_in_even.astype(BF16), "w_out_even": w_out_even.astype(BF16),
        "hgrn_lb_raw": hgrn_lb_raw, "hgrn_norm_g": hgrn_norm_g,
        "rel_bias": rel_bias, "attn_sinks": attn_sinks,
        "rw_mu": rw_mu, "w_rkv": jnp.stack([rw_wr, rw_wk, rw_wv], axis=1).astype(BF16),
        "rw_wo": rw_wo.astype(BF16),
        "rw_w0": rw_w0, "rw_w1": rw_w1, "rw_w2": rw_w2, "rw_a0": rw_a0, "rw_a1": rw_a1, "rw_a2": rw_a2,
        "rw_v0": rw_v0, "rw_v1": rw_v1, "rw_v2": rw_v2, "rw_g1": rw_g1, "rw_g2": rw_g2,
        "rw_kk": rw_kk, "rw_ka": rw_ka, "rw_rk": rw_rk, "rw_lnx_g": rw_lnx_g, "rw_lnx_b": rw_lnx_b,
        "w_up": w_up.astype(BF16), "w_down": w_down.astype(BF16),
    }
    y_p, hgrn_p, k_p, v_p, rwkv_p, shift_p = _trunk(x_prompt, None, None, None, None, None, p)
    y_s, hgrn_s, k_s, v_s, rwkv_s, shift_s = _trunk(
        x_sample, state_hgrn, cache_swa_k, cache_swa_v, state_rwkv, state_shift, p)
    return (y_p, y_s, hgrn_p, hgrn_s, k_p, k_s, v_p, v_s, rwkv_p, rwkv_s, shift_p, shift_s)
```

```python
import functools
import math

import numpy as np
import jax
import jax.numpy as jnp
from jax import lax
from jax.experimental import pallas as pl
from jax.experimental.pallas import tpu as pltpu

F32 = jnp.float32
BF16 = jnp.bfloat16

D_MODEL = 2048
DEPTH = 4
N_EVEN = 2
N_ODD = 2
A_HEADS = 8
A_KDIM = 128
A_VDIM = 128
A_WIDTH = 1024
A_QK = 1024
B_HEADS = 16
B_HEAD_DIM = 64
B_KV_HEADS = 4
B_GROUP = 4
B_WIDTH = 1024
B_KV_WIDTH = 256
WINDOW = 128
N_BUCKETS = 32
MAX_DISTANCE = 128
MASK_VALUE = -1e30
IN_A = 4096
IN_EVEN = 5632
C_HEAD = 64
C_HEADS = 32
GN_EPS = 64e-5
D_FF = 8192
NORM_EPS = 1e-6

LANES = 128
VMEM_LIMIT = 56 * 1024 * 1024

HGRN_CHUNK = 128
RWKV_CHUNK = 64


def _cparams(sem):
    return pltpu.CompilerParams(dimension_semantics=sem, vmem_limit_bytes=VMEM_LIMIT)


def _rms(x, g):
    return x * lax.rsqrt(jnp.mean(x * x, axis=-1, keepdims=True) + NORM_EPS) * g


def _dot(a, b):
    return jnp.dot(a.astype(BF16), b.astype(BF16), preferred_element_type=F32)


def _dot_nt(a, b):
    return lax.dot_general(a.astype(BF16), b.astype(BF16), (((1,), (1,)), ((), ())),
                           preferred_element_type=F32)


def _dot_tn(a, b):
    return lax.dot_general(a.astype(BF16), b.astype(BF16), (((0,), (0,)), ((), ())),
                           preferred_element_type=F32)


def _split2(x):
    hi = x.astype(BF16)
    return hi, (x - hi.astype(F32)).astype(BF16)


def _dot3_shared(lhs_list, b):
    bh, bl = _split2(b)
    parts = [_split2(a) for a in lhs_list]
    his = [p[0] for p in parts]
    los = [p[1] for p in parts]
    by_hi = jnp.dot(jnp.concatenate(his + los, axis=0), bh, preferred_element_type=F32)
    by_lo = jnp.dot(jnp.concatenate(his, axis=0), bl, preferred_element_type=F32)
    n = sum(a.shape[0] for a in lhs_list)
    out, off = [], 0
    for a in lhs_list:
        m = a.shape[0]
        out.append(by_hi[off:off + m] + by_hi[n + off:n + off + m] + by_lo[off:off + m])
        off += m
    return out


def _dot_exact01(sel, x):
    hi = x.astype(BF16)
    r1 = x - hi.astype(F32)
    mid = r1.astype(BF16)
    lo = (r1 - mid.astype(F32)).astype(BF16)
    s = sel.astype(BF16)
    return (jnp.dot(s, hi, preferred_element_type=F32)
            + jnp.dot(s, mid, preferred_element_type=F32)
            + jnp.dot(s, lo, preferred_element_type=F32))


def _pick_tile(m, cands):
    for c in cands:
        if m % c == 0:
            return c
    return m


def _norm_matmul_kernel(x_ref, g_ref, w_ref, o_ref, xn_ref):
    @pl.when(pl.program_id(1) == 0)
    def _():
        xn_ref[...] = _rms(x_ref[...], g_ref[...]).astype(BF16)

    o_ref[...] = jnp.dot(xn_ref[...], w_ref[...], preferred_element_type=F32)


def norm_matmul(x, g, w_stack_bf16, li, tn=1408):
    m, d = x.shape
    n = w_stack_bf16.shape[2]
    tm = _pick_tile(m, (1024, 512, 256, 128, 64, 32, 16, 8))
    return pl.pallas_call(
        _norm_matmul_kernel,
        grid=(m // tm, n // tn),
        in_specs=[pl.BlockSpec((tm, d), lambda i, j: (i, 0)),
                  pl.BlockSpec((1, d), lambda i, j: (0, 0)),
                  pl.BlockSpec((None, d, tn), lambda i, j: (li, 0, j))],
        out_specs=pl.BlockSpec((tm, tn), lambda i, j: (i, j)),
        out_shape=jax.ShapeDtypeStruct((m, n), F32),
        scratch_shapes=[pltpu.VMEM((tm, d), BF16)],
        compiler_params=_cparams(("parallel", "arbitrary")),
        name="norm_matmul",
    )(x, g.reshape(1, d), w_stack_bf16)


def _ffn_kernel(x_ref, gpre_ref, wup_ref, wdn_ref, gpost_ref, o_ref, xn_ref, acc_ref):
    f = pl.program_id(1)

    @pl.when(f == 0)
    def _():
        xn_ref[...] = _rms(x_ref[...], gpre_ref[...]).astype(BF16)
        acc_ref[...] = jnp.zeros_like(acc_ref)

    h = jnp.dot(xn_ref[...], wup_ref[...], preferred_element_type=F32)
    h = jnp.square(jnp.maximum(h, 0.0)).astype(BF16)
    acc_ref[...] += jnp.dot(h, wdn_ref[...], preferred_element_type=F32)

    @pl.when(f == pl.num_programs(1) - 1)
    def _():
        o_ref[...] = x_ref[...] + _rms(acc_ref[...], gpost_ref[...])


def ffn(x, gpre, wup_bf16, wdn_bf16, gpost, li, tf=1024):
    m, d = x.shape
    dff = wup_bf16.shape[2]
    tm = _pick_tile(m, (512, 256, 128, 64, 32, 16, 8))
    return pl.pallas_call(
        _ffn_kernel,
        grid=(m // tm, dff // tf),
        in_specs=[pl.BlockSpec((tm, d), lambda i, f: (i, 0)),
                  pl.BlockSpec((1, d), lambda i, f: (0, 0)),
                  pl.BlockSpec((None, d, tf), lambda i, f: (li, 0, f)),
                  pl.BlockSpec((None, tf, d), lambda i, f: (li, f, 0)),
                  pl.BlockSpec((1, d), lambda i, f: (0, 0))],
        out_specs=pl.BlockSpec((tm, d), lambda i, f: (i, 0)),
        out_shape=jax.ShapeDtypeStruct((m, d), F32),
        scratch_shapes=[pltpu.VMEM((tm, d), BF16), pltpu.VMEM((tm, d), F32)],
        compiler_params=_cparams(("parallel", "arbitrary")),
        name="ffn",
    )(x, gpre.reshape(1, d), wup_bf16, wdn_bf16, gpost.reshape(1, d))


def _even_out_kernel(oa_ref, ga_ref, ob_ref, x_ref, ag_ref, w_ref, gpost_ref, o_ref):
    ga = ga_ref[...]
    oan = _rms(oa_ref[...], ag_ref[...]) * (ga * jax.nn.sigmoid(ga))
    mix = (jnp.dot(oan.astype(BF16), w_ref[:A_WIDTH, :], preferred_element_type=F32)
           + jnp.dot(ob_ref[...].astype(BF16), w_ref[A_WIDTH:, :], preferred_element_type=F32))
    o_ref[...] = x_ref[...] + _rms(mix, gpost_ref[...])


def even_out(o_a, proj, o_b, x, a_norm_g, w_out_bf16, gpost, li):
    m, d = x.shape
    tm = _pick_tile(m, (384, 256, 128, 64, 32, 16, 8))
    ga_blk = (3 * A_WIDTH) // A_WIDTH
    return pl.pallas_call(
        _even_out_kernel,
        grid=(m // tm,),
        in_specs=[pl.BlockSpec((tm, A_WIDTH), lambda i: (i, 0)),
                  pl.BlockSpec((tm, A_WIDTH), lambda i: (i, ga_blk)),
                  pl.BlockSpec((tm, B_WIDTH), lambda i: (i, 0)),
                  pl.BlockSpec((tm, d), lambda i: (i, 0)),
                  pl.BlockSpec((1, A_WIDTH), lambda i: (0, 0)),
                  pl.BlockSpec((None, A_WIDTH + B_WIDTH, d), lambda i: (li, 0, 0)),
                  pl.BlockSpec((1, d), lambda i: (0, 0))],
        out_specs=pl.BlockSpec((tm, d), lambda i: (i, 0)),
        out_shape=jax.ShapeDtypeStruct((m, d), F32),
        compiler_params=_cparams(("parallel",)),
        name="even_out",
    )(o_a, proj, o_b, x, a_norm_g.reshape(1, A_WIDTH), w_out_bf16, gpost.reshape(1, d))


def _odd_out_kernel(y_ref, x_ref, w_ref, gpost_ref, o_ref):
    mix = jnp.dot(y_ref[...].astype(BF16), w_ref[...], preferred_element_type=F32)
    o_ref[...] = x_ref[...] + _rms(mix, gpost_ref[...])


def odd_out(yg, x, wo_bf16, gpost, li):
    m, d = x.shape
    tm = _pick_tile(m, (384, 256, 128, 64, 32, 16, 8))
    return pl.pallas_call(
        _odd_out_kernel,
        grid=(m // tm,),
        in_specs=[pl.BlockSpec((tm, d), lambda i: (i, 0)),
                  pl.BlockSpec((tm, d), lambda i: (i, 0)),
                  pl.BlockSpec((None, d, d), lambda i: (li, 0, 0)),
                  pl.BlockSpec((1, d), lambda i: (0, 0))],
        out_specs=pl.BlockSpec((tm, d), lambda i: (i, 0)),
        out_shape=jax.ShapeDtypeStruct((m, d), F32),
        compiler_params=_cparams(("parallel",)),
        name="odd_out",
    )(yg, x, wo_bf16, gpost.reshape(1, d))


def _level_consts(c):
    levels = []
    s = c // 2
    while s >= 1:
        levels.append(s)
        s //= 2
    nl = len(levels)
    sel = np.zeros((nl, c, c), np.float32)
    mask = np.zeros((nl, c, c), np.float32)
    idx = np.arange(c)
    for l, s in enumerate(levels):
        ref_row = (idx // (2 * s)) * (2 * s) + s - 1
        sel[l, idx, ref_row] = 1.0
        same = (idx[:, None] // (2 * s)) == (idx[None, :] // (2 * s))
        upper = (idx[:, None] % (2 * s)) >= s
        lower = (idx[None, :] % (2 * s)) < s
        mask[l] = (same & upper & lower).astype(np.float32)
    tril = np.tril(np.ones((c, c), np.float32))
    return nl, sel.reshape(nl * c, c), mask, tril


def _hgrn_kernel(*refs, layer, chunk, nchunks, nlevels, has_state, nheads):
    if has_state:
        (q_ref, f_ref, i_ref, lb_ref, sel_ref, mask_ref, tril_ref, s0_ref,
         o_ref, s_ref, st_ref) = refs
    else:
        (q_ref, f_ref, i_ref, lb_ref, sel_ref, mask_ref, tril_ref,
         o_ref, s_ref, st_ref) = refs
    c = chunk
    nh = nheads
    l_idx = pl.program_id(2)

    @pl.when(l_idx == 0)
    def _():
        for hi in range(nh):
            if has_state:
                st_ref[hi] = s0_ref[0, hi].T
            else:
                st_ref[hi] = jnp.zeros((A_VDIM, A_KDIM), F32)

    lbr = lb_ref[...]
    e = jnp.exp(lbr - jnp.max(lbr, axis=0, keepdims=True))
    p = e / jnp.sum(e, axis=0, keepdims=True)
    lb = jnp.zeros((1, nh * LANES), F32)
    for i in range(1, layer + 1):
        lb = lb + p[i:i + 1, :]
    one_m_lb = 1.0 - lb
    head = lambda x, hi: x[:, hi * LANES:(hi + 1) * LANES]

    def body(ci, carry):
        rows = pl.ds(pl.multiple_of(ci * c, c), c)
        fq = f_ref[rows, :]
        qr = q_ref[rows, :]
        v = i_ref[rows, :]
        logf = jnp.log(lb + one_m_lb * jax.nn.sigmoid(fq))
        k = one_m_lb * jax.nn.sigmoid(-fq)
        q = qr * jax.nn.sigmoid(qr) * (A_KDIM ** -0.5)

        g = _dot_exact01(tril_ref[...], logf)
        refs_g = _dot_exact01(sel_ref[...], g)
        glast = g[c - 1:c, :]
        q_in = q * jnp.exp(g)
        kd = k * jnp.exp(glast - g)
        dec = jnp.exp(glast)
        diag = q * k

        sts = [st_ref[hi] for hi in range(nh)]
        os_ = [_dot_nt(head(q_in, hi), sts[hi]) for hi in range(nh)]
        attns = [jnp.zeros((c, c), F32) for _ in range(nh)]
        for l in range(nlevels):
            gr = refs_g[l * c:(l + 1) * c, :]
            qs = q * jnp.exp(jnp.minimum(g - gr, 0.0))
            ks = k * jnp.exp(jnp.minimum(gr - g, 0.0))
            ml = mask_ref[l]
            attns = [at + ml * _dot_nt(head(qs, hi), head(ks, hi)) for hi, at in enumerate(attns)]
        for hi in range(nh):
            vh = head(v, hi)
            o = os_[hi] + _dot(attns[hi], vh) + jnp.sum(head(diag, hi), axis=-1, keepdims=True) * vh
            o_ref[rows, hi * LANES:(hi + 1) * LANES] = o
        for hi in range(nh):
            st_ref[hi] = sts[hi] * head(dec, hi) + _dot_tn(head(v, hi), head(kd, hi))
        return carry

    lax.fori_loop(0, nchunks, body, 0, unroll=2 if nchunks % 2 == 0 else 1)

    @pl.when(l_idx == pl.num_programs(2) - 1)
    def _():
        for hi in range(nh):
            s_ref[0, hi] = st_ref[hi].T


def hgrn(proj, lb_raw, layer, batch, seqlen, s0=None, nheads=4):
    m = proj.shape[0]
    c = math.gcd(seqlen, HGRN_CHUNK)
    lblk = _pick_tile(seqlen, (512, 256, 128, 64, 32, 16, 8))
    nl_blocks = seqlen // lblk
    nlevels, sel, mask, tril = _level_consts(c)
    has_state = s0 is not None
    w = nheads * LANES
    ngroups = A_HEADS // nheads
    kern = functools.partial(_hgrn_kernel, layer=layer, chunk=c, nchunks=lblk // c,
                             nlevels=nlevels, has_state=has_state, nheads=nheads)
    row = lambda b, h, l: b * nl_blocks + l
    in_specs = [pl.BlockSpec((lblk, w), lambda b, h, l: (row(b, h, l), h)),
                pl.BlockSpec((lblk, w), lambda b, h, l: (row(b, h, l), ngroups + h)),
                pl.BlockSpec((lblk, w), lambda b, h, l: (row(b, h, l), 2 * ngroups + h)),
                pl.BlockSpec((N_EVEN, w), lambda b, h, l: (0, h)),
                pl.BlockSpec((nlevels * c, c), lambda b, h, l: (0, 0)),
                pl.BlockSpec((nlevels, c, c), lambda b, h, l: (0, 0, 0)),
                pl.BlockSpec((c, c), lambda b, h, l: (0, 0))]
    args = [proj, proj, proj, lb_raw, jnp.asarray(sel), jnp.asarray(mask), jnp.asarray(tril)]
    st_blk = pl.BlockSpec((1, nheads, A_KDIM, A_VDIM), lambda b, h, l: (b, h, 0, 0))
    if has_state:
        in_specs.append(pl.BlockSpec((None, 1, nheads, A_KDIM, A_VDIM), lambda b, h, l: (layer, b, h, 0, 0)))
        args.append(s0)
    return pl.pallas_call(
        kern,
        grid=(batch, ngroups, nl_blocks),
        in_specs=in_specs,
        out_specs=[pl.BlockSpec((lblk, w), lambda b, h, l: (row(b, h, l), h)), st_blk],
        out_shape=[jax.ShapeDtypeStruct((m, A_WIDTH), F32),
                   jax.ShapeDtypeStruct((batch, A_HEADS, A_KDIM, A_VDIM), F32)],
        scratch_shapes=[pltpu.VMEM((nheads, A_VDIM, A_KDIM), F32)],
        compiler_params=_cparams(("parallel", "parallel", "arbitrary")),
        name="hgrn",
    )(*args)


def _t5_bucket(dist):
    max_exact = N_BUCKETS // 2
    d = np.maximum(dist, 0)
    large = max_exact + (np.log(np.maximum(d, max_exact).astype(np.float32) / max_exact)
                         / math.log(MAX_DISTANCE / max_exact) * (N_BUCKETS - max_exact)).astype(np.int32)
    large = np.minimum(large, N_BUCKETS - 1)
    return np.where(d < max_exact, d, large).astype(np.int32)


def _swa_kernel(q_ref, kp_ref, kc_ref, vp_ref, vc_ref, bucket_ref, band_ref, rb_ref, sink_ref,
                o_ref, bias_ref, *, qb, span, prev_always_valid):
    first = (pl.program_id(0) == 0) & (pl.program_id(1) == 0)

    @pl.when(first)
    def _():
        bk = bucket_ref[...]
        band = band_ref[...]

        def per_head(h, carry):
            def per_bucket(bi, acc):
                return jnp.where(bk == bi, rb_ref[bi, h], acc)
            acc = lax.fori_loop(0, N_BUCKETS, per_bucket, jnp.zeros((qb, span), F32))
            bias_ref[h] = jnp.where(band > 0, acc, MASK_VALUE)
            return carry

        lax.fori_loop(0, B_HEADS, per_head, 0)

    scale = B_HEAD_DIM ** -0.5
    q = q_ref[...]
    kall = jnp.concatenate([kp_ref[...], kc_ref[...]], axis=0)
    vall = jnp.concatenate([vp_ref[...], vc_ref[...]], axis=0)
    if not prev_always_valid:
        col = lax.broadcasted_iota(jnp.int32, (qb, span), 1)
        no_prev = (col < WINDOW) & (pl.program_id(1) == 0)
    heads = range(B_HEADS)
    ks = [kall[:, kh * B_HEAD_DIM:(kh + 1) * B_HEAD_DIM].astype(BF16) for kh in range(B_KV_HEADS)]
    vs = [vall[:, kh * B_HEAD_DIM:(kh + 1) * B_HEAD_DIM].astype(BF16) for kh in range(B_KV_HEADS)]
    qs = [(q[:, h * B_HEAD_DIM:(h + 1) * B_HEAD_DIM] * scale).astype(BF16) for h in heads]
    ss = [_dot_nt(qs[h], ks[h // B_GROUP]) + bias_ref[h] for h in heads]
    if not prev_always_valid:
        ss = [jnp.where(no_prev, MASK_VALUE, s) for s in ss]
    ms = [jnp.maximum(jnp.max(ss[h], axis=-1, keepdims=True), sink_ref[h]) for h in heads]
    ps = [jnp.exp(s - m) for s, m in zip(ss, ms)]
    denoms = [jnp.sum(ps[h], axis=-1, keepdims=True) + jnp.exp(sink_ref[h] - ms[h]) for h in heads]
    outs = [_dot(ps[h], vs[h // B_GROUP]) / denoms[h] for h in heads]
    o_ref[...] = jnp.concatenate(outs, axis=1)


def swa(proj, batch, seqlen, rel_bias, sinks, layer, k_past=None, v_past=None):
    m = proj.shape[0]
    has_cache = k_past is not None
    qb = math.gcd(seqlen, WINDOW)
    nb = seqlen // qb
    span = WINDOW + qb
    dist = np.arange(qb)[:, None] + WINDOW - np.arange(span)[None, :]
    band = ((dist >= 0) & (dist < WINDOW)).astype(np.float32)
    bucket = _t5_bucket(dist)
    q_col = IN_A // B_WIDTH
    k_col = (IN_A + B_WIDTH) // B_KV_WIDTH
    v_col = k_col + 1
    cur = lambda c: (lambda b, n: (b * nb + n, c))
    if has_cache:
        assert nb == 1
        prev_k = pl.BlockSpec((None, WINDOW, B_KV_WIDTH), lambda b, n: (layer, b, 0))
        prev_v = pl.BlockSpec((None, WINDOW, B_KV_WIDTH), lambda b, n: (layer, b, 0))
        kp_arr = k_past.reshape(k_past.shape[0], batch * WINDOW, B_KV_WIDTH)
        vp_arr = v_past.reshape(v_past.shape[0], batch * WINDOW, B_KV_WIDTH)
    else:
        assert qb == WINDOW
        prev = lambda c: (lambda b, n: (b * nb + jnp.maximum(n - 1, 0), c))
        prev_k = pl.BlockSpec((WINDOW, B_KV_WIDTH), prev(k_col))
        prev_v = pl.BlockSpec((WINDOW, B_KV_WIDTH), prev(v_col))
        kp_arr, vp_arr = proj, proj
    kern = functools.partial(_swa_kernel, qb=qb, span=span, prev_always_valid=has_cache)
    return pl.pallas_call(
        kern,
        grid=(batch, nb),
        in_specs=[pl.BlockSpec((qb, B_WIDTH), cur(q_col)),
                  prev_k,
                  pl.BlockSpec((qb, B_KV_WIDTH), cur(k_col)),
                  prev_v,
                  pl.BlockSpec((qb, B_KV_WIDTH), cur(v_col)),
                  pl.BlockSpec((qb, span), lambda b, n: (0, 0)),
                  pl.BlockSpec((qb, span), lambda b, n: (0, 0)),
                  pl.BlockSpec(memory_space=pltpu.SMEM),
                  pl.BlockSpec(memory_space=pltpu.SMEM)],
        out_specs=pl.BlockSpec((qb, B_WIDTH), lambda b, n: (b * nb + n, 0)),
        out_shape=jax.ShapeDtypeStruct((m, B_WIDTH), F32),
        scratch_shapes=[pltpu.VMEM((B_HEADS, qb, span), F32)],
        compiler_params=_cparams(("arbitrary", "arbitrary")),
        name="swa",
    )(proj, kp_arr, proj, vp_arr, proj, jnp.asarray(bucket), jnp.asarray(band), rel_bias, sinks)


def _rmsnorm_kernel(x_ref, g_ref, o_ref):
    o_ref[...] = _rms(x_ref[...], g_ref[...])


def rmsnorm_rows(x, g):
    m, d = x.shape
    tm = _pick_tile(m, (512, 256, 128, 64, 32, 16, 8))
    return pl.pallas_call(
        _rmsnorm_kernel,
        grid=(m // tm,),
        in_specs=[pl.BlockSpec((tm, d), lambda i: (i, 0)), pl.BlockSpec((1, d), lambda i: (0, 0))],
        out_specs=pl.BlockSpec((tm, d), lambda i: (i, 0)),
        out_shape=jax.ShapeDtypeStruct((m, d), F32),
        compiler_params=_cparams(("parallel",)),
        name="rmsnorm",
    )(x, g.reshape(1, d))


LORA_W = (0, 128)
LORA_A = (128, 256)
LORA_G = (256, 512)
LORA_V = (512, 640)


def _rwkv_in_kernel(x_ref, xp_ref, s_ref, g_ref, mu3_ref, mul_ref, w_ref, w1_ref,
                    rkv_ref, mid_ref, h_s, hp_s, *, tm, seqlen, has_vres):
    ph = pl.program_id(1)

    @pl.when(ph == 0)
    def _():
        g = g_ref[...]
        h = _rms(x_ref[...], g)
        rowid = lax.broadcasted_iota(jnp.int32, h.shape, 0)
        rolled = pltpu.roll(h, 1, 0)
        if seqlen % tm == 0:
            prev_last = _rms(xp_ref[...], g)[7:8, :]
            at_start = pl.program_id(0) % (seqlen // tm) == 0
            first = jnp.where(at_start, s_ref[...], prev_last)
            hp = jnp.where(rowid == 0, first, rolled)
        else:
            hp = jnp.where(rowid % seqlen == 0, s_ref[...], rolled)
        h_s[...] = h
        hp_s[...] = hp

    @pl.when(ph < 3)
    def _():
        h = h_s[...]
        xm = (h + (hp_s[...] - h) * mu3_ref[0]).astype(BF16)
        rkv_ref[0] = jnp.dot(xm, w_ref[...], preferred_element_type=F32)

    @pl.when(ph == 3)
    def _():
        h = h_s[...]
        xx = hp_s[...] - h
        mix = lambda i: (h + xx * mul_ref[i:i + 1, :]).astype(BF16)
        low = lambda i, rng: jnp.dot(mix(i), w1_ref[:, rng[0]:rng[1]], preferred_element_type=F32)
        parts = [jnp.tanh(low(0, LORA_W)), low(1, LORA_A), jax.nn.sigmoid(low(2, LORA_G))]
        if has_vres:
            parts.append(low(3, LORA_V))
        mid_ref[...] = jnp.concatenate(parts, axis=1).astype(BF16)


def rwkv_in(x, g, shift0, batch, seqlen, mu, w3_bf16, li, w1cat_bf16, has_vres):
    m, d = x.shape
    tm = next(t for t in (512, 256, 128, 64, 32, 16, 8)
              if m % t == 0 and (seqlen % t == 0 or t % seqlen == 0))
    midw = w1cat_bf16.shape[1]
    mu_rkv = jnp.stack([mu[0], mu[2], mu[3]])[:, None, :]
    mu_low = jnp.stack([mu[1], mu[4], mu[5], mu[3]])
    if shift0 is None:
        shift0 = jnp.zeros((batch, d), F32)
    if seqlen % tm == 0:
        srow = shift0[:, None, :]
        tps = seqlen // tm
        s_spec = pl.BlockSpec((None, 1, d), lambda i, p: (i // tps, 0, 0))
    else:
        srow = jnp.repeat(shift0, seqlen, axis=0)
        s_spec = pl.BlockSpec((tm, d), lambda i, p: (i, 0))
    sub = tm // 8
    kern = functools.partial(_rwkv_in_kernel, tm=tm, seqlen=seqlen, has_vres=has_vres)
    return pl.pallas_call(
        kern,
        grid=(m // tm, 4),
        in_specs=[pl.BlockSpec((tm, d), lambda i, p: (i, 0)),
                  pl.BlockSpec((8, d), lambda i, p: (jnp.maximum(i * sub - 1, 0), 0)),
                  s_spec,
                  pl.BlockSpec((1, d), lambda i, p: (0, 0)),
                  pl.BlockSpec((1, 1, d), lambda i, p: (jnp.minimum(p, 2), 0, 0)),
                  pl.BlockSpec((4, d), lambda i, p: (0, 0)),
                  pl.BlockSpec((None, None, d, d), lambda i, p: (li, jnp.minimum(p, 2), 0, 0)),
                  pl.BlockSpec((d, midw), lambda i, p: (0, 0))],
        out_specs=[pl.BlockSpec((1, tm, d), lambda i, p: (jnp.minimum(p, 2), i, 0)),
                   pl.BlockSpec((tm, midw), lambda i, p: (i, 0))],
        out_shape=[jax.ShapeDtypeStruct((3, m, d), F32),
                   jax.ShapeDtypeStruct((m, midw), BF16)],
        scratch_shapes=[pltpu.VMEM((tm, d), F32), pltpu.VMEM((tm, d), F32)],
        compiler_params=_cparams(("parallel", "arbitrary")),
        name="rwkv_in",
    )(x, x, srow, g.reshape(1, d), mu_rkv, mu_low, w3_bf16, w1cat_bf16)


def _softplus(z):
    return jnp.maximum(z, 0.0) + jnp.log(1.0 + jnp.exp(-jnp.abs(z)))


def _pad_lora(w1, w2, width):
    r = w1.shape[1]
    return (jnp.pad(w1, ((0, 0), (0, width - r))).astype(BF16),
            jnp.pad(w2, ((0, width - r), (0, 0))).astype(BF16))


def _rwkv_consts(c):
    i = np.arange(2 * c)
    same = (i[:, None] // c) == (i[None, :] // c)
    strict = (same & ((i[:, None] % c) > (i[None, :] % c))).astype(np.float32)
    incl = (same & ((i[:, None] % c) >= (i[None, :] % c))).astype(np.float32)
    eye = np.eye(2 * c, dtype=np.float32)
    tril = np.tril(np.ones((c, c), np.float32))
    l = np.arange(LANES)
    headones = ((l[:, None] // C_HEAD) == (l[None, :] // C_HEAD)).astype(np.float32)
    return strict, incl, eye, tril, headones


def _dot2_ones(x, ones_bf16):
    hi = x.astype(BF16)
    lo = (x - hi.astype(F32)).astype(BF16)
    return (jnp.dot(hi, ones_bf16, preferred_element_type=F32)
            + jnp.dot(lo, ones_bf16, preferred_element_type=F32))


def _rwkv_core_kernel(*refs, chunk, nchunks, npairs, has_state, has_vres):
    it = iter(refs)
    r_ref, k_ref, v_ref, mid_ref = (next(it) for _ in range(4))
    w0_ref, w2_ref, a0_ref, a2_ref, g2_ref = (next(it) for _ in range(5))
    if has_vres:
        vf_ref, v0_ref, v2_ref = next(it), next(it), next(it)
    kk_ref, ka_ref, rk_ref, lg_ref, lb_ref = (next(it) for _ in range(5))
    strict_ref, incl_ref, eye_ref, tril_ref, hones_ref = (next(it) for _ in range(5))
    if has_state:
        s0_ref = next(it)
    y_ref, s_ref, st_ref, wl_ref, a_ref, g_ref = (next(it) for _ in range(6))
    if has_vres:
        vg_ref = next(it)
    c = chunk
    l_idx = pl.program_id(2)
    lane = lax.broadcasted_iota(jnp.int32, (1, LANES), 1)
    m0 = (lane < C_HEAD).astype(F32)
    m1 = 1.0 - m0

    low = lambda rng, w_ref: jnp.dot(mid_ref[:, rng[0]:rng[1]], w_ref[...], preferred_element_type=F32)
    w_log = -_softplus(-(w0_ref[...] + low(LORA_W, w2_ref))) - 0.5
    wl_ref[...] = -jnp.exp(w_log)
    a_ref[...] = jax.nn.sigmoid(a0_ref[...] + low(LORA_A, a2_ref))
    g_ref[...] = low(LORA_G, g2_ref)
    if has_vres:
        vg_ref[...] = jax.nn.sigmoid(v0_ref[...] + low(LORA_V, v2_ref))

    @pl.when(l_idx == 0)
    def _():
        for pi in range(npairs):
            if has_state:
                z = jnp.zeros((C_HEAD, C_HEAD), F32)
                top = jnp.concatenate([s0_ref[0, 2 * pi], z], axis=1)
                bot = jnp.concatenate([z, s0_ref[0, 2 * pi + 1]], axis=1)
                st_ref[pi] = jnp.concatenate([top, bot], axis=0)
            else:
                st_ref[pi] = jnp.zeros((LANES, LANES), F32)

    hones = hones_ref[...].astype(BF16)
    strict = strict_ref[...]
    incl = incl_ref[...]
    eye = eye_ref[...]
    tril = tril_ref[...]

    def stack(x):
        return jnp.concatenate([x * m0, x * m1], axis=0)

    def rowsums(xs):
        tot = _dot2_ones(jnp.concatenate(xs, axis=0), hones)
        return [tot[i * c:(i + 1) * c] for i in range(len(xs))]

    def load(pi, rows):
        cols = slice(pi * LANES, (pi + 1) * LANES)
        k = k_ref[0, rows, cols]
        v = v_ref[0, rows, cols]
        a = a_ref[rows, cols]
        if has_vres:
            v = v + (vf_ref[0, rows, cols] - v) * vg_ref[rows, cols]
        return dict(cols=cols, r=r_ref[0, rows, cols], v=v, a=a, wl=wl_ref[rows, cols],
                    kr=k * kk_ref[:, cols], kh=k * (1.0 + (a - 1.0) * ka_ref[:, cols]))

    def decays(p, ss):
        kk = p["kr"] * lax.rsqrt(jnp.maximum(ss, 1e-24))
        b = kk * p["a"]
        gc = _dot_exact01(tril, p["wl"])
        gl = gc[c - 1:c, :]
        e_neg = jnp.exp(-gc)
        e_out = jnp.exp(gl - gc)
        p.update(gl=gl, ab=-kk * jnp.exp(gc - p["wl"]), rb=p["r"] * jnp.exp(gc),
                 bt=b * e_neg, kt=p["kh"] * e_neg, bh=b * e_out, khat=p["kh"] * e_out)

    def intra(p):
        lhs = jnp.concatenate([stack(p["ab"]), stack(p["rb"])], axis=0)
        with_b = _dot_nt(lhs, jnp.concatenate([p["bt"], p["bt"]], axis=0))
        with_k = _dot_nt(lhs, jnp.concatenate([p["kt"], p["kt"]], axis=0))
        p.update(a_ab=with_b[:2 * c] * strict, a_rb=with_b[2 * c:] * incl,
                 a_ak=with_k[:2 * c] * strict, a_rk=with_k[2 * c:] * incl)

    def body(ci, carry):
        rows = pl.ds(pl.multiple_of(ci * c, c), c)
        ps = [load(pi, rows) for pi in range(npairs)]
        for p, ss in zip(ps, rowsums([p["kr"] * p["kr"] for p in ps])):
            decays(p, ss)
        for p in ps:
            intra(p)

        ts = [eye + p["a_ab"] for p in ps]
        pws = [p["a_ab"] for p in ps]
        if c > 2:
            pws = [_dot3_shared([pw], pw)[0] for pw in pws]
            n = 2
            while 2 * n < c:
                res = [_dot3_shared([t, pw], pw) for t, pw in zip(ts, pws)]
                ts = [t + r[0] for t, r in zip(ts, res)]
                pws = [r[1] for r in res]
                n *= 2
            ts = [t + _dot3_shared([t], pw)[0] for t, pw in zip(ts, pws)]

        sts = [st_ref[pi] for pi in range(npairs)]
        fss = [_dot_nt(jnp.concatenate([p["ab"], p["rb"]], axis=0), st) for p, st in zip(ps, sts)]
        vss = [stack(p["v"]) for p in ps]
        rhss = [stack(fs[:c]) + _dot(p["a_ak"], vs) for p, fs, vs in zip(ps, fss, vss)]
        uss = [_dot(t, rhs) for t, rhs in zip(ts, rhss)]
        yss = [_dot(p["a_rb"], us) + _dot(p["a_rk"], vs) for p, us, vs in zip(ps, uss, vss)]
        for pi, (p, st, us) in enumerate(zip(ps, sts, uss)):
            u = us[:c] + us[c:]
            upd = _dot_tn(jnp.concatenate([u, p["v"]], axis=0),
                          jnp.concatenate([p["bh"], p["khat"]], axis=0))
            st_ref[pi] = st * jnp.exp(p["gl"]) + upd * hones_ref[...]
        inv_n = 1.0 / C_HEAD
        ys_ = [fs[c:] + ys[:c] + ys[c:] for fs, ys in zip(fss, yss)]
        sums = rowsums(ys_ + [p["r"] * p["kh"] * rk_ref[:, p["cols"]] for p in ps])
        dlts = [y - m * inv_n for y, m in zip(ys_, sums[:npairs])]
        vars_ = rowsums([d * d for d in dlts])
        for p, dlt, var, bsum in zip(ps, dlts, vars_, sums[npairs:]):
            cols = p["cols"]
            yn = dlt * lax.rsqrt(var * inv_n + GN_EPS) * lg_ref[:, cols] + lb_ref[:, cols]
            y_ref[rows, cols] = ((yn + bsum * p["v"]) * g_ref[rows, cols]).astype(y_ref.dtype)
        return carry

    lax.fori_loop(0, nchunks, body, 0)

    @pl.when(l_idx == pl.num_programs(2) - 1)
    def _():
        for pi in range(npairs):
            st = st_ref[pi]
            s_ref[0, 2 * pi] = st[:C_HEAD, :C_HEAD]
            s_ref[0, 2 * pi + 1] = st[C_HEAD:, C_HEAD:]


def rwkv_core(rkv, mid, low2, kk_p, ka_p, rk_p, lnx_g, lnx_b, batch, seqlen, layer,
              s0=None, v_first=None, npairs=8):
    _, m, d = rkv.shape
    c = math.gcd(seqlen, RWKV_CHUNK)
    lblk = _pick_tile(seqlen, (256, 128, 64, 32, 16, 8))
    nl_blocks = seqlen // lblk
    has_state = s0 is not None
    has_vres = v_first is not None
    w = npairs * LANES
    ngroups = d // w
    midw = mid.shape[1]
    strict, incl, eye, tril, hones = (jnp.asarray(x) for x in _rwkv_consts(c))
    row = lambda b, p, l: b * nl_blocks + l
    blk3 = lambda which: pl.BlockSpec((1, lblk, w), lambda b, p, l: (which, row(b, p, l), p))
    blk2 = pl.BlockSpec((lblk, w), lambda b, p, l: (row(b, p, l), p))
    par = pl.BlockSpec((1, w), lambda b, p, l: (0, p))
    cols = lambda arr: pl.BlockSpec((arr.shape[0], w), lambda b, p, l: (0, p))
    full = lambda arr: pl.BlockSpec(arr.shape, lambda b, p, l: (0,) * arr.ndim)
    w0, w2, a0, a2, g2 = low2[:5]
    in_specs = [blk3(0), blk3(1), blk3(2), pl.BlockSpec((lblk, midw), lambda b, p, l: (row(b, p, l), 0)),
                par, cols(w2), par, cols(a2), cols(g2)]
    args = [rkv, rkv, rkv, mid, w0.reshape(1, d), w2, a0.reshape(1, d), a2, g2]
    if has_vres:
        v0, v2 = low2[5:]
        in_specs += [blk3(2), par, cols(v2)]
        args += [v_first, v0.reshape(1, d), v2]
    in_specs += [par] * 5
    args += [x.reshape(1, d) for x in (kk_p, ka_p, rk_p, lnx_g, lnx_b)]
    in_specs += [full(x) for x in (strict, incl, eye, tril, hones)]
    args += [strict, incl, eye, tril, hones]
    st_blk = pl.BlockSpec((1, 2 * npairs, C_HEAD, C_HEAD), lambda b, p, l: (b, p, 0, 0))
    if has_state:
        in_specs.append(pl.BlockSpec((None, 1, 2 * npairs, C_HEAD, C_HEAD),
                                     lambda b, p, l: (layer, b, p, 0, 0)))
        args.append(s0)
    kern = functools.partial(_rwkv_core_kernel, chunk=c, nchunks=lblk // c, npairs=npairs,
                             has_state=has_state, has_vres=has_vres)
    return pl.pallas_call(
        kern,
        grid=(batch, ngroups, nl_blocks),
        in_specs=in_specs,
        out_specs=[blk2, st_blk],
        out_shape=[jax.ShapeDtypeStruct((m, d), F32),
                   jax.ShapeDtypeStruct((batch, C_HEADS, C_HEAD, C_HEAD), F32)],
        scratch_shapes=[pltpu.VMEM((npairs, LANES, LANES), F32)]
        + [pltpu.VMEM((lblk, w), F32)] * (4 if has_vres else 3),
        compiler_params=_cparams(("parallel", "parallel", "arbitrary")),
        name="rwkv_core",
    )(*args)


def _even_layer(x, batch, seqlen, e, layer, p, st_hgrn, k_cache, v_cache):
    proj = norm_matmul(x, p["norm_mix_pre"][layer], p["w_in_even"], e)
    o_a, s_new = hgrn(proj, p["hgrn_lb_raw"], e, batch, seqlen, st_hgrn)
    k_lo = IN_A + B_WIDTH
    new_rows = min(seqlen, WINDOW)
    p3 = proj.reshape(batch, seqlen, IN_EVEN)[:, seqlen - new_rows:]
    kb = p3[:, :, k_lo:k_lo + B_KV_WIDTH].reshape(batch, new_rows, B_KV_HEADS, B_HEAD_DIM)
    vb = p3[:, :, k_lo + B_KV_WIDTH:].reshape(batch, new_rows, B_KV_HEADS, B_HEAD_DIM)
    o_b = swa(proj, batch, seqlen, p["rel_bias"], p["attn_sinks"][e], e, k_cache, v_cache)
    if k_cache is None:
        k_new, v_new = kb, vb
    else:
        k_new = jnp.concatenate([k_cache[e, :, new_rows:], kb], axis=1)
        v_new = jnp.concatenate([v_cache[e, :, new_rows:], vb], axis=1)
    x = even_out(o_a, proj, o_b, x, p["hgrn_norm_g"][e], p["w_out_even"], p["norm_mix_post"][layer], e)
    return x, s_new, k_new, v_new


def _odd_layer(x, batch, seqlen, o, layer, p, shift0, s0, v_first):
    m, d = x.shape
    g_pre = p["norm_mix_pre"][layer]
    has_vres = o > 0
    width = lambda rng: rng[1] - rng[0]
    w1p, w2p = _pad_lora(p["rw_w1"][o], p["rw_w2"][o], width(LORA_W))
    a1p, a2p = _pad_lora(p["rw_a1"][o], p["rw_a2"][o], width(LORA_A))
    g1p, g2p = _pad_lora(p["rw_g1"][o], p["rw_g2"][o], width(LORA_G))
    first, low2 = [w1p, a1p, g1p], [p["rw_w0"][o], w2p, p["rw_a0"][o], a2p, g2p]
    if has_vres:
        v1p, v2p = _pad_lora(p["rw_v1"][o - 1], p["rw_v2"][o - 1], width(LORA_V))
        first.append(v1p)
        low2 += [p["rw_v0"][o - 1], v2p]
    rkv, mid = rwkv_in(x, g_pre, shift0, batch, seqlen, p["rw_mu"][o], p["w_rkv"], o,
                       jnp.concatenate(first, axis=1), has_vres)
    yg, s_new = rwkv_core(rkv, mid, low2, p["rw_kk"][o], p["rw_ka"][o], p["rw_rk"][o],
                          p["rw_lnx_g"][o], p["rw_lnx_b"][o], batch, seqlen, o, s0,
                          v_first if has_vres else None)
    shift_new = rmsnorm_rows(x.reshape(batch, seqlen, d)[:, -1], g_pre)
    x = odd_out(yg, x, p["rw_wo"], p["norm_mix_post"][layer], o)
    return x, s_new, shift_new, rkv


def _trunk(x3, st_hgrn, k_cache, v_cache, st_rwkv, st_shift, p):
    batch, seqlen, d = x3.shape
    x = x3.reshape(batch * seqlen, d)
    has_state = st_hgrn is not None
    hgrn_out, k_out, v_out, rwkv_out, shift_out = [], [], [], [], []
    v_first = None
    for layer in range(DEPTH):
        if layer % 2 == 0:
            e = layer // 2
            x, s_new, k_new, v_new = _even_layer(x, batch, seqlen, e, layer, p, st_hgrn, k_cache, v_cache)
            hgrn_out.append(s_new)
            k_out.append(k_new)
            v_out.append(v_new)
        else:
            o = layer // 2
            x, s_new, sh_new, rkv = _odd_layer(
                x, batch, seqlen, o, layer, p,
                st_shift[o] if has_state else None,
                st_rwkv,
                v_first)
            if o == 0:
                v_first = rkv
            rwkv_out.append(s_new)
            shift_out.append(sh_new)
        x = ffn(x, p["norm_ffn_pre"][layer], p["w_up"], p["w_down"], p["norm_ffn_post"][layer], layer)
    return (x.reshape(batch, seqlen, d), jnp.stack(hgrn_out), jnp.stack(k_out), jnp.stack(v_out),
            jnp.stack(rwkv_out), jnp.stack(shift_out))


def kernel(x_prompt, x_sample, state_hgrn, cache_swa_k, cache_swa_v, state_rwkv, state_shift,
           norm_mix_pre, norm_mix_post, norm_ffn_pre, norm_ffn_post,
           w_in_even, w_out_even, hgrn_lb_raw, hgrn_norm_g, rel_bias, attn_sinks,
           rw_mu, rw_wr, rw_wk, rw_wv, rw_wo, rw_w0, rw_w1, rw_w2, rw_a0, rw_a1, rw_a2,
           rw_v0, rw_v1, rw_v2, rw_g1, rw_g2, rw_kk, rw_ka, rw_rk, rw_lnx_g, rw_lnx_b,
           w_up, w_down):
    p = {
        "norm_mix_pre": norm_mix_pre, "norm_mix_post": norm_mix_post,
        "norm_ffn_pre": norm_ffn_pre, "norm_ffn_post": norm_ffn_post,
        "w_in_even": w_in_even.astype(BF16), "w_out_even": w_out_even.astype(BF16),
        "hgrn_lb_raw": hgrn_lb_raw, "hgrn_norm_g": hgrn_norm_g,
        "rel_bias": rel_bias, "attn_sinks": attn_sinks,
        "rw_mu": rw_mu, "w_rkv": jnp.stack([rw_wr, rw_wk, rw_wv], axis=1).astype(BF16),
        "rw_wo": rw_wo.astype(BF16),
        "rw_w0": rw_w0, "rw_w1": rw_w1, "rw_w2": rw_w2, "rw_a0": rw_a0, "rw_a1": rw_a1, "rw_a2": rw_a2,
        "rw_v0": rw_v0, "rw_v1": rw_v1, "rw_v2": rw_v2, "rw_g1": rw_g1, "rw_g2": rw_g2,
        "rw_kk": rw_kk, "rw_ka": rw_ka, "rw_rk": rw_rk, "rw_lnx_g": rw_lnx_g, "rw_lnx_b": rw_lnx_b,
        "w_up": w_up.astype(BF16), "w_down": w_down.astype(BF16),
    }
    y_p, hgrn_p, k_p, v_p, rwkv_p, shift_p = _trunk(x_prompt, None, None, None, None, None, p)
    y_s, hgrn_s, k_s, v_s, rwkv_s, shift_s = _trunk(
        x_sample, state_hgrn, cache_swa_k, cache_swa_v, state_rwkv, state_shift, p)
    return (y_p, y_s, hgrn_p, hgrn_s, k_p, k_s, v_p, v_s, rwkv_p, rwkv_s, shift_p, shift_s)
```

```python
import functools
import math

import numpy as np
import jax
import jax.numpy as jnp
from jax import lax
from jax.experimental import pallas as pl
from jax.experimental.pallas import tpu as pltpu

F32 = jnp.float32
BF16 = jnp.bfloat16

D_MODEL = 2048
DEPTH = 4
N_EVEN = 2
N_ODD = 2
A_HEADS = 8
A_KDIM = 128
A_VDIM = 128
A_WIDTH = 1024
A_QK = 1024
B_HEADS = 16
B_HEAD_DIM = 64
B_KV_HEADS = 4
B_GROUP = 4
B_WIDTH = 1024
B_KV_WIDTH = 256
WINDOW = 128
N_BUCKETS = 32
MAX_DISTANCE = 128
MASK_VALUE = -1e30
IN_A = 4096
IN_EVEN = 5632
C_HEAD = 64
C_HEADS = 32
GN_EPS = 64e-5
D_FF = 8192
NORM_EPS = 1e-6

LANES = 128
VMEM_LIMIT = 56 * 1024 * 1024

HGRN_CHUNK = 128
RWKV_CHUNK = 64


def _cparams(sem):
    return pltpu.CompilerParams(dimension_semantics=sem, vmem_limit_bytes=VMEM_LIMIT)


def _rms(x, g):
    return x * lax.rsqrt(jnp.mean(x * x, axis=-1, keepdims=True) + NORM_EPS) * g


def _dot(a, b):
    return jnp.dot(a.astype(BF16), b.astype(BF16), preferred_element_type=F32)


def _dot_nt(a, b):
    return lax.dot_general(a.astype(BF16), b.astype(BF16), (((1,), (1,)), ((), ())),
                           preferred_element_type=F32)


def _dot_tn(a, b):
    return lax.dot_general(a.astype(BF16), b.astype(BF16), (((0,), (0,)), ((), ())),
                           preferred_element_type=F32)


def _split2(x):
    hi = x.astype(BF16)
    return hi, (x - hi.astype(F32)).astype(BF16)


def _dot3_shared(lhs_list, b):
    bh, bl = _split2(b)
    parts = [_split2(a) for a in lhs_list]
    his = [p[0] for p in parts]
    los = [p[1] for p in parts]
    by_hi = jnp.dot(jnp.concatenate(his + los, axis=0), bh, preferred_element_type=F32)
    by_lo = jnp.dot(jnp.concatenate(his, axis=0), bl, preferred_element_type=F32)
    n = sum(a.shape[0] for a in lhs_list)
    out, off = [], 0
    for a in lhs_list:
        m = a.shape[0]
        out.append(by_hi[off:off + m] + by_hi[n + off:n + off + m] + by_lo[off:off + m])
        off += m
    return out


def _pick_tile(m, cands):
    for c in cands:
        if m % c == 0:
            return c
    return m


def _norm_matmul_kernel(x_ref, g_ref, w_ref, o_ref, xn_ref):
    @pl.when(pl.program_id(1) == 0)
    def _():
        xn_ref[...] = _rms(x_ref[...], g_ref[...]).astype(BF16)

    o_ref[...] = jnp.dot(xn_ref[...], w_ref[...], preferred_element_type=F32)


def norm_matmul(x, g, w_stack_bf16, li, tn=1408):
    m, d = x.shape
    n = w_stack_bf16.shape[2]
    tm = _pick_tile(m, (1024, 512, 256, 128, 64, 32, 16, 8))
    return pl.pallas_call(
        _norm_matmul_kernel,
        grid=(m // tm, n // tn),
        in_specs=[pl.BlockSpec((tm, d), lambda i, j: (i, 0)),
                  pl.BlockSpec((1, d), lambda i, j: (0, 0)),
                  pl.BlockSpec((None, d, tn), lambda i, j: (li, 0, j))],
        out_specs=pl.BlockSpec((tm, tn), lambda i, j: (i, j)),
        out_shape=jax.ShapeDtypeStruct((m, n), F32),
        scratch_shapes=[pltpu.VMEM((tm, d), BF16)],
        compiler_params=_cparams(("parallel", "arbitrary")),
        name="norm_matmul",
    )(x, g.reshape(1, d), w_stack_bf16)


def _ffn_kernel(x_ref, gpre_ref, wup_ref, wdn_ref, gpost_ref, o_ref, xn_ref, acc_ref):
    f = pl.program_id(1)

    @pl.when(f == 0)
    def _():
        xn_ref[...] = _rms(x_ref[...], gpre_ref[...]).astype(BF16)
        acc_ref[...] = jnp.zeros_like(acc_ref)

    h = jnp.dot(xn_ref[...], wup_ref[...], preferred_element_type=F32)
    h = jnp.square(jnp.maximum(h, 0.0)).astype(BF16)
    acc_ref[...] += jnp.dot(h, wdn_ref[...], preferred_element_type=F32)

    @pl.when(f == pl.num_programs(1) - 1)
    def _():
        o_ref[...] = x_ref[...] + _rms(acc_ref[...], gpost_ref[...])


def ffn(x, gpre, wup_bf16, wdn_bf16, gpost, li, tf=1024):
    m, d = x.shape
    dff = wup_bf16.shape[2]
    tm = _pick_tile(m, (512, 256, 128, 64, 32, 16, 8))
    return pl.pallas_call(
        _ffn_kernel,
        grid=(m // tm, dff // tf),
        in_specs=[pl.BlockSpec((tm, d), lambda i, f: (i, 0)),
                  pl.BlockSpec((1, d), lambda i, f: (0, 0)),
                  pl.BlockSpec((None, d, tf), lambda i, f: (li, 0, f)),
                  pl.BlockSpec((None, tf, d), lambda i, f: (li, f, 0)),
                  pl.BlockSpec((1, d), lambda i, f: (0, 0))],
        out_specs=pl.BlockSpec((tm, d), lambda i, f: (i, 0)),
        out_shape=jax.ShapeDtypeStruct((m, d), F32),
        scratch_shapes=[pltpu.VMEM((tm, d), BF16), pltpu.VMEM((tm, d), F32)],
        compiler_params=_cparams(("parallel", "arbitrary")),
        name="ffn",
    )(x, gpre.reshape(1, d), wup_bf16, wdn_bf16, gpost.reshape(1, d))


def _even_out_kernel(oa_ref, ga_ref, ob_ref, x_ref, ag_ref, w_ref, gpost_ref, o_ref):
    ga = ga_ref[...]
    oan = _rms(oa_ref[...], ag_ref[...]) * (ga * jax.nn.sigmoid(ga))
    mix = (jnp.dot(oan.astype(BF16), w_ref[:A_WIDTH, :], preferred_element_type=F32)
           + jnp.dot(ob_ref[...].astype(BF16), w_ref[A_WIDTH:, :], preferred_element_type=F32))
    o_ref[...] = x_ref[...] + _rms(mix, gpost_ref[...])


def even_out(o_a, proj, o_b, x, a_norm_g, w_out_bf16, gpost, li):
    m, d = x.shape
    tm = _pick_tile(m, (384, 256, 128, 64, 32, 16, 8))
    ga_blk = (3 * A_WIDTH) // A_WIDTH
    return pl.pallas_call(
        _even_out_kernel,
        grid=(m // tm,),
        in_specs=[pl.BlockSpec((tm, A_WIDTH), lambda i: (i, 0)),
                  pl.BlockSpec((tm, A_WIDTH), lambda i: (i, ga_blk)),
                  pl.BlockSpec((tm, B_WIDTH), lambda i: (i, 0)),
                  pl.BlockSpec((tm, d), lambda i: (i, 0)),
                  pl.BlockSpec((1, A_WIDTH), lambda i: (0, 0)),
                  pl.BlockSpec((None, A_WIDTH + B_WIDTH, d), lambda i: (li, 0, 0)),
                  pl.BlockSpec((1, d), lambda i: (0, 0))],
        out_specs=pl.BlockSpec((tm, d), lambda i: (i, 0)),
        out_shape=jax.ShapeDtypeStruct((m, d), F32),
        compiler_params=_cparams(("parallel",)),
        name="even_out",
    )(o_a, proj, o_b, x, a_norm_g.reshape(1, A_WIDTH), w_out_bf16, gpost.reshape(1, d))


def _odd_out_kernel(y_ref, x_ref, w_ref, gpost_ref, o_ref):
    mix = jnp.dot(y_ref[...].astype(BF16), w_ref[...], preferred_element_type=F32)
    o_ref[...] = x_ref[...] + _rms(mix, gpost_ref[...])


def odd_out(yg, x, wo_bf16, gpost, li):
    m, d = x.shape
    tm = _pick_tile(m, (384, 256, 128, 64, 32, 16, 8))
    return pl.pallas_call(
        _odd_out_kernel,
        grid=(m // tm,),
        in_specs=[pl.BlockSpec((tm, d), lambda i: (i, 0)),
                  pl.BlockSpec((tm, d), lambda i: (i, 0)),
                  pl.BlockSpec((None, d, d), lambda i: (li, 0, 0)),
                  pl.BlockSpec((1, d), lambda i: (0, 0))],
        out_specs=pl.BlockSpec((tm, d), lambda i: (i, 0)),
        out_shape=jax.ShapeDtypeStruct((m, d), F32),
        compiler_params=_cparams(("parallel",)),
        name="odd_out",
    )(yg, x, wo_bf16, gpost.reshape(1, d))


def _level_consts(c):
    levels = []
    s = c // 2
    while s >= 1:
        levels.append(s)
        s //= 2
    mask = np.zeros((len(levels), c, c), np.float32)
    idx = np.arange(c)
    for l, s in enumerate(levels):
        same = (idx[:, None] // (2 * s)) == (idx[None, :] // (2 * s))
        upper = (idx[:, None] % (2 * s)) >= s
        lower = (idx[None, :] % (2 * s)) < s
        mask[l] = (same & upper & lower).astype(np.float32)
    return levels, mask


def _split_rows(g, s, rowid):
    c = g.shape[0]
    if 2 * s >= 8:
        return jnp.concatenate(
            [jnp.broadcast_to(g[b + s - 1:b + s, :], (2 * s, g.shape[1])) for b in range(0, c, 2 * s)], axis=0)
    r = rowid % (2 * s)
    out = g
    for off in range(-(s - 1), s + 1):
        if off != 0:
            out = jnp.where(r == s - 1 + off, pltpu.roll(g, off % c, 0), out)
    return out


def _hgrn_kernel(*refs, layer, chunk, nchunks, levels, has_state, nheads):
    if has_state:
        q_ref, f_ref, i_ref, lb_ref, mask_ref, s0_ref, o_ref, s_ref, st_ref = refs
    else:
        q_ref, f_ref, i_ref, lb_ref, mask_ref, o_ref, s_ref, st_ref = refs
    c = chunk
    nh = nheads
    rowid = lax.broadcasted_iota(jnp.int32, (c, nh * LANES), 0)
    l_idx = pl.program_id(2)

    @pl.when(l_idx == 0)
    def _():
        for hi in range(nh):
            if has_state:
                st_ref[hi] = s0_ref[0, hi].T
            else:
                st_ref[hi] = jnp.zeros((A_VDIM, A_KDIM), F32)

    lbr = lb_ref[...]
    e = jnp.exp(lbr - jnp.max(lbr, axis=0, keepdims=True))
    p = e / jnp.sum(e, axis=0, keepdims=True)
    lb = jnp.zeros((1, nh * LANES), F32)
    for i in range(1, layer + 1):
        lb = lb + p[i:i + 1, :]
    one_m_lb = 1.0 - lb
    head = lambda x, hi: x[:, hi * LANES:(hi + 1) * LANES]

    def body(ci, carry):
        rows = pl.ds(pl.multiple_of(ci * c, c), c)
        fq = f_ref[rows, :]
        qr = q_ref[rows, :]
        v = i_ref[rows, :]
        logf = jnp.log(lb + one_m_lb * jax.nn.sigmoid(fq))
        k = one_m_lb * jax.nn.sigmoid(-fq)
        q = qr * jax.nn.sigmoid(qr) * (A_KDIM ** -0.5)

        g = logf
        sft = 1
        while sft < c:
            g = g + jnp.where(rowid >= sft, pltpu.roll(g, sft, 0), 0.0)
            sft *= 2
        glast = g[c - 1:c, :]
        q_in = q * jnp.exp(g)
        kd = k * jnp.exp(glast - g)
        dec = jnp.exp(glast)
        diag = q * k

        sts = [st_ref[hi] for hi in range(nh)]
        os_ = [_dot_nt(head(q_in, hi), sts[hi]) for hi in range(nh)]
        attns = [jnp.zeros((c, c), F32) for _ in range(nh)]
        for l, s in enumerate(levels):
            gr = _split_rows(g, s, rowid)
            qs = q * jnp.exp(jnp.minimum(g - gr, 0.0))
            ks = k * jnp.exp(jnp.minimum(gr - g, 0.0))
            ml = mask_ref[l]
            attns = [at + ml * _dot_nt(head(qs, hi), head(ks, hi)) for hi, at in enumerate(attns)]
        for hi in range(nh):
            vh = head(v, hi)
            o = os_[hi] + _dot(attns[hi], vh) + jnp.sum(head(diag, hi), axis=-1, keepdims=True) * vh
            o_ref[rows, hi * LANES:(hi + 1) * LANES] = o
        for hi in range(nh):
            st_ref[hi] = sts[hi] * head(dec, hi) + _dot_tn(head(v, hi), head(kd, hi))
        return carry

    lax.fori_loop(0, nchunks, body, 0, unroll=2 if nchunks % 2 == 0 else 1)

    @pl.when(l_idx == pl.num_programs(2) - 1)
    def _():
        for hi in range(nh):
            s_ref[0, hi] = st_ref[hi].T


def hgrn(proj, lb_raw, layer, batch, seqlen, s0=None, nheads=4):
    m = proj.shape[0]
    c = math.gcd(seqlen, HGRN_CHUNK)
    lblk = _pick_tile(seqlen, (512, 256, 128, 64, 32, 16, 8))
    nl_blocks = seqlen // lblk
    levels, mask = _level_consts(c)
    has_state = s0 is not None
    w = nheads * LANES
    ngroups = A_HEADS // nheads
    kern = functools.partial(_hgrn_kernel, layer=layer, chunk=c, nchunks=lblk // c,
                             levels=tuple(levels), has_state=has_state, nheads=nheads)
    row = lambda b, h, l: b * nl_blocks + l
    in_specs = [pl.BlockSpec((lblk, w), lambda b, h, l: (row(b, h, l), h)),
                pl.BlockSpec((lblk, w), lambda b, h, l: (row(b, h, l), ngroups + h)),
                pl.BlockSpec((lblk, w), lambda b, h, l: (row(b, h, l), 2 * ngroups + h)),
                pl.BlockSpec((N_EVEN, w), lambda b, h, l: (0, h)),
                pl.BlockSpec((len(levels), c, c), lambda b, h, l: (0, 0, 0))]
    args = [proj, proj, proj, lb_raw, jnp.asarray(mask)]
    st_blk = pl.BlockSpec((1, nheads, A_KDIM, A_VDIM), lambda b, h, l: (b, h, 0, 0))
    if has_state:
        in_specs.append(pl.BlockSpec((None, 1, nheads, A_KDIM, A_VDIM), lambda b, h, l: (layer, b, h, 0, 0)))
        args.append(s0)
    return pl.pallas_call(
        kern,
        grid=(batch, ngroups, nl_blocks),
        in_specs=in_specs,
        out_specs=[pl.BlockSpec((lblk, w), lambda b, h, l: (row(b, h, l), h)), st_blk],
        out_shape=[jax.ShapeDtypeStruct((m, A_WIDTH), F32),
                   jax.ShapeDtypeStruct((batch, A_HEADS, A_KDIM, A_VDIM), F32)],
        scratch_shapes=[pltpu.VMEM((nheads, A_VDIM, A_KDIM), F32)],
        compiler_params=_cparams(("parallel", "parallel", "arbitrary")),
        name="hgrn",
    )(*args)


def _t5_bucket(dist):
    max_exact = N_BUCKETS // 2
    d = np.maximum(dist, 0)
    large = max_exact + (np.log(np.maximum(d, max_exact).astype(np.float32) / max_exact)
                         / math.log(MAX_DISTANCE / max_exact) * (N_BUCKETS - max_exact)).astype(np.int32)
    large = np.minimum(large, N_BUCKETS - 1)
    return np.where(d < max_exact, d, large).astype(np.int32)


def _swa_kernel(q_ref, kp_ref, kc_ref, vp_ref, vc_ref, bucket_ref, band_ref, rb_ref, sink_ref,
                o_ref, bias_ref, *, qb, span, prev_always_valid):
    first = (pl.program_id(0) == 0) & (pl.program_id(1) == 0)

    @pl.when(first)
    def _():
        bk = bucket_ref[...]
        band = band_ref[...]

        def per_head(h, carry):
            def per_bucket(bi, acc):
                return jnp.where(bk == bi, rb_ref[bi, h], acc)
            acc = lax.fori_loop(0, N_BUCKETS, per_bucket, jnp.zeros((qb, span), F32))
            bias_ref[h] = jnp.where(band > 0, acc, MASK_VALUE)
            return carry

        lax.fori_loop(0, B_HEADS, per_head, 0)

    scale = B_HEAD_DIM ** -0.5
    q = q_ref[...]
    kall = jnp.concatenate([kp_ref[...], kc_ref[...]], axis=0)
    vall = jnp.concatenate([vp_ref[...], vc_ref[...]], axis=0)
    if not prev_always_valid:
        col = lax.broadcasted_iota(jnp.int32, (qb, span), 1)
        no_prev = (col < WINDOW) & (pl.program_id(1) == 0)
    heads = range(B_HEADS)
    ks = [kall[:, kh * B_HEAD_DIM:(kh + 1) * B_HEAD_DIM].astype(BF16) for kh in range(B_KV_HEADS)]
    vs = [vall[:, kh * B_HEAD_DIM:(kh + 1) * B_HEAD_DIM].astype(BF16) for kh in range(B_KV_HEADS)]
    qs = [(q[:, h * B_HEAD_DIM:(h + 1) * B_HEAD_DIM] * scale).astype(BF16) for h in heads]
    ss = [_dot_nt(qs[h], ks[h // B_GROUP]) + bias_ref[h] for h in heads]
    if not prev_always_valid:
        ss = [jnp.where(no_prev, MASK_VALUE, s) for s in ss]
    ms = [jnp.maximum(jnp.max(ss[h], axis=-1, keepdims=True), sink_ref[h]) for h in heads]
    ps = [jnp.exp(s - m) for s, m in zip(ss, ms)]
    denoms = [jnp.sum(ps[h], axis=-1, keepdims=True) + jnp.exp(sink_ref[h] - ms[h]) for h in heads]
    outs = [_dot(ps[h], vs[h // B_GROUP]) / denoms[h] for h in heads]
    o_ref[...] = jnp.concatenate(outs, axis=1)


def swa(proj, batch, seqlen, rel_bias, sinks, layer, k_past=None, v_past=None):
    m = proj.shape[0]
    has_cache = k_past is not None
    qb = math.gcd(seqlen, WINDOW)
    nb = seqlen // qb
    span = WINDOW + qb
    dist = np.arange(qb)[:, None] + WINDOW - np.arange(span)[None, :]
    band = ((dist >= 0) & (dist < WINDOW)).astype(np.float32)
    bucket = _t5_bucket(dist)
    q_col = IN_A // B_WIDTH
    k_col = (IN_A + B_WIDTH) // B_KV_WIDTH
    v_col = k_col + 1
    cur = lambda c: (lambda b, n: (b * nb + n, c))
    if has_cache:
        assert nb == 1
        prev_k = pl.BlockSpec((None, WINDOW, B_KV_WIDTH), lambda b, n: (layer, b, 0))
        prev_v = pl.BlockSpec((None, WINDOW, B_KV_WIDTH), lambda b, n: (layer, b, 0))
        kp_arr = k_past.reshape(k_past.shape[0], batch * WINDOW, B_KV_WIDTH)
        vp_arr = v_past.reshape(v_past.shape[0], batch * WINDOW, B_KV_WIDTH)
    else:
        assert qb == WINDOW
        prev = lambda c: (lambda b, n: (b * nb + jnp.maximum(n - 1, 0), c))
        prev_k = pl.BlockSpec((WINDOW, B_KV_WIDTH), prev(k_col))
        prev_v = pl.BlockSpec((WINDOW, B_KV_WIDTH), prev(v_col))
        kp_arr, vp_arr = proj, proj
    kern = functools.partial(_swa_kernel, qb=qb, span=span, prev_always_valid=has_cache)
    return pl.pallas_call(
        kern,
        grid=(batch, nb),
        in_specs=[pl.BlockSpec((qb, B_WIDTH), cur(q_col)),
                  prev_k,
                  pl.BlockSpec((qb, B_KV_WIDTH), cur(k_col)),
                  prev_v,
                  pl.BlockSpec((qb, B_KV_WIDTH), cur(v_col)),
                  pl.BlockSpec((qb, span), lambda b, n: (0, 0)),
                  pl.BlockSpec((qb, span), lambda b, n: (0, 0)),
                  pl.BlockSpec(memory_space=pltpu.SMEM),
                  pl.BlockSpec(memory_space=pltpu.SMEM)],
        out_specs=pl.BlockSpec((qb, B_WIDTH), lambda b, n: (b * nb + n, 0)),
        out_shape=jax.ShapeDtypeStruct((m, B_WIDTH), F32),
        scratch_shapes=[pltpu.VMEM((B_HEADS, qb, span), F32)],
        compiler_params=_cparams(("arbitrary", "arbitrary")),
        name="swa",
    )(proj, kp_arr, proj, vp_arr, proj, jnp.asarray(bucket), jnp.asarray(band), rel_bias, sinks)


def _rmsnorm_kernel(x_ref, g_ref, o_ref):
    o_ref[...] = _rms(x_ref[...], g_ref[...])


def rmsnorm_rows(x, g):
    m, d = x.shape
    tm = _pick_tile(m, (512, 256, 128, 64, 32, 16, 8))
    return pl.pallas_call(
        _rmsnorm_kernel,
        grid=(m // tm,),
        in_specs=[pl.BlockSpec((tm, d), lambda i: (i, 0)), pl.BlockSpec((1, d), lambda i: (0, 0))],
        out_specs=pl.BlockSpec((tm, d), lambda i: (i, 0)),
        out_shape=jax.ShapeDtypeStruct((m, d), F32),
        compiler_params=_cparams(("parallel",)),
        name="rmsnorm",
    )(x, g.reshape(1, d))


LORA_W = (0, 128)
LORA_A = (128, 256)
LORA_G = (256, 512)
LORA_V = (512, 640)


def _rwkv_in_kernel(x_ref, xp_ref, s_ref, g_ref, mu3_ref, mul_ref, w_ref, w1_ref,
                    rkv_ref, mid_ref, h_s, hp_s, *, tm, seqlen, has_vres):
    ph = pl.program_id(1)

    @pl.when(ph == 0)
    def _():
        g = g_ref[...]
        h = _rms(x_ref[...], g)
        rowid = lax.broadcasted_iota(jnp.int32, h.shape, 0)
        rolled = pltpu.roll(h, 1, 0)
        if seqlen % tm == 0:
            prev_last = _rms(xp_ref[...], g)[7:8, :]
            at_start = pl.program_id(0) % (seqlen // tm) == 0
            first = jnp.where(at_start, s_ref[...], prev_last)
            hp = jnp.where(rowid == 0, first, rolled)
        else:
            hp = jnp.where(rowid % seqlen == 0, s_ref[...], rolled)
        h_s[...] = h
        hp_s[...] = hp

    @pl.when(ph < 3)
    def _():
        h = h_s[...]
        xm = (h + (hp_s[...] - h) * mu3_ref[0]).astype(BF16)
        rkv_ref[0] = jnp.dot(xm, w_ref[...], preferred_element_type=F32)

    @pl.when(ph == 3)
    def _():
        h = h_s[...]
        xx = hp_s[...] - h
        mix = lambda i: (h + xx * mul_ref[i:i + 1, :]).astype(BF16)
        low = lambda i, rng: jnp.dot(mix(i), w1_ref[:, rng[0]:rng[1]], preferred_element_type=F32)
        parts = [jnp.tanh(low(0, LORA_W)), low(1, LORA_A), jax.nn.sigmoid(low(2, LORA_G))]
        if has_vres:
            parts.append(low(3, LORA_V))
        mid_ref[...] = jnp.concatenate(parts, axis=1).astype(BF16)


def rwkv_in(x, g, shift0, batch, seqlen, mu, w3_bf16, li, w1cat_bf16, has_vres):
    m, d = x.shape
    tm = next(t for t in (512, 256, 128, 64, 32, 16, 8)
              if m % t == 0 and (seqlen % t == 0 or t % seqlen == 0))
    midw = w1cat_bf16.shape[1]
    mu_rkv = jnp.stack([mu[0], mu[2], mu[3]])[:, None, :]
    mu_low = jnp.stack([mu[1], mu[4], mu[5], mu[3]])
    if shift0 is None:
        shift0 = jnp.zeros((batch, d), F32)
    if seqlen % tm == 0:
        srow = shift0[:, None, :]
        tps = seqlen // tm
        s_spec = pl.BlockSpec((None, 1, d), lambda i, p: (i // tps, 0, 0))
    else:
        srow = jnp.repeat(shift0, seqlen, axis=0)
        s_spec = pl.BlockSpec((tm, d), lambda i, p: (i, 0))
    sub = tm // 8
    kern = functools.partial(_rwkv_in_kernel, tm=tm, seqlen=seqlen, has_vres=has_vres)
    return pl.pallas_call(
        kern,
        grid=(m // tm, 4),
        in_specs=[pl.BlockSpec((tm, d), lambda i, p: (i, 0)),
                  pl.BlockSpec((8, d), lambda i, p: (jnp.maximum(i * sub - 1, 0), 0)),
                  s_spec,
                  pl.BlockSpec((1, d), lambda i, p: (0, 0)),
                  pl.BlockSpec((1, 1, d), lambda i, p: (jnp.minimum(p, 2), 0, 0)),
                  pl.BlockSpec((4, d), lambda i, p: (0, 0)),
                  pl.BlockSpec((None, None, d, d), lambda i, p: (li, jnp.minimum(p, 2), 0, 0)),
                  pl.BlockSpec((d, midw), lambda i, p: (0, 0))],
        out_specs=[pl.BlockSpec((1, tm, d), lambda i, p: (jnp.minimum(p, 2), i, 0)),
                   pl.BlockSpec((tm, midw), lambda i, p: (i, 0))],
        out_shape=[jax.ShapeDtypeStruct((3, m, d), F32),
                   jax.ShapeDtypeStruct((m, midw), BF16)],
        scratch_shapes=[pltpu.VMEM((tm, d), F32), pltpu.VMEM((tm, d), F32)],
        compiler_params=_cparams(("parallel", "arbitrary")),
        name="rwkv_in",
    )(x, x, srow, g.reshape(1, d), mu_rkv, mu_low, w3_bf16, w1cat_bf16)


def _softplus(z):
    return jnp.maximum(z, 0.0) + jnp.log(1.0 + jnp.exp(-jnp.abs(z)))


def _pad_lora(w1, w2, width):
    r = w1.shape[1]
    return (jnp.pad(w1, ((0, 0), (0, width - r))).astype(BF16),
            jnp.pad(w2, ((0, width - r), (0, 0))).astype(BF16))


def _rwkv_consts(c):
    i = np.arange(2 * c)
    same = (i[:, None] // c) == (i[None, :] // c)
    strict = (same & ((i[:, None] % c) > (i[None, :] % c))).astype(np.float32)
    incl = (same & ((i[:, None] % c) >= (i[None, :] % c))).astype(np.float32)
    eye = np.eye(2 * c, dtype=np.float32)
    l = np.arange(LANES)
    headones = ((l[:, None] // C_HEAD) == (l[None, :] // C_HEAD)).astype(np.float32)
    return strict, incl, eye, headones


def _rwkv_core_kernel(*refs, chunk, nchunks, npairs, has_state, has_vres):
    it = iter(refs)
    r_ref, k_ref, v_ref, mid_ref = (next(it) for _ in range(4))
    w0_ref, w2_ref, a0_ref, a2_ref, g2_ref = (next(it) for _ in range(5))
    if has_vres:
        vf_ref, v0_ref, v2_ref = next(it), next(it), next(it)
    kk_ref, ka_ref, rk_ref, lg_ref, lb_ref = (next(it) for _ in range(5))
    strict_ref, incl_ref, eye_ref, hones_ref = (next(it) for _ in range(4))
    if has_state:
        s0_ref = next(it)
    y_ref, s_ref, st_ref, wl_ref, a_ref, g_ref = (next(it) for _ in range(6))
    if has_vres:
        vg_ref = next(it)
    c = chunk
    l_idx = pl.program_id(2)
    lane = lax.broadcasted_iota(jnp.int32, (1, LANES), 1)
    m0 = (lane < C_HEAD).astype(F32)
    m1 = 1.0 - m0

    low = lambda rng, w_ref: jnp.dot(mid_ref[:, rng[0]:rng[1]], w_ref[...], preferred_element_type=F32)
    w_log = -_softplus(-(w0_ref[...] + low(LORA_W, w2_ref))) - 0.5
    wl_ref[...] = -jnp.exp(w_log)
    a_ref[...] = jax.nn.sigmoid(a0_ref[...] + low(LORA_A, a2_ref))
    g_ref[...] = low(LORA_G, g2_ref)
    if has_vres:
        vg_ref[...] = jax.nn.sigmoid(v0_ref[...] + low(LORA_V, v2_ref))

    @pl.when(l_idx == 0)
    def _():
        for pi in range(npairs):
            if has_state:
                z = jnp.zeros((C_HEAD, C_HEAD), F32)
                top = jnp.concatenate([s0_ref[0, 2 * pi], z], axis=1)
                bot = jnp.concatenate([z, s0_ref[0, 2 * pi + 1]], axis=1)
                st_ref[pi] = jnp.concatenate([top, bot], axis=0)
            else:
                st_ref[pi] = jnp.zeros((LANES, LANES), F32)

    strict = strict_ref[...]
    incl = incl_ref[...]
    eye = eye_ref[...]

    def stack(x):
        return jnp.concatenate([x * m0, x * m1], axis=0)

    first_head = lax.broadcasted_iota(jnp.int32, (c, LANES), 1) < C_HEAD
    rowid = lax.broadcasted_iota(jnp.int32, (c, LANES), 0)

    def rowsums(xs):
        return [jnp.where(first_head,
                          jnp.sum(x * m0, axis=-1, keepdims=True),
                          jnp.sum(x * m1, axis=-1, keepdims=True)) for x in xs]

    def cumsum_rows(x):
        s = 1
        while s < c:
            x = x + jnp.where(rowid >= s, pltpu.roll(x, s, 0), 0.0)
            s *= 2
        return x

    def load(pi, rows):
        cols = slice(pi * LANES, (pi + 1) * LANES)
        k = k_ref[0, rows, cols]
        v = v_ref[0, rows, cols]
        a = a_ref[rows, cols]
        if has_vres:
            v = v + (vf_ref[0, rows, cols] - v) * vg_ref[rows, cols]
        return dict(cols=cols, r=r_ref[0, rows, cols], v=v, a=a, wl=wl_ref[rows, cols],
                    kr=k * kk_ref[:, cols], kh=k * (1.0 + (a - 1.0) * ka_ref[:, cols]))

    def decays(p, ss):
        kk = p["kr"] * lax.rsqrt(jnp.maximum(ss, 1e-24))
        b = kk * p["a"]
        gc = cumsum_rows(p["wl"])
        gl = gc[c - 1:c, :]
        e_neg = jnp.exp(-gc)
        e_out = jnp.exp(gl - gc)
        p.update(gl=gl, ab=-kk * jnp.exp(gc - p["wl"]), rb=p["r"] * jnp.exp(gc),
                 bt=b * e_neg, kt=p["kh"] * e_neg, bh=b * e_out, khat=p["kh"] * e_out)

    def intra(p):
        lhs = jnp.concatenate([stack(p["ab"]), stack(p["rb"])], axis=0)
        with_b = _dot_nt(lhs, jnp.concatenate([p["bt"], p["bt"]], axis=0))
        with_k = _dot_nt(lhs, jnp.concatenate([p["kt"], p["kt"]], axis=0))
        p.update(a_ab=with_b[:2 * c] * strict, a_rb=with_b[2 * c:] * incl,
                 a_ak=with_k[:2 * c] * strict, a_rk=with_k[2 * c:] * incl)

    def body(ci, carry):
        rows = pl.ds(pl.multiple_of(ci * c, c), c)
        ps = [load(pi, rows) for pi in range(npairs)]
        for p, ss in zip(ps, rowsums([p["kr"] * p["kr"] for p in ps])):
            decays(p, ss)
        for p in ps:
            intra(p)

        ts = [eye + p["a_ab"] for p in ps]
        pws = [p["a_ab"] for p in ps]
        if c > 2:
            pws = [_dot3_shared([pw], pw)[0] for pw in pws]
            n = 2
            while 2 * n < c:
                res = [_dot3_shared([t, pw], pw) for t, pw in zip(ts, pws)]
                ts = [t + r[0] for t, r in zip(ts, res)]
                pws = [r[1] for r in res]
                n *= 2
            ts = [t + _dot3_shared([t], pw)[0] for t, pw in zip(ts, pws)]

        sts = [st_ref[pi] for pi in range(npairs)]
        fss = [_dot_nt(jnp.concatenate([p["ab"], p["rb"]], axis=0), st) for p, st in zip(ps, sts)]
        vss = [stack(p["v"]) for p in ps]
        rhss = [stack(fs[:c]) + _dot(p["a_ak"], vs) for p, fs, vs in zip(ps, fss, vss)]
        uss = [_dot(t, rhs) for t, rhs in zip(ts, rhss)]
        yss = [_dot(p["a_rb"], us) + _dot(p["a_rk"], vs) for p, us, vs in zip(ps, uss, vss)]
        for pi, (p, st, us) in enumerate(zip(ps, sts, uss)):
            u = us[:c] + us[c:]
            upd = _dot_tn(jnp.concatenate([u, p["v"]], axis=0),
                          jnp.concatenate([p["bh"], p["khat"]], axis=0))
            st_ref[pi] = st * jnp.exp(p["gl"]) + upd * hones_ref[...]
        inv_n = 1.0 / C_HEAD
        ys_ = [fs[c:] + ys[:c] + ys[c:] for fs, ys in zip(fss, yss)]
        sums = rowsums(ys_ + [p["r"] * p["kh"] * rk_ref[:, p["cols"]] for p in ps])
        dlts = [y - m * inv_n for y, m in zip(ys_, sums[:npairs])]
        vars_ = rowsums([d * d for d in dlts])
        for p, dlt, var, bsum in zip(ps, dlts, vars_, sums[npairs:]):
            cols = p["cols"]
            yn = dlt * lax.rsqrt(var * inv_n + GN_EPS) * lg_ref[:, cols] + lb_ref[:, cols]
            y_ref[rows, cols] = ((yn + bsum * p["v"]) * g_ref[rows, cols]).astype(y_ref.dtype)
        return carry

    lax.fori_loop(0, nchunks, body, 0)

    @pl.when(l_idx == pl.num_programs(2) - 1)
    def _():
        for pi in range(npairs):
            st = st_ref[pi]
            s_ref[0, 2 * pi] = st[:C_HEAD, :C_HEAD]
            s_ref[0, 2 * pi + 1] = st[C_HEAD:, C_HEAD:]


def rwkv_core(rkv, mid, low2, kk_p, ka_p, rk_p, lnx_g, lnx_b, batch, seqlen, layer,
              s0=None, v_first=None, npairs=8):
    _, m, d = rkv.shape
    c = math.gcd(seqlen, RWKV_CHUNK)
    lblk = _pick_tile(seqlen, (256, 128, 64, 32, 16, 8))
    nl_blocks = seqlen // lblk
    has_state = s0 is not None
    has_vres = v_first is not None
    w = npairs * LANES
    ngroups = d // w
    midw = mid.shape[1]
    strict, incl, eye, hones = (jnp.asarray(x) for x in _rwkv_consts(c))
    row = lambda b, p, l: b * nl_blocks + l
    blk3 = lambda which: pl.BlockSpec((1, lblk, w), lambda b, p, l: (which, row(b, p, l), p))
    blk2 = pl.BlockSpec((lblk, w), lambda b, p, l: (row(b, p, l), p))
    par = pl.BlockSpec((1, w), lambda b, p, l: (0, p))
    cols = lambda arr: pl.BlockSpec((arr.shape[0], w), lambda b, p, l: (0, p))
    full = lambda arr: pl.BlockSpec(arr.shape, lambda b, p, l: (0,) * arr.ndim)
    w0, w2, a0, a2, g2 = low2[:5]
    in_specs = [blk3(0), blk3(1), blk3(2), pl.BlockSpec((lblk, midw), lambda b, p, l: (row(b, p, l), 0)),
                par, cols(w2), par, cols(a2), cols(g2)]
    args = [rkv, rkv, rkv, mid, w0.reshape(1, d), w2, a0.reshape(1, d), a2, g2]
    if has_vres:
        v0, v2 = low2[5:]
        in_specs += [blk3(2), par, cols(v2)]
        args += [v_first, v0.reshape(1, d), v2]
    in_specs += [par] * 5
    args += [x.reshape(1, d) for x in (kk_p, ka_p, rk_p, lnx_g, lnx_b)]
    in_specs += [full(x) for x in (strict, incl, eye, hones)]
    args += [strict, incl, eye, hones]
    st_blk = pl.BlockSpec((1, 2 * npairs, C_HEAD, C_HEAD), lambda b, p, l: (b, p, 0, 0))
    if has_state:
        in_specs.append(pl.BlockSpec((None, 1, 2 * npairs, C_HEAD, C_HEAD),
                                     lambda b, p, l: (layer, b, p, 0, 0)))
        args.append(s0)
    kern = functools.partial(_rwkv_core_kernel, chunk=c, nchunks=lblk // c, npairs=npairs,
                             has_state=has_state, has_vres=has_vres)
    return pl.pallas_call(
        kern,
        grid=(batch, ngroups, nl_blocks),
        in_specs=in_specs,
        out_specs=[blk2, st_blk],
        out_shape=[jax.ShapeDtypeStruct((m, d), F32),
                   jax.ShapeDtypeStruct((batch, C_HEADS, C_HEAD, C_HEAD), F32)],
        scratch_shapes=[pltpu.VMEM((npairs, LANES, LANES), F32)]
        + [pltpu.VMEM((lblk, w), F32)] * (4 if has_vres else 3),
        compiler_params=_cparams(("parallel", "parallel", "arbitrary")),
        name="rwkv_core",
    )(*args)


def _even_layer(x, batch, seqlen, e, layer, p, st_hgrn, k_cache, v_cache):
    proj = norm_matmul(x, p["norm_mix_pre"][layer], p["w_in_even"], e)
    o_a, s_new = hgrn(proj, p["hgrn_lb_raw"], e, batch, seqlen, st_hgrn)
    k_lo = IN_A + B_WIDTH
    new_rows = min(seqlen, WINDOW)
    p3 = proj.reshape(batch, seqlen, IN_EVEN)[:, seqlen - new_rows:]
    kb = p3[:, :, k_lo:k_lo + B_KV_WIDTH].reshape(batch, new_rows, B_KV_HEADS, B_HEAD_DIM)
    vb = p3[:, :, k_lo + B_KV_WIDTH:].reshape(batch, new_rows, B_KV_HEADS, B_HEAD_DIM)
    o_b = swa(proj, batch, seqlen, p["rel_bias"], p["attn_sinks"][e], e, k_cache, v_cache)
    if k_cache is None:
        k_new, v_new = kb, vb
    else:
        k_new = jnp.concatenate([k_cache[e, :, new_rows:], kb], axis=1)
        v_new = jnp.concatenate([v_cache[e, :, new_rows:], vb], axis=1)
    x = even_out(o_a, proj, o_b, x, p["hgrn_norm_g"][e], p["w_out_even"], p["norm_mix_post"][layer], e)
    return x, s_new, k_new, v_new


def _odd_layer(x, batch, seqlen, o, layer, p, shift0, s0, v_first):
    m, d = x.shape
    g_pre = p["norm_mix_pre"][layer]
    has_vres = o > 0
    width = lambda rng: rng[1] - rng[0]
    w1p, w2p = _pad_lora(p["rw_w1"][o], p["rw_w2"][o], width(LORA_W))
    a1p, a2p = _pad_lora(p["rw_a1"][o], p["rw_a2"][o], width(LORA_A))
    g1p, g2p = _pad_lora(p["rw_g1"][o], p["rw_g2"][o], width(LORA_G))
    first, low2 = [w1p, a1p, g1p], [p["rw_w0"][o], w2p, p["rw_a0"][o], a2p, g2p]
    if has_vres:
        v1p, v2p = _pad_lora(p["rw_v1"][o - 1], p["rw_v2"][o - 1], width(LORA_V))
        first.append(v1p)
        low2 += [p["rw_v0"][o - 1], v2p]
    rkv, mid = rwkv_in(x, g_pre, shift0, batch, seqlen, p["rw_mu"][o], p["w_rkv"], o,
                       jnp.concatenate(first, axis=1), has_vres)
    yg, s_new = rwkv_core(rkv, mid, low2, p["rw_kk"][o], p["rw_ka"][o], p["rw_rk"][o],
                          p["rw_lnx_g"][o], p["rw_lnx_b"][o], batch, seqlen, o, s0,
                          v_first if has_vres else None)
    shift_new = rmsnorm_rows(x.reshape(batch, seqlen, d)[:, -1], g_pre)
    x = odd_out(yg, x, p["rw_wo"], p["norm_mix_post"][layer], o)
    return x, s_new, shift_new, rkv


def _trunk(x3, st_hgrn, k_cache, v_cache, st_rwkv, st_shift, p):
    batch, seqlen, d = x3.shape
    x = x3.reshape(batch * seqlen, d)
    has_state = st_hgrn is not None
    hgrn_out, k_out, v_out, rwkv_out, shift_out = [], [], [], [], []
    v_first = None
    for layer in range(DEPTH):
        if layer % 2 == 0:
            e = layer // 2
            x, s_new, k_new, v_new = _even_layer(x, batch, seqlen, e, layer, p, st_hgrn, k_cache, v_cache)
            hgrn_out.append(s_new)
            k_out.append(k_new)
            v_out.append(v_new)
        else:
            o = layer // 2
            x, s_new, sh_new, rkv = _odd_layer(
                x, batch, seqlen, o, layer, p,
                st_shift[o] if has_state else None,
                st_rwkv,
                v_first)
            if o == 0:
                v_first = rkv
            rwkv_out.append(s_new)
            shift_out.append(sh_new)
        x = ffn(x, p["norm_ffn_pre"][layer], p["w_up"], p["w_down"], p["norm_ffn_post"][layer], layer)
    return (x.reshape(batch, seqlen, d), jnp.stack(hgrn_out), jnp.stack(k_out), jnp.stack(v_out),
            jnp.stack(rwkv_out), jnp.stack(shift_out))


def kernel(x_prompt, x_sample, state_hgrn, cache_swa_k, cache_swa_v, state_rwkv, state_shift,
           norm_mix_pre, norm_mix_post, norm_ffn_pre, norm_ffn_post,
           w_in_even, w_out_even, hgrn_lb_raw, hgrn_norm_g, rel_bias, attn_sinks,
           rw_mu, rw_wr, rw_wk, rw_wv, rw_wo, rw_w0, rw_w1, rw_w2, rw_a0, rw_a1, rw_a2,
           rw_v0, rw_v1, rw_v2, rw_g1, rw_g2, rw_kk, rw_ka, rw_rk, rw_lnx_g, rw_lnx_b,
           w_up, w_down):
    p = {
        "norm_mix_pre": norm_mix_pre, "norm_mix_post": norm_mix_post,
        "norm_ffn_pre": norm_ffn_pre, "norm_ffn_post": norm_ffn_post,
        "w_in_even": w_in_even.astype(BF16), "w_out_even": w_out_even.astype(BF16),
        "hgrn_lb_raw": hgrn_lb_raw, "hgrn_norm_g": hgrn_norm_g,
        "rel_bias": rel_bias, "attn_sinks": attn_sinks,
        "rw_mu": rw_mu, "w_rkv": jnp.stack([rw_wr, rw_wk, rw_wv], axis=1).astype(BF16),
        "rw_wo": rw_wo.astype(BF16),
        "rw_w0": rw_w0, "rw_w1": rw_w1, "rw_w2": rw_w2, "rw_a0": rw_a0, "rw_a1": rw_a1, "rw_a2": rw_a2,
        "rw_v0": rw_v0, "rw_v1": rw_v1, "rw_v2": rw_v2, "rw_g1": rw_g1, "rw_g2": rw_g2,
        "rw_kk": rw_kk, "rw_ka": rw_ka, "rw_rk": rw_rk, "rw_lnx_g": rw_lnx_g, "rw_lnx_b": rw_lnx_b,
        "w_up": w_up.astype(BF16), "w_down": w_down.astype(BF16),
    }
    y_p, hgrn_p, k_p, v_p, rwkv_p, shift_p = _trunk(x_prompt, None, None, None, None, None, p)
    y_s, hgrn_s, k_s, v_s, rwkv_s, shift_s = _trunk(
        x_sample, state_hgrn, cache_swa_k, cache_swa_v, state_rwkv, state_shift, p)
    return (y_p, y_s, hgrn_p, hgrn_s, k_p, k_s, v_p, v_s, rwkv_p, rwkv_s, shift_p, shift_s)
```

```python
import functools
import math

import numpy as np
import jax
import jax.numpy as jnp
from jax import lax
from jax.experimental import pallas as pl
from jax.experimental.pallas import tpu as pltpu

F32 = jnp.float32
BF16 = jnp.bfloat16

D_MODEL = 2048
DEPTH = 4
N_EVEN = 2
N_ODD = 2
A_HEADS = 8
A_KDIM = 128
A_VDIM = 128
A_WIDTH = 1024
A_QK = 1024
B_HEADS = 16
B_HEAD_DIM = 64
B_KV_HEADS = 4
B_GROUP = 4
B_WIDTH = 1024
B_KV_WIDTH = 256
WINDOW = 128
N_BUCKETS = 32
MAX_DISTANCE = 128
MASK_VALUE = -1e30
IN_A = 4096
IN_EVEN = 5632
C_HEAD = 64
C_HEADS = 32
GN_EPS = 64e-5
D_FF = 8192
NORM_EPS = 1e-6

LANES = 128
VMEM_LIMIT = 56 * 1024 * 1024

HGRN_CHUNK = 128
RWKV_CHUNK = 64


def _cparams(sem):
    return pltpu.CompilerParams(dimension_semantics=sem, vmem_limit_bytes=VMEM_LIMIT)


def _rms(x, g):
    return x * lax.rsqrt(jnp.mean(x * x, axis=-1, keepdims=True) + NORM_EPS) * g


def _dot(a, b):
    return jnp.dot(a.astype(BF16), b.astype(BF16), preferred_element_type=F32)


def _dot_nt(a, b):
    return lax.dot_general(a.astype(BF16), b.astype(BF16), (((1,), (1,)), ((), ())),
                           preferred_element_type=F32)


def _dot_tn(a, b):
    return lax.dot_general(a.astype(BF16), b.astype(BF16), (((0,), (0,)), ((), ())),
                           preferred_element_type=F32)


def _split2(x):
    hi = x.astype(BF16)
    return hi, (x - hi.astype(F32)).astype(BF16)


def _dot3_shared(lhs_list, b):
    bh, bl = _split2(b)
    parts = [_split2(a) for a in lhs_list]
    his = [p[0] for p in parts]
    los = [p[1] for p in parts]
    by_hi = jnp.dot(jnp.concatenate(his + los, axis=0), bh, preferred_element_type=F32)
    by_lo = jnp.dot(jnp.concatenate(his, axis=0), bl, preferred_element_type=F32)
    n = sum(a.shape[0] for a in lhs_list)
    out, off = [], 0
    for a in lhs_list:
        m = a.shape[0]
        out.append(by_hi[off:off + m] + by_hi[n + off:n + off + m] + by_lo[off:off + m])
        off += m
    return out


def _pick_tile(m, cands):
    for c in cands:
        if m % c == 0:
            return c
    return m


def _norm_matmul_kernel(x_ref, g_ref, w_ref, o_ref, xn_ref):
    @pl.when(pl.program_id(1) == 0)
    def _():
        xn_ref[...] = _rms(x_ref[...], g_ref[...]).astype(BF16)

    o_ref[...] = jnp.dot(xn_ref[...], w_ref[...], preferred_element_type=F32)


def norm_matmul(x, g, w_stack_bf16, li, tn=1408):
    m, d = x.shape
    n = w_stack_bf16.shape[2]
    tm = _pick_tile(m, (1024, 512, 256, 128, 64, 32, 16, 8))
    return pl.pallas_call(
        _norm_matmul_kernel,
        grid=(m // tm, n // tn),
        in_specs=[pl.BlockSpec((tm, d), lambda i, j: (i, 0)),
                  pl.BlockSpec((1, d), lambda i, j: (0, 0)),
                  pl.BlockSpec((None, d, tn), lambda i, j: (li, 0, j))],
        out_specs=pl.BlockSpec((tm, tn), lambda i, j: (i, j)),
        out_shape=jax.ShapeDtypeStruct((m, n), F32),
        scratch_shapes=[pltpu.VMEM((tm, d), BF16)],
        compiler_params=_cparams(("parallel", "arbitrary")),
        name="norm_matmul",
    )(x, g.reshape(1, d), w_stack_bf16)


def _ffn_kernel(x_ref, gpre_ref, wup_ref, wdn_ref, gpost_ref, o_ref, xn_ref, acc_ref):
    f = pl.program_id(1)

    @pl.when(f == 0)
    def _():
        xn_ref[...] = _rms(x_ref[...], gpre_ref[...]).astype(BF16)
        acc_ref[...] = jnp.zeros_like(acc_ref)

    h = jnp.dot(xn_ref[...], wup_ref[...], preferred_element_type=F32)
    h = jnp.square(jnp.maximum(h, 0.0)).astype(BF16)
    acc_ref[...] += jnp.dot(h, wdn_ref[...], preferred_element_type=F32)

    @pl.when(f == pl.num_programs(1) - 1)
    def _():
        o_ref[...] = x_ref[...] + _rms(acc_ref[...], gpost_ref[...])


def ffn(x, gpre, wup_bf16, wdn_bf16, gpost, li, tf=1024):
    m, d = x.shape
    dff = wup_bf16.shape[2]
    tm = _pick_tile(m, (512, 256, 128, 64, 32, 16, 8))
    return pl.pallas_call(
        _ffn_kernel,
        grid=(m // tm, dff // tf),
        in_specs=[pl.BlockSpec((tm, d), lambda i, f: (i, 0)),
                  pl.BlockSpec((1, d), lambda i, f: (0, 0)),
                  pl.BlockSpec((None, d, tf), lambda i, f: (li, 0, f)),
                  pl.BlockSpec((None, tf, d), lambda i, f: (li, f, 0)),
                  pl.BlockSpec((1, d), lambda i, f: (0, 0))],
        out_specs=pl.BlockSpec((tm, d), lambda i, f: (i, 0)),
        out_shape=jax.ShapeDtypeStruct((m, d), F32),
        scratch_shapes=[pltpu.VMEM((tm, d), BF16), pltpu.VMEM((tm, d), F32)],
        compiler_params=_cparams(("parallel", "arbitrary")),
        name="ffn",
    )(x, gpre.reshape(1, d), wup_bf16, wdn_bf16, gpost.reshape(1, d))


def _even_out_kernel(oa_ref, ga_ref, ob_ref, x_ref, ag_ref, w_ref, gpost_ref, o_ref):
    ga = ga_ref[...]
    oan = _rms(oa_ref[...], ag_ref[...]) * (ga * jax.nn.sigmoid(ga))
    mix = (jnp.dot(oan.astype(BF16), w_ref[:A_WIDTH, :], preferred_element_type=F32)
           + jnp.dot(ob_ref[...].astype(BF16), w_ref[A_WIDTH:, :], preferred_element_type=F32))
    o_ref[...] = x_ref[...] + _rms(mix, gpost_ref[...])


def even_out(o_a, proj, o_b, x, a_norm_g, w_out_bf16, gpost, li):
    m, d = x.shape
    tm = _pick_tile(m, (384, 256, 128, 64, 32, 16, 8))
    ga_blk = (3 * A_WIDTH) // A_WIDTH
    return pl.pallas_call(
        _even_out_kernel,
        grid=(m // tm,),
        in_specs=[pl.BlockSpec((tm, A_WIDTH), lambda i: (i, 0)),
                  pl.BlockSpec((tm, A_WIDTH), lambda i: (i, ga_blk)),
                  pl.BlockSpec((tm, B_WIDTH), lambda i: (i, 0)),
                  pl.BlockSpec((tm, d), lambda i: (i, 0)),
                  pl.BlockSpec((1, A_WIDTH), lambda i: (0, 0)),
                  pl.BlockSpec((None, A_WIDTH + B_WIDTH, d), lambda i: (li, 0, 0)),
                  pl.BlockSpec((1, d), lambda i: (0, 0))],
        out_specs=pl.BlockSpec((tm, d), lambda i: (i, 0)),
        out_shape=jax.ShapeDtypeStruct((m, d), F32),
        compiler_params=_cparams(("parallel",)),
        name="even_out",
    )(o_a, proj, o_b, x, a_norm_g.reshape(1, A_WIDTH), w_out_bf16, gpost.reshape(1, d))


def _odd_out_kernel(y_ref, x_ref, w_ref, gpost_ref, o_ref):
    mix = jnp.dot(y_ref[...].astype(BF16), w_ref[...], preferred_element_type=F32)
    o_ref[...] = x_ref[...] + _rms(mix, gpost_ref[...])


def odd_out(yg, x, wo_bf16, gpost, li):
    m, d = x.shape
    tm = _pick_tile(m, (384, 256, 128, 64, 32, 16, 8))
    return pl.pallas_call(
        _odd_out_kernel,
        grid=(m // tm,),
        in_specs=[pl.BlockSpec((tm, d), lambda i: (i, 0)),
                  pl.BlockSpec((tm, d), lambda i: (i, 0)),
                  pl.BlockSpec((None, d, d), lambda i: (li, 0, 0)),
                  pl.BlockSpec((1, d), lambda i: (0, 0))],
        out_specs=pl.BlockSpec((tm, d), lambda i: (i, 0)),
        out_shape=jax.ShapeDtypeStruct((m, d), F32),
        compiler_params=_cparams(("parallel",)),
        name="odd_out",
    )(yg, x, wo_bf16, gpost.reshape(1, d))


def _level_consts(c):
    levels = []
    s = c // 2
    while s >= 1:
        levels.append(s)
        s //= 2
    mask = np.zeros((len(levels), c, c), np.float32)
    idx = np.arange(c)
    for l, s in enumerate(levels):
        same = (idx[:, None] // (2 * s)) == (idx[None, :] // (2 * s))
        upper = (idx[:, None] % (2 * s)) >= s
        lower = (idx[None, :] % (2 * s)) < s
        mask[l] = (same & upper & lower).astype(np.float32)
    return levels, mask


def _split_rows(g, s, rowid):
    c = g.shape[0]
    if 2 * s >= 8:
        return jnp.concatenate(
            [jnp.broadcast_to(g[b + s - 1:b + s, :], (2 * s, g.shape[1])) for b in range(0, c, 2 * s)], axis=0)
    r = rowid % (2 * s)
    out = g
    for off in range(-(s - 1), s + 1):
        if off != 0:
            out = jnp.where(r == s - 1 + off, pltpu.roll(g, off % c, 0), out)
    return out


def _hgrn_kernel(*refs, layer, chunk, nchunks, levels, has_state, nheads):
    if has_state:
        q_ref, f_ref, i_ref, lb_ref, mask_ref, s0_ref, o_ref, s_ref, st_ref = refs
    else:
        q_ref, f_ref, i_ref, lb_ref, mask_ref, o_ref, s_ref, st_ref = refs
    c = chunk
    nh = nheads
    rowid = lax.broadcasted_iota(jnp.int32, (c, nh * LANES), 0)
    l_idx = pl.program_id(2)

    @pl.when(l_idx == 0)
    def _():
        for hi in range(nh):
            if has_state:
                st_ref[hi] = s0_ref[0, hi].T
            else:
                st_ref[hi] = jnp.zeros((A_VDIM, A_KDIM), F32)

    lbr = lb_ref[...]
    e = jnp.exp(lbr - jnp.max(lbr, axis=0, keepdims=True))
    p = e / jnp.sum(e, axis=0, keepdims=True)
    lb = jnp.zeros((1, nh * LANES), F32)
    for i in range(1, layer + 1):
        lb = lb + p[i:i + 1, :]
    one_m_lb = 1.0 - lb
    head = lambda x, hi: x[:, hi * LANES:(hi + 1) * LANES]

    def body(ci, carry):
        rows = pl.ds(pl.multiple_of(ci * c, c), c)
        fq = f_ref[rows, :]
        qr = q_ref[rows, :]
        v = i_ref[rows, :]
        logf = jnp.log(lb + one_m_lb * jax.nn.sigmoid(fq))
        k = one_m_lb * jax.nn.sigmoid(-fq)
        q = qr * jax.nn.sigmoid(qr) * (A_KDIM ** -0.5)

        g = logf
        sft = 1
        while sft < c:
            g = g + jnp.where(rowid >= sft, pltpu.roll(g, sft, 0), 0.0)
            sft *= 2
        glast = g[c - 1:c, :]
        q_in = q * jnp.exp(g)
        kd = k * jnp.exp(glast - g)
        dec = jnp.exp(glast)
        diag = q * k

        sts = [st_ref[hi] for hi in range(nh)]
        os_ = [_dot_nt(head(q_in, hi), sts[hi]) for hi in range(nh)]
        attns = [jnp.zeros((c, c), F32) for _ in range(nh)]
        for l, s in enumerate(levels):
            gr = _split_rows(g, s, rowid)
            qs = q * jnp.exp(jnp.minimum(g - gr, 0.0))
            ks = k * jnp.exp(jnp.minimum(gr - g, 0.0))
            ml = mask_ref[l]
            attns = [at + ml * _dot_nt(head(qs, hi), head(ks, hi)) for hi, at in enumerate(attns)]
        for hi in range(nh):
            vh = head(v, hi)
            o = os_[hi] + _dot(attns[hi], vh) + jnp.sum(head(diag, hi), axis=-1, keepdims=True) * vh
            o_ref[rows, hi * LANES:(hi + 1) * LANES] = o
        for hi in range(nh):
            st_ref[hi] = sts[hi] * head(dec, hi) + _dot_tn(head(v, hi), head(kd, hi))
        return carry

    lax.fori_loop(0, nchunks, body, 0, unroll=2 if nchunks % 2 == 0 else 1)

    @pl.when(l_idx == pl.num_programs(2) - 1)
    def _():
        for hi in range(nh):
            s_ref[0, hi] = st_ref[hi].T


def hgrn(proj, lb_raw, layer, batch, seqlen, s0=None, nheads=4):
    m = proj.shape[0]
    c = math.gcd(seqlen, HGRN_CHUNK)
    lblk = _pick_tile(seqlen, (512, 256, 128, 64, 32, 16, 8))
    nl_blocks = seqlen // lblk
    levels, mask = _level_consts(c)
    has_state = s0 is not None
    if c * 4 <= HGRN_CHUNK:
        nheads = A_HEADS
    w = nheads * LANES
    ngroups = A_HEADS // nheads
    kern = functools.partial(_hgrn_kernel, layer=layer, chunk=c, nchunks=lblk // c,
                             levels=tuple(levels), has_state=has_state, nheads=nheads)
    row = lambda b, h, l: b * nl_blocks + l
    in_specs = [pl.BlockSpec((lblk, w), lambda b, h, l: (row(b, h, l), h)),
                pl.BlockSpec((lblk, w), lambda b, h, l: (row(b, h, l), ngroups + h)),
                pl.BlockSpec((lblk, w), lambda b, h, l: (row(b, h, l), 2 * ngroups + h)),
                pl.BlockSpec((N_EVEN, w), lambda b, h, l: (0, h)),
                pl.BlockSpec((len(levels), c, c), lambda b, h, l: (0, 0, 0))]
    args = [proj, proj, proj, lb_raw, jnp.asarray(mask)]
    st_blk = pl.BlockSpec((1, nheads, A_KDIM, A_VDIM), lambda b, h, l: (b, h, 0, 0))
    if has_state:
        in_specs.append(pl.BlockSpec((None, 1, nheads, A_KDIM, A_VDIM), lambda b, h, l: (layer, b, h, 0, 0)))
        args.append(s0)
    return pl.pallas_call(
        kern,
        grid=(batch, ngroups, nl_blocks),
        in_specs=in_specs,
        out_specs=[pl.BlockSpec((lblk, w), lambda b, h, l: (row(b, h, l), h)), st_blk],
        out_shape=[jax.ShapeDtypeStruct((m, A_WIDTH), F32),
                   jax.ShapeDtypeStruct((batch, A_HEADS, A_KDIM, A_VDIM), F32)],
        scratch_shapes=[pltpu.VMEM((nheads, A_VDIM, A_KDIM), F32)],
        compiler_params=_cparams(("parallel", "parallel", "arbitrary")),
        name="hgrn",
    )(*args)


def _t5_bucket(dist):
    max_exact = N_BUCKETS // 2
    d = np.maximum(dist, 0)
    large = max_exact + (np.log(np.maximum(d, max_exact).astype(np.float32) / max_exact)
                         / math.log(MAX_DISTANCE / max_exact) * (N_BUCKETS - max_exact)).astype(np.int32)
    large = np.minimum(large, N_BUCKETS - 1)
    return np.where(d < max_exact, d, large).astype(np.int32)


def _swa_kernel(q_ref, kp_ref, kc_ref, vp_ref, vc_ref, bucket_ref, band_ref, rb_ref, sink_ref,
                o_ref, bias_ref, *, qb, span, prev_always_valid):
    first = (pl.program_id(0) == 0) & (pl.program_id(1) == 0)

    @pl.when(first)
    def _():
        bk = bucket_ref[...]
        band = band_ref[...]

        def per_head(h, carry):
            def per_bucket(bi, acc):
                return jnp.where(bk == bi, rb_ref[bi, h], acc)
            acc = lax.fori_loop(0, N_BUCKETS, per_bucket, jnp.zeros((qb, span), F32))
            bias_ref[h] = jnp.where(band > 0, acc, MASK_VALUE)
            return carry

        lax.fori_loop(0, B_HEADS, per_head, 0)

    scale = B_HEAD_DIM ** -0.5
    q = q_ref[...]
    kall = jnp.concatenate([kp_ref[...], kc_ref[...]], axis=0)
    vall = jnp.concatenate([vp_ref[...], vc_ref[...]], axis=0)
    if not prev_always_valid:
        col = lax.broadcasted_iota(jnp.int32, (qb, span), 1)
        no_prev = (col < WINDOW) & (pl.program_id(1) == 0)
    heads = range(B_HEADS)
    ks = [kall[:, kh * B_HEAD_DIM:(kh + 1) * B_HEAD_DIM].astype(BF16) for kh in range(B_KV_HEADS)]
    vs = [vall[:, kh * B_HEAD_DIM:(kh + 1) * B_HEAD_DIM].astype(BF16) for kh in range(B_KV_HEADS)]
    qs = [(q[:, h * B_HEAD_DIM:(h + 1) * B_HEAD_DIM] * scale).astype(BF16) for h in heads]
    ss = [_dot_nt(qs[h], ks[h // B_GROUP]) + bias_ref[h] for h in heads]
    if not prev_always_valid:
        ss = [jnp.where(no_prev, MASK_VALUE, s) for s in ss]
    ms = [jnp.maximum(jnp.max(ss[h], axis=-1, keepdims=True), sink_ref[h]) for h in heads]
    ps = [jnp.exp(s - m) for s, m in zip(ss, ms)]
    denoms = [jnp.sum(ps[h], axis=-1, keepdims=True) + jnp.exp(sink_ref[h] - ms[h]) for h in heads]
    outs = [_dot(ps[h], vs[h // B_GROUP]) / denoms[h] for h in heads]
    o_ref[...] = jnp.concatenate(outs, axis=1).astype(o_ref.dtype)


def swa(proj, batch, seqlen, rel_bias, sinks, layer, k_past=None, v_past=None):
    m = proj.shape[0]
    has_cache = k_past is not None
    qb = math.gcd(seqlen, WINDOW)
    nb = seqlen // qb
    span = WINDOW + qb
    dist = np.arange(qb)[:, None] + WINDOW - np.arange(span)[None, :]
    band = ((dist >= 0) & (dist < WINDOW)).astype(np.float32)
    bucket = _t5_bucket(dist)
    q_col = IN_A // B_WIDTH
    k_col = (IN_A + B_WIDTH) // B_KV_WIDTH
    v_col = k_col + 1
    cur = lambda c: (lambda b, n: (b * nb + n, c))
    if has_cache:
        assert nb == 1
        prev_k = pl.BlockSpec((None, WINDOW, B_KV_WIDTH), lambda b, n: (layer, b, 0))
        prev_v = pl.BlockSpec((None, WINDOW, B_KV_WIDTH), lambda b, n: (layer, b, 0))
        kp_arr = k_past.reshape(k_past.shape[0], batch * WINDOW, B_KV_WIDTH)
        vp_arr = v_past.reshape(v_past.shape[0], batch * WINDOW, B_KV_WIDTH)
    else:
        assert qb == WINDOW
        prev = lambda c: (lambda b, n: (b * nb + jnp.maximum(n - 1, 0), c))
        prev_k = pl.BlockSpec((WINDOW, B_KV_WIDTH), prev(k_col))
        prev_v = pl.BlockSpec((WINDOW, B_KV_WIDTH), prev(v_col))
        kp_arr, vp_arr = proj, proj
    kern = functools.partial(_swa_kernel, qb=qb, span=span, prev_always_valid=has_cache)
    return pl.pallas_call(
        kern,
        grid=(batch, nb),
        in_specs=[pl.BlockSpec((qb, B_WIDTH), cur(q_col)),
                  prev_k,
                  pl.BlockSpec((qb, B_KV_WIDTH), cur(k_col)),
                  prev_v,
                  pl.BlockSpec((qb, B_KV_WIDTH), cur(v_col)),
                  pl.BlockSpec((qb, span), lambda b, n: (0, 0)),
                  pl.BlockSpec((qb, span), lambda b, n: (0, 0)),
                  pl.BlockSpec(memory_space=pltpu.SMEM),
                  pl.BlockSpec(memory_space=pltpu.SMEM)],
        out_specs=pl.BlockSpec((qb, B_WIDTH), lambda b, n: (b * nb + n, 0)),
        out_shape=jax.ShapeDtypeStruct((m, B_WIDTH), BF16 if qb % 16 == 0 else F32),
        scratch_shapes=[pltpu.VMEM((B_HEADS, qb, span), F32)],
        compiler_params=_cparams(("arbitrary", "arbitrary")),
        name="swa",
    )(proj, kp_arr, proj, vp_arr, proj, jnp.asarray(bucket), jnp.asarray(band), rel_bias, sinks)


def _rmsnorm_kernel(x_ref, g_ref, o_ref):
    o_ref[...] = _rms(x_ref[...], g_ref[...])


def rmsnorm_rows(x, g):
    m, d = x.shape
    tm = _pick_tile(m, (512, 256, 128, 64, 32, 16, 8))
    return pl.pallas_call(
        _rmsnorm_kernel,
        grid=(m // tm,),
        in_specs=[pl.BlockSpec((tm, d), lambda i: (i, 0)), pl.BlockSpec((1, d), lambda i: (0, 0))],
        out_specs=pl.BlockSpec((tm, d), lambda i: (i, 0)),
        out_shape=jax.ShapeDtypeStruct((m, d), F32),
        compiler_params=_cparams(("parallel",)),
        name="rmsnorm",
    )(x, g.reshape(1, d))


LORA_W = (0, 128)
LORA_A = (128, 256)
LORA_G = (256, 512)
LORA_V = (512, 640)


def _rwkv_in_kernel(x_ref, xp_ref, s_ref, g_ref, mu3_ref, mul_ref, w_ref, w1_ref,
                    rkv_ref, mid_ref, h_s, hp_s, *, tm, seqlen, has_vres):
    ph = pl.program_id(1)

    @pl.when(ph == 0)
    def _():
        g = g_ref[...]
        h = _rms(x_ref[...], g)
        rowid = lax.broadcasted_iota(jnp.int32, h.shape, 0)
        rolled = pltpu.roll(h, 1, 0)
        if seqlen % tm == 0:
            prev_last = _rms(xp_ref[...], g)[7:8, :]
            at_start = pl.program_id(0) % (seqlen // tm) == 0
            first = jnp.where(at_start, s_ref[...], prev_last)
            hp = jnp.where(rowid == 0, first, rolled)
        else:
            hp = jnp.where(rowid % seqlen == 0, s_ref[...], rolled)
        h_s[...] = h
        hp_s[...] = hp

    @pl.when(ph < 3)
    def _():
        h = h_s[...]
        xm = (h + (hp_s[...] - h) * mu3_ref[0]).astype(BF16)
        rkv_ref[0] = jnp.dot(xm, w_ref[...], preferred_element_type=F32)

    @pl.when(ph == 3)
    def _():
        h = h_s[...]
        xx = hp_s[...] - h
        mix = lambda i: (h + xx * mul_ref[i:i + 1, :]).astype(BF16)
        low = lambda i, rng: jnp.dot(mix(i), w1_ref[:, rng[0]:rng[1]], preferred_element_type=F32)
        parts = [jnp.tanh(low(0, LORA_W)), low(1, LORA_A), jax.nn.sigmoid(low(2, LORA_G))]
        if has_vres:
            parts.append(low(3, LORA_V))
        mid_ref[...] = jnp.concatenate(parts, axis=1).astype(BF16)


def rwkv_in(x, g, shift0, batch, seqlen, mu, w3_bf16, li, w1cat_bf16, has_vres):
    m, d = x.shape
    tm = next(t for t in (512, 256, 128, 64, 32, 16, 8)
              if m % t == 0 and (seqlen % t == 0 or t % seqlen == 0))
    midw = w1cat_bf16.shape[1]
    mu_rkv = jnp.stack([mu[0], mu[2], mu[3]])[:, None, :]
    mu_low = jnp.stack([mu[1], mu[4], mu[5], mu[3]])
    if shift0 is None:
        shift0 = jnp.zeros((batch, d), F32)
    if seqlen % tm == 0:
        srow = shift0[:, None, :]
        tps = seqlen // tm
        s_spec = pl.BlockSpec((None, 1, d), lambda i, p: (i // tps, 0, 0))
    else:
        srow = jnp.repeat(shift0, seqlen, axis=0)
        s_spec = pl.BlockSpec((tm, d), lambda i, p: (i, 0))
    sub = tm // 8
    kern = functools.partial(_rwkv_in_kernel, tm=tm, seqlen=seqlen, has_vres=has_vres)
    return pl.pallas_call(
        kern,
        grid=(m // tm, 4),
        in_specs=[pl.BlockSpec((tm, d), lambda i, p: (i, 0)),
                  pl.BlockSpec((8, d), lambda i, p: (jnp.maximum(i * sub - 1, 0), 0)),
                  s_spec,
                  pl.BlockSpec((1, d), lambda i, p: (0, 0)),
                  pl.BlockSpec((1, 1, d), lambda i, p: (jnp.minimum(p, 2), 0, 0)),
                  pl.BlockSpec((4, d), lambda i, p: (0, 0)),
                  pl.BlockSpec((None, None, d, d), lambda i, p: (li, jnp.minimum(p, 2), 0, 0)),
                  pl.BlockSpec((d, midw), lambda i, p: (0, 0))],
        out_specs=[pl.BlockSpec((1, tm, d), lambda i, p: (jnp.minimum(p, 2), i, 0)),
                   pl.BlockSpec((tm, midw), lambda i, p: (i, 0))],
        out_shape=[jax.ShapeDtypeStruct((3, m, d), F32),
                   jax.ShapeDtypeStruct((m, midw), BF16)],
        scratch_shapes=[pltpu.VMEM((tm, d), F32), pltpu.VMEM((tm, d), F32)],
        compiler_params=_cparams(("parallel", "arbitrary")),
        name="rwkv_in",
    )(x, x, srow, g.reshape(1, d), mu_rkv, mu_low, w3_bf16, w1cat_bf16)


def _softplus(z):
    return jnp.maximum(z, 0.0) + jnp.log(1.0 + jnp.exp(-jnp.abs(z)))


def _pad_lora(w1, w2, width):
    r = w1.shape[1]
    return (jnp.pad(w1, ((0, 0), (0, width - r))).astype(BF16),
            jnp.pad(w2, ((0, width - r), (0, 0))).astype(BF16))


def _rwkv_consts(c):
    i = np.arange(2 * c)
    same = (i[:, None] // c) == (i[None, :] // c)
    strict = (same & ((i[:, None] % c) > (i[None, :] % c))).astype(np.float32)
    incl = (same & ((i[:, None] % c) >= (i[None, :] % c))).astype(np.float32)
    eye = np.eye(2 * c, dtype=np.float32)
    l = np.arange(LANES)
    headones = ((l[:, None] // C_HEAD) == (l[None, :] // C_HEAD)).astype(np.float32)
    return strict, incl, eye, headones


def _rwkv_core_kernel(*refs, chunk, nchunks, npairs, has_state, has_vres):
    it = iter(refs)
    r_ref, k_ref, v_ref, mid_ref = (next(it) for _ in range(4))
    w0_ref, w2_ref, a0_ref, a2_ref, g2_ref = (next(it) for _ in range(5))
    if has_vres:
        vf_ref, v0_ref, v2_ref = next(it), next(it), next(it)
    kk_ref, ka_ref, rk_ref, lg_ref, lb_ref = (next(it) for _ in range(5))
    strict_ref, incl_ref, eye_ref, hones_ref = (next(it) for _ in range(4))
    if has_state:
        s0_ref = next(it)
    y_ref, s_ref, st_ref, wl_ref, a_ref, g_ref = (next(it) for _ in range(6))
    if has_vres:
        vg_ref = next(it)
    c = chunk
    l_idx = pl.program_id(2)
    lane = lax.broadcasted_iota(jnp.int32, (1, LANES), 1)
    m0 = (lane < C_HEAD).astype(F32)
    m1 = 1.0 - m0

    low = lambda rng, w_ref: jnp.dot(mid_ref[:, rng[0]:rng[1]], w_ref[...], preferred_element_type=F32)
    w_log = -_softplus(-(w0_ref[...] + low(LORA_W, w2_ref))) - 0.5
    wl_ref[...] = -jnp.exp(w_log)
    a_ref[...] = jax.nn.sigmoid(a0_ref[...] + low(LORA_A, a2_ref))
    g_ref[...] = low(LORA_G, g2_ref)
    if has_vres:
        vg_ref[...] = jax.nn.sigmoid(v0_ref[...] + low(LORA_V, v2_ref))

    @pl.when(l_idx == 0)
    def _():
        for pi in range(npairs):
            if has_state:
                z = jnp.zeros((C_HEAD, C_HEAD), F32)
                top = jnp.concatenate([s0_ref[0, 2 * pi], z], axis=1)
                bot = jnp.concatenate([z, s0_ref[0, 2 * pi + 1]], axis=1)
                st_ref[pi] = jnp.concatenate([top, bot], axis=0)
            else:
                st_ref[pi] = jnp.zeros((LANES, LANES), F32)

    strict = strict_ref[...]
    incl = incl_ref[...]
    eye = eye_ref[...]

    def stack(x):
        return jnp.concatenate([x * m0, x * m1], axis=0)

    first_head = lax.broadcasted_iota(jnp.int32, (c, LANES), 1) < C_HEAD
    rowid = lax.broadcasted_iota(jnp.int32, (c, LANES), 0)

    def rowsums(xs):
        return [jnp.where(first_head,
                          jnp.sum(x * m0, axis=-1, keepdims=True),
                          jnp.sum(x * m1, axis=-1, keepdims=True)) for x in xs]

    def cumsum_rows(x):
        s = 1
        while s < c:
            x = x + jnp.where(rowid >= s, pltpu.roll(x, s, 0), 0.0)
            s *= 2
        return x

    def load(pi, rows):
        cols = slice(pi * LANES, (pi + 1) * LANES)
        k = k_ref[0, rows, cols]
        v = v_ref[0, rows, cols]
        a = a_ref[rows, cols]
        if has_vres:
            v = v + (vf_ref[0, rows, cols] - v) * vg_ref[rows, cols]
        return dict(cols=cols, r=r_ref[0, rows, cols], v=v, a=a, wl=wl_ref[rows, cols],
                    kr=k * kk_ref[:, cols], kh=k * (1.0 + (a - 1.0) * ka_ref[:, cols]))

    def decays(p, ss):
        kk = p["kr"] * lax.rsqrt(jnp.maximum(ss, 1e-24))
        b = kk * p["a"]
        gc = cumsum_rows(p["wl"])
        gl = gc[c - 1:c, :]
        e_neg = jnp.exp(-gc)
        e_out = jnp.exp(gl - gc)
        p.update(gl=gl, ab=-kk * jnp.exp(gc - p["wl"]), rb=p["r"] * jnp.exp(gc),
                 bt=b * e_neg, kt=p["kh"] * e_neg, bh=b * e_out, khat=p["kh"] * e_out)

    def intra(p):
        lhs = jnp.concatenate([stack(p["ab"]), stack(p["rb"])], axis=0)
        with_b = _dot_nt(lhs, jnp.concatenate([p["bt"], p["bt"]], axis=0))
        with_k = _dot_nt(lhs, jnp.concatenate([p["kt"], p["kt"]], axis=0))
        p.update(a_ab=with_b[:2 * c] * strict, a_rb=with_b[2 * c:] * incl,
                 a_ak=with_k[:2 * c] * strict, a_rk=with_k[2 * c:] * incl)

    def body(ci, carry):
        rows = pl.ds(pl.multiple_of(ci * c, c), c)
        ps = [load(pi, rows) for pi in range(npairs)]
        for p, ss in zip(ps, rowsums([p["kr"] * p["kr"] for p in ps])):
            decays(p, ss)
        for p in ps:
            intra(p)

        ts = [eye + p["a_ab"] for p in ps]
        pws = [p["a_ab"] for p in ps]
        if c > 2:
            pws = [_dot3_shared([pw], pw)[0] for pw in pws]
            n = 2
            while 2 * n < c:
                res = [_dot3_shared([t, pw], pw) for t, pw in zip(ts, pws)]
                ts = [t + r[0] for t, r in zip(ts, res)]
                pws = [r[1] for r in res]
                n *= 2
            ts = [t + _dot3_shared([t], pw)[0] for t, pw in zip(ts, pws)]

        sts = [st_ref[pi] for pi in range(npairs)]
        fss = [_dot_nt(jnp.concatenate([p["ab"], p["rb"]], axis=0), st) for p, st in zip(ps, sts)]
        vss = [stack(p["v"]) for p in ps]
        rhss = [stack(fs[:c]) + _dot(p["a_ak"], vs) for p, fs, vs in zip(ps, fss, vss)]
        uss = [_dot(t, rhs) for t, rhs in zip(ts, rhss)]
        yss = [_dot(p["a_rb"], us) + _dot(p["a_rk"], vs) for p, us, vs in zip(ps, uss, vss)]
        for pi, (p, st, us) in enumerate(zip(ps, sts, uss)):
            u = us[:c] + us[c:]
            upd = _dot_tn(jnp.concatenate([u, p["v"]], axis=0),
                          jnp.concatenate([p["bh"], p["khat"]], axis=0))
            st_ref[pi] = st * jnp.exp(p["gl"]) + upd * hones_ref[...]
        inv_n = 1.0 / C_HEAD
        ys_ = [fs[c:] + ys[:c] + ys[c:] for fs, ys in zip(fss, yss)]
        sums = rowsums(ys_ + [p["r"] * p["kh"] * rk_ref[:, p["cols"]] for p in ps])
        dlts = [y - m * inv_n for y, m in zip(ys_, sums[:npairs])]
        vars_ = rowsums([d * d for d in dlts])
        for p, dlt, var, bsum in zip(ps, dlts, vars_, sums[npairs:]):
            cols = p["cols"]
            yn = dlt * lax.rsqrt(var * inv_n + GN_EPS) * lg_ref[:, cols] + lb_ref[:, cols]
            y_ref[rows, cols] = ((yn + bsum * p["v"]) * g_ref[rows, cols]).astype(y_ref.dtype)
        return carry

    lax.fori_loop(0, nchunks, body, 0)

    @pl.when(l_idx == pl.num_programs(2) - 1)
    def _():
        for pi in range(npairs):
            st = st_ref[pi]
            s_ref[0, 2 * pi] = st[:C_HEAD, :C_HEAD]
            s_ref[0, 2 * pi + 1] = st[C_HEAD:, C_HEAD:]


def rwkv_core(rkv, mid, low2, kk_p, ka_p, rk_p, lnx_g, lnx_b, batch, seqlen, layer,
              s0=None, v_first=None, npairs=8):
    _, m, d = rkv.shape
    c = math.gcd(seqlen, RWKV_CHUNK)
    lblk = _pick_tile(seqlen, (256, 128, 64, 32, 16, 8))
    nl_blocks = seqlen // lblk
    has_state = s0 is not None
    has_vres = v_first is not None
    if c * 4 <= RWKV_CHUNK:
        npairs = C_HEADS // 2
    w = npairs * LANES
    ngroups = d // w
    midw = mid.shape[1]
    strict, incl, eye, hones = (jnp.asarray(x) for x in _rwkv_consts(c))
    row = lambda b, p, l: b * nl_blocks + l
    blk3 = lambda which: pl.BlockSpec((1, lblk, w), lambda b, p, l: (which, row(b, p, l), p))
    blk2 = pl.BlockSpec((lblk, w), lambda b, p, l: (row(b, p, l), p))
    par = pl.BlockSpec((1, w), lambda b, p, l: (0, p))
    cols = lambda arr: pl.BlockSpec((arr.shape[0], w), lambda b, p, l: (0, p))
    full = lambda arr: pl.BlockSpec(arr.shape, lambda b, p, l: (0,) * arr.ndim)
    w0, w2, a0, a2, g2 = low2[:5]
    in_specs = [blk3(0), blk3(1), blk3(2), pl.BlockSpec((lblk, midw), lambda b, p, l: (row(b, p, l), 0)),
                par, cols(w2), par, cols(a2), cols(g2)]
    args = [rkv, rkv, rkv, mid, w0.reshape(1, d), w2, a0.reshape(1, d), a2, g2]
    if has_vres:
        v0, v2 = low2[5:]
        in_specs += [blk3(2), par, cols(v2)]
        args += [v_first, v0.reshape(1, d), v2]
    in_specs += [par] * 5
    args += [x.reshape(1, d) for x in (kk_p, ka_p, rk_p, lnx_g, lnx_b)]
    in_specs += [full(x) for x in (strict, incl, eye, hones)]
    args += [strict, incl, eye, hones]
    st_blk = pl.BlockSpec((1, 2 * npairs, C_HEAD, C_HEAD), lambda b, p, l: (b, p, 0, 0))
    if has_state:
        in_specs.append(pl.BlockSpec((None, 1, 2 * npairs, C_HEAD, C_HEAD),
                                     lambda b, p, l: (layer, b, p, 0, 0)))
        args.append(s0)
    kern = functools.partial(_rwkv_core_kernel, chunk=c, nchunks=lblk // c, npairs=npairs,
                             has_state=has_state, has_vres=has_vres)
    return pl.pallas_call(
        kern,
        grid=(batch, ngroups, nl_blocks),
        in_specs=in_specs,
        out_specs=[blk2, st_blk],
        out_shape=[jax.ShapeDtypeStruct((m, d), BF16 if lblk % 16 == 0 else F32),
                   jax.ShapeDtypeStruct((batch, C_HEADS, C_HEAD, C_HEAD), F32)],
        scratch_shapes=[pltpu.VMEM((npairs, LANES, LANES), F32)]
        + [pltpu.VMEM((lblk, w), F32)] * (4 if has_vres else 3),
        compiler_params=_cparams(("parallel", "parallel", "arbitrary")),
        name="rwkv_core",
    )(*args)


def _even_layer(x, batch, seqlen, e, layer, p, st_hgrn, k_cache, v_cache):
    proj = norm_matmul(x, p["norm_mix_pre"][layer], p["w_in_even"], e)
    o_a, s_new = hgrn(proj, p["hgrn_lb_raw"], e, batch, seqlen, st_hgrn)
    k_lo = IN_A + B_WIDTH
    new_rows = min(seqlen, WINDOW)
    tails = jnp.stack([proj[(b + 1) * seqlen - new_rows:(b + 1) * seqlen, k_lo:] for b in range(batch)])
    kb = tails[:, :, :B_KV_WIDTH].reshape(batch, new_rows, B_KV_HEADS, B_HEAD_DIM)
    vb = tails[:, :, B_KV_WIDTH:].reshape(batch, new_rows, B_KV_HEADS, B_HEAD_DIM)
    o_b = swa(proj, batch, seqlen, p["rel_bias"], p["attn_sinks"][e], e, k_cache, v_cache)
    if k_cache is None:
        k_new, v_new = kb, vb
    else:
        k_new = jnp.concatenate([k_cache[e, :, new_rows:], kb], axis=1)
        v_new = jnp.concatenate([v_cache[e, :, new_rows:], vb], axis=1)
    x = even_out(o_a, proj, o_b, x, p["hgrn_norm_g"][e], p["w_out_even"], p["norm_mix_post"][layer], e)
    return x, s_new, k_new, v_new


def _odd_layer(x, batch, seqlen, o, layer, p, shift0, s0, v_first):
    m, d = x.shape
    g_pre = p["norm_mix_pre"][layer]
    has_vres = o > 0
    width = lambda rng: rng[1] - rng[0]
    w1p, w2p = _pad_lora(p["rw_w1"][o], p["rw_w2"][o], width(LORA_W))
    a1p, a2p = _pad_lora(p["rw_a1"][o], p["rw_a2"][o], width(LORA_A))
    g1p, g2p = _pad_lora(p["rw_g1"][o], p["rw_g2"][o], width(LORA_G))
    first, low2 = [w1p, a1p, g1p], [p["rw_w0"][o], w2p, p["rw_a0"][o], a2p, g2p]
    if has_vres:
        v1p, v2p = _pad_lora(p["rw_v1"][o - 1], p["rw_v2"][o - 1], width(LORA_V))
        first.append(v1p)
        low2 += [p["rw_v0"][o - 1], v2p]
    rkv, mid = rwkv_in(x, g_pre, shift0, batch, seqlen, p["rw_mu"][o], p["w_rkv"], o,
                       jnp.concatenate(first, axis=1), has_vres)
    yg, s_new = rwkv_core(rkv, mid, low2, p["rw_kk"][o], p["rw_ka"][o], p["rw_rk"][o],
                          p["rw_lnx_g"][o], p["rw_lnx_b"][o], batch, seqlen, o, s0,
                          v_first if has_vres else None)
    shift_new = rmsnorm_rows(x.reshape(batch, seqlen, d)[:, -1], g_pre)
    x = odd_out(yg, x, p["rw_wo"], p["norm_mix_post"][layer], o)
    return x, s_new, shift_new, rkv


def _trunk(x3, st_hgrn, k_cache, v_cache, st_rwkv, st_shift, p):
    batch, seqlen, d = x3.shape
    x = x3.reshape(batch * seqlen, d)
    has_state = st_hgrn is not None
    hgrn_out, k_out, v_out, rwkv_out, shift_out = [], [], [], [], []
    v_first = None
    for layer in range(DEPTH):
        if layer % 2 == 0:
            e = layer // 2
            x, s_new, k_new, v_new = _even_layer(x, batch, seqlen, e, layer, p, st_hgrn, k_cache, v_cache)
            hgrn_out.append(s_new)
            k_out.append(k_new)
            v_out.append(v_new)
        else:
            o = layer // 2
            x, s_new, sh_new, rkv = _odd_layer(
                x, batch, seqlen, o, layer, p,
                st_shift[o] if has_state else None,
                st_rwkv,
                v_first)
            if o == 0:
                v_first = rkv
            rwkv_out.append(s_new)
            shift_out.append(sh_new)
        x = ffn(x, p["norm_ffn_pre"][layer], p["w_up"], p["w_down"], p["norm_ffn_post"][layer], layer)
    return (x.reshape(batch, seqlen, d), jnp.stack(hgrn_out), jnp.stack(k_out), jnp.stack(v_out),
            jnp.stack(rwkv_out), jnp.stack(shift_out))


def kernel(x_prompt, x_sample, state_hgrn, cache_swa_k, cache_swa_v, state_rwkv, state_shift,
           norm_mix_pre, norm_mix_post, norm_ffn_pre, norm_ffn_post,
           w_in_even, w_out_even, hgrn_lb_raw, hgrn_norm_g, rel_bias, attn_sinks,
           rw_mu, rw_wr, rw_wk, rw_wv, rw_wo, rw_w0, rw_w1, rw_w2, rw_a0, rw_a1, rw_a2,
           rw_v0, rw_v1, rw_v2, rw_g1, rw_g2, rw_kk, rw_ka, rw_rk, rw_lnx_g, rw_lnx_b,
           w_up, w_down):
    p = {
        "norm_mix_pre": norm_mix_pre, "norm_mix_post": norm_mix_post,
        "norm_ffn_pre": norm_ffn_pre, "norm_ffn_post": norm_ffn_post,
        "w_in_even": w_in_even.astype(BF16), "w_out_even": w_out_even.astype(BF16),
        "hgrn_lb_raw": hgrn_lb_raw, "hgrn_norm_g": hgrn_norm_g,
        "rel_bias": rel_bias, "attn_sinks": attn_sinks,
        "rw_mu": rw_mu, "w_rkv": jnp.stack([rw_wr, rw_wk, rw_wv], axis=1).astype(BF16),
        "rw_wo": rw_wo.astype(BF16),
        "rw_w0": rw_w0, "rw_w1": rw_w1, "rw_w2": rw_w2, "rw_a0": rw_a0, "rw_a1": rw_a1, "rw_a2": rw_a2,
        "rw_v0": rw_v0, "rw_v1": rw_v1, "rw_v2": rw_v2, "rw_g1": rw_g1, "rw_g2": rw_g2,
        "rw_kk": rw_kk, "rw_ka": rw_ka, "rw_rk": rw_rk, "rw_lnx_g": rw_lnx_g, "rw_lnx_b": rw_lnx_b,
        "w_up": w_up.astype(BF16), "w_down": w_down.astype(BF16),
    }
    y_p, hgrn_p, k_p, v_p, rwkv_p, shift_p = _trunk(x_prompt, None, None, None, None, None, p)
    y_s, hgrn_s, k_s, v_s, rwkv_s, shift_s = _trunk(
        x_sample, state_hgrn, cache_swa_k, cache_swa_v, state_rwkv, state_shift, p)
    return (y_p, y_s, hgrn_p, hgrn_s, k_p, k_s, v_p, v_s, rwkv_p, rwkv_s, shift_p, shift_s)
```

```python
import functools
import math

import numpy as np
import jax
import jax.numpy as jnp
from jax import lax
from jax.experimental import pallas as pl
from jax.experimental.pallas import tpu as pltpu

F32 = jnp.float32
BF16 = jnp.bfloat16

D_MODEL = 2048
DEPTH = 4
N_EVEN = 2
N_ODD = 2
A_HEADS = 8
A_KDIM = 128
A_VDIM = 128
A_WIDTH = 1024
A_QK = 1024
B_HEADS = 16
B_HEAD_DIM = 64
B_KV_HEADS = 4
B_GROUP = 4
B_WIDTH = 1024
B_KV_WIDTH = 256
WINDOW = 128
N_BUCKETS = 32
MAX_DISTANCE = 128
MASK_VALUE = -1e30
IN_A = 4096
IN_EVEN = 5632
C_HEAD = 64
C_HEADS = 32
GN_EPS = 64e-5
D_FF = 8192
NORM_EPS = 1e-6

LANES = 128
VMEM_LIMIT = 56 * 1024 * 1024

HGRN_CHUNK = 128
RWKV_CHUNK = 64


def _cparams(sem):
    return pltpu.CompilerParams(dimension_semantics=sem, vmem_limit_bytes=VMEM_LIMIT)


def _rms(x, g):
    return x * lax.rsqrt(jnp.mean(x * x, axis=-1, keepdims=True) + NORM_EPS) * g


def _dot(a, b):
    return jnp.dot(a.astype(BF16), b.astype(BF16), preferred_element_type=F32)


def _dot_nt(a, b):
    return lax.dot_general(a.astype(BF16), b.astype(BF16), (((1,), (1,)), ((), ())),
                           preferred_element_type=F32)


def _dot_tn(a, b):
    return lax.dot_general(a.astype(BF16), b.astype(BF16), (((0,), (0,)), ((), ())),
                           preferred_element_type=F32)


def _split2(x):
    hi = x.astype(BF16)
    return hi, (x - hi.astype(F32)).astype(BF16)


def _dot3_shared(lhs_list, b):
    bh, bl = _split2(b)
    parts = [_split2(a) for a in lhs_list]
    his = [p[0] for p in parts]
    los = [p[1] for p in parts]
    by_hi = jnp.dot(jnp.concatenate(his + los, axis=0), bh, preferred_element_type=F32)
    by_lo = jnp.dot(jnp.concatenate(his, axis=0), bl, preferred_element_type=F32)
    n = sum(a.shape[0] for a in lhs_list)
    out, off = [], 0
    for a in lhs_list:
        m = a.shape[0]
        out.append(by_hi[off:off + m] + by_hi[n + off:n + off + m] + by_lo[off:off + m])
        off += m
    return out


def _pick_tile(m, cands):
    for c in cands:
        if m % c == 0:
            return c
    return m


def _norm_matmul_kernel(x_ref, g_ref, w_ref, o_ref, xn_ref):
    @pl.when(pl.program_id(1) == 0)
    def _():
        xn_ref[...] = _rms(x_ref[...], g_ref[...]).astype(BF16)

    o_ref[...] = jnp.dot(xn_ref[...], w_ref[...], preferred_element_type=F32)


def norm_matmul(x, g, w_stack_bf16, li, tn=1408):
    m, d = x.shape
    n = w_stack_bf16.shape[2]
    tm = _pick_tile(m, (1024, 512, 256, 128, 64, 32, 16, 8))
    return pl.pallas_call(
        _norm_matmul_kernel,
        grid=(m // tm, n // tn),
        in_specs=[pl.BlockSpec((tm, d), lambda i, j: (i, 0)),
                  pl.BlockSpec((1, d), lambda i, j: (0, 0)),
                  pl.BlockSpec((None, d, tn), lambda i, j: (li, 0, j))],
        out_specs=pl.BlockSpec((tm, tn), lambda i, j: (i, j)),
        out_shape=jax.ShapeDtypeStruct((m, n), F32),
        scratch_shapes=[pltpu.VMEM((tm, d), BF16)],
        compiler_params=_cparams(("parallel", "arbitrary")),
        name="norm_matmul",
    )(x, g.reshape(1, d), w_stack_bf16)


def _ffn_kernel(x_ref, gpre_ref, wup_ref, wdn_ref, gpost_ref, o_ref, xn_ref, acc_ref):
    f = pl.program_id(1)

    @pl.when(f == 0)
    def _():
        xn_ref[...] = _rms(x_ref[...], gpre_ref[...]).astype(BF16)
        acc_ref[...] = jnp.zeros_like(acc_ref)

    h = jnp.dot(xn_ref[...], wup_ref[...], preferred_element_type=F32)
    h = jnp.square(jnp.maximum(h, 0.0)).astype(BF16)
    acc_ref[...] += jnp.dot(h, wdn_ref[...], preferred_element_type=F32)

    @pl.when(f == pl.num_programs(1) - 1)
    def _():
        o_ref[...] = x_ref[...] + _rms(acc_ref[...], gpost_ref[...])


def ffn(x, gpre, wup_bf16, wdn_bf16, gpost, li, tf=1024):
    m, d = x.shape
    dff = wup_bf16.shape[2]
    tm = _pick_tile(m, (512, 256, 128, 64, 32, 16, 8))
    return pl.pallas_call(
        _ffn_kernel,
        grid=(m // tm, dff // tf),
        in_specs=[pl.BlockSpec((tm, d), lambda i, f: (i, 0)),
                  pl.BlockSpec((1, d), lambda i, f: (0, 0)),
                  pl.BlockSpec((None, d, tf), lambda i, f: (li, 0, f)),
                  pl.BlockSpec((None, tf, d), lambda i, f: (li, f, 0)),
                  pl.BlockSpec((1, d), lambda i, f: (0, 0))],
        out_specs=pl.BlockSpec((tm, d), lambda i, f: (i, 0)),
        out_shape=jax.ShapeDtypeStruct((m, d), F32),
        scratch_shapes=[pltpu.VMEM((tm, d), BF16), pltpu.VMEM((tm, d), F32)],
        compiler_params=_cparams(("parallel", "arbitrary")),
        name="ffn",
    )(x, gpre.reshape(1, d), wup_bf16, wdn_bf16, gpost.reshape(1, d))


def _even_out_kernel(oa_ref, ga_ref, ob_ref, x_ref, ag_ref, w_ref, gpost_ref, o_ref):
    ga = ga_ref[...]
    oan = _rms(oa_ref[...], ag_ref[...]) * (ga * jax.nn.sigmoid(ga))
    mix = (jnp.dot(oan.astype(BF16), w_ref[:A_WIDTH, :], preferred_element_type=F32)
           + jnp.dot(ob_ref[...].astype(BF16), w_ref[A_WIDTH:, :], preferred_element_type=F32))
    o_ref[...] = x_ref[...] + _rms(mix, gpost_ref[...])


def even_out(o_a, proj, o_b, x, a_norm_g, w_out_bf16, gpost, li):
    m, d = x.shape
    tm = _pick_tile(m, (384, 256, 128, 64, 32, 16, 8))
    ga_blk = (3 * A_WIDTH) // A_WIDTH
    return pl.pallas_call(
        _even_out_kernel,
        grid=(m // tm,),
        in_specs=[pl.BlockSpec((tm, A_WIDTH), lambda i: (i, 0)),
                  pl.BlockSpec((tm, A_WIDTH), lambda i: (i, ga_blk)),
                  pl.BlockSpec((tm, B_WIDTH), lambda i: (i, 0)),
                  pl.BlockSpec((tm, d), lambda i: (i, 0)),
                  pl.BlockSpec((1, A_WIDTH), lambda i: (0, 0)),
                  pl.BlockSpec((None, A_WIDTH + B_WIDTH, d), lambda i: (li, 0, 0)),
                  pl.BlockSpec((1, d), lambda i: (0, 0))],
        out_specs=pl.BlockSpec((tm, d), lambda i: (i, 0)),
        out_shape=jax.ShapeDtypeStruct((m, d), F32),
        compiler_params=_cparams(("parallel",)),
        name="even_out",
    )(o_a, proj, o_b, x, a_norm_g.reshape(1, A_WIDTH), w_out_bf16, gpost.reshape(1, d))


def _odd_out_kernel(y_ref, x_ref, w_ref, gpost_ref, o_ref):
    mix = jnp.dot(y_ref[...].astype(BF16), w_ref[...], preferred_element_type=F32)
    o_ref[...] = x_ref[...] + _rms(mix, gpost_ref[...])


def odd_out(yg, x, wo_bf16, gpost, li):
    m, d = x.shape
    tm = _pick_tile(m, (384, 256, 128, 64, 32, 16, 8))
    return pl.pallas_call(
        _odd_out_kernel,
        grid=(m // tm,),
        in_specs=[pl.BlockSpec((tm, d), lambda i: (i, 0)),
                  pl.BlockSpec((tm, d), lambda i: (i, 0)),
                  pl.BlockSpec((None, d, d), lambda i: (li, 0, 0)),
                  pl.BlockSpec((1, d), lambda i: (0, 0))],
        out_specs=pl.BlockSpec((tm, d), lambda i: (i, 0)),
        out_shape=jax.ShapeDtypeStruct((m, d), F32),
        compiler_params=_cparams(("parallel",)),
        name="odd_out",
    )(yg, x, wo_bf16, gpost.reshape(1, d))


def _level_consts(c):
    levels = []
    s = c // 2
    while s >= 1:
        levels.append(s)
        s //= 2
    mask = np.zeros((len(levels), c, c), np.float32)
    idx = np.arange(c)
    for l, s in enumerate(levels):
        same = (idx[:, None] // (2 * s)) == (idx[None, :] // (2 * s))
        upper = (idx[:, None] % (2 * s)) >= s
        lower = (idx[None, :] % (2 * s)) < s
        mask[l] = (same & upper & lower).astype(np.float32)
    return levels, mask


def _split_rows(g, s, rowid):
    c = g.shape[0]
    if 2 * s >= 8:
        return jnp.concatenate(
            [jnp.broadcast_to(g[b + s - 1:b + s, :], (2 * s, g.shape[1])) for b in range(0, c, 2 * s)], axis=0)
    r = rowid % (2 * s)
    out = g
    for off in range(-(s - 1), s + 1):
        if off != 0:
            out = jnp.where(r == s - 1 + off, pltpu.roll(g, off % c, 0), out)
    return out


def _hgrn_kernel(*refs, layer, chunk, nchunks, levels, has_state, nheads):
    if has_state:
        q_ref, f_ref, i_ref, lb_ref, mask_ref, s0_ref, o_ref, s_ref, st_ref = refs
    else:
        q_ref, f_ref, i_ref, lb_ref, mask_ref, o_ref, s_ref, st_ref = refs
    c = chunk
    nh = nheads
    rowid = lax.broadcasted_iota(jnp.int32, (c, nh * LANES), 0)
    l_idx = pl.program_id(2)

    @pl.when(l_idx == 0)
    def _():
        for hi in range(nh):
            if has_state:
                st_ref[hi] = s0_ref[0, hi].T
            else:
                st_ref[hi] = jnp.zeros((A_VDIM, A_KDIM), F32)

    lbr = lb_ref[...]
    e = jnp.exp(lbr - jnp.max(lbr, axis=0, keepdims=True))
    p = e / jnp.sum(e, axis=0, keepdims=True)
    lb = jnp.zeros((1, nh * LANES), F32)
    for i in range(1, layer + 1):
        lb = lb + p[i:i + 1, :]
    one_m_lb = 1.0 - lb
    head = lambda x, hi: x[:, hi * LANES:(hi + 1) * LANES]

    def body(ci, carry):
        rows = pl.ds(pl.multiple_of(ci * c, c), c)
        fq = f_ref[rows, :]
        qr = q_ref[rows, :]
        v = i_ref[rows, :]
        f = lb + one_m_lb * jax.nn.sigmoid(fq)
        k = 1.0 - f
        q = qr * jax.nn.sigmoid(qr) * (A_KDIM ** -0.5)

        g = jnp.log2(f)
        sft = 1
        while sft < c:
            g = g + jnp.where(rowid >= sft, pltpu.roll(g, sft, 0), 0.0)
            sft *= 2
        glast = g[c - 1:c, :]
        q_in = q * jnp.exp2(g)
        kd = k * jnp.exp2(glast - g)
        dec = jnp.exp2(glast)
        diag = q * k

        sts = [st_ref[hi] for hi in range(nh)]
        os_ = [_dot_nt(head(q_in, hi), sts[hi]) for hi in range(nh)]
        attns = [jnp.zeros((c, c), F32) for _ in range(nh)]
        for l, s in enumerate(levels):
            e = jnp.exp2(-jnp.abs(g - _split_rows(g, s, rowid)))
            qs = q * e
            ks = k * e
            ml = mask_ref[l]
            attns = [at + ml * _dot_nt(head(qs, hi), head(ks, hi)) for hi, at in enumerate(attns)]
        for hi in range(nh):
            vh = head(v, hi)
            o = os_[hi] + _dot(attns[hi], vh) + jnp.sum(head(diag, hi), axis=-1, keepdims=True) * vh
            o_ref[rows, hi * LANES:(hi + 1) * LANES] = o
        for hi in range(nh):
            st_ref[hi] = sts[hi] * head(dec, hi) + _dot_tn(head(v, hi), head(kd, hi))
        return carry

    lax.fori_loop(0, nchunks, body, 0, unroll=2 if nchunks % 2 == 0 else 1)

    @pl.when(l_idx == pl.num_programs(2) - 1)
    def _():
        for hi in range(nh):
            s_ref[0, hi] = st_ref[hi].T


def hgrn(proj, lb_raw, layer, batch, seqlen, s0=None, nheads=4):
    m = proj.shape[0]
    c = math.gcd(seqlen, HGRN_CHUNK)
    lblk = _pick_tile(seqlen, (512, 256, 128, 64, 32, 16, 8))
    nl_blocks = seqlen // lblk
    levels, mask = _level_consts(c)
    has_state = s0 is not None
    if c * 4 <= HGRN_CHUNK:
        nheads = A_HEADS
    w = nheads * LANES
    ngroups = A_HEADS // nheads
    kern = functools.partial(_hgrn_kernel, layer=layer, chunk=c, nchunks=lblk // c,
                             levels=tuple(levels), has_state=has_state, nheads=nheads)
    row = lambda b, h, l: b * nl_blocks + l
    in_specs = [pl.BlockSpec((lblk, w), lambda b, h, l: (row(b, h, l), h)),
                pl.BlockSpec((lblk, w), lambda b, h, l: (row(b, h, l), ngroups + h)),
                pl.BlockSpec((lblk, w), lambda b, h, l: (row(b, h, l), 2 * ngroups + h)),
                pl.BlockSpec((N_EVEN, w), lambda b, h, l: (0, h)),
                pl.BlockSpec((len(levels), c, c), lambda b, h, l: (0, 0, 0))]
    args = [proj, proj, proj, lb_raw, jnp.asarray(mask)]
    st_blk = pl.BlockSpec((1, nheads, A_KDIM, A_VDIM), lambda b, h, l: (b, h, 0, 0))
    if has_state:
        in_specs.append(pl.BlockSpec((None, 1, nheads, A_KDIM, A_VDIM), lambda b, h, l: (layer, b, h, 0, 0)))
        args.append(s0)
    return pl.pallas_call(
        kern,
        grid=(batch, ngroups, nl_blocks),
        in_specs=in_specs,
        out_specs=[pl.BlockSpec((lblk, w), lambda b, h, l: (row(b, h, l), h)), st_blk],
        out_shape=[jax.ShapeDtypeStruct((m, A_WIDTH), F32),
                   jax.ShapeDtypeStruct((batch, A_HEADS, A_KDIM, A_VDIM), F32)],
        scratch_shapes=[pltpu.VMEM((nheads, A_VDIM, A_KDIM), F32)],
        compiler_params=_cparams(("parallel", "parallel", "arbitrary")),
        name="hgrn",
    )(*args)


def _t5_bucket(dist):
    max_exact = N_BUCKETS // 2
    d = np.maximum(dist, 0)
    large = max_exact + (np.log(np.maximum(d, max_exact).astype(np.float32) / max_exact)
                         / math.log(MAX_DISTANCE / max_exact) * (N_BUCKETS - max_exact)).astype(np.int32)
    large = np.minimum(large, N_BUCKETS - 1)
    return np.where(d < max_exact, d, large).astype(np.int32)


def _swa_kernel(q_ref, kp_ref, kc_ref, vp_ref, vc_ref, bucket_ref, band_ref, rb_ref, sink_ref,
                o_ref, bias_ref, *, qb, span, prev_always_valid):
    first = (pl.program_id(0) == 0) & (pl.program_id(1) == 0)

    @pl.when(first)
    def _():
        bk = bucket_ref[...]
        band = band_ref[...]

        def per_head(h, carry):
            def per_bucket(bi, acc):
                return jnp.where(bk == bi, rb_ref[bi, h], acc)
            acc = lax.fori_loop(0, N_BUCKETS, per_bucket, jnp.zeros((qb, span), F32))
            bias_ref[h] = jnp.where(band > 0, acc, MASK_VALUE)
            return carry

        lax.fori_loop(0, B_HEADS, per_head, 0)

    scale = B_HEAD_DIM ** -0.5
    q = q_ref[...]
    kall = jnp.concatenate([kp_ref[...], kc_ref[...]], axis=0)
    vall = jnp.concatenate([vp_ref[...], vc_ref[...]], axis=0)
    if not prev_always_valid:
        col = lax.broadcasted_iota(jnp.int32, (qb, span), 1)
        no_prev = (col < WINDOW) & (pl.program_id(1) == 0)
    heads = range(B_HEADS)
    ks = [kall[:, kh * B_HEAD_DIM:(kh + 1) * B_HEAD_DIM].astype(BF16) for kh in range(B_KV_HEADS)]
    vs = [vall[:, kh * B_HEAD_DIM:(kh + 1) * B_HEAD_DIM].astype(BF16) for kh in range(B_KV_HEADS)]
    qs = [(q[:, h * B_HEAD_DIM:(h + 1) * B_HEAD_DIM] * scale).astype(BF16) for h in heads]
    ss = [_dot_nt(qs[h], ks[h // B_GROUP]) + bias_ref[h] for h in heads]
    if not prev_always_valid:
        ss = [jnp.where(no_prev, MASK_VALUE, s) for s in ss]
    ms = [jnp.maximum(jnp.max(ss[h], axis=-1, keepdims=True), sink_ref[h]) for h in heads]
    ps = [jnp.exp(s - m) for s, m in zip(ss, ms)]
    denoms = [jnp.sum(ps[h], axis=-1, keepdims=True) + jnp.exp(sink_ref[h] - ms[h]) for h in heads]
    outs = [_dot(ps[h], vs[h // B_GROUP]) / denoms[h] for h in heads]
    o_ref[...] = jnp.concatenate(outs, axis=1).astype(o_ref.dtype)


def swa(proj, batch, seqlen, rel_bias, sinks, layer, k_past=None, v_past=None):
    m = proj.shape[0]
    has_cache = k_past is not None
    qb = math.gcd(seqlen, WINDOW)
    nb = seqlen // qb
    span = WINDOW + qb
    dist = np.arange(qb)[:, None] + WINDOW - np.arange(span)[None, :]
    band = ((dist >= 0) & (dist < WINDOW)).astype(np.float32)
    bucket = _t5_bucket(dist)
    q_col = IN_A // B_WIDTH
    k_col = (IN_A + B_WIDTH) // B_KV_WIDTH
    v_col = k_col + 1
    cur = lambda c: (lambda b, n: (b * nb + n, c))
    if has_cache:
        assert nb == 1
        prev_k = pl.BlockSpec((None, WINDOW, B_KV_WIDTH), lambda b, n: (layer, b, 0))
        prev_v = pl.BlockSpec((None, WINDOW, B_KV_WIDTH), lambda b, n: (layer, b, 0))
        kp_arr = k_past.reshape(k_past.shape[0], batch * WINDOW, B_KV_WIDTH)
        vp_arr = v_past.reshape(v_past.shape[0], batch * WINDOW, B_KV_WIDTH)
    else:
        assert qb == WINDOW
        prev = lambda c: (lambda b, n: (b * nb + jnp.maximum(n - 1, 0), c))
        prev_k = pl.BlockSpec((WINDOW, B_KV_WIDTH), prev(k_col))
        prev_v = pl.BlockSpec((WINDOW, B_KV_WIDTH), prev(v_col))
        kp_arr, vp_arr = proj, proj
    kern = functools.partial(_swa_kernel, qb=qb, span=span, prev_always_valid=has_cache)
    return pl.pallas_call(
        kern,
        grid=(batch, nb),
        in_specs=[pl.BlockSpec((qb, B_WIDTH), cur(q_col)),
                  prev_k,
                  pl.BlockSpec((qb, B_KV_WIDTH), cur(k_col)),
                  prev_v,
                  pl.BlockSpec((qb, B_KV_WIDTH), cur(v_col)),
                  pl.BlockSpec((qb, span), lambda b, n: (0, 0)),
                  pl.BlockSpec((qb, span), lambda b, n: (0, 0)),
                  pl.BlockSpec(memory_space=pltpu.SMEM),
                  pl.BlockSpec(memory_space=pltpu.SMEM)],
        out_specs=pl.BlockSpec((qb, B_WIDTH), lambda b, n: (b * nb + n, 0)),
        out_shape=jax.ShapeDtypeStruct((m, B_WIDTH), BF16 if qb % 16 == 0 else F32),
        scratch_shapes=[pltpu.VMEM((B_HEADS, qb, span), F32)],
        compiler_params=_cparams(("arbitrary", "arbitrary")),
        name="swa",
    )(proj, kp_arr, proj, vp_arr, proj, jnp.asarray(bucket), jnp.asarray(band), rel_bias, sinks)


def _rmsnorm_kernel(x_ref, g_ref, o_ref):
    o_ref[...] = _rms(x_ref[...], g_ref[...])


def rmsnorm_rows(x, g):
    m, d = x.shape
    tm = _pick_tile(m, (512, 256, 128, 64, 32, 16, 8))
    return pl.pallas_call(
        _rmsnorm_kernel,
        grid=(m // tm,),
        in_specs=[pl.BlockSpec((tm, d), lambda i: (i, 0)), pl.BlockSpec((1, d), lambda i: (0, 0))],
        out_specs=pl.BlockSpec((tm, d), lambda i: (i, 0)),
        out_shape=jax.ShapeDtypeStruct((m, d), F32),
        compiler_params=_cparams(("parallel",)),
        name="rmsnorm",
    )(x, g.reshape(1, d))


LORA_W = (0, 128)
LORA_A = (128, 256)
LORA_G = (256, 512)
LORA_V = (512, 640)


def _rwkv_in_kernel(x_ref, xp_ref, s_ref, g_ref, mu3_ref, mul_ref, w_ref, w1_ref,
                    rkv_ref, mid_ref, h_s, hp_s, *, tm, seqlen, has_vres):
    ph = pl.program_id(1)

    @pl.when(ph == 0)
    def _():
        g = g_ref[...]
        h = _rms(x_ref[...], g)
        rowid = lax.broadcasted_iota(jnp.int32, h.shape, 0)
        rolled = pltpu.roll(h, 1, 0)
        if seqlen % tm == 0:
            prev_last = _rms(xp_ref[...], g)[7:8, :]
            at_start = pl.program_id(0) % (seqlen // tm) == 0
            first = jnp.where(at_start, s_ref[...], prev_last)
            hp = jnp.where(rowid == 0, first, rolled)
        else:
            hp = jnp.where(rowid % seqlen == 0, s_ref[...], rolled)
        h_s[...] = h
        hp_s[...] = hp

    @pl.when(ph < 3)
    def _():
        h = h_s[...]
        xm = (h + (hp_s[...] - h) * mu3_ref[0]).astype(BF16)
        rkv_ref[0] = jnp.dot(xm, w_ref[...], preferred_element_type=F32)

    @pl.when(ph == 3)
    def _():
        h = h_s[...]
        xx = hp_s[...] - h
        mix = lambda i: (h + xx * mul_ref[i:i + 1, :]).astype(BF16)
        low = lambda i, rng: jnp.dot(mix(i), w1_ref[:, rng[0]:rng[1]], preferred_element_type=F32)
        parts = [jnp.tanh(low(0, LORA_W)), low(1, LORA_A), jax.nn.sigmoid(low(2, LORA_G))]
        if has_vres:
            parts.append(low(3, LORA_V))
        mid_ref[...] = jnp.concatenate(parts, axis=1).astype(BF16)


def rwkv_in(x, g, shift0, batch, seqlen, mu, w3_bf16, li, w1cat_bf16, has_vres):
    m, d = x.shape
    tm = next(t for t in (512, 256, 128, 64, 32, 16, 8)
              if m % t == 0 and (seqlen % t == 0 or t % seqlen == 0))
    midw = w1cat_bf16.shape[1]
    mu_rkv = jnp.stack([mu[0], mu[2], mu[3]])[:, None, :]
    mu_low = jnp.stack([mu[1], mu[4], mu[5], mu[3]])
    if shift0 is None:
        shift0 = jnp.zeros((batch, d), F32)
    if seqlen % tm == 0:
        srow = shift0[:, None, :]
        tps = seqlen // tm
        s_spec = pl.BlockSpec((None, 1, d), lambda i, p: (i // tps, 0, 0))
    else:
        srow = jnp.repeat(shift0, seqlen, axis=0)
        s_spec = pl.BlockSpec((tm, d), lambda i, p: (i, 0))
    sub = tm // 8
    kern = functools.partial(_rwkv_in_kernel, tm=tm, seqlen=seqlen, has_vres=has_vres)
    return pl.pallas_call(
        kern,
        grid=(m // tm, 4),
        in_specs=[pl.BlockSpec((tm, d), lambda i, p: (i, 0)),
                  pl.BlockSpec((8, d), lambda i, p: (jnp.maximum(i * sub - 1, 0), 0)),
                  s_spec,
                  pl.BlockSpec((1, d), lambda i, p: (0, 0)),
                  pl.BlockSpec((1, 1, d), lambda i, p: (jnp.minimum(p, 2), 0, 0)),
                  pl.BlockSpec((4, d), lambda i, p: (0, 0)),
                  pl.BlockSpec((None, None, d, d), lambda i, p: (li, jnp.minimum(p, 2), 0, 0)),
                  pl.BlockSpec((d, midw), lambda i, p: (0, 0))],
        out_specs=[pl.BlockSpec((1, tm, d), lambda i, p: (jnp.minimum(p, 2), i, 0)),
                   pl.BlockSpec((tm, midw), lambda i, p: (i, 0))],
        out_shape=[jax.ShapeDtypeStruct((3, m, d), F32),
                   jax.ShapeDtypeStruct((m, midw), BF16)],
        scratch_shapes=[pltpu.VMEM((tm, d), F32), pltpu.VMEM((tm, d), F32)],
        compiler_params=_cparams(("parallel", "arbitrary")),
        name="rwkv_in",
    )(x, x, srow, g.reshape(1, d), mu_rkv, mu_low, w3_bf16, w1cat_bf16)


def _pad_lora(w1, w2, width):
    r = w1.shape[1]
    return (jnp.pad(w1, ((0, 0), (0, width - r))).astype(BF16),
            jnp.pad(w2, ((0, width - r), (0, 0))).astype(BF16))


def _rwkv_consts(c):
    i = np.arange(2 * c)
    same = (i[:, None] // c) == (i[None, :] // c)
    strict = (same & ((i[:, None] % c) > (i[None, :] % c))).astype(np.float32)
    incl = (same & ((i[:, None] % c) >= (i[None, :] % c))).astype(np.float32)
    eye = np.eye(2 * c, dtype=np.float32)
    l = np.arange(LANES)
    headones = ((l[:, None] // C_HEAD) == (l[None, :] // C_HEAD)).astype(np.float32)
    return strict, incl, eye, headones


def _rwkv_core_kernel(*refs, chunk, nchunks, npairs, has_state, has_vres):
    it = iter(refs)
    r_ref, k_ref, v_ref, mid_ref = (next(it) for _ in range(4))
    w0_ref, w2_ref, a0_ref, a2_ref, g2_ref = (next(it) for _ in range(5))
    if has_vres:
        vf_ref, v0_ref, v2_ref = next(it), next(it), next(it)
    kk_ref, ka_ref, rk_ref, lg_ref, lb_ref = (next(it) for _ in range(5))
    strict_ref, incl_ref, eye_ref, hones_ref = (next(it) for _ in range(4))
    if has_state:
        s0_ref = next(it)
    y_ref, s_ref, st_ref, wl_ref, a_ref, g_ref = (next(it) for _ in range(6))
    if has_vres:
        vg_ref = next(it)
    c = chunk
    l_idx = pl.program_id(2)
    lane = lax.broadcasted_iota(jnp.int32, (1, LANES), 1)
    m0 = (lane < C_HEAD).astype(F32)
    m1 = 1.0 - m0

    low = lambda rng, w_ref: jnp.dot(mid_ref[:, rng[0]:rng[1]], w_ref[...], preferred_element_type=F32)
    wl_ref[...] = -math.exp(-0.5) * jax.nn.sigmoid(w0_ref[...] + low(LORA_W, w2_ref))
    a_ref[...] = jax.nn.sigmoid(a0_ref[...] + low(LORA_A, a2_ref))
    g_ref[...] = low(LORA_G, g2_ref)
    if has_vres:
        vg_ref[...] = jax.nn.sigmoid(v0_ref[...] + low(LORA_V, v2_ref))

    @pl.when(l_idx == 0)
    def _():
        for pi in range(npairs):
            if has_state:
                z = jnp.zeros((C_HEAD, C_HEAD), F32)
                top = jnp.concatenate([s0_ref[0, 2 * pi], z], axis=1)
                bot = jnp.concatenate([z, s0_ref[0, 2 * pi + 1]], axis=1)
                st_ref[pi] = jnp.concatenate([top, bot], axis=0)
            else:
                st_ref[pi] = jnp.zeros((LANES, LANES), F32)

    strict = strict_ref[...]
    incl = incl_ref[...]
    eye = eye_ref[...]

    def stack(x):
        return jnp.concatenate([x * m0, x * m1], axis=0)

    first_head = lax.broadcasted_iota(jnp.int32, (c, LANES), 1) < C_HEAD
    rowid = lax.broadcasted_iota(jnp.int32, (c, LANES), 0)

    def rowsums(xs):
        return [jnp.where(first_head,
                          jnp.sum(x * m0, axis=-1, keepdims=True),
                          jnp.sum(x * m1, axis=-1, keepdims=True)) for x in xs]

    def cumsum_rows(x):
        s = 1
        while s < c:
            x = x + jnp.where(rowid >= s, pltpu.roll(x, s, 0), 0.0)
            s *= 2
        return x

    def load(pi, rows):
        cols = slice(pi * LANES, (pi + 1) * LANES)
        k = k_ref[0, rows, cols]
        v = v_ref[0, rows, cols]
        a = a_ref[rows, cols]
        if has_vres:
            v = v + (vf_ref[0, rows, cols] - v) * vg_ref[rows, cols]
        return dict(cols=cols, r=r_ref[0, rows, cols], v=v, a=a, wl=wl_ref[rows, cols],
                    kr=k * kk_ref[:, cols], kh=k * (1.0 + (a - 1.0) * ka_ref[:, cols]))

    def decays(p, ss):
        kk = p["kr"] * lax.rsqrt(jnp.maximum(ss, 1e-24))
        b = kk * p["a"]
        gc = cumsum_rows(p["wl"])
        gl = gc[c - 1:c, :]
        e_neg = jnp.exp(-gc)
        e_out = jnp.exp(gl - gc)
        p.update(gl=gl, ab=-kk * jnp.exp(gc - p["wl"]), rb=p["r"] * jnp.exp(gc),
                 bt=b * e_neg, kt=p["kh"] * e_neg, bh=b * e_out, khat=p["kh"] * e_out)

    def intra(p):
        lhs = jnp.concatenate([stack(p["ab"]), stack(p["rb"])], axis=0)
        with_b = _dot_nt(lhs, jnp.concatenate([p["bt"], p["bt"]], axis=0))
        with_k = _dot_nt(lhs, jnp.concatenate([p["kt"], p["kt"]], axis=0))
        p.update(a_ab=with_b[:2 * c] * strict, a_rb=with_b[2 * c:] * incl,
                 a_ak=with_k[:2 * c] * strict, a_rk=with_k[2 * c:] * incl)

    def body(ci, carry):
        rows = pl.ds(pl.multiple_of(ci * c, c), c)
        ps = [load(pi, rows) for pi in range(npairs)]
        for p, ss in zip(ps, rowsums([p["kr"] * p["kr"] for p in ps])):
            decays(p, ss)
        for p in ps:
            intra(p)

        ts = [eye + p["a_ab"] for p in ps]
        pws = [p["a_ab"] for p in ps]
        if c > 2:
            pws = [_dot3_shared([pw], pw)[0] for pw in pws]
            n = 2
            while 2 * n < c:
                res = [_dot3_shared([t, pw], pw) for t, pw in zip(ts, pws)]
                ts = [t + r[0] for t, r in zip(ts, res)]
                pws = [r[1] for r in res]
                n *= 2
            ts = [t + _dot3_shared([t], pw)[0] for t, pw in zip(ts, pws)]

        sts = [st_ref[pi] for pi in range(npairs)]
        fss = [_dot_nt(jnp.concatenate([p["ab"], p["rb"]], axis=0), st) for p, st in zip(ps, sts)]
        vss = [stack(p["v"]) for p in ps]
        rhss = [stack(fs[:c]) + _dot(p["a_ak"], vs) for p, fs, vs in zip(ps, fss, vss)]
        uss = [_dot(t, rhs) for t, rhs in zip(ts, rhss)]
        yss = [_dot(p["a_rb"], us) + _dot(p["a_rk"], vs) for p, us, vs in zip(ps, uss, vss)]
        for pi, (p, st, us) in enumerate(zip(ps, sts, uss)):
            u = us[:c] + us[c:]
            upd = _dot_tn(jnp.concatenate([u, p["v"]], axis=0),
                          jnp.concatenate([p["bh"], p["khat"]], axis=0))
            st_ref[pi] = st * jnp.exp(p["gl"]) + upd * hones_ref[...]
        inv_n = 1.0 / C_HEAD
        ys_ = [fs[c:] + ys[:c] + ys[c:] for fs, ys in zip(fss, yss)]
        sums = rowsums(ys_ + [p["r"] * p["kh"] * rk_ref[:, p["cols"]] for p in ps])
        dlts = [y - m * inv_n for y, m in zip(ys_, sums[:npairs])]
        vars_ = rowsums([d * d for d in dlts])
        for p, dlt, var, bsum in zip(ps, dlts, vars_, sums[npairs:]):
            cols = p["cols"]
            yn = dlt * lax.rsqrt(var * inv_n + GN_EPS) * lg_ref[:, cols] + lb_ref[:, cols]
            y_ref[rows, cols] = ((yn + bsum * p["v"]) * g_ref[rows, cols]).astype(y_ref.dtype)
        return carry

    lax.fori_loop(0, nchunks, body, 0)

    @pl.when(l_idx == pl.num_programs(2) - 1)
    def _():
        for pi in range(npairs):
            st = st_ref[pi]
            s_ref[0, 2 * pi] = st[:C_HEAD, :C_HEAD]
            s_ref[0, 2 * pi + 1] = st[C_HEAD:, C_HEAD:]


def rwkv_core(rkv, mid, low2, kk_p, ka_p, rk_p, lnx_g, lnx_b, batch, seqlen, layer,
              s0=None, v_first=None, npairs=8):
    _, m, d = rkv.shape
    c = math.gcd(seqlen, RWKV_CHUNK)
    lblk = _pick_tile(seqlen, (512, 256, 128, 64, 32, 16, 8))
    nl_blocks = seqlen // lblk
    has_state = s0 is not None
    has_vres = v_first is not None
    if c * 4 <= RWKV_CHUNK:
        npairs = C_HEADS // 2
    w = npairs * LANES
    ngroups = d // w
    midw = mid.shape[1]
    strict, incl, eye, hones = (jnp.asarray(x) for x in _rwkv_consts(c))
    row = lambda b, p, l: b * nl_blocks + l
    blk3 = lambda which: pl.BlockSpec((1, lblk, w), lambda b, p, l: (which, row(b, p, l), p))
    blk2 = pl.BlockSpec((lblk, w), lambda b, p, l: (row(b, p, l), p))
    par = pl.BlockSpec((1, w), lambda b, p, l: (0, p))
    cols = lambda arr: pl.BlockSpec((arr.shape[0], w), lambda b, p, l: (0, p))
    full = lambda arr: pl.BlockSpec(arr.shape, lambda b, p, l: (0,) * arr.ndim)
    w0, w2, a0, a2, g2 = low2[:5]
    in_specs = [blk3(0), blk3(1), blk3(2), pl.BlockSpec((lblk, midw), lambda b, p, l: (row(b, p, l), 0)),
                par, cols(w2), par, cols(a2), cols(g2)]
    args = [rkv, rkv, rkv, mid, w0.reshape(1, d), w2, a0.reshape(1, d), a2, g2]
    if has_vres:
        v0, v2 = low2[5:]
        in_specs += [blk3(2), par, cols(v2)]
        args += [v_first, v0.reshape(1, d), v2]
    in_specs += [par] * 5
    args += [x.reshape(1, d) for x in (kk_p, ka_p, rk_p, lnx_g, lnx_b)]
    in_specs += [full(x) for x in (strict, incl, eye, hones)]
    args += [strict, incl, eye, hones]
    st_blk = pl.BlockSpec((1, 2 * npairs, C_HEAD, C_HEAD), lambda b, p, l: (b, p, 0, 0))
    if has_state:
        in_specs.append(pl.BlockSpec((None, 1, 2 * npairs, C_HEAD, C_HEAD),
                                     lambda b, p, l: (layer, b, p, 0, 0)))
        args.append(s0)
    kern = functools.partial(_rwkv_core_kernel, chunk=c, nchunks=lblk // c, npairs=npairs,
                             has_state=has_state, has_vres=has_vres)
    return pl.pallas_call(
        kern,
        grid=(batch, ngroups, nl_blocks),
        in_specs=in_specs,
        out_specs=[blk2, st_blk],
        out_shape=[jax.ShapeDtypeStruct((m, d), BF16 if lblk % 16 == 0 else F32),
                   jax.ShapeDtypeStruct((batch, C_HEADS, C_HEAD, C_HEAD), F32)],
        scratch_shapes=[pltpu.VMEM((npairs, LANES, LANES), F32)]
        + [pltpu.VMEM((lblk, w), F32)] * (4 if has_vres else 3),
        compiler_params=_cparams(("parallel", "parallel", "arbitrary")),
        name="rwkv_core",
    )(*args)


def _even_layer(x, batch, seqlen, e, layer, p, st_hgrn, k_cache, v_cache):
    proj = norm_matmul(x, p["norm_mix_pre"][layer], p["w_in_even"], e)
    o_a, s_new = hgrn(proj, p["hgrn_lb_raw"], e, batch, seqlen, st_hgrn)
    k_lo = IN_A + B_WIDTH
    new_rows = min(seqlen, WINDOW)
    tails = jnp.stack([proj[(b + 1) * seqlen - new_rows:(b + 1) * seqlen, k_lo:] for b in range(batch)])
    kb = tails[:, :, :B_KV_WIDTH].reshape(batch, new_rows, B_KV_HEADS, B_HEAD_DIM)
    vb = tails[:, :, B_KV_WIDTH:].reshape(batch, new_rows, B_KV_HEADS, B_HEAD_DIM)
    o_b = swa(proj, batch, seqlen, p["rel_bias"], p["attn_sinks"][e], e, k_cache, v_cache)
    if k_cache is None:
        k_new, v_new = kb, vb
    else:
        k_new = jnp.concatenate([k_cache[e, :, new_rows:], kb], axis=1)
        v_new = jnp.concatenate([v_cache[e, :, new_rows:], vb], axis=1)
    x = even_out(o_a, proj, o_b, x, p["hgrn_norm_g"][e], p["w_out_even"], p["norm_mix_post"][layer], e)
    return x, s_new, k_new, v_new


def _odd_layer(x, batch, seqlen, o, layer, p, shift0, s0, v_first):
    m, d = x.shape
    g_pre = p["norm_mix_pre"][layer]
    has_vres = o > 0
    width = lambda rng: rng[1] - rng[0]
    w1p, w2p = _pad_lora(p["rw_w1"][o], p["rw_w2"][o], width(LORA_W))
    a1p, a2p = _pad_lora(p["rw_a1"][o], p["rw_a2"][o], width(LORA_A))
    g1p, g2p = _pad_lora(p["rw_g1"][o], p["rw_g2"][o], width(LORA_G))
    first, low2 = [w1p, a1p, g1p], [p["rw_w0"][o], w2p, p["rw_a0"][o], a2p, g2p]
    if has_vres:
        v1p, v2p = _pad_lora(p["rw_v1"][o - 1], p["rw_v2"][o - 1], width(LORA_V))
        first.append(v1p)
        low2 += [p["rw_v0"][o - 1], v2p]
    rkv, mid = rwkv_in(x, g_pre, shift0, batch, seqlen, p["rw_mu"][o], p["w_rkv"], o,
                       jnp.concatenate(first, axis=1), has_vres)
    yg, s_new = rwkv_core(rkv, mid, low2, p["rw_kk"][o], p["rw_ka"][o], p["rw_rk"][o],
                          p["rw_lnx_g"][o], p["rw_lnx_b"][o], batch, seqlen, o, s0,
                          v_first if has_vres else None)
    shift_new = rmsnorm_rows(x.reshape(batch, seqlen, d)[:, -1], g_pre)
    x = odd_out(yg, x, p["rw_wo"], p["norm_mix_post"][layer], o)
    return x, s_new, shift_new, rkv


def _trunk(x3, st_hgrn, k_cache, v_cache, st_rwkv, st_shift, p):
    batch, seqlen, d = x3.shape
    x = x3.reshape(batch * seqlen, d)
    has_state = st_hgrn is not None
    hgrn_out, k_out, v_out, rwkv_out, shift_out = [], [], [], [], []
    v_first = None
    for layer in range(DEPTH):
        if layer % 2 == 0:
            e = layer // 2
            x, s_new, k_new, v_new = _even_layer(x, batch, seqlen, e, layer, p, st_hgrn, k_cache, v_cache)
            hgrn_out.append(s_new)
            k_out.append(k_new)
            v_out.append(v_new)
        else:
            o = layer // 2
            x, s_new, sh_new, rkv = _odd_layer(
                x, batch, seqlen, o, layer, p,
                st_shift[o] if has_state else None,
                st_rwkv,
                v_first)
            if o == 0:
                v_first = rkv
            rwkv_out.append(s_new)
            shift_out.append(sh_new)
        x = ffn(x, p["norm_ffn_pre"][layer], p["w_up"], p["w_down"], p["norm_ffn_post"][layer], layer)
    return (x.reshape(batch, seqlen, d), jnp.stack(hgrn_out), jnp.stack(k_out), jnp.stack(v_out),
            jnp.stack(rwkv_out), jnp.stack(shift_out))


def kernel(x_prompt, x_sample, state_hgrn, cache_swa_k, cache_swa_v, state_rwkv, state_shift,
           norm_mix_pre, norm_mix_post, norm_ffn_pre, norm_ffn_post,
           w_in_even, w_out_even, hgrn_lb_raw, hgrn_norm_g, rel_bias, attn_sinks,
           rw_mu, rw_wr, rw_wk, rw_wv, rw_wo, rw_w0, rw_w1, rw_w2, rw_a0, rw_a1, rw_a2,
           rw_v0, rw_v1, rw_v2, rw_g1, rw_g2, rw_kk, rw_ka, rw_rk, rw_lnx_g, rw_lnx_b,
           w_up, w_down):
    p = {
        "norm_mix_pre": norm_mix_pre, "norm_mix_post": norm_mix_post,
        "norm_ffn_pre": norm_ffn_pre, "norm_ffn_post": norm_ffn_post,
        "w_in_even": w_in_even.astype(BF16), "w_out_even": w_out_even.astype(BF16),
        "hgrn_lb_raw": hgrn_lb_raw, "hgrn_norm_g": hgrn_norm_g,
        "rel_bias": rel_bias, "attn_sinks": attn_sinks,
        "rw_mu": rw_mu, "w_rkv": jnp.stack([rw_wr, rw_wk, rw_wv], axis=1).astype(BF16),
        "rw_wo": rw_wo.astype(BF16),
        "rw_w0": rw_w0, "rw_w1": rw_w1, "rw_w2": rw_w2, "rw_a0": rw_a0, "rw_a1": rw_a1, "rw_a2": rw_a2,
        "rw_v0": rw_v0, "rw_v1": rw_v1, "rw_v2": rw_v2, "rw_g1": rw_g1, "rw_g2": rw_g2,
        "rw_kk": rw_kk, "rw_ka": rw_ka, "rw_rk": rw_rk, "rw_lnx_g": rw_lnx_g, "rw_lnx_b": rw_lnx_b,
        "w_up": w_up.astype(BF16), "w_down": w_down.astype(BF16),
    }
    y_p, hgrn_p, k_p, v_p, rwkv_p, shift_p = _trunk(x_prompt, None, None, None, None, None, p)
    y_s, hgrn_s, k_s, v_s, rwkv_s, shift_s = _trunk(
        x_sample, state_hgrn, cache_swa_k, cache_swa_v, state_rwkv, state_shift, p)
    return (y_p, y_s, hgrn_p, hgrn_s, k_p, k_s, v_p, v_s, rwkv_p, rwkv_s, shift_p, shift_s)
```

```python
import functools
import math

import numpy as np
import jax
import jax.numpy as jnp
from jax import lax
from jax.experimental import pallas as pl
from jax.experimental.pallas import tpu as pltpu

F32 = jnp.float32
BF16 = jnp.bfloat16

D_MODEL = 2048
DEPTH = 4
N_EVEN = 2
N_ODD = 2
A_HEADS = 8
A_KDIM = 128
A_VDIM = 128
A_WIDTH = 1024
A_QK = 1024
B_HEADS = 16
B_HEAD_DIM = 64
B_KV_HEADS = 4
B_GROUP = 4
B_WIDTH = 1024
B_KV_WIDTH = 256
WINDOW = 128
N_BUCKETS = 32
MAX_DISTANCE = 128
MASK_VALUE = -1e30
IN_A = 4096
IN_EVEN = 5632
C_HEAD = 64
C_HEADS = 32
GN_EPS = 64e-5
D_FF = 8192
NORM_EPS = 1e-6

LANES = 128
VMEM_LIMIT = 56 * 1024 * 1024

HGRN_CHUNK = 128
RWKV_CHUNK = 64


def _cparams(sem):
    return pltpu.CompilerParams(dimension_semantics=sem, vmem_limit_bytes=VMEM_LIMIT)


def _rms(x, g):
    return x * lax.rsqrt(jnp.mean(x * x, axis=-1, keepdims=True) + NORM_EPS) * g


def _dot(a, b):
    return jnp.dot(a.astype(BF16), b.astype(BF16), preferred_element_type=F32)


def _dot_nt(a, b):
    return lax.dot_general(a.astype(BF16), b.astype(BF16), (((1,), (1,)), ((), ())),
                           preferred_element_type=F32)


def _dot_tn(a, b):
    return lax.dot_general(a.astype(BF16), b.astype(BF16), (((0,), (0,)), ((), ())),
                           preferred_element_type=F32)


def _split2(x):
    hi = x.astype(BF16)
    return hi, (x - hi.astype(F32)).astype(BF16)


def _dot3_shared(lhs_list, b):
    bh, bl = _split2(b)
    parts = [_split2(a) for a in lhs_list]
    his = [p[0] for p in parts]
    los = [p[1] for p in parts]
    by_hi = jnp.dot(jnp.concatenate(his + los, axis=0), bh, preferred_element_type=F32)
    by_lo = jnp.dot(jnp.concatenate(his, axis=0), bl, preferred_element_type=F32)
    n = sum(a.shape[0] for a in lhs_list)
    out, off = [], 0
    for a in lhs_list:
        m = a.shape[0]
        out.append(by_hi[off:off + m] + by_hi[n + off:n + off + m] + by_lo[off:off + m])
        off += m
    return out


def _pick_tile(m, cands):
    for c in cands:
        if m % c == 0:
            return c
    return m


def _norm_matmul_kernel(x_ref, g_ref, w_ref, o_ref, xn_ref):
    @pl.when(pl.program_id(1) == 0)
    def _():
        xn_ref[...] = _rms(x_ref[...], g_ref[...]).astype(BF16)

    o_ref[...] = jnp.dot(xn_ref[...], w_ref[...], preferred_element_type=F32)


def norm_matmul(x, g, w_stack_bf16, li, tn=1408):
    m, d = x.shape
    n = w_stack_bf16.shape[2]
    tm = _pick_tile(m, (1024, 512, 256, 128, 64, 32, 16, 8))
    return pl.pallas_call(
        _norm_matmul_kernel,
        grid=(m // tm, n // tn),
        in_specs=[pl.BlockSpec((tm, d), lambda i, j: (i, 0)),
                  pl.BlockSpec((1, d), lambda i, j: (0, 0)),
                  pl.BlockSpec((None, d, tn), lambda i, j: (li, 0, j))],
        out_specs=pl.BlockSpec((tm, tn), lambda i, j: (i, j)),
        out_shape=jax.ShapeDtypeStruct((m, n), F32),
        scratch_shapes=[pltpu.VMEM((tm, d), BF16)],
        compiler_params=_cparams(("parallel", "arbitrary")),
        name="norm_matmul",
    )(x, g.reshape(1, d), w_stack_bf16)


def _ffn_kernel(*refs, cast_next):
    if cast_next:
        (x_ref, gpre_ref, wup_ref, wdn_ref, gpost_ref, nup_ref, ndn_ref,
         o_ref, nup_out, ndn_out, xn_ref, acc_ref) = refs
        nup_out[...] = nup_ref[...].astype(BF16)
        ndn_out[...] = ndn_ref[...].astype(BF16)
    else:
        x_ref, gpre_ref, wup_ref, wdn_ref, gpost_ref, o_ref, xn_ref, acc_ref = refs
    f = pl.program_id(1)

    @pl.when(f == 0)
    def _():
        xn_ref[...] = _rms(x_ref[...], gpre_ref[...]).astype(BF16)
        acc_ref[...] = jnp.zeros_like(acc_ref)

    h = jnp.dot(xn_ref[...], wup_ref[...], preferred_element_type=F32)
    h = jnp.square(jnp.maximum(h, 0.0)).astype(BF16)
    acc_ref[...] += jnp.dot(h, wdn_ref[...], preferred_element_type=F32)

    @pl.when(f == pl.num_programs(1) - 1)
    def _():
        o_ref[...] = x_ref[...] + _rms(acc_ref[...], gpost_ref[...])


def ffn_cast_blocks(m, d, dff, tf=1024):
    tm = _pick_tile(m, (512, 256, 128, 64, 32, 16, 8))
    nsteps = (m // tm) * (dff // tf)
    ncol = dff // LANES
    if nsteps % ncol or d % (nsteps // ncol) or dff % nsteps:
        return None
    return (d // (nsteps // ncol), LANES), (dff // nsteps, d)


def ffn(x, gpre, wup_bf16, wdn_bf16, gpost, cast_next=None, tf=1024):
    m, d = x.shape
    dff = wup_bf16.shape[1]
    tm = _pick_tile(m, (512, 256, 128, 64, 32, 16, 8))
    nf = dff // tf
    in_specs = [pl.BlockSpec((tm, d), lambda i, f: (i, 0)),
                pl.BlockSpec((1, d), lambda i, f: (0, 0)),
                pl.BlockSpec((d, tf), lambda i, f: (0, f)),
                pl.BlockSpec((tf, d), lambda i, f: (f, 0)),
                pl.BlockSpec((1, d), lambda i, f: (0, 0))]
    args = [x, gpre.reshape(1, d), wup_bf16, wdn_bf16, gpost.reshape(1, d)]
    out_specs = [pl.BlockSpec((tm, d), lambda i, f: (i, 0))]
    out_shape = [jax.ShapeDtypeStruct((m, d), F32)]
    if cast_next is not None:
        up_f32, dn_f32, ln = cast_next
        (ur, uc), (dr, dc) = ffn_cast_blocks(m, d, dff, tf)
        nrow = d // ur
        up_map = lambda i, f: ((i * nf + f) % nrow, (i * nf + f) // nrow)
        dn_map = lambda i, f: (i * nf + f, 0)
        in_specs += [pl.BlockSpec((None, ur, uc), lambda i, f: (ln,) + up_map(i, f)),
                     pl.BlockSpec((None, dr, dc), lambda i, f: (ln,) + dn_map(i, f))]
        args += [up_f32, dn_f32]
        out_specs += [pl.BlockSpec((ur, uc), up_map), pl.BlockSpec((dr, dc), dn_map)]
        out_shape += [jax.ShapeDtypeStruct((d, dff), BF16), jax.ShapeDtypeStruct((dff, d), BF16)]
    outs = pl.pallas_call(
        functools.partial(_ffn_kernel, cast_next=cast_next is not None),
        grid=(m // tm, nf),
        in_specs=in_specs,
        out_specs=out_specs,
        out_shape=out_shape,
        scratch_shapes=[pltpu.VMEM((tm, d), BF16), pltpu.VMEM((tm, d), F32)],
        compiler_params=_cparams(("parallel", "arbitrary")),
        name="ffn",
    )(*args)
    return outs[0], tuple(outs[1:])


def _even_out_kernel(oa_ref, ga_ref, ob_ref, x_ref, ag_ref, w_ref, gpost_ref, o_ref):
    ga = ga_ref[...]
    oan = _rms(oa_ref[...], ag_ref[...]) * (ga * jax.nn.sigmoid(ga))
    mix = (jnp.dot(oan.astype(BF16), w_ref[:A_WIDTH, :], preferred_element_type=F32)
           + jnp.dot(ob_ref[...].astype(BF16), w_ref[A_WIDTH:, :], preferred_element_type=F32))
    o_ref[...] = x_ref[...] + _rms(mix, gpost_ref[...])


def even_out(o_a, proj, o_b, x, a_norm_g, w_out_bf16, gpost, li):
    m, d = x.shape
    tm = _pick_tile(m, (384, 256, 128, 64, 32, 16, 8))
    ga_blk = (3 * A_WIDTH) // A_WIDTH
    return pl.pallas_call(
        _even_out_kernel,
        grid=(m // tm,),
        in_specs=[pl.BlockSpec((tm, A_WIDTH), lambda i: (i, 0)),
                  pl.BlockSpec((tm, A_WIDTH), lambda i: (i, ga_blk)),
                  pl.BlockSpec((tm, B_WIDTH), lambda i: (i, 0)),
                  pl.BlockSpec((tm, d), lambda i: (i, 0)),
                  pl.BlockSpec((1, A_WIDTH), lambda i: (0, 0)),
                  pl.BlockSpec((None, A_WIDTH + B_WIDTH, d), lambda i: (li, 0, 0)),
                  pl.BlockSpec((1, d), lambda i: (0, 0))],
        out_specs=pl.BlockSpec((tm, d), lambda i: (i, 0)),
        out_shape=jax.ShapeDtypeStruct((m, d), F32),
        compiler_params=_cparams(("parallel",)),
        name="even_out",
    )(o_a, proj, o_b, x, a_norm_g.reshape(1, A_WIDTH), w_out_bf16, gpost.reshape(1, d))


def _odd_out_kernel(y_ref, x_ref, w_ref, gpost_ref, o_ref):
    mix = jnp.dot(y_ref[...].astype(BF16), w_ref[...], preferred_element_type=F32)
    o_ref[...] = x_ref[...] + _rms(mix, gpost_ref[...])


def odd_out(yg, x, wo_bf16, gpost, li):
    m, d = x.shape
    tm = _pick_tile(m, (384, 256, 128, 64, 32, 16, 8))
    return pl.pallas_call(
        _odd_out_kernel,
        grid=(m // tm,),
        in_specs=[pl.BlockSpec((tm, d), lambda i: (i, 0)),
                  pl.BlockSpec((tm, d), lambda i: (i, 0)),
                  pl.BlockSpec((None, d, d), lambda i: (li, 0, 0)),
                  pl.BlockSpec((1, d), lambda i: (0, 0))],
        out_specs=pl.BlockSpec((tm, d), lambda i: (i, 0)),
        out_shape=jax.ShapeDtypeStruct((m, d), F32),
        compiler_params=_cparams(("parallel",)),
        name="odd_out",
    )(yg, x, wo_bf16, gpost.reshape(1, d))


def _level_consts(c):
    levels = []
    s = c // 2
    while s >= 1:
        levels.append(s)
        s //= 2
    mask = np.zeros((len(levels), c, c), np.float32)
    idx = np.arange(c)
    for l, s in enumerate(levels):
        same = (idx[:, None] // (2 * s)) == (idx[None, :] // (2 * s))
        upper = (idx[:, None] % (2 * s)) >= s
        lower = (idx[None, :] % (2 * s)) < s
        mask[l] = (same & upper & lower).astype(np.float32)
    return levels, mask


def _split_rows(g, s, rowid):
    c = g.shape[0]
    if 2 * s >= 8:
        return jnp.concatenate(
            [jnp.broadcast_to(g[b + s - 1:b + s, :], (2 * s, g.shape[1])) for b in range(0, c, 2 * s)], axis=0)
    r = rowid % (2 * s)
    out = g
    for off in range(-(s - 1), s + 1):
        if off != 0:
            out = jnp.where(r == s - 1 + off, pltpu.roll(g, off % c, 0), out)
    return out


def _hgrn_kernel(*refs, layer, chunk, nchunks, levels, has_state, nheads):
    if has_state:
        q_ref, f_ref, i_ref, lb_ref, mask_ref, s0_ref, o_ref, s_ref, st_ref = refs
    else:
        q_ref, f_ref, i_ref, lb_ref, mask_ref, o_ref, s_ref, st_ref = refs
    c = chunk
    nh = nheads
    rowid = lax.broadcasted_iota(jnp.int32, (c, nh * LANES), 0)
    l_idx = pl.program_id(2)

    @pl.when(l_idx == 0)
    def _():
        for hi in range(nh):
            if has_state:
                st_ref[hi] = s0_ref[0, hi].T
            else:
                st_ref[hi] = jnp.zeros((A_VDIM, A_KDIM), F32)

    lbr = lb_ref[...]
    e = jnp.exp(lbr - jnp.max(lbr, axis=0, keepdims=True))
    p = e / jnp.sum(e, axis=0, keepdims=True)
    lb = jnp.zeros((1, nh * LANES), F32)
    for i in range(1, layer + 1):
        lb = lb + p[i:i + 1, :]
    one_m_lb = 1.0 - lb
    head = lambda x, hi: x[:, hi * LANES:(hi + 1) * LANES]

    def body(ci, carry):
        rows = pl.ds(pl.multiple_of(ci * c, c), c)
        fq = f_ref[rows, :]
        qr = q_ref[rows, :]
        v = i_ref[rows, :]
        f = lb + one_m_lb * jax.nn.sigmoid(fq)
        k = 1.0 - f
        q = qr * jax.nn.sigmoid(qr) * (A_KDIM ** -0.5)

        g = jnp.log2(f)
        sft = 1
        while sft < c:
            g = g + jnp.where(rowid >= sft, pltpu.roll(g, sft, 0), 0.0)
            sft *= 2
        glast = g[c - 1:c, :]
        q_in = q * jnp.exp2(g)
        kd = k * jnp.exp2(glast - g)
        dec = jnp.exp2(glast)
        diag = q * k

        sts = [st_ref[hi] for hi in range(nh)]
        os_ = [_dot_nt(head(q_in, hi), sts[hi]) for hi in range(nh)]
        attns = [jnp.zeros((c, c), F32) for _ in range(nh)]
        for l, s in enumerate(levels):
            e = jnp.exp2(-jnp.abs(g - _split_rows(g, s, rowid)))
            qs = q * e
            ks = k * e
            ml = mask_ref[l]
            attns = [at + ml * _dot_nt(head(qs, hi), head(ks, hi)) for hi, at in enumerate(attns)]
        for hi in range(nh):
            vh = head(v, hi)
            o = os_[hi] + _dot(attns[hi], vh) + jnp.sum(head(diag, hi), axis=-1, keepdims=True) * vh
            o_ref[rows, hi * LANES:(hi + 1) * LANES] = o
        for hi in range(nh):
            st_ref[hi] = sts[hi] * head(dec, hi) + _dot_tn(head(v, hi), head(kd, hi))
        return carry

    lax.fori_loop(0, nchunks, body, 0, unroll=2 if nchunks % 2 == 0 else 1)

    @pl.when(l_idx == pl.num_programs(2) - 1)
    def _():
        for hi in range(nh):
            s_ref[0, hi] = st_ref[hi].T


def hgrn(proj, lb_raw, layer, batch, seqlen, s0=None, nheads=4):
    m = proj.shape[0]
    c = math.gcd(seqlen, HGRN_CHUNK)
    lblk = _pick_tile(seqlen, (512, 256, 128, 64, 32, 16, 8))
    nl_blocks = seqlen // lblk
    levels, mask = _level_consts(c)
    has_state = s0 is not None
    if c * 4 <= HGRN_CHUNK:
        nheads = A_HEADS
    w = nheads * LANES
    ngroups = A_HEADS // nheads
    kern = functools.partial(_hgrn_kernel, layer=layer, chunk=c, nchunks=lblk // c,
                             levels=tuple(levels), has_state=has_state, nheads=nheads)
    row = lambda b, h, l: b * nl_blocks + l
    in_specs = [pl.BlockSpec((lblk, w), lambda b, h, l: (row(b, h, l), h)),
                pl.BlockSpec((lblk, w), lambda b, h, l: (row(b, h, l), ngroups + h)),
                pl.BlockSpec((lblk, w), lambda b, h, l: (row(b, h, l), 2 * ngroups + h)),
                pl.BlockSpec((N_EVEN, w), lambda b, h, l: (0, h)),
                pl.BlockSpec((len(levels), c, c), lambda b, h, l: (0, 0, 0))]
    args = [proj, proj, proj, lb_raw, jnp.asarray(mask)]
    st_blk = pl.BlockSpec((1, nheads, A_KDIM, A_VDIM), lambda b, h, l: (b, h, 0, 0))
    if has_state:
        in_specs.append(pl.BlockSpec((None, 1, nheads, A_KDIM, A_VDIM), lambda b, h, l: (layer, b, h, 0, 0)))
        args.append(s0)
    return pl.pallas_call(
        kern,
        grid=(batch, ngroups, nl_blocks),
        in_specs=in_specs,
        out_specs=[pl.BlockSpec((lblk, w), lambda b, h, l: (row(b, h, l), h)), st_blk],
        out_shape=[jax.ShapeDtypeStruct((m, A_WIDTH), F32),
                   jax.ShapeDtypeStruct((batch, A_HEADS, A_KDIM, A_VDIM), F32)],
        scratch_shapes=[pltpu.VMEM((nheads, A_VDIM, A_KDIM), F32)],
        compiler_params=_cparams(("parallel", "parallel", "arbitrary")),
        name="hgrn",
    )(*args)


def _t5_bucket(dist):
    max_exact = N_BUCKETS // 2
    d = np.maximum(dist, 0)
    large = max_exact + (np.log(np.maximum(d, max_exact).astype(np.float32) / max_exact)
                         / math.log(MAX_DISTANCE / max_exact) * (N_BUCKETS - max_exact)).astype(np.int32)
    large = np.minimum(large, N_BUCKETS - 1)
    return np.where(d < max_exact, d, large).astype(np.int32)


def _swa_kernel(q_ref, kp_ref, kc_ref, vp_ref, vc_ref, bucket_ref, band_ref, rb_ref, sink_ref,
                o_ref, bias_ref, *, qb, span, prev_always_valid):
    first = (pl.program_id(0) == 0) & (pl.program_id(1) == 0)

    @pl.when(first)
    def _():
        bk = bucket_ref[...]
        band = band_ref[...]

        def per_head(h, carry):
            def per_bucket(bi, acc):
                return jnp.where(bk == bi, rb_ref[bi, h], acc)
            acc = lax.fori_loop(0, N_BUCKETS, per_bucket, jnp.zeros((qb, span), F32))
            bias_ref[h] = jnp.where(band > 0, acc, MASK_VALUE)
            return carry

        lax.fori_loop(0, B_HEADS, per_head, 0)

    scale = B_HEAD_DIM ** -0.5
    q = q_ref[...]
    kall = jnp.concatenate([kp_ref[...], kc_ref[...]], axis=0)
    vall = jnp.concatenate([vp_ref[...], vc_ref[...]], axis=0)
    if not prev_always_valid:
        col = lax.broadcasted_iota(jnp.int32, (qb, span), 1)
        no_prev = (col < WINDOW) & (pl.program_id(1) == 0)
    heads = range(B_HEADS)
    ks = [kall[:, kh * B_HEAD_DIM:(kh + 1) * B_HEAD_DIM].astype(BF16) for kh in range(B_KV_HEADS)]
    vs = [vall[:, kh * B_HEAD_DIM:(kh + 1) * B_HEAD_DIM].astype(BF16) for kh in range(B_KV_HEADS)]
    qs = [(q[:, h * B_HEAD_DIM:(h + 1) * B_HEAD_DIM] * scale).astype(BF16) for h in heads]
    ss = [_dot_nt(qs[h], ks[h // B_GROUP]) + bias_ref[h] for h in heads]
    if not prev_always_valid:
        ss = [jnp.where(no_prev, MASK_VALUE, s) for s in ss]
    ms = [jnp.maximum(jnp.max(ss[h], axis=-1, keepdims=True), sink_ref[h]) for h in heads]
    ps = [jnp.exp(s - m) for s, m in zip(ss, ms)]
    denoms = [jnp.sum(ps[h], axis=-1, keepdims=True) + jnp.exp(sink_ref[h] - ms[h]) for h in heads]
    outs = [_dot(ps[h], vs[h // B_GROUP]) / denoms[h] for h in heads]
    o_ref[...] = jnp.concatenate(outs, axis=1).astype(o_ref.dtype)


def swa(proj, batch, seqlen, rel_bias, sinks, layer, k_past=None, v_past=None):
    m = proj.shape[0]
    has_cache = k_past is not None
    qb = math.gcd(seqlen, WINDOW)
    nb = seqlen // qb
    span = WINDOW + qb
    dist = np.arange(qb)[:, None] + WINDOW - np.arange(span)[None, :]
    band = ((dist >= 0) & (dist < WINDOW)).astype(np.float32)
    bucket = _t5_bucket(dist)
    q_col = IN_A // B_WIDTH
    k_col = (IN_A + B_WIDTH) // B_KV_WIDTH
    v_col = k_col + 1
    cur = lambda c: (lambda b, n: (b * nb + n, c))
    if has_cache:
        assert nb == 1
        prev_k = pl.BlockSpec((None, WINDOW, B_KV_WIDTH), lambda b, n: (layer, b, 0))
        prev_v = pl.BlockSpec((None, WINDOW, B_KV_WIDTH), lambda b, n: (layer, b, 0))
        kp_arr = k_past.reshape(k_past.shape[0], batch * WINDOW, B_KV_WIDTH)
        vp_arr = v_past.reshape(v_past.shape[0], batch * WINDOW, B_KV_WIDTH)
    else:
        assert qb == WINDOW
        prev = lambda c: (lambda b, n: (b * nb + jnp.maximum(n - 1, 0), c))
        prev_k = pl.BlockSpec((WINDOW, B_KV_WIDTH), prev(k_col))
        prev_v = pl.BlockSpec((WINDOW, B_KV_WIDTH), prev(v_col))
        kp_arr, vp_arr = proj, proj
    kern = functools.partial(_swa_kernel, qb=qb, span=span, prev_always_valid=has_cache)
    return pl.pallas_call(
        kern,
        grid=(batch, nb),
        in_specs=[pl.BlockSpec((qb, B_WIDTH), cur(q_col)),
                  prev_k,
                  pl.BlockSpec((qb, B_KV_WIDTH), cur(k_col)),
                  prev_v,
                  pl.BlockSpec((qb, B_KV_WIDTH), cur(v_col)),
                  pl.BlockSpec((qb, span), lambda b, n: (0, 0)),
                  pl.BlockSpec((qb, span), lambda b, n: (0, 0)),
                  pl.BlockSpec(memory_space=pltpu.SMEM),
                  pl.BlockSpec(memory_space=pltpu.SMEM)],
        out_specs=pl.BlockSpec((qb, B_WIDTH), lambda b, n: (b * nb + n, 0)),
        out_shape=jax.ShapeDtypeStruct((m, B_WIDTH), BF16 if qb % 16 == 0 else F32),
        scratch_shapes=[pltpu.VMEM((B_HEADS, qb, span), F32)],
        compiler_params=_cparams(("arbitrary", "arbitrary")),
        name="swa",
    )(proj, kp_arr, proj, vp_arr, proj, jnp.asarray(bucket), jnp.asarray(band), rel_bias, sinks)


def _rmsnorm_kernel(x_ref, g_ref, o_ref):
    o_ref[...] = _rms(x_ref[...], g_ref[...])


def rmsnorm_rows(x, g):
    m, d = x.shape
    tm = _pick_tile(m, (512, 256, 128, 64, 32, 16, 8))
    return pl.pallas_call(
        _rmsnorm_kernel,
        grid=(m // tm,),
        in_specs=[pl.BlockSpec((tm, d), lambda i: (i, 0)), pl.BlockSpec((1, d), lambda i: (0, 0))],
        out_specs=pl.BlockSpec((tm, d), lambda i: (i, 0)),
        out_shape=jax.ShapeDtypeStruct((m, d), F32),
        compiler_params=_cparams(("parallel",)),
        name="rmsnorm",
    )(x, g.reshape(1, d))


LORA_W = (0, 128)
LORA_A = (128, 256)
LORA_G = (256, 512)
LORA_V = (512, 640)


def _rwkv_in_kernel(x_ref, xp_ref, s_ref, g_ref, mu3_ref, mul_ref, w_ref, w1_ref,
                    rkv_ref, mid_ref, h_s, hp_s, *, tm, seqlen, has_vres):
    ph = pl.program_id(1)

    @pl.when(ph == 0)
    def _():
        g = g_ref[...]
        h = _rms(x_ref[...], g)
        rowid = lax.broadcasted_iota(jnp.int32, h.shape, 0)
        rolled = pltpu.roll(h, 1, 0)
        if seqlen % tm == 0:
            prev_last = _rms(xp_ref[...], g)[7:8, :]
            at_start = pl.program_id(0) % (seqlen // tm) == 0
            first = jnp.where(at_start, s_ref[...], prev_last)
            hp = jnp.where(rowid == 0, first, rolled)
        else:
            hp = jnp.where(rowid % seqlen == 0, s_ref[...], rolled)
        h_s[...] = h
        hp_s[...] = hp

    @pl.when(ph < 3)
    def _():
        h = h_s[...]
        xm = (h + (hp_s[...] - h) * mu3_ref[0]).astype(BF16)
        rkv_ref[0] = jnp.dot(xm, w_ref[...], preferred_element_type=F32)

    @pl.when(ph == 3)
    def _():
        h = h_s[...]
        xx = hp_s[...] - h
        mix = lambda i: (h + xx * mul_ref[i:i + 1, :]).astype(BF16)
        low = lambda i, rng: jnp.dot(mix(i), w1_ref[:, rng[0]:rng[1]], preferred_element_type=F32)
        parts = [jnp.tanh(low(0, LORA_W)), low(1, LORA_A), jax.nn.sigmoid(low(2, LORA_G))]
        if has_vres:
            parts.append(low(3, LORA_V))
        mid_ref[...] = jnp.concatenate(parts, axis=1).astype(BF16)


def rwkv_in(x, g, shift0, batch, seqlen, mu, w3_bf16, li, w1cat_bf16, has_vres):
    m, d = x.shape
    tm = next(t for t in (512, 256, 128, 64, 32, 16, 8)
              if m % t == 0 and (seqlen % t == 0 or t % seqlen == 0))
    midw = w1cat_bf16.shape[1]
    mu_rkv = jnp.stack([mu[0], mu[2], mu[3]])[:, None, :]
    mu_low = jnp.stack([mu[1], mu[4], mu[5], mu[3]])
    if shift0 is None:
        shift0 = jnp.zeros((batch, d), F32)
    if seqlen % tm == 0:
        srow = shift0[:, None, :]
        tps = seqlen // tm
        s_spec = pl.BlockSpec((None, 1, d), lambda i, p: (i // tps, 0, 0))
    else:
        srow = jnp.repeat(shift0, seqlen, axis=0)
        s_spec = pl.BlockSpec((tm, d), lambda i, p: (i, 0))
    sub = tm // 8
    kern = functools.partial(_rwkv_in_kernel, tm=tm, seqlen=seqlen, has_vres=has_vres)
    return pl.pallas_call(
        kern,
        grid=(m // tm, 4),
        in_specs=[pl.BlockSpec((tm, d), lambda i, p: (i, 0)),
                  pl.BlockSpec((8, d), lambda i, p: (jnp.maximum(i * sub - 1, 0), 0)),
                  s_spec,
                  pl.BlockSpec((1, d), lambda i, p: (0, 0)),
                  pl.BlockSpec((1, 1, d), lambda i, p: (jnp.minimum(p, 2), 0, 0)),
                  pl.BlockSpec((4, d), lambda i, p: (0, 0)),
                  pl.BlockSpec((None, None, d, d), lambda i, p: (li, jnp.minimum(p, 2), 0, 0)),
                  pl.BlockSpec((d, midw), lambda i, p: (0, 0))],
        out_specs=[pl.BlockSpec((1, tm, d), lambda i, p: (jnp.minimum(p, 2), i, 0)),
                   pl.BlockSpec((tm, midw), lambda i, p: (i, 0))],
        out_shape=[jax.ShapeDtypeStruct((3, m, d), F32),
                   jax.ShapeDtypeStruct((m, midw), BF16)],
        scratch_shapes=[pltpu.VMEM((tm, d), F32), pltpu.VMEM((tm, d), F32)],
        compiler_params=_cparams(("parallel", "arbitrary")),
        name="rwkv_in",
    )(x, x, srow, g.reshape(1, d), mu_rkv, mu_low, w3_bf16, w1cat_bf16)


def _pad_lora(w1, w2, width):
    r = w1.shape[1]
    return (jnp.pad(w1, ((0, 0), (0, width - r))).astype(BF16),
            jnp.pad(w2, ((0, width - r), (0, 0))).astype(BF16))


def _rwkv_consts(c):
    i = np.arange(2 * c)
    same = (i[:, None] // c) == (i[None, :] // c)
    strict = (same & ((i[:, None] % c) > (i[None, :] % c))).astype(np.float32)
    incl = (same & ((i[:, None] % c) >= (i[None, :] % c))).astype(np.float32)
    eye = np.eye(2 * c, dtype=np.float32)
    l = np.arange(LANES)
    headones = ((l[:, None] // C_HEAD) == (l[None, :] // C_HEAD)).astype(np.float32)
    return strict, incl, eye, headones


def _rwkv_core_kernel(*refs, chunk, nchunks, npairs, has_state, has_vres):
    it = iter(refs)
    r_ref, k_ref, v_ref, mid_ref = (next(it) for _ in range(4))
    w0_ref, w2_ref, a0_ref, a2_ref, g2_ref = (next(it) for _ in range(5))
    if has_vres:
        vf_ref, v0_ref, v2_ref = next(it), next(it), next(it)
    kk_ref, ka_ref, rk_ref, lg_ref, lb_ref = (next(it) for _ in range(5))
    strict_ref, incl_ref, eye_ref, hones_ref = (next(it) for _ in range(4))
    if has_state:
        s0_ref = next(it)
    y_ref, s_ref, st_ref, wl_ref, a_ref, g_ref = (next(it) for _ in range(6))
    if has_vres:
        vg_ref = next(it)
    c = chunk
    l_idx = pl.program_id(2)
    lane = lax.broadcasted_iota(jnp.int32, (1, LANES), 1)
    m0 = (lane < C_HEAD).astype(F32)
    m1 = 1.0 - m0

    low = lambda rng, w_ref: jnp.dot(mid_ref[:, rng[0]:rng[1]], w_ref[...], preferred_element_type=F32)
    wl_ref[...] = -math.exp(-0.5) * jax.nn.sigmoid(w0_ref[...] + low(LORA_W, w2_ref))
    a_ref[...] = jax.nn.sigmoid(a0_ref[...] + low(LORA_A, a2_ref))
    g_ref[...] = low(LORA_G, g2_ref)
    if has_vres:
        vg_ref[...] = jax.nn.sigmoid(v0_ref[...] + low(LORA_V, v2_ref))

    @pl.when(l_idx == 0)
    def _():
        for pi in range(npairs):
            if has_state:
                z = jnp.zeros((C_HEAD, C_HEAD), F32)
                top = jnp.concatenate([s0_ref[0, 2 * pi], z], axis=1)
                bot = jnp.concatenate([z, s0_ref[0, 2 * pi + 1]], axis=1)
                st_ref[pi] = jnp.concatenate([top, bot], axis=0)
            else:
                st_ref[pi] = jnp.zeros((LANES, LANES), F32)

    strict = strict_ref[...]
    incl = incl_ref[...]
    eye = eye_ref[...]

    def stack(x):
        return jnp.concatenate([x * m0, x * m1], axis=0)

    first_head = lax.broadcasted_iota(jnp.int32, (c, LANES), 1) < C_HEAD
    rowid = lax.broadcasted_iota(jnp.int32, (c, LANES), 0)

    def rowsums(xs):
        return [jnp.where(first_head,
                          jnp.sum(x * m0, axis=-1, keepdims=True),
                          jnp.sum(x * m1, axis=-1, keepdims=True)) for x in xs]

    def cumsum_rows(x):
        s = 1
        while s < c:
            x = x + jnp.where(rowid >= s, pltpu.roll(x, s, 0), 0.0)
            s *= 2
        return x

    def load(pi, rows):
        cols = slice(pi * LANES, (pi + 1) * LANES)
        k = k_ref[0, rows, cols]
        v = v_ref[0, rows, cols]
        a = a_ref[rows, cols]
        if has_vres:
            v = v + (vf_ref[0, rows, cols] - v) * vg_ref[rows, cols]
        return dict(cols=cols, r=r_ref[0, rows, cols], v=v, a=a, wl=wl_ref[rows, cols],
                    kr=k * kk_ref[:, cols], kh=k * (1.0 + (a - 1.0) * ka_ref[:, cols]))

    def decays(p, ss):
        kk = p["kr"] * lax.rsqrt(jnp.maximum(ss, 1e-24))
        b = kk * p["a"]
        gc = cumsum_rows(p["wl"])
        gl = gc[c - 1:c, :]
        e_neg = jnp.exp(-gc)
        e_out = jnp.exp(gl - gc)
        p.update(gl=gl, ab=-kk * jnp.exp(gc - p["wl"]), rb=p["r"] * jnp.exp(gc),
                 bt=b * e_neg, kt=p["kh"] * e_neg, bh=b * e_out, khat=p["kh"] * e_out)

    def intra(p):
        lhs = jnp.concatenate([stack(p["ab"]), stack(p["rb"])], axis=0)
        with_b = _dot_nt(lhs, jnp.concatenate([p["bt"], p["bt"]], axis=0))
        with_k = _dot_nt(lhs, jnp.concatenate([p["kt"], p["kt"]], axis=0))
        p.update(a_ab=with_b[:2 * c] * strict, a_rb=with_b[2 * c:] * incl,
                 a_ak=with_k[:2 * c] * strict, a_rk=with_k[2 * c:] * incl)

    def body(ci, carry):
        rows = pl.ds(pl.multiple_of(ci * c, c), c)
        ps = [load(pi, rows) for pi in range(npairs)]
        for p, ss in zip(ps, rowsums([p["kr"] * p["kr"] for p in ps])):
            decays(p, ss)
        for p in ps:
            intra(p)

        ts = [eye + p["a_ab"] for p in ps]
        pws = [p["a_ab"] for p in ps]
        if c > 2:
            pws = [_dot3_shared([pw], pw)[0] for pw in pws]
            n = 2
            while 2 * n < c:
                res = [_dot3_shared([t, pw], pw) for t, pw in zip(ts, pws)]
                ts = [t + r[0] for t, r in zip(ts, res)]
                pws = [r[1] for r in res]
                n *= 2
            ts = [t + _dot3_shared([t], pw)[0] for t, pw in zip(ts, pws)]

        sts = [st_ref[pi] for pi in range(npairs)]
        fss = [_dot_nt(jnp.concatenate([p["ab"], p["rb"]], axis=0), st) for p, st in zip(ps, sts)]
        vss = [stack(p["v"]) for p in ps]
        rhss = [stack(fs[:c]) + _dot(p["a_ak"], vs) for p, fs, vs in zip(ps, fss, vss)]
        uss = [_dot(t, rhs) for t, rhs in zip(ts, rhss)]
        yss = [_dot(p["a_rb"], us) + _dot(p["a_rk"], vs) for p, us, vs in zip(ps, uss, vss)]
        for pi, (p, st, us) in enumerate(zip(ps, sts, uss)):
            u = us[:c] + us[c:]
            upd = _dot_tn(jnp.concatenate([u, p["v"]], axis=0),
                          jnp.concatenate([p["bh"], p["khat"]], axis=0))
            st_ref[pi] = st * jnp.exp(p["gl"]) + upd * hones_ref[...]
        inv_n = 1.0 / C_HEAD
        ys_ = [fs[c:] + ys[:c] + ys[c:] for fs, ys in zip(fss, yss)]
        sums = rowsums(ys_ + [p["r"] * p["kh"] * rk_ref[:, p["cols"]] for p in ps])
        dlts = [y - m * inv_n for y, m in zip(ys_, sums[:npairs])]
        vars_ = rowsums([d * d for d in dlts])
        for p, dlt, var, bsum in zip(ps, dlts, vars_, sums[npairs:]):
            cols = p["cols"]
            yn = dlt * lax.rsqrt(var * inv_n + GN_EPS) * lg_ref[:, cols] + lb_ref[:, cols]
            y_ref[rows, cols] = ((yn + bsum * p["v"]) * g_ref[rows, cols]).astype(y_ref.dtype)
        return carry

    lax.fori_loop(0, nchunks, body, 0)

    @pl.when(l_idx == pl.num_programs(2) - 1)
    def _():
        for pi in range(npairs):
            st = st_ref[pi]
            s_ref[0, 2 * pi] = st[:C_HEAD, :C_HEAD]
            s_ref[0, 2 * pi + 1] = st[C_HEAD:, C_HEAD:]


def rwkv_core(rkv, mid, low2, kk_p, ka_p, rk_p, lnx_g, lnx_b, batch, seqlen, layer,
              s0=None, v_first=None, npairs=8):
    _, m, d = rkv.shape
    c = math.gcd(seqlen, RWKV_CHUNK)
    lblk = _pick_tile(seqlen, (512, 256, 128, 64, 32, 16, 8))
    nl_blocks = seqlen // lblk
    has_state = s0 is not None
    has_vres = v_first is not None
    if c * 4 <= RWKV_CHUNK:
        npairs = C_HEADS // 2
    w = npairs * LANES
    ngroups = d // w
    midw = mid.shape[1]
    strict, incl, eye, hones = (jnp.asarray(x) for x in _rwkv_consts(c))
    row = lambda b, p, l: b * nl_blocks + l
    blk3 = lambda which: pl.BlockSpec((1, lblk, w), lambda b, p, l: (which, row(b, p, l), p))
    blk2 = pl.BlockSpec((lblk, w), lambda b, p, l: (row(b, p, l), p))
    par = pl.BlockSpec((1, w), lambda b, p, l: (0, p))
    cols = lambda arr: pl.BlockSpec((arr.shape[0], w), lambda b, p, l: (0, p))
    full = lambda arr: pl.BlockSpec(arr.shape, lambda b, p, l: (0,) * arr.ndim)
    w0, w2, a0, a2, g2 = low2[:5]
    in_specs = [blk3(0), blk3(1), blk3(2), pl.BlockSpec((lblk, midw), lambda b, p, l: (row(b, p, l), 0)),
                par, cols(w2), par, cols(a2), cols(g2)]
    args = [rkv, rkv, rkv, mid, w0.reshape(1, d), w2, a0.reshape(1, d), a2, g2]
    if has_vres:
        v0, v2 = low2[5:]
        in_specs += [blk3(2), par, cols(v2)]
        args += [v_first, v0.reshape(1, d), v2]
    in_specs += [par] * 5
    args += [x.reshape(1, d) for x in (kk_p, ka_p, rk_p, lnx_g, lnx_b)]
    in_specs += [full(x) for x in (strict, incl, eye, hones)]
    args += [strict, incl, eye, hones]
    st_blk = pl.BlockSpec((1, 2 * npairs, C_HEAD, C_HEAD), lambda b, p, l: (b, p, 0, 0))
    if has_state:
        in_specs.append(pl.BlockSpec((None, 1, 2 * npairs, C_HEAD, C_HEAD),
                                     lambda b, p, l: (layer, b, p, 0, 0)))
        args.append(s0)
    kern = functools.partial(_rwkv_core_kernel, chunk=c, nchunks=lblk // c, npairs=npairs,
                             has_state=has_state, has_vres=has_vres)
    return pl.pallas_call(
        kern,
        grid=(batch, ngroups, nl_blocks),
        in_specs=in_specs,
        out_specs=[blk2, st_blk],
        out_shape=[jax.ShapeDtypeStruct((m, d), BF16 if lblk % 16 == 0 else F32),
                   jax.ShapeDtypeStruct((batch, C_HEADS, C_HEAD, C_HEAD), F32)],
        scratch_shapes=[pltpu.VMEM((npairs, LANES, LANES), F32)]
        + [pltpu.VMEM((lblk, w), F32)] * (4 if has_vres else 3),
        compiler_params=_cparams(("parallel", "parallel", "arbitrary")),
        name="rwkv_core",
    )(*args)


def _even_layer(x, batch, seqlen, e, layer, p, st_hgrn, k_cache, v_cache):
    proj = norm_matmul(x, p["norm_mix_pre"][layer], p["w_in_even"], e)
    o_a, s_new = hgrn(proj, p["hgrn_lb_raw"], e, batch, seqlen, st_hgrn)
    k_lo = IN_A + B_WIDTH
    new_rows = min(seqlen, WINDOW)
    tails = jnp.stack([proj[(b + 1) * seqlen - new_rows:(b + 1) * seqlen, k_lo:] for b in range(batch)])
    kb = tails[:, :, :B_KV_WIDTH].reshape(batch, new_rows, B_KV_HEADS, B_HEAD_DIM)
    vb = tails[:, :, B_KV_WIDTH:].reshape(batch, new_rows, B_KV_HEADS, B_HEAD_DIM)
    o_b = swa(proj, batch, seqlen, p["rel_bias"], p["attn_sinks"][e], e, k_cache, v_cache)
    if k_cache is None:
        k_new, v_new = kb, vb
    else:
        k_new = jnp.concatenate([k_cache[e, :, new_rows:], kb], axis=1)
        v_new = jnp.concatenate([v_cache[e, :, new_rows:], vb], axis=1)
    x = even_out(o_a, proj, o_b, x, p["hgrn_norm_g"][e], p["w_out_even"], p["norm_mix_post"][layer], e)
    return x, s_new, k_new, v_new


def _odd_layer(x, batch, seqlen, o, layer, p, shift0, s0, v_first):
    m, d = x.shape
    g_pre = p["norm_mix_pre"][layer]
    has_vres = o > 0
    width = lambda rng: rng[1] - rng[0]
    w1p, w2p = _pad_lora(p["rw_w1"][o], p["rw_w2"][o], width(LORA_W))
    a1p, a2p = _pad_lora(p["rw_a1"][o], p["rw_a2"][o], width(LORA_A))
    g1p, g2p = _pad_lora(p["rw_g1"][o], p["rw_g2"][o], width(LORA_G))
    first, low2 = [w1p, a1p, g1p], [p["rw_w0"][o], w2p, p["rw_a0"][o], a2p, g2p]
    if has_vres:
        v1p, v2p = _pad_lora(p["rw_v1"][o - 1], p["rw_v2"][o - 1], width(LORA_V))
        first.append(v1p)
        low2 += [p["rw_v0"][o - 1], v2p]
    rkv, mid = rwkv_in(x, g_pre, shift0, batch, seqlen, p["rw_mu"][o], p["w_rkv"], o,
                       jnp.concatenate(first, axis=1), has_vres)
    yg, s_new = rwkv_core(rkv, mid, low2, p["rw_kk"][o], p["rw_ka"][o], p["rw_rk"][o],
                          p["rw_lnx_g"][o], p["rw_lnx_b"][o], batch, seqlen, o, s0,
                          v_first if has_vres else None)
    shift_new = rmsnorm_rows(x.reshape(batch, seqlen, d)[:, -1], g_pre)
    x = odd_out(yg, x, p["rw_wo"], p["norm_mix_post"][layer], o)
    return x, s_new, shift_new, rkv


def _trunk(x3, st_hgrn, k_cache, v_cache, st_rwkv, st_shift, p, mlp_w):
    batch, seqlen, d = x3.shape
    x = x3.reshape(batch * seqlen, d)
    has_state = st_hgrn is not None
    hgrn_out, k_out, v_out, rwkv_out, shift_out = [], [], [], [], []
    v_first = None
    for layer in range(DEPTH):
        if layer % 2 == 0:
            e = layer // 2
            x, s_new, k_new, v_new = _even_layer(x, batch, seqlen, e, layer, p, st_hgrn, k_cache, v_cache)
            hgrn_out.append(s_new)
            k_out.append(k_new)
            v_out.append(v_new)
        else:
            o = layer // 2
            x, s_new, sh_new, rkv = _odd_layer(
                x, batch, seqlen, o, layer, p,
                st_shift[o] if has_state else None,
                st_rwkv,
                v_first)
            if o == 0:
                v_first = rkv
            rwkv_out.append(s_new)
            shift_out.append(sh_new)
        if mlp_w[layer] is None:
            mlp_w[layer] = (p["w_up"][layer].astype(BF16), p["w_down"][layer].astype(BF16))
        nxt = None
        if (layer + 1 < DEPTH and mlp_w[layer + 1] is None
                and ffn_cast_blocks(batch * seqlen, d, D_FF) is not None):
            nxt = (p["w_up"], p["w_down"], layer + 1)
        x, cast = ffn(x, p["norm_ffn_pre"][layer], *mlp_w[layer], p["norm_ffn_post"][layer], nxt)
        if nxt is not None:
            mlp_w[layer + 1] = cast
    return (x.reshape(batch, seqlen, d), jnp.stack(hgrn_out), jnp.stack(k_out), jnp.stack(v_out),
            jnp.stack(rwkv_out), jnp.stack(shift_out))


def kernel(x_prompt, x_sample, state_hgrn, cache_swa_k, cache_swa_v, state_rwkv, state_shift,
           norm_mix_pre, norm_mix_post, norm_ffn_pre, norm_ffn_post,
           w_in_even, w_out_even, hgrn_lb_raw, hgrn_norm_g, rel_bias, attn_sinks,
           rw_mu, rw_wr, rw_wk, rw_wv, rw_wo, rw_w0, rw_w1, rw_w2, rw_a0, rw_a1, rw_a2,
           rw_v0, rw_v1, rw_v2, rw_g1, rw_g2, rw_kk, rw_ka, rw_rk, rw_lnx_g, rw_lnx_b,
           w_up, w_down):
    p = {
        "norm_mix_pre": norm_mix_pre, "norm_mix_post": norm_mix_post,
        "norm_ffn_pre": norm_ffn_pre, "norm_ffn_post": norm_ffn_post,
        "w_in_even": w_in_even.astype(BF16), "w_out_even": w_out_even.astype(BF16),
        "hgrn_lb_raw": hgrn_lb_raw, "hgrn_norm_g": hgrn_norm_g,
        "rel_bias": rel_bias, "attn_sinks": attn_sinks,
        "rw_mu": rw_mu, "w_rkv": jnp.stack([rw_wr, rw_wk, rw_wv], axis=1).astype(BF16),
        "rw_wo": rw_wo.astype(BF16),
        "rw_w0": rw_w0, "rw_w1": rw_w1, "rw_w2": rw_w2, "rw_a0": rw_a0, "rw_a1": rw_a1, "rw_a2": rw_a2,
        "rw_v0": rw_v0, "rw_v1": rw_v1, "rw_v2": rw_v2, "rw_g1": rw_g1, "rw_g2": rw_g2,
        "rw_kk": rw_kk, "rw_ka": rw_ka, "rw_rk": rw_rk, "rw_lnx_g": rw_lnx_g, "rw_lnx_b": rw_lnx_b,
        "w_up": w_up, "w_down": w_down,
    }
    mlp_w = [None] * DEPTH
    y_p, hgrn_p, k_p, v_p, rwkv_p, shift_p = _trunk(x_prompt, None, None, None, None, None, p, mlp_w)
    y_s, hgrn_s, k_s, v_s, rwkv_s, shift_s = _trunk(
        x_sample, state_hgrn, cache_swa_k, cache_swa_v, state_rwkv, state_shift, p, mlp_w)
    return (y_p, y_s, hgrn_p, hgrn_s, k_p, k_s, v_p, v_s, rwkv_p, rwkv_s, shift_p, shift_s)
```

```python
import functools
import math

import numpy as np
import jax
import jax.numpy as jnp
from jax import lax
from jax.experimental import pallas as pl
from jax.experimental.pallas import tpu as pltpu

F32 = jnp.float32
BF16 = jnp.bfloat16

D_MODEL = 2048
DEPTH = 4
N_EVEN = 2
N_ODD = 2
A_HEADS = 8
A_KDIM = 128
A_VDIM = 128
A_WIDTH = 1024
A_QK = 1024
B_HEADS = 16
B_HEAD_DIM = 64
B_KV_HEADS = 4
B_GROUP = 4
B_WIDTH = 1024
B_KV_WIDTH = 256
WINDOW = 128
N_BUCKETS = 32
MAX_DISTANCE = 128
MASK_VALUE = -1e30
IN_A = 4096
IN_EVEN = 5632
C_HEAD = 64
C_HEADS = 32
GN_EPS = 64e-5
D_FF = 8192
NORM_EPS = 1e-6

LANES = 128
VMEM_LIMIT = 56 * 1024 * 1024

HGRN_CHUNK = 128
RWKV_CHUNK = 64


def _cparams(sem):
    return pltpu.CompilerParams(dimension_semantics=sem, vmem_limit_bytes=VMEM_LIMIT)


def _rms(x, g):
    return x * lax.rsqrt(jnp.mean(x * x, axis=-1, keepdims=True) + NORM_EPS) * g


def _dot(a, b):
    return jnp.dot(a.astype(BF16), b.astype(BF16), preferred_element_type=F32)


def _dot_nt(a, b):
    return lax.dot_general(a.astype(BF16), b.astype(BF16), (((1,), (1,)), ((), ())),
                           preferred_element_type=F32)


def _dot_tn(a, b):
    return lax.dot_general(a.astype(BF16), b.astype(BF16), (((0,), (0,)), ((), ())),
                           preferred_element_type=F32)


def _split2(x):
    hi = x.astype(BF16)
    return hi, (x - hi.astype(F32)).astype(BF16)


def _dot3_shared(lhs_list, b):
    bh, bl = _split2(b)
    parts = [_split2(a) for a in lhs_list]
    his = [p[0] for p in parts]
    los = [p[1] for p in parts]
    by_hi = jnp.dot(jnp.concatenate(his + los, axis=0), bh, preferred_element_type=F32)
    by_lo = jnp.dot(jnp.concatenate(his, axis=0), bl, preferred_element_type=F32)
    n = sum(a.shape[0] for a in lhs_list)
    out, off = [], 0
    for a in lhs_list:
        m = a.shape[0]
        out.append(by_hi[off:off + m] + by_hi[n + off:n + off + m] + by_lo[off:off + m])
        off += m
    return out


def _pick_tile(m, cands):
    for c in cands:
        if m % c == 0:
            return c
    return m


def _norm_matmul_kernel(x_ref, g_ref, w_ref, o_ref, xn_ref):
    @pl.when(pl.program_id(1) == 0)
    def _():
        xn_ref[...] = _rms(x_ref[...], g_ref[...]).astype(BF16)

    o_ref[...] = jnp.dot(xn_ref[...], w_ref[...], preferred_element_type=F32)


def norm_matmul(x, g, w_bf16, tn=1408):
    m, d = x.shape
    n = w_bf16.shape[1]
    tm = _pick_tile(m, (1024, 512, 256, 128, 64, 32, 16, 8))
    return pl.pallas_call(
        _norm_matmul_kernel,
        grid=(m // tm, n // tn),
        in_specs=[pl.BlockSpec((tm, d), lambda i, j: (i, 0)),
                  pl.BlockSpec((1, d), lambda i, j: (0, 0)),
                  pl.BlockSpec((d, tn), lambda i, j: (0, j))],
        out_specs=pl.BlockSpec((tm, tn), lambda i, j: (i, j)),
        out_shape=jax.ShapeDtypeStruct((m, n), F32),
        scratch_shapes=[pltpu.VMEM((tm, d), BF16)],
        compiler_params=_cparams(("parallel", "arbitrary")),
        name="norm_matmul",
    )(x, g.reshape(1, d), w_bf16)


def _ffn_kernel(*refs, cast_srcs):
    nsrc = sum(cast_srcs)
    x_ref, gpre_ref, wup_ref, wdn_ref, gpost_ref = refs[:5]
    src_refs = refs[5:5 + nsrc]
    o_ref = refs[5 + nsrc]
    cast_out = refs[6 + nsrc:6 + nsrc + len(cast_srcs)]
    xn_ref, acc_ref = refs[-2:]
    k = 0
    for out_ref, n in zip(cast_out, cast_srcs):
        for j in range(n):
            if n == 1:
                out_ref[...] = src_refs[k][...].astype(BF16)
            else:
                out_ref[j] = src_refs[k][...].astype(BF16)
            k += 1
    f = pl.program_id(1)

    @pl.when(f == 0)
    def _():
        xn_ref[...] = _rms(x_ref[...], gpre_ref[...]).astype(BF16)
        acc_ref[...] = jnp.zeros_like(acc_ref)

    h = jnp.dot(xn_ref[...], wup_ref[...], preferred_element_type=F32)
    h = jnp.square(jnp.maximum(h, 0.0)).astype(BF16)
    acc_ref[...] += jnp.dot(h, wdn_ref[...], preferred_element_type=F32)

    @pl.when(f == pl.num_programs(1) - 1)
    def _():
        o_ref[...] = x_ref[...] + _rms(acc_ref[...], gpost_ref[...])


BF16_SUBLANES = 16


def ffn_steps(m, dff, tf=1024):
    return (m // _pick_tile(m, (512, 256, 128, 64, 32, 16, 8))) * (dff // tf)


CAST_BLOCK_BYTES = 1 << 20


def can_cast_in(nsteps, rows, cols):
    return (rows % nsteps == 0 and (rows // nsteps) % BF16_SUBLANES == 0
            and (rows // nsteps) * cols * 4 <= CAST_BLOCK_BYTES)


def ffn(x, gpre, wup_bf16, wdn_bf16, gpost, casts=(), tf=1024):
    m, d = x.shape
    dff = wup_bf16.shape[1]
    tm = _pick_tile(m, (512, 256, 128, 64, 32, 16, 8))
    nf = dff // tf
    nsteps = (m // tm) * nf
    step = lambda i, f: i * nf + f
    in_specs = [pl.BlockSpec((tm, d), lambda i, f: (i, 0)),
                pl.BlockSpec((1, d), lambda i, f: (0, 0)),
                pl.BlockSpec((d, tf), lambda i, f: (0, f)),
                pl.BlockSpec((tf, d), lambda i, f: (f, 0)),
                pl.BlockSpec((1, d), lambda i, f: (0, 0))]
    args = [x, gpre.reshape(1, d), wup_bf16, wdn_bf16, gpost.reshape(1, d)]
    out_specs = [pl.BlockSpec((tm, d), lambda i, f: (i, 0))]
    out_shape = [jax.ShapeDtypeStruct((m, d), F32)]
    for job in casts:
        rows, cols = job[0][0].shape[len(job[0][1]):]
        assert can_cast_in(nsteps, rows, cols)
        rb = rows // nsteps
        for arr, lead in job:
            in_specs.append(pl.BlockSpec((None,) * len(lead) + (rb, cols),
                                         lambda i, f, lead=lead: lead + (step(i, f), 0)))
            args.append(arr)
        if len(job) == 1:
            out_specs.append(pl.BlockSpec((rb, cols), lambda i, f: (step(i, f), 0)))
            out_shape.append(jax.ShapeDtypeStruct((rows, cols), BF16))
        else:
            out_specs.append(pl.BlockSpec((len(job), rb, cols), lambda i, f: (0, step(i, f), 0)))
            out_shape.append(jax.ShapeDtypeStruct((len(job), rows, cols), BF16))
    outs = pl.pallas_call(
        functools.partial(_ffn_kernel, cast_srcs=tuple(len(job) for job in casts)),
        grid=(m // tm, nf),
        in_specs=in_specs,
        out_specs=out_specs,
        out_shape=out_shape,
        scratch_shapes=[pltpu.VMEM((tm, d), BF16), pltpu.VMEM((tm, d), F32)],
        compiler_params=_cparams(("parallel", "arbitrary")),
        name="ffn",
    )(*args)
    return outs[0], list(outs[1:])


def _even_out_kernel(oa_ref, ga_ref, ob_ref, x_ref, ag_ref, w_ref, gpost_ref, o_ref):
    ga = ga_ref[...]
    oan = _rms(oa_ref[...], ag_ref[...]) * (ga * jax.nn.sigmoid(ga))
    mix = (jnp.dot(oan.astype(BF16), w_ref[:A_WIDTH, :], preferred_element_type=F32)
           + jnp.dot(ob_ref[...].astype(BF16), w_ref[A_WIDTH:, :], preferred_element_type=F32))
    o_ref[...] = x_ref[...] + _rms(mix, gpost_ref[...])


def even_out(o_a, proj, o_b, x, a_norm_g, w_out_bf16, gpost):
    m, d = x.shape
    tm = _pick_tile(m, (384, 256, 128, 64, 32, 16, 8))
    ga_blk = (3 * A_WIDTH) // A_WIDTH
    return pl.pallas_call(
        _even_out_kernel,
        grid=(m // tm,),
        in_specs=[pl.BlockSpec((tm, A_WIDTH), lambda i: (i, 0)),
                  pl.BlockSpec((tm, A_WIDTH), lambda i: (i, ga_blk)),
                  pl.BlockSpec((tm, B_WIDTH), lambda i: (i, 0)),
                  pl.BlockSpec((tm, d), lambda i: (i, 0)),
                  pl.BlockSpec((1, A_WIDTH), lambda i: (0, 0)),
                  pl.BlockSpec((A_WIDTH + B_WIDTH, d), lambda i: (0, 0)),
                  pl.BlockSpec((1, d), lambda i: (0, 0))],
        out_specs=pl.BlockSpec((tm, d), lambda i: (i, 0)),
        out_shape=jax.ShapeDtypeStruct((m, d), F32),
        compiler_params=_cparams(("parallel",)),
        name="even_out",
    )(o_a, proj, o_b, x, a_norm_g.reshape(1, A_WIDTH), w_out_bf16, gpost.reshape(1, d))


def _odd_out_kernel(y_ref, x_ref, w_ref, gpost_ref, o_ref):
    mix = jnp.dot(y_ref[...].astype(BF16), w_ref[...], preferred_element_type=F32)
    o_ref[...] = x_ref[...] + _rms(mix, gpost_ref[...])


def odd_out(yg, x, wo_bf16, gpost):
    m, d = x.shape
    tm = _pick_tile(m, (384, 256, 128, 64, 32, 16, 8))
    return pl.pallas_call(
        _odd_out_kernel,
        grid=(m // tm,),
        in_specs=[pl.BlockSpec((tm, d), lambda i: (i, 0)),
                  pl.BlockSpec((tm, d), lambda i: (i, 0)),
                  pl.BlockSpec((d, d), lambda i: (0, 0)),
                  pl.BlockSpec((1, d), lambda i: (0, 0))],
        out_specs=pl.BlockSpec((tm, d), lambda i: (i, 0)),
        out_shape=jax.ShapeDtypeStruct((m, d), F32),
        compiler_params=_cparams(("parallel",)),
        name="odd_out",
    )(yg, x, wo_bf16, gpost.reshape(1, d))


def _level_consts(c):
    levels = []
    s = c // 2
    while s >= 1:
        levels.append(s)
        s //= 2
    mask = np.zeros((len(levels), c, c), np.float32)
    idx = np.arange(c)
    for l, s in enumerate(levels):
        same = (idx[:, None] // (2 * s)) == (idx[None, :] // (2 * s))
        upper = (idx[:, None] % (2 * s)) >= s
        lower = (idx[None, :] % (2 * s)) < s
        mask[l] = (same & upper & lower).astype(np.float32)
    return levels, mask


def _split_rows(g, s, rowid):
    c = g.shape[0]
    if 2 * s >= 8:
        return jnp.concatenate(
            [jnp.broadcast_to(g[b + s - 1:b + s, :], (2 * s, g.shape[1])) for b in range(0, c, 2 * s)], axis=0)
    r = rowid % (2 * s)
    out = g
    for off in range(-(s - 1), s + 1):
        if off != 0:
            out = jnp.where(r == s - 1 + off, pltpu.roll(g, off % c, 0), out)
    return out


def _hgrn_kernel(*refs, layer, chunk, nchunks, levels, has_state, nheads):
    if has_state:
        q_ref, f_ref, i_ref, lb_ref, mask_ref, s0_ref, o_ref, s_ref, st_ref = refs
    else:
        q_ref, f_ref, i_ref, lb_ref, mask_ref, o_ref, s_ref, st_ref = refs
    c = chunk
    nh = nheads
    rowid = lax.broadcasted_iota(jnp.int32, (c, nh * LANES), 0)
    l_idx = pl.program_id(2)

    @pl.when(l_idx == 0)
    def _():
        for hi in range(nh):
            if has_state:
                st_ref[hi] = s0_ref[0, hi].T
            else:
                st_ref[hi] = jnp.zeros((A_VDIM, A_KDIM), F32)

    lbr = lb_ref[...]
    e = jnp.exp(lbr - jnp.max(lbr, axis=0, keepdims=True))
    p = e / jnp.sum(e, axis=0, keepdims=True)
    lb = jnp.zeros((1, nh * LANES), F32)
    for i in range(1, layer + 1):
        lb = lb + p[i:i + 1, :]
    one_m_lb = 1.0 - lb
    head = lambda x, hi: x[:, hi * LANES:(hi + 1) * LANES]

    def body(ci, carry):
        rows = pl.ds(pl.multiple_of(ci * c, c), c)
        fq = f_ref[rows, :]
        qr = q_ref[rows, :]
        v = i_ref[rows, :]
        f = lb + one_m_lb * jax.nn.sigmoid(fq)
        k = 1.0 - f
        q = qr * jax.nn.sigmoid(qr) * (A_KDIM ** -0.5)

        g = jnp.log2(f)
        sft = 1
        while sft < c:
            g = g + jnp.where(rowid >= sft, pltpu.roll(g, sft, 0), 0.0)
            sft *= 2
        glast = g[c - 1:c, :]
        q_in = q * jnp.exp2(g)
        kd = k * jnp.exp2(glast - g)
        dec = jnp.exp2(glast)
        diag = q * k

        sts = [st_ref[hi] for hi in range(nh)]
        os_ = [_dot_nt(head(q_in, hi), sts[hi]) for hi in range(nh)]
        attns = [jnp.zeros((c, c), F32) for _ in range(nh)]
        for l, s in enumerate(levels):
            e = jnp.exp2(-jnp.abs(g - _split_rows(g, s, rowid)))
            qs = q * e
            ks = k * e
            ml = mask_ref[l]
            attns = [at + ml * _dot_nt(head(qs, hi), head(ks, hi)) for hi, at in enumerate(attns)]
        for hi in range(nh):
            vh = head(v, hi)
            o = os_[hi] + _dot(attns[hi], vh) + jnp.sum(head(diag, hi), axis=-1, keepdims=True) * vh
            o_ref[rows, hi * LANES:(hi + 1) * LANES] = o
        for hi in range(nh):
            st_ref[hi] = sts[hi] * head(dec, hi) + _dot_tn(head(v, hi), head(kd, hi))
        return carry

    lax.fori_loop(0, nchunks, body, 0, unroll=2 if nchunks % 2 == 0 else 1)

    @pl.when(l_idx == pl.num_programs(2) - 1)
    def _():
        for hi in range(nh):
            s_ref[0, hi] = st_ref[hi].T


def hgrn(proj, lb_raw, layer, batch, seqlen, s0=None, nheads=4):
    m = proj.shape[0]
    c = math.gcd(seqlen, HGRN_CHUNK)
    lblk = _pick_tile(seqlen, (512, 256, 128, 64, 32, 16, 8))
    nl_blocks = seqlen // lblk
    levels, mask = _level_consts(c)
    has_state = s0 is not None
    if c * 4 <= HGRN_CHUNK:
        nheads = A_HEADS
    w = nheads * LANES
    ngroups = A_HEADS // nheads
    kern = functools.partial(_hgrn_kernel, layer=layer, chunk=c, nchunks=lblk // c,
                             levels=tuple(levels), has_state=has_state, nheads=nheads)
    row = lambda b, h, l: b * nl_blocks + l
    in_specs = [pl.BlockSpec((lblk, w), lambda b, h, l: (row(b, h, l), h)),
                pl.BlockSpec((lblk, w), lambda b, h, l: (row(b, h, l), ngroups + h)),
                pl.BlockSpec((lblk, w), lambda b, h, l: (row(b, h, l), 2 * ngroups + h)),
                pl.BlockSpec((N_EVEN, w), lambda b, h, l: (0, h)),
                pl.BlockSpec((len(levels), c, c), lambda b, h, l: (0, 0, 0))]
    args = [proj, proj, proj, lb_raw, jnp.asarray(mask)]
    st_blk = pl.BlockSpec((1, nheads, A_KDIM, A_VDIM), lambda b, h, l: (b, h, 0, 0))
    if has_state:
        in_specs.append(pl.BlockSpec((None, 1, nheads, A_KDIM, A_VDIM), lambda b, h, l: (layer, b, h, 0, 0)))
        args.append(s0)
    return pl.pallas_call(
        kern,
        grid=(batch, ngroups, nl_blocks),
        in_specs=in_specs,
        out_specs=[pl.BlockSpec((lblk, w), lambda b, h, l: (row(b, h, l), h)), st_blk],
        out_shape=[jax.ShapeDtypeStruct((m, A_WIDTH), F32),
                   jax.ShapeDtypeStruct((batch, A_HEADS, A_KDIM, A_VDIM), F32)],
        scratch_shapes=[pltpu.VMEM((nheads, A_VDIM, A_KDIM), F32)],
        compiler_params=_cparams(("parallel", "parallel", "arbitrary")),
        name="hgrn",
    )(*args)


def _t5_bucket(dist):
    max_exact = N_BUCKETS // 2
    d = np.maximum(dist, 0)
    large = max_exact + (np.log(np.maximum(d, max_exact).astype(np.float32) / max_exact)
                         / math.log(MAX_DISTANCE / max_exact) * (N_BUCKETS - max_exact)).astype(np.int32)
    large = np.minimum(large, N_BUCKETS - 1)
    return np.where(d < max_exact, d, large).astype(np.int32)


def _swa_kernel(q_ref, kp_ref, kc_ref, vp_ref, vc_ref, bucket_ref, band_ref, rb_ref, sink_ref,
                o_ref, bias_ref, *, qb, span, prev_always_valid):
    first = (pl.program_id(0) == 0) & (pl.program_id(1) == 0)

    @pl.when(first)
    def _():
        bk = bucket_ref[...]
        band = band_ref[...]

        def per_head(h, carry):
            def per_bucket(bi, acc):
                return jnp.where(bk == bi, rb_ref[bi, h], acc)
            acc = lax.fori_loop(0, N_BUCKETS, per_bucket, jnp.zeros((qb, span), F32))
            bias_ref[h] = jnp.where(band > 0, acc, MASK_VALUE)
            return carry

        lax.fori_loop(0, B_HEADS, per_head, 0)

    scale = B_HEAD_DIM ** -0.5
    q = q_ref[...]
    kall = jnp.concatenate([kp_ref[...], kc_ref[...]], axis=0)
    vall = jnp.concatenate([vp_ref[...], vc_ref[...]], axis=0)
    if not prev_always_valid:
        col = lax.broadcasted_iota(jnp.int32, (qb, span), 1)
        no_prev = (col < WINDOW) & (pl.program_id(1) == 0)
    heads = range(B_HEADS)
    ks = [kall[:, kh * B_HEAD_DIM:(kh + 1) * B_HEAD_DIM].astype(BF16) for kh in range(B_KV_HEADS)]
    vs = [vall[:, kh * B_HEAD_DIM:(kh + 1) * B_HEAD_DIM].astype(BF16) for kh in range(B_KV_HEADS)]
    qs = [(q[:, h * B_HEAD_DIM:(h + 1) * B_HEAD_DIM] * scale).astype(BF16) for h in heads]
    ss = [_dot_nt(qs[h], ks[h // B_GROUP]) + bias_ref[h] for h in heads]
    if not prev_always_valid:
        ss = [jnp.where(no_prev, MASK_VALUE, s) for s in ss]
    ms = [jnp.maximum(jnp.max(ss[h], axis=-1, keepdims=True), sink_ref[h]) for h in heads]
    ps = [jnp.exp(s - m) for s, m in zip(ss, ms)]
    denoms = [jnp.sum(ps[h], axis=-1, keepdims=True) + jnp.exp(sink_ref[h] - ms[h]) for h in heads]
    outs = [_dot(ps[h], vs[h // B_GROUP]) / denoms[h] for h in heads]
    o_ref[...] = jnp.concatenate(outs, axis=1).astype(o_ref.dtype)


def swa(proj, batch, seqlen, rel_bias, sinks, layer, k_past=None, v_past=None):
    m = proj.shape[0]
    has_cache = k_past is not None
    qb = math.gcd(seqlen, WINDOW)
    nb = seqlen // qb
    span = WINDOW + qb
    dist = np.arange(qb)[:, None] + WINDOW - np.arange(span)[None, :]
    band = ((dist >= 0) & (dist < WINDOW)).astype(np.float32)
    bucket = _t5_bucket(dist)
    q_col = IN_A // B_WIDTH
    k_col = (IN_A + B_WIDTH) // B_KV_WIDTH
    v_col = k_col + 1
    cur = lambda c: (lambda b, n: (b * nb + n, c))
    if has_cache:
        assert nb == 1
        prev_k = pl.BlockSpec((None, WINDOW, B_KV_WIDTH), lambda b, n: (layer, b, 0))
        prev_v = pl.BlockSpec((None, WINDOW, B_KV_WIDTH), lambda b, n: (layer, b, 0))
        kp_arr = k_past.reshape(k_past.shape[0], batch * WINDOW, B_KV_WIDTH)
        vp_arr = v_past.reshape(v_past.shape[0], batch * WINDOW, B_KV_WIDTH)
    else:
        assert qb == WINDOW
        prev = lambda c: (lambda b, n: (b * nb + jnp.maximum(n - 1, 0), c))
        prev_k = pl.BlockSpec((WINDOW, B_KV_WIDTH), prev(k_col))
        prev_v = pl.BlockSpec((WINDOW, B_KV_WIDTH), prev(v_col))
        kp_arr, vp_arr = proj, proj
    kern = functools.partial(_swa_kernel, qb=qb, span=span, prev_always_valid=has_cache)
    return pl.pallas_call(
        kern,
        grid=(batch, nb),
        in_specs=[pl.BlockSpec((qb, B_WIDTH), cur(q_col)),
                  prev_k,
                  pl.BlockSpec((qb, B_KV_WIDTH), cur(k_col)),
                  prev_v,
                  pl.BlockSpec((qb, B_KV_WIDTH), cur(v_col)),
                  pl.BlockSpec((qb, span), lambda b, n: (0, 0)),
                  pl.BlockSpec((qb, span), lambda b, n: (0, 0)),
                  pl.BlockSpec(memory_space=pltpu.SMEM),
                  pl.BlockSpec(memory_space=pltpu.SMEM)],
        out_specs=pl.BlockSpec((qb, B_WIDTH), lambda b, n: (b * nb + n, 0)),
        out_shape=jax.ShapeDtypeStruct((m, B_WIDTH), BF16 if qb % 16 == 0 else F32),
        scratch_shapes=[pltpu.VMEM((B_HEADS, qb, span), F32)],
        compiler_params=_cparams(("arbitrary", "arbitrary")),
        name="swa",
    )(proj, kp_arr, proj, vp_arr, proj, jnp.asarray(bucket), jnp.asarray(band), rel_bias, sinks)


def _rmsnorm_kernel(x_ref, g_ref, o_ref):
    o_ref[...] = _rms(x_ref[...], g_ref[...])


def rmsnorm_rows(x, g):
    m, d = x.shape
    tm = _pick_tile(m, (512, 256, 128, 64, 32, 16, 8))
    return pl.pallas_call(
        _rmsnorm_kernel,
        grid=(m // tm,),
        in_specs=[pl.BlockSpec((tm, d), lambda i: (i, 0)), pl.BlockSpec((1, d), lambda i: (0, 0))],
        out_specs=pl.BlockSpec((tm, d), lambda i: (i, 0)),
        out_shape=jax.ShapeDtypeStruct((m, d), F32),
        compiler_params=_cparams(("parallel",)),
        name="rmsnorm",
    )(x, g.reshape(1, d))


LORA_W = (0, 128)
LORA_A = (128, 256)
LORA_G = (256, 512)
LORA_V = (512, 640)


def _rwkv_in_kernel(x_ref, xp_ref, s_ref, g_ref, mu3_ref, mul_ref, w_ref, w1_ref,
                    rkv_ref, mid_ref, h_s, hp_s, *, tm, seqlen, has_vres):
    ph = pl.program_id(1)

    @pl.when(ph == 0)
    def _():
        g = g_ref[...]
        h = _rms(x_ref[...], g)
        rowid = lax.broadcasted_iota(jnp.int32, h.shape, 0)
        rolled = pltpu.roll(h, 1, 0)
        if seqlen % tm == 0:
            prev_last = _rms(xp_ref[...], g)[7:8, :]
            at_start = pl.program_id(0) % (seqlen // tm) == 0
            first = jnp.where(at_start, s_ref[...], prev_last)
            hp = jnp.where(rowid == 0, first, rolled)
        else:
            hp = jnp.where(rowid % seqlen == 0, s_ref[...], rolled)
        h_s[...] = h
        hp_s[...] = hp

    @pl.when(ph < 3)
    def _():
        h = h_s[...]
        xm = (h + (hp_s[...] - h) * mu3_ref[0]).astype(BF16)
        rkv_ref[0] = jnp.dot(xm, w_ref[...], preferred_element_type=F32)

    @pl.when(ph == 3)
    def _():
        h = h_s[...]
        xx = hp_s[...] - h
        mix = lambda i: (h + xx * mul_ref[i:i + 1, :]).astype(BF16)
        low = lambda i, rng: jnp.dot(mix(i), w1_ref[:, rng[0]:rng[1]], preferred_element_type=F32)
        parts = [jnp.tanh(low(0, LORA_W)), low(1, LORA_A), jax.nn.sigmoid(low(2, LORA_G))]
        if has_vres:
            parts.append(low(3, LORA_V))
        mid_ref[...] = jnp.concatenate(parts, axis=1).astype(BF16)


def rwkv_in(x, g, shift0, batch, seqlen, mu, w3_bf16, w1cat_bf16, has_vres):
    m, d = x.shape
    tm = next(t for t in (512, 256, 128, 64, 32, 16, 8)
              if m % t == 0 and (seqlen % t == 0 or t % seqlen == 0))
    midw = w1cat_bf16.shape[1]
    mu_rkv = jnp.stack([mu[0], mu[2], mu[3]])[:, None, :]
    mu_low = jnp.stack([mu[1], mu[4], mu[5], mu[3]])
    if shift0 is None:
        shift0 = jnp.zeros((batch, d), F32)
    if seqlen % tm == 0:
        srow = shift0[:, None, :]
        tps = seqlen // tm
        s_spec = pl.BlockSpec((None, 1, d), lambda i, p: (i // tps, 0, 0))
    else:
        srow = jnp.repeat(shift0, seqlen, axis=0)
        s_spec = pl.BlockSpec((tm, d), lambda i, p: (i, 0))
    sub = tm // 8
    kern = functools.partial(_rwkv_in_kernel, tm=tm, seqlen=seqlen, has_vres=has_vres)
    return pl.pallas_call(
        kern,
        grid=(m // tm, 4),
        in_specs=[pl.BlockSpec((tm, d), lambda i, p: (i, 0)),
                  pl.BlockSpec((8, d), lambda i, p: (jnp.maximum(i * sub - 1, 0), 0)),
                  s_spec,
                  pl.BlockSpec((1, d), lambda i, p: (0, 0)),
                  pl.BlockSpec((1, 1, d), lambda i, p: (jnp.minimum(p, 2), 0, 0)),
                  pl.BlockSpec((4, d), lambda i, p: (0, 0)),
                  pl.BlockSpec((None, d, d), lambda i, p: (jnp.minimum(p, 2), 0, 0)),
                  pl.BlockSpec((d, midw), lambda i, p: (0, 0))],
        out_specs=[pl.BlockSpec((1, tm, d), lambda i, p: (jnp.minimum(p, 2), i, 0)),
                   pl.BlockSpec((tm, midw), lambda i, p: (i, 0))],
        out_shape=[jax.ShapeDtypeStruct((3, m, d), F32),
                   jax.ShapeDtypeStruct((m, midw), BF16)],
        scratch_shapes=[pltpu.VMEM((tm, d), F32), pltpu.VMEM((tm, d), F32)],
        compiler_params=_cparams(("parallel", "arbitrary")),
        name="rwkv_in",
    )(x, x, srow, g.reshape(1, d), mu_rkv, mu_low, w3_bf16, w1cat_bf16)


def _pad_lora(w1, w2, width):
    r = w1.shape[1]
    return (jnp.pad(w1, ((0, 0), (0, width - r))).astype(BF16),
            jnp.pad(w2, ((0, width - r), (0, 0))).astype(BF16))


def _rwkv_consts(c):
    i = np.arange(2 * c)
    same = (i[:, None] // c) == (i[None, :] // c)
    strict = (same & ((i[:, None] % c) > (i[None, :] % c))).astype(np.float32)
    incl = (same & ((i[:, None] % c) >= (i[None, :] % c))).astype(np.float32)
    eye = np.eye(2 * c, dtype=np.float32)
    l = np.arange(LANES)
    headones = ((l[:, None] // C_HEAD) == (l[None, :] // C_HEAD)).astype(np.float32)
    return strict, incl, eye, headones


def _rwkv_core_kernel(*refs, chunk, nchunks, npairs, has_state, has_vres):
    it = iter(refs)
    r_ref, k_ref, v_ref, mid_ref = (next(it) for _ in range(4))
    w0_ref, w2_ref, a0_ref, a2_ref, g2_ref = (next(it) for _ in range(5))
    if has_vres:
        vf_ref, v0_ref, v2_ref = next(it), next(it), next(it)
    kk_ref, ka_ref, rk_ref, lg_ref, lb_ref = (next(it) for _ in range(5))
    strict_ref, incl_ref, eye_ref, hones_ref = (next(it) for _ in range(4))
    if has_state:
        s0_ref = next(it)
    y_ref, s_ref, st_ref, wl_ref, a_ref, g_ref = (next(it) for _ in range(6))
    if has_vres:
        vg_ref = next(it)
    c = chunk
    l_idx = pl.program_id(2)
    lane = lax.broadcasted_iota(jnp.int32, (1, LANES), 1)
    m0 = (lane < C_HEAD).astype(F32)
    m1 = 1.0 - m0

    low = lambda rng, w_ref: jnp.dot(mid_ref[:, rng[0]:rng[1]], w_ref[...], preferred_element_type=F32)
    wl_ref[...] = -math.exp(-0.5) * jax.nn.sigmoid(w0_ref[...] + low(LORA_W, w2_ref))
    a_ref[...] = jax.nn.sigmoid(a0_ref[...] + low(LORA_A, a2_ref))
    g_ref[...] = low(LORA_G, g2_ref)
    if has_vres:
        vg_ref[...] = jax.nn.sigmoid(v0_ref[...] + low(LORA_V, v2_ref))

    @pl.when(l_idx == 0)
    def _():
        for pi in range(npairs):
            if has_state:
                z = jnp.zeros((C_HEAD, C_HEAD), F32)
                top = jnp.concatenate([s0_ref[0, 2 * pi], z], axis=1)
                bot = jnp.concatenate([z, s0_ref[0, 2 * pi + 1]], axis=1)
                st_ref[pi] = jnp.concatenate([top, bot], axis=0)
            else:
                st_ref[pi] = jnp.zeros((LANES, LANES), F32)

    strict = strict_ref[...]
    incl = incl_ref[...]
    eye = eye_ref[...]

    def stack(x):
        return jnp.concatenate([x * m0, x * m1], axis=0)

    first_head = lax.broadcasted_iota(jnp.int32, (c, LANES), 1) < C_HEAD
    rowid = lax.broadcasted_iota(jnp.int32, (c, LANES), 0)

    def rowsums(xs):
        return [jnp.where(first_head,
                          jnp.sum(x * m0, axis=-1, keepdims=True),
                          jnp.sum(x * m1, axis=-1, keepdims=True)) for x in xs]

    def cumsum_rows(x):
        s = 1
        while s < c:
            x = x + jnp.where(rowid >= s, pltpu.roll(x, s, 0), 0.0)
            s *= 2
        return x

    def load(pi, rows):
        cols = slice(pi * LANES, (pi + 1) * LANES)
        k = k_ref[0, rows, cols]
        v = v_ref[0, rows, cols]
        a = a_ref[rows, cols]
        if has_vres:
            v = v + (vf_ref[0, rows, cols] - v) * vg_ref[rows, cols]
        return dict(cols=cols, r=r_ref[0, rows, cols], v=v, a=a, wl=wl_ref[rows, cols],
                    kr=k * kk_ref[:, cols], kh=k * (1.0 + (a - 1.0) * ka_ref[:, cols]))

    def decays(p, ss):
        kk = p["kr"] * lax.rsqrt(jnp.maximum(ss, 1e-24))
        b = kk * p["a"]
        gc = cumsum_rows(p["wl"])
        gl = gc[c - 1:c, :]
        e_neg = jnp.exp(-gc)
        e_out = jnp.exp(gl - gc)
        p.update(gl=gl, ab=-kk * jnp.exp(gc - p["wl"]), rb=p["r"] * jnp.exp(gc),
                 bt=b * e_neg, kt=p["kh"] * e_neg, bh=b * e_out, khat=p["kh"] * e_out)

    def intra(p):
        lhs = jnp.concatenate([stack(p["ab"]), stack(p["rb"])], axis=0)
        with_b = _dot_nt(lhs, jnp.concatenate([p["bt"], p["bt"]], axis=0))
        with_k = _dot_nt(lhs, jnp.concatenate([p["kt"], p["kt"]], axis=0))
        p.update(a_ab=with_b[:2 * c] * strict, a_rb=with_b[2 * c:] * incl,
                 a_ak=with_k[:2 * c] * strict, a_rk=with_k[2 * c:] * incl)

    def body(ci, carry):
        rows = pl.ds(pl.multiple_of(ci * c, c), c)
        ps = [load(pi, rows) for pi in range(npairs)]
        for p, ss in zip(ps, rowsums([p["kr"] * p["kr"] for p in ps])):
            decays(p, ss)
        for p in ps:
            intra(p)

        ts = [eye + p["a_ab"] for p in ps]
        pws = [p["a_ab"] for p in ps]
        if c > 2:
            pws = [_dot3_shared([pw], pw)[0] for pw in pws]
            n = 2
            while 2 * n < c:
                res = [_dot3_shared([t, pw], pw) for t, pw in zip(ts, pws)]
                ts = [t + r[0] for t, r in zip(ts, res)]
                pws = [r[1] for r in res]
                n *= 2
            ts = [t + _dot3_shared([t], pw)[0] for t, pw in zip(ts, pws)]

        sts = [st_ref[pi] for pi in range(npairs)]
        fss = [_dot_nt(jnp.concatenate([p["ab"], p["rb"]], axis=0), st) for p, st in zip(ps, sts)]
        vss = [stack(p["v"]) for p in ps]
        rhss = [stack(fs[:c]) + _dot(p["a_ak"], vs) for p, fs, vs in zip(ps, fss, vss)]
        uss = [_dot(t, rhs) for t, rhs in zip(ts, rhss)]
        yss = [_dot(p["a_rb"], us) + _dot(p["a_rk"], vs) for p, us, vs in zip(ps, uss, vss)]
        for pi, (p, st, us) in enumerate(zip(ps, sts, uss)):
            u = us[:c] + us[c:]
            upd = _dot_tn(jnp.concatenate([u, p["v"]], axis=0),
                          jnp.concatenate([p["bh"], p["khat"]], axis=0))
            st_ref[pi] = st * jnp.exp(p["gl"]) + upd * hones_ref[...]
        inv_n = 1.0 / C_HEAD
        ys_ = [fs[c:] + ys[:c] + ys[c:] for fs, ys in zip(fss, yss)]
        sums = rowsums(ys_ + [p["r"] * p["kh"] * rk_ref[:, p["cols"]] for p in ps])
        dlts = [y - m * inv_n for y, m in zip(ys_, sums[:npairs])]
        vars_ = rowsums([d * d for d in dlts])
        for p, dlt, var, bsum in zip(ps, dlts, vars_, sums[npairs:]):
            cols = p["cols"]
            yn = dlt * lax.rsqrt(var * inv_n + GN_EPS) * lg_ref[:, cols] + lb_ref[:, cols]
            y_ref[rows, cols] = ((yn + bsum * p["v"]) * g_ref[rows, cols]).astype(y_ref.dtype)
        return carry

    lax.fori_loop(0, nchunks, body, 0)

    @pl.when(l_idx == pl.num_programs(2) - 1)
    def _():
        for pi in range(npairs):
            st = st_ref[pi]
            s_ref[0, 2 * pi] = st[:C_HEAD, :C_HEAD]
            s_ref[0, 2 * pi + 1] = st[C_HEAD:, C_HEAD:]


def rwkv_core(rkv, mid, low2, kk_p, ka_p, rk_p, lnx_g, lnx_b, batch, seqlen, layer,
              s0=None, v_first=None, npairs=8):
    _, m, d = rkv.shape
    c = math.gcd(seqlen, RWKV_CHUNK)
    lblk = _pick_tile(seqlen, (512, 256, 128, 64, 32, 16, 8))
    nl_blocks = seqlen // lblk
    has_state = s0 is not None
    has_vres = v_first is not None
    if c * 4 <= RWKV_CHUNK:
        npairs = C_HEADS // 2
    w = npairs * LANES
    ngroups = d // w
    midw = mid.shape[1]
    strict, incl, eye, hones = (jnp.asarray(x) for x in _rwkv_consts(c))
    row = lambda b, p, l: b * nl_blocks + l
    blk3 = lambda which: pl.BlockSpec((1, lblk, w), lambda b, p, l: (which, row(b, p, l), p))
    blk2 = pl.BlockSpec((lblk, w), lambda b, p, l: (row(b, p, l), p))
    par = pl.BlockSpec((1, w), lambda b, p, l: (0, p))
    cols = lambda arr: pl.BlockSpec((arr.shape[0], w), lambda b, p, l: (0, p))
    full = lambda arr: pl.BlockSpec(arr.shape, lambda b, p, l: (0,) * arr.ndim)
    w0, w2, a0, a2, g2 = low2[:5]
    in_specs = [blk3(0), blk3(1), blk3(2), pl.BlockSpec((lblk, midw), lambda b, p, l: (row(b, p, l), 0)),
                par, cols(w2), par, cols(a2), cols(g2)]
    args = [rkv, rkv, rkv, mid, w0.reshape(1, d), w2, a0.reshape(1, d), a2, g2]
    if has_vres:
        v0, v2 = low2[5:]
        in_specs += [blk3(2), par, cols(v2)]
        args += [v_first, v0.reshape(1, d), v2]
    in_specs += [par] * 5
    args += [x.reshape(1, d) for x in (kk_p, ka_p, rk_p, lnx_g, lnx_b)]
    in_specs += [full(x) for x in (strict, incl, eye, hones)]
    args += [strict, incl, eye, hones]
    st_blk = pl.BlockSpec((1, 2 * npairs, C_HEAD, C_HEAD), lambda b, p, l: (b, p, 0, 0))
    if has_state:
        in_specs.append(pl.BlockSpec((None, 1, 2 * npairs, C_HEAD, C_HEAD),
                                     lambda b, p, l: (layer, b, p, 0, 0)))
        args.append(s0)
    kern = functools.partial(_rwkv_core_kernel, chunk=c, nchunks=lblk // c, npairs=npairs,
                             has_state=has_state, has_vres=has_vres)
    return pl.pallas_call(
        kern,
        grid=(batch, ngroups, nl_blocks),
        in_specs=in_specs,
        out_specs=[blk2, st_blk],
        out_shape=[jax.ShapeDtypeStruct((m, d), BF16 if lblk % 16 == 0 else F32),
                   jax.ShapeDtypeStruct((batch, C_HEADS, C_HEAD, C_HEAD), F32)],
        scratch_shapes=[pltpu.VMEM((npairs, LANES, LANES), F32)]
        + [pltpu.VMEM((lblk, w), F32)] * (4 if has_vres else 3),
        compiler_params=_cparams(("parallel", "parallel", "arbitrary")),
        name="rwkv_core",
    )(*args)


def _weight_sources(p):
    src = {}
    for l in range(DEPTH):
        src[("up", l)] = [(p["w_up"], (l,))]
        src[("down", l)] = [(p["w_down"], (l,))]
    for e in range(N_EVEN):
        src[("in", e)] = [(p["w_in_even"], (e,))]
        src[("out", e)] = [(p["w_out_even"], (e,))]
    for o in range(N_ODD):
        src[("rkv", o)] = [(p["rw_wr"], (o,)), (p["rw_wk"], (o,)), (p["rw_wv"], (o,))]
        src[("wo", o)] = [(p["rw_wo"], (o,))]
    return src


def _bf16_weight(bank, src, key):
    if key not in bank:
        parts = [arr[lead].astype(BF16) for arr, lead in src[key]]
        bank[key] = parts[0] if len(parts) == 1 else jnp.stack(parts)
    return bank[key]


def _even_layer(x, batch, seqlen, e, layer, p, wget, st_hgrn, k_cache, v_cache):
    proj = norm_matmul(x, p["norm_mix_pre"][layer], wget(("in", e)))
    o_a, s_new = hgrn(proj, p["hgrn_lb_raw"], e, batch, seqlen, st_hgrn)
    k_lo = IN_A + B_WIDTH
    new_rows = min(seqlen, WINDOW)
    tails = jnp.stack([proj[(b + 1) * seqlen - new_rows:(b + 1) * seqlen, k_lo:] for b in range(batch)])
    kb = tails[:, :, :B_KV_WIDTH].reshape(batch, new_rows, B_KV_HEADS, B_HEAD_DIM)
    vb = tails[:, :, B_KV_WIDTH:].reshape(batch, new_rows, B_KV_HEADS, B_HEAD_DIM)
    o_b = swa(proj, batch, seqlen, p["rel_bias"], p["attn_sinks"][e], e, k_cache, v_cache)
    if k_cache is None:
        k_new, v_new = kb, vb
    else:
        k_new = jnp.concatenate([k_cache[e, :, new_rows:], kb], axis=1)
        v_new = jnp.concatenate([v_cache[e, :, new_rows:], vb], axis=1)
    x = even_out(o_a, proj, o_b, x, p["hgrn_norm_g"][e], wget(("out", e)), p["norm_mix_post"][layer])
    return x, s_new, k_new, v_new


def _odd_layer(x, batch, seqlen, o, layer, p, wget, shift0, s0, v_first):
    m, d = x.shape
    g_pre = p["norm_mix_pre"][layer]
    has_vres = o > 0
    width = lambda rng: rng[1] - rng[0]
    w1p, w2p = _pad_lora(p["rw_w1"][o], p["rw_w2"][o], width(LORA_W))
    a1p, a2p = _pad_lora(p["rw_a1"][o], p["rw_a2"][o], width(LORA_A))
    g1p, g2p = _pad_lora(p["rw_g1"][o], p["rw_g2"][o], width(LORA_G))
    first, low2 = [w1p, a1p, g1p], [p["rw_w0"][o], w2p, p["rw_a0"][o], a2p, g2p]
    if has_vres:
        v1p, v2p = _pad_lora(p["rw_v1"][o - 1], p["rw_v2"][o - 1], width(LORA_V))
        first.append(v1p)
        low2 += [p["rw_v0"][o - 1], v2p]
    rkv, mid = rwkv_in(x, g_pre, shift0, batch, seqlen, p["rw_mu"][o], wget(("rkv", o)),
                       jnp.concatenate(first, axis=1), has_vres)
    yg, s_new = rwkv_core(rkv, mid, low2, p["rw_kk"][o], p["rw_ka"][o], p["rw_rk"][o],
                          p["rw_lnx_g"][o], p["rw_lnx_b"][o], batch, seqlen, o, s0,
                          v_first if has_vres else None)
    shift_new = rmsnorm_rows(x.reshape(batch, seqlen, d)[:, -1], g_pre)
    x = odd_out(yg, x, wget(("wo", o)), p["norm_mix_post"][layer])
    return x, s_new, shift_new, rkv


def _trunk(x3, st_hgrn, k_cache, v_cache, st_rwkv, st_shift, p, bank):
    batch, seqlen, d = x3.shape
    x = x3.reshape(batch * seqlen, d)
    has_state = st_hgrn is not None
    hgrn_out, k_out, v_out, rwkv_out, shift_out = [], [], [], [], []
    v_first = None
    src = _weight_sources(p)
    wget = functools.partial(_bf16_weight, bank, src)
    nsteps = ffn_steps(batch * seqlen, D_FF)
    for layer in range(DEPTH):
        if layer % 2 == 0:
            e = layer // 2
            x, s_new, k_new, v_new = _even_layer(x, batch, seqlen, e, layer, p, wget, st_hgrn, k_cache, v_cache)
            hgrn_out.append(s_new)
            k_out.append(k_new)
            v_out.append(v_new)
        else:
            o = layer // 2
            x, s_new, sh_new, rkv = _odd_layer(
                x, batch, seqlen, o, layer, p, wget,
                st_shift[o] if has_state else None,
                st_rwkv,
                v_first)
            if o == 0:
                v_first = rkv
            rwkv_out.append(s_new)
            shift_out.append(sh_new)
        nl = layer + 1
        wanted = []
        if nl < DEPTH:
            wanted = [("up", nl), ("down", nl)]
            wanted += [("in", nl // 2), ("out", nl // 2)] if nl % 2 == 0 else [("rkv", nl // 2), ("wo", nl // 2)]
        jobs = [k for k in wanted
                if k not in bank and can_cast_in(nsteps, *src[k][0][0].shape[-2:])]
        x, cast = ffn(x, p["norm_ffn_pre"][layer], wget(("up", layer)), wget(("down", layer)),
                      p["norm_ffn_post"][layer], [src[k] for k in jobs])
        bank.update(zip(jobs, cast))
    return (x.reshape(batch, seqlen, d), jnp.stack(hgrn_out), jnp.stack(k_out), jnp.stack(v_out),
            jnp.stack(rwkv_out), jnp.stack(shift_out))


def kernel(x_prompt, x_sample, state_hgrn, cache_swa_k, cache_swa_v, state_rwkv, state_shift,
           norm_mix_pre, norm_mix_post, norm_ffn_pre, norm_ffn_post,
           w_in_even, w_out_even, hgrn_lb_raw, hgrn_norm_g, rel_bias, attn_sinks,
           rw_mu, rw_wr, rw_wk, rw_wv, rw_wo, rw_w0, rw_w1, rw_w2, rw_a0, rw_a1, rw_a2,
           rw_v0, rw_v1, rw_v2, rw_g1, rw_g2, rw_kk, rw_ka, rw_rk, rw_lnx_g, rw_lnx_b,
           w_up, w_down):
    p = {
        "norm_mix_pre": norm_mix_pre, "norm_mix_post": norm_mix_post,
        "norm_ffn_pre": norm_ffn_pre, "norm_ffn_post": norm_ffn_post,
        "w_in_even": w_in_even, "w_out_even": w_out_even,
        "hgrn_lb_raw": hgrn_lb_raw, "hgrn_norm_g": hgrn_norm_g,
        "rel_bias": rel_bias, "attn_sinks": attn_sinks,
        "rw_mu": rw_mu, "rw_wr": rw_wr, "rw_wk": rw_wk, "rw_wv": rw_wv, "rw_wo": rw_wo,
        "rw_w0": rw_w0, "rw_w1": rw_w1, "rw_w2": rw_w2, "rw_a0": rw_a0, "rw_a1": rw_a1, "rw_a2": rw_a2,
        "rw_v0": rw_v0, "rw_v1": rw_v1, "rw_v2": rw_v2, "rw_g1": rw_g1, "rw_g2": rw_g2,
        "rw_kk": rw_kk, "rw_ka": rw_ka, "rw_rk": rw_rk, "rw_lnx_g": rw_lnx_g, "rw_lnx_b": rw_lnx_b,
        "w_up": w_up, "w_down": w_down,
    }
    bank = {}
    y_p, hgrn_p, k_p, v_p, rwkv_p, shift_p = _trunk(x_prompt, None, None, None, None, None, p, bank)
    y_s, hgrn_s, k_s, v_s, rwkv_s, shift_s = _trunk(
        x_sample, state_hgrn, cache_swa_k, cache_swa_v, state_rwkv, state_shift, p, bank)
    return (y_p, y_s, hgrn_p, hgrn_s, k_p, k_s, v_p, v_s, rwkv_p, rwkv_s, shift_p, shift_s)
```

```python
import functools
import math

import numpy as np
import jax
import jax.numpy as jnp
from jax import lax
from jax.experimental import pallas as pl
from jax.experimental.pallas import tpu as pltpu

F32 = jnp.float32
BF16 = jnp.bfloat16

D_MODEL = 2048
DEPTH = 4
N_EVEN = 2
N_ODD = 2
A_HEADS = 8
A_KDIM = 128
A_VDIM = 128
A_WIDTH = 1024
A_QK = 1024
B_HEADS = 16
B_HEAD_DIM = 64
B_KV_HEADS = 4
B_GROUP = 4
B_WIDTH = 1024
B_KV_WIDTH = 256
WINDOW = 128
N_BUCKETS = 32
MAX_DISTANCE = 128
MASK_VALUE = -1e30
IN_A = 4096
IN_EVEN = 5632
C_HEAD = 64
C_HEADS = 32
GN_EPS = 64e-5
D_FF = 8192
NORM_EPS = 1e-6

LANES = 128
VMEM_LIMIT = 56 * 1024 * 1024

HGRN_CHUNK = 128
RWKV_CHUNK = 64
RWKV_TRI_BLOCK = 16


def _cparams(sem):
    return pltpu.CompilerParams(dimension_semantics=sem, vmem_limit_bytes=VMEM_LIMIT)


def _rms(x, g):
    return x * lax.rsqrt(jnp.mean(x * x, axis=-1, keepdims=True) + NORM_EPS) * g


def _dot(a, b):
    return jnp.dot(a.astype(BF16), b.astype(BF16), preferred_element_type=F32)


def _dot_nt(a, b):
    return lax.dot_general(a.astype(BF16), b.astype(BF16), (((1,), (1,)), ((), ())),
                           preferred_element_type=F32)


def _dot_tn(a, b):
    return lax.dot_general(a.astype(BF16), b.astype(BF16), (((0,), (0,)), ((), ())),
                           preferred_element_type=F32)


def _dot_shared(lhs_list, b):
    res = _dot(jnp.concatenate(lhs_list, axis=0), b)
    out, off = [], 0
    for a in lhs_list:
        out.append(res[off:off + a.shape[0]])
        off += a.shape[0]
    return out


def _pick_tile(m, cands):
    for c in cands:
        if m % c == 0:
            return c
    return m


def _norm_matmul_kernel(x_ref, g_ref, w_ref, o_ref, xn_ref):
    @pl.when(pl.program_id(1) == 0)
    def _():
        xn_ref[...] = _rms(x_ref[...], g_ref[...]).astype(BF16)

    o_ref[...] = jnp.dot(xn_ref[...], w_ref[...], preferred_element_type=F32)


def norm_matmul(x, g, w_bf16, tn=1408):
    m, d = x.shape
    n = w_bf16.shape[1]
    tm = _pick_tile(m, (1024, 512, 256, 128, 64, 32, 16, 8))
    return pl.pallas_call(
        _norm_matmul_kernel,
        grid=(m // tm, n // tn),
        in_specs=[pl.BlockSpec((tm, d), lambda i, j: (i, 0)),
                  pl.BlockSpec((1, d), lambda i, j: (0, 0)),
                  pl.BlockSpec((d, tn), lambda i, j: (0, j))],
        out_specs=pl.BlockSpec((tm, tn), lambda i, j: (i, j)),
        out_shape=jax.ShapeDtypeStruct((m, n), F32),
        scratch_shapes=[pltpu.VMEM((tm, d), BF16)],
        compiler_params=_cparams(("parallel", "arbitrary")),
        name="norm_matmul",
    )(x, g.reshape(1, d), w_bf16)


def _ffn_kernel(*refs, cast_srcs):
    nsrc = sum(cast_srcs)
    x_ref, gpre_ref, wup_ref, wdn_ref, gpost_ref = refs[:5]
    src_refs = refs[5:5 + nsrc]
    o_ref = refs[5 + nsrc]
    cast_out = refs[6 + nsrc:6 + nsrc + len(cast_srcs)]
    xn_ref, acc_ref = refs[-2:]
    k = 0
    for out_ref, n in zip(cast_out, cast_srcs):
        for j in range(n):
            if n == 1:
                out_ref[...] = src_refs[k][...].astype(BF16)
            else:
                out_ref[j] = src_refs[k][...].astype(BF16)
            k += 1
    f = pl.program_id(1)

    @pl.when(f == 0)
    def _():
        xn_ref[...] = _rms(x_ref[...], gpre_ref[...]).astype(BF16)
        acc_ref[...] = jnp.zeros_like(acc_ref)

    h = jnp.dot(xn_ref[...], wup_ref[...], preferred_element_type=F32)
    h = jnp.square(jnp.maximum(h, 0.0)).astype(BF16)
    acc_ref[...] += jnp.dot(h, wdn_ref[...], preferred_element_type=F32)

    @pl.when(f == pl.num_programs(1) - 1)
    def _():
        o_ref[...] = x_ref[...] + _rms(acc_ref[...], gpost_ref[...])


BF16_SUBLANES = 16


def ffn_steps(m, dff, tf=1024):
    return (m // _pick_tile(m, (512, 256, 128, 64, 32, 16, 8))) * (dff // tf)


CAST_BLOCK_BYTES = 1 << 20


def can_cast_in(nsteps, rows, cols):
    return (rows % nsteps == 0 and (rows // nsteps) % BF16_SUBLANES == 0
            and (rows // nsteps) * cols * 4 <= CAST_BLOCK_BYTES)


def ffn(x, gpre, wup_bf16, wdn_bf16, gpost, casts=(), tf=1024):
    m, d = x.shape
    dff = wup_bf16.shape[1]
    tm = _pick_tile(m, (512, 256, 128, 64, 32, 16, 8))
    nf = dff // tf
    nsteps = (m // tm) * nf
    step = lambda i, f: i * nf + f
    in_specs = [pl.BlockSpec((tm, d), lambda i, f: (i, 0)),
                pl.BlockSpec((1, d), lambda i, f: (0, 0)),
                pl.BlockSpec((d, tf), lambda i, f: (0, f)),
                pl.BlockSpec((tf, d), lambda i, f: (f, 0)),
                pl.BlockSpec((1, d), lambda i, f: (0, 0))]
    args = [x, gpre.reshape(1, d), wup_bf16, wdn_bf16, gpost.reshape(1, d)]
    out_specs = [pl.BlockSpec((tm, d), lambda i, f: (i, 0))]
    out_shape = [jax.ShapeDtypeStruct((m, d), F32)]
    for job in casts:
        rows, cols = job[0][0].shape[len(job[0][1]):]
        assert can_cast_in(nsteps, rows, cols)
        rb = rows // nsteps
        for arr, lead in job:
            in_specs.append(pl.BlockSpec((None,) * len(lead) + (rb, cols),
                                         lambda i, f, lead=lead: lead + (step(i, f), 0)))
            args.append(arr)
        if len(job) == 1:
            out_specs.append(pl.BlockSpec((rb, cols), lambda i, f: (step(i, f), 0)))
            out_shape.append(jax.ShapeDtypeStruct((rows, cols), BF16))
        else:
            out_specs.append(pl.BlockSpec((len(job), rb, cols), lambda i, f: (0, step(i, f), 0)))
            out_shape.append(jax.ShapeDtypeStruct((len(job), rows, cols), BF16))
    outs = pl.pallas_call(
        functools.partial(_ffn_kernel, cast_srcs=tuple(len(job) for job in casts)),
        grid=(m // tm, nf),
        in_specs=in_specs,
        out_specs=out_specs,
        out_shape=out_shape,
        scratch_shapes=[pltpu.VMEM((tm, d), BF16), pltpu.VMEM((tm, d), F32)],
        compiler_params=_cparams(("parallel", "arbitrary")),
        name="ffn",
    )(*args)
    return outs[0], list(outs[1:])


def _even_out_kernel(oa_ref, ga_ref, ob_ref, x_ref, ag_ref, w_ref, gpost_ref, o_ref):
    ga = ga_ref[...]
    oan = _rms(oa_ref[...], ag_ref[...]) * (ga * jax.nn.sigmoid(ga))
    mix = (jnp.dot(oan.astype(BF16), w_ref[:A_WIDTH, :], preferred_element_type=F32)
           + jnp.dot(ob_ref[...].astype(BF16), w_ref[A_WIDTH:, :], preferred_element_type=F32))
    o_ref[...] = x_ref[...] + _rms(mix, gpost_ref[...])


def even_out(o_a, proj, o_b, x, a_norm_g, w_out_bf16, gpost):
    m, d = x.shape
    tm = _pick_tile(m, (384, 256, 128, 64, 32, 16, 8))
    ga_blk = (3 * A_WIDTH) // A_WIDTH
    return pl.pallas_call(
        _even_out_kernel,
        grid=(m // tm,),
        in_specs=[pl.BlockSpec((tm, A_WIDTH), lambda i: (i, 0)),
                  pl.BlockSpec((tm, A_WIDTH), lambda i: (i, ga_blk)),
                  pl.BlockSpec((tm, B_WIDTH), lambda i: (i, 0)),
                  pl.BlockSpec((tm, d), lambda i: (i, 0)),
                  pl.BlockSpec((1, A_WIDTH), lambda i: (0, 0)),
                  pl.BlockSpec((A_WIDTH + B_WIDTH, d), lambda i: (0, 0)),
                  pl.BlockSpec((1, d), lambda i: (0, 0))],
        out_specs=pl.BlockSpec((tm, d), lambda i: (i, 0)),
        out_shape=jax.ShapeDtypeStruct((m, d), F32),
        compiler_params=_cparams(("parallel",)),
        name="even_out",
    )(o_a, proj, o_b, x, a_norm_g.reshape(1, A_WIDTH), w_out_bf16, gpost.reshape(1, d))


def _odd_out_kernel(y_ref, x_ref, w_ref, gpost_ref, o_ref):
    mix = jnp.dot(y_ref[...].astype(BF16), w_ref[...], preferred_element_type=F32)
    o_ref[...] = x_ref[...] + _rms(mix, gpost_ref[...])


def odd_out(yg, x, wo_bf16, gpost):
    m, d = x.shape
    tm = _pick_tile(m, (384, 256, 128, 64, 32, 16, 8))
    return pl.pallas_call(
        _odd_out_kernel,
        grid=(m // tm,),
        in_specs=[pl.BlockSpec((tm, d), lambda i: (i, 0)),
                  pl.BlockSpec((tm, d), lambda i: (i, 0)),
                  pl.BlockSpec((d, d), lambda i: (0, 0)),
                  pl.BlockSpec((1, d), lambda i: (0, 0))],
        out_specs=pl.BlockSpec((tm, d), lambda i: (i, 0)),
        out_shape=jax.ShapeDtypeStruct((m, d), F32),
        compiler_params=_cparams(("parallel",)),
        name="odd_out",
    )(yg, x, wo_bf16, gpost.reshape(1, d))


def _level_consts(c):
    levels = []
    s = c // 2
    while s >= 1:
        levels.append(s)
        s //= 2
    mask = np.zeros((len(levels), c, c), np.float32)
    idx = np.arange(c)
    for l, s in enumerate(levels):
        same = (idx[:, None] // (2 * s)) == (idx[None, :] // (2 * s))
        upper = (idx[:, None] % (2 * s)) >= s
        lower = (idx[None, :] % (2 * s)) < s
        mask[l] = (same & upper & lower).astype(np.float32)
    return levels, mask


def _split_rows(g, s, rowid):
    c = g.shape[0]
    if 2 * s >= 8:
        return jnp.concatenate(
            [jnp.broadcast_to(g[b + s - 1:b + s, :], (2 * s, g.shape[1])) for b in range(0, c, 2 * s)], axis=0)
    r = rowid % (2 * s)
    out = g
    for off in range(-(s - 1), s + 1):
        if off != 0:
            out = jnp.where(r == s - 1 + off, pltpu.roll(g, off % c, 0), out)
    return out


def _hgrn_kernel(*refs, layer, chunk, nchunks, levels, has_state, nheads):
    if has_state:
        q_ref, f_ref, i_ref, lb_ref, mask_ref, s0_ref, o_ref, s_ref, st_ref = refs
    else:
        q_ref, f_ref, i_ref, lb_ref, mask_ref, o_ref, s_ref, st_ref = refs
    c = chunk
    nh = nheads
    rowid = lax.broadcasted_iota(jnp.int32, (c, nh * LANES), 0)
    l_idx = pl.program_id(2)

    @pl.when(l_idx == 0)
    def _():
        for hi in range(nh):
            if has_state:
                st_ref[hi] = s0_ref[0, hi].T
            else:
                st_ref[hi] = jnp.zeros((A_VDIM, A_KDIM), F32)

    lbr = lb_ref[...]
    e = jnp.exp(lbr - jnp.max(lbr, axis=0, keepdims=True))
    p = e / jnp.sum(e, axis=0, keepdims=True)
    lb = jnp.zeros((1, nh * LANES), F32)
    for i in range(1, layer + 1):
        lb = lb + p[i:i + 1, :]
    one_m_lb = 1.0 - lb
    head = lambda x, hi: x[:, hi * LANES:(hi + 1) * LANES]

    def body(ci, carry):
        rows = pl.ds(pl.multiple_of(ci * c, c), c)
        fq = f_ref[rows, :]
        qr = q_ref[rows, :]
        v = i_ref[rows, :]
        f = lb + one_m_lb * jax.nn.sigmoid(fq)
        k = 1.0 - f
        q = qr * jax.nn.sigmoid(qr) * (A_KDIM ** -0.5)

        g = jnp.log2(f)
        sft = 1
        while sft < c:
            g = g + jnp.where(rowid >= sft, pltpu.roll(g, sft, 0), 0.0)
            sft *= 2
        glast = g[c - 1:c, :]
        q_in = q * jnp.exp2(g)
        kd = k * jnp.exp2(glast - g)
        dec = jnp.exp2(glast)
        diag = q * k

        sts = [st_ref[hi] for hi in range(nh)]
        os_ = [_dot_nt(head(q_in, hi), sts[hi]) for hi in range(nh)]
        attns = [jnp.zeros((c, c), F32) for _ in range(nh)]
        for l, s in enumerate(levels):
            e = jnp.exp2(-jnp.abs(g - _split_rows(g, s, rowid)))
            qs = q * e
            ks = k * e
            ml = mask_ref[l]
            attns = [at + ml * _dot_nt(head(qs, hi), head(ks, hi)) for hi, at in enumerate(attns)]
        for hi in range(nh):
            vh = head(v, hi)
            o = os_[hi] + _dot(attns[hi], vh) + jnp.sum(head(diag, hi), axis=-1, keepdims=True) * vh
            o_ref[rows, hi * LANES:(hi + 1) * LANES] = o
        for hi in range(nh):
            st_ref[hi] = sts[hi] * head(dec, hi) + _dot_tn(head(v, hi), head(kd, hi))
        return carry

    lax.fori_loop(0, nchunks, body, 0, unroll=2 if nchunks % 2 == 0 else 1)

    @pl.when(l_idx == pl.num_programs(2) - 1)
    def _():
        for hi in range(nh):
            s_ref[0, hi] = st_ref[hi].T


def hgrn(proj, lb_raw, layer, batch, seqlen, s0=None, nheads=4):
    m = proj.shape[0]
    c = math.gcd(seqlen, HGRN_CHUNK)
    lblk = _pick_tile(seqlen, (512, 256, 128, 64, 32, 16, 8))
    nl_blocks = seqlen // lblk
    levels, mask = _level_consts(c)
    has_state = s0 is not None
    if c * 4 <= HGRN_CHUNK:
        nheads = A_HEADS
    w = nheads * LANES
    ngroups = A_HEADS // nheads
    kern = functools.partial(_hgrn_kernel, layer=layer, chunk=c, nchunks=lblk // c,
                             levels=tuple(levels), has_state=has_state, nheads=nheads)
    row = lambda b, h, l: b * nl_blocks + l
    in_specs = [pl.BlockSpec((lblk, w), lambda b, h, l: (row(b, h, l), h)),
                pl.BlockSpec((lblk, w), lambda b, h, l: (row(b, h, l), ngroups + h)),
                pl.BlockSpec((lblk, w), lambda b, h, l: (row(b, h, l), 2 * ngroups + h)),
                pl.BlockSpec((N_EVEN, w), lambda b, h, l: (0, h)),
                pl.BlockSpec((len(levels), c, c), lambda b, h, l: (0, 0, 0))]
    args = [proj, proj, proj, lb_raw, jnp.asarray(mask)]
    st_blk = pl.BlockSpec((1, nheads, A_KDIM, A_VDIM), lambda b, h, l: (b, h, 0, 0))
    if has_state:
        in_specs.append(pl.BlockSpec((None, 1, nheads, A_KDIM, A_VDIM), lambda b, h, l: (layer, b, h, 0, 0)))
        args.append(s0)
    return pl.pallas_call(
        kern,
        grid=(batch, ngroups, nl_blocks),
        in_specs=in_specs,
        out_specs=[pl.BlockSpec((lblk, w), lambda b, h, l: (row(b, h, l), h)), st_blk],
        out_shape=[jax.ShapeDtypeStruct((m, A_WIDTH), F32),
                   jax.ShapeDtypeStruct((batch, A_HEADS, A_KDIM, A_VDIM), F32)],
        scratch_shapes=[pltpu.VMEM((nheads, A_VDIM, A_KDIM), F32)],
        compiler_params=_cparams(("parallel", "parallel", "arbitrary")),
        name="hgrn",
    )(*args)


def _t5_bucket(dist):
    max_exact = N_BUCKETS // 2
    d = np.maximum(dist, 0)
    large = max_exact + (np.log(np.maximum(d, max_exact).astype(np.float32) / max_exact)
                         / math.log(MAX_DISTANCE / max_exact) * (N_BUCKETS - max_exact)).astype(np.int32)
    large = np.minimum(large, N_BUCKETS - 1)
    return np.where(d < max_exact, d, large).astype(np.int32)


def _swa_kernel(q_ref, kp_ref, kc_ref, vp_ref, vc_ref, bucket_ref, band_ref, rb_ref, sink_ref,
                o_ref, bias_ref, *, qb, span, prev_always_valid):
    first = (pl.program_id(0) == 0) & (pl.program_id(1) == 0)

    @pl.when(first)
    def _():
        bk = bucket_ref[...]
        band = band_ref[...]

        def per_head(h, carry):
            def per_bucket(bi, acc):
                return jnp.where(bk == bi, rb_ref[bi, h], acc)
            acc = lax.fori_loop(0, N_BUCKETS, per_bucket, jnp.zeros((qb, span), F32))
            bias_ref[h] = jnp.where(band > 0, acc, MASK_VALUE)
            return carry

        lax.fori_loop(0, B_HEADS, per_head, 0)

    scale = B_HEAD_DIM ** -0.5
    q = q_ref[...]
    kall = jnp.concatenate([kp_ref[...], kc_ref[...]], axis=0)
    vall = jnp.concatenate([vp_ref[...], vc_ref[...]], axis=0)
    if not prev_always_valid:
        col = lax.broadcasted_iota(jnp.int32, (qb, span), 1)
        no_prev = (col < WINDOW) & (pl.program_id(1) == 0)
    heads = range(B_HEADS)
    ks = [kall[:, kh * B_HEAD_DIM:(kh + 1) * B_HEAD_DIM].astype(BF16) for kh in range(B_KV_HEADS)]
    vs = [vall[:, kh * B_HEAD_DIM:(kh + 1) * B_HEAD_DIM].astype(BF16) for kh in range(B_KV_HEADS)]
    qs = [(q[:, h * B_HEAD_DIM:(h + 1) * B_HEAD_DIM] * scale).astype(BF16) for h in heads]
    ss = [_dot_nt(qs[h], ks[h // B_GROUP]) + bias_ref[h] for h in heads]
    if not prev_always_valid:
        ss = [jnp.where(no_prev, MASK_VALUE, s) for s in ss]
    ms = [jnp.maximum(jnp.max(ss[h], axis=-1, keepdims=True), sink_ref[h]) for h in heads]
    ps = [jnp.exp(s - m) for s, m in zip(ss, ms)]
    denoms = [jnp.sum(ps[h], axis=-1, keepdims=True) + jnp.exp(sink_ref[h] - ms[h]) for h in heads]
    outs = [_dot(ps[h], vs[h // B_GROUP]) / denoms[h] for h in heads]
    o_ref[...] = jnp.concatenate(outs, axis=1).astype(o_ref.dtype)


def swa(proj, batch, seqlen, rel_bias, sinks, layer, k_past=None, v_past=None):
    m = proj.shape[0]
    has_cache = k_past is not None
    qb = math.gcd(seqlen, WINDOW)
    nb = seqlen // qb
    span = WINDOW + qb
    dist = np.arange(qb)[:, None] + WINDOW - np.arange(span)[None, :]
    band = ((dist >= 0) & (dist < WINDOW)).astype(np.float32)
    bucket = _t5_bucket(dist)
    q_col = IN_A // B_WIDTH
    k_col = (IN_A + B_WIDTH) // B_KV_WIDTH
    v_col = k_col + 1
    cur = lambda c: (lambda b, n: (b * nb + n, c))
    if has_cache:
        assert nb == 1
        prev_k = pl.BlockSpec((None, WINDOW, B_KV_WIDTH), lambda b, n: (layer, b, 0))
        prev_v = pl.BlockSpec((None, WINDOW, B_KV_WIDTH), lambda b, n: (layer, b, 0))
        kp_arr = k_past.reshape(k_past.shape[0], batch * WINDOW, B_KV_WIDTH)
        vp_arr = v_past.reshape(v_past.shape[0], batch * WINDOW, B_KV_WIDTH)
    else:
        assert qb == WINDOW
        prev = lambda c: (lambda b, n: (b * nb + jnp.maximum(n - 1, 0), c))
        prev_k = pl.BlockSpec((WINDOW, B_KV_WIDTH), prev(k_col))
        prev_v = pl.BlockSpec((WINDOW, B_KV_WIDTH), prev(v_col))
        kp_arr, vp_arr = proj, proj
    kern = functools.partial(_swa_kernel, qb=qb, span=span, prev_always_valid=has_cache)
    return pl.pallas_call(
        kern,
        grid=(batch, nb),
        in_specs=[pl.BlockSpec((qb, B_WIDTH), cur(q_col)),
                  prev_k,
                  pl.BlockSpec((qb, B_KV_WIDTH), cur(k_col)),
                  prev_v,
                  pl.BlockSpec((qb, B_KV_WIDTH), cur(v_col)),
                  pl.BlockSpec((qb, span), lambda b, n: (0, 0)),
                  pl.BlockSpec((qb, span), lambda b, n: (0, 0)),
                  pl.BlockSpec(memory_space=pltpu.SMEM),
                  pl.BlockSpec(memory_space=pltpu.SMEM)],
        out_specs=pl.BlockSpec((qb, B_WIDTH), lambda b, n: (b * nb + n, 0)),
        out_shape=jax.ShapeDtypeStruct((m, B_WIDTH), BF16 if qb % 16 == 0 else F32),
        scratch_shapes=[pltpu.VMEM((B_HEADS, qb, span), F32)],
        compiler_params=_cparams(("arbitrary", "arbitrary")),
        name="swa",
    )(proj, kp_arr, proj, vp_arr, proj, jnp.asarray(bucket), jnp.asarray(band), rel_bias, sinks)


def _rmsnorm_kernel(x_ref, g_ref, o_ref):
    o_ref[...] = _rms(x_ref[...], g_ref[...])


def rmsnorm_rows(x, g):
    m, d = x.shape
    tm = _pick_tile(m, (512, 256, 128, 64, 32, 16, 8))
    return pl.pallas_call(
        _rmsnorm_kernel,
        grid=(m // tm,),
        in_specs=[pl.BlockSpec((tm, d), lambda i: (i, 0)), pl.BlockSpec((1, d), lambda i: (0, 0))],
        out_specs=pl.BlockSpec((tm, d), lambda i: (i, 0)),
        out_shape=jax.ShapeDtypeStruct((m, d), F32),
        compiler_params=_cparams(("parallel",)),
        name="rmsnorm",
    )(x, g.reshape(1, d))


LORA_W = (0, 128)
LORA_A = (128, 256)
LORA_G = (256, 512)
LORA_V = (512, 640)


def _rwkv_in_kernel(x_ref, xp_ref, s_ref, g_ref, mu3_ref, mul_ref, w_ref, w1_ref,
                    rkv_ref, mid_ref, h_s, hp_s, *, tm, seqlen, has_vres):
    ph = pl.program_id(1)

    @pl.when(ph == 0)
    def _():
        g = g_ref[...]
        h = _rms(x_ref[...], g)
        rowid = lax.broadcasted_iota(jnp.int32, h.shape, 0)
        rolled = pltpu.roll(h, 1, 0)
        if seqlen % tm == 0:
            prev_last = _rms(xp_ref[...], g)[7:8, :]
            at_start = pl.program_id(0) % (seqlen // tm) == 0
            first = jnp.where(at_start, s_ref[...], prev_last)
            hp = jnp.where(rowid == 0, first, rolled)
        else:
            hp = jnp.where(rowid % seqlen == 0, s_ref[...], rolled)
        h_s[...] = h
        hp_s[...] = hp

    @pl.when(ph < 3)
    def _():
        h = h_s[...]
        xm = (h + (hp_s[...] - h) * mu3_ref[0]).astype(BF16)
        rkv_ref[0] = jnp.dot(xm, w_ref[...], preferred_element_type=F32)

    @pl.when(ph == 3)
    def _():
        h = h_s[...]
        xx = hp_s[...] - h
        mix = lambda i: (h + xx * mul_ref[i:i + 1, :]).astype(BF16)
        low = lambda i, rng: jnp.dot(mix(i), w1_ref[:, rng[0]:rng[1]], preferred_element_type=F32)
        parts = [jnp.tanh(low(0, LORA_W)), low(1, LORA_A), jax.nn.sigmoid(low(2, LORA_G))]
        if has_vres:
            parts.append(low(3, LORA_V))
        mid_ref[...] = jnp.concatenate(parts, axis=1).astype(BF16)


def rwkv_in(x, g, shift0, batch, seqlen, mu, w3_bf16, w1cat_bf16, has_vres):
    m, d = x.shape
    tm = next(t for t in (512, 256, 128, 64, 32, 16, 8)
              if m % t == 0 and (seqlen % t == 0 or t % seqlen == 0))
    midw = w1cat_bf16.shape[1]
    mu_rkv = jnp.stack([mu[0], mu[2], mu[3]])[:, None, :]
    mu_low = jnp.stack([mu[1], mu[4], mu[5], mu[3]])
    if shift0 is None:
        shift0 = jnp.zeros((batch, d), F32)
    if seqlen % tm == 0:
        srow = shift0[:, None, :]
        tps = seqlen // tm
        s_spec = pl.BlockSpec((None, 1, d), lambda i, p: (i // tps, 0, 0))
    else:
        srow = jnp.repeat(shift0, seqlen, axis=0)
        s_spec = pl.BlockSpec((tm, d), lambda i, p: (i, 0))
    sub = tm // 8
    kern = functools.partial(_rwkv_in_kernel, tm=tm, seqlen=seqlen, has_vres=has_vres)
    return pl.pallas_call(
        kern,
        grid=(m // tm, 4),
        in_specs=[pl.BlockSpec((tm, d), lambda i, p: (i, 0)),
                  pl.BlockSpec((8, d), lambda i, p: (jnp.maximum(i * sub - 1, 0), 0)),
                  s_spec,
                  pl.BlockSpec((1, d), lambda i, p: (0, 0)),
                  pl.BlockSpec((1, 1, d), lambda i, p: (jnp.minimum(p, 2), 0, 0)),
                  pl.BlockSpec((4, d), lambda i, p: (0, 0)),
                  pl.BlockSpec((None, d, d), lambda i, p: (jnp.minimum(p, 2), 0, 0)),
                  pl.BlockSpec((d, midw), lambda i, p: (0, 0))],
        out_specs=[pl.BlockSpec((1, tm, d), lambda i, p: (jnp.minimum(p, 2), i, 0)),
                   pl.BlockSpec((tm, midw), lambda i, p: (i, 0))],
        out_shape=[jax.ShapeDtypeStruct((3, m, d), F32),
                   jax.ShapeDtypeStruct((m, midw), BF16)],
        scratch_shapes=[pltpu.VMEM((tm, d), F32), pltpu.VMEM((tm, d), F32)],
        compiler_params=_cparams(("parallel", "arbitrary")),
        name="rwkv_in",
    )(x, x, srow, g.reshape(1, d), mu_rkv, mu_low, w3_bf16, w1cat_bf16)


def _pad_lora(w1, w2, width):
    r = w1.shape[1]
    return (jnp.pad(w1, ((0, 0), (0, width - r))).astype(BF16),
            jnp.pad(w2, ((0, width - r), (0, 0))).astype(BF16))


def _rwkv_consts(c):
    i = np.arange(2 * c)
    same = (i[:, None] // c) == (i[None, :] // c)
    strict = (same & ((i[:, None] % c) > (i[None, :] % c))).astype(np.float32)
    incl = (same & ((i[:, None] % c) >= (i[None, :] % c))).astype(np.float32)
    eye = np.eye(2 * c, dtype=np.float32)
    l = np.arange(LANES)
    headones = ((l[:, None] // C_HEAD) == (l[None, :] // C_HEAD)).astype(np.float32)
    bs = min(RWKV_TRI_BLOCK, c)
    blk = lambda n: (i[:, None] // n) == (i[None, :] // n)
    tmasks = [blk(bs)]
    s = bs
    while s < c:
        tmasks.append(blk(2 * s) & ~blk(s))
        s *= 2
    return strict, incl, eye, headones, np.stack(tmasks).astype(np.float32), bs


def _rwkv_core_kernel(*refs, chunk, nchunks, npairs, has_state, has_vres, bs):
    it = iter(refs)
    r_ref, k_ref, v_ref, mid_ref = (next(it) for _ in range(4))
    w0_ref, w2_ref, a0_ref, a2_ref, g2_ref = (next(it) for _ in range(5))
    if has_vres:
        vf_ref, v0_ref, v2_ref = next(it), next(it), next(it)
    kk_ref, ka_ref, rk_ref, lg_ref, lb_ref = (next(it) for _ in range(5))
    strict_ref, incl_ref, eye_ref, hones_ref, tmask_ref = (next(it) for _ in range(5))
    if has_state:
        s0_ref = next(it)
    y_ref, s_ref, st_ref, wl_ref, a_ref, g_ref = (next(it) for _ in range(6))
    if has_vres:
        vg_ref = next(it)
    c = chunk
    l_idx = pl.program_id(2)
    lane = lax.broadcasted_iota(jnp.int32, (1, LANES), 1)
    m0 = (lane < C_HEAD).astype(F32)
    m1 = 1.0 - m0

    low = lambda rng, w_ref: jnp.dot(mid_ref[:, rng[0]:rng[1]], w_ref[...], preferred_element_type=F32)
    wl_ref[...] = -math.exp(-0.5) * jax.nn.sigmoid(w0_ref[...] + low(LORA_W, w2_ref))
    a_ref[...] = jax.nn.sigmoid(a0_ref[...] + low(LORA_A, a2_ref))
    g_ref[...] = low(LORA_G, g2_ref)
    if has_vres:
        vg_ref[...] = jax.nn.sigmoid(v0_ref[...] + low(LORA_V, v2_ref))

    @pl.when(l_idx == 0)
    def _():
        for pi in range(npairs):
            if has_state:
                z = jnp.zeros((C_HEAD, C_HEAD), F32)
                top = jnp.concatenate([s0_ref[0, 2 * pi], z], axis=1)
                bot = jnp.concatenate([z, s0_ref[0, 2 * pi + 1]], axis=1)
                st_ref[pi] = jnp.concatenate([top, bot], axis=0)
            else:
                st_ref[pi] = jnp.zeros((LANES, LANES), F32)

    strict = strict_ref[...]
    incl = incl_ref[...]
    eye = eye_ref[...]

    def stack(x):
        return jnp.concatenate([x * m0, x * m1], axis=0)

    first_head = lax.broadcasted_iota(jnp.int32, (c, LANES), 1) < C_HEAD
    rowid = lax.broadcasted_iota(jnp.int32, (c, LANES), 0)

    def rowsums(xs):
        return [jnp.where(first_head,
                          jnp.sum(x * m0, axis=-1, keepdims=True),
                          jnp.sum(x * m1, axis=-1, keepdims=True)) for x in xs]

    def cumsum_rows(x):
        s = 1
        while s < c:
            x = x + jnp.where(rowid >= s, pltpu.roll(x, s, 0), 0.0)
            s *= 2
        return x

    def load(pi, rows):
        cols = slice(pi * LANES, (pi + 1) * LANES)
        k = k_ref[0, rows, cols]
        v = v_ref[0, rows, cols]
        a = a_ref[rows, cols]
        if has_vres:
            v = v + (vf_ref[0, rows, cols] - v) * vg_ref[rows, cols]
        return dict(cols=cols, r=r_ref[0, rows, cols], v=v, a=a, wl=wl_ref[rows, cols],
                    kr=k * kk_ref[:, cols], kh=k * (1.0 + (a - 1.0) * ka_ref[:, cols]))

    def decays(p, ss):
        kk = p["kr"] * lax.rsqrt(jnp.maximum(ss, 1e-24))
        b = kk * p["a"]
        gc = cumsum_rows(p["wl"])
        gl = gc[c - 1:c, :]
        e_neg = jnp.exp(-gc)
        e_out = jnp.exp(gl - gc)
        p.update(gl=gl, ab=-kk * jnp.exp(gc - p["wl"]), rb=p["r"] * jnp.exp(gc),
                 bt=b * e_neg, kt=p["kh"] * e_neg, bh=b * e_out, khat=p["kh"] * e_out)

    def intra(p):
        lhs = jnp.concatenate([stack(p["ab"]), stack(p["rb"])], axis=0)
        with_b = _dot_nt(lhs, jnp.concatenate([p["bt"], p["bt"]], axis=0))
        with_k = _dot_nt(lhs, jnp.concatenate([p["kt"], p["kt"]], axis=0))
        p.update(a_ab=with_b[:2 * c] * strict, a_rb=with_b[2 * c:] * incl,
                 a_ak=with_k[:2 * c] * strict, a_rk=with_k[2 * c:] * incl)

    def body(ci, carry):
        rows = pl.ds(pl.multiple_of(ci * c, c), c)
        ps = [load(pi, rows) for pi in range(npairs)]
        for p, ss in zip(ps, rowsums([p["kr"] * p["kr"] for p in ps])):
            decays(p, ss)
        for p in ps:
            intra(p)

        pws = [p["a_ab"] * tmask_ref[0] for p in ps]
        ts = [eye + pw for pw in pws]
        if bs > 2:
            pws = [_dot(pw, pw) for pw in pws]
            n = 2
            while 2 * n < bs:
                res = [_dot_shared([t, pw], pw) for t, pw in zip(ts, pws)]
                ts = [t + r[0] for t, r in zip(ts, res)]
                pws = [r[1] for r in res]
                n *= 2
            ts = [t + _dot(t, pw) for t, pw in zip(ts, pws)]
        s, lvl = bs, 1
        while s < c:
            ms = [_dot(t, p["a_ab"] * tmask_ref[lvl]) for t, p in zip(ts, ps)]
            ts = [t + _dot(m, t) for t, m in zip(ts, ms)]
            s, lvl = 2 * s, lvl + 1

        sts = [st_ref[pi] for pi in range(npairs)]
        fss = [_dot_nt(jnp.concatenate([p["ab"], p["rb"]], axis=0), st) for p, st in zip(ps, sts)]
        vss = [stack(p["v"]) for p in ps]
        rhss = [stack(fs[:c]) + _dot(p["a_ak"], vs) for p, fs, vs in zip(ps, fss, vss)]
        uss = [_dot(t, rhs) for t, rhs in zip(ts, rhss)]
        yss = [_dot(p["a_rb"], us) + _dot(p["a_rk"], vs) for p, us, vs in zip(ps, uss, vss)]
        for pi, (p, st, us) in enumerate(zip(ps, sts, uss)):
            u = us[:c] + us[c:]
            upd = _dot_tn(jnp.concatenate([u, p["v"]], axis=0),
                          jnp.concatenate([p["bh"], p["khat"]], axis=0))
            st_ref[pi] = st * jnp.exp(p["gl"]) + upd * hones_ref[...]
        inv_n = 1.0 / C_HEAD
        ys_ = [fs[c:] + ys[:c] + ys[c:] for fs, ys in zip(fss, yss)]
        sums = rowsums(ys_ + [p["r"] * p["kh"] * rk_ref[:, p["cols"]] for p in ps])
        dlts = [y - m * inv_n for y, m in zip(ys_, sums[:npairs])]
        vars_ = rowsums([d * d for d in dlts])
        for p, dlt, var, bsum in zip(ps, dlts, vars_, sums[npairs:]):
            cols = p["cols"]
            yn = dlt * lax.rsqrt(var * inv_n + GN_EPS) * lg_ref[:, cols] + lb_ref[:, cols]
            y_ref[rows, cols] = ((yn + bsum * p["v"]) * g_ref[rows, cols]).astype(y_ref.dtype)
        return carry

    lax.fori_loop(0, nchunks, body, 0)

    @pl.when(l_idx == pl.num_programs(2) - 1)
    def _():
        for pi in range(npairs):
            st = st_ref[pi]
            s_ref[0, 2 * pi] = st[:C_HEAD, :C_HEAD]
            s_ref[0, 2 * pi + 1] = st[C_HEAD:, C_HEAD:]


def rwkv_core(rkv, mid, low2, kk_p, ka_p, rk_p, lnx_g, lnx_b, batch, seqlen, layer,
              s0=None, v_first=None, npairs=8):
    _, m, d = rkv.shape
    c = math.gcd(seqlen, RWKV_CHUNK)
    lblk = _pick_tile(seqlen, (512, 256, 128, 64, 32, 16, 8))
    nl_blocks = seqlen // lblk
    has_state = s0 is not None
    has_vres = v_first is not None
    if c * 4 <= RWKV_CHUNK:
        npairs = C_HEADS // 2
    w = npairs * LANES
    ngroups = d // w
    midw = mid.shape[1]
    *consts, bs = _rwkv_consts(c)
    strict, incl, eye, hones, tmasks = (jnp.asarray(x) for x in consts)
    row = lambda b, p, l: b * nl_blocks + l
    blk3 = lambda which: pl.BlockSpec((1, lblk, w), lambda b, p, l: (which, row(b, p, l), p))
    blk2 = pl.BlockSpec((lblk, w), lambda b, p, l: (row(b, p, l), p))
    par = pl.BlockSpec((1, w), lambda b, p, l: (0, p))
    cols = lambda arr: pl.BlockSpec((arr.shape[0], w), lambda b, p, l: (0, p))
    full = lambda arr: pl.BlockSpec(arr.shape, lambda b, p, l: (0,) * arr.ndim)
    w0, w2, a0, a2, g2 = low2[:5]
    in_specs = [blk3(0), blk3(1), blk3(2), pl.BlockSpec((lblk, midw), lambda b, p, l: (row(b, p, l), 0)),
                par, cols(w2), par, cols(a2), cols(g2)]
    args = [rkv, rkv, rkv, mid, w0.reshape(1, d), w2, a0.reshape(1, d), a2, g2]
    if has_vres:
        v0, v2 = low2[5:]
        in_specs += [blk3(2), par, cols(v2)]
        args += [v_first, v0.reshape(1, d), v2]
    in_specs += [par] * 5
    args += [x.reshape(1, d) for x in (kk_p, ka_p, rk_p, lnx_g, lnx_b)]
    in_specs += [full(x) for x in (strict, incl, eye, hones, tmasks)]
    args += [strict, incl, eye, hones, tmasks]
    st_blk = pl.BlockSpec((1, 2 * npairs, C_HEAD, C_HEAD), lambda b, p, l: (b, p, 0, 0))
    if has_state:
        in_specs.append(pl.BlockSpec((None, 1, 2 * npairs, C_HEAD, C_HEAD),
                                     lambda b, p, l: (layer, b, p, 0, 0)))
        args.append(s0)
    kern = functools.partial(_rwkv_core_kernel, chunk=c, nchunks=lblk // c, npairs=npairs,
                             has_state=has_state, has_vres=has_vres, bs=bs)
    return pl.pallas_call(
        kern,
        grid=(batch, ngroups, nl_blocks),
        in_specs=in_specs,
        out_specs=[blk2, st_blk],
        out_shape=[jax.ShapeDtypeStruct((m, d), BF16 if lblk % 16 == 0 else F32),
                   jax.ShapeDtypeStruct((batch, C_HEADS, C_HEAD, C_HEAD), F32)],
        scratch_shapes=[pltpu.VMEM((npairs, LANES, LANES), F32)]
        + [pltpu.VMEM((lblk, w), F32)] * (4 if has_vres else 3),
        compiler_params=_cparams(("parallel", "parallel", "arbitrary")),
        name="rwkv_core",
    )(*args)


def _weight_sources(p):
    src = {}
    for l in range(DEPTH):
        src[("up", l)] = [(p["w_up"], (l,))]
        src[("down", l)] = [(p["w_down"], (l,))]
    for e in range(N_EVEN):
        src[("in", e)] = [(p["w_in_even"], (e,))]
        src[("out", e)] = [(p["w_out_even"], (e,))]
    for o in range(N_ODD):
        src[("rkv", o)] = [(p["rw_wr"], (o,)), (p["rw_wk"], (o,)), (p["rw_wv"], (o,))]
        src[("wo", o)] = [(p["rw_wo"], (o,))]
    return src


def _bf16_weight(bank, src, key):
    if key not in bank:
        parts = [arr[lead].astype(BF16) for arr, lead in src[key]]
        bank[key] = parts[0] if len(parts) == 1 else jnp.stack(parts)
    return bank[key]


def _even_layer(x, batch, seqlen, e, layer, p, wget, st_hgrn, k_cache, v_cache):
    proj = norm_matmul(x, p["norm_mix_pre"][layer], wget(("in", e)))
    o_a, s_new = hgrn(proj, p["hgrn_lb_raw"], e, batch, seqlen, st_hgrn)
    k_lo = IN_A + B_WIDTH
    new_rows = min(seqlen, WINDOW)
    tails = jnp.stack([proj[(b + 1) * seqlen - new_rows:(b + 1) * seqlen, k_lo:] for b in range(batch)])
    kb = tails[:, :, :B_KV_WIDTH].reshape(batch, new_rows, B_KV_HEADS, B_HEAD_DIM)
    vb = tails[:, :, B_KV_WIDTH:].reshape(batch, new_rows, B_KV_HEADS, B_HEAD_DIM)
    o_b = swa(proj, batch, seqlen, p["rel_bias"], p["attn_sinks"][e], e, k_cache, v_cache)
    if k_cache is None:
        k_new, v_new = kb, vb
    else:
        k_new = jnp.concatenate([k_cache[e, :, new_rows:], kb], axis=1)
        v_new = jnp.concatenate([v_cache[e, :, new_rows:], vb], axis=1)
    x = even_out(o_a, proj, o_b, x, p["hgrn_norm_g"][e], wget(("out", e)), p["norm_mix_post"][layer])
    return x, s_new, k_new, v_new


def _odd_layer(x, batch, seqlen, o, layer, p, wget, shift0, s0, v_first):
    m, d = x.shape
    g_pre = p["norm_mix_pre"][layer]
    has_vres = o > 0
    width = lambda rng: rng[1] - rng[0]
    w1p, w2p = _pad_lora(p["rw_w1"][o], p["rw_w2"][o], width(LORA_W))
    a1p, a2p = _pad_lora(p["rw_a1"][o], p["rw_a2"][o], width(LORA_A))
    g1p, g2p = _pad_lora(p["rw_g1"][o], p["rw_g2"][o], width(LORA_G))
    first, low2 = [w1p, a1p, g1p], [p["rw_w0"][o], w2p, p["rw_a0"][o], a2p, g2p]
    if has_vres:
        v1p, v2p = _pad_lora(p["rw_v1"][o - 1], p["rw_v2"][o - 1], width(LORA_V))
        first.append(v1p)
        low2 += [p["rw_v0"][o - 1], v2p]
    rkv, mid = rwkv_in(x, g_pre, shift0, batch, seqlen, p["rw_mu"][o], wget(("rkv", o)),
                       jnp.concatenate(first, axis=1), has_vres)
    yg, s_new = rwkv_core(rkv, mid, low2, p["rw_kk"][o], p["rw_ka"][o], p["rw_rk"][o],
                          p["rw_lnx_g"][o], p["rw_lnx_b"][o], batch, seqlen, o, s0,
                          v_first if has_vres else None)
    shift_new = rmsnorm_rows(x.reshape(batch, seqlen, d)[:, -1], g_pre)
    x = odd_out(yg, x, wget(("wo", o)), p["norm_mix_post"][layer])
    return x, s_new, shift_new, rkv


def _trunk(x3, st_hgrn, k_cache, v_cache, st_rwkv, st_shift, p, bank):
    batch, seqlen, d = x3.shape
    x = x3.reshape(batch * seqlen, d)
    has_state = st_hgrn is not None
    hgrn_out, k_out, v_out, rwkv_out, shift_out = [], [], [], [], []
    v_first = None
    src = _weight_sources(p)
    wget = functools.partial(_bf16_weight, bank, src)
    nsteps = ffn_steps(batch * seqlen, D_FF)
    for layer in range(DEPTH):
        if layer % 2 == 0:
            e = layer // 2
            x, s_new, k_new, v_new = _even_layer(x, batch, seqlen, e, layer, p, wget, st_hgrn, k_cache, v_cache)
            hgrn_out.append(s_new)
            k_out.append(k_new)
            v_out.append(v_new)
        else:
            o = layer // 2
            x, s_new, sh_new, rkv = _odd_layer(
                x, batch, seqlen, o, layer, p, wget,
                st_shift[o] if has_state else None,
                st_rwkv,
                v_first)
            if o == 0:
                v_first = rkv
            rwkv_out.append(s_new)
            shift_out.append(sh_new)
        nl = layer + 1
        wanted = []
        if nl < DEPTH:
            wanted = [("up", nl), ("down", nl)]
            wanted += [("in", nl // 2), ("out", nl // 2)] if nl % 2 == 0 else [("rkv", nl // 2), ("wo", nl // 2)]
        jobs = [k for k in wanted
                if k not in bank and can_cast_in(nsteps, *src[k][0][0].shape[-2:])]
        x, cast = ffn(x, p["norm_ffn_pre"][layer], wget(("up", layer)), wget(("down", layer)),
                      p["norm_ffn_post"][layer], [src[k] for k in jobs])
        bank.update(zip(jobs, cast))
    return (x.reshape(batch, seqlen, d), jnp.stack(hgrn_out), jnp.stack(k_out), jnp.stack(v_out),
            jnp.stack(rwkv_out), jnp.stack(shift_out))


def kernel(x_prompt, x_sample, state_hgrn, cache_swa_k, cache_swa_v, state_rwkv, state_shift,
           norm_mix_pre, norm_mix_post, norm_ffn_pre, norm_ffn_post,
           w_in_even, w_out_even, hgrn_lb_raw, hgrn_norm_g, rel_bias, attn_sinks,
           rw_mu, rw_wr, rw_wk, rw_wv, rw_wo, rw_w0, rw_w1, rw_w2, rw_a0, rw_a1, rw_a2,
           rw_v0, rw_v1, rw_v2, rw_g1, rw_g2, rw_kk, rw_ka, rw_rk, rw_lnx_g, rw_lnx_b,
           w_up, w_down):
    p = {
        "norm_mix_pre": norm_mix_pre, "norm_mix_post": norm_mix_post,
        "norm_ffn_pre": norm_ffn_pre, "norm_ffn_post": norm_ffn_post,
        "w_in_even": w_in_even, "w_out_even": w_out_even,
        "hgrn_lb_raw": hgrn_lb_raw, "hgrn_norm_g": hgrn_norm_g,
        "rel_bias": rel_bias, "attn_sinks": attn_sinks,
        "rw_mu": rw_mu, "rw_wr": rw_wr, "rw_wk": rw_wk, "rw_wv": rw_wv, "rw_wo": rw_wo,
        "rw_w0": rw_w0, "rw_w1": rw_w1, "rw_w2": rw_w2, "rw_a0": rw_a0, "rw_a1": rw_a1, "rw_a2": rw_a2,
        "rw_v0": rw_v0, "rw_v1": rw_v1, "rw_v2": rw_v2, "rw_g1": rw_g1, "rw_g2": rw_g2,
        "rw_kk": rw_kk, "rw_ka": rw_ka, "rw_rk": rw_rk, "rw_lnx_g": rw_lnx_g, "rw_lnx_b": rw_lnx_b,
        "w_up": w_up, "w_down": w_down,
    }
    bank = {}
    y_p, hgrn_p, k_p, v_p, rwkv_p, shift_p = _trunk(x_prompt, None, None, None, None, None, p, bank)
    y_s, hgrn_s, k_s, v_s, rwkv_s, shift_s = _trunk(
        x_sample, state_hgrn, cache_swa_k, cache_swa_v, state_rwkv, state_shift, p, bank)
    return (y_p, y_s, hgrn_p, hgrn_s, k_p, k_s, v_p, v_s, rwkv_p, rwkv_s, shift_p, shift_s)
```

```python
import functools
import math

import numpy as np
import jax
import jax.numpy as jnp
from jax import lax
from jax.experimental import pallas as pl
from jax.experimental.pallas import tpu as pltpu

F32 = jnp.float32
BF16 = jnp.bfloat16

D_MODEL = 2048
DEPTH = 4
N_EVEN = 2
N_ODD = 2
A_HEADS = 8
A_KDIM = 128
A_VDIM = 128
A_WIDTH = 1024
A_QK = 1024
B_HEADS = 16
B_HEAD_DIM = 64
B_KV_HEADS = 4
B_GROUP = 4
B_WIDTH = 1024
B_KV_WIDTH = 256
WINDOW = 128
N_BUCKETS = 32
MAX_DISTANCE = 128
MASK_VALUE = -1e30
IN_A = 4096
IN_EVEN = 5632
C_HEAD = 64
C_HEADS = 32
GN_EPS = 64e-5
D_FF = 8192
NORM_EPS = 1e-6

LANES = 128
VMEM_LIMIT = 56 * 1024 * 1024

HGRN_CHUNK = 128
RWKV_CHUNK = 64
RWKV_TRI_BLOCK = 16


def _cparams(sem):
    return pltpu.CompilerParams(dimension_semantics=sem, vmem_limit_bytes=VMEM_LIMIT)


def _rms(x, g):
    return x * lax.rsqrt(jnp.mean(x * x, axis=-1, keepdims=True) + NORM_EPS) * g


def _dot(a, b):
    return jnp.dot(a.astype(BF16), b.astype(BF16), preferred_element_type=F32)


def _dot_nt(a, b):
    return lax.dot_general(a.astype(BF16), b.astype(BF16), (((1,), (1,)), ((), ())),
                           preferred_element_type=F32)


def _dot_tn(a, b):
    return lax.dot_general(a.astype(BF16), b.astype(BF16), (((0,), (0,)), ((), ())),
                           preferred_element_type=F32)


def _dot_shared(lhs_list, b):
    res = _dot(jnp.concatenate(lhs_list, axis=0), b)
    out, off = [], 0
    for a in lhs_list:
        out.append(res[off:off + a.shape[0]])
        off += a.shape[0]
    return out


def _pick_tile(m, cands):
    for c in cands:
        if m % c == 0:
            return c
    return m


def _norm_matmul_kernel(x_ref, g_ref, w_ref, o_ref, xn_ref):
    @pl.when(pl.program_id(1) == 0)
    def _():
        xn_ref[...] = _rms(x_ref[...], g_ref[...]).astype(BF16)

    o_ref[...] = jnp.dot(xn_ref[...], w_ref[...], preferred_element_type=F32)


def norm_matmul(x, g, w_bf16, tn=1408):
    m, d = x.shape
    n = w_bf16.shape[1]
    tm = _pick_tile(m, (1024, 512, 256, 128, 64, 32, 16, 8))
    return pl.pallas_call(
        _norm_matmul_kernel,
        grid=(m // tm, n // tn),
        in_specs=[pl.BlockSpec((tm, d), lambda i, j: (i, 0)),
                  pl.BlockSpec((1, d), lambda i, j: (0, 0)),
                  pl.BlockSpec((d, tn), lambda i, j: (0, j))],
        out_specs=pl.BlockSpec((tm, tn), lambda i, j: (i, j)),
        out_shape=jax.ShapeDtypeStruct((m, n), F32),
        scratch_shapes=[pltpu.VMEM((tm, d), BF16)],
        compiler_params=_cparams(("parallel", "arbitrary")),
        name="norm_matmul",
    )(x, g.reshape(1, d), w_bf16)


def _ffn_kernel(*refs, cast_srcs):
    nsrc = sum(cast_srcs)
    x_ref, gpre_ref, wup_ref, wdn_ref, gpost_ref = refs[:5]
    src_refs = refs[5:5 + nsrc]
    o_ref = refs[5 + nsrc]
    cast_out = refs[6 + nsrc:6 + nsrc + len(cast_srcs)]
    xn_ref, acc_ref = refs[-2:]
    k = 0
    for out_ref, n in zip(cast_out, cast_srcs):
        for j in range(n):
            if n == 1:
                out_ref[...] = src_refs[k][...].astype(BF16)
            else:
                out_ref[j] = src_refs[k][...].astype(BF16)
            k += 1
    f = pl.program_id(1)

    @pl.when(f == 0)
    def _():
        xn_ref[...] = _rms(x_ref[...], gpre_ref[...]).astype(BF16)
        acc_ref[...] = jnp.zeros_like(acc_ref)

    h = jnp.dot(xn_ref[...], wup_ref[...], preferred_element_type=F32)
    h = jnp.square(jnp.maximum(h, 0.0)).astype(BF16)
    acc_ref[...] += jnp.dot(h, wdn_ref[...], preferred_element_type=F32)

    @pl.when(f == pl.num_programs(1) - 1)
    def _():
        o_ref[...] = x_ref[...] + _rms(acc_ref[...], gpost_ref[...])


BF16_SUBLANES = 16


def ffn_steps(m, dff, tf=1024):
    return (m // _pick_tile(m, (512, 256, 128, 64, 32, 16, 8))) * (dff // tf)


CAST_BLOCK_BYTES = 1 << 20


def can_cast_in(nsteps, rows, cols):
    return (rows % nsteps == 0 and (rows // nsteps) % BF16_SUBLANES == 0
            and (rows // nsteps) * cols * 4 <= CAST_BLOCK_BYTES)


def ffn(x, gpre, wup_bf16, wdn_bf16, gpost, casts=(), tf=1024):
    m, d = x.shape
    dff = wup_bf16.shape[1]
    tm = _pick_tile(m, (512, 256, 128, 64, 32, 16, 8))
    nf = dff // tf
    nsteps = (m // tm) * nf
    step = lambda i, f: i * nf + f
    in_specs = [pl.BlockSpec((tm, d), lambda i, f: (i, 0)),
                pl.BlockSpec((1, d), lambda i, f: (0, 0)),
                pl.BlockSpec((d, tf), lambda i, f: (0, f)),
                pl.BlockSpec((tf, d), lambda i, f: (f, 0)),
                pl.BlockSpec((1, d), lambda i, f: (0, 0))]
    args = [x, gpre.reshape(1, d), wup_bf16, wdn_bf16, gpost.reshape(1, d)]
    out_specs = [pl.BlockSpec((tm, d), lambda i, f: (i, 0))]
    out_shape = [jax.ShapeDtypeStruct((m, d), F32)]
    for job in casts:
        rows, cols = job[0][0].shape[len(job[0][1]):]
        assert can_cast_in(nsteps, rows, cols)
        rb = rows // nsteps
        for arr, lead in job:
            in_specs.append(pl.BlockSpec((None,) * len(lead) + (rb, cols),
                                         lambda i, f, lead=lead: lead + (step(i, f), 0)))
            args.append(arr)
        if len(job) == 1:
            out_specs.append(pl.BlockSpec((rb, cols), lambda i, f: (step(i, f), 0)))
            out_shape.append(jax.ShapeDtypeStruct((rows, cols), BF16))
        else:
            out_specs.append(pl.BlockSpec((len(job), rb, cols), lambda i, f: (0, step(i, f), 0)))
            out_shape.append(jax.ShapeDtypeStruct((len(job), rows, cols), BF16))
    outs = pl.pallas_call(
        functools.partial(_ffn_kernel, cast_srcs=tuple(len(job) for job in casts)),
        grid=(m // tm, nf),
        in_specs=in_specs,
        out_specs=out_specs,
        out_shape=out_shape,
        scratch_shapes=[pltpu.VMEM((tm, d), BF16), pltpu.VMEM((tm, d), F32)],
        compiler_params=_cparams(("parallel", "arbitrary")),
        name="ffn",
    )(*args)
    return outs[0], list(outs[1:])


def _even_out_kernel(oa_ref, ga_ref, ob_ref, x_ref, ag_ref, w_ref, gpost_ref, o_ref):
    ga = ga_ref[...]
    oan = _rms(oa_ref[...], ag_ref[...]) * (ga * jax.nn.sigmoid(ga))
    mix = (jnp.dot(oan.astype(BF16), w_ref[:A_WIDTH, :], preferred_element_type=F32)
           + jnp.dot(ob_ref[...].astype(BF16), w_ref[A_WIDTH:, :], preferred_element_type=F32))
    o_ref[...] = x_ref[...] + _rms(mix, gpost_ref[...])


def even_out(o_a, proj, o_b, x, a_norm_g, w_out_bf16, gpost):
    m, d = x.shape
    tm = _pick_tile(m, (384, 256, 128, 64, 32, 16, 8))
    ga_blk = (3 * A_WIDTH) // A_WIDTH
    return pl.pallas_call(
        _even_out_kernel,
        grid=(m // tm,),
        in_specs=[pl.BlockSpec((tm, A_WIDTH), lambda i: (i, 0)),
                  pl.BlockSpec((tm, A_WIDTH), lambda i: (i, ga_blk)),
                  pl.BlockSpec((tm, B_WIDTH), lambda i: (i, 0)),
                  pl.BlockSpec((tm, d), lambda i: (i, 0)),
                  pl.BlockSpec((1, A_WIDTH), lambda i: (0, 0)),
                  pl.BlockSpec((A_WIDTH + B_WIDTH, d), lambda i: (0, 0)),
                  pl.BlockSpec((1, d), lambda i: (0, 0))],
        out_specs=pl.BlockSpec((tm, d), lambda i: (i, 0)),
        out_shape=jax.ShapeDtypeStruct((m, d), F32),
        compiler_params=_cparams(("parallel",)),
        name="even_out",
    )(o_a, proj, o_b, x, a_norm_g.reshape(1, A_WIDTH), w_out_bf16, gpost.reshape(1, d))


def _odd_out_kernel(y_ref, x_ref, w_ref, gpost_ref, o_ref):
    mix = jnp.dot(y_ref[...].astype(BF16), w_ref[...], preferred_element_type=F32)
    o_ref[...] = x_ref[...] + _rms(mix, gpost_ref[...])


def odd_out(yg, x, wo_bf16, gpost):
    m, d = x.shape
    tm = _pick_tile(m, (384, 256, 128, 64, 32, 16, 8))
    return pl.pallas_call(
        _odd_out_kernel,
        grid=(m // tm,),
        in_specs=[pl.BlockSpec((tm, d), lambda i: (i, 0)),
                  pl.BlockSpec((tm, d), lambda i: (i, 0)),
                  pl.BlockSpec((d, d), lambda i: (0, 0)),
                  pl.BlockSpec((1, d), lambda i: (0, 0))],
        out_specs=pl.BlockSpec((tm, d), lambda i: (i, 0)),
        out_shape=jax.ShapeDtypeStruct((m, d), F32),
        compiler_params=_cparams(("parallel",)),
        name="odd_out",
    )(yg, x, wo_bf16, gpost.reshape(1, d))


def _level_consts(c):
    levels = []
    s = c // 2
    while s >= 1:
        levels.append(s)
        s //= 2
    mask = np.zeros((len(levels), c, c), np.float32)
    idx = np.arange(c)
    for l, s in enumerate(levels):
        same = (idx[:, None] // (2 * s)) == (idx[None, :] // (2 * s))
        upper = (idx[:, None] % (2 * s)) >= s
        lower = (idx[None, :] % (2 * s)) < s
        mask[l] = (same & upper & lower).astype(np.float32)
    return levels, mask


def _split_rows(g, s, rowid):
    c = g.shape[0]
    if 2 * s >= 8:
        return jnp.concatenate(
            [jnp.broadcast_to(g[b + s - 1:b + s, :], (2 * s, g.shape[1])) for b in range(0, c, 2 * s)], axis=0)
    r = rowid % (2 * s)
    out = g
    for off in range(-(s - 1), s + 1):
        if off != 0:
            out = jnp.where(r == s - 1 + off, pltpu.roll(g, off % c, 0), out)
    return out


def _hgrn_kernel(*refs, layer, chunk, nchunks, levels, has_state, nheads):
    if has_state:
        q_ref, f_ref, i_ref, lb_ref, mask_ref, s0_ref, o_ref, s_ref, st_ref = refs
    else:
        q_ref, f_ref, i_ref, lb_ref, mask_ref, o_ref, s_ref, st_ref = refs
    c = chunk
    nh = nheads
    rowid = lax.broadcasted_iota(jnp.int32, (c, nh * LANES), 0)
    l_idx = pl.program_id(2)

    @pl.when(l_idx == 0)
    def _():
        for hi in range(nh):
            if has_state:
                st_ref[hi] = s0_ref[0, hi].T
            else:
                st_ref[hi] = jnp.zeros((A_VDIM, A_KDIM), F32)

    lbr = lb_ref[...]
    e = jnp.exp(lbr - jnp.max(lbr, axis=0, keepdims=True))
    p = e / jnp.sum(e, axis=0, keepdims=True)
    lb = jnp.zeros((1, nh * LANES), F32)
    for i in range(1, layer + 1):
        lb = lb + p[i:i + 1, :]
    one_m_lb = 1.0 - lb
    head = lambda x, hi: x[:, hi * LANES:(hi + 1) * LANES]

    def body(ci, carry):
        rows = pl.ds(pl.multiple_of(ci * c, c), c)
        fq = f_ref[rows, :]
        qr = q_ref[rows, :]
        v = i_ref[rows, :]
        f = lb + one_m_lb * jax.nn.sigmoid(fq)
        k = 1.0 - f
        q = qr * jax.nn.sigmoid(qr) * (A_KDIM ** -0.5)

        g = jnp.log2(f)
        sft = 1
        while sft < c:
            g = g + jnp.where(rowid >= sft, pltpu.roll(g, sft, 0), 0.0)
            sft *= 2
        glast = g[c - 1:c, :]
        q_in = q * jnp.exp2(g)
        kd = k * jnp.exp2(glast - g)
        dec = jnp.exp2(glast)
        diag = q * k

        sts = [st_ref[hi] for hi in range(nh)]
        os_ = [_dot_nt(head(q_in, hi), sts[hi]) for hi in range(nh)]
        attns = [jnp.zeros((c, c), F32) for _ in range(nh)]
        for l, s in enumerate(levels):
            e = jnp.exp2(-jnp.abs(g - _split_rows(g, s, rowid)))
            qs = q * e
            ks = k * e
            ml = mask_ref[l]
            attns = [at + ml * _dot_nt(head(qs, hi), head(ks, hi)) for hi, at in enumerate(attns)]
        for hi in range(nh):
            vh = head(v, hi)
            o = os_[hi] + _dot(attns[hi], vh) + jnp.sum(head(diag, hi), axis=-1, keepdims=True) * vh
            o_ref[rows, hi * LANES:(hi + 1) * LANES] = o
        for hi in range(nh):
            st_ref[hi] = sts[hi] * head(dec, hi) + _dot_tn(head(v, hi), head(kd, hi))
        return carry

    lax.fori_loop(0, nchunks, body, 0, unroll=2 if nchunks % 2 == 0 else 1)

    @pl.when(l_idx == pl.num_programs(2) - 1)
    def _():
        for hi in range(nh):
            s_ref[0, hi] = st_ref[hi].T


def hgrn(proj, lb_raw, layer, batch, seqlen, s0=None, nheads=4):
    m = proj.shape[0]
    c = math.gcd(seqlen, HGRN_CHUNK)
    lblk = _pick_tile(seqlen, (512, 256, 128, 64, 32, 16, 8))
    nl_blocks = seqlen // lblk
    levels, mask = _level_consts(c)
    has_state = s0 is not None
    if c * 4 <= HGRN_CHUNK:
        nheads = A_HEADS
    w = nheads * LANES
    ngroups = A_HEADS // nheads
    kern = functools.partial(_hgrn_kernel, layer=layer, chunk=c, nchunks=lblk // c,
                             levels=tuple(levels), has_state=has_state, nheads=nheads)
    row = lambda b, h, l: b * nl_blocks + l
    in_specs = [pl.BlockSpec((lblk, w), lambda b, h, l: (row(b, h, l), h)),
                pl.BlockSpec((lblk, w), lambda b, h, l: (row(b, h, l), ngroups + h)),
                pl.BlockSpec((lblk, w), lambda b, h, l: (row(b, h, l), 2 * ngroups + h)),
                pl.BlockSpec((N_EVEN, w), lambda b, h, l: (0, h)),
                pl.BlockSpec((len(levels), c, c), lambda b, h, l: (0, 0, 0))]
    args = [proj, proj, proj, lb_raw, jnp.asarray(mask)]
    st_blk = pl.BlockSpec((1, nheads, A_KDIM, A_VDIM), lambda b, h, l: (b, h, 0, 0))
    if has_state:
        in_specs.append(pl.BlockSpec((None, 1, nheads, A_KDIM, A_VDIM), lambda b, h, l: (layer, b, h, 0, 0)))
        args.append(s0)
    return pl.pallas_call(
        kern,
        grid=(batch, ngroups, nl_blocks),
        in_specs=in_specs,
        out_specs=[pl.BlockSpec((lblk, w), lambda b, h, l: (row(b, h, l), h)), st_blk],
        out_shape=[jax.ShapeDtypeStruct((m, A_WIDTH), F32),
                   jax.ShapeDtypeStruct((batch, A_HEADS, A_KDIM, A_VDIM), F32)],
        scratch_shapes=[pltpu.VMEM((nheads, A_VDIM, A_KDIM), F32)],
        compiler_params=_cparams(("parallel", "parallel", "arbitrary")),
        name="hgrn",
    )(*args)


def _t5_bucket(dist):
    max_exact = N_BUCKETS // 2
    d = np.maximum(dist, 0)
    large = max_exact + (np.log(np.maximum(d, max_exact).astype(np.float32) / max_exact)
                         / math.log(MAX_DISTANCE / max_exact) * (N_BUCKETS - max_exact)).astype(np.int32)
    large = np.minimum(large, N_BUCKETS - 1)
    return np.where(d < max_exact, d, large).astype(np.int32)


def _swa_kernel(q_ref, kp_ref, kc_ref, vp_ref, vc_ref, bucket_ref, band_ref, rb_ref, sink_ref,
                o_ref, bias_ref, *, qb, span, prev_always_valid):
    first = (pl.program_id(0) == 0) & (pl.program_id(1) == 0)

    @pl.when(first)
    def _():
        bk = bucket_ref[...]
        band = band_ref[...]

        def per_head(h, carry):
            def per_bucket(bi, acc):
                return jnp.where(bk == bi, rb_ref[bi, h], acc)
            acc = lax.fori_loop(0, N_BUCKETS, per_bucket, jnp.zeros((qb, span), F32))
            bias_ref[h] = jnp.where(band > 0, acc, MASK_VALUE)
            return carry

        lax.fori_loop(0, B_HEADS, per_head, 0)

    scale = B_HEAD_DIM ** -0.5
    q = q_ref[...]
    kall = jnp.concatenate([kp_ref[...], kc_ref[...]], axis=0)
    vall = jnp.concatenate([vp_ref[...], vc_ref[...]], axis=0)
    if not prev_always_valid:
        col = lax.broadcasted_iota(jnp.int32, (qb, span), 1)
        no_prev = (col < WINDOW) & (pl.program_id(1) == 0)
    heads = range(B_HEADS)
    ks = [kall[:, kh * B_HEAD_DIM:(kh + 1) * B_HEAD_DIM].astype(BF16) for kh in range(B_KV_HEADS)]
    vs = [vall[:, kh * B_HEAD_DIM:(kh + 1) * B_HEAD_DIM].astype(BF16) for kh in range(B_KV_HEADS)]
    qs = [(q[:, h * B_HEAD_DIM:(h + 1) * B_HEAD_DIM] * scale).astype(BF16) for h in heads]
    ss = [_dot_nt(qs[h], ks[h // B_GROUP]) + bias_ref[h] for h in heads]
    if not prev_always_valid:
        ss = [jnp.where(no_prev, MASK_VALUE, s) for s in ss]
    ms = [jnp.maximum(jnp.max(ss[h], axis=-1, keepdims=True), sink_ref[h]) for h in heads]
    ps = [jnp.exp(s - m) for s, m in zip(ss, ms)]
    denoms = [jnp.sum(ps[h], axis=-1, keepdims=True) + jnp.exp(sink_ref[h] - ms[h]) for h in heads]
    outs = [_dot(ps[h], vs[h // B_GROUP]) / denoms[h] for h in heads]
    o_ref[...] = jnp.concatenate(outs, axis=1).astype(o_ref.dtype)


def swa(proj, batch, seqlen, rel_bias, sinks, layer, k_past=None, v_past=None):
    m = proj.shape[0]
    has_cache = k_past is not None
    qb = math.gcd(seqlen, WINDOW)
    nb = seqlen // qb
    span = WINDOW + qb
    dist = np.arange(qb)[:, None] + WINDOW - np.arange(span)[None, :]
    band = ((dist >= 0) & (dist < WINDOW)).astype(np.float32)
    bucket = _t5_bucket(dist)
    q_col = IN_A // B_WIDTH
    k_col = (IN_A + B_WIDTH) // B_KV_WIDTH
    v_col = k_col + 1
    cur = lambda c: (lambda b, n: (b * nb + n, c))
    if has_cache:
        assert nb == 1
        prev_k = pl.BlockSpec((None, WINDOW, B_KV_WIDTH), lambda b, n: (layer, b, 0))
        prev_v = pl.BlockSpec((None, WINDOW, B_KV_WIDTH), lambda b, n: (layer, b, 0))
        kp_arr = k_past.reshape(k_past.shape[0], batch * WINDOW, B_KV_WIDTH)
        vp_arr = v_past.reshape(v_past.shape[0], batch * WINDOW, B_KV_WIDTH)
    else:
        assert qb == WINDOW
        prev = lambda c: (lambda b, n: (b * nb + jnp.maximum(n - 1, 0), c))
        prev_k = pl.BlockSpec((WINDOW, B_KV_WIDTH), prev(k_col))
        prev_v = pl.BlockSpec((WINDOW, B_KV_WIDTH), prev(v_col))
        kp_arr, vp_arr = proj, proj
    kern = functools.partial(_swa_kernel, qb=qb, span=span, prev_always_valid=has_cache)
    return pl.pallas_call(
        kern,
        grid=(batch, nb),
        in_specs=[pl.BlockSpec((qb, B_WIDTH), cur(q_col)),
                  prev_k,
                  pl.BlockSpec((qb, B_KV_WIDTH), cur(k_col)),
                  prev_v,
                  pl.BlockSpec((qb, B_KV_WIDTH), cur(v_col)),
                  pl.BlockSpec((qb, span), lambda b, n: (0, 0)),
                  pl.BlockSpec((qb, span), lambda b, n: (0, 0)),
                  pl.BlockSpec(memory_space=pltpu.SMEM),
                  pl.BlockSpec(memory_space=pltpu.SMEM)],
        out_specs=pl.BlockSpec((qb, B_WIDTH), lambda b, n: (b * nb + n, 0)),
        out_shape=jax.ShapeDtypeStruct((m, B_WIDTH), BF16 if qb % 16 == 0 else F32),
        scratch_shapes=[pltpu.VMEM((B_HEADS, qb, span), F32)],
        compiler_params=_cparams(("arbitrary", "arbitrary")),
        name="swa",
    )(proj, kp_arr, proj, vp_arr, proj, jnp.asarray(bucket), jnp.asarray(band), rel_bias, sinks)


def _rmsnorm_kernel(x_ref, g_ref, o_ref):
    o_ref[...] = _rms(x_ref[...], g_ref[...])


def rmsnorm_rows(x, g):
    m, d = x.shape
    tm = _pick_tile(m, (512, 256, 128, 64, 32, 16, 8))
    return pl.pallas_call(
        _rmsnorm_kernel,
        grid=(m // tm,),
        in_specs=[pl.BlockSpec((tm, d), lambda i: (i, 0)), pl.BlockSpec((1, d), lambda i: (0, 0))],
        out_specs=pl.BlockSpec((tm, d), lambda i: (i, 0)),
        out_shape=jax.ShapeDtypeStruct((m, d), F32),
        compiler_params=_cparams(("parallel",)),
        name="rmsnorm",
    )(x, g.reshape(1, d))


LORA_W = (0, 128)
LORA_A = (128, 256)
LORA_G = (256, 512)
LORA_V = (512, 640)


def _rwkv_in_kernel(x_ref, xp_ref, s_ref, g_ref, mu3_ref, mul_ref, w_ref, w1_ref,
                    rkv_ref, mid_ref, h_s, hp_s, *, tm, seqlen, has_vres):
    ph = pl.program_id(1)

    @pl.when(ph == 0)
    def _():
        g = g_ref[...]
        h = _rms(x_ref[...], g)
        rowid = lax.broadcasted_iota(jnp.int32, h.shape, 0)
        rolled = pltpu.roll(h, 1, 0)
        if seqlen % tm == 0:
            prev_last = _rms(xp_ref[...], g)[7:8, :]
            at_start = pl.program_id(0) % (seqlen // tm) == 0
            first = jnp.where(at_start, s_ref[...], prev_last)
            hp = jnp.where(rowid == 0, first, rolled)
        else:
            hp = jnp.where(rowid % seqlen == 0, s_ref[...], rolled)
        h_s[...] = h
        hp_s[...] = hp

    @pl.when(ph < 3)
    def _():
        h = h_s[...]
        xm = (h + (hp_s[...] - h) * mu3_ref[0]).astype(BF16)
        rkv_ref[0] = jnp.dot(xm, w_ref[...], preferred_element_type=F32)

    @pl.when(ph == 3)
    def _():
        h = h_s[...]
        xx = hp_s[...] - h
        mix = lambda i: (h + xx * mul_ref[i:i + 1, :]).astype(BF16)
        low = lambda i, rng: jnp.dot(mix(i), w1_ref[:, rng[0]:rng[1]], preferred_element_type=F32)
        parts = [jnp.tanh(low(0, LORA_W)), low(1, LORA_A), jax.nn.sigmoid(low(2, LORA_G))]
        if has_vres:
            parts.append(low(3, LORA_V))
        mid_ref[...] = jnp.concatenate(parts, axis=1).astype(BF16)


def rwkv_in(x, g, shift0, batch, seqlen, mu, w3_bf16, w1cat_bf16, has_vres):
    m, d = x.shape
    tm = next(t for t in (512, 256, 128, 64, 32, 16, 8)
              if m % t == 0 and (seqlen % t == 0 or t % seqlen == 0))
    midw = w1cat_bf16.shape[1]
    mu_rkv = jnp.stack([mu[0], mu[2], mu[3]])[:, None, :]
    mu_low = jnp.stack([mu[1], mu[4], mu[5], mu[3]])
    if shift0 is None:
        shift0 = jnp.zeros((batch, d), F32)
    if seqlen % tm == 0:
        srow = shift0[:, None, :]
        tps = seqlen // tm
        s_spec = pl.BlockSpec((None, 1, d), lambda i, p: (i // tps, 0, 0))
    else:
        srow = jnp.repeat(shift0, seqlen, axis=0)
        s_spec = pl.BlockSpec((tm, d), lambda i, p: (i, 0))
    sub = tm // 8
    kern = functools.partial(_rwkv_in_kernel, tm=tm, seqlen=seqlen, has_vres=has_vres)
    return pl.pallas_call(
        kern,
        grid=(m // tm, 4),
        in_specs=[pl.BlockSpec((tm, d), lambda i, p: (i, 0)),
                  pl.BlockSpec((8, d), lambda i, p: (jnp.maximum(i * sub - 1, 0), 0)),
                  s_spec,
                  pl.BlockSpec((1, d), lambda i, p: (0, 0)),
                  pl.BlockSpec((1, 1, d), lambda i, p: (jnp.minimum(p, 2), 0, 0)),
                  pl.BlockSpec((4, d), lambda i, p: (0, 0)),
                  pl.BlockSpec((None, d, d), lambda i, p: (jnp.minimum(p, 2), 0, 0)),
                  pl.BlockSpec((d, midw), lambda i, p: (0, 0))],
        out_specs=[pl.BlockSpec((1, tm, d), lambda i, p: (jnp.minimum(p, 2), i, 0)),
                   pl.BlockSpec((tm, midw), lambda i, p: (i, 0))],
        out_shape=[jax.ShapeDtypeStruct((3, m, d), F32),
                   jax.ShapeDtypeStruct((m, midw), BF16)],
        scratch_shapes=[pltpu.VMEM((tm, d), F32), pltpu.VMEM((tm, d), F32)],
        compiler_params=_cparams(("parallel", "arbitrary")),
        name="rwkv_in",
    )(x, x, srow, g.reshape(1, d), mu_rkv, mu_low, w3_bf16, w1cat_bf16)


def _pad_lora(w1, w2, width):
    r = w1.shape[1]
    return (jnp.pad(w1, ((0, 0), (0, width - r))).astype(BF16),
            jnp.pad(w2, ((0, width - r), (0, 0))).astype(BF16))


def _rwkv_consts(c):
    i = np.arange(2 * c)
    same = (i[:, None] // c) == (i[None, :] // c)
    strict = (same & ((i[:, None] % c) > (i[None, :] % c))).astype(np.float32)
    incl = (same & ((i[:, None] % c) >= (i[None, :] % c))).astype(np.float32)
    eye = np.eye(2 * c, dtype=np.float32)
    l = np.arange(LANES)
    headones = ((l[:, None] // C_HEAD) == (l[None, :] // C_HEAD)).astype(np.float32)
    bs = min(RWKV_TRI_BLOCK, c)
    blk = lambda n: (i[:, None] // n) == (i[None, :] // n)
    tmasks = [blk(bs)]
    s = bs
    while s < c:
        tmasks.append(blk(2 * s) & ~blk(s))
        s *= 2
    return strict, incl, eye, headones, np.stack(tmasks).astype(np.float32), bs


def _rwkv_core_kernel(*refs, chunk, nchunks, npairs, has_state, has_vres, bs):
    it = iter(refs)
    r_ref, k_ref, v_ref, mid_ref = (next(it) for _ in range(4))
    w0_ref, w2_ref, a0_ref, a2_ref, g2_ref = (next(it) for _ in range(5))
    if has_vres:
        vf_ref, v0_ref, v2_ref = next(it), next(it), next(it)
    kk_ref, ka_ref, rk_ref, lg_ref, lb_ref = (next(it) for _ in range(5))
    strict_ref, incl_ref, eye_ref, hones_ref, tmask_ref = (next(it) for _ in range(5))
    if has_state:
        s0_ref = next(it)
    y_ref, s_ref, st_ref, wl_ref, a_ref, g_ref = (next(it) for _ in range(6))
    if has_vres:
        vg_ref = next(it)
    c = chunk
    l_idx = pl.program_id(2)
    lane = lax.broadcasted_iota(jnp.int32, (1, LANES), 1)
    m0 = (lane < C_HEAD).astype(F32)
    m1 = 1.0 - m0

    low = lambda rng, w_ref: jnp.dot(mid_ref[:, rng[0]:rng[1]], w_ref[...], preferred_element_type=F32)
    wl_ref[...] = -math.exp(-0.5) * jax.nn.sigmoid(w0_ref[...] + low(LORA_W, w2_ref))
    a_ref[...] = jax.nn.sigmoid(a0_ref[...] + low(LORA_A, a2_ref))
    g_ref[...] = low(LORA_G, g2_ref)
    if has_vres:
        vg_ref[...] = jax.nn.sigmoid(v0_ref[...] + low(LORA_V, v2_ref))

    @pl.when(l_idx == 0)
    def _():
        for pi in range(npairs):
            if has_state:
                z = jnp.zeros((C_HEAD, C_HEAD), F32)
                top = jnp.concatenate([s0_ref[0, 2 * pi], z], axis=1)
                bot = jnp.concatenate([z, s0_ref[0, 2 * pi + 1]], axis=1)
                st_ref[pi] = jnp.concatenate([top, bot], axis=0)
            else:
                st_ref[pi] = jnp.zeros((LANES, LANES), F32)

    strict = strict_ref[...]
    incl = incl_ref[...]
    eye = eye_ref[...]

    def stack(x):
        return jnp.concatenate([x * m0, x * m1], axis=0)

    first_head = lax.broadcasted_iota(jnp.int32, (c, LANES), 1) < C_HEAD
    rowid = lax.broadcasted_iota(jnp.int32, (c, LANES), 0)

    def rowsums(xs):
        return [jnp.where(first_head,
                          jnp.sum(x * m0, axis=-1, keepdims=True),
                          jnp.sum(x * m1, axis=-1, keepdims=True)) for x in xs]

    def cumsum_rows(x):
        s = 1
        while s < c:
            x = x + jnp.where(rowid >= s, pltpu.roll(x, s, 0), 0.0)
            s *= 2
        return x

    def load(pi, rows):
        cols = slice(pi * LANES, (pi + 1) * LANES)
        k = k_ref[0, rows, cols]
        v = v_ref[0, rows, cols]
        a = a_ref[rows, cols]
        if has_vres:
            v = v + (vf_ref[0, rows, cols] - v) * vg_ref[rows, cols]
        return dict(cols=cols, r=r_ref[0, rows, cols], v=v, a=a, wl=wl_ref[rows, cols],
                    kr=k * kk_ref[:, cols], kh=k * (1.0 + (a - 1.0) * ka_ref[:, cols]))

    def decays(p, ss):
        kk = p["kr"] * lax.rsqrt(jnp.maximum(ss, 1e-24))
        b = kk * p["a"]
        gc = cumsum_rows(p["wl"])
        gl = gc[c - 1:c, :]
        e_neg = jnp.exp(-gc)
        e_out = jnp.exp(gl - gc)
        p.update(gl=gl, ab=-kk * jnp.exp(gc - p["wl"]), rb=p["r"] * jnp.exp(gc),
                 bt=b * e_neg, kt=p["kh"] * e_neg, bh=b * e_out, khat=p["kh"] * e_out)

    def intra(p):
        lhs = jnp.concatenate([stack(p["ab"]), stack(p["rb"])], axis=0)
        with_b = _dot_nt(lhs, jnp.concatenate([p["bt"], p["bt"]], axis=0))
        with_k = _dot_nt(lhs, jnp.concatenate([p["kt"], p["kt"]], axis=0))
        p.update(a_ab=with_b[:2 * c] * strict, a_rb=with_b[2 * c:] * incl,
                 a_ak=with_k[:2 * c] * strict, a_rk=with_k[2 * c:] * incl)

    def body(ci, carry):
        rows = pl.ds(pl.multiple_of(ci * c, c), c)
        ps = [load(pi, rows) for pi in range(npairs)]
        for p, ss in zip(ps, rowsums([p["kr"] * p["kr"] for p in ps])):
            decays(p, ss)
        for p in ps:
            intra(p)

        pws = [p["a_ab"] * tmask_ref[0] for p in ps]
        ts = [eye + pw for pw in pws]
        if bs > 2:
            pws = [_dot(pw, pw) for pw in pws]
            n = 2
            while 2 * n < bs:
                res = [_dot_shared([t, pw], pw) for t, pw in zip(ts, pws)]
                ts = [t + r[0] for t, r in zip(ts, res)]
                pws = [r[1] for r in res]
                n *= 2
            ts = [t + _dot(t, pw) for t, pw in zip(ts, pws)]
        s, lvl = bs, 1
        while s < c:
            ms = [_dot(t, p["a_ab"] * tmask_ref[lvl]) for t, p in zip(ts, ps)]
            ts = [t + _dot(m, t) for t, m in zip(ts, ms)]
            s, lvl = 2 * s, lvl + 1

        sts = [st_ref[pi] for pi in range(npairs)]
        fss = [_dot_nt(jnp.concatenate([p["ab"], p["rb"]], axis=0), st) for p, st in zip(ps, sts)]
        vss = [stack(p["v"]) for p in ps]
        rhss = [stack(fs[:c]) + _dot(p["a_ak"], vs) for p, fs, vs in zip(ps, fss, vss)]
        uss = [_dot(t, rhs) for t, rhs in zip(ts, rhss)]
        yss = [_dot(p["a_rb"], us) + _dot(p["a_rk"], vs) for p, us, vs in zip(ps, uss, vss)]
        for pi, (p, st, us) in enumerate(zip(ps, sts, uss)):
            u = us[:c] + us[c:]
            upd = _dot_tn(jnp.concatenate([u, p["v"]], axis=0),
                          jnp.concatenate([p["bh"], p["khat"]], axis=0))
            st_ref[pi] = st * jnp.exp(p["gl"]) + upd * hones_ref[...]
        inv_n = 1.0 / C_HEAD
        ys_ = [fs[c:] + ys[:c] + ys[c:] for fs, ys in zip(fss, yss)]
        sums = rowsums(ys_ + [p["r"] * p["kh"] * rk_ref[:, p["cols"]] for p in ps])
        dlts = [y - m * inv_n for y, m in zip(ys_, sums[:npairs])]
        vars_ = rowsums([d * d for d in dlts])
        for p, dlt, var, bsum in zip(ps, dlts, vars_, sums[npairs:]):
            cols = p["cols"]
            yn = dlt * lax.rsqrt(var * inv_n + GN_EPS) * lg_ref[:, cols] + lb_ref[:, cols]
            y_ref[rows, cols] = ((yn + bsum * p["v"]) * g_ref[rows, cols]).astype(y_ref.dtype)
        return carry

    lax.fori_loop(0, nchunks, body, 0)

    @pl.when(l_idx == pl.num_programs(2) - 1)
    def _():
        for pi in range(npairs):
            st = st_ref[pi]
            s_ref[0, 2 * pi] = st[:C_HEAD, :C_HEAD]
            s_ref[0, 2 * pi + 1] = st[C_HEAD:, C_HEAD:]


def rwkv_core(rkv, mid, low2, kk_p, ka_p, rk_p, lnx_g, lnx_b, batch, seqlen, layer,
              s0=None, v_first=None):
    _, m, d = rkv.shape
    c = math.gcd(seqlen, RWKV_CHUNK)
    npairs = C_HEADS // 2
    lblk = _pick_tile(seqlen, (256, 128, 64, 32, 16, 8))
    nl_blocks = seqlen // lblk
    has_state = s0 is not None
    has_vres = v_first is not None
    w = npairs * LANES
    ngroups = d // w
    midw = mid.shape[1]
    *consts, bs = _rwkv_consts(c)
    strict, incl, eye, hones, tmasks = (jnp.asarray(x) for x in consts)
    row = lambda b, p, l: b * nl_blocks + l
    blk3 = lambda which: pl.BlockSpec((1, lblk, w), lambda b, p, l: (which, row(b, p, l), p))
    blk2 = pl.BlockSpec((lblk, w), lambda b, p, l: (row(b, p, l), p))
    par = pl.BlockSpec((1, w), lambda b, p, l: (0, p))
    cols = lambda arr: pl.BlockSpec((arr.shape[0], w), lambda b, p, l: (0, p))
    full = lambda arr: pl.BlockSpec(arr.shape, lambda b, p, l: (0,) * arr.ndim)
    w0, w2, a0, a2, g2 = low2[:5]
    in_specs = [blk3(0), blk3(1), blk3(2), pl.BlockSpec((lblk, midw), lambda b, p, l: (row(b, p, l), 0)),
                par, cols(w2), par, cols(a2), cols(g2)]
    args = [rkv, rkv, rkv, mid, w0.reshape(1, d), w2, a0.reshape(1, d), a2, g2]
    if has_vres:
        v0, v2 = low2[5:]
        in_specs += [blk3(2), par, cols(v2)]
        args += [v_first, v0.reshape(1, d), v2]
    in_specs += [par] * 5
    args += [x.reshape(1, d) for x in (kk_p, ka_p, rk_p, lnx_g, lnx_b)]
    in_specs += [full(x) for x in (strict, incl, eye, hones, tmasks)]
    args += [strict, incl, eye, hones, tmasks]
    st_blk = pl.BlockSpec((1, 2 * npairs, C_HEAD, C_HEAD), lambda b, p, l: (b, p, 0, 0))
    if has_state:
        in_specs.append(pl.BlockSpec((None, 1, 2 * npairs, C_HEAD, C_HEAD),
                                     lambda b, p, l: (layer, b, p, 0, 0)))
        args.append(s0)
    kern = functools.partial(_rwkv_core_kernel, chunk=c, nchunks=lblk // c, npairs=npairs,
                             has_state=has_state, has_vres=has_vres, bs=bs)
    return pl.pallas_call(
        kern,
        grid=(batch, ngroups, nl_blocks),
        in_specs=in_specs,
        out_specs=[blk2, st_blk],
        out_shape=[jax.ShapeDtypeStruct((m, d), BF16 if lblk % 16 == 0 else F32),
                   jax.ShapeDtypeStruct((batch, C_HEADS, C_HEAD, C_HEAD), F32)],
        scratch_shapes=[pltpu.VMEM((npairs, LANES, LANES), F32)]
        + [pltpu.VMEM((lblk, w), F32)] * (4 if has_vres else 3),
        compiler_params=_cparams(("parallel", "parallel", "arbitrary")),
        name="rwkv_core",
    )(*args)


def _weight_sources(p):
    src = {}
    for l in range(DEPTH):
        src[("up", l)] = [(p["w_up"], (l,))]
        src[("down", l)] = [(p["w_down"], (l,))]
    for e in range(N_EVEN):
        src[("in", e)] = [(p["w_in_even"], (e,))]
        src[("out", e)] = [(p["w_out_even"], (e,))]
    for o in range(N_ODD):
        src[("rkv", o)] = [(p["rw_wr"], (o,)), (p["rw_wk"], (o,)), (p["rw_wv"], (o,))]
        src[("wo", o)] = [(p["rw_wo"], (o,))]
    return src


def _bf16_weight(bank, src, key):
    if key not in bank:
        parts = [arr[lead].astype(BF16) for arr, lead in src[key]]
        bank[key] = parts[0] if len(parts) == 1 else jnp.stack(parts)
    return bank[key]


def _even_layer(x, batch, seqlen, e, layer, p, wget, st_hgrn, k_cache, v_cache):
    proj = norm_matmul(x, p["norm_mix_pre"][layer], wget(("in", e)))
    o_a, s_new = hgrn(proj, p["hgrn_lb_raw"], e, batch, seqlen, st_hgrn)
    k_lo = IN_A + B_WIDTH
    new_rows = min(seqlen, WINDOW)
    tails = jnp.stack([proj[(b + 1) * seqlen - new_rows:(b + 1) * seqlen, k_lo:] for b in range(batch)])
    kb = tails[:, :, :B_KV_WIDTH].reshape(batch, new_rows, B_KV_HEADS, B_HEAD_DIM)
    vb = tails[:, :, B_KV_WIDTH:].reshape(batch, new_rows, B_KV_HEADS, B_HEAD_DIM)
    o_b = swa(proj, batch, seqlen, p["rel_bias"], p["attn_sinks"][e], e, k_cache, v_cache)
    if k_cache is None:
        k_new, v_new = kb, vb
    else:
        k_new = jnp.concatenate([k_cache[e, :, new_rows:], kb], axis=1)
        v_new = jnp.concatenate([v_cache[e, :, new_rows:], vb], axis=1)
    x = even_out(o_a, proj, o_b, x, p["hgrn_norm_g"][e], wget(("out", e)), p["norm_mix_post"][layer])
    return x, s_new, k_new, v_new


def _odd_layer(x, batch, seqlen, o, layer, p, wget, shift0, s0, v_first):
    m, d = x.shape
    g_pre = p["norm_mix_pre"][layer]
    has_vres = o > 0
    width = lambda rng: rng[1] - rng[0]
    w1p, w2p = _pad_lora(p["rw_w1"][o], p["rw_w2"][o], width(LORA_W))
    a1p, a2p = _pad_lora(p["rw_a1"][o], p["rw_a2"][o], width(LORA_A))
    g1p, g2p = _pad_lora(p["rw_g1"][o], p["rw_g2"][o], width(LORA_G))
    first, low2 = [w1p, a1p, g1p], [p["rw_w0"][o], w2p, p["rw_a0"][o], a2p, g2p]
    if has_vres:
        v1p, v2p = _pad_lora(p["rw_v1"][o - 1], p["rw_v2"][o - 1], width(LORA_V))
        first.append(v1p)
        low2 += [p["rw_v0"][o - 1], v2p]
    rkv, mid = rwkv_in(x, g_pre, shift0, batch, seqlen, p["rw_mu"][o], wget(("rkv", o)),
                       jnp.concatenate(first, axis=1), has_vres)
    yg, s_new = rwkv_core(rkv, mid, low2, p["rw_kk"][o], p["rw_ka"][o], p["rw_rk"][o],
                          p["rw_lnx_g"][o], p["rw_lnx_b"][o], batch, seqlen, o, s0,
                          v_first if has_vres else None)
    shift_new = rmsnorm_rows(x.reshape(batch, seqlen, d)[:, -1], g_pre)
    x = odd_out(yg, x, wget(("wo", o)), p["norm_mix_post"][layer])
    return x, s_new, shift_new, rkv


def _trunk(x3, st_hgrn, k_cache, v_cache, st_rwkv, st_shift, p, bank):
    batch, seqlen, d = x3.shape
    x = x3.reshape(batch * seqlen, d)
    has_state = st_hgrn is not None
    hgrn_out, k_out, v_out, rwkv_out, shift_out = [], [], [], [], []
    v_first = None
    src = _weight_sources(p)
    wget = functools.partial(_bf16_weight, bank, src)
    nsteps = ffn_steps(batch * seqlen, D_FF)
    for layer in range(DEPTH):
        if layer % 2 == 0:
            e = layer // 2
            x, s_new, k_new, v_new = _even_layer(x, batch, seqlen, e, layer, p, wget, st_hgrn, k_cache, v_cache)
            hgrn_out.append(s_new)
            k_out.append(k_new)
            v_out.append(v_new)
        else:
            o = layer // 2
            x, s_new, sh_new, rkv = _odd_layer(
                x, batch, seqlen, o, layer, p, wget,
                st_shift[o] if has_state else None,
                st_rwkv,
                v_first)
            if o == 0:
                v_first = rkv
            rwkv_out.append(s_new)
            shift_out.append(sh_new)
        nl = layer + 1
        wanted = []
        if nl < DEPTH:
            wanted = [("up", nl), ("down", nl)]
            wanted += [("in", nl // 2), ("out", nl // 2)] if nl % 2 == 0 else [("rkv", nl // 2), ("wo", nl // 2)]
        jobs = [k for k in wanted
                if k not in bank and can_cast_in(nsteps, *src[k][0][0].shape[-2:])]
        x, cast = ffn(x, p["norm_ffn_pre"][layer], wget(("up", layer)), wget(("down", layer)),
                      p["norm_ffn_post"][layer], [src[k] for k in jobs])
        bank.update(zip(jobs, cast))
    return (x.reshape(batch, seqlen, d), jnp.stack(hgrn_out), jnp.stack(k_out), jnp.stack(v_out),
            jnp.stack(rwkv_out), jnp.stack(shift_out))


def kernel(x_prompt, x_sample, state_hgrn, cache_swa_k, cache_swa_v, state_rwkv, state_shift,
           norm_mix_pre, norm_mix_post, norm_ffn_pre, norm_ffn_post,
           w_in_even, w_out_even, hgrn_lb_raw, hgrn_norm_g, rel_bias, attn_sinks,
           rw_mu, rw_wr, rw_wk, rw_wv, rw_wo, rw_w0, rw_w1, rw_w2, rw_a0, rw_a1, rw_a2,
           rw_v0, rw_v1, rw_v2, rw_g1, rw_g2, rw_kk, rw_ka, rw_rk, rw_lnx_g, rw_lnx_b,
           w_up, w_down):
    p = {
        "norm_mix_pre": norm_mix_pre, "norm_mix_post": norm_mix_post,
        "norm_ffn_pre": norm_ffn_pre, "norm_ffn_post": norm_ffn_post,
        "w_in_even": w_in_even, "w_out_even": w_out_even,
        "hgrn_lb_raw": hgrn_lb_raw, "hgrn_norm_g": hgrn_norm_g,
        "rel_bias": rel_bias, "attn_sinks": attn_sinks,
        "rw_mu": rw_mu, "rw_wr": rw_wr, "rw_wk": rw_wk, "rw_wv": rw_wv, "rw_wo": rw_wo,
        "rw_w0": rw_w0, "rw_w1": rw_w1, "rw_w2": rw_w2, "rw_a0": rw_a0, "rw_a1": rw_a1, "rw_a2": rw_a2,
        "rw_v0": rw_v0, "rw_v1": rw_v1, "rw_v2": rw_v2, "rw_g1": rw_g1, "rw_g2": rw_g2,
        "rw_kk": rw_kk, "rw_ka": rw_ka, "rw_rk": rw_rk, "rw_lnx_g": rw_lnx_g, "rw_lnx_b": rw_lnx_b,
        "w_up": w_up, "w_down": w_down,
    }
    bank = {}
    y_p, hgrn_p, k_p, v_p, rwkv_p, shift_p = _trunk(x_prompt, None, None, None, None, None, p, bank)
    y_s, hgrn_s, k_s, v_s, rwkv_s, shift_s = _trunk(
        x_sample, state_hgrn, cache_swa_k, cache_swa_v, state_rwkv, state_shift, p, bank)
    return (y_p, y_s, hgrn_p, hgrn_s, k_p, k_s, v_p, v_s, rwkv_p, rwkv_s, shift_p, shift_s)
```

```python
import functools
import math

import numpy as np
import jax
import jax.numpy as jnp
from jax import lax
from jax.experimental import pallas as pl
from jax.experimental.pallas import tpu as pltpu

F32 = jnp.float32
BF16 = jnp.bfloat16

D_MODEL = 2048
DEPTH = 4
N_EVEN = 2
N_ODD = 2
A_HEADS = 8
A_KDIM = 128
A_VDIM = 128
A_WIDTH = 1024
A_QK = 1024
B_HEADS = 16
B_HEAD_DIM = 64
B_KV_HEADS = 4
B_GROUP = 4
B_WIDTH = 1024
B_KV_WIDTH = 256
WINDOW = 128
N_BUCKETS = 32
MAX_DISTANCE = 128
MASK_VALUE = -1e30
IN_A = 4096
IN_EVEN = 5632
C_HEAD = 64
C_HEADS = 32
GN_EPS = 64e-5
D_FF = 8192
NORM_EPS = 1e-6

LANES = 128
VMEM_LIMIT = 56 * 1024 * 1024

HGRN_CHUNK = 128
RWKV_CHUNK = 64
RWKV_TRI_BLOCK = 16


def _cparams(sem):
    return pltpu.CompilerParams(dimension_semantics=sem, vmem_limit_bytes=VMEM_LIMIT)


def _rms(x, g):
    return x * lax.rsqrt(jnp.mean(x * x, axis=-1, keepdims=True) + NORM_EPS) * g


def _dot(a, b):
    return jnp.dot(a.astype(BF16), b.astype(BF16), preferred_element_type=F32)


def _dot_nt(a, b):
    return lax.dot_general(a.astype(BF16), b.astype(BF16), (((1,), (1,)), ((), ())),
                           preferred_element_type=F32)


def _dot_tn(a, b):
    return lax.dot_general(a.astype(BF16), b.astype(BF16), (((0,), (0,)), ((), ())),
                           preferred_element_type=F32)


def _dot_shared(lhs_list, b):
    res = _dot(jnp.concatenate(lhs_list, axis=0), b)
    out, off = [], 0
    for a in lhs_list:
        out.append(res[off:off + a.shape[0]])
        off += a.shape[0]
    return out


def _pick_tile(m, cands):
    for c in cands:
        if m % c == 0:
            return c
    return m


def _norm_matmul_kernel(x_ref, g_ref, w_ref, o_ref, xn_ref):
    @pl.when(pl.program_id(1) == 0)
    def _():
        xn_ref[...] = _rms(x_ref[...], g_ref[...]).astype(BF16)

    o_ref[...] = jnp.dot(xn_ref[...], w_ref[...], preferred_element_type=F32)


def norm_matmul(x, g, w_bf16, tn=1408):
    m, d = x.shape
    n = w_bf16.shape[1]
    tm = _pick_tile(m, (1024, 512, 256, 128, 64, 32, 16, 8))
    return pl.pallas_call(
        _norm_matmul_kernel,
        grid=(m // tm, n // tn),
        in_specs=[pl.BlockSpec((tm, d), lambda i, j: (i, 0)),
                  pl.BlockSpec((1, d), lambda i, j: (0, 0)),
                  pl.BlockSpec((d, tn), lambda i, j: (0, j))],
        out_specs=pl.BlockSpec((tm, tn), lambda i, j: (i, j)),
        out_shape=jax.ShapeDtypeStruct((m, n), F32),
        scratch_shapes=[pltpu.VMEM((tm, d), BF16)],
        compiler_params=_cparams(("parallel", "arbitrary")),
        name="norm_matmul",
    )(x, g.reshape(1, d), w_bf16)


def _ffn_kernel(*refs, cast_srcs):
    nsrc = sum(cast_srcs)
    x_ref, gpre_ref, wup_ref, wdn_ref, gpost_ref = refs[:5]
    src_refs = refs[5:5 + nsrc]
    o_ref = refs[5 + nsrc]
    cast_out = refs[6 + nsrc:6 + nsrc + len(cast_srcs)]
    xn_ref, acc_ref = refs[-2:]
    k = 0
    for out_ref, n in zip(cast_out, cast_srcs):
        for j in range(n):
            if n == 1:
                out_ref[...] = src_refs[k][...].astype(BF16)
            else:
                out_ref[j] = src_refs[k][...].astype(BF16)
            k += 1
    f = pl.program_id(1)

    @pl.when(f == 0)
    def _():
        xn_ref[...] = _rms(x_ref[...], gpre_ref[...]).astype(BF16)
        acc_ref[...] = jnp.zeros_like(acc_ref)

    h = jnp.dot(xn_ref[...], wup_ref[...], preferred_element_type=F32)
    h = jnp.square(jnp.maximum(h, 0.0)).astype(BF16)
    acc_ref[...] += jnp.dot(h, wdn_ref[...], preferred_element_type=F32)

    @pl.when(f == pl.num_programs(1) - 1)
    def _():
        o_ref[...] = x_ref[...] + _rms(acc_ref[...], gpost_ref[...])


BF16_SUBLANES = 16


def ffn_steps(m, dff, tf=1024):
    return (m // _pick_tile(m, (512, 256, 128, 64, 32, 16, 8))) * (dff // tf)


CAST_BLOCK_BYTES = 1 << 20


def can_cast_in(nsteps, rows, cols):
    return (rows % nsteps == 0 and (rows // nsteps) % BF16_SUBLANES == 0
            and (rows // nsteps) * cols * 4 <= CAST_BLOCK_BYTES)


def ffn(x, gpre, wup_bf16, wdn_bf16, gpost, casts=(), tf=1024):
    m, d = x.shape
    dff = wup_bf16.shape[1]
    tm = _pick_tile(m, (512, 256, 128, 64, 32, 16, 8))
    nf = dff // tf
    nsteps = (m // tm) * nf
    step = lambda i, f: i * nf + f
    in_specs = [pl.BlockSpec((tm, d), lambda i, f: (i, 0)),
                pl.BlockSpec((1, d), lambda i, f: (0, 0)),
                pl.BlockSpec((d, tf), lambda i, f: (0, f)),
                pl.BlockSpec((tf, d), lambda i, f: (f, 0)),
                pl.BlockSpec((1, d), lambda i, f: (0, 0))]
    args = [x, gpre.reshape(1, d), wup_bf16, wdn_bf16, gpost.reshape(1, d)]
    out_specs = [pl.BlockSpec((tm, d), lambda i, f: (i, 0))]
    out_shape = [jax.ShapeDtypeStruct((m, d), F32)]
    for job in casts:
        rows, cols = job[0][0].shape[len(job[0][1]):]
        assert can_cast_in(nsteps, rows, cols)
        rb = rows // nsteps
        for arr, lead in job:
            in_specs.append(pl.BlockSpec((None,) * len(lead) + (rb, cols),
                                         lambda i, f, lead=lead: lead + (step(i, f), 0)))
            args.append(arr)
        if len(job) == 1:
            out_specs.append(pl.BlockSpec((rb, cols), lambda i, f: (step(i, f), 0)))
            out_shape.append(jax.ShapeDtypeStruct((rows, cols), BF16))
        else:
            out_specs.append(pl.BlockSpec((len(job), rb, cols), lambda i, f: (0, step(i, f), 0)))
            out_shape.append(jax.ShapeDtypeStruct((len(job), rows, cols), BF16))
    outs = pl.pallas_call(
        functools.partial(_ffn_kernel, cast_srcs=tuple(len(job) for job in casts)),
        grid=(m // tm, nf),
        in_specs=in_specs,
        out_specs=out_specs,
        out_shape=out_shape,
        scratch_shapes=[pltpu.VMEM((tm, d), BF16), pltpu.VMEM((tm, d), F32)],
        compiler_params=_cparams(("parallel", "arbitrary")),
        name="ffn",
    )(*args)
    return outs[0], list(outs[1:])


def _even_out_kernel(oa_ref, ga_ref, ob_ref, x_ref, ag_ref, w_ref, gpost_ref, o_ref):
    ga = ga_ref[...]
    oan = _rms(oa_ref[...], ag_ref[...]) * (ga * jax.nn.sigmoid(ga))
    mix = (jnp.dot(oan.astype(BF16), w_ref[:A_WIDTH, :], preferred_element_type=F32)
           + jnp.dot(ob_ref[...].astype(BF16), w_ref[A_WIDTH:, :], preferred_element_type=F32))
    o_ref[...] = x_ref[...] + _rms(mix, gpost_ref[...])


def even_out(o_a, proj, o_b, x, a_norm_g, w_out_bf16, gpost):
    m, d = x.shape
    tm = _pick_tile(m, (384, 256, 128, 64, 32, 16, 8))
    ga_blk = (3 * A_WIDTH) // A_WIDTH
    return pl.pallas_call(
        _even_out_kernel,
        grid=(m // tm,),
        in_specs=[pl.BlockSpec((tm, A_WIDTH), lambda i: (i, 0)),
                  pl.BlockSpec((tm, A_WIDTH), lambda i: (i, ga_blk)),
                  pl.BlockSpec((tm, B_WIDTH), lambda i: (i, 0)),
                  pl.BlockSpec((tm, d), lambda i: (i, 0)),
                  pl.BlockSpec((1, A_WIDTH), lambda i: (0, 0)),
                  pl.BlockSpec((A_WIDTH + B_WIDTH, d), lambda i: (0, 0)),
                  pl.BlockSpec((1, d), lambda i: (0, 0))],
        out_specs=pl.BlockSpec((tm, d), lambda i: (i, 0)),
        out_shape=jax.ShapeDtypeStruct((m, d), F32),
        compiler_params=_cparams(("parallel",)),
        name="even_out",
    )(o_a, proj, o_b, x, a_norm_g.reshape(1, A_WIDTH), w_out_bf16, gpost.reshape(1, d))


def _odd_out_kernel(y_ref, x_ref, w_ref, gpost_ref, o_ref):
    mix = jnp.dot(y_ref[...].astype(BF16), w_ref[...], preferred_element_type=F32)
    o_ref[...] = x_ref[...] + _rms(mix, gpost_ref[...])


def odd_out(yg, x, wo_bf16, gpost):
    m, d = x.shape
    tm = _pick_tile(m, (384, 256, 128, 64, 32, 16, 8))
    return pl.pallas_call(
        _odd_out_kernel,
        grid=(m // tm,),
        in_specs=[pl.BlockSpec((tm, d), lambda i: (i, 0)),
                  pl.BlockSpec((tm, d), lambda i: (i, 0)),
                  pl.BlockSpec((d, d), lambda i: (0, 0)),
                  pl.BlockSpec((1, d), lambda i: (0, 0))],
        out_specs=pl.BlockSpec((tm, d), lambda i: (i, 0)),
        out_shape=jax.ShapeDtypeStruct((m, d), F32),
        compiler_params=_cparams(("parallel",)),
        name="odd_out",
    )(yg, x, wo_bf16, gpost.reshape(1, d))


def _level_consts(c):
    levels = []
    s = c // 2
    while s >= 1:
        levels.append(s)
        s //= 2
    mask = np.zeros((len(levels), c, c), np.float32)
    idx = np.arange(c)
    for l, s in enumerate(levels):
        same = (idx[:, None] // (2 * s)) == (idx[None, :] // (2 * s))
        upper = (idx[:, None] % (2 * s)) >= s
        lower = (idx[None, :] % (2 * s)) < s
        mask[l] = (same & upper & lower).astype(np.float32)
    return levels, mask


def _split_rows(g, s, rowid):
    c = g.shape[0]
    if 2 * s >= 8:
        return jnp.concatenate(
            [jnp.broadcast_to(g[b + s - 1:b + s, :], (2 * s, g.shape[1])) for b in range(0, c, 2 * s)], axis=0)
    r = rowid % (2 * s)
    out = g
    for off in range(-(s - 1), s + 1):
        if off != 0:
            out = jnp.where(r == s - 1 + off, pltpu.roll(g, off % c, 0), out)
    return out


def _hgrn_kernel(*refs, layer, chunk, nchunks, levels, has_state, nheads):
    if has_state:
        q_ref, f_ref, i_ref, lb_ref, mask_ref, s0_ref, o_ref, s_ref, st_ref = refs
    else:
        q_ref, f_ref, i_ref, lb_ref, mask_ref, o_ref, s_ref, st_ref = refs
    c = chunk
    nh = nheads
    rowid = lax.broadcasted_iota(jnp.int32, (c, nh * LANES), 0)
    l_idx = pl.program_id(2)

    @pl.when(l_idx == 0)
    def _():
        for hi in range(nh):
            if has_state:
                st_ref[hi] = s0_ref[0, hi].T
            else:
                st_ref[hi] = jnp.zeros((A_VDIM, A_KDIM), F32)

    lbr = lb_ref[...]
    e = jnp.exp(lbr - jnp.max(lbr, axis=0, keepdims=True))
    p = e / jnp.sum(e, axis=0, keepdims=True)
    lb = jnp.zeros((1, nh * LANES), F32)
    for i in range(1, layer + 1):
        lb = lb + p[i:i + 1, :]
    one_m_lb = 1.0 - lb
    head = lambda x, hi: x[:, hi * LANES:(hi + 1) * LANES]

    def body(ci, carry):
        rows = pl.ds(pl.multiple_of(ci * c, c), c)
        fq = f_ref[rows, :]
        qr = q_ref[rows, :]
        v = i_ref[rows, :]
        f = lb + one_m_lb * jax.nn.sigmoid(fq)
        k = 1.0 - f
        q = qr * jax.nn.sigmoid(qr) * (A_KDIM ** -0.5)

        g = jnp.log2(f)
        sft = 1
        while sft < c:
            g = g + jnp.where(rowid >= sft, pltpu.roll(g, sft, 0), 0.0)
            sft *= 2
        glast = g[c - 1:c, :]
        q_in = q * jnp.exp2(g)
        kd = k * jnp.exp2(glast - g)
        dec = jnp.exp2(glast)
        diag = q * k

        sts = [st_ref[hi] for hi in range(nh)]
        os_ = [_dot_nt(head(q_in, hi), sts[hi]) for hi in range(nh)]
        attns = [jnp.zeros((c, c), F32) for _ in range(nh)]
        for l, s in enumerate(levels):
            e = jnp.exp2(-jnp.abs(g - _split_rows(g, s, rowid)))
            qs = q * e
            ks = k * e
            ml = mask_ref[l]
            attns = [at + ml * _dot_nt(head(qs, hi), head(ks, hi)) for hi, at in enumerate(attns)]
        for hi in range(nh):
            vh = head(v, hi)
            o = os_[hi] + _dot(attns[hi], vh) + jnp.sum(head(diag, hi), axis=-1, keepdims=True) * vh
            o_ref[rows, hi * LANES:(hi + 1) * LANES] = o
        for hi in range(nh):
            st_ref[hi] = sts[hi] * head(dec, hi) + _dot_tn(head(v, hi), head(kd, hi))
        return carry

    lax.fori_loop(0, nchunks, body, 0, unroll=2 if nchunks % 2 == 0 else 1)

    @pl.when(l_idx == pl.num_programs(2) - 1)
    def _():
        for hi in range(nh):
            s_ref[0, hi] = st_ref[hi].T


def hgrn(proj, lb_raw, layer, batch, seqlen, s0=None, nheads=4):
    m = proj.shape[0]
    c = math.gcd(seqlen, HGRN_CHUNK)
    lblk = _pick_tile(seqlen, (512, 256, 128, 64, 32, 16, 8))
    nl_blocks = seqlen // lblk
    levels, mask = _level_consts(c)
    has_state = s0 is not None
    if c * 4 <= HGRN_CHUNK:
        nheads = A_HEADS
    w = nheads * LANES
    ngroups = A_HEADS // nheads
    kern = functools.partial(_hgrn_kernel, layer=layer, chunk=c, nchunks=lblk // c,
                             levels=tuple(levels), has_state=has_state, nheads=nheads)
    row = lambda b, h, l: b * nl_blocks + l
    in_specs = [pl.BlockSpec((lblk, w), lambda b, h, l: (row(b, h, l), h)),
                pl.BlockSpec((lblk, w), lambda b, h, l: (row(b, h, l), ngroups + h)),
                pl.BlockSpec((lblk, w), lambda b, h, l: (row(b, h, l), 2 * ngroups + h)),
                pl.BlockSpec((N_EVEN, w), lambda b, h, l: (0, h)),
                pl.BlockSpec((len(levels), c, c), lambda b, h, l: (0, 0, 0))]
    args = [proj, proj, proj, lb_raw, jnp.asarray(mask)]
    st_blk = pl.BlockSpec((1, nheads, A_KDIM, A_VDIM), lambda b, h, l: (b, h, 0, 0))
    if has_state:
        in_specs.append(pl.BlockSpec((None, 1, nheads, A_KDIM, A_VDIM), lambda b, h, l: (layer, b, h, 0, 0)))
        args.append(s0)
    return pl.pallas_call(
        kern,
        grid=(batch, ngroups, nl_blocks),
        in_specs=in_specs,
        out_specs=[pl.BlockSpec((lblk, w), lambda b, h, l: (row(b, h, l), h)), st_blk],
        out_shape=[jax.ShapeDtypeStruct((m, A_WIDTH), F32),
                   jax.ShapeDtypeStruct((batch, A_HEADS, A_KDIM, A_VDIM), F32)],
        scratch_shapes=[pltpu.VMEM((nheads, A_VDIM, A_KDIM), F32)],
        compiler_params=_cparams(("parallel", "parallel", "arbitrary")),
        name="hgrn",
    )(*args)


def _t5_bucket(dist):
    max_exact = N_BUCKETS // 2
    d = np.maximum(dist, 0)
    large = max_exact + (np.log(np.maximum(d, max_exact).astype(np.float32) / max_exact)
                         / math.log(MAX_DISTANCE / max_exact) * (N_BUCKETS - max_exact)).astype(np.int32)
    large = np.minimum(large, N_BUCKETS - 1)
    return np.where(d < max_exact, d, large).astype(np.int32)


def _swa_kernel(q_ref, kp_ref, kc_ref, vp_ref, vc_ref, bucket_ref, band_ref, rb_ref, sink_ref,
                o_ref, bias_ref, *, qb, span, prev_always_valid):
    first = (pl.program_id(0) == 0) & (pl.program_id(1) == 0)

    @pl.when(first)
    def _():
        bk = bucket_ref[...]
        band = band_ref[...]

        def per_head(h, carry):
            def per_bucket(bi, acc):
                return jnp.where(bk == bi, rb_ref[bi, h], acc)
            acc = lax.fori_loop(0, N_BUCKETS, per_bucket, jnp.zeros((qb, span), F32))
            bias_ref[h] = jnp.where(band > 0, acc, MASK_VALUE)
            return carry

        lax.fori_loop(0, B_HEADS, per_head, 0)

    scale = B_HEAD_DIM ** -0.5
    q = q_ref[...]
    kall = jnp.concatenate([kp_ref[...], kc_ref[...]], axis=0)
    vall = jnp.concatenate([vp_ref[...], vc_ref[...]], axis=0)
    if not prev_always_valid:
        col = lax.broadcasted_iota(jnp.int32, (qb, span), 1)
        no_prev = (col < WINDOW) & (pl.program_id(1) == 0)
    heads = range(B_HEADS)
    ks = [kall[:, kh * B_HEAD_DIM:(kh + 1) * B_HEAD_DIM].astype(BF16) for kh in range(B_KV_HEADS)]
    vs = [vall[:, kh * B_HEAD_DIM:(kh + 1) * B_HEAD_DIM].astype(BF16) for kh in range(B_KV_HEADS)]
    qs = [(q[:, h * B_HEAD_DIM:(h + 1) * B_HEAD_DIM] * scale).astype(BF16) for h in heads]
    ss = [_dot_nt(qs[h], ks[h // B_GROUP]) + bias_ref[h] for h in heads]
    if not prev_always_valid:
        ss = [jnp.where(no_prev, MASK_VALUE, s) for s in ss]
    ms = [jnp.maximum(jnp.max(ss[h], axis=-1, keepdims=True), sink_ref[h]) for h in heads]
    ps = [jnp.exp(s - m) for s, m in zip(ss, ms)]
    denoms = [jnp.sum(ps[h], axis=-1, keepdims=True) + jnp.exp(sink_ref[h] - ms[h]) for h in heads]
    outs = [_dot(ps[h], vs[h // B_GROUP]) / denoms[h] for h in heads]
    o_ref[...] = jnp.concatenate(outs, axis=1).astype(o_ref.dtype)


def swa(proj, batch, seqlen, rel_bias, sinks, layer, k_past=None, v_past=None):
    m = proj.shape[0]
    has_cache = k_past is not None
    qb = math.gcd(seqlen, WINDOW)
    nb = seqlen // qb
    span = WINDOW + qb
    dist = np.arange(qb)[:, None] + WINDOW - np.arange(span)[None, :]
    band = ((dist >= 0) & (dist < WINDOW)).astype(np.float32)
    bucket = _t5_bucket(dist)
    q_col = IN_A // B_WIDTH
    k_col = (IN_A + B_WIDTH) // B_KV_WIDTH
    v_col = k_col + 1
    cur = lambda c: (lambda b, n: (b * nb + n, c))
    if has_cache:
        assert nb == 1
        prev_k = pl.BlockSpec((None, WINDOW, B_KV_WIDTH), lambda b, n: (layer, b, 0))
        prev_v = pl.BlockSpec((None, WINDOW, B_KV_WIDTH), lambda b, n: (layer, b, 0))
        kp_arr = k_past.reshape(k_past.shape[0], batch * WINDOW, B_KV_WIDTH)
        vp_arr = v_past.reshape(v_past.shape[0], batch * WINDOW, B_KV_WIDTH)
    else:
        assert qb == WINDOW
        prev = lambda c: (lambda b, n: (b * nb + jnp.maximum(n - 1, 0), c))
        prev_k = pl.BlockSpec((WINDOW, B_KV_WIDTH), prev(k_col))
        prev_v = pl.BlockSpec((WINDOW, B_KV_WIDTH), prev(v_col))
        kp_arr, vp_arr = proj, proj
    kern = functools.partial(_swa_kernel, qb=qb, span=span, prev_always_valid=has_cache)
    return pl.pallas_call(
        kern,
        grid=(batch, nb),
        in_specs=[pl.BlockSpec((qb, B_WIDTH), cur(q_col)),
                  prev_k,
                  pl.BlockSpec((qb, B_KV_WIDTH), cur(k_col)),
                  prev_v,
                  pl.BlockSpec((qb, B_KV_WIDTH), cur(v_col)),
                  pl.BlockSpec((qb, span), lambda b, n: (0, 0)),
                  pl.BlockSpec((qb, span), lambda b, n: (0, 0)),
                  pl.BlockSpec(memory_space=pltpu.SMEM),
                  pl.BlockSpec(memory_space=pltpu.SMEM)],
        out_specs=pl.BlockSpec((qb, B_WIDTH), lambda b, n: (b * nb + n, 0)),
        out_shape=jax.ShapeDtypeStruct((m, B_WIDTH), BF16 if qb % 16 == 0 else F32),
        scratch_shapes=[pltpu.VMEM((B_HEADS, qb, span), F32)],
        compiler_params=_cparams(("arbitrary", "arbitrary")),
        name="swa",
    )(proj, kp_arr, proj, vp_arr, proj, jnp.asarray(bucket), jnp.asarray(band), rel_bias, sinks)


def _rmsnorm_kernel(x_ref, g_ref, o_ref):
    o_ref[...] = _rms(x_ref[...], g_ref[...])


def rmsnorm_rows(x, g):
    m, d = x.shape
    tm = _pick_tile(m, (512, 256, 128, 64, 32, 16, 8))
    return pl.pallas_call(
        _rmsnorm_kernel,
        grid=(m // tm,),
        in_specs=[pl.BlockSpec((tm, d), lambda i: (i, 0)), pl.BlockSpec((1, d), lambda i: (0, 0))],
        out_specs=pl.BlockSpec((tm, d), lambda i: (i, 0)),
        out_shape=jax.ShapeDtypeStruct((m, d), F32),
        compiler_params=_cparams(("parallel",)),
        name="rmsnorm",
    )(x, g.reshape(1, d))


LORA_W = (0, 128)
LORA_A = (128, 256)
LORA_G = (256, 512)
LORA_V = (512, 640)


def _rwkv_in_kernel(x_ref, xp_ref, s_ref, g_ref, mu3_ref, mul_ref, w_ref, w1_ref,
                    rkv_ref, mid_ref, h_s, hp_s, *, tm, seqlen, has_vres):
    ph = pl.program_id(1)

    @pl.when(ph == 0)
    def _():
        g = g_ref[...]
        h = _rms(x_ref[...], g)
        rowid = lax.broadcasted_iota(jnp.int32, h.shape, 0)
        rolled = pltpu.roll(h, 1, 0)
        if seqlen % tm == 0:
            prev_last = _rms(xp_ref[...], g)[7:8, :]
            at_start = pl.program_id(0) % (seqlen // tm) == 0
            first = jnp.where(at_start, s_ref[...], prev_last)
            hp = jnp.where(rowid == 0, first, rolled)
        else:
            hp = jnp.where(rowid % seqlen == 0, s_ref[...], rolled)
        h_s[...] = h
        hp_s[...] = hp

    @pl.when(ph < 3)
    def _():
        h = h_s[...]
        xm = (h + (hp_s[...] - h) * mu3_ref[0]).astype(BF16)
        rkv_ref[0] = jnp.dot(xm, w_ref[...], preferred_element_type=F32)

    @pl.when(ph == 3)
    def _():
        h = h_s[...]
        xx = hp_s[...] - h
        mix = lambda i: (h + xx * mul_ref[i:i + 1, :]).astype(BF16)
        low = lambda i, rng: jnp.dot(mix(i), w1_ref[:, rng[0]:rng[1]], preferred_element_type=F32)
        parts = [jnp.tanh(low(0, LORA_W)), low(1, LORA_A), jax.nn.sigmoid(low(2, LORA_G))]
        if has_vres:
            parts.append(low(3, LORA_V))
        mid_ref[...] = jnp.concatenate(parts, axis=1).astype(BF16)


def rwkv_in(x, g, shift0, batch, seqlen, mu, w3_bf16, w1cat_bf16, has_vres):
    m, d = x.shape
    tm = next(t for t in (512, 256, 128, 64, 32, 16, 8)
              if m % t == 0 and (seqlen % t == 0 or t % seqlen == 0))
    midw = w1cat_bf16.shape[1]
    mu_rkv = jnp.stack([mu[0], mu[2], mu[3]])[:, None, :]
    mu_low = jnp.stack([mu[1], mu[4], mu[5], mu[3]])
    if shift0 is None:
        shift0 = jnp.zeros((batch, d), F32)
    if seqlen % tm == 0:
        srow = shift0[:, None, :]
        tps = seqlen // tm
        s_spec = pl.BlockSpec((None, 1, d), lambda i, p: (i // tps, 0, 0))
    else:
        srow = jnp.repeat(shift0, seqlen, axis=0)
        s_spec = pl.BlockSpec((tm, d), lambda i, p: (i, 0))
    sub = tm // 8
    kern = functools.partial(_rwkv_in_kernel, tm=tm, seqlen=seqlen, has_vres=has_vres)
    return pl.pallas_call(
        kern,
        grid=(m // tm, 4),
        in_specs=[pl.BlockSpec((tm, d), lambda i, p: (i, 0)),
                  pl.BlockSpec((8, d), lambda i, p: (jnp.maximum(i * sub - 1, 0), 0)),
                  s_spec,
                  pl.BlockSpec((1, d), lambda i, p: (0, 0)),
                  pl.BlockSpec((1, 1, d), lambda i, p: (jnp.minimum(p, 2), 0, 0)),
                  pl.BlockSpec((4, d), lambda i, p: (0, 0)),
                  pl.BlockSpec((None, d, d), lambda i, p: (jnp.minimum(p, 2), 0, 0)),
                  pl.BlockSpec((d, midw), lambda i, p: (0, 0))],
        out_specs=[pl.BlockSpec((1, tm, d), lambda i, p: (jnp.minimum(p, 2), i, 0)),
                   pl.BlockSpec((tm, midw), lambda i, p: (i, 0))],
        out_shape=[jax.ShapeDtypeStruct((3, m, d), F32),
                   jax.ShapeDtypeStruct((m, midw), BF16)],
        scratch_shapes=[pltpu.VMEM((tm, d), F32), pltpu.VMEM((tm, d), F32)],
        compiler_params=_cparams(("parallel", "arbitrary")),
        name="rwkv_in",
    )(x, x, srow, g.reshape(1, d), mu_rkv, mu_low, w3_bf16, w1cat_bf16)


def _pad_lora(w1, w2, width):
    r = w1.shape[1]
    return (jnp.pad(w1, ((0, 0), (0, width - r))).astype(BF16),
            jnp.pad(w2, ((0, width - r), (0, 0))).astype(BF16))


def _rwkv_consts(c):
    i = np.arange(2 * c)
    same = (i[:, None] // c) == (i[None, :] // c)
    strict = (same & ((i[:, None] % c) > (i[None, :] % c))).astype(np.float32)
    incl = (same & ((i[:, None] % c) >= (i[None, :] % c))).astype(np.float32)
    eye = np.eye(2 * c, dtype=np.float32)
    l = np.arange(LANES)
    headones = ((l[:, None] // C_HEAD) == (l[None, :] // C_HEAD)).astype(np.float32)
    bs = min(RWKV_TRI_BLOCK, c)
    blk = lambda n: (i[:, None] // n) == (i[None, :] // n)
    tmasks = [blk(bs)]
    s = bs
    while s < c:
        tmasks.append(blk(2 * s) & ~blk(s))
        s *= 2
    return strict, incl, eye, headones, np.stack(tmasks).astype(np.float32), bs


def _rwkv_core_kernel(*refs, chunk, nchunks, npairs, has_state, has_vres, bs):
    it = iter(refs)
    r_ref, k_ref, v_ref, mid_ref = (next(it) for _ in range(4))
    w0_ref, w2_ref, a0_ref, a2_ref, g2_ref = (next(it) for _ in range(5))
    if has_vres:
        vf_ref, v0_ref, v2_ref = next(it), next(it), next(it)
    kk_ref, ka_ref, rk_ref, lg_ref, lb_ref = (next(it) for _ in range(5))
    strict_ref, incl_ref, eye_ref, hones_ref, tmask_ref = (next(it) for _ in range(5))
    if has_state:
        s0_ref = next(it)
    y_ref, s_ref, st_ref, wl_ref, a_ref, g_ref = (next(it) for _ in range(6))
    if has_vres:
        vg_ref = next(it)
    c = chunk
    l_idx = pl.program_id(2)
    lane = lax.broadcasted_iota(jnp.int32, (1, LANES), 1)
    m0 = (lane < C_HEAD).astype(F32)
    m1 = 1.0 - m0

    low = lambda rng, w_ref: jnp.dot(mid_ref[:, rng[0]:rng[1]], w_ref[...], preferred_element_type=F32)
    sig = lambda z: 0.5 * jnp.tanh(0.5 * z) + 0.5
    wl_ref[...] = -math.exp(-0.5) * sig(w0_ref[...] + low(LORA_W, w2_ref))
    a_ref[...] = sig(a0_ref[...] + low(LORA_A, a2_ref))
    g_ref[...] = low(LORA_G, g2_ref)
    if has_vres:
        vg_ref[...] = sig(v0_ref[...] + low(LORA_V, v2_ref))

    @pl.when(l_idx == 0)
    def _():
        for pi in range(npairs):
            if has_state:
                z = jnp.zeros((C_HEAD, C_HEAD), F32)
                top = jnp.concatenate([s0_ref[0, 2 * pi], z], axis=1)
                bot = jnp.concatenate([z, s0_ref[0, 2 * pi + 1]], axis=1)
                st_ref[pi] = jnp.concatenate([top, bot], axis=0)
            else:
                st_ref[pi] = jnp.zeros((LANES, LANES), F32)

    strict = strict_ref[...]
    incl = incl_ref[...]
    eye = eye_ref[...]

    def stack(x):
        return jnp.concatenate([x * m0, x * m1], axis=0)

    first_head = lax.broadcasted_iota(jnp.int32, (c, LANES), 1) < C_HEAD
    rowid = lax.broadcasted_iota(jnp.int32, (c, LANES), 0)

    def rowsums(xs):
        return [jnp.where(first_head,
                          jnp.sum(x * m0, axis=-1, keepdims=True),
                          jnp.sum(x * m1, axis=-1, keepdims=True)) for x in xs]

    def cumsum_rows(x):
        s = 1
        while s < c:
            x = x + jnp.where(rowid >= s, pltpu.roll(x, s, 0), 0.0)
            s *= 2
        return x

    def load(pi, rows):
        cols = slice(pi * LANES, (pi + 1) * LANES)
        k = k_ref[0, rows, cols]
        v = v_ref[0, rows, cols]
        a = a_ref[rows, cols]
        if has_vres:
            v = v + (vf_ref[0, rows, cols] - v) * vg_ref[rows, cols]
        return dict(cols=cols, r=r_ref[0, rows, cols], v=v, a=a, wl=wl_ref[rows, cols],
                    kr=k * kk_ref[:, cols], kh=k * (1.0 + (a - 1.0) * ka_ref[:, cols]))

    def decays(p, ss):
        kk = p["kr"] * lax.rsqrt(jnp.maximum(ss, 1e-24))
        b = kk * p["a"]
        gc = cumsum_rows(p["wl"])
        gl = gc[c - 1:c, :]
        e_neg = jnp.exp(-gc)
        e_out = jnp.exp(gl - gc)
        p.update(gl=gl, ab=-kk * jnp.exp(gc - p["wl"]), rb=p["r"] * jnp.exp(gc),
                 bt=b * e_neg, kt=p["kh"] * e_neg, bh=b * e_out, khat=p["kh"] * e_out)

    def intra(p):
        lhs = jnp.concatenate([stack(p["ab"]), stack(p["rb"])], axis=0)
        with_b = _dot_nt(lhs, jnp.concatenate([p["bt"], p["bt"]], axis=0))
        with_k = _dot_nt(lhs, jnp.concatenate([p["kt"], p["kt"]], axis=0))
        p.update(a_ab=with_b[:2 * c] * strict, a_rb=with_b[2 * c:] * incl,
                 a_ak=with_k[:2 * c] * strict, a_rk=with_k[2 * c:] * incl)

    def body(ci, carry):
        rows = pl.ds(pl.multiple_of(ci * c, c), c)
        ps = [load(pi, rows) for pi in range(npairs)]
        for p, ss in zip(ps, rowsums([p["kr"] * p["kr"] for p in ps])):
            decays(p, ss)
        for p in ps:
            intra(p)

        pws = [p["a_ab"] * tmask_ref[0] for p in ps]
        ts = [eye + pw for pw in pws]
        if bs > 2:
            pws = [_dot(pw, pw) for pw in pws]
            n = 2
            while 2 * n < bs:
                res = [_dot_shared([t, pw], pw) for t, pw in zip(ts, pws)]
                ts = [t + r[0] for t, r in zip(ts, res)]
                pws = [r[1] for r in res]
                n *= 2
            ts = [t + _dot(t, pw) for t, pw in zip(ts, pws)]
        s, lvl = bs, 1
        while s < c:
            ms = [_dot(t, p["a_ab"] * tmask_ref[lvl]) for t, p in zip(ts, ps)]
            ts = [t + _dot(m, t) for t, m in zip(ts, ms)]
            s, lvl = 2 * s, lvl + 1

        sts = [st_ref[pi] for pi in range(npairs)]
        fss = [_dot_nt(jnp.concatenate([p["ab"], p["rb"]], axis=0), st) for p, st in zip(ps, sts)]
        vss = [stack(p["v"]) for p in ps]
        rhss = [stack(fs[:c]) + _dot(p["a_ak"], vs) for p, fs, vs in zip(ps, fss, vss)]
        uss = [_dot(t, rhs) for t, rhs in zip(ts, rhss)]
        yss = [_dot(p["a_rb"], us) + _dot(p["a_rk"], vs) for p, us, vs in zip(ps, uss, vss)]
        for pi, (p, st, us) in enumerate(zip(ps, sts, uss)):
            u = us[:c] + us[c:]
            upd = _dot_tn(jnp.concatenate([u, p["v"]], axis=0),
                          jnp.concatenate([p["bh"], p["khat"]], axis=0))
            st_ref[pi] = st * jnp.exp(p["gl"]) + upd * hones_ref[...]
        inv_n = 1.0 / C_HEAD
        ys_ = [fs[c:] + ys[:c] + ys[c:] for fs, ys in zip(fss, yss)]
        sums = rowsums(ys_ + [p["r"] * p["kh"] * rk_ref[:, p["cols"]] for p in ps])
        dlts = [y - m * inv_n for y, m in zip(ys_, sums[:npairs])]
        vars_ = rowsums([d * d for d in dlts])
        for p, dlt, var, bsum in zip(ps, dlts, vars_, sums[npairs:]):
            cols = p["cols"]
            yn = dlt * lax.rsqrt(var * inv_n + GN_EPS) * lg_ref[:, cols] + lb_ref[:, cols]
            y_ref[rows, cols] = ((yn + bsum * p["v"]) * g_ref[rows, cols]).astype(y_ref.dtype)
        return carry

    lax.fori_loop(0, nchunks, body, 0)

    @pl.when(l_idx == pl.num_programs(2) - 1)
    def _():
        for pi in range(npairs):
            st = st_ref[pi]
            s_ref[0, 2 * pi] = st[:C_HEAD, :C_HEAD]
            s_ref[0, 2 * pi + 1] = st[C_HEAD:, C_HEAD:]


def rwkv_core(rkv, mid, low2, kk_p, ka_p, rk_p, lnx_g, lnx_b, batch, seqlen, layer,
              s0=None, v_first=None):
    _, m, d = rkv.shape
    c = math.gcd(seqlen, RWKV_CHUNK)
    npairs = C_HEADS // 2
    lblk = _pick_tile(seqlen, (256, 128, 64, 32, 16, 8))
    nl_blocks = seqlen // lblk
    has_state = s0 is not None
    has_vres = v_first is not None
    w = npairs * LANES
    ngroups = d // w
    midw = mid.shape[1]
    *consts, bs = _rwkv_consts(c)
    strict, incl, eye, hones, tmasks = (jnp.asarray(x) for x in consts)
    row = lambda b, p, l: b * nl_blocks + l
    blk3 = lambda which: pl.BlockSpec((1, lblk, w), lambda b, p, l: (which, row(b, p, l), p))
    blk2 = pl.BlockSpec((lblk, w), lambda b, p, l: (row(b, p, l), p))
    par = pl.BlockSpec((1, w), lambda b, p, l: (0, p))
    cols = lambda arr: pl.BlockSpec((arr.shape[0], w), lambda b, p, l: (0, p))
    full = lambda arr: pl.BlockSpec(arr.shape, lambda b, p, l: (0,) * arr.ndim)
    w0, w2, a0, a2, g2 = low2[:5]
    in_specs = [blk3(0), blk3(1), blk3(2), pl.BlockSpec((lblk, midw), lambda b, p, l: (row(b, p, l), 0)),
                par, cols(w2), par, cols(a2), cols(g2)]
    args = [rkv, rkv, rkv, mid, w0.reshape(1, d), w2, a0.reshape(1, d), a2, g2]
    if has_vres:
        v0, v2 = low2[5:]
        in_specs += [blk3(2), par, cols(v2)]
        args += [v_first, v0.reshape(1, d), v2]
    in_specs += [par] * 5
    args += [x.reshape(1, d) for x in (kk_p, ka_p, rk_p, lnx_g, lnx_b)]
    in_specs += [full(x) for x in (strict, incl, eye, hones, tmasks)]
    args += [strict, incl, eye, hones, tmasks]
    st_blk = pl.BlockSpec((1, 2 * npairs, C_HEAD, C_HEAD), lambda b, p, l: (b, p, 0, 0))
    if has_state:
        in_specs.append(pl.BlockSpec((None, 1, 2 * npairs, C_HEAD, C_HEAD),
                                     lambda b, p, l: (layer, b, p, 0, 0)))
        args.append(s0)
    kern = functools.partial(_rwkv_core_kernel, chunk=c, nchunks=lblk // c, npairs=npairs,
                             has_state=has_state, has_vres=has_vres, bs=bs)
    return pl.pallas_call(
        kern,
        grid=(batch, ngroups, nl_blocks),
        in_specs=in_specs,
        out_specs=[blk2, st_blk],
        out_shape=[jax.ShapeDtypeStruct((m, d), BF16 if lblk % 16 == 0 else F32),
                   jax.ShapeDtypeStruct((batch, C_HEADS, C_HEAD, C_HEAD), F32)],
        scratch_shapes=[pltpu.VMEM((npairs, LANES, LANES), F32)]
        + [pltpu.VMEM((lblk, w), F32)] * (4 if has_vres else 3),
        compiler_params=_cparams(("parallel", "parallel", "arbitrary")),
        name="rwkv_core",
    )(*args)


def _weight_sources(p):
    src = {}
    for l in range(DEPTH):
        src[("up", l)] = [(p["w_up"], (l,))]
        src[("down", l)] = [(p["w_down"], (l,))]
    for e in range(N_EVEN):
        src[("in", e)] = [(p["w_in_even"], (e,))]
        src[("out", e)] = [(p["w_out_even"], (e,))]
    for o in range(N_ODD):
        src[("rkv", o)] = [(p["rw_wr"], (o,)), (p["rw_wk"], (o,)), (p["rw_wv"], (o,))]
        src[("wo", o)] = [(p["rw_wo"], (o,))]
    return src


def _bf16_weight(bank, src, key):
    if key not in bank:
        parts = [arr[lead].astype(BF16) for arr, lead in src[key]]
        bank[key] = parts[0] if len(parts) == 1 else jnp.stack(parts)
    return bank[key]


def _even_layer(x, batch, seqlen, e, layer, p, wget, st_hgrn, k_cache, v_cache):
    proj = norm_matmul(x, p["norm_mix_pre"][layer], wget(("in", e)))
    o_a, s_new = hgrn(proj, p["hgrn_lb_raw"], e, batch, seqlen, st_hgrn)
    k_lo = IN_A + B_WIDTH
    new_rows = min(seqlen, WINDOW)
    tails = jnp.stack([proj[(b + 1) * seqlen - new_rows:(b + 1) * seqlen, k_lo:] for b in range(batch)])
    kb = tails[:, :, :B_KV_WIDTH].reshape(batch, new_rows, B_KV_HEADS, B_HEAD_DIM)
    vb = tails[:, :, B_KV_WIDTH:].reshape(batch, new_rows, B_KV_HEADS, B_HEAD_DIM)
    o_b = swa(proj, batch, seqlen, p["rel_bias"], p["attn_sinks"][e], e, k_cache, v_cache)
    if k_cache is None:
        k_new, v_new = kb, vb
    else:
        k_new = jnp.concatenate([k_cache[e, :, new_rows:], kb], axis=1)
        v_new = jnp.concatenate([v_cache[e, :, new_rows:], vb], axis=1)
    x = even_out(o_a, proj, o_b, x, p["hgrn_norm_g"][e], wget(("out", e)), p["norm_mix_post"][layer])
    return x, s_new, k_new, v_new


def _odd_layer(x, batch, seqlen, o, layer, p, wget, shift0, s0, v_first):
    m, d = x.shape
    g_pre = p["norm_mix_pre"][layer]
    has_vres = o > 0
    width = lambda rng: rng[1] - rng[0]
    w1p, w2p = _pad_lora(p["rw_w1"][o], p["rw_w2"][o], width(LORA_W))
    a1p, a2p = _pad_lora(p["rw_a1"][o], p["rw_a2"][o], width(LORA_A))
    g1p, g2p = _pad_lora(p["rw_g1"][o], p["rw_g2"][o], width(LORA_G))
    first, low2 = [w1p, a1p, g1p], [p["rw_w0"][o], w2p, p["rw_a0"][o], a2p, g2p]
    if has_vres:
        v1p, v2p = _pad_lora(p["rw_v1"][o - 1], p["rw_v2"][o - 1], width(LORA_V))
        first.append(v1p)
        low2 += [p["rw_v0"][o - 1], v2p]
    rkv, mid = rwkv_in(x, g_pre, shift0, batch, seqlen, p["rw_mu"][o], wget(("rkv", o)),
                       jnp.concatenate(first, axis=1), has_vres)
    yg, s_new = rwkv_core(rkv, mid, low2, p["rw_kk"][o], p["rw_ka"][o], p["rw_rk"][o],
                          p["rw_lnx_g"][o], p["rw_lnx_b"][o], batch, seqlen, o, s0,
                          v_first if has_vres else None)
    shift_new = rmsnorm_rows(x.reshape(batch, seqlen, d)[:, -1], g_pre)
    x = odd_out(yg, x, wget(("wo", o)), p["norm_mix_post"][layer])
    return x, s_new, shift_new, rkv


def _trunk(x3, st_hgrn, k_cache, v_cache, st_rwkv, st_shift, p, bank):
    batch, seqlen, d = x3.shape
    x = x3.reshape(batch * seqlen, d)
    has_state = st_hgrn is not None
    hgrn_out, k_out, v_out, rwkv_out, shift_out = [], [], [], [], []
    v_first = None
    src = _weight_sources(p)
    wget = functools.partial(_bf16_weight, bank, src)
    nsteps = ffn_steps(batch * seqlen, D_FF)
    for layer in range(DEPTH):
        if layer % 2 == 0:
            e = layer // 2
            x, s_new, k_new, v_new = _even_layer(x, batch, seqlen, e, layer, p, wget, st_hgrn, k_cache, v_cache)
            hgrn_out.append(s_new)
            k_out.append(k_new)
            v_out.append(v_new)
        else:
            o = layer // 2
            x, s_new, sh_new, rkv = _odd_layer(
                x, batch, seqlen, o, layer, p, wget,
                st_shift[o] if has_state else None,
                st_rwkv,
                v_first)
            if o == 0:
                v_first = rkv
            rwkv_out.append(s_new)
            shift_out.append(sh_new)
        nl = layer + 1
        wanted = []
        if nl < DEPTH:
            wanted = [("up", nl), ("down", nl)]
            wanted += [("in", nl // 2), ("out", nl // 2)] if nl % 2 == 0 else [("rkv", nl // 2), ("wo", nl // 2)]
        jobs = [k for k in wanted
                if k not in bank and can_cast_in(nsteps, *src[k][0][0].shape[-2:])]
        x, cast = ffn(x, p["norm_ffn_pre"][layer], wget(("up", layer)), wget(("down", layer)),
                      p["norm_ffn_post"][layer], [src[k] for k in jobs])
        bank.update(zip(jobs, cast))
    return (x.reshape(batch, seqlen, d), jnp.stack(hgrn_out), jnp.stack(k_out), jnp.stack(v_out),
            jnp.stack(rwkv_out), jnp.stack(shift_out))


def kernel(x_prompt, x_sample, state_hgrn, cache_swa_k, cache_swa_v, state_rwkv, state_shift,
           norm_mix_pre, norm_mix_post, norm_ffn_pre, norm_ffn_post,
           w_in_even, w_out_even, hgrn_lb_raw, hgrn_norm_g, rel_bias, attn_sinks,
           rw_mu, rw_wr, rw_wk, rw_wv, rw_wo, rw_w0, rw_w1, rw_w2, rw_a0, rw_a1, rw_a2,
           rw_v0, rw_v1, rw_v2, rw_g1, rw_g2, rw_kk, rw_ka, rw_rk, rw_lnx_g, rw_lnx_b,
           w_up, w_down):
    p = {
        "norm_mix_pre": norm_mix_pre, "norm_mix_post": norm_mix_post,
        "norm_ffn_pre": norm_ffn_pre, "norm_ffn_post": norm_ffn_post,
        "w_in_even": w_in_even, "w_out_even": w_out_even,
        "hgrn_lb_raw": hgrn_lb_raw, "hgrn_norm_g": hgrn_norm_g,
        "rel_bias": rel_bias, "attn_sinks": attn_sinks,
        "rw_mu": rw_mu, "rw_wr": rw_wr, "rw_wk": rw_wk, "rw_wv": rw_wv, "rw_wo": rw_wo,
        "rw_w0": rw_w0, "rw_w1": rw_w1, "rw_w2": rw_w2, "rw_a0": rw_a0, "rw_a1": rw_a1, "rw_a2": rw_a2,
        "rw_v0": rw_v0, "rw_v1": rw_v1, "rw_v2": rw_v2, "rw_g1": rw_g1, "rw_g2": rw_g2,
        "rw_kk": rw_kk, "rw_ka": rw_ka, "rw_rk": rw_rk, "rw_lnx_g": rw_lnx_g, "rw_lnx_b": rw_lnx_b,
        "w_up": w_up, "w_down": w_down,
    }
    bank = {}
    y_p, hgrn_p, k_p, v_p, rwkv_p, shift_p = _trunk(x_prompt, None, None, None, None, None, p, bank)
    y_s, hgrn_s, k_s, v_s, rwkv_s, shift_s = _trunk(
        x_sample, state_hgrn, cache_swa_k, cache_swa_v, state_rwkv, state_shift, p, bank)
    return (y_p, y_s, hgrn_p, hgrn_s, k_p, k_s, v_p, v_s, rwkv_p, rwkv_s, shift_p, shift_s)
```

```python
import functools
import math

import numpy as np
import jax
import jax.numpy as jnp
from jax import lax
from jax.experimental import pallas as pl
from jax.experimental.pallas import tpu as pltpu

F32 = jnp.float32
BF16 = jnp.bfloat16

D_MODEL = 2048
DEPTH = 4
N_EVEN = 2
N_ODD = 2
A_HEADS = 8
A_KDIM = 128
A_VDIM = 128
A_WIDTH = 1024
A_QK = 1024
B_HEADS = 16
B_HEAD_DIM = 64
B_KV_HEADS = 4
B_GROUP = 4
B_WIDTH = 1024
B_KV_WIDTH = 256
WINDOW = 128
N_BUCKETS = 32
MAX_DISTANCE = 128
MASK_VALUE = -1e30
IN_A = 4096
IN_EVEN = 5632
C_HEAD = 64
C_HEADS = 32
GN_EPS = 64e-5
D_FF = 8192
NORM_EPS = 1e-6

LANES = 128
VMEM_LIMIT = 56 * 1024 * 1024

HGRN_CHUNK = 128
RWKV_CHUNK = 64
RWKV_TRI_BLOCK = 16


def _cparams(sem):
    return pltpu.CompilerParams(dimension_semantics=sem, vmem_limit_bytes=VMEM_LIMIT)


def _rms(x, g):
    return x * lax.rsqrt(jnp.mean(x * x, axis=-1, keepdims=True) + NORM_EPS) * g


def _dot(a, b):
    return jnp.dot(a.astype(BF16), b.astype(BF16), preferred_element_type=F32)


def _dot_nt(a, b):
    return lax.dot_general(a.astype(BF16), b.astype(BF16), (((1,), (1,)), ((), ())),
                           preferred_element_type=F32)


def _dot_tn(a, b):
    return lax.dot_general(a.astype(BF16), b.astype(BF16), (((0,), (0,)), ((), ())),
                           preferred_element_type=F32)


def _dot_shared(lhs_list, b):
    res = _dot(jnp.concatenate(lhs_list, axis=0), b)
    out, off = [], 0
    for a in lhs_list:
        out.append(res[off:off + a.shape[0]])
        off += a.shape[0]
    return out


def _pick_tile(m, cands):
    for c in cands:
        if m % c == 0:
            return c
    return m


def _norm_matmul_kernel(x_ref, g_ref, w_ref, o_ref, xn_ref):
    @pl.when(pl.program_id(1) == 0)
    def _():
        xn_ref[...] = _rms(x_ref[...], g_ref[...]).astype(BF16)

    o_ref[...] = jnp.dot(xn_ref[...], w_ref[...], preferred_element_type=F32)


def norm_matmul(x, g, w_bf16, tn=1408):
    m, d = x.shape
    n = w_bf16.shape[1]
    tm = _pick_tile(m, (1024, 512, 256, 128, 64, 32, 16, 8))
    return pl.pallas_call(
        _norm_matmul_kernel,
        grid=(m // tm, n // tn),
        in_specs=[pl.BlockSpec((tm, d), lambda i, j: (i, 0)),
                  pl.BlockSpec((1, d), lambda i, j: (0, 0)),
                  pl.BlockSpec((d, tn), lambda i, j: (0, j))],
        out_specs=pl.BlockSpec((tm, tn), lambda i, j: (i, j)),
        out_shape=jax.ShapeDtypeStruct((m, n), F32),
        scratch_shapes=[pltpu.VMEM((tm, d), BF16)],
        compiler_params=_cparams(("parallel", "arbitrary")),
        name="norm_matmul",
    )(x, g.reshape(1, d), w_bf16)


def _ffn_kernel(*refs, cast_srcs):
    nsrc = sum(cast_srcs)
    x_ref, gpre_ref, wup_ref, wdn_ref, gpost_ref = refs[:5]
    src_refs = refs[5:5 + nsrc]
    o_ref = refs[5 + nsrc]
    cast_out = refs[6 + nsrc:6 + nsrc + len(cast_srcs)]
    xn_ref, acc_ref = refs[-2:]
    k = 0
    for out_ref, n in zip(cast_out, cast_srcs):
        for j in range(n):
            if n == 1:
                out_ref[...] = src_refs[k][...].astype(BF16)
            else:
                out_ref[j] = src_refs[k][...].astype(BF16)
            k += 1
    f = pl.program_id(1)

    @pl.when(f == 0)
    def _():
        xn_ref[...] = _rms(x_ref[...], gpre_ref[...]).astype(BF16)
        acc_ref[...] = jnp.zeros_like(acc_ref)

    h = jnp.dot(xn_ref[...], wup_ref[...], preferred_element_type=F32)
    h = jnp.square(jnp.maximum(h, 0.0)).astype(BF16)
    acc_ref[...] += jnp.dot(h, wdn_ref[...], preferred_element_type=F32)

    @pl.when(f == pl.num_programs(1) - 1)
    def _():
        o_ref[...] = x_ref[...] + _rms(acc_ref[...], gpost_ref[...])


BF16_SUBLANES = 16


def ffn_steps(m, dff, tf=1024):
    return (m // _pick_tile(m, (512, 256, 128, 64, 32, 16, 8))) * (dff // tf)


CAST_BLOCK_BYTES = 1 << 20


def can_cast_in(nsteps, rows, cols):
    return (rows % nsteps == 0 and (rows // nsteps) % BF16_SUBLANES == 0
            and (rows // nsteps) * cols * 4 <= CAST_BLOCK_BYTES)


def ffn(x, gpre, wup_bf16, wdn_bf16, gpost, casts=(), tf=1024):
    m, d = x.shape
    dff = wup_bf16.shape[1]
    tm = _pick_tile(m, (512, 256, 128, 64, 32, 16, 8))
    nf = dff // tf
    nsteps = (m // tm) * nf
    step = lambda i, f: i * nf + f
    in_specs = [pl.BlockSpec((tm, d), lambda i, f: (i, 0)),
                pl.BlockSpec((1, d), lambda i, f: (0, 0)),
                pl.BlockSpec((d, tf), lambda i, f: (0, f)),
                pl.BlockSpec((tf, d), lambda i, f: (f, 0)),
                pl.BlockSpec((1, d), lambda i, f: (0, 0))]
    args = [x, gpre.reshape(1, d), wup_bf16, wdn_bf16, gpost.reshape(1, d)]
    out_specs = [pl.BlockSpec((tm, d), lambda i, f: (i, 0))]
    out_shape = [jax.ShapeDtypeStruct((m, d), F32)]
    for job in casts:
        rows, cols = job[0][0].shape[len(job[0][1]):]
        assert can_cast_in(nsteps, rows, cols)
        rb = rows // nsteps
        for arr, lead in job:
            in_specs.append(pl.BlockSpec((None,) * len(lead) + (rb, cols),
                                         lambda i, f, lead=lead: lead + (step(i, f), 0)))
            args.append(arr)
        if len(job) == 1:
            out_specs.append(pl.BlockSpec((rb, cols), lambda i, f: (step(i, f), 0)))
            out_shape.append(jax.ShapeDtypeStruct((rows, cols), BF16))
        else:
            out_specs.append(pl.BlockSpec((len(job), rb, cols), lambda i, f: (0, step(i, f), 0)))
            out_shape.append(jax.ShapeDtypeStruct((len(job), rows, cols), BF16))
    outs = pl.pallas_call(
        functools.partial(_ffn_kernel, cast_srcs=tuple(len(job) for job in casts)),
        grid=(m // tm, nf),
        in_specs=in_specs,
        out_specs=out_specs,
        out_shape=out_shape,
        scratch_shapes=[pltpu.VMEM((tm, d), BF16), pltpu.VMEM((tm, d), F32)],
        compiler_params=_cparams(("parallel", "arbitrary")),
        name="ffn",
    )(*args)
    return outs[0], list(outs[1:])


def _even_out_kernel(oa_ref, ga_ref, ob_ref, x_ref, ag_ref, w_ref, gpost_ref, o_ref):
    ga = ga_ref[...]
    oan = _rms(oa_ref[...], ag_ref[...]) * (ga * jax.nn.sigmoid(ga))
    mix = (jnp.dot(oan.astype(BF16), w_ref[:A_WIDTH, :], preferred_element_type=F32)
           + jnp.dot(ob_ref[...].astype(BF16), w_ref[A_WIDTH:, :], preferred_element_type=F32))
    o_ref[...] = x_ref[...] + _rms(mix, gpost_ref[...])


def even_out(o_a, proj, o_b, x, a_norm_g, w_out_bf16, gpost):
    m, d = x.shape
    tm = _pick_tile(m, (384, 256, 128, 64, 32, 16, 8))
    ga_blk = (3 * A_WIDTH) // A_WIDTH
    return pl.pallas_call(
        _even_out_kernel,
        grid=(m // tm,),
        in_specs=[pl.BlockSpec((tm, A_WIDTH), lambda i: (i, 0)),
                  pl.BlockSpec((tm, A_WIDTH), lambda i: (i, ga_blk)),
                  pl.BlockSpec((tm, B_WIDTH), lambda i: (i, 0)),
                  pl.BlockSpec((tm, d), lambda i: (i, 0)),
                  pl.BlockSpec((1, A_WIDTH), lambda i: (0, 0)),
                  pl.BlockSpec((A_WIDTH + B_WIDTH, d), lambda i: (0, 0)),
                  pl.BlockSpec((1, d), lambda i: (0, 0))],
        out_specs=pl.BlockSpec((tm, d), lambda i: (i, 0)),
        out_shape=jax.ShapeDtypeStruct((m, d), F32),
        compiler_params=_cparams(("parallel",)),
        name="even_out",
    )(o_a, proj, o_b, x, a_norm_g.reshape(1, A_WIDTH), w_out_bf16, gpost.reshape(1, d))


def _odd_out_kernel(y_ref, x_ref, w_ref, gpost_ref, o_ref):
    mix = jnp.dot(y_ref[...].astype(BF16), w_ref[...], preferred_element_type=F32)
    o_ref[...] = x_ref[...] + _rms(mix, gpost_ref[...])


def odd_out(yg, x, wo_bf16, gpost):
    m, d = x.shape
    tm = _pick_tile(m, (384, 256, 128, 64, 32, 16, 8))
    return pl.pallas_call(
        _odd_out_kernel,
        grid=(m // tm,),
        in_specs=[pl.BlockSpec((tm, d), lambda i: (i, 0)),
                  pl.BlockSpec((tm, d), lambda i: (i, 0)),
                  pl.BlockSpec((d, d), lambda i: (0, 0)),
                  pl.BlockSpec((1, d), lambda i: (0, 0))],
        out_specs=pl.BlockSpec((tm, d), lambda i: (i, 0)),
        out_shape=jax.ShapeDtypeStruct((m, d), F32),
        compiler_params=_cparams(("parallel",)),
        name="odd_out",
    )(yg, x, wo_bf16, gpost.reshape(1, d))


def _level_consts(c):
    levels = []
    s = c // 2
    while s >= 1:
        levels.append(s)
        s //= 2
    mask = np.zeros((len(levels), c, c), np.float32)
    idx = np.arange(c)
    for l, s in enumerate(levels):
        same = (idx[:, None] // (2 * s)) == (idx[None, :] // (2 * s))
        upper = (idx[:, None] % (2 * s)) >= s
        lower = (idx[None, :] % (2 * s)) < s
        mask[l] = (same & upper & lower).astype(np.float32)
    return levels, mask


def _split_rows(g, s, rowid):
    c = g.shape[0]
    if 2 * s >= 8:
        return jnp.concatenate(
            [jnp.broadcast_to(g[b + s - 1:b + s, :], (2 * s, g.shape[1])) for b in range(0, c, 2 * s)], axis=0)
    r = rowid % (2 * s)
    out = g
    for off in range(-(s - 1), s + 1):
        if off != 0:
            out = jnp.where(r == s - 1 + off, pltpu.roll(g, off % c, 0), out)
    return out


def _hgrn_kernel(*refs, layer, chunk, nchunks, levels, has_state, nheads):
    if has_state:
        q_ref, f_ref, i_ref, lb_ref, mask_ref, s0_ref, o_ref, s_ref, st_ref = refs
    else:
        q_ref, f_ref, i_ref, lb_ref, mask_ref, o_ref, s_ref, st_ref = refs
    c = chunk
    nh = nheads
    rowid = lax.broadcasted_iota(jnp.int32, (c, nh * LANES), 0)
    l_idx = pl.program_id(2)

    @pl.when(l_idx == 0)
    def _():
        for hi in range(nh):
            if has_state:
                st_ref[hi] = s0_ref[0, hi].T
            else:
                st_ref[hi] = jnp.zeros((A_VDIM, A_KDIM), F32)

    lbr = lb_ref[...]
    e = jnp.exp(lbr - jnp.max(lbr, axis=0, keepdims=True))
    p = e / jnp.sum(e, axis=0, keepdims=True)
    lb = jnp.zeros((1, nh * LANES), F32)
    for i in range(1, layer + 1):
        lb = lb + p[i:i + 1, :]
    one_m_lb = 1.0 - lb
    head = lambda x, hi: x[:, hi * LANES:(hi + 1) * LANES]

    def body(ci, carry):
        rows = pl.ds(pl.multiple_of(ci * c, c), c)
        fq = f_ref[rows, :]
        qr = q_ref[rows, :]
        v = i_ref[rows, :]
        f = lb + one_m_lb * jax.nn.sigmoid(fq)
        k = 1.0 - f
        q = qr * jax.nn.sigmoid(qr) * (A_KDIM ** -0.5)

        g = jnp.log2(f)
        sft = 1
        while sft < c:
            g = g + jnp.where(rowid >= sft, pltpu.roll(g, sft, 0), 0.0)
            sft *= 2
        glast = g[c - 1:c, :]
        q_in = q * jnp.exp2(g)
        kd = k * jnp.exp2(glast - g)
        dec = jnp.exp2(glast)
        diag = q * k

        sts = [st_ref[hi] for hi in range(nh)]
        os_ = [_dot_nt(head(q_in, hi), sts[hi]) for hi in range(nh)]
        attns = [jnp.zeros((c, c), F32) for _ in range(nh)]
        for l, s in enumerate(levels):
            e = jnp.exp2(-jnp.abs(g - _split_rows(g, s, rowid)))
            qs = q * e
            ks = k * e
            ml = mask_ref[l]
            attns = [at + ml * _dot_nt(head(qs, hi), head(ks, hi)) for hi, at in enumerate(attns)]
        for hi in range(nh):
            vh = head(v, hi)
            o = os_[hi] + _dot(attns[hi], vh) + jnp.sum(head(diag, hi), axis=-1, keepdims=True) * vh
            o_ref[rows, hi * LANES:(hi + 1) * LANES] = o
        for hi in range(nh):
            st_ref[hi] = sts[hi] * head(dec, hi) + _dot_tn(head(v, hi), head(kd, hi))
        return carry

    lax.fori_loop(0, nchunks, body, 0, unroll=2 if nchunks % 2 == 0 else 1)

    @pl.when(l_idx == pl.num_programs(2) - 1)
    def _():
        for hi in range(nh):
            s_ref[0, hi] = st_ref[hi].T


def hgrn(proj, lb_raw, layer, batch, seqlen, s0=None, nheads=A_HEADS):
    m = proj.shape[0]
    c = math.gcd(seqlen, HGRN_CHUNK)
    lblk = _pick_tile(seqlen, (512, 256, 128, 64, 32, 16, 8))
    nl_blocks = seqlen // lblk
    levels, mask = _level_consts(c)
    has_state = s0 is not None
    if c * 4 <= HGRN_CHUNK:
        nheads = A_HEADS
    w = nheads * LANES
    ngroups = A_HEADS // nheads
    kern = functools.partial(_hgrn_kernel, layer=layer, chunk=c, nchunks=lblk // c,
                             levels=tuple(levels), has_state=has_state, nheads=nheads)
    row = lambda b, h, l: b * nl_blocks + l
    in_specs = [pl.BlockSpec((lblk, w), lambda b, h, l: (row(b, h, l), h)),
                pl.BlockSpec((lblk, w), lambda b, h, l: (row(b, h, l), ngroups + h)),
                pl.BlockSpec((lblk, w), lambda b, h, l: (row(b, h, l), 2 * ngroups + h)),
                pl.BlockSpec((N_EVEN, w), lambda b, h, l: (0, h)),
                pl.BlockSpec((len(levels), c, c), lambda b, h, l: (0, 0, 0))]
    args = [proj, proj, proj, lb_raw, jnp.asarray(mask)]
    st_blk = pl.BlockSpec((1, nheads, A_KDIM, A_VDIM), lambda b, h, l: (b, h, 0, 0))
    if has_state:
        in_specs.append(pl.BlockSpec((None, 1, nheads, A_KDIM, A_VDIM), lambda b, h, l: (layer, b, h, 0, 0)))
        args.append(s0)
    return pl.pallas_call(
        kern,
        grid=(batch, ngroups, nl_blocks),
        in_specs=in_specs,
        out_specs=[pl.BlockSpec((lblk, w), lambda b, h, l: (row(b, h, l), h)), st_blk],
        out_shape=[jax.ShapeDtypeStruct((m, A_WIDTH), F32),
                   jax.ShapeDtypeStruct((batch, A_HEADS, A_KDIM, A_VDIM), F32)],
        scratch_shapes=[pltpu.VMEM((nheads, A_VDIM, A_KDIM), F32)],
        compiler_params=_cparams(("parallel", "parallel", "arbitrary")),
        name="hgrn",
    )(*args)


def _t5_bucket(dist):
    max_exact = N_BUCKETS // 2
    d = np.maximum(dist, 0)
    large = max_exact + (np.log(np.maximum(d, max_exact).astype(np.float32) / max_exact)
                         / math.log(MAX_DISTANCE / max_exact) * (N_BUCKETS - max_exact)).astype(np.int32)
    large = np.minimum(large, N_BUCKETS - 1)
    return np.where(d < max_exact, d, large).astype(np.int32)


def _swa_kernel(q_ref, kp_ref, kc_ref, vp_ref, vc_ref, bucket_ref, band_ref, rb_ref, sink_ref,
                o_ref, bias_ref, *, qb, span, prev_always_valid):
    first = (pl.program_id(0) == 0) & (pl.program_id(1) == 0)

    @pl.when(first)
    def _():
        bk = bucket_ref[...]
        band = band_ref[...]

        def per_head(h, carry):
            def per_bucket(bi, acc):
                return jnp.where(bk == bi, rb_ref[bi, h], acc)
            acc = lax.fori_loop(0, N_BUCKETS, per_bucket, jnp.zeros((qb, span), F32))
            bias_ref[h] = jnp.where(band > 0, acc, MASK_VALUE)
            return carry

        lax.fori_loop(0, B_HEADS, per_head, 0)

    scale = B_HEAD_DIM ** -0.5
    q = q_ref[...]
    kall = jnp.concatenate([kp_ref[...], kc_ref[...]], axis=0)
    vall = jnp.concatenate([vp_ref[...], vc_ref[...]], axis=0)
    if not prev_always_valid:
        col = lax.broadcasted_iota(jnp.int32, (qb, span), 1)
        no_prev = (col < WINDOW) & (pl.program_id(1) == 0)
    heads = range(B_HEADS)
    ks = [kall[:, kh * B_HEAD_DIM:(kh + 1) * B_HEAD_DIM].astype(BF16) for kh in range(B_KV_HEADS)]
    vs = [vall[:, kh * B_HEAD_DIM:(kh + 1) * B_HEAD_DIM].astype(BF16) for kh in range(B_KV_HEADS)]
    qs = [(q[:, h * B_HEAD_DIM:(h + 1) * B_HEAD_DIM] * scale).astype(BF16) for h in heads]
    ss = [_dot_nt(qs[h], ks[h // B_GROUP]) + bias_ref[h] for h in heads]
    if not prev_always_valid:
        ss = [jnp.where(no_prev, MASK_VALUE, s) for s in ss]
    ms = [jnp.maximum(jnp.max(ss[h], axis=-1, keepdims=True), sink_ref[h]) for h in heads]
    ps = [jnp.exp(s - m) for s, m in zip(ss, ms)]
    denoms = [jnp.sum(ps[h], axis=-1, keepdims=True) + jnp.exp(sink_ref[h] - ms[h]) for h in heads]
    outs = [_dot(ps[h], vs[h // B_GROUP]) / denoms[h] for h in heads]
    o_ref[...] = jnp.concatenate(outs, axis=1).astype(o_ref.dtype)


def swa(proj, batch, seqlen, rel_bias, sinks, layer, k_past=None, v_past=None):
    m = proj.shape[0]
    has_cache = k_past is not None
    qb = math.gcd(seqlen, WINDOW)
    nb = seqlen // qb
    span = WINDOW + qb
    dist = np.arange(qb)[:, None] + WINDOW - np.arange(span)[None, :]
    band = ((dist >= 0) & (dist < WINDOW)).astype(np.float32)
    bucket = _t5_bucket(dist)
    q_col = IN_A // B_WIDTH
    k_col = (IN_A + B_WIDTH) // B_KV_WIDTH
    v_col = k_col + 1
    cur = lambda c: (lambda b, n: (b * nb + n, c))
    if has_cache:
        assert nb == 1
        prev_k = pl.BlockSpec((None, WINDOW, B_KV_WIDTH), lambda b, n: (layer, b, 0))
        prev_v = pl.BlockSpec((None, WINDOW, B_KV_WIDTH), lambda b, n: (layer, b, 0))
        kp_arr = k_past.reshape(k_past.shape[0], batch * WINDOW, B_KV_WIDTH)
        vp_arr = v_past.reshape(v_past.shape[0], batch * WINDOW, B_KV_WIDTH)
    else:
        assert qb == WINDOW
        prev = lambda c: (lambda b, n: (b * nb + jnp.maximum(n - 1, 0), c))
        prev_k = pl.BlockSpec((WINDOW, B_KV_WIDTH), prev(k_col))
        prev_v = pl.BlockSpec((WINDOW, B_KV_WIDTH), prev(v_col))
        kp_arr, vp_arr = proj, proj
    kern = functools.partial(_swa_kernel, qb=qb, span=span, prev_always_valid=has_cache)
    return pl.pallas_call(
        kern,
        grid=(batch, nb),
        in_specs=[pl.BlockSpec((qb, B_WIDTH), cur(q_col)),
                  prev_k,
                  pl.BlockSpec((qb, B_KV_WIDTH), cur(k_col)),
                  prev_v,
                  pl.BlockSpec((qb, B_KV_WIDTH), cur(v_col)),
                  pl.BlockSpec((qb, span), lambda b, n: (0, 0)),
                  pl.BlockSpec((qb, span), lambda b, n: (0, 0)),
                  pl.BlockSpec(memory_space=pltpu.SMEM),
                  pl.BlockSpec(memory_space=pltpu.SMEM)],
        out_specs=pl.BlockSpec((qb, B_WIDTH), lambda b, n: (b * nb + n, 0)),
        out_shape=jax.ShapeDtypeStruct((m, B_WIDTH), BF16 if qb % 16 == 0 else F32),
        scratch_shapes=[pltpu.VMEM((B_HEADS, qb, span), F32)],
        compiler_params=_cparams(("arbitrary", "arbitrary")),
        name="swa",
    )(proj, kp_arr, proj, vp_arr, proj, jnp.asarray(bucket), jnp.asarray(band), rel_bias, sinks)


def _rmsnorm_kernel(x_ref, g_ref, o_ref):
    o_ref[...] = _rms(x_ref[...], g_ref[...])


def rmsnorm_rows(x, g):
    m, d = x.shape
    tm = _pick_tile(m, (512, 256, 128, 64, 32, 16, 8))
    return pl.pallas_call(
        _rmsnorm_kernel,
        grid=(m // tm,),
        in_specs=[pl.BlockSpec((tm, d), lambda i: (i, 0)), pl.BlockSpec((1, d), lambda i: (0, 0))],
        out_specs=pl.BlockSpec((tm, d), lambda i: (i, 0)),
        out_shape=jax.ShapeDtypeStruct((m, d), F32),
        compiler_params=_cparams(("parallel",)),
        name="rmsnorm",
    )(x, g.reshape(1, d))


LORA_W = (0, 128)
LORA_A = (128, 256)
LORA_G = (256, 512)
LORA_V = (512, 640)


def _rwkv_in_phase(i, step):
    return jnp.where(i % 2 == 0, step, 3 - step)


def _rwkv_in_kernel(x_ref, xp_ref, s_ref, g_ref, mu3_ref, mul_ref, w_ref, w1_ref,
                    rkv_ref, mid_ref, h_s, hp_s, *, tm, seqlen, has_vres):
    step = pl.program_id(1)
    ph = _rwkv_in_phase(pl.program_id(0), step)

    @pl.when(step == 0)
    def _():
        g = g_ref[...]
        h = _rms(x_ref[...], g)
        rowid = lax.broadcasted_iota(jnp.int32, h.shape, 0)
        rolled = pltpu.roll(h, 1, 0)
        if seqlen % tm == 0:
            prev_last = _rms(xp_ref[...], g)[7:8, :]
            at_start = pl.program_id(0) % (seqlen // tm) == 0
            first = jnp.where(at_start, s_ref[...], prev_last)
            hp = jnp.where(rowid == 0, first, rolled)
        else:
            hp = jnp.where(rowid % seqlen == 0, s_ref[...], rolled)
        h_s[...] = h
        hp_s[...] = hp

    @pl.when(ph < 3)
    def _():
        h = h_s[...]
        xm = (h + (hp_s[...] - h) * mu3_ref[0]).astype(BF16)
        rkv_ref[0] = jnp.dot(xm, w_ref[...], preferred_element_type=F32)

    @pl.when(ph == 3)
    def _():
        h = h_s[...]
        xx = hp_s[...] - h
        mix = lambda i: (h + xx * mul_ref[i:i + 1, :]).astype(BF16)
        low = lambda i, rng: jnp.dot(mix(i), w1_ref[:, rng[0]:rng[1]], preferred_element_type=F32)
        parts = [jnp.tanh(low(0, LORA_W)), low(1, LORA_A), jax.nn.sigmoid(low(2, LORA_G))]
        if has_vres:
            parts.append(low(3, LORA_V))
        mid_ref[...] = jnp.concatenate(parts, axis=1).astype(BF16)


def rwkv_in(x, g, shift0, batch, seqlen, mu, w3_bf16, w1cat_bf16, has_vres):
    m, d = x.shape
    tm = next(t for t in (512, 256, 128, 64, 32, 16, 8)
              if m % t == 0 and (seqlen % t == 0 or t % seqlen == 0))
    midw = w1cat_bf16.shape[1]
    mu_rkv = jnp.stack([mu[0], mu[2], mu[3]])[:, None, :]
    mu_low = jnp.stack([mu[1], mu[4], mu[5], mu[3]])
    if shift0 is None:
        shift0 = jnp.zeros((batch, d), F32)
    if seqlen % tm == 0:
        srow = shift0[:, None, :]
        tps = seqlen // tm
        s_spec = pl.BlockSpec((None, 1, d), lambda i, p: (i // tps, 0, 0))
    else:
        srow = jnp.repeat(shift0, seqlen, axis=0)
        s_spec = pl.BlockSpec((tm, d), lambda i, p: (i, 0))
    sub = tm // 8
    proj_of = lambda i, p: jnp.minimum(_rwkv_in_phase(i, p), 2)
    kern = functools.partial(_rwkv_in_kernel, tm=tm, seqlen=seqlen, has_vres=has_vres)
    return pl.pallas_call(
        kern,
        grid=(m // tm, 4),
        in_specs=[pl.BlockSpec((tm, d), lambda i, p: (i, 0)),
                  pl.BlockSpec((8, d), lambda i, p: (jnp.maximum(i * sub - 1, 0), 0)),
                  s_spec,
                  pl.BlockSpec((1, d), lambda i, p: (0, 0)),
                  pl.BlockSpec((1, 1, d), lambda i, p: (proj_of(i, p), 0, 0)),
                  pl.BlockSpec((4, d), lambda i, p: (0, 0)),
                  pl.BlockSpec((None, d, d), lambda i, p: (proj_of(i, p), 0, 0)),
                  pl.BlockSpec((d, midw), lambda i, p: (0, 0))],
        out_specs=[pl.BlockSpec((1, tm, d), lambda i, p: (proj_of(i, p), i, 0)),
                   pl.BlockSpec((tm, midw), lambda i, p: (i, 0))],
        out_shape=[jax.ShapeDtypeStruct((3, m, d), F32),
                   jax.ShapeDtypeStruct((m, midw), BF16)],
        scratch_shapes=[pltpu.VMEM((tm, d), F32), pltpu.VMEM((tm, d), F32)],
        compiler_params=_cparams(("parallel", "arbitrary")),
        name="rwkv_in",
    )(x, x, srow, g.reshape(1, d), mu_rkv, mu_low, w3_bf16, w1cat_bf16)


def _pad_lora(w1, w2, width):
    r = w1.shape[1]
    return (jnp.pad(w1, ((0, 0), (0, width - r))).astype(BF16),
            jnp.pad(w2, ((0, width - r), (0, 0))).astype(BF16))


def _rwkv_consts(c):
    i = np.arange(2 * c)
    same = (i[:, None] // c) == (i[None, :] // c)
    strict = (same & ((i[:, None] % c) > (i[None, :] % c))).astype(np.float32)
    incl = (same & ((i[:, None] % c) >= (i[None, :] % c))).astype(np.float32)
    eye = np.eye(2 * c, dtype=np.float32)
    l = np.arange(LANES)
    headones = ((l[:, None] // C_HEAD) == (l[None, :] // C_HEAD)).astype(np.float32)
    bs = min(RWKV_TRI_BLOCK, c)
    blk = lambda n: (i[:, None] // n) == (i[None, :] // n)
    tmasks = [blk(bs)]
    s = bs
    while s < c:
        tmasks.append(blk(2 * s) & ~blk(s))
        s *= 2
    return strict, incl, eye, headones, np.stack(tmasks).astype(np.float32), bs


def _rwkv_core_kernel(*refs, chunk, nchunks, npairs, has_state, has_vres, bs):
    it = iter(refs)
    r_ref, k_ref, v_ref, mid_ref = (next(it) for _ in range(4))
    w0_ref, w2_ref, a0_ref, a2_ref, g2_ref = (next(it) for _ in range(5))
    if has_vres:
        vf_ref, v0_ref, v2_ref = next(it), next(it), next(it)
    kk_ref, ka_ref, rk_ref, lg_ref, lb_ref = (next(it) for _ in range(5))
    strict_ref, incl_ref, eye_ref, hones_ref, tmask_ref = (next(it) for _ in range(5))
    if has_state:
        s0_ref = next(it)
    y_ref, s_ref, st_ref, wl_ref, a_ref, g_ref = (next(it) for _ in range(6))
    if has_vres:
        vg_ref = next(it)
    c = chunk
    l_idx = pl.program_id(2)
    lane = lax.broadcasted_iota(jnp.int32, (1, LANES), 1)
    m0 = (lane < C_HEAD).astype(F32)
    m1 = 1.0 - m0

    low = lambda rng, w_ref: jnp.dot(mid_ref[:, rng[0]:rng[1]], w_ref[...], preferred_element_type=F32)
    sig = lambda z: 0.5 * jnp.tanh(0.5 * z) + 0.5
    wl_ref[...] = -math.exp(-0.5) * sig(w0_ref[...] + low(LORA_W, w2_ref))
    a_ref[...] = sig(a0_ref[...] + low(LORA_A, a2_ref))
    g_ref[...] = low(LORA_G, g2_ref)
    if has_vres:
        vg_ref[...] = sig(v0_ref[...] + low(LORA_V, v2_ref))

    @pl.when(l_idx == 0)
    def _():
        for pi in range(npairs):
            if has_state:
                z = jnp.zeros((C_HEAD, C_HEAD), F32)
                top = jnp.concatenate([s0_ref[0, 2 * pi], z], axis=1)
                bot = jnp.concatenate([z, s0_ref[0, 2 * pi + 1]], axis=1)
                st_ref[pi] = jnp.concatenate([top, bot], axis=0)
            else:
                st_ref[pi] = jnp.zeros((LANES, LANES), F32)

    strict = strict_ref[...]
    incl = incl_ref[...]
    eye = eye_ref[...]

    def stack(x):
        return jnp.concatenate([x * m0, x * m1], axis=0)

    first_head = lax.broadcasted_iota(jnp.int32, (c, LANES), 1) < C_HEAD
    rowid = lax.broadcasted_iota(jnp.int32, (c, LANES), 0)

    def rowsums(xs):
        return [jnp.where(first_head,
                          jnp.sum(x * m0, axis=-1, keepdims=True),
                          jnp.sum(x * m1, axis=-1, keepdims=True)) for x in xs]

    def cumsum_rows(x):
        s = 1
        while s < c:
            x = x + jnp.where(rowid >= s, pltpu.roll(x, s, 0), 0.0)
            s *= 2
        return x

    def load(pi, rows):
        cols = slice(pi * LANES, (pi + 1) * LANES)
        k = k_ref[0, rows, cols]
        v = v_ref[0, rows, cols]
        a = a_ref[rows, cols]
        if has_vres:
            v = v + (vf_ref[0, rows, cols] - v) * vg_ref[rows, cols]
        return dict(cols=cols, r=r_ref[0, rows, cols], v=v, a=a, wl=wl_ref[rows, cols],
                    kr=k * kk_ref[:, cols], kh=k * (1.0 + (a - 1.0) * ka_ref[:, cols]))

    def decays(p, ss):
        kk = p["kr"] * lax.rsqrt(jnp.maximum(ss, 1e-24))
        b = kk * p["a"]
        gc = cumsum_rows(p["wl"])
        gl = gc[c - 1:c, :]
        e_neg = jnp.exp(-gc)
        e_out = jnp.exp(gl - gc)
        p.update(gl=gl, ab=-kk * jnp.exp(gc - p["wl"]), rb=p["r"] * jnp.exp(gc),
                 bt=b * e_neg, kt=p["kh"] * e_neg, bh=b * e_out, khat=p["kh"] * e_out)

    def intra(p):
        lhs = jnp.concatenate([stack(p["ab"]), stack(p["rb"])], axis=0)
        with_b = _dot_nt(lhs, jnp.concatenate([p["bt"], p["bt"]], axis=0))
        with_k = _dot_nt(lhs, jnp.concatenate([p["kt"], p["kt"]], axis=0))
        p.update(a_ab=with_b[:2 * c] * strict, a_rb=with_b[2 * c:] * incl,
                 a_ak=with_k[:2 * c] * strict, a_rk=with_k[2 * c:] * incl)

    def body(ci, carry):
        rows = pl.ds(pl.multiple_of(ci * c, c), c)
        ps = [load(pi, rows) for pi in range(npairs)]
        for p, ss in zip(ps, rowsums([p["kr"] * p["kr"] for p in ps])):
            decays(p, ss)
        for p in ps:
            intra(p)

        pws = [p["a_ab"] * tmask_ref[0] for p in ps]
        ts = [eye + pw for pw in pws]
        if bs > 2:
            pws = [_dot(pw, pw) for pw in pws]
            n = 2
            while 2 * n < bs:
                res = [_dot_shared([t, pw], pw) for t, pw in zip(ts, pws)]
                ts = [t + r[0] for t, r in zip(ts, res)]
                pws = [r[1] for r in res]
                n *= 2
            ts = [t + _dot(t, pw) for t, pw in zip(ts, pws)]
        s, lvl = bs, 1
        while s < c:
            ms = [_dot(t, p["a_ab"] * tmask_ref[lvl]) for t, p in zip(ts, ps)]
            ts = [t + _dot(m, t) for t, m in zip(ts, ms)]
            s, lvl = 2 * s, lvl + 1

        sts = [st_ref[pi] for pi in range(npairs)]
        fss = [_dot_nt(jnp.concatenate([p["ab"], p["rb"]], axis=0), st) for p, st in zip(ps, sts)]
        vss = [stack(p["v"]) for p in ps]
        rhss = [stack(fs[:c]) + _dot(p["a_ak"], vs) for p, fs, vs in zip(ps, fss, vss)]
        uss = [_dot(t, rhs) for t, rhs in zip(ts, rhss)]
        yss = [_dot(p["a_rb"], us) + _dot(p["a_rk"], vs) for p, us, vs in zip(ps, uss, vss)]
        for pi, (p, st, us) in enumerate(zip(ps, sts, uss)):
            u = us[:c] + us[c:]
            upd = _dot_tn(jnp.concatenate([u, p["v"]], axis=0),
                          jnp.concatenate([p["bh"], p["khat"]], axis=0))
            st_ref[pi] = st * jnp.exp(p["gl"]) + upd * hones_ref[...]
        inv_n = 1.0 / C_HEAD
        ys_ = [fs[c:] + ys[:c] + ys[c:] for fs, ys in zip(fss, yss)]
        sums = rowsums(ys_ + [p["r"] * p["kh"] * rk_ref[:, p["cols"]] for p in ps])
        dlts = [y - m * inv_n for y, m in zip(ys_, sums[:npairs])]
        vars_ = rowsums([d * d for d in dlts])
        for p, dlt, var, bsum in zip(ps, dlts, vars_, sums[npairs:]):
            cols = p["cols"]
            yn = dlt * lax.rsqrt(var * inv_n + GN_EPS) * lg_ref[:, cols] + lb_ref[:, cols]
            y_ref[rows, cols] = ((yn + bsum * p["v"]) * g_ref[rows, cols]).astype(y_ref.dtype)
        return carry

    lax.fori_loop(0, nchunks, body, 0)

    @pl.when(l_idx == pl.num_programs(2) - 1)
    def _():
        for pi in range(npairs):
            st = st_ref[pi]
            s_ref[0, 2 * pi] = st[:C_HEAD, :C_HEAD]
            s_ref[0, 2 * pi + 1] = st[C_HEAD:, C_HEAD:]


def rwkv_core(rkv, mid, low2, kk_p, ka_p, rk_p, lnx_g, lnx_b, batch, seqlen, layer,
              s0=None, v_first=None):
    _, m, d = rkv.shape
    c = math.gcd(seqlen, RWKV_CHUNK)
    npairs = C_HEADS // 2
    lblk = _pick_tile(seqlen, (256, 128, 64, 32, 16, 8))
    nl_blocks = seqlen // lblk
    has_state = s0 is not None
    has_vres = v_first is not None
    w = npairs * LANES
    ngroups = d // w
    midw = mid.shape[1]
    *consts, bs = _rwkv_consts(c)
    strict, incl, eye, hones, tmasks = (jnp.asarray(x) for x in consts)
    row = lambda b, p, l: b * nl_blocks + l
    blk3 = lambda which: pl.BlockSpec((1, lblk, w), lambda b, p, l: (which, row(b, p, l), p))
    blk2 = pl.BlockSpec((lblk, w), lambda b, p, l: (row(b, p, l), p))
    par = pl.BlockSpec((1, w), lambda b, p, l: (0, p))
    cols = lambda arr: pl.BlockSpec((arr.shape[0], w), lambda b, p, l: (0, p))
    full = lambda arr: pl.BlockSpec(arr.shape, lambda b, p, l: (0,) * arr.ndim)
    w0, w2, a0, a2, g2 = low2[:5]
    in_specs = [blk3(0), blk3(1), blk3(2), pl.BlockSpec((lblk, midw), lambda b, p, l: (row(b, p, l), 0)),
                par, cols(w2), par, cols(a2), cols(g2)]
    args = [rkv, rkv, rkv, mid, w0.reshape(1, d), w2, a0.reshape(1, d), a2, g2]
    if has_vres:
        v0, v2 = low2[5:]
        in_specs += [blk3(2), par, cols(v2)]
        args += [v_first, v0.reshape(1, d), v2]
    in_specs += [par] * 5
    args += [x.reshape(1, d) for x in (kk_p, ka_p, rk_p, lnx_g, lnx_b)]
    in_specs += [full(x) for x in (strict, incl, eye, hones, tmasks)]
    args += [strict, incl, eye, hones, tmasks]
    st_blk = pl.BlockSpec((1, 2 * npairs, C_HEAD, C_HEAD), lambda b, p, l: (b, p, 0, 0))
    if has_state:
        in_specs.append(pl.BlockSpec((None, 1, 2 * npairs, C_HEAD, C_HEAD),
                                     lambda b, p, l: (layer, b, p, 0, 0)))
        args.append(s0)
    kern = functools.partial(_rwkv_core_kernel, chunk=c, nchunks=lblk // c, npairs=npairs,
                             has_state=has_state, has_vres=has_vres, bs=bs)
    return pl.pallas_call(
        kern,
        grid=(batch, ngroups, nl_blocks),
        in_specs=in_specs,
        out_specs=[blk2, st_blk],
        out_shape=[jax.ShapeDtypeStruct((m, d), BF16 if lblk % 16 == 0 else F32),
                   jax.ShapeDtypeStruct((batch, C_HEADS, C_HEAD, C_HEAD), F32)],
        scratch_shapes=[pltpu.VMEM((npairs, LANES, LANES), F32)]
        + [pltpu.VMEM((lblk, w), F32)] * (4 if has_vres else 3),
        compiler_params=_cparams(("parallel", "parallel", "arbitrary")),
        name="rwkv_core",
    )(*args)


def _weight_sources(p):
    src = {}
    for l in range(DEPTH):
        src[("up", l)] = [(p["w_up"], (l,))]
        src[("down", l)] = [(p["w_down"], (l,))]
    for e in range(N_EVEN):
        src[("in", e)] = [(p["w_in_even"], (e,))]
        src[("out", e)] = [(p["w_out_even"], (e,))]
    for o in range(N_ODD):
        src[("rkv", o)] = [(p["rw_wr"], (o,)), (p["rw_wk"], (o,)), (p["rw_wv"], (o,))]
        src[("wo", o)] = [(p["rw_wo"], (o,))]
    return src


def _bf16_weight(bank, src, key):
    if key not in bank:
        parts = [arr[lead].astype(BF16) for arr, lead in src[key]]
        bank[key] = parts[0] if len(parts) == 1 else jnp.stack(parts)
    return bank[key]


def _even_layer(x, batch, seqlen, e, layer, p, wget, st_hgrn, k_cache, v_cache):
    proj = norm_matmul(x, p["norm_mix_pre"][layer], wget(("in", e)))
    o_a, s_new = hgrn(proj, p["hgrn_lb_raw"], e, batch, seqlen, st_hgrn)
    k_lo = IN_A + B_WIDTH
    new_rows = min(seqlen, WINDOW)
    tails = jnp.stack([proj[(b + 1) * seqlen - new_rows:(b + 1) * seqlen, k_lo:] for b in range(batch)])
    kb = tails[:, :, :B_KV_WIDTH].reshape(batch, new_rows, B_KV_HEADS, B_HEAD_DIM)
    vb = tails[:, :, B_KV_WIDTH:].reshape(batch, new_rows, B_KV_HEADS, B_HEAD_DIM)
    o_b = swa(proj, batch, seqlen, p["rel_bias"], p["attn_sinks"][e], e, k_cache, v_cache)
    if k_cache is None:
        k_new, v_new = kb, vb
    else:
        k_new = jnp.concatenate([k_cache[e, :, new_rows:], kb], axis=1)
        v_new = jnp.concatenate([v_cache[e, :, new_rows:], vb], axis=1)
    x = even_out(o_a, proj, o_b, x, p["hgrn_norm_g"][e], wget(("out", e)), p["norm_mix_post"][layer])
    return x, s_new, k_new, v_new


def _odd_layer(x, batch, seqlen, o, layer, p, wget, shift0, s0, v_first):
    m, d = x.shape
    g_pre = p["norm_mix_pre"][layer]
    has_vres = o > 0
    width = lambda rng: rng[1] - rng[0]
    w1p, w2p = _pad_lora(p["rw_w1"][o], p["rw_w2"][o], width(LORA_W))
    a1p, a2p = _pad_lora(p["rw_a1"][o], p["rw_a2"][o], width(LORA_A))
    g1p, g2p = _pad_lora(p["rw_g1"][o], p["rw_g2"][o], width(LORA_G))
    first, low2 = [w1p, a1p, g1p], [p["rw_w0"][o], w2p, p["rw_a0"][o], a2p, g2p]
    if has_vres:
        v1p, v2p = _pad_lora(p["rw_v1"][o - 1], p["rw_v2"][o - 1], width(LORA_V))
        first.append(v1p)
        low2 += [p["rw_v0"][o - 1], v2p]
    rkv, mid = rwkv_in(x, g_pre, shift0, batch, seqlen, p["rw_mu"][o], wget(("rkv", o)),
                       jnp.concatenate(first, axis=1), has_vres)
    yg, s_new = rwkv_core(rkv, mid, low2, p["rw_kk"][o], p["rw_ka"][o], p["rw_rk"][o],
                          p["rw_lnx_g"][o], p["rw_lnx_b"][o], batch, seqlen, o, s0,
                          v_first if has_vres else None)
    shift_new = rmsnorm_rows(x.reshape(batch, seqlen, d)[:, -1], g_pre)
    x = odd_out(yg, x, wget(("wo", o)), p["norm_mix_post"][layer])
    return x, s_new, shift_new, rkv


def _trunk(x3, st_hgrn, k_cache, v_cache, st_rwkv, st_shift, p, bank):
    batch, seqlen, d = x3.shape
    x = x3.reshape(batch * seqlen, d)
    has_state = st_hgrn is not None
    hgrn_out, k_out, v_out, rwkv_out, shift_out = [], [], [], [], []
    v_first = None
    src = _weight_sources(p)
    wget = functools.partial(_bf16_weight, bank, src)
    nsteps = ffn_steps(batch * seqlen, D_FF)
    for layer in range(DEPTH):
        if layer % 2 == 0:
            e = layer // 2
            x, s_new, k_new, v_new = _even_layer(x, batch, seqlen, e, layer, p, wget, st_hgrn, k_cache, v_cache)
            hgrn_out.append(s_new)
            k_out.append(k_new)
            v_out.append(v_new)
        else:
            o = layer // 2
            x, s_new, sh_new, rkv = _odd_layer(
                x, batch, seqlen, o, layer, p, wget,
                st_shift[o] if has_state else None,
                st_rwkv,
                v_first)
            if o == 0:
                v_first = rkv
            rwkv_out.append(s_new)
            shift_out.append(sh_new)
        nl = layer + 1
        wanted = []
        if nl < DEPTH:
            wanted = [("up", nl), ("down", nl)]
            wanted += [("in", nl // 2), ("out", nl // 2)] if nl % 2 == 0 else [("rkv", nl // 2), ("wo", nl // 2)]
        jobs = [k for k in wanted
                if k not in bank and can_cast_in(nsteps, *src[k][0][0].shape[-2:])]
        x, cast = ffn(x, p["norm_ffn_pre"][layer], wget(("up", layer)), wget(("down", layer)),
                      p["norm_ffn_post"][layer], [src[k] for k in jobs])
        bank.update(zip(jobs, cast))
    return (x.reshape(batch, seqlen, d), jnp.stack(hgrn_out), jnp.stack(k_out), jnp.stack(v_out),
            jnp.stack(rwkv_out), jnp.stack(shift_out))


def kernel(x_prompt, x_sample, state_hgrn, cache_swa_k, cache_swa_v, state_rwkv, state_shift,
           norm_mix_pre, norm_mix_post, norm_ffn_pre, norm_ffn_post,
           w_in_even, w_out_even, hgrn_lb_raw, hgrn_norm_g, rel_bias, attn_sinks,
           rw_mu, rw_wr, rw_wk, rw_wv, rw_wo, rw_w0, rw_w1, rw_w2, rw_a0, rw_a1, rw_a2,
           rw_v0, rw_v1, rw_v2, rw_g1, rw_g2, rw_kk, rw_ka, rw_rk, rw_lnx_g, rw_lnx_b,
           w_up, w_down):
    p = {
        "norm_mix_pre": norm_mix_pre, "norm_mix_post": norm_mix_post,
        "norm_ffn_pre": norm_ffn_pre, "norm_ffn_post": norm_ffn_post,
        "w_in_even": w_in_even, "w_out_even": w_out_even,
        "hgrn_lb_raw": hgrn_lb_raw, "hgrn_norm_g": hgrn_norm_g,
        "rel_bias": rel_bias, "attn_sinks": attn_sinks,
        "rw_mu": rw_mu, "rw_wr": rw_wr, "rw_wk": rw_wk, "rw_wv": rw_wv, "rw_wo": rw_wo,
        "rw_w0": rw_w0, "rw_w1": rw_w1, "rw_w2": rw_w2, "rw_a0": rw_a0, "rw_a1": rw_a1, "rw_a2": rw_a2,
        "rw_v0": rw_v0, "rw_v1": rw_v1, "rw_v2": rw_v2, "rw_g1": rw_g1, "rw_g2": rw_g2,
        "rw_kk": rw_kk, "rw_ka": rw_ka, "rw_rk": rw_rk, "rw_lnx_g": rw_lnx_g, "rw_lnx_b": rw_lnx_b,
        "w_up": w_up, "w_down": w_down,
    }
    bank = {}
    y_p, hgrn_p, k_p, v_p, rwkv_p, shift_p = _trunk(x_prompt, None, None, None, None, None, p, bank)
    y_s, hgrn_s, k_s, v_s, rwkv_s, shift_s = _trunk(
        x_sample, state_hgrn, cache_swa_k, cache_swa_v, state_rwkv, state_shift, p, bank)
    return (y_p, y_s, hgrn_p, hgrn_s, k_p, k_s, v_p, v_s, rwkv_p, rwkv_s, shift_p, shift_s)
```

```python
import functools
import math

import numpy as np
import jax
import jax.numpy as jnp
from jax import lax
from jax.experimental import pallas as pl
from jax.experimental.pallas import tpu as pltpu

F32 = jnp.float32
BF16 = jnp.bfloat16

D_MODEL = 2048
DEPTH = 4
N_EVEN = 2
N_ODD = 2
A_HEADS = 8
A_KDIM = 128
A_VDIM = 128
A_WIDTH = 1024
A_QK = 1024
B_HEADS = 16
B_HEAD_DIM = 64
B_KV_HEADS = 4
B_GROUP = 4
B_WIDTH = 1024
B_KV_WIDTH = 256
WINDOW = 128
N_BUCKETS = 32
MAX_DISTANCE = 128
MASK_VALUE = -1e30
IN_A = 4096
IN_EVEN = 5632
C_HEAD = 64
C_HEADS = 32
GN_EPS = 64e-5
D_FF = 8192
NORM_EPS = 1e-6

LANES = 128
VMEM_LIMIT = 56 * 1024 * 1024

HGRN_CHUNK = 128
RWKV_CHUNK = 64
RWKV_TRI_BLOCK = 16


def _cparams(sem):
    return pltpu.CompilerParams(dimension_semantics=sem, vmem_limit_bytes=VMEM_LIMIT)


def _rms(x, g):
    return x * lax.rsqrt(jnp.mean(x * x, axis=-1, keepdims=True) + NORM_EPS) * g


def _dot(a, b):
    return jnp.dot(a.astype(BF16), b.astype(BF16), preferred_element_type=F32)


def _dot_nt(a, b):
    return lax.dot_general(a.astype(BF16), b.astype(BF16), (((1,), (1,)), ((), ())),
                           preferred_element_type=F32)


def _dot_tn(a, b):
    return lax.dot_general(a.astype(BF16), b.astype(BF16), (((0,), (0,)), ((), ())),
                           preferred_element_type=F32)


def _dot_shared(lhs_list, b):
    res = _dot(jnp.concatenate(lhs_list, axis=0), b)
    out, off = [], 0
    for a in lhs_list:
        out.append(res[off:off + a.shape[0]])
        off += a.shape[0]
    return out


def _pick_tile(m, cands):
    for c in cands:
        if m % c == 0:
            return c
    return m


def _norm_matmul_kernel(x_ref, g_ref, w_ref, o_ref, xn_ref):
    @pl.when(pl.program_id(1) == 0)
    def _():
        xn_ref[...] = _rms(x_ref[...], g_ref[...]).astype(BF16)

    o_ref[...] = jnp.dot(xn_ref[...], w_ref[...], preferred_element_type=F32)


def norm_matmul(x, g, w_bf16, tn=1408):
    m, d = x.shape
    n = w_bf16.shape[1]
    tm = _pick_tile(m, (1024, 512, 256, 128, 64, 32, 16, 8))
    nj = n // tn
    col = lambda i, j: jnp.where(i % 2 == 0, j, nj - 1 - j)
    return pl.pallas_call(
        _norm_matmul_kernel,
        grid=(m // tm, nj),
        in_specs=[pl.BlockSpec((tm, d), lambda i, j: (i, 0)),
                  pl.BlockSpec((1, d), lambda i, j: (0, 0)),
                  pl.BlockSpec((d, tn), lambda i, j: (0, col(i, j)))],
        out_specs=pl.BlockSpec((tm, tn), lambda i, j: (i, col(i, j))),
        out_shape=jax.ShapeDtypeStruct((m, n), F32),
        scratch_shapes=[pltpu.VMEM((tm, d), BF16)],
        compiler_params=_cparams(("parallel", "arbitrary")),
        name="norm_matmul",
    )(x, g.reshape(1, d), w_bf16)


def _ffn_kernel(*refs, cast_srcs):
    nsrc = sum(cast_srcs)
    x_ref, gpre_ref, wup_ref, wdn_ref, gpost_ref = refs[:5]
    src_refs = refs[5:5 + nsrc]
    o_ref = refs[5 + nsrc]
    cast_out = refs[6 + nsrc:6 + nsrc + len(cast_srcs)]
    xn_ref, acc_ref = refs[-2:]
    k = 0
    for out_ref, n in zip(cast_out, cast_srcs):
        for j in range(n):
            if n == 1:
                out_ref[...] = src_refs[k][...].astype(BF16)
            else:
                out_ref[j] = src_refs[k][...].astype(BF16)
            k += 1
    f = pl.program_id(1)

    @pl.when(f == 0)
    def _():
        xn_ref[...] = _rms(x_ref[...], gpre_ref[...]).astype(BF16)
        acc_ref[...] = jnp.zeros_like(acc_ref)

    h = jnp.dot(xn_ref[...], wup_ref[...], preferred_element_type=F32)
    h = jnp.square(jnp.maximum(h, 0.0)).astype(BF16)
    acc_ref[...] += jnp.dot(h, wdn_ref[...], preferred_element_type=F32)

    @pl.when(f == pl.num_programs(1) - 1)
    def _():
        o_ref[...] = x_ref[...] + _rms(acc_ref[...], gpost_ref[...])


BF16_SUBLANES = 16


def ffn_steps(m, dff, tf=1024):
    return (m // _pick_tile(m, (512, 256, 128, 64, 32, 16, 8))) * (dff // tf)


CAST_BLOCK_BYTES = 1 << 20


def can_cast_in(nsteps, rows, cols):
    return (rows % nsteps == 0 and (rows // nsteps) % BF16_SUBLANES == 0
            and (rows // nsteps) * cols * 4 <= CAST_BLOCK_BYTES)


def ffn(x, gpre, wup_bf16, wdn_bf16, gpost, casts=(), tf=1024):
    m, d = x.shape
    dff = wup_bf16.shape[1]
    tm = _pick_tile(m, (512, 256, 128, 64, 32, 16, 8))
    nf = dff // tf
    nsteps = (m // tm) * nf
    step = lambda i, f: i * nf + f
    in_specs = [pl.BlockSpec((tm, d), lambda i, f: (i, 0)),
                pl.BlockSpec((1, d), lambda i, f: (0, 0)),
                pl.BlockSpec((d, tf), lambda i, f: (0, f)),
                pl.BlockSpec((tf, d), lambda i, f: (f, 0)),
                pl.BlockSpec((1, d), lambda i, f: (0, 0))]
    args = [x, gpre.reshape(1, d), wup_bf16, wdn_bf16, gpost.reshape(1, d)]
    out_specs = [pl.BlockSpec((tm, d), lambda i, f: (i, 0))]
    out_shape = [jax.ShapeDtypeStruct((m, d), F32)]
    for job in casts:
        rows, cols = job[0][0].shape[len(job[0][1]):]
        assert can_cast_in(nsteps, rows, cols)
        rb = rows // nsteps
        for arr, lead in job:
            in_specs.append(pl.BlockSpec((None,) * len(lead) + (rb, cols),
                                         lambda i, f, lead=lead: lead + (step(i, f), 0)))
            args.append(arr)
        if len(job) == 1:
            out_specs.append(pl.BlockSpec((rb, cols), lambda i, f: (step(i, f), 0)))
            out_shape.append(jax.ShapeDtypeStruct((rows, cols), BF16))
        else:
            out_specs.append(pl.BlockSpec((len(job), rb, cols), lambda i, f: (0, step(i, f), 0)))
            out_shape.append(jax.ShapeDtypeStruct((len(job), rows, cols), BF16))
    outs = pl.pallas_call(
        functools.partial(_ffn_kernel, cast_srcs=tuple(len(job) for job in casts)),
        grid=(m // tm, nf),
        in_specs=in_specs,
        out_specs=out_specs,
        out_shape=out_shape,
        scratch_shapes=[pltpu.VMEM((tm, d), BF16), pltpu.VMEM((tm, d), F32)],
        compiler_params=_cparams(("parallel", "arbitrary")),
        name="ffn",
    )(*args)
    return outs[0], list(outs[1:])


def _even_out_kernel(oa_ref, ga_ref, ob_ref, x_ref, ag_ref, w_ref, gpost_ref, o_ref):
    ga = ga_ref[...]
    oan = _rms(oa_ref[...], ag_ref[...]) * (ga * jax.nn.sigmoid(ga))
    mix = (jnp.dot(oan.astype(BF16), w_ref[:A_WIDTH, :], preferred_element_type=F32)
           + jnp.dot(ob_ref[...].astype(BF16), w_ref[A_WIDTH:, :], preferred_element_type=F32))
    o_ref[...] = x_ref[...] + _rms(mix, gpost_ref[...])


def even_out(o_a, proj, o_b, x, a_norm_g, w_out_bf16, gpost):
    m, d = x.shape
    tm = _pick_tile(m, (384, 256, 128, 64, 32, 16, 8))
    ga_blk = (3 * A_WIDTH) // A_WIDTH
    return pl.pallas_call(
        _even_out_kernel,
        grid=(m // tm,),
        in_specs=[pl.BlockSpec((tm, A_WIDTH), lambda i: (i, 0)),
                  pl.BlockSpec((tm, A_WIDTH), lambda i: (i, ga_blk)),
                  pl.BlockSpec((tm, B_WIDTH), lambda i: (i, 0)),
                  pl.BlockSpec((tm, d), lambda i: (i, 0)),
                  pl.BlockSpec((1, A_WIDTH), lambda i: (0, 0)),
                  pl.BlockSpec((A_WIDTH + B_WIDTH, d), lambda i: (0, 0)),
                  pl.BlockSpec((1, d), lambda i: (0, 0))],
        out_specs=pl.BlockSpec((tm, d), lambda i: (i, 0)),
        out_shape=jax.ShapeDtypeStruct((m, d), F32),
        compiler_params=_cparams(("parallel",)),
        name="even_out",
    )(o_a, proj, o_b, x, a_norm_g.reshape(1, A_WIDTH), w_out_bf16, gpost.reshape(1, d))


def _odd_out_kernel(y_ref, x_ref, w_ref, gpost_ref, o_ref):
    mix = jnp.dot(y_ref[...].astype(BF16), w_ref[...], preferred_element_type=F32)
    o_ref[...] = x_ref[...] + _rms(mix, gpost_ref[...])


def odd_out(yg, x, wo_bf16, gpost):
    m, d = x.shape
    tm = _pick_tile(m, (384, 256, 128, 64, 32, 16, 8))
    return pl.pallas_call(
        _odd_out_kernel,
        grid=(m // tm,),
        in_specs=[pl.BlockSpec((tm, d), lambda i: (i, 0)),
                  pl.BlockSpec((tm, d), lambda i: (i, 0)),
                  pl.BlockSpec((d, d), lambda i: (0, 0)),
                  pl.BlockSpec((1, d), lambda i: (0, 0))],
        out_specs=pl.BlockSpec((tm, d), lambda i: (i, 0)),
        out_shape=jax.ShapeDtypeStruct((m, d), F32),
        compiler_params=_cparams(("parallel",)),
        name="odd_out",
    )(yg, x, wo_bf16, gpost.reshape(1, d))


def _level_consts(c):
    levels = []
    s = c // 2
    while s >= 1:
        levels.append(s)
        s //= 2
    mask = np.zeros((len(levels), c, c), np.float32)
    idx = np.arange(c)
    for l, s in enumerate(levels):
        same = (idx[:, None] // (2 * s)) == (idx[None, :] // (2 * s))
        upper = (idx[:, None] % (2 * s)) >= s
        lower = (idx[None, :] % (2 * s)) < s
        mask[l] = (same & upper & lower).astype(np.float32)
    return levels, mask


def _split_rows(g, s, rowid):
    c = g.shape[0]
    if 2 * s >= 8:
        return jnp.concatenate(
            [jnp.broadcast_to(g[b + s - 1:b + s, :], (2 * s, g.shape[1])) for b in range(0, c, 2 * s)], axis=0)
    r = rowid % (2 * s)
    out = g
    for off in range(-(s - 1), s + 1):
        if off != 0:
            out = jnp.where(r == s - 1 + off, pltpu.roll(g, off % c, 0), out)
    return out


def _hgrn_kernel(*refs, layer, chunk, nchunks, levels, has_state, nheads):
    if has_state:
        q_ref, f_ref, i_ref, lb_ref, mask_ref, s0_ref, o_ref, s_ref, st_ref = refs
    else:
        q_ref, f_ref, i_ref, lb_ref, mask_ref, o_ref, s_ref, st_ref = refs
    c = chunk
    nh = nheads
    rowid = lax.broadcasted_iota(jnp.int32, (c, nh * LANES), 0)
    l_idx = pl.program_id(2)

    @pl.when(l_idx == 0)
    def _():
        for hi in range(nh):
            if has_state:
                st_ref[hi] = s0_ref[0, hi].T
            else:
                st_ref[hi] = jnp.zeros((A_VDIM, A_KDIM), F32)

    lbr = lb_ref[...]
    e = jnp.exp(lbr - jnp.max(lbr, axis=0, keepdims=True))
    p = e / jnp.sum(e, axis=0, keepdims=True)
    lb = jnp.zeros((1, nh * LANES), F32)
    for i in range(1, layer + 1):
        lb = lb + p[i:i + 1, :]
    one_m_lb = 1.0 - lb
    head = lambda x, hi: x[:, hi * LANES:(hi + 1) * LANES]

    def body(ci, carry):
        rows = pl.ds(pl.multiple_of(ci * c, c), c)
        fq = f_ref[rows, :]
        qr = q_ref[rows, :]
        v = i_ref[rows, :]
        f = lb + one_m_lb * jax.nn.sigmoid(fq)
        k = 1.0 - f
        q = qr * jax.nn.sigmoid(qr) * (A_KDIM ** -0.5)

        g = jnp.log2(f)
        sft = 1
        while sft < c:
            g = g + jnp.where(rowid >= sft, pltpu.roll(g, sft, 0), 0.0)
            sft *= 2
        glast = g[c - 1:c, :]
        q_in = q * jnp.exp2(g)
        kd = k * jnp.exp2(glast - g)
        dec = jnp.exp2(glast)
        diag = q * k

        sts = [st_ref[hi] for hi in range(nh)]
        os_ = [_dot_nt(head(q_in, hi), sts[hi]) for hi in range(nh)]
        attns = [jnp.zeros((c, c), F32) for _ in range(nh)]
        for l, s in enumerate(levels):
            e = jnp.exp2(-jnp.abs(g - _split_rows(g, s, rowid)))
            qs = q * e
            ks = k * e
            ml = mask_ref[l]
            attns = [at + ml * _dot_nt(head(qs, hi), head(ks, hi)) for hi, at in enumerate(attns)]
        for hi in range(nh):
            vh = head(v, hi)
            o = os_[hi] + _dot(attns[hi], vh) + jnp.sum(head(diag, hi), axis=-1, keepdims=True) * vh
            o_ref[rows, hi * LANES:(hi + 1) * LANES] = o
        for hi in range(nh):
            st_ref[hi] = sts[hi] * head(dec, hi) + _dot_tn(head(v, hi), head(kd, hi))
        return carry

    lax.fori_loop(0, nchunks, body, 0, unroll=2 if nchunks % 2 == 0 else 1)

    @pl.when(l_idx == pl.num_programs(2) - 1)
    def _():
        for hi in range(nh):
            s_ref[0, hi] = st_ref[hi].T


def hgrn(proj, lb_raw, layer, batch, seqlen, s0=None, nheads=A_HEADS):
    m = proj.shape[0]
    c = math.gcd(seqlen, HGRN_CHUNK)
    lblk = _pick_tile(seqlen, (512, 256, 128, 64, 32, 16, 8))
    nl_blocks = seqlen // lblk
    levels, mask = _level_consts(c)
    has_state = s0 is not None
    if c * 4 <= HGRN_CHUNK:
        nheads = A_HEADS
    w = nheads * LANES
    ngroups = A_HEADS // nheads
    kern = functools.partial(_hgrn_kernel, layer=layer, chunk=c, nchunks=lblk // c,
                             levels=tuple(levels), has_state=has_state, nheads=nheads)
    row = lambda b, h, l: b * nl_blocks + l
    in_specs = [pl.BlockSpec((lblk, w), lambda b, h, l: (row(b, h, l), h)),
                pl.BlockSpec((lblk, w), lambda b, h, l: (row(b, h, l), ngroups + h)),
                pl.BlockSpec((lblk, w), lambda b, h, l: (row(b, h, l), 2 * ngroups + h)),
                pl.BlockSpec((N_EVEN, w), lambda b, h, l: (0, h)),
                pl.BlockSpec((len(levels), c, c), lambda b, h, l: (0, 0, 0))]
    args = [proj, proj, proj, lb_raw, jnp.asarray(mask)]
    st_blk = pl.BlockSpec((1, nheads, A_KDIM, A_VDIM), lambda b, h, l: (b, h, 0, 0))
    if has_state:
        in_specs.append(pl.BlockSpec((None, 1, nheads, A_KDIM, A_VDIM), lambda b, h, l: (layer, b, h, 0, 0)))
        args.append(s0)
    return pl.pallas_call(
        kern,
        grid=(batch, ngroups, nl_blocks),
        in_specs=in_specs,
        out_specs=[pl.BlockSpec((lblk, w), lambda b, h, l: (row(b, h, l), h)), st_blk],
        out_shape=[jax.ShapeDtypeStruct((m, A_WIDTH), F32),
                   jax.ShapeDtypeStruct((batch, A_HEADS, A_KDIM, A_VDIM), F32)],
        scratch_shapes=[pltpu.VMEM((nheads, A_VDIM, A_KDIM), F32)],
        compiler_params=_cparams(("parallel", "parallel", "arbitrary")),
        name="hgrn",
    )(*args)


def _t5_bucket(dist):
    max_exact = N_BUCKETS // 2
    d = np.maximum(dist, 0)
    large = max_exact + (np.log(np.maximum(d, max_exact).astype(np.float32) / max_exact)
                         / math.log(MAX_DISTANCE / max_exact) * (N_BUCKETS - max_exact)).astype(np.int32)
    large = np.minimum(large, N_BUCKETS - 1)
    return np.where(d < max_exact, d, large).astype(np.int32)


def _swa_kernel(q_ref, kp_ref, kc_ref, vp_ref, vc_ref, bucket_ref, band_ref, rb_ref, sink_ref,
                o_ref, bias_ref, *, qb, span, prev_always_valid):
    first = (pl.program_id(0) == 0) & (pl.program_id(1) == 0)

    @pl.when(first)
    def _():
        bk = bucket_ref[...]
        band = band_ref[...]

        def per_head(h, carry):
            def per_bucket(bi, acc):
                return jnp.where(bk == bi, rb_ref[bi, h], acc)
            acc = lax.fori_loop(0, N_BUCKETS, per_bucket, jnp.zeros((qb, span), F32))
            bias_ref[h] = jnp.where(band > 0, acc, MASK_VALUE)
            return carry

        lax.fori_loop(0, B_HEADS, per_head, 0)

    scale = B_HEAD_DIM ** -0.5
    q = q_ref[...]
    kall = jnp.concatenate([kp_ref[...], kc_ref[...]], axis=0)
    vall = jnp.concatenate([vp_ref[...], vc_ref[...]], axis=0)
    if not prev_always_valid:
        col = lax.broadcasted_iota(jnp.int32, (qb, span), 1)
        no_prev = (col < WINDOW) & (pl.program_id(1) == 0)
    heads = range(B_HEADS)
    ks = [kall[:, kh * B_HEAD_DIM:(kh + 1) * B_HEAD_DIM].astype(BF16) for kh in range(B_KV_HEADS)]
    vs = [vall[:, kh * B_HEAD_DIM:(kh + 1) * B_HEAD_DIM].astype(BF16) for kh in range(B_KV_HEADS)]
    qs = [(q[:, h * B_HEAD_DIM:(h + 1) * B_HEAD_DIM] * scale).astype(BF16) for h in heads]
    ss = [_dot_nt(qs[h], ks[h // B_GROUP]) + bias_ref[h] for h in heads]
    if not prev_always_valid:
        ss = [jnp.where(no_prev, MASK_VALUE, s) for s in ss]
    ms = [jnp.maximum(jnp.max(ss[h], axis=-1, keepdims=True), sink_ref[h]) for h in heads]
    ps = [jnp.exp(s - m) for s, m in zip(ss, ms)]
    denoms = [jnp.sum(ps[h], axis=-1, keepdims=True) + jnp.exp(sink_ref[h] - ms[h]) for h in heads]
    outs = [_dot(ps[h], vs[h // B_GROUP]) / denoms[h] for h in heads]
    o_ref[...] = jnp.concatenate(outs, axis=1).astype(o_ref.dtype)


def swa(proj, batch, seqlen, rel_bias, sinks, layer, k_past=None, v_past=None):
    m = proj.shape[0]
    has_cache = k_past is not None
    qb = math.gcd(seqlen, WINDOW)
    nb = seqlen // qb
    span = WINDOW + qb
    dist = np.arange(qb)[:, None] + WINDOW - np.arange(span)[None, :]
    band = ((dist >= 0) & (dist < WINDOW)).astype(np.float32)
    bucket = _t5_bucket(dist)
    q_col = IN_A // B_WIDTH
    k_col = (IN_A + B_WIDTH) // B_KV_WIDTH
    v_col = k_col + 1
    cur = lambda c: (lambda b, n: (b * nb + n, c))
    if has_cache:
        assert nb == 1
        prev_k = pl.BlockSpec((None, WINDOW, B_KV_WIDTH), lambda b, n: (layer, b, 0))
        prev_v = pl.BlockSpec((None, WINDOW, B_KV_WIDTH), lambda b, n: (layer, b, 0))
        kp_arr = k_past.reshape(k_past.shape[0], batch * WINDOW, B_KV_WIDTH)
        vp_arr = v_past.reshape(v_past.shape[0], batch * WINDOW, B_KV_WIDTH)
    else:
        assert qb == WINDOW
        prev = lambda c: (lambda b, n: (b * nb + jnp.maximum(n - 1, 0), c))
        prev_k = pl.BlockSpec((WINDOW, B_KV_WIDTH), prev(k_col))
        prev_v = pl.BlockSpec((WINDOW, B_KV_WIDTH), prev(v_col))
        kp_arr, vp_arr = proj, proj
    kern = functools.partial(_swa_kernel, qb=qb, span=span, prev_always_valid=has_cache)
    return pl.pallas_call(
        kern,
        grid=(batch, nb),
        in_specs=[pl.BlockSpec((qb, B_WIDTH), cur(q_col)),
                  prev_k,
                  pl.BlockSpec((qb, B_KV_WIDTH), cur(k_col)),
                  prev_v,
                  pl.BlockSpec((qb, B_KV_WIDTH), cur(v_col)),
                  pl.BlockSpec((qb, span), lambda b, n: (0, 0)),
                  pl.BlockSpec((qb, span), lambda b, n: (0, 0)),
                  pl.BlockSpec(memory_space=pltpu.SMEM),
                  pl.BlockSpec(memory_space=pltpu.SMEM)],
        out_specs=pl.BlockSpec((qb, B_WIDTH), lambda b, n: (b * nb + n, 0)),
        out_shape=jax.ShapeDtypeStruct((m, B_WIDTH), BF16 if qb % 16 == 0 else F32),
        scratch_shapes=[pltpu.VMEM((B_HEADS, qb, span), F32)],
        compiler_params=_cparams(("arbitrary", "arbitrary")),
        name="swa",
    )(proj, kp_arr, proj, vp_arr, proj, jnp.asarray(bucket), jnp.asarray(band), rel_bias, sinks)


def _rmsnorm_kernel(x_ref, g_ref, o_ref):
    o_ref[...] = _rms(x_ref[...], g_ref[...])


def rmsnorm_rows(x, g):
    m, d = x.shape
    tm = _pick_tile(m, (512, 256, 128, 64, 32, 16, 8))
    return pl.pallas_call(
        _rmsnorm_kernel,
        grid=(m // tm,),
        in_specs=[pl.BlockSpec((tm, d), lambda i: (i, 0)), pl.BlockSpec((1, d), lambda i: (0, 0))],
        out_specs=pl.BlockSpec((tm, d), lambda i: (i, 0)),
        out_shape=jax.ShapeDtypeStruct((m, d), F32),
        compiler_params=_cparams(("parallel",)),
        name="rmsnorm",
    )(x, g.reshape(1, d))


LORA_W = (0, 128)
LORA_A = (128, 256)
LORA_G = (256, 512)
LORA_V = (512, 640)


def _rwkv_in_phase(i, step):
    return jnp.where(i % 2 == 0, step, 3 - step)


def _rwkv_in_kernel(x_ref, xp_ref, s_ref, g_ref, mu3_ref, mul_ref, w_ref, w1_ref,
                    rkv_ref, mid_ref, h_s, hp_s, *, tm, seqlen, has_vres):
    step = pl.program_id(1)
    ph = _rwkv_in_phase(pl.program_id(0), step)

    @pl.when(step == 0)
    def _():
        g = g_ref[...]
        h = _rms(x_ref[...], g)
        rowid = lax.broadcasted_iota(jnp.int32, h.shape, 0)
        rolled = pltpu.roll(h, 1, 0)
        if seqlen % tm == 0:
            prev_last = _rms(xp_ref[...], g)[7:8, :]
            at_start = pl.program_id(0) % (seqlen // tm) == 0
            first = jnp.where(at_start, s_ref[...], prev_last)
            hp = jnp.where(rowid == 0, first, rolled)
        else:
            hp = jnp.where(rowid % seqlen == 0, s_ref[...], rolled)
        h_s[...] = h
        hp_s[...] = hp

    @pl.when(ph < 3)
    def _():
        h = h_s[...]
        xm = (h + (hp_s[...] - h) * mu3_ref[0]).astype(BF16)
        rkv_ref[0] = jnp.dot(xm, w_ref[...], preferred_element_type=F32)

    @pl.when(ph == 3)
    def _():
        h = h_s[...]
        xx = hp_s[...] - h
        mix = lambda i: (h + xx * mul_ref[i:i + 1, :]).astype(BF16)
        low = lambda i, rng: jnp.dot(mix(i), w1_ref[:, rng[0]:rng[1]], preferred_element_type=F32)
        parts = [jnp.tanh(low(0, LORA_W)), low(1, LORA_A), jax.nn.sigmoid(low(2, LORA_G))]
        if has_vres:
            parts.append(low(3, LORA_V))
        mid_ref[...] = jnp.concatenate(parts, axis=1).astype(BF16)


def rwkv_in(x, g, shift0, batch, seqlen, mu, w3_bf16, w1cat_bf16, has_vres):
    m, d = x.shape
    tm = next(t for t in (512, 256, 128, 64, 32, 16, 8)
              if m % t == 0 and (seqlen % t == 0 or t % seqlen == 0))
    midw = w1cat_bf16.shape[1]
    mu_rkv = jnp.stack([mu[0], mu[2], mu[3]])[:, None, :]
    mu_low = jnp.stack([mu[1], mu[4], mu[5], mu[3]])
    if shift0 is None:
        shift0 = jnp.zeros((batch, d), F32)
    if seqlen % tm == 0:
        srow = shift0[:, None, :]
        tps = seqlen // tm
        s_spec = pl.BlockSpec((None, 1, d), lambda i, p: (i // tps, 0, 0))
    else:
        srow = jnp.repeat(shift0, seqlen, axis=0)
        s_spec = pl.BlockSpec((tm, d), lambda i, p: (i, 0))
    sub = tm // 8
    proj_of = lambda i, p: jnp.minimum(_rwkv_in_phase(i, p), 2)
    kern = functools.partial(_rwkv_in_kernel, tm=tm, seqlen=seqlen, has_vres=has_vres)
    return pl.pallas_call(
        kern,
        grid=(m // tm, 4),
        in_specs=[pl.BlockSpec((tm, d), lambda i, p: (i, 0)),
                  pl.BlockSpec((8, d), lambda i, p: (jnp.maximum(i * sub - 1, 0), 0)),
                  s_spec,
                  pl.BlockSpec((1, d), lambda i, p: (0, 0)),
                  pl.BlockSpec((1, 1, d), lambda i, p: (proj_of(i, p), 0, 0)),
                  pl.BlockSpec((4, d), lambda i, p: (0, 0)),
                  pl.BlockSpec((None, d, d), lambda i, p: (proj_of(i, p), 0, 0)),
                  pl.BlockSpec((d, midw), lambda i, p: (0, 0))],
        out_specs=[pl.BlockSpec((1, tm, d), lambda i, p: (proj_of(i, p), i, 0)),
                   pl.BlockSpec((tm, midw), lambda i, p: (i, 0))],
        out_shape=[jax.ShapeDtypeStruct((3, m, d), F32),
                   jax.ShapeDtypeStruct((m, midw), BF16)],
        scratch_shapes=[pltpu.VMEM((tm, d), F32), pltpu.VMEM((tm, d), F32)],
        compiler_params=_cparams(("parallel", "arbitrary")),
        name="rwkv_in",
    )(x, x, srow, g.reshape(1, d), mu_rkv, mu_low, w3_bf16, w1cat_bf16)


def _pad_lora(w1, w2, width):
    r = w1.shape[1]
    return (jnp.pad(w1, ((0, 0), (0, width - r))).astype(BF16),
            jnp.pad(w2, ((0, width - r), (0, 0))).astype(BF16))


def _rwkv_consts(c):
    i = np.arange(2 * c)
    same = (i[:, None] // c) == (i[None, :] // c)
    strict = (same & ((i[:, None] % c) > (i[None, :] % c))).astype(np.float32)
    incl = (same & ((i[:, None] % c) >= (i[None, :] % c))).astype(np.float32)
    eye = np.eye(2 * c, dtype=np.float32)
    l = np.arange(LANES)
    headones = ((l[:, None] // C_HEAD) == (l[None, :] // C_HEAD)).astype(np.float32)
    bs = min(RWKV_TRI_BLOCK, c)
    blk = lambda n: (i[:, None] // n) == (i[None, :] // n)
    tmasks = [blk(bs)]
    s = bs
    while s < c:
        tmasks.append(blk(2 * s) & ~blk(s))
        s *= 2
    return strict, incl, eye, headones, np.stack(tmasks).astype(np.float32), bs


def _rwkv_core_kernel(*refs, chunk, nchunks, npairs, has_state, has_vres, bs):
    it = iter(refs)
    r_ref, k_ref, v_ref, mid_ref = (next(it) for _ in range(4))
    w0_ref, w2_ref, a0_ref, a2_ref, g2_ref = (next(it) for _ in range(5))
    if has_vres:
        vf_ref, v0_ref, v2_ref = next(it), next(it), next(it)
    kk_ref, ka_ref, rk_ref, lg_ref, lb_ref = (next(it) for _ in range(5))
    strict_ref, incl_ref, eye_ref, hones_ref, tmask_ref = (next(it) for _ in range(5))
    if has_state:
        s0_ref = next(it)
    y_ref, s_ref, st_ref, wl_ref, a_ref, g_ref = (next(it) for _ in range(6))
    if has_vres:
        vg_ref = next(it)
    c = chunk
    l_idx = pl.program_id(2)
    lane = lax.broadcasted_iota(jnp.int32, (1, LANES), 1)
    m0 = (lane < C_HEAD).astype(F32)
    m1 = 1.0 - m0

    low = lambda rng, w_ref: jnp.dot(mid_ref[:, rng[0]:rng[1]], w_ref[...], preferred_element_type=F32)
    sig = lambda z: 0.5 * jnp.tanh(0.5 * z) + 0.5
    wl_ref[...] = -math.exp(-0.5) * sig(w0_ref[...] + low(LORA_W, w2_ref))
    a_ref[...] = sig(a0_ref[...] + low(LORA_A, a2_ref))
    g_ref[...] = low(LORA_G, g2_ref)
    if has_vres:
        vg_ref[...] = sig(v0_ref[...] + low(LORA_V, v2_ref))

    @pl.when(l_idx == 0)
    def _():
        for pi in range(npairs):
            if has_state:
                z = jnp.zeros((C_HEAD, C_HEAD), F32)
                top = jnp.concatenate([s0_ref[0, 2 * pi], z], axis=1)
                bot = jnp.concatenate([z, s0_ref[0, 2 * pi + 1]], axis=1)
                st_ref[pi] = jnp.concatenate([top, bot], axis=0)
            else:
                st_ref[pi] = jnp.zeros((LANES, LANES), F32)

    strict = strict_ref[...]
    incl = incl_ref[...]
    eye = eye_ref[...]

    def stack(x):
        return jnp.concatenate([x * m0, x * m1], axis=0)

    first_head = lax.broadcasted_iota(jnp.int32, (c, LANES), 1) < C_HEAD
    rowid = lax.broadcasted_iota(jnp.int32, (c, LANES), 0)

    def rowsums(xs):
        return [jnp.where(first_head,
                          jnp.sum(x * m0, axis=-1, keepdims=True),
                          jnp.sum(x * m1, axis=-1, keepdims=True)) for x in xs]

    def cumsum_rows(x):
        s = 1
        while s < c:
            x = x + jnp.where(rowid >= s, pltpu.roll(x, s, 0), 0.0)
            s *= 2
        return x

    def load(pi, rows):
        cols = slice(pi * LANES, (pi + 1) * LANES)
        k = k_ref[0, rows, cols]
        v = v_ref[0, rows, cols]
        a = a_ref[rows, cols]
        if has_vres:
            v = v + (vf_ref[0, rows, cols] - v) * vg_ref[rows, cols]
        return dict(cols=cols, r=r_ref[0, rows, cols], v=v, a=a, wl=wl_ref[rows, cols],
                    kr=k * kk_ref[:, cols], kh=k * (1.0 + (a - 1.0) * ka_ref[:, cols]))

    def decays(p, ss):
        kk = p["kr"] * lax.rsqrt(jnp.maximum(ss, 1e-24))
        b = kk * p["a"]
        gc = cumsum_rows(p["wl"])
        gl = gc[c - 1:c, :]
        e_neg = jnp.exp(-gc)
        e_out = jnp.exp(gl - gc)
        p.update(gl=gl, ab=-kk * jnp.exp(gc - p["wl"]), rb=p["r"] * jnp.exp(gc),
                 bt=b * e_neg, kt=p["kh"] * e_neg, bh=b * e_out, khat=p["kh"] * e_out)

    def intra(p):
        lhs = jnp.concatenate([stack(p["ab"]), stack(p["rb"])], axis=0)
        with_b = _dot_nt(lhs, jnp.concatenate([p["bt"], p["bt"]], axis=0))
        with_k = _dot_nt(lhs, jnp.concatenate([p["kt"], p["kt"]], axis=0))
        p.update(a_ab=with_b[:2 * c] * strict, a_rb=with_b[2 * c:] * incl,
                 a_ak=with_k[:2 * c] * strict, a_rk=with_k[2 * c:] * incl)

    def body(ci, carry):
        rows = pl.ds(pl.multiple_of(ci * c, c), c)
        ps = [load(pi, rows) for pi in range(npairs)]
        for p, ss in zip(ps, rowsums([p["kr"] * p["kr"] for p in ps])):
            decays(p, ss)
        for p in ps:
            intra(p)

        pws = [p["a_ab"] * tmask_ref[0] for p in ps]
        ts = [eye + pw for pw in pws]
        if bs > 2:
            pws = [_dot(pw, pw) for pw in pws]
            n = 2
            while 2 * n < bs:
                res = [_dot_shared([t, pw], pw) for t, pw in zip(ts, pws)]
                ts = [t + r[0] for t, r in zip(ts, res)]
                pws = [r[1] for r in res]
                n *= 2
            ts = [t + _dot(t, pw) for t, pw in zip(ts, pws)]
        s, lvl = bs, 1
        while s < c:
            ms = [_dot(t, p["a_ab"] * tmask_ref[lvl]) for t, p in zip(ts, ps)]
            ts = [t + _dot(m, t) for t, m in zip(ts, ms)]
            s, lvl = 2 * s, lvl + 1

        sts = [st_ref[pi] for pi in range(npairs)]
        fss = [_dot_nt(jnp.concatenate([p["ab"], p["rb"]], axis=0), st) for p, st in zip(ps, sts)]
        vss = [stack(p["v"]) for p in ps]
        rhss = [stack(fs[:c]) + _dot(p["a_ak"], vs) for p, fs, vs in zip(ps, fss, vss)]
        uss = [_dot(t, rhs) for t, rhs in zip(ts, rhss)]
        yss = [_dot(p["a_rb"], us) + _dot(p["a_rk"], vs) for p, us, vs in zip(ps, uss, vss)]
        for pi, (p, st, us) in enumerate(zip(ps, sts, uss)):
            u = us[:c] + us[c:]
            upd = _dot_tn(jnp.concatenate([u, p["v"]], axis=0),
                          jnp.concatenate([p["bh"], p["khat"]], axis=0))
            st_ref[pi] = st * jnp.exp(p["gl"]) + upd * hones_ref[...]
        inv_n = 1.0 / C_HEAD
        ys_ = [fs[c:] + ys[:c] + ys[c:] for fs, ys in zip(fss, yss)]
        sums = rowsums(ys_ + [p["r"] * p["kh"] * rk_ref[:, p["cols"]] for p in ps])
        dlts = [y - m * inv_n for y, m in zip(ys_, sums[:npairs])]
        vars_ = rowsums([d * d for d in dlts])
        for p, dlt, var, bsum in zip(ps, dlts, vars_, sums[npairs:]):
            cols = p["cols"]
            yn = dlt * lax.rsqrt(var * inv_n + GN_EPS) * lg_ref[:, cols] + lb_ref[:, cols]
            y_ref[rows, cols] = ((yn + bsum * p["v"]) * g_ref[rows, cols]).astype(y_ref.dtype)
        return carry

    lax.fori_loop(0, nchunks, body, 0)

    @pl.when(l_idx == pl.num_programs(2) - 1)
    def _():
        for pi in range(npairs):
            st = st_ref[pi]
            s_ref[0, 2 * pi] = st[:C_HEAD, :C_HEAD]
            s_ref[0, 2 * pi + 1] = st[C_HEAD:, C_HEAD:]


def rwkv_core(rkv, mid, low2, kk_p, ka_p, rk_p, lnx_g, lnx_b, batch, seqlen, layer,
              s0=None, v_first=None):
    _, m, d = rkv.shape
    c = math.gcd(seqlen, RWKV_CHUNK)
    npairs = C_HEADS // 2
    lblk = _pick_tile(seqlen, (256, 128, 64, 32, 16, 8))
    nl_blocks = seqlen // lblk
    has_state = s0 is not None
    has_vres = v_first is not None
    w = npairs * LANES
    ngroups = d // w
    midw = mid.shape[1]
    *consts, bs = _rwkv_consts(c)
    strict, incl, eye, hones, tmasks = (jnp.asarray(x) for x in consts)
    row = lambda b, p, l: b * nl_blocks + l
    blk3 = lambda which: pl.BlockSpec((1, lblk, w), lambda b, p, l: (which, row(b, p, l), p))
    blk2 = pl.BlockSpec((lblk, w), lambda b, p, l: (row(b, p, l), p))
    par = pl.BlockSpec((1, w), lambda b, p, l: (0, p))
    cols = lambda arr: pl.BlockSpec((arr.shape[0], w), lambda b, p, l: (0, p))
    full = lambda arr: pl.BlockSpec(arr.shape, lambda b, p, l: (0,) * arr.ndim)
    w0, w2, a0, a2, g2 = low2[:5]
    in_specs = [blk3(0), blk3(1), blk3(2), pl.BlockSpec((lblk, midw), lambda b, p, l: (row(b, p, l), 0)),
                par, cols(w2), par, cols(a2), cols(g2)]
    args = [rkv, rkv, rkv, mid, w0.reshape(1, d), w2, a0.reshape(1, d), a2, g2]
    if has_vres:
        v0, v2 = low2[5:]
        in_specs += [blk3(2), par, cols(v2)]
        args += [v_first, v0.reshape(1, d), v2]
    in_specs += [par] * 5
    args += [x.reshape(1, d) for x in (kk_p, ka_p, rk_p, lnx_g, lnx_b)]
    in_specs += [full(x) for x in (strict, incl, eye, hones, tmasks)]
    args += [strict, incl, eye, hones, tmasks]
    st_blk = pl.BlockSpec((1, 2 * npairs, C_HEAD, C_HEAD), lambda b, p, l: (b, p, 0, 0))
    if has_state:
        in_specs.append(pl.BlockSpec((None, 1, 2 * npairs, C_HEAD, C_HEAD),
                                     lambda b, p, l: (layer, b, p, 0, 0)))
        args.append(s0)
    kern = functools.partial(_rwkv_core_kernel, chunk=c, nchunks=lblk // c, npairs=npairs,
                             has_state=has_state, has_vres=has_vres, bs=bs)
    return pl.pallas_call(
        kern,
        grid=(batch, ngroups, nl_blocks),
        in_specs=in_specs,
        out_specs=[blk2, st_blk],
        out_shape=[jax.ShapeDtypeStruct((m, d), BF16 if lblk % 16 == 0 else F32),
                   jax.ShapeDtypeStruct((batch, C_HEADS, C_HEAD, C_HEAD), F32)],
        scratch_shapes=[pltpu.VMEM((npairs, LANES, LANES), F32)]
        + [pltpu.VMEM((lblk, w), F32)] * (4 if has_vres else 3),
        compiler_params=_cparams(("parallel", "parallel", "arbitrary")),
        name="rwkv_core",
    )(*args)


def _weight_sources(p):
    src = {}
    for l in range(DEPTH):
        src[("up", l)] = [(p["w_up"], (l,))]
        src[("down", l)] = [(p["w_down"], (l,))]
    for e in range(N_EVEN):
        src[("in", e)] = [(p["w_in_even"], (e,))]
        src[("out", e)] = [(p["w_out_even"], (e,))]
    for o in range(N_ODD):
        src[("rkv", o)] = [(p["rw_wr"], (o,)), (p["rw_wk"], (o,)), (p["rw_wv"], (o,))]
        src[("wo", o)] = [(p["rw_wo"], (o,))]
    return src


def _bf16_weight(bank, src, key):
    if key not in bank:
        parts = [arr[lead].astype(BF16) for arr, lead in src[key]]
        bank[key] = parts[0] if len(parts) == 1 else jnp.stack(parts)
    return bank[key]


def _even_layer(x, batch, seqlen, e, layer, p, wget, st_hgrn, k_cache, v_cache):
    proj = norm_matmul(x, p["norm_mix_pre"][layer], wget(("in", e)))
    o_a, s_new = hgrn(proj, p["hgrn_lb_raw"], e, batch, seqlen, st_hgrn)
    k_lo = IN_A + B_WIDTH
    new_rows = min(seqlen, WINDOW)
    tails = jnp.stack([proj[(b + 1) * seqlen - new_rows:(b + 1) * seqlen, k_lo:] for b in range(batch)])
    kb = tails[:, :, :B_KV_WIDTH].reshape(batch, new_rows, B_KV_HEADS, B_HEAD_DIM)
    vb = tails[:, :, B_KV_WIDTH:].reshape(batch, new_rows, B_KV_HEADS, B_HEAD_DIM)
    o_b = swa(proj, batch, seqlen, p["rel_bias"], p["attn_sinks"][e], e, k_cache, v_cache)
    if k_cache is None:
        k_new, v_new = kb, vb
    else:
        k_new = jnp.concatenate([k_cache[e, :, new_rows:], kb], axis=1)
        v_new = jnp.concatenate([v_cache[e, :, new_rows:], vb], axis=1)
    x = even_out(o_a, proj, o_b, x, p["hgrn_norm_g"][e], wget(("out", e)), p["norm_mix_post"][layer])
    return x, s_new, k_new, v_new


def _odd_layer(x, batch, seqlen, o, layer, p, wget, shift0, s0, v_first):
    m, d = x.shape
    g_pre = p["norm_mix_pre"][layer]
    has_vres = o > 0
    width = lambda rng: rng[1] - rng[0]
    w1p, w2p = _pad_lora(p["rw_w1"][o], p["rw_w2"][o], width(LORA_W))
    a1p, a2p = _pad_lora(p["rw_a1"][o], p["rw_a2"][o], width(LORA_A))
    g1p, g2p = _pad_lora(p["rw_g1"][o], p["rw_g2"][o], width(LORA_G))
    first, low2 = [w1p, a1p, g1p], [p["rw_w0"][o], w2p, p["rw_a0"][o], a2p, g2p]
    if has_vres:
        v1p, v2p = _pad_lora(p["rw_v1"][o - 1], p["rw_v2"][o - 1], width(LORA_V))
        first.append(v1p)
        low2 += [p["rw_v0"][o - 1], v2p]
    rkv, mid = rwkv_in(x, g_pre, shift0, batch, seqlen, p["rw_mu"][o], wget(("rkv", o)),
                       jnp.concatenate(first, axis=1), has_vres)
    yg, s_new = rwkv_core(rkv, mid, low2, p["rw_kk"][o], p["rw_ka"][o], p["rw_rk"][o],
                          p["rw_lnx_g"][o], p["rw_lnx_b"][o], batch, seqlen, o, s0,
                          v_first if has_vres else None)
    shift_new = rmsnorm_rows(x.reshape(batch, seqlen, d)[:, -1], g_pre)
    x = odd_out(yg, x, wget(("wo", o)), p["norm_mix_post"][layer])
    return x, s_new, shift_new, rkv


def _trunk(x3, st_hgrn, k_cache, v_cache, st_rwkv, st_shift, p, bank):
    batch, seqlen, d = x3.shape
    x = x3.reshape(batch * seqlen, d)
    has_state = st_hgrn is not None
    hgrn_out, k_out, v_out, rwkv_out, shift_out = [], [], [], [], []
    v_first = None
    src = _weight_sources(p)
    wget = functools.partial(_bf16_weight, bank, src)
    nsteps = ffn_steps(batch * seqlen, D_FF)
    for layer in range(DEPTH):
        if layer % 2 == 0:
            e = layer // 2
            x, s_new, k_new, v_new = _even_layer(x, batch, seqlen, e, layer, p, wget, st_hgrn, k_cache, v_cache)
            hgrn_out.append(s_new)
            k_out.append(k_new)
            v_out.append(v_new)
        else:
            o = layer // 2
            x, s_new, sh_new, rkv = _odd_layer(
                x, batch, seqlen, o, layer, p, wget,
                st_shift[o] if has_state else None,
                st_rwkv,
                v_first)
            if o == 0:
                v_first = rkv
            rwkv_out.append(s_new)
            shift_out.append(sh_new)
        nl = layer + 1
        wanted = []
        if nl < DEPTH:
            wanted = [("up", nl), ("down", nl)]
            wanted += [("in", nl // 2), ("out", nl // 2)] if nl % 2 == 0 else [("rkv", nl // 2), ("wo", nl // 2)]
        jobs = [k for k in wanted
                if k not in bank and can_cast_in(nsteps, *src[k][0][0].shape[-2:])]
        x, cast = ffn(x, p["norm_ffn_pre"][layer], wget(("up", layer)), wget(("down", layer)),
                      p["norm_ffn_post"][layer], [src[k] for k in jobs])
        bank.update(zip(jobs, cast))
    return (x.reshape(batch, seqlen, d), jnp.stack(hgrn_out), jnp.stack(k_out), jnp.stack(v_out),
            jnp.stack(rwkv_out), jnp.stack(shift_out))


def kernel(x_prompt, x_sample, state_hgrn, cache_swa_k, cache_swa_v, state_rwkv, state_shift,
           norm_mix_pre, norm_mix_post, norm_ffn_pre, norm_ffn_post,
           w_in_even, w_out_even, hgrn_lb_raw, hgrn_norm_g, rel_bias, attn_sinks,
           rw_mu, rw_wr, rw_wk, rw_wv, rw_wo, rw_w0, rw_w1, rw_w2, rw_a0, rw_a1, rw_a2,
           rw_v0, rw_v1, rw_v2, rw_g1, rw_g2, rw_kk, rw_ka, rw_rk, rw_lnx_g, rw_lnx_b,
           w_up, w_down):
    p = {
        "norm_mix_pre": norm_mix_pre, "norm_mix_post": norm_mix_post,
        "norm_ffn_pre": norm_ffn_pre, "norm_ffn_post": norm_ffn_post,
        "w_in_even": w_in_even, "w_out_even": w_out_even,
        "hgrn_lb_raw": hgrn_lb_raw, "hgrn_norm_g": hgrn_norm_g,
        "rel_bias": rel_bias, "attn_sinks": attn_sinks,
        "rw_mu": rw_mu, "rw_wr": rw_wr, "rw_wk": rw_wk, "rw_wv": rw_wv, "rw_wo": rw_wo,
        "rw_w0": rw_w0, "rw_w1": rw_w1, "rw_w2": rw_w2, "rw_a0": rw_a0, "rw_a1": rw_a1, "rw_a2": rw_a2,
        "rw_v0": rw_v0, "rw_v1": rw_v1, "rw_v2": rw_v2, "rw_g1": rw_g1, "rw_g2": rw_g2,
        "rw_kk": rw_kk, "rw_ka": rw_ka, "rw_rk": rw_rk, "rw_lnx_g": rw_lnx_g, "rw_lnx_b": rw_lnx_b,
        "w_up": w_up, "w_down": w_down,
    }
    bank = {}
    y_p, hgrn_p, k_p, v_p, rwkv_p, shift_p = _trunk(x_prompt, None, None, None, None, None, p, bank)
    y_s, hgrn_s, k_s, v_s, rwkv_s, shift_s = _trunk(
        x_sample, state_hgrn, cache_swa_k, cache_swa_v, state_rwkv, state_shift, p, bank)
    return (y_p, y_s, hgrn_p, hgrn_s, k_p, k_s, v_p, v_s, rwkv_p, rwkv_s, shift_p, shift_s)
```

```python
import functools
import math

import numpy as np
import jax
import jax.numpy as jnp
from jax import lax
from jax.experimental import pallas as pl
from jax.experimental.pallas import tpu as pltpu

F32 = jnp.float32
BF16 = jnp.bfloat16

DEPTH = 4
N_EVEN = 2
N_ODD = 2
A_HEADS = 8
A_KDIM = 128
A_VDIM = 128
A_WIDTH = 1024
B_HEADS = 16
B_HEAD_DIM = 64
B_KV_HEADS = 4
B_GROUP = 4
B_WIDTH = 1024
B_KV_WIDTH = 256
WINDOW = 128
N_BUCKETS = 32
MAX_DISTANCE = 128
MASK_VALUE = -1e30
IN_A = 4096
IN_EVEN = 5632
C_HEAD = 64
C_HEADS = 32
GN_EPS = 64e-5
D_FF = 8192
NORM_EPS = 1e-6

LANES = 128
SUBLANES = 8
VMEM_LIMIT = 56 * 1024 * 1024

HGRN_CHUNK = 128
RWKV_CHUNK = 64
RWKV_TRI_BLOCK = 16


def _cparams(sem):
    return pltpu.CompilerParams(dimension_semantics=sem, vmem_limit_bytes=VMEM_LIMIT)


def _rms(x, g):
    return x * lax.rsqrt(jnp.mean(x * x, axis=-1, keepdims=True) + NORM_EPS) * g


def _dot(a, b):
    return jnp.dot(a.astype(BF16), b.astype(BF16), preferred_element_type=F32)


def _dot_nt(a, b):
    return lax.dot_general(a.astype(BF16), b.astype(BF16), (((1,), (1,)), ((), ())),
                           preferred_element_type=F32)


def _dot_tn(a, b):
    return lax.dot_general(a.astype(BF16), b.astype(BF16), (((0,), (0,)), ((), ())),
                           preferred_element_type=F32)


def _dot_shared(lhs_list, b):
    res = _dot(jnp.concatenate(lhs_list, axis=0), b)
    out, off = [], 0
    for a in lhs_list:
        out.append(res[off:off + a.shape[0]])
        off += a.shape[0]
    return out


def _pick_tile(m, cands):
    for c in cands:
        if m % c == 0:
            return c
    return m


def _norm_matmul_kernel(x_ref, g_ref, w_ref, o_ref, xn_ref):
    @pl.when(pl.program_id(1) == 0)
    def _():
        xn_ref[...] = _rms(x_ref[...], g_ref[...]).astype(BF16)

    o_ref[...] = jnp.dot(xn_ref[...], w_ref[...], preferred_element_type=F32)


def norm_matmul(x, g, w_bf16, tn=1408):
    m, d = x.shape
    n = w_bf16.shape[1]
    tm = _pick_tile(m, (1024, 512, 256, 128, 64, 32, 16, 8))
    nj = n // tn
    col = lambda i, j: jnp.where(i % 2 == 0, j, nj - 1 - j)
    return pl.pallas_call(
        _norm_matmul_kernel,
        grid=(m // tm, nj),
        in_specs=[pl.BlockSpec((tm, d), lambda i, j: (i, 0)),
                  pl.BlockSpec((1, d), lambda i, j: (0, 0)),
                  pl.BlockSpec((d, tn), lambda i, j: (0, col(i, j)))],
        out_specs=pl.BlockSpec((tm, tn), lambda i, j: (i, col(i, j))),
        out_shape=jax.ShapeDtypeStruct((m, n), F32),
        scratch_shapes=[pltpu.VMEM((tm, d), BF16)],
        compiler_params=_cparams(("parallel", "arbitrary")),
        name="norm_matmul",
    )(x, g.reshape(1, d), w_bf16)


def _ffn_kernel(*refs, cast_srcs):
    nsrc = sum(cast_srcs)
    x_ref, gpre_ref, wup_ref, wdn_ref, gpost_ref = refs[:5]
    src_refs = refs[5:5 + nsrc]
    o_ref = refs[5 + nsrc]
    cast_out = refs[6 + nsrc:6 + nsrc + len(cast_srcs)]
    xn_ref, acc_ref = refs[-2:]
    k = 0
    for out_ref, n in zip(cast_out, cast_srcs):
        for j in range(n):
            if n == 1:
                out_ref[...] = src_refs[k][...].astype(BF16)
            else:
                out_ref[j] = src_refs[k][...].astype(BF16)
            k += 1
    f = pl.program_id(1)

    @pl.when(f == 0)
    def _():
        xn_ref[...] = _rms(x_ref[...], gpre_ref[...]).astype(BF16)
        acc_ref[...] = jnp.zeros_like(acc_ref)

    h = jnp.dot(xn_ref[...], wup_ref[...], preferred_element_type=F32)
    h = jnp.square(jnp.maximum(h, 0.0)).astype(BF16)
    acc_ref[...] += jnp.dot(h, wdn_ref[...], preferred_element_type=F32)

    @pl.when(f == pl.num_programs(1) - 1)
    def _():
        o_ref[...] = x_ref[...] + _rms(acc_ref[...], gpost_ref[...])


BF16_SUBLANES = 16


def ffn_steps(m, dff, tf=1024):
    return (m // _pick_tile(m, (512, 256, 128, 64, 32, 16, 8))) * (dff // tf)


CAST_BLOCK_BYTES = 1 << 20


def can_cast_in(nsteps, rows, cols):
    return (rows % nsteps == 0 and (rows // nsteps) % BF16_SUBLANES == 0
            and (rows // nsteps) * cols * 4 <= CAST_BLOCK_BYTES)


def ffn(x, gpre, wup_bf16, wdn_bf16, gpost, casts=(), tf=1024):
    m, d = x.shape
    dff = wup_bf16.shape[1]
    tm = _pick_tile(m, (512, 256, 128, 64, 32, 16, 8))
    nf = dff // tf
    nsteps = (m // tm) * nf
    step = lambda i, f: i * nf + f
    in_specs = [pl.BlockSpec((tm, d), lambda i, f: (i, 0)),
                pl.BlockSpec((1, d), lambda i, f: (0, 0)),
                pl.BlockSpec((d, tf), lambda i, f: (0, f)),
                pl.BlockSpec((tf, d), lambda i, f: (f, 0)),
                pl.BlockSpec((1, d), lambda i, f: (0, 0))]
    args = [x, gpre.reshape(1, d), wup_bf16, wdn_bf16, gpost.reshape(1, d)]
    out_specs = [pl.BlockSpec((tm, d), lambda i, f: (i, 0))]
    out_shape = [jax.ShapeDtypeStruct((m, d), F32)]
    for job in casts:
        rows, cols = job[0][0].shape[len(job[0][1]):]
        assert can_cast_in(nsteps, rows, cols)
        rb = rows // nsteps
        for arr, lead in job:
            in_specs.append(pl.BlockSpec((None,) * len(lead) + (rb, cols),
                                         lambda i, f, lead=lead: lead + (step(i, f), 0)))
            args.append(arr)
        if len(job) == 1:
            out_specs.append(pl.BlockSpec((rb, cols), lambda i, f: (step(i, f), 0)))
            out_shape.append(jax.ShapeDtypeStruct((rows, cols), BF16))
        else:
            out_specs.append(pl.BlockSpec((len(job), rb, cols), lambda i, f: (0, step(i, f), 0)))
            out_shape.append(jax.ShapeDtypeStruct((len(job), rows, cols), BF16))
    outs = pl.pallas_call(
        functools.partial(_ffn_kernel, cast_srcs=tuple(len(job) for job in casts)),
        grid=(m // tm, nf),
        in_specs=in_specs,
        out_specs=out_specs,
        out_shape=out_shape,
        scratch_shapes=[pltpu.VMEM((tm, d), BF16), pltpu.VMEM((tm, d), F32)],
        compiler_params=_cparams(("parallel", "arbitrary")),
        name="ffn",
    )(*args)
    return outs[0], list(outs[1:])


def _even_out_kernel(oa_ref, ga_ref, ob_ref, x_ref, ag_ref, w_ref, gpost_ref, o_ref):
    ga = ga_ref[...]
    oan = _rms(oa_ref[...], ag_ref[...]) * (ga * jax.nn.sigmoid(ga))
    mix = (jnp.dot(oan.astype(BF16), w_ref[:A_WIDTH, :], preferred_element_type=F32)
           + jnp.dot(ob_ref[...].astype(BF16), w_ref[A_WIDTH:, :], preferred_element_type=F32))
    o_ref[...] = x_ref[...] + _rms(mix, gpost_ref[...])


def even_out(o_a, proj, o_b, x, a_norm_g, w_out_bf16, gpost):
    m, d = x.shape
    tm = _pick_tile(m, (512, 256, 128, 64, 32, 16, 8))
    ga_blk = 3
    return pl.pallas_call(
        _even_out_kernel,
        grid=(m // tm,),
        in_specs=[pl.BlockSpec((tm, A_WIDTH), lambda i: (i, 0)),
                  pl.BlockSpec((tm, A_WIDTH), lambda i: (i, ga_blk)),
                  pl.BlockSpec((tm, B_WIDTH), lambda i: (i, 0)),
                  pl.BlockSpec((tm, d), lambda i: (i, 0)),
                  pl.BlockSpec((1, A_WIDTH), lambda i: (0, 0)),
                  pl.BlockSpec((A_WIDTH + B_WIDTH, d), lambda i: (0, 0)),
                  pl.BlockSpec((1, d), lambda i: (0, 0))],
        out_specs=pl.BlockSpec((tm, d), lambda i: (i, 0)),
        out_shape=jax.ShapeDtypeStruct((m, d), F32),
        compiler_params=_cparams(("parallel",)),
        name="even_out",
    )(o_a, proj, o_b, x, a_norm_g.reshape(1, A_WIDTH), w_out_bf16, gpost.reshape(1, d))


def _odd_out_kernel(y_ref, x_ref, w_ref, gpost_ref, o_ref):
    mix = jnp.dot(y_ref[...].astype(BF16), w_ref[...], preferred_element_type=F32)
    o_ref[...] = x_ref[...] + _rms(mix, gpost_ref[...])


def odd_out(yg, x, wo_bf16, gpost):
    m, d = x.shape
    tm = _pick_tile(m, (512, 256, 128, 64, 32, 16, 8))
    return pl.pallas_call(
        _odd_out_kernel,
        grid=(m // tm,),
        in_specs=[pl.BlockSpec((tm, d), lambda i: (i, 0)),
                  pl.BlockSpec((tm, d), lambda i: (i, 0)),
                  pl.BlockSpec((d, d), lambda i: (0, 0)),
                  pl.BlockSpec((1, d), lambda i: (0, 0))],
        out_specs=pl.BlockSpec((tm, d), lambda i: (i, 0)),
        out_shape=jax.ShapeDtypeStruct((m, d), F32),
        compiler_params=_cparams(("parallel",)),
        name="odd_out",
    )(yg, x, wo_bf16, gpost.reshape(1, d))


def _level_consts(c):
    levels = []
    s = c // 2
    while s >= 1:
        levels.append(s)
        s //= 2
    mask = np.zeros((len(levels), c, c), np.float32)
    idx = np.arange(c)
    for l, s in enumerate(levels):
        same = (idx[:, None] // (2 * s)) == (idx[None, :] // (2 * s))
        upper = (idx[:, None] % (2 * s)) >= s
        lower = (idx[None, :] % (2 * s)) < s
        mask[l] = (same & upper & lower).astype(np.float32)
    return levels, mask


def _split_rows(g, s, rowid):
    c = g.shape[0]
    if 2 * s >= SUBLANES:
        return jnp.concatenate(
            [jnp.broadcast_to(g[b + s - 1:b + s, :], (2 * s, g.shape[1])) for b in range(0, c, 2 * s)], axis=0)
    r = rowid % (2 * s)
    out = g
    for off in range(-(s - 1), s + 1):
        if off != 0:
            out = jnp.where(r == s - 1 + off, pltpu.roll(g, off % c, 0), out)
    return out


def _hgrn_kernel(*refs, layer, chunk, nchunks, levels, has_state, nheads):
    if has_state:
        q_ref, f_ref, i_ref, lb_ref, mask_ref, s0_ref, o_ref, s_ref, st_ref = refs
    else:
        q_ref, f_ref, i_ref, lb_ref, mask_ref, o_ref, s_ref, st_ref = refs
    c = chunk
    nh = nheads
    rowid = lax.broadcasted_iota(jnp.int32, (c, nh * LANES), 0)
    l_idx = pl.program_id(2)

    @pl.when(l_idx == 0)
    def _():
        for hi in range(nh):
            if has_state:
                st_ref[hi] = s0_ref[0, hi].T
            else:
                st_ref[hi] = jnp.zeros((A_VDIM, A_KDIM), F32)

    lbr = lb_ref[...]
    e = jnp.exp(lbr - jnp.max(lbr, axis=0, keepdims=True))
    p = e / jnp.sum(e, axis=0, keepdims=True)
    lb = jnp.zeros((1, nh * LANES), F32)
    for i in range(1, layer + 1):
        lb = lb + p[i:i + 1, :]
    one_m_lb = 1.0 - lb
    head = lambda x, hi: x[:, hi * LANES:(hi + 1) * LANES]

    def body(ci, carry):
        rows = pl.ds(pl.multiple_of(ci * c, c), c)
        fq = f_ref[rows, :]
        qr = q_ref[rows, :]
        v = i_ref[rows, :]
        f = lb + one_m_lb * jax.nn.sigmoid(fq)
        k = 1.0 - f
        q = qr * jax.nn.sigmoid(qr) * (A_KDIM ** -0.5)

        g = jnp.log2(f)
        sft = 1
        while sft < c:
            g = g + jnp.where(rowid >= sft, pltpu.roll(g, sft, 0), 0.0)
            sft *= 2
        glast = g[c - 1:c, :]
        q_in = q * jnp.exp2(g)
        kd = k * jnp.exp2(glast - g)
        dec = jnp.exp2(glast)
        diag = q * k

        sts = [st_ref[hi] for hi in range(nh)]
        os_ = [_dot_nt(head(q_in, hi), sts[hi]) for hi in range(nh)]
        attns = [jnp.zeros((c, c), F32) for _ in range(nh)]
        for l, s in enumerate(levels):
            e = jnp.exp2(-jnp.abs(g - _split_rows(g, s, rowid)))
            qs = q * e
            ks = k * e
            ml = mask_ref[l]
            attns = [at + ml * _dot_nt(head(qs, hi), head(ks, hi)) for hi, at in enumerate(attns)]
        for hi in range(nh):
            vh = head(v, hi)
            o = os_[hi] + _dot(attns[hi], vh) + jnp.sum(head(diag, hi), axis=-1, keepdims=True) * vh
            o_ref[rows, hi * LANES:(hi + 1) * LANES] = o
        for hi in range(nh):
            st_ref[hi] = sts[hi] * head(dec, hi) + _dot_tn(head(v, hi), head(kd, hi))
        return carry

    lax.fori_loop(0, nchunks, body, 0, unroll=2 if nchunks % 2 == 0 else 1)

    @pl.when(l_idx == pl.num_programs(2) - 1)
    def _():
        for hi in range(nh):
            s_ref[0, hi] = st_ref[hi].T


def hgrn(proj, lb_raw, layer, batch, seqlen, s0=None, nheads=A_HEADS):
    m = proj.shape[0]
    c = math.gcd(seqlen, HGRN_CHUNK)
    lblk = _pick_tile(seqlen, (512, 256, 128, 64, 32, 16, 8))
    nl_blocks = seqlen // lblk
    levels, mask = _level_consts(c)
    has_state = s0 is not None
    if c * 4 <= HGRN_CHUNK:
        nheads = A_HEADS
    w = nheads * LANES
    ngroups = A_HEADS // nheads
    kern = functools.partial(_hgrn_kernel, layer=layer, chunk=c, nchunks=lblk // c,
                             levels=tuple(levels), has_state=has_state, nheads=nheads)
    row = lambda b, h, l: b * nl_blocks + l
    in_specs = [pl.BlockSpec((lblk, w), lambda b, h, l: (row(b, h, l), h)),
                pl.BlockSpec((lblk, w), lambda b, h, l: (row(b, h, l), ngroups + h)),
                pl.BlockSpec((lblk, w), lambda b, h, l: (row(b, h, l), 2 * ngroups + h)),
                pl.BlockSpec((N_EVEN, w), lambda b, h, l: (0, h)),
                pl.BlockSpec((len(levels), c, c), lambda b, h, l: (0, 0, 0))]
    args = [proj, proj, proj, lb_raw, jnp.asarray(mask)]
    st_blk = pl.BlockSpec((1, nheads, A_KDIM, A_VDIM), lambda b, h, l: (b, h, 0, 0))
    if has_state:
        in_specs.append(pl.BlockSpec((None, 1, nheads, A_KDIM, A_VDIM), lambda b, h, l: (layer, b, h, 0, 0)))
        args.append(s0)
    return pl.pallas_call(
        kern,
        grid=(batch, ngroups, nl_blocks),
        in_specs=in_specs,
        out_specs=[pl.BlockSpec((lblk, w), lambda b, h, l: (row(b, h, l), h)), st_blk],
        out_shape=[jax.ShapeDtypeStruct((m, A_WIDTH), F32),
                   jax.ShapeDtypeStruct((batch, A_HEADS, A_KDIM, A_VDIM), F32)],
        scratch_shapes=[pltpu.VMEM((nheads, A_VDIM, A_KDIM), F32)],
        compiler_params=_cparams(("parallel", "parallel", "arbitrary")),
        name="hgrn",
    )(*args)


def _t5_bucket(dist):
    max_exact = N_BUCKETS // 2
    d = np.maximum(dist, 0)
    large = max_exact + (np.log(np.maximum(d, max_exact).astype(np.float32) / max_exact)
                         / math.log(MAX_DISTANCE / max_exact) * (N_BUCKETS - max_exact)).astype(np.int32)
    large = np.minimum(large, N_BUCKETS - 1)
    return np.where(d < max_exact, d, large).astype(np.int32)


def _swa_kernel(q_ref, kp_ref, kc_ref, vp_ref, vc_ref, bucket_ref, band_ref, rb_ref, sink_ref,
                o_ref, bias_ref, *, qb, span, prev_always_valid):
    first = (pl.program_id(0) == 0) & (pl.program_id(1) == 0)

    @pl.when(first)
    def _():
        bk = bucket_ref[...]
        band = band_ref[...]

        def per_head(h, carry):
            def per_bucket(bi, acc):
                return jnp.where(bk == bi, rb_ref[bi, h], acc)
            acc = lax.fori_loop(0, N_BUCKETS, per_bucket, jnp.zeros((qb, span), F32))
            bias_ref[h] = jnp.where(band > 0, acc, MASK_VALUE)
            return carry

        lax.fori_loop(0, B_HEADS, per_head, 0)

    scale = B_HEAD_DIM ** -0.5
    q = q_ref[...]
    kall = jnp.concatenate([kp_ref[...], kc_ref[...]], axis=0)
    vall = jnp.concatenate([vp_ref[...], vc_ref[...]], axis=0)
    if not prev_always_valid:
        col = lax.broadcasted_iota(jnp.int32, (qb, span), 1)
        no_prev = (col < WINDOW) & (pl.program_id(1) == 0)
    heads = range(B_HEADS)
    ks = [kall[:, kh * B_HEAD_DIM:(kh + 1) * B_HEAD_DIM].astype(BF16) for kh in range(B_KV_HEADS)]
    vs = [vall[:, kh * B_HEAD_DIM:(kh + 1) * B_HEAD_DIM].astype(BF16) for kh in range(B_KV_HEADS)]
    qs = [(q[:, h * B_HEAD_DIM:(h + 1) * B_HEAD_DIM] * scale).astype(BF16) for h in heads]
    ss = [_dot_nt(qs[h], ks[h // B_GROUP]) + bias_ref[h] for h in heads]
    if not prev_always_valid:
        ss = [jnp.where(no_prev, MASK_VALUE, s) for s in ss]
    ms = [jnp.maximum(jnp.max(ss[h], axis=-1, keepdims=True), sink_ref[h]) for h in heads]
    ps = [jnp.exp(s - m) for s, m in zip(ss, ms)]
    denoms = [jnp.sum(ps[h], axis=-1, keepdims=True) + jnp.exp(sink_ref[h] - ms[h]) for h in heads]
    outs = [_dot(ps[h], vs[h // B_GROUP]) / denoms[h] for h in heads]
    o_ref[...] = jnp.concatenate(outs, axis=1).astype(o_ref.dtype)


def swa(proj, batch, seqlen, rel_bias, sinks, layer, k_past=None, v_past=None):
    m = proj.shape[0]
    has_cache = k_past is not None
    qb = math.gcd(seqlen, WINDOW)
    nb = seqlen // qb
    span = WINDOW + qb
    dist = np.arange(qb)[:, None] + WINDOW - np.arange(span)[None, :]
    band = ((dist >= 0) & (dist < WINDOW)).astype(np.float32)
    bucket = _t5_bucket(dist)
    q_col = IN_A // B_WIDTH
    k_col = (IN_A + B_WIDTH) // B_KV_WIDTH
    v_col = k_col + 1
    cur = lambda c: (lambda b, n: (b * nb + n, c))
    if has_cache:
        assert nb == 1
        prev_k = pl.BlockSpec((None, WINDOW, B_KV_WIDTH), lambda b, n: (layer, b, 0))
        prev_v = pl.BlockSpec((None, WINDOW, B_KV_WIDTH), lambda b, n: (layer, b, 0))
        kp_arr = k_past.reshape(k_past.shape[0], batch * WINDOW, B_KV_WIDTH)
        vp_arr = v_past.reshape(v_past.shape[0], batch * WINDOW, B_KV_WIDTH)
    else:
        assert qb == WINDOW
        prev = lambda c: (lambda b, n: (b * nb + jnp.maximum(n - 1, 0), c))
        prev_k = pl.BlockSpec((WINDOW, B_KV_WIDTH), prev(k_col))
        prev_v = pl.BlockSpec((WINDOW, B_KV_WIDTH), prev(v_col))
        kp_arr, vp_arr = proj, proj
    kern = functools.partial(_swa_kernel, qb=qb, span=span, prev_always_valid=has_cache)
    return pl.pallas_call(
        kern,
        grid=(batch, nb),
        in_specs=[pl.BlockSpec((qb, B_WIDTH), cur(q_col)),
                  prev_k,
                  pl.BlockSpec((qb, B_KV_WIDTH), cur(k_col)),
                  prev_v,
                  pl.BlockSpec((qb, B_KV_WIDTH), cur(v_col)),
                  pl.BlockSpec((qb, span), lambda b, n: (0, 0)),
                  pl.BlockSpec((qb, span), lambda b, n: (0, 0)),
                  pl.BlockSpec(memory_space=pltpu.SMEM),
                  pl.BlockSpec(memory_space=pltpu.SMEM)],
        out_specs=pl.BlockSpec((qb, B_WIDTH), lambda b, n: (b * nb + n, 0)),
        out_shape=jax.ShapeDtypeStruct((m, B_WIDTH), BF16 if qb % 16 == 0 else F32),
        scratch_shapes=[pltpu.VMEM((B_HEADS, qb, span), F32)],
        compiler_params=_cparams(("arbitrary", "arbitrary")),
        name="swa",
    )(proj, kp_arr, proj, vp_arr, proj, jnp.asarray(bucket), jnp.asarray(band), rel_bias, sinks)


def _rmsnorm_kernel(x_ref, g_ref, o_ref):
    o_ref[...] = _rms(x_ref[...], g_ref[...])


def rmsnorm_rows(x, g):
    m, d = x.shape
    tm = _pick_tile(m, (512, 256, 128, 64, 32, 16, 8))
    return pl.pallas_call(
        _rmsnorm_kernel,
        grid=(m // tm,),
        in_specs=[pl.BlockSpec((tm, d), lambda i: (i, 0)), pl.BlockSpec((1, d), lambda i: (0, 0))],
        out_specs=pl.BlockSpec((tm, d), lambda i: (i, 0)),
        out_shape=jax.ShapeDtypeStruct((m, d), F32),
        compiler_params=_cparams(("parallel",)),
        name="rmsnorm",
    )(x, g.reshape(1, d))


LORA_W = (0, 128)
LORA_A = (128, 256)
LORA_G = (256, 512)
LORA_V = (512, 640)


def _rwkv_in_phase(i, step):
    return jnp.where(i % 2 == 0, step, 3 - step)


def _rwkv_in_kernel(x_ref, xp_ref, s_ref, g_ref, mu3_ref, mul_ref, w_ref, w1_ref,
                    rkv_ref, mid_ref, h_s, hp_s, *, tm, seqlen, has_vres):
    step = pl.program_id(1)
    ph = _rwkv_in_phase(pl.program_id(0), step)

    @pl.when(step == 0)
    def _():
        g = g_ref[...]
        h = _rms(x_ref[...], g)
        rowid = lax.broadcasted_iota(jnp.int32, h.shape, 0)
        rolled = pltpu.roll(h, 1, 0)
        if seqlen % tm == 0:
            prev_last = _rms(xp_ref[...], g)[SUBLANES - 1:SUBLANES, :]
            at_start = pl.program_id(0) % (seqlen // tm) == 0
            first = jnp.where(at_start, s_ref[...], prev_last)
            hp = jnp.where(rowid == 0, first, rolled)
        else:
            hp = jnp.where(rowid % seqlen == 0, s_ref[...], rolled)
        h_s[...] = h
        hp_s[...] = hp

    @pl.when(ph < 3)
    def _():
        h = h_s[...]
        xm = (h + (hp_s[...] - h) * mu3_ref[0]).astype(BF16)
        rkv_ref[0] = jnp.dot(xm, w_ref[...], preferred_element_type=F32)

    @pl.when(ph == 3)
    def _():
        h = h_s[...]
        xx = hp_s[...] - h
        mix = lambda i: (h + xx * mul_ref[i:i + 1, :]).astype(BF16)
        low = lambda i, rng: jnp.dot(mix(i), w1_ref[:, rng[0]:rng[1]], preferred_element_type=F32)
        parts = [jnp.tanh(low(0, LORA_W)), low(1, LORA_A), jax.nn.sigmoid(low(2, LORA_G))]
        if has_vres:
            parts.append(low(3, LORA_V))
        mid_ref[...] = jnp.concatenate(parts, axis=1).astype(BF16)


def rwkv_in(x, g, shift0, batch, seqlen, mu, w3_bf16, w1cat_bf16, has_vres):
    m, d = x.shape
    tm = next(t for t in (512, 256, 128, 64, 32, 16, 8)
              if m % t == 0 and (seqlen % t == 0 or t % seqlen == 0))
    midw = w1cat_bf16.shape[1]
    mu_rkv = jnp.stack([mu[0], mu[2], mu[3]])[:, None, :]
    mu_low = jnp.stack([mu[1], mu[4], mu[5], mu[3]])
    if shift0 is None:
        shift0 = jnp.zeros((batch, d), F32)
    if seqlen % tm == 0:
        srow = shift0[:, None, :]
        tps = seqlen // tm
        s_spec = pl.BlockSpec((None, 1, d), lambda i, p: (i // tps, 0, 0))
    else:
        srow = jnp.repeat(shift0, seqlen, axis=0)
        s_spec = pl.BlockSpec((tm, d), lambda i, p: (i, 0))
    sub = tm // SUBLANES
    proj_of = lambda i, p: jnp.minimum(_rwkv_in_phase(i, p), 2)
    kern = functools.partial(_rwkv_in_kernel, tm=tm, seqlen=seqlen, has_vres=has_vres)
    return pl.pallas_call(
        kern,
        grid=(m // tm, 4),
        in_specs=[pl.BlockSpec((tm, d), lambda i, p: (i, 0)),
                  pl.BlockSpec((SUBLANES, d), lambda i, p: (jnp.maximum(i * sub - 1, 0), 0)),
                  s_spec,
                  pl.BlockSpec((1, d), lambda i, p: (0, 0)),
                  pl.BlockSpec((1, 1, d), lambda i, p: (proj_of(i, p), 0, 0)),
                  pl.BlockSpec((4, d), lambda i, p: (0, 0)),
                  pl.BlockSpec((None, d, d), lambda i, p: (proj_of(i, p), 0, 0)),
                  pl.BlockSpec((d, midw), lambda i, p: (0, 0))],
        out_specs=[pl.BlockSpec((1, tm, d), lambda i, p: (proj_of(i, p), i, 0)),
                   pl.BlockSpec((tm, midw), lambda i, p: (i, 0))],
        out_shape=[jax.ShapeDtypeStruct((3, m, d), F32),
                   jax.ShapeDtypeStruct((m, midw), BF16)],
        scratch_shapes=[pltpu.VMEM((tm, d), F32), pltpu.VMEM((tm, d), F32)],
        compiler_params=_cparams(("parallel", "arbitrary")),
        name="rwkv_in",
    )(x, x, srow, g.reshape(1, d), mu_rkv, mu_low, w3_bf16, w1cat_bf16)


def _pad_lora(w1, w2, width):
    r = w1.shape[1]
    return (jnp.pad(w1, ((0, 0), (0, width - r))).astype(BF16),
            jnp.pad(w2, ((0, width - r), (0, 0))).astype(BF16))


def _rwkv_consts(c):
    i = np.arange(2 * c)
    same = (i[:, None] // c) == (i[None, :] // c)
    strict = (same & ((i[:, None] % c) > (i[None, :] % c))).astype(np.float32)
    incl = (same & ((i[:, None] % c) >= (i[None, :] % c))).astype(np.float32)
    eye = np.eye(2 * c, dtype=np.float32)
    l = np.arange(LANES)
    headones = ((l[:, None] // C_HEAD) == (l[None, :] // C_HEAD)).astype(np.float32)
    bs = min(RWKV_TRI_BLOCK, c)
    blk = lambda n: (i[:, None] // n) == (i[None, :] // n)
    tmasks = [blk(bs)]
    s = bs
    while s < c:
        tmasks.append(blk(2 * s) & ~blk(s))
        s *= 2
    return strict, incl, eye, headones, np.stack(tmasks).astype(np.float32), bs


def _rwkv_core_kernel(*refs, chunk, nchunks, npairs, has_state, has_vres, bs):
    it = iter(refs)
    r_ref, k_ref, v_ref, mid_ref = (next(it) for _ in range(4))
    w0_ref, w2_ref, a0_ref, a2_ref, g2_ref = (next(it) for _ in range(5))
    if has_vres:
        vf_ref, v0_ref, v2_ref = next(it), next(it), next(it)
    kk_ref, ka_ref, rk_ref, lg_ref, lb_ref = (next(it) for _ in range(5))
    strict_ref, incl_ref, eye_ref, hones_ref, tmask_ref = (next(it) for _ in range(5))
    if has_state:
        s0_ref = next(it)
    y_ref, s_ref, st_ref, wl_ref, a_ref, g_ref = (next(it) for _ in range(6))
    if has_vres:
        vg_ref = next(it)
    c = chunk
    l_idx = pl.program_id(2)
    lane = lax.broadcasted_iota(jnp.int32, (1, LANES), 1)
    m0 = (lane < C_HEAD).astype(F32)
    m1 = 1.0 - m0

    low = lambda rng, w_ref: jnp.dot(mid_ref[:, rng[0]:rng[1]], w_ref[...], preferred_element_type=F32)
    sig = lambda z: 0.5 * jnp.tanh(0.5 * z) + 0.5
    wl_ref[...] = -math.exp(-0.5) * sig(w0_ref[...] + low(LORA_W, w2_ref))
    a_ref[...] = sig(a0_ref[...] + low(LORA_A, a2_ref))
    g_ref[...] = low(LORA_G, g2_ref)
    if has_vres:
        vg_ref[...] = sig(v0_ref[...] + low(LORA_V, v2_ref))

    @pl.when(l_idx == 0)
    def _():
        for pi in range(npairs):
            if has_state:
                z = jnp.zeros((C_HEAD, C_HEAD), F32)
                top = jnp.concatenate([s0_ref[0, 2 * pi], z], axis=1)
                bot = jnp.concatenate([z, s0_ref[0, 2 * pi + 1]], axis=1)
                st_ref[pi] = jnp.concatenate([top, bot], axis=0)
            else:
                st_ref[pi] = jnp.zeros((LANES, LANES), F32)

    strict = strict_ref[...]
    incl = incl_ref[...]
    eye = eye_ref[...]

    def stack(x):
        return jnp.concatenate([x * m0, x * m1], axis=0)

    first_head = lax.broadcasted_iota(jnp.int32, (c, LANES), 1) < C_HEAD
    rowid = lax.broadcasted_iota(jnp.int32, (c, LANES), 0)

    def rowsums(xs):
        return [jnp.where(first_head,
                          jnp.sum(x * m0, axis=-1, keepdims=True),
                          jnp.sum(x * m1, axis=-1, keepdims=True)) for x in xs]

    def cumsum_rows(x):
        s = 1
        while s < c:
            x = x + jnp.where(rowid >= s, pltpu.roll(x, s, 0), 0.0)
            s *= 2
        return x

    def load(pi, rows):
        cols = slice(pi * LANES, (pi + 1) * LANES)
        k = k_ref[0, rows, cols]
        v = v_ref[0, rows, cols]
        a = a_ref[rows, cols]
        if has_vres:
            v = v + (vf_ref[0, rows, cols] - v) * vg_ref[rows, cols]
        return dict(cols=cols, r=r_ref[0, rows, cols], v=v, a=a, wl=wl_ref[rows, cols],
                    kr=k * kk_ref[:, cols], kh=k * (1.0 + (a - 1.0) * ka_ref[:, cols]))

    def decays(p, ss):
        kk = p["kr"] * lax.rsqrt(jnp.maximum(ss, 1e-24))
        b = kk * p["a"]
        gc = cumsum_rows(p["wl"])
        gl = gc[c - 1:c, :]
        e_neg = jnp.exp(-gc)
        e_out = jnp.exp(gl - gc)
        p.update(gl=gl, ab=-kk * jnp.exp(gc - p["wl"]), rb=p["r"] * jnp.exp(gc),
                 bt=b * e_neg, kt=p["kh"] * e_neg, bh=b * e_out, khat=p["kh"] * e_out)

    def intra(p):
        lhs = jnp.concatenate([stack(p["ab"]), stack(p["rb"])], axis=0)
        with_b = _dot_nt(lhs, jnp.concatenate([p["bt"], p["bt"]], axis=0))
        with_k = _dot_nt(lhs, jnp.concatenate([p["kt"], p["kt"]], axis=0))
        p.update(a_ab=with_b[:2 * c] * strict, a_rb=with_b[2 * c:] * incl,
                 a_ak=with_k[:2 * c] * strict, a_rk=with_k[2 * c:] * incl)

    def body(ci, carry):
        rows = pl.ds(pl.multiple_of(ci * c, c), c)
        ps = [load(pi, rows) for pi in range(npairs)]
        for p, ss in zip(ps, rowsums([p["kr"] * p["kr"] for p in ps])):
            decays(p, ss)
        for p in ps:
            intra(p)

        pws = [p["a_ab"] * tmask_ref[0] for p in ps]
        ts = [eye + pw for pw in pws]
        if bs > 2:
            pws = [_dot(pw, pw) for pw in pws]
            n = 2
            while 2 * n < bs:
                res = [_dot_shared([t, pw], pw) for t, pw in zip(ts, pws)]
                ts = [t + r[0] for t, r in zip(ts, res)]
                pws = [r[1] for r in res]
                n *= 2
            ts = [t + _dot(t, pw) for t, pw in zip(ts, pws)]
        s, lvl = bs, 1
        while s < c:
            ms = [_dot(t, p["a_ab"] * tmask_ref[lvl]) for t, p in zip(ts, ps)]
            ts = [t + _dot(m, t) for t, m in zip(ts, ms)]
            s, lvl = 2 * s, lvl + 1

        sts = [st_ref[pi] for pi in range(npairs)]
        fss = [_dot_nt(jnp.concatenate([p["ab"], p["rb"]], axis=0), st) for p, st in zip(ps, sts)]
        vss = [stack(p["v"]) for p in ps]
        rhss = [stack(fs[:c]) + _dot(p["a_ak"], vs) for p, fs, vs in zip(ps, fss, vss)]
        uss = [_dot(t, rhs) for t, rhs in zip(ts, rhss)]
        yss = [_dot(p["a_rb"], us) + _dot(p["a_rk"], vs) for p, us, vs in zip(ps, uss, vss)]
        for pi, (p, st, us) in enumerate(zip(ps, sts, uss)):
            u = us[:c] + us[c:]
            upd = _dot_tn(jnp.concatenate([u, p["v"]], axis=0),
                          jnp.concatenate([p["bh"], p["khat"]], axis=0))
            st_ref[pi] = st * jnp.exp(p["gl"]) + upd * hones_ref[...]
        inv_n = 1.0 / C_HEAD
        ys_ = [fs[c:] + ys[:c] + ys[c:] for fs, ys in zip(fss, yss)]
        sums = rowsums(ys_ + [p["r"] * p["kh"] * rk_ref[:, p["cols"]] for p in ps])
        dlts = [y - m * inv_n for y, m in zip(ys_, sums[:npairs])]
        vars_ = rowsums([d * d for d in dlts])
        for p, dlt, var, bsum in zip(ps, dlts, vars_, sums[npairs:]):
            cols = p["cols"]
            yn = dlt * lax.rsqrt(var * inv_n + GN_EPS) * lg_ref[:, cols] + lb_ref[:, cols]
            y_ref[rows, cols] = ((yn + bsum * p["v"]) * g_ref[rows, cols]).astype(y_ref.dtype)
        return carry

    lax.fori_loop(0, nchunks, body, 0)

    @pl.when(l_idx == pl.num_programs(2) - 1)
    def _():
        for pi in range(npairs):
            st = st_ref[pi]
            s_ref[0, 2 * pi] = st[:C_HEAD, :C_HEAD]
            s_ref[0, 2 * pi + 1] = st[C_HEAD:, C_HEAD:]


def rwkv_core(rkv, mid, low2, kk_p, ka_p, rk_p, lnx_g, lnx_b, batch, seqlen, layer,
              s0=None, v_first=None):
    _, m, d = rkv.shape
    c = math.gcd(seqlen, RWKV_CHUNK)
    npairs = C_HEADS // 2
    lblk = _pick_tile(seqlen, (256, 128, 64, 32, 16, 8))
    nl_blocks = seqlen // lblk
    has_state = s0 is not None
    has_vres = v_first is not None
    w = npairs * LANES
    ngroups = d // w
    midw = mid.shape[1]
    *consts, bs = _rwkv_consts(c)
    strict, incl, eye, hones, tmasks = (jnp.asarray(x) for x in consts)
    row = lambda b, p, l: b * nl_blocks + l
    blk3 = lambda which: pl.BlockSpec((1, lblk, w), lambda b, p, l: (which, row(b, p, l), p))
    blk2 = pl.BlockSpec((lblk, w), lambda b, p, l: (row(b, p, l), p))
    par = pl.BlockSpec((1, w), lambda b, p, l: (0, p))
    cols = lambda arr: pl.BlockSpec((arr.shape[0], w), lambda b, p, l: (0, p))
    full = lambda arr: pl.BlockSpec(arr.shape, lambda b, p, l: (0,) * arr.ndim)
    w0, w2, a0, a2, g2 = low2[:5]
    in_specs = [blk3(0), blk3(1), blk3(2), pl.BlockSpec((lblk, midw), lambda b, p, l: (row(b, p, l), 0)),
                par, cols(w2), par, cols(a2), cols(g2)]
    args = [rkv, rkv, rkv, mid, w0.reshape(1, d), w2, a0.reshape(1, d), a2, g2]
    if has_vres:
        v0, v2 = low2[5:]
        in_specs += [blk3(2), par, cols(v2)]
        args += [v_first, v0.reshape(1, d), v2]
    in_specs += [par] * 5
    args += [x.reshape(1, d) for x in (kk_p, ka_p, rk_p, lnx_g, lnx_b)]
    in_specs += [full(x) for x in (strict, incl, eye, hones, tmasks)]
    args += [strict, incl, eye, hones, tmasks]
    st_blk = pl.BlockSpec((1, 2 * npairs, C_HEAD, C_HEAD), lambda b, p, l: (b, p, 0, 0))
    if has_state:
        in_specs.append(pl.BlockSpec((None, 1, 2 * npairs, C_HEAD, C_HEAD),
                                     lambda b, p, l: (layer, b, p, 0, 0)))
        args.append(s0)
    kern = functools.partial(_rwkv_core_kernel, chunk=c, nchunks=lblk // c, npairs=npairs,
                             has_state=has_state, has_vres=has_vres, bs=bs)
    return pl.pallas_call(
        kern,
        grid=(batch, ngroups, nl_blocks),
        in_specs=in_specs,
        out_specs=[blk2, st_blk],
        out_shape=[jax.ShapeDtypeStruct((m, d), BF16 if lblk % 16 == 0 else F32),
                   jax.ShapeDtypeStruct((batch, C_HEADS, C_HEAD, C_HEAD), F32)],
        scratch_shapes=[pltpu.VMEM((npairs, LANES, LANES), F32)]
        + [pltpu.VMEM((lblk, w), F32)] * (4 if has_vres else 3),
        compiler_params=_cparams(("parallel", "parallel", "arbitrary")),
        name="rwkv_core",
    )(*args)


def _weight_sources(p):
    src = {}
    for l in range(DEPTH):
        src[("up", l)] = [(p["w_up"], (l,))]
        src[("down", l)] = [(p["w_down"], (l,))]
    for e in range(N_EVEN):
        src[("in", e)] = [(p["w_in_even"], (e,))]
        src[("out", e)] = [(p["w_out_even"], (e,))]
    for o in range(N_ODD):
        src[("rkv", o)] = [(p["rw_wr"], (o,)), (p["rw_wk"], (o,)), (p["rw_wv"], (o,))]
        src[("wo", o)] = [(p["rw_wo"], (o,))]
    return src


def _bf16_weight(bank, src, key):
    if key not in bank:
        parts = [arr[lead].astype(BF16) for arr, lead in src[key]]
        bank[key] = parts[0] if len(parts) == 1 else jnp.stack(parts)
    return bank[key]


def _even_layer(x, batch, seqlen, e, layer, p, wget, st_hgrn, k_cache, v_cache):
    proj = norm_matmul(x, p["norm_mix_pre"][layer], wget(("in", e)))
    o_a, s_new = hgrn(proj, p["hgrn_lb_raw"], e, batch, seqlen, st_hgrn)
    k_lo = IN_A + B_WIDTH
    new_rows = min(seqlen, WINDOW)
    tails = jnp.stack([proj[(b + 1) * seqlen - new_rows:(b + 1) * seqlen, k_lo:] for b in range(batch)])
    kb = tails[:, :, :B_KV_WIDTH].reshape(batch, new_rows, B_KV_HEADS, B_HEAD_DIM)
    vb = tails[:, :, B_KV_WIDTH:].reshape(batch, new_rows, B_KV_HEADS, B_HEAD_DIM)
    o_b = swa(proj, batch, seqlen, p["rel_bias"], p["attn_sinks"][e], e, k_cache, v_cache)
    if k_cache is None:
        k_new, v_new = kb, vb
    else:
        k_new = jnp.concatenate([k_cache[e, :, new_rows:], kb], axis=1)
        v_new = jnp.concatenate([v_cache[e, :, new_rows:], vb], axis=1)
    x = even_out(o_a, proj, o_b, x, p["hgrn_norm_g"][e], wget(("out", e)), p["norm_mix_post"][layer])
    return x, s_new, k_new, v_new


def _odd_layer(x, batch, seqlen, o, layer, p, wget, shift0, s0, v_first):
    m, d = x.shape
    g_pre = p["norm_mix_pre"][layer]
    has_vres = o > 0
    width = lambda rng: rng[1] - rng[0]
    w1p, w2p = _pad_lora(p["rw_w1"][o], p["rw_w2"][o], width(LORA_W))
    a1p, a2p = _pad_lora(p["rw_a1"][o], p["rw_a2"][o], width(LORA_A))
    g1p, g2p = _pad_lora(p["rw_g1"][o], p["rw_g2"][o], width(LORA_G))
    first, low2 = [w1p, a1p, g1p], [p["rw_w0"][o], w2p, p["rw_a0"][o], a2p, g2p]
    if has_vres:
        v1p, v2p = _pad_lora(p["rw_v1"][o - 1], p["rw_v2"][o - 1], width(LORA_V))
        first.append(v1p)
        low2 += [p["rw_v0"][o - 1], v2p]
    rkv, mid = rwkv_in(x, g_pre, shift0, batch, seqlen, p["rw_mu"][o], wget(("rkv", o)),
                       jnp.concatenate(first, axis=1), has_vres)
    yg, s_new = rwkv_core(rkv, mid, low2, p["rw_kk"][o], p["rw_ka"][o], p["rw_rk"][o],
                          p["rw_lnx_g"][o], p["rw_lnx_b"][o], batch, seqlen, o, s0,
                          v_first if has_vres else None)
    shift_new = rmsnorm_rows(x.reshape(batch, seqlen, d)[:, -1], g_pre)
    x = odd_out(yg, x, wget(("wo", o)), p["norm_mix_post"][layer])
    return x, s_new, shift_new, rkv


def _trunk(x3, st_hgrn, k_cache, v_cache, st_rwkv, st_shift, p, bank):
    batch, seqlen, d = x3.shape
    x = x3.reshape(batch * seqlen, d)
    has_state = st_hgrn is not None
    hgrn_out, k_out, v_out, rwkv_out, shift_out = [], [], [], [], []
    v_first = None
    src = _weight_sources(p)
    wget = functools.partial(_bf16_weight, bank, src)
    nsteps = ffn_steps(batch * seqlen, D_FF)
    for layer in range(DEPTH):
        if layer % 2 == 0:
            e = layer // 2
            x, s_new, k_new, v_new = _even_layer(x, batch, seqlen, e, layer, p, wget, st_hgrn, k_cache, v_cache)
            hgrn_out.append(s_new)
            k_out.append(k_new)
            v_out.append(v_new)
        else:
            o = layer // 2
            x, s_new, sh_new, rkv = _odd_layer(
                x, batch, seqlen, o, layer, p, wget,
                st_shift[o] if has_state else None,
                st_rwkv,
                v_first)
            if o == 0:
                v_first = rkv
            rwkv_out.append(s_new)
            shift_out.append(sh_new)
        nl = layer + 1
        wanted = []
        if nl < DEPTH:
            wanted = [("up", nl), ("down", nl)]
            wanted += [("in", nl // 2), ("out", nl // 2)] if nl % 2 == 0 else [("rkv", nl // 2), ("wo", nl // 2)]
        jobs = [k for k in wanted
                if k not in bank and can_cast_in(nsteps, *src[k][0][0].shape[-2:])]
        x, cast = ffn(x, p["norm_ffn_pre"][layer], wget(("up", layer)), wget(("down", layer)),
                      p["norm_ffn_post"][layer], [src[k] for k in jobs])
        bank.update(zip(jobs, cast))
    return (x.reshape(batch, seqlen, d), jnp.stack(hgrn_out), jnp.stack(k_out), jnp.stack(v_out),
            jnp.stack(rwkv_out), jnp.stack(shift_out))


def kernel(x_prompt, x_sample, state_hgrn, cache_swa_k, cache_swa_v, state_rwkv, state_shift,
           norm_mix_pre, norm_mix_post, norm_ffn_pre, norm_ffn_post,
           w_in_even, w_out_even, hgrn_lb_raw, hgrn_norm_g, rel_bias, attn_sinks,
           rw_mu, rw_wr, rw_wk, rw_wv, rw_wo, rw_w0, rw_w1, rw_w2, rw_a0, rw_a1, rw_a2,
           rw_v0, rw_v1, rw_v2, rw_g1, rw_g2, rw_kk, rw_ka, rw_rk, rw_lnx_g, rw_lnx_b,
           w_up, w_down):
    p = {
        "norm_mix_pre": norm_mix_pre, "norm_mix_post": norm_mix_post,
        "norm_ffn_pre": norm_ffn_pre, "norm_ffn_post": norm_ffn_post,
        "w_in_even": w_in_even, "w_out_even": w_out_even,
        "hgrn_lb_raw": hgrn_lb_raw, "hgrn_norm_g": hgrn_norm_g,
        "rel_bias": rel_bias, "attn_sinks": attn_sinks,
        "rw_mu": rw_mu, "rw_wr": rw_wr, "rw_wk": rw_wk, "rw_wv": rw_wv, "rw_wo": rw_wo,
        "rw_w0": rw_w0, "rw_w1": rw_w1, "rw_w2": rw_w2, "rw_a0": rw_a0, "rw_a1": rw_a1, "rw_a2": rw_a2,
        "rw_v0": rw_v0, "rw_v1": rw_v1, "rw_v2": rw_v2, "rw_g1": rw_g1, "rw_g2": rw_g2,
        "rw_kk": rw_kk, "rw_ka": rw_ka, "rw_rk": rw_rk, "rw_lnx_g": rw_lnx_g, "rw_lnx_b": rw_lnx_b,
        "w_up": w_up, "w_down": w_down,
    }
    bank = {}
    y_p, hgrn_p, k_p, v_p, rwkv_p, shift_p = _trunk(x_prompt, None, None, None, None, None, p, bank)
    y_s, hgrn_s, k_s, v_s, rwkv_s, shift_s = _trunk(
        x_sample, state_hgrn, cache_swa_k, cache_swa_v, state_rwkv, state_shift, p, bank)
    return (y_p, y_s, hgrn_p, hgrn_s, k_p, k_s, v_p, v_s, rwkv_p, rwkv_s, shift_p, shift_s)
```

```python
import functools
import math

import numpy as np
import jax
import jax.numpy as jnp
from jax import lax
from jax.experimental import pallas as pl
from jax.experimental.pallas import tpu as pltpu

F32 = jnp.float32
BF16 = jnp.bfloat16

DEPTH = 4
N_EVEN = 2
N_ODD = 2
A_HEADS = 8
A_KDIM = 128
A_VDIM = 128
A_WIDTH = 1024
B_HEADS = 16
B_HEAD_DIM = 64
B_KV_HEADS = 4
B_GROUP = 4
B_WIDTH = 1024
B_KV_WIDTH = 256
WINDOW = 128
N_BUCKETS = 32
MAX_DISTANCE = 128
MASK_VALUE = -1e30
IN_A = 4096
IN_EVEN = 5632
C_HEAD = 64
C_HEADS = 32
GN_EPS = 64e-5
D_FF = 8192
NORM_EPS = 1e-6

LANES = 128
SUBLANES = 8
VMEM_LIMIT = 56 * 1024 * 1024

HGRN_CHUNK = 128
RWKV_CHUNK = 64
RWKV_TRI_BLOCK = 16


def _cparams(sem):
    return pltpu.CompilerParams(dimension_semantics=sem, vmem_limit_bytes=VMEM_LIMIT)


def _rms(x, g):
    return x * lax.rsqrt(jnp.mean(x * x, axis=-1, keepdims=True) + NORM_EPS) * g


def _dot(a, b):
    return jnp.dot(a.astype(BF16), b.astype(BF16), preferred_element_type=F32)


def _dot_nt(a, b):
    return lax.dot_general(a.astype(BF16), b.astype(BF16), (((1,), (1,)), ((), ())),
                           preferred_element_type=F32)


def _dot_tn(a, b):
    return lax.dot_general(a.astype(BF16), b.astype(BF16), (((0,), (0,)), ((), ())),
                           preferred_element_type=F32)


def _dot_shared(lhs_list, b):
    res = _dot(jnp.concatenate(lhs_list, axis=0), b)
    out, off = [], 0
    for a in lhs_list:
        out.append(res[off:off + a.shape[0]])
        off += a.shape[0]
    return out


def _pick_tile(m, cands):
    for c in cands:
        if m % c == 0:
            return c
    return m


def _norm_matmul_kernel(x_ref, g_ref, w_ref, o_ref, xn_ref):
    @pl.when(pl.program_id(1) == 0)
    def _():
        xn_ref[...] = _rms(x_ref[...], g_ref[...]).astype(BF16)

    o_ref[...] = jnp.dot(xn_ref[...], w_ref[...], preferred_element_type=F32)


def norm_matmul(x, g, w_bf16, tn=1408):
    m, d = x.shape
    n = w_bf16.shape[1]
    tm = _pick_tile(m, (1024, 512, 256, 128, 64, 32, 16, 8))
    nj = n // tn
    col = lambda i, j: jnp.where(i % 2 == 0, j, nj - 1 - j)
    return pl.pallas_call(
        _norm_matmul_kernel,
        grid=(m // tm, nj),
        in_specs=[pl.BlockSpec((tm, d), lambda i, j: (i, 0)),
                  pl.BlockSpec((1, d), lambda i, j: (0, 0)),
                  pl.BlockSpec((d, tn), lambda i, j: (0, col(i, j)))],
        out_specs=pl.BlockSpec((tm, tn), lambda i, j: (i, col(i, j))),
        out_shape=jax.ShapeDtypeStruct((m, n), F32),
        scratch_shapes=[pltpu.VMEM((tm, d), BF16)],
        compiler_params=_cparams(("parallel", "arbitrary")),
        name="norm_matmul",
    )(x, g.reshape(1, d), w_bf16)


def _ffn_kernel(*refs, cast_srcs):
    nsrc = sum(cast_srcs)
    x_ref, gpre_ref, wup_ref, wdn_ref, gpost_ref = refs[:5]
    src_refs = refs[5:5 + nsrc]
    o_ref = refs[5 + nsrc]
    cast_out = refs[6 + nsrc:6 + nsrc + len(cast_srcs)]
    xn_ref, acc_ref = refs[-2:]
    k = 0
    for out_ref, n in zip(cast_out, cast_srcs):
        for j in range(n):
            if n == 1:
                out_ref[...] = src_refs[k][...].astype(BF16)
            else:
                out_ref[j] = src_refs[k][...].astype(BF16)
            k += 1
    f = pl.program_id(1)

    @pl.when(f == 0)
    def _():
        xn_ref[...] = _rms(x_ref[...], gpre_ref[...]).astype(BF16)
        acc_ref[...] = jnp.zeros_like(acc_ref)

    h = jnp.dot(xn_ref[...], wup_ref[...], preferred_element_type=F32)
    h = jnp.square(jnp.maximum(h, 0.0)).astype(BF16)
    acc_ref[...] += jnp.dot(h, wdn_ref[...], preferred_element_type=F32)

    @pl.when(f == pl.num_programs(1) - 1)
    def _():
        o_ref[...] = x_ref[...] + _rms(acc_ref[...], gpost_ref[...])


BF16_SUBLANES = 16


def ffn_steps(m, dff, tf=1024):
    return (m // _pick_tile(m, (512, 256, 128, 64, 32, 16, 8))) * (dff // tf)


CAST_BLOCK_BYTES = 1 << 20


def can_cast_in(nsteps, rows, cols):
    return (rows % nsteps == 0 and (rows // nsteps) % BF16_SUBLANES == 0
            and (rows // nsteps) * cols * 4 <= CAST_BLOCK_BYTES)


def ffn(x, gpre, wup_bf16, wdn_bf16, gpost, casts=(), tf=1024):
    m, d = x.shape
    dff = wup_bf16.shape[1]
    tm = _pick_tile(m, (512, 256, 128, 64, 32, 16, 8))
    nf = dff // tf
    nsteps = (m // tm) * nf
    step = lambda i, f: i * nf + f
    in_specs = [pl.BlockSpec((tm, d), lambda i, f: (i, 0)),
                pl.BlockSpec((1, d), lambda i, f: (0, 0)),
                pl.BlockSpec((d, tf), lambda i, f: (0, f)),
                pl.BlockSpec((tf, d), lambda i, f: (f, 0)),
                pl.BlockSpec((1, d), lambda i, f: (0, 0))]
    args = [x, gpre.reshape(1, d), wup_bf16, wdn_bf16, gpost.reshape(1, d)]
    out_specs = [pl.BlockSpec((tm, d), lambda i, f: (i, 0))]
    out_shape = [jax.ShapeDtypeStruct((m, d), F32)]
    for job in casts:
        rows, cols = job[0][0].shape[len(job[0][1]):]
        assert can_cast_in(nsteps, rows, cols)
        rb = rows // nsteps
        for arr, lead in job:
            in_specs.append(pl.BlockSpec((None,) * len(lead) + (rb, cols),
                                         lambda i, f, lead=lead: lead + (step(i, f), 0)))
            args.append(arr)
        if len(job) == 1:
            out_specs.append(pl.BlockSpec((rb, cols), lambda i, f: (step(i, f), 0)))
            out_shape.append(jax.ShapeDtypeStruct((rows, cols), BF16))
        else:
            out_specs.append(pl.BlockSpec((len(job), rb, cols), lambda i, f: (0, step(i, f), 0)))
            out_shape.append(jax.ShapeDtypeStruct((len(job), rows, cols), BF16))
    outs = pl.pallas_call(
        functools.partial(_ffn_kernel, cast_srcs=tuple(len(job) for job in casts)),
        grid=(m // tm, nf),
        in_specs=in_specs,
        out_specs=out_specs,
        out_shape=out_shape,
        scratch_shapes=[pltpu.VMEM((tm, d), BF16), pltpu.VMEM((tm, d), F32)],
        compiler_params=_cparams(("parallel", "arbitrary")),
        name="ffn",
    )(*args)
    return outs[0], list(outs[1:])


def _even_out_kernel(oa_ref, ga_ref, ob_ref, x_ref, ag_ref, w_ref, gpost_ref, o_ref):
    ga = ga_ref[...]
    oan = _rms(oa_ref[...], ag_ref[...]) * (ga * jax.nn.sigmoid(ga))
    mix = (jnp.dot(oan.astype(BF16), w_ref[:A_WIDTH, :], preferred_element_type=F32)
           + jnp.dot(ob_ref[...].astype(BF16), w_ref[A_WIDTH:, :], preferred_element_type=F32))
    o_ref[...] = x_ref[...] + _rms(mix, gpost_ref[...])


def even_out(o_a, proj, o_b, x, a_norm_g, w_out_bf16, gpost):
    m, d = x.shape
    tm = _pick_tile(m, (512, 256, 128, 64, 32, 16, 8))
    ga_blk = 3
    return pl.pallas_call(
        _even_out_kernel,
        grid=(m // tm,),
        in_specs=[pl.BlockSpec((tm, A_WIDTH), lambda i: (i, 0)),
                  pl.BlockSpec((tm, A_WIDTH), lambda i: (i, ga_blk)),
                  pl.BlockSpec((tm, B_WIDTH), lambda i: (i, 0)),
                  pl.BlockSpec((tm, d), lambda i: (i, 0)),
                  pl.BlockSpec((1, A_WIDTH), lambda i: (0, 0)),
                  pl.BlockSpec((A_WIDTH + B_WIDTH, d), lambda i: (0, 0)),
                  pl.BlockSpec((1, d), lambda i: (0, 0))],
        out_specs=pl.BlockSpec((tm, d), lambda i: (i, 0)),
        out_shape=jax.ShapeDtypeStruct((m, d), F32),
        compiler_params=_cparams(("parallel",)),
        name="even_out",
    )(o_a, proj, o_b, x, a_norm_g.reshape(1, A_WIDTH), w_out_bf16, gpost.reshape(1, d))


def _odd_out_kernel(y_ref, x_ref, w_ref, gpost_ref, o_ref):
    mix = jnp.dot(y_ref[...].astype(BF16), w_ref[...], preferred_element_type=F32)
    o_ref[...] = x_ref[...] + _rms(mix, gpost_ref[...])


def odd_out(yg, x, wo_bf16, gpost):
    m, d = x.shape
    tm = _pick_tile(m, (512, 256, 128, 64, 32, 16, 8))
    return pl.pallas_call(
        _odd_out_kernel,
        grid=(m // tm,),
        in_specs=[pl.BlockSpec((tm, d), lambda i: (i, 0)),
                  pl.BlockSpec((tm, d), lambda i: (i, 0)),
                  pl.BlockSpec((d, d), lambda i: (0, 0)),
                  pl.BlockSpec((1, d), lambda i: (0, 0))],
        out_specs=pl.BlockSpec((tm, d), lambda i: (i, 0)),
        out_shape=jax.ShapeDtypeStruct((m, d), F32),
        compiler_params=_cparams(("parallel",)),
        name="odd_out",
    )(yg, x, wo_bf16, gpost.reshape(1, d))


def _level_consts(c):
    levels = []
    s = c // 2
    while s >= 1:
        levels.append(s)
        s //= 2
    mask = np.zeros((len(levels), c, c), np.float32)
    idx = np.arange(c)
    for l, s in enumerate(levels):
        same = (idx[:, None] // (2 * s)) == (idx[None, :] // (2 * s))
        upper = (idx[:, None] % (2 * s)) >= s
        lower = (idx[None, :] % (2 * s)) < s
        mask[l] = (same & upper & lower).astype(np.float32)
    return levels, mask


def _split_rows(g, s, rowid):
    c = g.shape[0]
    if 2 * s >= SUBLANES:
        return jnp.concatenate(
            [jnp.broadcast_to(g[b + s - 1:b + s, :], (2 * s, g.shape[1])) for b in range(0, c, 2 * s)], axis=0)
    r = rowid % (2 * s)
    out = g
    for off in range(-(s - 1), s + 1):
        if off != 0:
            out = jnp.where(r == s - 1 + off, pltpu.roll(g, off % c, 0), out)
    return out


def _hgrn_kernel(*refs, layer, chunk, nchunks, levels, has_state, nheads):
    if has_state:
        q_ref, f_ref, i_ref, lb_ref, mask_ref, s0_ref, o_ref, s_ref, st_ref = refs
    else:
        q_ref, f_ref, i_ref, lb_ref, mask_ref, o_ref, s_ref, st_ref = refs
    c = chunk
    nh = nheads
    rowid = lax.broadcasted_iota(jnp.int32, (c, nh * LANES), 0)
    l_idx = pl.program_id(2)

    @pl.when(l_idx == 0)
    def _():
        for hi in range(nh):
            if has_state:
                st_ref[hi] = s0_ref[0, hi].T
            else:
                st_ref[hi] = jnp.zeros((A_VDIM, A_KDIM), F32)

    lbr = lb_ref[...]
    e = jnp.exp(lbr - jnp.max(lbr, axis=0, keepdims=True))
    p = e / jnp.sum(e, axis=0, keepdims=True)
    lb = jnp.zeros((1, nh * LANES), F32)
    for i in range(1, layer + 1):
        lb = lb + p[i:i + 1, :]
    one_m_lb = 1.0 - lb
    head = lambda x, hi: x[:, hi * LANES:(hi + 1) * LANES]

    def body(ci, carry):
        rows = pl.ds(pl.multiple_of(ci * c, c), c)
        fq = f_ref[rows, :]
        qr = q_ref[rows, :]
        v = i_ref[rows, :]
        f = lb + one_m_lb * jax.nn.sigmoid(fq)
        k = 1.0 - f
        q = qr * jax.nn.sigmoid(qr) * (A_KDIM ** -0.5)

        g = jnp.log2(f)
        sft = 1
        while sft < c:
            g = g + jnp.where(rowid >= sft, pltpu.roll(g, sft, 0), 0.0)
            sft *= 2
        glast = g[c - 1:c, :]
        q_in = q * jnp.exp2(g)
        kd = k * jnp.exp2(glast - g)
        dec = jnp.exp2(glast)
        diag = q * k

        sts = [st_ref[hi] for hi in range(nh)]
        os_ = [_dot_nt(head(q_in, hi), sts[hi]) for hi in range(nh)]
        attns = [jnp.zeros((c, c), F32) for _ in range(nh)]
        for l, s in enumerate(levels):
            e = jnp.exp2(-jnp.abs(g - _split_rows(g, s, rowid)))
            qs = q * e
            ks = k * e
            ml = mask_ref[l]
            attns = [at + ml * _dot_nt(head(qs, hi), head(ks, hi)) for hi, at in enumerate(attns)]
        for hi in range(nh):
            vh = head(v, hi)
            o = os_[hi] + _dot(attns[hi], vh) + jnp.sum(head(diag, hi), axis=-1, keepdims=True) * vh
            o_ref[rows, hi * LANES:(hi + 1) * LANES] = o
        for hi in range(nh):
            st_ref[hi] = sts[hi] * head(dec, hi) + _dot_tn(head(v, hi), head(kd, hi))
        return carry

    lax.fori_loop(0, nchunks, body, 0, unroll=2 if nchunks % 2 == 0 else 1)

    @pl.when(l_idx == pl.num_programs(2) - 1)
    def _():
        for hi in range(nh):
            s_ref[0, hi] = st_ref[hi].T


def hgrn(proj, lb_raw, layer, batch, seqlen, s0=None, nheads=A_HEADS):
    m = proj.shape[0]
    c = math.gcd(seqlen, HGRN_CHUNK)
    lblk = _pick_tile(seqlen, (512, 256, 128, 64, 32, 16, 8))
    nl_blocks = seqlen // lblk
    levels, mask = _level_consts(c)
    has_state = s0 is not None
    if c * 4 <= HGRN_CHUNK:
        nheads = A_HEADS
    w = nheads * LANES
    ngroups = A_HEADS // nheads
    kern = functools.partial(_hgrn_kernel, layer=layer, chunk=c, nchunks=lblk // c,
                             levels=tuple(levels), has_state=has_state, nheads=nheads)
    row = lambda b, h, l: b * nl_blocks + l
    in_specs = [pl.BlockSpec((lblk, w), lambda b, h, l: (row(b, h, l), h)),
                pl.BlockSpec((lblk, w), lambda b, h, l: (row(b, h, l), ngroups + h)),
                pl.BlockSpec((lblk, w), lambda b, h, l: (row(b, h, l), 2 * ngroups + h)),
                pl.BlockSpec((N_EVEN, w), lambda b, h, l: (0, h)),
                pl.BlockSpec((len(levels), c, c), lambda b, h, l: (0, 0, 0))]
    args = [proj, proj, proj, lb_raw, jnp.asarray(mask)]
    st_blk = pl.BlockSpec((1, nheads, A_KDIM, A_VDIM), lambda b, h, l: (b, h, 0, 0))
    if has_state:
        in_specs.append(pl.BlockSpec((None, 1, nheads, A_KDIM, A_VDIM), lambda b, h, l: (layer, b, h, 0, 0)))
        args.append(s0)
    return pl.pallas_call(
        kern,
        grid=(batch, ngroups, nl_blocks),
        in_specs=in_specs,
        out_specs=[pl.BlockSpec((lblk, w), lambda b, h, l: (row(b, h, l), h)), st_blk],
        out_shape=[jax.ShapeDtypeStruct((m, A_WIDTH), F32),
                   jax.ShapeDtypeStruct((batch, A_HEADS, A_KDIM, A_VDIM), F32)],
        scratch_shapes=[pltpu.VMEM((nheads, A_VDIM, A_KDIM), F32)],
        compiler_params=_cparams(("parallel", "parallel", "arbitrary")),
        name="hgrn",
    )(*args)


def _t5_bucket(dist):
    max_exact = N_BUCKETS // 2
    d = np.maximum(dist, 0)
    large = max_exact + (np.log(np.maximum(d, max_exact).astype(np.float32) / max_exact)
                         / math.log(MAX_DISTANCE / max_exact) * (N_BUCKETS - max_exact)).astype(np.int32)
    large = np.minimum(large, N_BUCKETS - 1)
    return np.where(d < max_exact, d, large).astype(np.int32)


def _swa_kernel(q_ref, kp_ref, kc_ref, vp_ref, vc_ref, bucket_ref, band_ref, rb_ref, sink_ref,
                o_ref, bias_ref, *, qb, span, nsub, prev_always_valid):
    first = (pl.program_id(0) == 0) & (pl.program_id(1) == 0)

    @pl.when(first)
    def _():
        bk = bucket_ref[...]
        band = band_ref[...]

        def per_head(h, carry):
            def per_bucket(bi, acc):
                return jnp.where(bk == bi, rb_ref[bi, h], acc)
            acc = lax.fori_loop(0, N_BUCKETS, per_bucket, jnp.zeros((qb, span), F32))
            bias_ref[h] = jnp.where(band > 0, acc, MASK_VALUE)
            return carry

        lax.fori_loop(0, B_HEADS, per_head, 0)

    scale = B_HEAD_DIM ** -0.5
    q = q_ref[...]
    kfull = jnp.concatenate([kp_ref[...], kc_ref[...]], axis=0)
    vfull = jnp.concatenate([vp_ref[...], vc_ref[...]], axis=0)
    if not prev_always_valid:
        col = lax.broadcasted_iota(jnp.int32, (qb, span), 1)
        no_prev = (col < WINDOW) & (pl.program_id(1) == 0)
    jh = [(j, h) for j in range(nsub) for h in range(B_HEADS)]
    kv_head = lambda x, j, kh: x[j * qb:j * qb + span, kh * B_HEAD_DIM:(kh + 1) * B_HEAD_DIM].astype(BF16)
    ks = {(j, kh): kv_head(kfull, j, kh) for j in range(nsub) for kh in range(B_KV_HEADS)}
    vs = {(j, kh): kv_head(vfull, j, kh) for j in range(nsub) for kh in range(B_KV_HEADS)}
    qs = [(q[j * qb:(j + 1) * qb, h * B_HEAD_DIM:(h + 1) * B_HEAD_DIM] * scale).astype(BF16) for j, h in jh]
    ss = [_dot_nt(qv, ks[j, h // B_GROUP]) + bias_ref[h] for qv, (j, h) in zip(qs, jh)]
    if not prev_always_valid:
        ss = [jnp.where(no_prev, MASK_VALUE, s) if j == 0 else s for s, (j, h) in zip(ss, jh)]
    ms = [jnp.maximum(jnp.max(s, axis=-1, keepdims=True), sink_ref[h]) for s, (j, h) in zip(ss, jh)]
    ps = [jnp.exp(s - m) for s, m in zip(ss, ms)]
    denoms = [jnp.sum(p, axis=-1, keepdims=True) + jnp.exp(sink_ref[h] - m) for p, m, (j, h) in zip(ps, ms, jh)]
    outs = [_dot(p, vs[j, h // B_GROUP]) / dn for p, dn, (j, h) in zip(ps, denoms, jh)]
    for j in range(nsub):
        o_ref[j * qb:(j + 1) * qb, :] = jnp.concatenate(
            outs[j * B_HEADS:(j + 1) * B_HEADS], axis=1).astype(o_ref.dtype)


def swa(proj, batch, seqlen, rel_bias, sinks, layer, k_past=None, v_past=None):
    m = proj.shape[0]
    has_cache = k_past is not None
    qb = math.gcd(seqlen, WINDOW)
    nb = seqlen // qb
    span = WINDOW + qb
    dist = np.arange(qb)[:, None] + WINDOW - np.arange(span)[None, :]
    band = ((dist >= 0) & (dist < WINDOW)).astype(np.float32)
    bucket = _t5_bucket(dist)
    q_col = IN_A // B_WIDTH
    k_col = (IN_A + B_WIDTH) // B_KV_WIDTH
    v_col = k_col + 1
    nsub = 2 if (not has_cache and nb % 2 == 0) else 1
    ng = nb // nsub
    cur = lambda c: (lambda b, n: (b * ng + n, c))
    if has_cache:
        assert nb == 1
        prev_k = pl.BlockSpec((None, WINDOW, B_KV_WIDTH), lambda b, n: (layer, b, 0))
        prev_v = pl.BlockSpec((None, WINDOW, B_KV_WIDTH), lambda b, n: (layer, b, 0))
        kp_arr = k_past.reshape(k_past.shape[0], batch * WINDOW, B_KV_WIDTH)
        vp_arr = v_past.reshape(v_past.shape[0], batch * WINDOW, B_KV_WIDTH)
    else:
        assert qb == WINDOW
        prev = lambda c: (lambda b, n: (b * nb + jnp.maximum(n * nsub - 1, 0), c))
        prev_k = pl.BlockSpec((WINDOW, B_KV_WIDTH), prev(k_col))
        prev_v = pl.BlockSpec((WINDOW, B_KV_WIDTH), prev(v_col))
        kp_arr, vp_arr = proj, proj
    kern = functools.partial(_swa_kernel, qb=qb, span=span, nsub=nsub, prev_always_valid=has_cache)
    return pl.pallas_call(
        kern,
        grid=(batch, ng),
        in_specs=[pl.BlockSpec((nsub * qb, B_WIDTH), cur(q_col)),
                  prev_k,
                  pl.BlockSpec((nsub * qb, B_KV_WIDTH), cur(k_col)),
                  prev_v,
                  pl.BlockSpec((nsub * qb, B_KV_WIDTH), cur(v_col)),
                  pl.BlockSpec((qb, span), lambda b, n: (0, 0)),
                  pl.BlockSpec((qb, span), lambda b, n: (0, 0)),
                  pl.BlockSpec(memory_space=pltpu.SMEM),
                  pl.BlockSpec(memory_space=pltpu.SMEM)],
        out_specs=pl.BlockSpec((nsub * qb, B_WIDTH), lambda b, n: (b * ng + n, 0)),
        out_shape=jax.ShapeDtypeStruct((m, B_WIDTH), BF16 if qb % 16 == 0 else F32),
        scratch_shapes=[pltpu.VMEM((B_HEADS, qb, span), F32)],
        compiler_params=_cparams(("arbitrary", "arbitrary")),
        name="swa",
    )(proj, kp_arr, proj, vp_arr, proj, jnp.asarray(bucket), jnp.asarray(band), rel_bias, sinks)


def _rmsnorm_kernel(x_ref, g_ref, o_ref):
    o_ref[...] = _rms(x_ref[...], g_ref[...])


def rmsnorm_rows(x, g):
    m, d = x.shape
    tm = _pick_tile(m, (512, 256, 128, 64, 32, 16, 8))
    return pl.pallas_call(
        _rmsnorm_kernel,
        grid=(m // tm,),
        in_specs=[pl.BlockSpec((tm, d), lambda i: (i, 0)), pl.BlockSpec((1, d), lambda i: (0, 0))],
        out_specs=pl.BlockSpec((tm, d), lambda i: (i, 0)),
        out_shape=jax.ShapeDtypeStruct((m, d), F32),
        compiler_params=_cparams(("parallel",)),
        name="rmsnorm",
    )(x, g.reshape(1, d))


LORA_W = (0, 128)
LORA_A = (128, 256)
LORA_G = (256, 512)
LORA_V = (512, 640)


def _rwkv_in_phase(i, step):
    return jnp.where(i % 2 == 0, step, 3 - step)


def _rwkv_in_kernel(x_ref, xp_ref, s_ref, g_ref, mu3_ref, mul_ref, w_ref, w1_ref,
                    rkv_ref, mid_ref, h_s, hp_s, *, tm, seqlen, has_vres):
    step = pl.program_id(1)
    ph = _rwkv_in_phase(pl.program_id(0), step)

    @pl.when(step == 0)
    def _():
        g = g_ref[...]
        h = _rms(x_ref[...], g)
        rowid = lax.broadcasted_iota(jnp.int32, h.shape, 0)
        rolled = pltpu.roll(h, 1, 0)
        if seqlen % tm == 0:
            prev_last = _rms(xp_ref[...], g)[SUBLANES - 1:SUBLANES, :]
            at_start = pl.program_id(0) % (seqlen // tm) == 0
            first = jnp.where(at_start, s_ref[...], prev_last)
            hp = jnp.where(rowid == 0, first, rolled)
        else:
            hp = jnp.where(rowid % seqlen == 0, s_ref[...], rolled)
        h_s[...] = h
        hp_s[...] = hp

    @pl.when(ph < 3)
    def _():
        h = h_s[...]
        xm = (h + (hp_s[...] - h) * mu3_ref[0]).astype(BF16)
        rkv_ref[0] = jnp.dot(xm, w_ref[...], preferred_element_type=F32)

    @pl.when(ph == 3)
    def _():
        h = h_s[...]
        xx = hp_s[...] - h
        mix = lambda i: (h + xx * mul_ref[i:i + 1, :]).astype(BF16)
        low = lambda i, rng: jnp.dot(mix(i), w1_ref[:, rng[0]:rng[1]], preferred_element_type=F32)
        parts = [jnp.tanh(low(0, LORA_W)), low(1, LORA_A), jax.nn.sigmoid(low(2, LORA_G))]
        if has_vres:
            parts.append(low(3, LORA_V))
        mid_ref[...] = jnp.concatenate(parts, axis=1).astype(BF16)


def rwkv_in(x, g, shift0, batch, seqlen, mu, w3_bf16, w1cat_bf16, has_vres):
    m, d = x.shape
    tm = next(t for t in (512, 256, 128, 64, 32, 16, 8)
              if m % t == 0 and (seqlen % t == 0 or t % seqlen == 0))
    midw = w1cat_bf16.shape[1]
    mu_rkv = jnp.stack([mu[0], mu[2], mu[3]])[:, None, :]
    mu_low = jnp.stack([mu[1], mu[4], mu[5], mu[3]])
    if shift0 is None:
        shift0 = jnp.zeros((batch, d), F32)
    if seqlen % tm == 0:
        srow = shift0[:, None, :]
        tps = seqlen // tm
        s_spec = pl.BlockSpec((None, 1, d), lambda i, p: (i // tps, 0, 0))
    else:
        srow = jnp.repeat(shift0, seqlen, axis=0)
        s_spec = pl.BlockSpec((tm, d), lambda i, p: (i, 0))
    sub = tm // SUBLANES
    proj_of = lambda i, p: jnp.minimum(_rwkv_in_phase(i, p), 2)
    kern = functools.partial(_rwkv_in_kernel, tm=tm, seqlen=seqlen, has_vres=has_vres)
    return pl.pallas_call(
        kern,
        grid=(m // tm, 4),
        in_specs=[pl.BlockSpec((tm, d), lambda i, p: (i, 0)),
                  pl.BlockSpec((SUBLANES, d), lambda i, p: (jnp.maximum(i * sub - 1, 0), 0)),
                  s_spec,
                  pl.BlockSpec((1, d), lambda i, p: (0, 0)),
                  pl.BlockSpec((1, 1, d), lambda i, p: (proj_of(i, p), 0, 0)),
                  pl.BlockSpec((4, d), lambda i, p: (0, 0)),
                  pl.BlockSpec((None, d, d), lambda i, p: (proj_of(i, p), 0, 0)),
                  pl.BlockSpec((d, midw), lambda i, p: (0, 0))],
        out_specs=[pl.BlockSpec((1, tm, d), lambda i, p: (proj_of(i, p), i, 0)),
                   pl.BlockSpec((tm, midw), lambda i, p: (i, 0))],
        out_shape=[jax.ShapeDtypeStruct((3, m, d), F32),
                   jax.ShapeDtypeStruct((m, midw), BF16)],
        scratch_shapes=[pltpu.VMEM((tm, d), F32), pltpu.VMEM((tm, d), F32)],
        compiler_params=_cparams(("parallel", "arbitrary")),
        name="rwkv_in",
    )(x, x, srow, g.reshape(1, d), mu_rkv, mu_low, w3_bf16, w1cat_bf16)


def _pad_lora(w1, w2, width):
    r = w1.shape[1]
    return (jnp.pad(w1, ((0, 0), (0, width - r))).astype(BF16),
            jnp.pad(w2, ((0, width - r), (0, 0))).astype(BF16))


def _rwkv_consts(c):
    i = np.arange(2 * c)
    same = (i[:, None] // c) == (i[None, :] // c)
    strict = (same & ((i[:, None] % c) > (i[None, :] % c))).astype(np.float32)
    incl = (same & ((i[:, None] % c) >= (i[None, :] % c))).astype(np.float32)
    eye = np.eye(2 * c, dtype=np.float32)
    l = np.arange(LANES)
    headones = ((l[:, None] // C_HEAD) == (l[None, :] // C_HEAD)).astype(np.float32)
    bs = min(RWKV_TRI_BLOCK, c)
    blk = lambda n: (i[:, None] // n) == (i[None, :] // n)
    tmasks = [blk(bs)]
    s = bs
    while s < c:
        tmasks.append(blk(2 * s) & ~blk(s))
        s *= 2
    return strict, incl, eye, headones, np.stack(tmasks).astype(np.float32), bs


def _rwkv_core_kernel(*refs, chunk, nchunks, npairs, has_state, has_vres, bs):
    it = iter(refs)
    r_ref, k_ref, v_ref, mid_ref = (next(it) for _ in range(4))
    w0_ref, w2_ref, a0_ref, a2_ref, g2_ref = (next(it) for _ in range(5))
    if has_vres:
        vf_ref, v0_ref, v2_ref = next(it), next(it), next(it)
    kk_ref, ka_ref, rk_ref, lg_ref, lb_ref = (next(it) for _ in range(5))
    strict_ref, incl_ref, eye_ref, hones_ref, tmask_ref = (next(it) for _ in range(5))
    if has_state:
        s0_ref = next(it)
    y_ref, s_ref, st_ref, wl_ref, a_ref, g_ref = (next(it) for _ in range(6))
    if has_vres:
        vg_ref = next(it)
    c = chunk
    l_idx = pl.program_id(2)
    lane = lax.broadcasted_iota(jnp.int32, (1, LANES), 1)
    m0 = (lane < C_HEAD).astype(F32)
    m1 = 1.0 - m0

    low = lambda rng, w_ref: jnp.dot(mid_ref[:, rng[0]:rng[1]], w_ref[...], preferred_element_type=F32)
    sig = lambda z: 0.5 * jnp.tanh(0.5 * z) + 0.5
    wl_ref[...] = -math.exp(-0.5) * sig(w0_ref[...] + low(LORA_W, w2_ref))
    a_ref[...] = sig(a0_ref[...] + low(LORA_A, a2_ref))
    g_ref[...] = low(LORA_G, g2_ref)
    if has_vres:
        vg_ref[...] = sig(v0_ref[...] + low(LORA_V, v2_ref))

    @pl.when(l_idx == 0)
    def _():
        for pi in range(npairs):
            if has_state:
                z = jnp.zeros((C_HEAD, C_HEAD), F32)
                top = jnp.concatenate([s0_ref[0, 2 * pi], z], axis=1)
                bot = jnp.concatenate([z, s0_ref[0, 2 * pi + 1]], axis=1)
                st_ref[pi] = jnp.concatenate([top, bot], axis=0)
            else:
                st_ref[pi] = jnp.zeros((LANES, LANES), F32)

    strict = strict_ref[...]
    incl = incl_ref[...]
    eye = eye_ref[...]

    def stack(x):
        return jnp.concatenate([x * m0, x * m1], axis=0)

    first_head = lax.broadcasted_iota(jnp.int32, (c, LANES), 1) < C_HEAD
    rowid = lax.broadcasted_iota(jnp.int32, (c, LANES), 0)

    def rowsums(xs):
        return [jnp.where(first_head,
                          jnp.sum(x * m0, axis=-1, keepdims=True),
                          jnp.sum(x * m1, axis=-1, keepdims=True)) for x in xs]

    def cumsum_rows(x):
        s = 1
        while s < c:
            x = x + jnp.where(rowid >= s, pltpu.roll(x, s, 0), 0.0)
            s *= 2
        return x

    def load(pi, rows):
        cols = slice(pi * LANES, (pi + 1) * LANES)
        k = k_ref[0, rows, cols]
        v = v_ref[0, rows, cols]
        a = a_ref[rows, cols]
        if has_vres:
            v = v + (vf_ref[0, rows, cols] - v) * vg_ref[rows, cols]
        return dict(cols=cols, r=r_ref[0, rows, cols], v=v, a=a, wl=wl_ref[rows, cols],
                    kr=k * kk_ref[:, cols], kh=k * (1.0 + (a - 1.0) * ka_ref[:, cols]))

    def decays(p, ss):
        kk = p["kr"] * lax.rsqrt(jnp.maximum(ss, 1e-24))
        b = kk * p["a"]
        gc = cumsum_rows(p["wl"])
        gl = gc[c - 1:c, :]
        e_neg = jnp.exp(-gc)
        e_out = jnp.exp(gl - gc)
        p.update(gl=gl, ab=-kk * jnp.exp(gc - p["wl"]), rb=p["r"] * jnp.exp(gc),
                 bt=b * e_neg, kt=p["kh"] * e_neg, bh=b * e_out, khat=p["kh"] * e_out)

    def intra(p):
        lhs = jnp.concatenate([stack(p["ab"]), stack(p["rb"])], axis=0)
        with_b = _dot_nt(lhs, jnp.concatenate([p["bt"], p["bt"]], axis=0))
        with_k = _dot_nt(lhs, jnp.concatenate([p["kt"], p["kt"]], axis=0))
        p.update(a_ab=with_b[:2 * c] * strict, a_rb=with_b[2 * c:] * incl,
                 a_ak=with_k[:2 * c] * strict, a_rk=with_k[2 * c:] * incl)

    def body(ci, carry):
        rows = pl.ds(pl.multiple_of(ci * c, c), c)
        ps = [load(pi, rows) for pi in range(npairs)]
        for p, ss in zip(ps, rowsums([p["kr"] * p["kr"] for p in ps])):
            decays(p, ss)
        for p in ps:
            intra(p)

        pws = [p["a_ab"] * tmask_ref[0] for p in ps]
        ts = [eye + pw for pw in pws]
        if bs > 2:
            pws = [_dot(pw, pw) for pw in pws]
            n = 2
            while 2 * n < bs:
                res = [_dot_shared([t, pw], pw) for t, pw in zip(ts, pws)]
                ts = [t + r[0] for t, r in zip(ts, res)]
                pws = [r[1] for r in res]
                n *= 2
            ts = [t + _dot(t, pw) for t, pw in zip(ts, pws)]
        s, lvl = bs, 1
        while s < c:
            ms = [_dot(t, p["a_ab"] * tmask_ref[lvl]) for t, p in zip(ts, ps)]
            ts = [t + _dot(m, t) for t, m in zip(ts, ms)]
            s, lvl = 2 * s, lvl + 1

        sts = [st_ref[pi] for pi in range(npairs)]
        fss = [_dot_nt(jnp.concatenate([p["ab"], p["rb"]], axis=0), st) for p, st in zip(ps, sts)]
        vss = [stack(p["v"]) for p in ps]
        rhss = [stack(fs[:c]) + _dot(p["a_ak"], vs) for p, fs, vs in zip(ps, fss, vss)]
        uss = [_dot(t, rhs) for t, rhs in zip(ts, rhss)]
        yss = [_dot(p["a_rb"], us) + _dot(p["a_rk"], vs) for p, us, vs in zip(ps, uss, vss)]
        for pi, (p, st, us) in enumerate(zip(ps, sts, uss)):
            u = us[:c] + us[c:]
            upd = _dot_tn(jnp.concatenate([u, p["v"]], axis=0),
                          jnp.concatenate([p["bh"], p["khat"]], axis=0))
            st_ref[pi] = st * jnp.exp(p["gl"]) + upd * hones_ref[...]
        inv_n = 1.0 / C_HEAD
        ys_ = [fs[c:] + ys[:c] + ys[c:] for fs, ys in zip(fss, yss)]
        sums = rowsums(ys_ + [p["r"] * p["kh"] * rk_ref[:, p["cols"]] for p in ps])
        dlts = [y - m * inv_n for y, m in zip(ys_, sums[:npairs])]
        vars_ = rowsums([d * d for d in dlts])
        for p, dlt, var, bsum in zip(ps, dlts, vars_, sums[npairs:]):
            cols = p["cols"]
            yn = dlt * lax.rsqrt(var * inv_n + GN_EPS) * lg_ref[:, cols] + lb_ref[:, cols]
            y_ref[rows, cols] = ((yn + bsum * p["v"]) * g_ref[rows, cols]).astype(y_ref.dtype)
        return carry

    lax.fori_loop(0, nchunks, body, 0)

    @pl.when(l_idx == pl.num_programs(2) - 1)
    def _():
        for pi in range(npairs):
            st = st_ref[pi]
            s_ref[0, 2 * pi] = st[:C_HEAD, :C_HEAD]
            s_ref[0, 2 * pi + 1] = st[C_HEAD:, C_HEAD:]


def rwkv_core(rkv, mid, low2, kk_p, ka_p, rk_p, lnx_g, lnx_b, batch, seqlen, layer,
              s0=None, v_first=None):
    _, m, d = rkv.shape
    c = math.gcd(seqlen, RWKV_CHUNK)
    npairs = C_HEADS // 2
    lblk = _pick_tile(seqlen, (256, 128, 64, 32, 16, 8))
    nl_blocks = seqlen // lblk
    has_state = s0 is not None
    has_vres = v_first is not None
    w = npairs * LANES
    ngroups = d // w
    midw = mid.shape[1]
    *consts, bs = _rwkv_consts(c)
    strict, incl, eye, hones, tmasks = (jnp.asarray(x) for x in consts)
    row = lambda b, p, l: b * nl_blocks + l
    blk3 = lambda which: pl.BlockSpec((1, lblk, w), lambda b, p, l: (which, row(b, p, l), p))
    blk2 = pl.BlockSpec((lblk, w), lambda b, p, l: (row(b, p, l), p))
    par = pl.BlockSpec((1, w), lambda b, p, l: (0, p))
    cols = lambda arr: pl.BlockSpec((arr.shape[0], w), lambda b, p, l: (0, p))
    full = lambda arr: pl.BlockSpec(arr.shape, lambda b, p, l: (0,) * arr.ndim)
    w0, w2, a0, a2, g2 = low2[:5]
    in_specs = [blk3(0), blk3(1), blk3(2), pl.BlockSpec((lblk, midw), lambda b, p, l: (row(b, p, l), 0)),
                par, cols(w2), par, cols(a2), cols(g2)]
    args = [rkv, rkv, rkv, mid, w0.reshape(1, d), w2, a0.reshape(1, d), a2, g2]
    if has_vres:
        v0, v2 = low2[5:]
        in_specs += [blk3(2), par, cols(v2)]
        args += [v_first, v0.reshape(1, d), v2]
    in_specs += [par] * 5
    args += [x.reshape(1, d) for x in (kk_p, ka_p, rk_p, lnx_g, lnx_b)]
    in_specs += [full(x) for x in (strict, incl, eye, hones, tmasks)]
    args += [strict, incl, eye, hones, tmasks]
    st_blk = pl.BlockSpec((1, 2 * npairs, C_HEAD, C_HEAD), lambda b, p, l: (b, p, 0, 0))
    if has_state:
        in_specs.append(pl.BlockSpec((None, 1, 2 * npairs, C_HEAD, C_HEAD),
                                     lambda b, p, l: (layer, b, p, 0, 0)))
        args.append(s0)
    kern = functools.partial(_rwkv_core_kernel, chunk=c, nchunks=lblk // c, npairs=npairs,
                             has_state=has_state, has_vres=has_vres, bs=bs)
    return pl.pallas_call(
        kern,
        grid=(batch, ngroups, nl_blocks),
        in_specs=in_specs,
        out_specs=[blk2, st_blk],
        out_shape=[jax.ShapeDtypeStruct((m, d), BF16 if lblk % 16 == 0 else F32),
                   jax.ShapeDtypeStruct((batch, C_HEADS, C_HEAD, C_HEAD), F32)],
        scratch_shapes=[pltpu.VMEM((npairs, LANES, LANES), F32)]
        + [pltpu.VMEM((lblk, w), F32)] * (4 if has_vres else 3),
        compiler_params=_cparams(("parallel", "parallel", "arbitrary")),
        name="rwkv_core",
    )(*args)


def _weight_sources(p):
    src = {}
    for l in range(DEPTH):
        src[("up", l)] = [(p["w_up"], (l,))]
        src[("down", l)] = [(p["w_down"], (l,))]
    for e in range(N_EVEN):
        src[("in", e)] = [(p["w_in_even"], (e,))]
        src[("out", e)] = [(p["w_out_even"], (e,))]
    for o in range(N_ODD):
        src[("rkv", o)] = [(p["rw_wr"], (o,)), (p["rw_wk"], (o,)), (p["rw_wv"], (o,))]
        src[("wo", o)] = [(p["rw_wo"], (o,))]
    return src


def _bf16_weight(bank, src, key):
    if key not in bank:
        parts = [arr[lead].astype(BF16) for arr, lead in src[key]]
        bank[key] = parts[0] if len(parts) == 1 else jnp.stack(parts)
    return bank[key]


def _even_layer(x, batch, seqlen, e, layer, p, wget, st_hgrn, k_cache, v_cache):
    proj = norm_matmul(x, p["norm_mix_pre"][layer], wget(("in", e)))
    o_a, s_new = hgrn(proj, p["hgrn_lb_raw"], e, batch, seqlen, st_hgrn)
    k_lo = IN_A + B_WIDTH
    new_rows = min(seqlen, WINDOW)
    tails = jnp.stack([proj[(b + 1) * seqlen - new_rows:(b + 1) * seqlen, k_lo:] for b in range(batch)])
    kb = tails[:, :, :B_KV_WIDTH].reshape(batch, new_rows, B_KV_HEADS, B_HEAD_DIM)
    vb = tails[:, :, B_KV_WIDTH:].reshape(batch, new_rows, B_KV_HEADS, B_HEAD_DIM)
    o_b = swa(proj, batch, seqlen, p["rel_bias"], p["attn_sinks"][e], e, k_cache, v_cache)
    if k_cache is None:
        k_new, v_new = kb, vb
    else:
        k_new = jnp.concatenate([k_cache[e, :, new_rows:], kb], axis=1)
        v_new = jnp.concatenate([v_cache[e, :, new_rows:], vb], axis=1)
    x = even_out(o_a, proj, o_b, x, p["hgrn_norm_g"][e], wget(("out", e)), p["norm_mix_post"][layer])
    return x, s_new, k_new, v_new


def _odd_layer(x, batch, seqlen, o, layer, p, wget, shift0, s0, v_first):
    m, d = x.shape
    g_pre = p["norm_mix_pre"][layer]
    has_vres = o > 0
    width = lambda rng: rng[1] - rng[0]
    w1p, w2p = _pad_lora(p["rw_w1"][o], p["rw_w2"][o], width(LORA_W))
    a1p, a2p = _pad_lora(p["rw_a1"][o], p["rw_a2"][o], width(LORA_A))
    g1p, g2p = _pad_lora(p["rw_g1"][o], p["rw_g2"][o], width(LORA_G))
    first, low2 = [w1p, a1p, g1p], [p["rw_w0"][o], w2p, p["rw_a0"][o], a2p, g2p]
    if has_vres:
        v1p, v2p = _pad_lora(p["rw_v1"][o - 1], p["rw_v2"][o - 1], width(LORA_V))
        first.append(v1p)
        low2 += [p["rw_v0"][o - 1], v2p]
    rkv, mid = rwkv_in(x, g_pre, shift0, batch, seqlen, p["rw_mu"][o], wget(("rkv", o)),
                       jnp.concatenate(first, axis=1), has_vres)
    yg, s_new = rwkv_core(rkv, mid, low2, p["rw_kk"][o], p["rw_ka"][o], p["rw_rk"][o],
                          p["rw_lnx_g"][o], p["rw_lnx_b"][o], batch, seqlen, o, s0,
                          v_first if has_vres else None)
    shift_new = rmsnorm_rows(x.reshape(batch, seqlen, d)[:, -1], g_pre)
    x = odd_out(yg, x, wget(("wo", o)), p["norm_mix_post"][layer])
    return x, s_new, shift_new, rkv


def _trunk(x3, st_hgrn, k_cache, v_cache, st_rwkv, st_shift, p, bank):
    batch, seqlen, d = x3.shape
    x = x3.reshape(batch * seqlen, d)
    has_state = st_hgrn is not None
    hgrn_out, k_out, v_out, rwkv_out, shift_out = [], [], [], [], []
    v_first = None
    src = _weight_sources(p)
    wget = functools.partial(_bf16_weight, bank, src)
    nsteps = ffn_steps(batch * seqlen, D_FF)
    for layer in range(DEPTH):
        if layer % 2 == 0:
            e = layer // 2
            x, s_new, k_new, v_new = _even_layer(x, batch, seqlen, e, layer, p, wget, st_hgrn, k_cache, v_cache)
            hgrn_out.append(s_new)
            k_out.append(k_new)
            v_out.append(v_new)
        else:
            o = layer // 2
            x, s_new, sh_new, rkv = _odd_layer(
                x, batch, seqlen, o, layer, p, wget,
                st_shift[o] if has_state else None,
                st_rwkv,
                v_first)
            if o == 0:
                v_first = rkv
            rwkv_out.append(s_new)
            shift_out.append(sh_new)
        nl = layer + 1
        wanted = []
        if nl < DEPTH:
            wanted = [("up", nl), ("down", nl)]
            wanted += [("in", nl // 2), ("out", nl // 2)] if nl % 2 == 0 else [("rkv", nl // 2), ("wo", nl // 2)]
        jobs = [k for k in wanted
                if k not in bank and can_cast_in(nsteps, *src[k][0][0].shape[-2:])]
        x, cast = ffn(x, p["norm_ffn_pre"][layer], wget(("up", layer)), wget(("down", layer)),
                      p["norm_ffn_post"][layer], [src[k] for k in jobs])
        bank.update(zip(jobs, cast))
    return (x.reshape(batch, seqlen, d), jnp.stack(hgrn_out), jnp.stack(k_out), jnp.stack(v_out),
            jnp.stack(rwkv_out), jnp.stack(shift_out))


def kernel(x_prompt, x_sample, state_hgrn, cache_swa_k, cache_swa_v, state_rwkv, state_shift,
           norm_mix_pre, norm_mix_post, norm_ffn_pre, norm_ffn_post,
           w_in_even, w_out_even, hgrn_lb_raw, hgrn_norm_g, rel_bias, attn_sinks,
           rw_mu, rw_wr, rw_wk, rw_wv, rw_wo, rw_w0, rw_w1, rw_w2, rw_a0, rw_a1, rw_a2,
           rw_v0, rw_v1, rw_v2, rw_g1, rw_g2, rw_kk, rw_ka, rw_rk, rw_lnx_g, rw_lnx_b,
           w_up, w_down):
    p = {
        "norm_mix_pre": norm_mix_pre, "norm_mix_post": norm_mix_post,
        "norm_ffn_pre": norm_ffn_pre, "norm_ffn_post": norm_ffn_post,
        "w_in_even": w_in_even, "w_out_even": w_out_even,
        "hgrn_lb_raw": hgrn_lb_raw, "hgrn_norm_g": hgrn_norm_g,
        "rel_bias": rel_bias, "attn_sinks": attn_sinks,
        "rw_mu": rw_mu, "rw_wr": rw_wr, "rw_wk": rw_wk, "rw_wv": rw_wv, "rw_wo": rw_wo,
        "rw_w0": rw_w0, "rw_w1": rw_w1, "rw_w2": rw_w2, "rw_a0": rw_a0, "rw_a1": rw_a1, "rw_a2": rw_a2,
        "rw_v0": rw_v0, "rw_v1": rw_v1, "rw_v2": rw_v2, "rw_g1": rw_g1, "rw_g2": rw_g2,
        "rw_kk": rw_kk, "rw_ka": rw_ka, "rw_rk": rw_rk, "rw_lnx_g": rw_lnx_g, "rw_lnx_b": rw_lnx_b,
        "w_up": w_up, "w_down": w_down,
    }
    bank = {}
    y_p, hgrn_p, k_p, v_p, rwkv_p, shift_p = _trunk(x_prompt, None, None, None, None, None, p, bank)
    y_s, hgrn_s, k_s, v_s, rwkv_s, shift_s = _trunk(
        x_sample, state_hgrn, cache_swa_k, cache_swa_v, state_rwkv, state_shift, p, bank)
    return (y_p, y_s, hgrn_p, hgrn_s, k_p, k_s, v_p, v_s, rwkv_p, rwkv_s, shift_p, shift_s)
```

```python
import functools
import math

import numpy as np
import jax
import jax.numpy as jnp
from jax import lax
from jax.experimental import pallas as pl
from jax.experimental.pallas import tpu as pltpu

F32 = jnp.float32
BF16 = jnp.bfloat16

DEPTH = 4
N_EVEN = 2
N_ODD = 2
A_HEADS = 8
A_KDIM = 128
A_VDIM = 128
A_WIDTH = 1024
B_HEADS = 16
B_HEAD_DIM = 64
B_KV_HEADS = 4
B_GROUP = 4
B_WIDTH = 1024
B_KV_WIDTH = 256
WINDOW = 128
N_BUCKETS = 32
MAX_DISTANCE = 128
MASK_VALUE = -1e30
IN_A = 4096
IN_EVEN = 5632
C_HEAD = 64
C_HEADS = 32
GN_EPS = 64e-5
D_FF = 8192
NORM_EPS = 1e-6

LANES = 128
SUBLANES = 8
VMEM_LIMIT = 56 * 1024 * 1024

HGRN_CHUNK = 128
RWKV_CHUNK = 64
RWKV_TRI_BLOCK = 16


def _cparams(sem):
    return pltpu.CompilerParams(dimension_semantics=sem, vmem_limit_bytes=VMEM_LIMIT)


def _rms(x, g):
    return x * lax.rsqrt(jnp.mean(x * x, axis=-1, keepdims=True) + NORM_EPS) * g


def _dot(a, b):
    return jnp.dot(a.astype(BF16), b.astype(BF16), preferred_element_type=F32)


def _dot_nt(a, b):
    return lax.dot_general(a.astype(BF16), b.astype(BF16), (((1,), (1,)), ((), ())),
                           preferred_element_type=F32)


def _dot_tn(a, b):
    return lax.dot_general(a.astype(BF16), b.astype(BF16), (((0,), (0,)), ((), ())),
                           preferred_element_type=F32)


def _dot_shared(lhs_list, b):
    res = _dot(jnp.concatenate(lhs_list, axis=0), b)
    out, off = [], 0
    for a in lhs_list:
        out.append(res[off:off + a.shape[0]])
        off += a.shape[0]
    return out


def _pick_tile(m, cands):
    for c in cands:
        if m % c == 0:
            return c
    return m


def _norm_matmul_kernel(x_ref, g_ref, w_ref, o_ref, xn_ref):
    @pl.when(pl.program_id(1) == 0)
    def _():
        xn_ref[...] = _rms(x_ref[...], g_ref[...]).astype(BF16)

    o_ref[...] = jnp.dot(xn_ref[...], w_ref[...], preferred_element_type=F32)


def norm_matmul(x, g, w_bf16, tn=1408):
    m, d = x.shape
    n = w_bf16.shape[1]
    tm = _pick_tile(m, (1024, 512, 256, 128, 64, 32, 16, 8))
    nj = n // tn
    col = lambda i, j: jnp.where(i % 2 == 0, j, nj - 1 - j)
    return pl.pallas_call(
        _norm_matmul_kernel,
        grid=(m // tm, nj),
        in_specs=[pl.BlockSpec((tm, d), lambda i, j: (i, 0)),
                  pl.BlockSpec((1, d), lambda i, j: (0, 0)),
                  pl.BlockSpec((d, tn), lambda i, j: (0, col(i, j)))],
        out_specs=pl.BlockSpec((tm, tn), lambda i, j: (i, col(i, j))),
        out_shape=jax.ShapeDtypeStruct((m, n), F32),
        scratch_shapes=[pltpu.VMEM((tm, d), BF16)],
        compiler_params=_cparams(("parallel", "arbitrary")),
        name="norm_matmul",
    )(x, g.reshape(1, d), w_bf16)


def _ffn_kernel(*refs, cast_srcs):
    nsrc = sum(cast_srcs)
    x_ref, gpre_ref, wup_ref, wdn_ref, gpost_ref = refs[:5]
    src_refs = refs[5:5 + nsrc]
    o_ref = refs[5 + nsrc]
    cast_out = refs[6 + nsrc:6 + nsrc + len(cast_srcs)]
    xn_ref, acc_ref = refs[-2:]
    k = 0
    for out_ref, n in zip(cast_out, cast_srcs):
        for j in range(n):
            if n == 1:
                out_ref[...] = src_refs[k][...].astype(BF16)
            else:
                out_ref[j] = src_refs[k][...].astype(BF16)
            k += 1
    f = pl.program_id(1)

    @pl.when(f == 0)
    def _():
        xn_ref[...] = _rms(x_ref[...], gpre_ref[...]).astype(BF16)
        acc_ref[...] = jnp.zeros_like(acc_ref)

    h = jnp.dot(xn_ref[...], wup_ref[...], preferred_element_type=F32)
    h = jnp.square(jnp.maximum(h, 0.0)).astype(BF16)
    acc_ref[...] += jnp.dot(h, wdn_ref[...], preferred_element_type=F32)

    @pl.when(f == pl.num_programs(1) - 1)
    def _():
        o_ref[...] = x_ref[...] + _rms(acc_ref[...], gpost_ref[...])


BF16_SUBLANES = 16


def ffn_steps(m, dff, tf=1024):
    return (m // _pick_tile(m, (512, 256, 128, 64, 32, 16, 8))) * (dff // tf)


CAST_BLOCK_BYTES = 1 << 20


def can_cast_in(nsteps, rows, cols):
    return (rows % nsteps == 0 and (rows // nsteps) % BF16_SUBLANES == 0
            and (rows // nsteps) * cols * 4 <= CAST_BLOCK_BYTES)


def ffn(x, gpre, wup_bf16, wdn_bf16, gpost, casts=(), tf=1024):
    m, d = x.shape
    dff = wup_bf16.shape[1]
    tm = _pick_tile(m, (512, 256, 128, 64, 32, 16, 8))
    nf = dff // tf
    nsteps = (m // tm) * nf
    step = lambda i, f: i * nf + f
    in_specs = [pl.BlockSpec((tm, d), lambda i, f: (i, 0)),
                pl.BlockSpec((1, d), lambda i, f: (0, 0)),
                pl.BlockSpec((d, tf), lambda i, f: (0, f)),
                pl.BlockSpec((tf, d), lambda i, f: (f, 0)),
                pl.BlockSpec((1, d), lambda i, f: (0, 0))]
    args = [x, gpre.reshape(1, d), wup_bf16, wdn_bf16, gpost.reshape(1, d)]
    out_specs = [pl.BlockSpec((tm, d), lambda i, f: (i, 0))]
    out_shape = [jax.ShapeDtypeStruct((m, d), F32)]
    for job in casts:
        rows, cols = job[0][0].shape[len(job[0][1]):]
        assert can_cast_in(nsteps, rows, cols)
        rb = rows // nsteps
        for arr, lead in job:
            in_specs.append(pl.BlockSpec((None,) * len(lead) + (rb, cols),
                                         lambda i, f, lead=lead: lead + (step(i, f), 0)))
            args.append(arr)
        if len(job) == 1:
            out_specs.append(pl.BlockSpec((rb, cols), lambda i, f: (step(i, f), 0)))
            out_shape.append(jax.ShapeDtypeStruct((rows, cols), BF16))
        else:
            out_specs.append(pl.BlockSpec((len(job), rb, cols), lambda i, f: (0, step(i, f), 0)))
            out_shape.append(jax.ShapeDtypeStruct((len(job), rows, cols), BF16))
    outs = pl.pallas_call(
        functools.partial(_ffn_kernel, cast_srcs=tuple(len(job) for job in casts)),
        grid=(m // tm, nf),
        in_specs=in_specs,
        out_specs=out_specs,
        out_shape=out_shape,
        scratch_shapes=[pltpu.VMEM((tm, d), BF16), pltpu.VMEM((tm, d), F32)],
        compiler_params=_cparams(("parallel", "arbitrary")),
        name="ffn",
    )(*args)
    return outs[0], list(outs[1:])


def _even_out_kernel(oa_ref, ga_ref, ob_ref, x_ref, ag_ref, w_ref, gpost_ref, o_ref):
    ga = ga_ref[...]
    oan = _rms(oa_ref[...], ag_ref[...]) * (ga * jax.nn.sigmoid(ga))
    mix = (jnp.dot(oan.astype(BF16), w_ref[:A_WIDTH, :], preferred_element_type=F32)
           + jnp.dot(ob_ref[...].astype(BF16), w_ref[A_WIDTH:, :], preferred_element_type=F32))
    o_ref[...] = x_ref[...] + _rms(mix, gpost_ref[...])


def even_out(o_a, proj, o_b, x, a_norm_g, w_out_bf16, gpost):
    m, d = x.shape
    tm = _pick_tile(m, (512, 256, 128, 64, 32, 16, 8))
    ga_blk = 3
    return pl.pallas_call(
        _even_out_kernel,
        grid=(m // tm,),
        in_specs=[pl.BlockSpec((tm, A_WIDTH), lambda i: (i, 0)),
                  pl.BlockSpec((tm, A_WIDTH), lambda i: (i, ga_blk)),
                  pl.BlockSpec((tm, B_WIDTH), lambda i: (i, 0)),
                  pl.BlockSpec((tm, d), lambda i: (i, 0)),
                  pl.BlockSpec((1, A_WIDTH), lambda i: (0, 0)),
                  pl.BlockSpec((A_WIDTH + B_WIDTH, d), lambda i: (0, 0)),
                  pl.BlockSpec((1, d), lambda i: (0, 0))],
        out_specs=pl.BlockSpec((tm, d), lambda i: (i, 0)),
        out_shape=jax.ShapeDtypeStruct((m, d), F32),
        compiler_params=_cparams(("parallel",)),
        name="even_out",
    )(o_a, proj, o_b, x, a_norm_g.reshape(1, A_WIDTH), w_out_bf16, gpost.reshape(1, d))


def _odd_out_kernel(y_ref, x_ref, w_ref, gpost_ref, o_ref):
    mix = jnp.dot(y_ref[...].astype(BF16), w_ref[...], preferred_element_type=F32)
    o_ref[...] = x_ref[...] + _rms(mix, gpost_ref[...])


def odd_out(yg, x, wo_bf16, gpost):
    m, d = x.shape
    tm = _pick_tile(m, (512, 256, 128, 64, 32, 16, 8))
    return pl.pallas_call(
        _odd_out_kernel,
        grid=(m // tm,),
        in_specs=[pl.BlockSpec((tm, d), lambda i: (i, 0)),
                  pl.BlockSpec((tm, d), lambda i: (i, 0)),
                  pl.BlockSpec((d, d), lambda i: (0, 0)),
                  pl.BlockSpec((1, d), lambda i: (0, 0))],
        out_specs=pl.BlockSpec((tm, d), lambda i: (i, 0)),
        out_shape=jax.ShapeDtypeStruct((m, d), F32),
        compiler_params=_cparams(("parallel",)),
        name="odd_out",
    )(yg, x, wo_bf16, gpost.reshape(1, d))


def _level_consts(c):
    levels = []
    s = c // 2
    while s >= 1:
        levels.append(s)
        s //= 2
    mask = np.zeros((len(levels), c, c), np.float32)
    idx = np.arange(c)
    for l, s in enumerate(levels):
        same = (idx[:, None] // (2 * s)) == (idx[None, :] // (2 * s))
        upper = (idx[:, None] % (2 * s)) >= s
        lower = (idx[None, :] % (2 * s)) < s
        mask[l] = (same & upper & lower).astype(np.float32)
    return levels, mask


def _split_rows(g, s, rowid):
    c = g.shape[0]
    if 2 * s >= SUBLANES:
        return jnp.concatenate(
            [jnp.broadcast_to(g[b + s - 1:b + s, :], (2 * s, g.shape[1])) for b in range(0, c, 2 * s)], axis=0)
    r = rowid % (2 * s)
    out = g
    for off in range(-(s - 1), s + 1):
        if off != 0:
            out = jnp.where(r == s - 1 + off, pltpu.roll(g, off % c, 0), out)
    return out


def _hgrn_kernel(*refs, layer, chunk, nchunks, levels, has_state, nheads):
    if has_state:
        q_ref, f_ref, i_ref, lb_ref, mask_ref, s0_ref, o_ref, s_ref, st_ref = refs
    else:
        q_ref, f_ref, i_ref, lb_ref, mask_ref, o_ref, s_ref, st_ref = refs
    c = chunk
    nh = nheads
    rowid = lax.broadcasted_iota(jnp.int32, (c, nh * LANES), 0)
    l_idx = pl.program_id(2)

    @pl.when(l_idx == 0)
    def _():
        for hi in range(nh):
            if has_state:
                st_ref[hi] = s0_ref[0, hi].T
            else:
                st_ref[hi] = jnp.zeros((A_VDIM, A_KDIM), F32)

    lbr = lb_ref[...]
    e = jnp.exp(lbr - jnp.max(lbr, axis=0, keepdims=True))
    p = e / jnp.sum(e, axis=0, keepdims=True)
    lb = jnp.zeros((1, nh * LANES), F32)
    for i in range(1, layer + 1):
        lb = lb + p[i:i + 1, :]
    one_m_lb = 1.0 - lb
    head = lambda x, hi: x[:, hi * LANES:(hi + 1) * LANES]

    def body(ci, carry):
        rows = pl.ds(pl.multiple_of(ci * c, c), c)
        fq = f_ref[rows, :]
        qr = q_ref[rows, :]
        v = i_ref[rows, :]
        f = lb + one_m_lb * jax.nn.sigmoid(fq)
        k = 1.0 - f
        q = qr * jax.nn.sigmoid(qr) * (A_KDIM ** -0.5)

        g = jnp.log2(f)
        sft = 1
        while sft < c:
            g = g + jnp.where(rowid >= sft, pltpu.roll(g, sft, 0), 0.0)
            sft *= 2
        glast = g[c - 1:c, :]
        q_in = q * jnp.exp2(g)
        kd = k * jnp.exp2(glast - g)
        dec = jnp.exp2(glast)
        diag = q * k

        sts = [st_ref[hi] for hi in range(nh)]
        os_ = [_dot_nt(head(q_in, hi), sts[hi]) for hi in range(nh)]
        attns = [jnp.zeros((c, c), F32) for _ in range(nh)]
        for l, s in enumerate(levels):
            e = jnp.exp2(-jnp.abs(g - _split_rows(g, s, rowid)))
            qs = q * e
            ks = k * e
            ml = mask_ref[l]
            attns = [at + ml * _dot_nt(head(qs, hi), head(ks, hi)) for hi, at in enumerate(attns)]
        for hi in range(nh):
            vh = head(v, hi)
            o = os_[hi] + _dot(attns[hi], vh) + jnp.sum(head(diag, hi), axis=-1, keepdims=True) * vh
            o_ref[rows, hi * LANES:(hi + 1) * LANES] = o
        for hi in range(nh):
            st_ref[hi] = sts[hi] * head(dec, hi) + _dot_tn(head(v, hi), head(kd, hi))
        return carry

    lax.fori_loop(0, nchunks, body, 0, unroll=2 if nchunks % 2 == 0 else 1)

    @pl.when(l_idx == pl.num_programs(2) - 1)
    def _():
        for hi in range(nh):
            s_ref[0, hi] = st_ref[hi].T


def hgrn(proj, lb_raw, layer, batch, seqlen, s0=None, nheads=A_HEADS):
    m = proj.shape[0]
    c = math.gcd(seqlen, HGRN_CHUNK)
    lblk = _pick_tile(seqlen, (512, 256, 128, 64, 32, 16, 8))
    nl_blocks = seqlen // lblk
    levels, mask = _level_consts(c)
    has_state = s0 is not None
    if c * 4 <= HGRN_CHUNK:
        nheads = A_HEADS
    w = nheads * LANES
    ngroups = A_HEADS // nheads
    kern = functools.partial(_hgrn_kernel, layer=layer, chunk=c, nchunks=lblk // c,
                             levels=tuple(levels), has_state=has_state, nheads=nheads)
    row = lambda b, h, l: b * nl_blocks + l
    in_specs = [pl.BlockSpec((lblk, w), lambda b, h, l: (row(b, h, l), h)),
                pl.BlockSpec((lblk, w), lambda b, h, l: (row(b, h, l), ngroups + h)),
                pl.BlockSpec((lblk, w), lambda b, h, l: (row(b, h, l), 2 * ngroups + h)),
                pl.BlockSpec((N_EVEN, w), lambda b, h, l: (0, h)),
                pl.BlockSpec((len(levels), c, c), lambda b, h, l: (0, 0, 0))]
    args = [proj, proj, proj, lb_raw, jnp.asarray(mask)]
    st_blk = pl.BlockSpec((1, nheads, A_KDIM, A_VDIM), lambda b, h, l: (b, h, 0, 0))
    if has_state:
        in_specs.append(pl.BlockSpec((None, 1, nheads, A_KDIM, A_VDIM), lambda b, h, l: (layer, b, h, 0, 0)))
        args.append(s0)
    return pl.pallas_call(
        kern,
        grid=(batch, ngroups, nl_blocks),
        in_specs=in_specs,
        out_specs=[pl.BlockSpec((lblk, w), lambda b, h, l: (row(b, h, l), h)), st_blk],
        out_shape=[jax.ShapeDtypeStruct((m, A_WIDTH), F32),
                   jax.ShapeDtypeStruct((batch, A_HEADS, A_KDIM, A_VDIM), F32)],
        scratch_shapes=[pltpu.VMEM((nheads, A_VDIM, A_KDIM), F32)],
        compiler_params=_cparams(("parallel", "parallel", "arbitrary")),
        name="hgrn",
    )(*args)


def _t5_bucket(dist):
    max_exact = N_BUCKETS // 2
    d = np.maximum(dist, 0)
    large = max_exact + (np.log(np.maximum(d, max_exact).astype(np.float32) / max_exact)
                         / math.log(MAX_DISTANCE / max_exact) * (N_BUCKETS - max_exact)).astype(np.int32)
    large = np.minimum(large, N_BUCKETS - 1)
    return np.where(d < max_exact, d, large).astype(np.int32)


def _swa_kernel(q_ref, kp_ref, kc_ref, vp_ref, vc_ref, bucket_ref, band_ref, rb_ref, sink_ref,
                o_ref, bias_ref, *, qb, span, nsub, prev_always_valid):
    first = (pl.program_id(0) == 0) & (pl.program_id(1) == 0)

    @pl.when(first)
    def _():
        bk = bucket_ref[...]
        band = band_ref[...]

        def per_head(h, carry):
            def per_bucket(bi, acc):
                return jnp.where(bk == bi, rb_ref[bi, h], acc)
            acc = lax.fori_loop(0, N_BUCKETS, per_bucket, jnp.zeros((qb, span), F32))
            bias_ref[h] = jnp.where(band > 0, acc, MASK_VALUE)
            return carry

        lax.fori_loop(0, B_HEADS, per_head, 0)

    scale = B_HEAD_DIM ** -0.5
    q = q_ref[...]
    kfull = jnp.concatenate([kp_ref[...], kc_ref[...]], axis=0)
    vfull = jnp.concatenate([vp_ref[...], vc_ref[...]], axis=0)
    if not prev_always_valid:
        col = lax.broadcasted_iota(jnp.int32, (qb, span), 1)
        no_prev = (col < WINDOW) & (pl.program_id(1) == 0)
    jh = [(j, h) for j in range(nsub) for h in range(B_HEADS)]
    kv_head = lambda x, j, kh: x[j * qb:j * qb + span, kh * B_HEAD_DIM:(kh + 1) * B_HEAD_DIM].astype(BF16)
    ks = {(j, kh): kv_head(kfull, j, kh) for j in range(nsub) for kh in range(B_KV_HEADS)}
    vs = {(j, kh): kv_head(vfull, j, kh) for j in range(nsub) for kh in range(B_KV_HEADS)}
    qs = [(q[j * qb:(j + 1) * qb, h * B_HEAD_DIM:(h + 1) * B_HEAD_DIM] * scale).astype(BF16) for j, h in jh]
    ss = [_dot_nt(qv, ks[j, h // B_GROUP]) + bias_ref[h] for qv, (j, h) in zip(qs, jh)]
    if not prev_always_valid:
        ss = [jnp.where(no_prev, MASK_VALUE, s) if j == 0 else s for s, (j, h) in zip(ss, jh)]
    ms = [jnp.maximum(jnp.max(s, axis=-1, keepdims=True), sink_ref[h]) for s, (j, h) in zip(ss, jh)]
    ps = [jnp.exp(s - m) for s, m in zip(ss, ms)]
    denoms = [jnp.sum(p, axis=-1, keepdims=True) + jnp.exp(sink_ref[h] - m) for p, m, (j, h) in zip(ps, ms, jh)]
    outs = [_dot(p, vs[j, h // B_GROUP]) / dn for p, dn, (j, h) in zip(ps, denoms, jh)]
    for j in range(nsub):
        o_ref[j * qb:(j + 1) * qb, :] = jnp.concatenate(
            outs[j * B_HEADS:(j + 1) * B_HEADS], axis=1).astype(o_ref.dtype)


def swa(proj, batch, seqlen, rel_bias, sinks, layer, k_past=None, v_past=None):
    m = proj.shape[0]
    has_cache = k_past is not None
    qb = math.gcd(seqlen, WINDOW)
    nb = seqlen // qb
    span = WINDOW + qb
    dist = np.arange(qb)[:, None] + WINDOW - np.arange(span)[None, :]
    band = ((dist >= 0) & (dist < WINDOW)).astype(np.float32)
    bucket = _t5_bucket(dist)
    q_col = IN_A // B_WIDTH
    k_col = (IN_A + B_WIDTH) // B_KV_WIDTH
    v_col = k_col + 1
    nsub = 4 if (not has_cache and nb % 4 == 0) else 1
    ng = nb // nsub
    cur = lambda c: (lambda b, n: (b * ng + n, c))
    if has_cache:
        assert nb == 1
        prev_k = pl.BlockSpec((None, WINDOW, B_KV_WIDTH), lambda b, n: (layer, b, 0))
        prev_v = pl.BlockSpec((None, WINDOW, B_KV_WIDTH), lambda b, n: (layer, b, 0))
        kp_arr = k_past.reshape(k_past.shape[0], batch * WINDOW, B_KV_WIDTH)
        vp_arr = v_past.reshape(v_past.shape[0], batch * WINDOW, B_KV_WIDTH)
    else:
        assert qb == WINDOW
        prev = lambda c: (lambda b, n: (b * nb + jnp.maximum(n * nsub - 1, 0), c))
        prev_k = pl.BlockSpec((WINDOW, B_KV_WIDTH), prev(k_col))
        prev_v = pl.BlockSpec((WINDOW, B_KV_WIDTH), prev(v_col))
        kp_arr, vp_arr = proj, proj
    kern = functools.partial(_swa_kernel, qb=qb, span=span, nsub=nsub, prev_always_valid=has_cache)
    return pl.pallas_call(
        kern,
        grid=(batch, ng),
        in_specs=[pl.BlockSpec((nsub * qb, B_WIDTH), cur(q_col)),
                  prev_k,
                  pl.BlockSpec((nsub * qb, B_KV_WIDTH), cur(k_col)),
                  prev_v,
                  pl.BlockSpec((nsub * qb, B_KV_WIDTH), cur(v_col)),
                  pl.BlockSpec((qb, span), lambda b, n: (0, 0)),
                  pl.BlockSpec((qb, span), lambda b, n: (0, 0)),
                  pl.BlockSpec(memory_space=pltpu.SMEM),
                  pl.BlockSpec(memory_space=pltpu.SMEM)],
        out_specs=pl.BlockSpec((nsub * qb, B_WIDTH), lambda b, n: (b * ng + n, 0)),
        out_shape=jax.ShapeDtypeStruct((m, B_WIDTH), BF16 if qb % 16 == 0 else F32),
        scratch_shapes=[pltpu.VMEM((B_HEADS, qb, span), F32)],
        compiler_params=_cparams(("arbitrary", "arbitrary")),
        name="swa",
    )(proj, kp_arr, proj, vp_arr, proj, jnp.asarray(bucket), jnp.asarray(band), rel_bias, sinks)


def _rmsnorm_kernel(x_ref, g_ref, o_ref):
    o_ref[...] = _rms(x_ref[...], g_ref[...])


def rmsnorm_rows(x, g):
    m, d = x.shape
    tm = _pick_tile(m, (512, 256, 128, 64, 32, 16, 8))
    return pl.pallas_call(
        _rmsnorm_kernel,
        grid=(m // tm,),
        in_specs=[pl.BlockSpec((tm, d), lambda i: (i, 0)), pl.BlockSpec((1, d), lambda i: (0, 0))],
        out_specs=pl.BlockSpec((tm, d), lambda i: (i, 0)),
        out_shape=jax.ShapeDtypeStruct((m, d), F32),
        compiler_params=_cparams(("parallel",)),
        name="rmsnorm",
    )(x, g.reshape(1, d))


LORA_W = (0, 128)
LORA_A = (128, 256)
LORA_G = (256, 512)
LORA_V = (512, 640)


def _rwkv_in_phase(i, step):
    return jnp.where(i % 2 == 0, step, 3 - step)


def _rwkv_in_kernel(x_ref, xp_ref, s_ref, g_ref, mu3_ref, mul_ref, w_ref, w1_ref,
                    rkv_ref, mid_ref, h_s, hp_s, *, tm, seqlen, has_vres):
    step = pl.program_id(1)
    ph = _rwkv_in_phase(pl.program_id(0), step)

    @pl.when(step == 0)
    def _():
        g = g_ref[...]
        h = _rms(x_ref[...], g)
        rowid = lax.broadcasted_iota(jnp.int32, h.shape, 0)
        rolled = pltpu.roll(h, 1, 0)
        if seqlen % tm == 0:
            prev_last = _rms(xp_ref[...], g)[SUBLANES - 1:SUBLANES, :]
            at_start = pl.program_id(0) % (seqlen // tm) == 0
            first = jnp.where(at_start, s_ref[...], prev_last)
            hp = jnp.where(rowid == 0, first, rolled)
        else:
            hp = jnp.where(rowid % seqlen == 0, s_ref[...], rolled)
        h_s[...] = h
        hp_s[...] = hp

    @pl.when(ph < 3)
    def _():
        h = h_s[...]
        xm = (h + (hp_s[...] - h) * mu3_ref[0]).astype(BF16)
        rkv_ref[0] = jnp.dot(xm, w_ref[...], preferred_element_type=F32)

    @pl.when(ph == 3)
    def _():
        h = h_s[...]
        xx = hp_s[...] - h
        mix = lambda i: (h + xx * mul_ref[i:i + 1, :]).astype(BF16)
        low = lambda i, rng: jnp.dot(mix(i), w1_ref[:, rng[0]:rng[1]], preferred_element_type=F32)
        parts = [jnp.tanh(low(0, LORA_W)), low(1, LORA_A), jax.nn.sigmoid(low(2, LORA_G))]
        if has_vres:
            parts.append(low(3, LORA_V))
        mid_ref[...] = jnp.concatenate(parts, axis=1).astype(BF16)


def rwkv_in(x, g, shift0, batch, seqlen, mu, w3_bf16, w1cat_bf16, has_vres):
    m, d = x.shape
    tm = next(t for t in (512, 256, 128, 64, 32, 16, 8)
              if m % t == 0 and (seqlen % t == 0 or t % seqlen == 0))
    midw = w1cat_bf16.shape[1]
    mu_rkv = jnp.stack([mu[0], mu[2], mu[3]])[:, None, :]
    mu_low = jnp.stack([mu[1], mu[4], mu[5], mu[3]])
    if shift0 is None:
        shift0 = jnp.zeros((batch, d), F32)
    if seqlen % tm == 0:
        srow = shift0[:, None, :]
        tps = seqlen // tm
        s_spec = pl.BlockSpec((None, 1, d), lambda i, p: (i // tps, 0, 0))
    else:
        srow = jnp.repeat(shift0, seqlen, axis=0)
        s_spec = pl.BlockSpec((tm, d), lambda i, p: (i, 0))
    sub = tm // SUBLANES
    proj_of = lambda i, p: jnp.minimum(_rwkv_in_phase(i, p), 2)
    kern = functools.partial(_rwkv_in_kernel, tm=tm, seqlen=seqlen, has_vres=has_vres)
    return pl.pallas_call(
        kern,
        grid=(m // tm, 4),
        in_specs=[pl.BlockSpec((tm, d), lambda i, p: (i, 0)),
                  pl.BlockSpec((SUBLANES, d), lambda i, p: (jnp.maximum(i * sub - 1, 0), 0)),
                  s_spec,
                  pl.BlockSpec((1, d), lambda i, p: (0, 0)),
                  pl.BlockSpec((1, 1, d), lambda i, p: (proj_of(i, p), 0, 0)),
                  pl.BlockSpec((4, d), lambda i, p: (0, 0)),
                  pl.BlockSpec((None, d, d), lambda i, p: (proj_of(i, p), 0, 0)),
                  pl.BlockSpec((d, midw), lambda i, p: (0, 0))],
        out_specs=[pl.BlockSpec((1, tm, d), lambda i, p: (proj_of(i, p), i, 0)),
                   pl.BlockSpec((tm, midw), lambda i, p: (i, 0))],
        out_shape=[jax.ShapeDtypeStruct((3, m, d), F32),
                   jax.ShapeDtypeStruct((m, midw), BF16)],
        scratch_shapes=[pltpu.VMEM((tm, d), F32), pltpu.VMEM((tm, d), F32)],
        compiler_params=_cparams(("parallel", "arbitrary")),
        name="rwkv_in",
    )(x, x, srow, g.reshape(1, d), mu_rkv, mu_low, w3_bf16, w1cat_bf16)


def _pad_lora(w1, w2, width):
    r = w1.shape[1]
    return (jnp.pad(w1, ((0, 0), (0, width - r))).astype(BF16),
            jnp.pad(w2, ((0, width - r), (0, 0))).astype(BF16))


def _rwkv_consts(c):
    i = np.arange(2 * c)
    same = (i[:, None] // c) == (i[None, :] // c)
    strict = (same & ((i[:, None] % c) > (i[None, :] % c))).astype(np.float32)
    incl = (same & ((i[:, None] % c) >= (i[None, :] % c))).astype(np.float32)
    eye = np.eye(2 * c, dtype=np.float32)
    l = np.arange(LANES)
    headones = ((l[:, None] // C_HEAD) == (l[None, :] // C_HEAD)).astype(np.float32)
    bs = min(RWKV_TRI_BLOCK, c)
    blk = lambda n: (i[:, None] // n) == (i[None, :] // n)
    tmasks = [blk(bs)]
    s = bs
    while s < c:
        tmasks.append(blk(2 * s) & ~blk(s))
        s *= 2
    return strict, incl, eye, headones, np.stack(tmasks).astype(np.float32), bs


def _rwkv_core_kernel(*refs, chunk, nchunks, npairs, has_state, has_vres, bs):
    it = iter(refs)
    r_ref, k_ref, v_ref, mid_ref = (next(it) for _ in range(4))
    w0_ref, w2_ref, a0_ref, a2_ref, g2_ref = (next(it) for _ in range(5))
    if has_vres:
        vf_ref, v0_ref, v2_ref = next(it), next(it), next(it)
    kk_ref, ka_ref, rk_ref, lg_ref, lb_ref = (next(it) for _ in range(5))
    strict_ref, incl_ref, eye_ref, hones_ref, tmask_ref = (next(it) for _ in range(5))
    if has_state:
        s0_ref = next(it)
    y_ref, s_ref, st_ref, wl_ref, a_ref, g_ref = (next(it) for _ in range(6))
    if has_vres:
        vg_ref = next(it)
    c = chunk
    l_idx = pl.program_id(2)
    lane = lax.broadcasted_iota(jnp.int32, (1, LANES), 1)
    m0 = (lane < C_HEAD).astype(F32)
    m1 = 1.0 - m0

    low = lambda rng, w_ref: jnp.dot(mid_ref[:, rng[0]:rng[1]], w_ref[...], preferred_element_type=F32)
    sig = lambda z: 0.5 * jnp.tanh(0.5 * z) + 0.5
    wl_ref[...] = -math.exp(-0.5) * sig(w0_ref[...] + low(LORA_W, w2_ref))
    a_ref[...] = sig(a0_ref[...] + low(LORA_A, a2_ref))
    g_ref[...] = low(LORA_G, g2_ref)
    if has_vres:
        vg_ref[...] = sig(v0_ref[...] + low(LORA_V, v2_ref))

    @pl.when(l_idx == 0)
    def _():
        for pi in range(npairs):
            if has_state:
                z = jnp.zeros((C_HEAD, C_HEAD), F32)
                top = jnp.concatenate([s0_ref[0, 2 * pi], z], axis=1)
                bot = jnp.concatenate([z, s0_ref[0, 2 * pi + 1]], axis=1)
                st_ref[pi] = jnp.concatenate([top, bot], axis=0)
            else:
                st_ref[pi] = jnp.zeros((LANES, LANES), F32)

    strict = strict_ref[...]
    incl = incl_ref[...]
    eye = eye_ref[...]

    def stack(x):
        return jnp.concatenate([x * m0, x * m1], axis=0)

    first_head = lax.broadcasted_iota(jnp.int32, (c, LANES), 1) < C_HEAD
    rowid = lax.broadcasted_iota(jnp.int32, (c, LANES), 0)

    def rowsums(xs):
        return [jnp.where(first_head,
                          jnp.sum(x * m0, axis=-1, keepdims=True),
                          jnp.sum(x * m1, axis=-1, keepdims=True)) for x in xs]

    def cumsum_rows(x):
        s = 1
        while s < c:
            x = x + jnp.where(rowid >= s, pltpu.roll(x, s, 0), 0.0)
            s *= 2
        return x

    def load(pi, rows):
        cols = slice(pi * LANES, (pi + 1) * LANES)
        k = k_ref[0, rows, cols]
        v = v_ref[0, rows, cols]
        a = a_ref[rows, cols]
        if has_vres:
            v = v + (vf_ref[0, rows, cols] - v) * vg_ref[rows, cols]
        return dict(cols=cols, r=r_ref[0, rows, cols], v=v, a=a, wl=wl_ref[rows, cols],
                    kr=k * kk_ref[:, cols], kh=k * (1.0 + (a - 1.0) * ka_ref[:, cols]))

    def decays(p, ss):
        kk = p["kr"] * lax.rsqrt(jnp.maximum(ss, 1e-24))
        b = kk * p["a"]
        gc = cumsum_rows(p["wl"])
        gl = gc[c - 1:c, :]
        e_neg = jnp.exp(-gc)
        e_out = jnp.exp(gl - gc)
        p.update(gl=gl, ab=-kk * jnp.exp(gc - p["wl"]), rb=p["r"] * jnp.exp(gc),
                 bt=b * e_neg, kt=p["kh"] * e_neg, bh=b * e_out, khat=p["kh"] * e_out)

    def intra(p):
        lhs = jnp.concatenate([stack(p["ab"]), stack(p["rb"])], axis=0)
        with_b = _dot_nt(lhs, jnp.concatenate([p["bt"], p["bt"]], axis=0))
        with_k = _dot_nt(lhs, jnp.concatenate([p["kt"], p["kt"]], axis=0))
        p.update(a_ab=with_b[:2 * c] * strict, a_rb=with_b[2 * c:] * incl,
                 a_ak=with_k[:2 * c] * strict, a_rk=with_k[2 * c:] * incl)

    def body(ci, carry):
        rows = pl.ds(pl.multiple_of(ci * c, c), c)
        ps = [load(pi, rows) for pi in range(npairs)]
        for p, ss in zip(ps, rowsums([p["kr"] * p["kr"] for p in ps])):
            decays(p, ss)
        for p in ps:
            intra(p)

        pws = [p["a_ab"] * tmask_ref[0] for p in ps]
        ts = [eye + pw for pw in pws]
        if bs > 2:
            pws = [_dot(pw, pw) for pw in pws]
            n = 2
            while 2 * n < bs:
                res = [_dot_shared([t, pw], pw) for t, pw in zip(ts, pws)]
                ts = [t + r[0] for t, r in zip(ts, res)]
                pws = [r[1] for r in res]
                n *= 2
            ts = [t + _dot(t, pw) for t, pw in zip(ts, pws)]
        s, lvl = bs, 1
        while s < c:
            ms = [_dot(t, p["a_ab"] * tmask_ref[lvl]) for t, p in zip(ts, ps)]
            ts = [t + _dot(m, t) for t, m in zip(ts, ms)]
            s, lvl = 2 * s, lvl + 1

        sts = [st_ref[pi] for pi in range(npairs)]
        fss = [_dot_nt(jnp.concatenate([p["ab"], p["rb"]], axis=0), st) for p, st in zip(ps, sts)]
        vss = [stack(p["v"]) for p in ps]
        rhss = [stack(fs[:c]) + _dot(p["a_ak"], vs) for p, fs, vs in zip(ps, fss, vss)]
        uss = [_dot(t, rhs) for t, rhs in zip(ts, rhss)]
        yss = [_dot(p["a_rb"], us) + _dot(p["a_rk"], vs) for p, us, vs in zip(ps, uss, vss)]
        for pi, (p, st, us) in enumerate(zip(ps, sts, uss)):
            u = us[:c] + us[c:]
            upd = _dot_tn(jnp.concatenate([u, p["v"]], axis=0),
                          jnp.concatenate([p["bh"], p["khat"]], axis=0))
            st_ref[pi] = st * jnp.exp(p["gl"]) + upd * hones_ref[...]
        inv_n = 1.0 / C_HEAD
        ys_ = [fs[c:] + ys[:c] + ys[c:] for fs, ys in zip(fss, yss)]
        sums = rowsums(ys_ + [p["r"] * p["kh"] * rk_ref[:, p["cols"]] for p in ps])
        dlts = [y - m * inv_n for y, m in zip(ys_, sums[:npairs])]
        vars_ = rowsums([d * d for d in dlts])
        for p, dlt, var, bsum in zip(ps, dlts, vars_, sums[npairs:]):
            cols = p["cols"]
            yn = dlt * lax.rsqrt(var * inv_n + GN_EPS) * lg_ref[:, cols] + lb_ref[:, cols]
            y_ref[rows, cols] = ((yn + bsum * p["v"]) * g_ref[rows, cols]).astype(y_ref.dtype)
        return carry

    lax.fori_loop(0, nchunks, body, 0)

    @pl.when(l_idx == pl.num_programs(2) - 1)
    def _():
        for pi in range(npairs):
            st = st_ref[pi]
            s_ref[0, 2 * pi] = st[:C_HEAD, :C_HEAD]
            s_ref[0, 2 * pi + 1] = st[C_HEAD:, C_HEAD:]


def rwkv_core(rkv, mid, low2, kk_p, ka_p, rk_p, lnx_g, lnx_b, batch, seqlen, layer,
              s0=None, v_first=None):
    _, m, d = rkv.shape
    c = math.gcd(seqlen, RWKV_CHUNK)
    npairs = C_HEADS // 2
    lblk = _pick_tile(seqlen, (256, 128, 64, 32, 16, 8))
    nl_blocks = seqlen // lblk
    has_state = s0 is not None
    has_vres = v_first is not None
    w = npairs * LANES
    ngroups = d // w
    midw = mid.shape[1]
    *consts, bs = _rwkv_consts(c)
    strict, incl, eye, hones, tmasks = (jnp.asarray(x) for x in consts)
    row = lambda b, p, l: b * nl_blocks + l
    blk3 = lambda which: pl.BlockSpec((1, lblk, w), lambda b, p, l: (which, row(b, p, l), p))
    blk2 = pl.BlockSpec((lblk, w), lambda b, p, l: (row(b, p, l), p))
    par = pl.BlockSpec((1, w), lambda b, p, l: (0, p))
    cols = lambda arr: pl.BlockSpec((arr.shape[0], w), lambda b, p, l: (0, p))
    full = lambda arr: pl.BlockSpec(arr.shape, lambda b, p, l: (0,) * arr.ndim)
    w0, w2, a0, a2, g2 = low2[:5]
    in_specs = [blk3(0), blk3(1), blk3(2), pl.BlockSpec((lblk, midw), lambda b, p, l: (row(b, p, l), 0)),
                par, cols(w2), par, cols(a2), cols(g2)]
    args = [rkv, rkv, rkv, mid, w0.reshape(1, d), w2, a0.reshape(1, d), a2, g2]
    if has_vres:
        v0, v2 = low2[5:]
        in_specs += [blk3(2), par, cols(v2)]
        args += [v_first, v0.reshape(1, d), v2]
    in_specs += [par] * 5
    args += [x.reshape(1, d) for x in (kk_p, ka_p, rk_p, lnx_g, lnx_b)]
    in_specs += [full(x) for x in (strict, incl, eye, hones, tmasks)]
    args += [strict, incl, eye, hones, tmasks]
    st_blk = pl.BlockSpec((1, 2 * npairs, C_HEAD, C_HEAD), lambda b, p, l: (b, p, 0, 0))
    if has_state:
        in_specs.append(pl.BlockSpec((None, 1, 2 * npairs, C_HEAD, C_HEAD),
                                     lambda b, p, l: (layer, b, p, 0, 0)))
        args.append(s0)
    kern = functools.partial(_rwkv_core_kernel, chunk=c, nchunks=lblk // c, npairs=npairs,
                             has_state=has_state, has_vres=has_vres, bs=bs)
    return pl.pallas_call(
        kern,
        grid=(batch, ngroups, nl_blocks),
        in_specs=in_specs,
        out_specs=[blk2, st_blk],
        out_shape=[jax.ShapeDtypeStruct((m, d), BF16 if lblk % 16 == 0 else F32),
                   jax.ShapeDtypeStruct((batch, C_HEADS, C_HEAD, C_HEAD), F32)],
        scratch_shapes=[pltpu.VMEM((npairs, LANES, LANES), F32)]
        + [pltpu.VMEM((lblk, w), F32)] * (4 if has_vres else 3),
        compiler_params=_cparams(("parallel", "parallel", "arbitrary")),
        name="rwkv_core",
    )(*args)


def _weight_sources(p):
    src = {}
    for l in range(DEPTH):
        src[("up", l)] = [(p["w_up"], (l,))]
        src[("down", l)] = [(p["w_down"], (l,))]
    for e in range(N_EVEN):
        src[("in", e)] = [(p["w_in_even"], (e,))]
        src[("out", e)] = [(p["w_out_even"], (e,))]
    for o in range(N_ODD):
        src[("rkv", o)] = [(p["rw_wr"], (o,)), (p["rw_wk"], (o,)), (p["rw_wv"], (o,))]
        src[("wo", o)] = [(p["rw_wo"], (o,))]
    return src


def _bf16_weight(bank, src, key):
    if key not in bank:
        parts = [arr[lead].astype(BF16) for arr, lead in src[key]]
        bank[key] = parts[0] if len(parts) == 1 else jnp.stack(parts)
    return bank[key]


def _even_layer(x, batch, seqlen, e, layer, p, wget, st_hgrn, k_cache, v_cache):
    proj = norm_matmul(x, p["norm_mix_pre"][layer], wget(("in", e)))
    o_a, s_new = hgrn(proj, p["hgrn_lb_raw"], e, batch, seqlen, st_hgrn)
    k_lo = IN_A + B_WIDTH
    new_rows = min(seqlen, WINDOW)
    tails = jnp.stack([proj[(b + 1) * seqlen - new_rows:(b + 1) * seqlen, k_lo:] for b in range(batch)])
    kb = tails[:, :, :B_KV_WIDTH].reshape(batch, new_rows, B_KV_HEADS, B_HEAD_DIM)
    vb = tails[:, :, B_KV_WIDTH:].reshape(batch, new_rows, B_KV_HEADS, B_HEAD_DIM)
    o_b = swa(proj, batch, seqlen, p["rel_bias"], p["attn_sinks"][e], e, k_cache, v_cache)
    if k_cache is None:
        k_new, v_new = kb, vb
    else:
        k_new = jnp.concatenate([k_cache[e, :, new_rows:], kb], axis=1)
        v_new = jnp.concatenate([v_cache[e, :, new_rows:], vb], axis=1)
    x = even_out(o_a, proj, o_b, x, p["hgrn_norm_g"][e], wget(("out", e)), p["norm_mix_post"][layer])
    return x, s_new, k_new, v_new


def _odd_layer(x, batch, seqlen, o, layer, p, wget, shift0, s0, v_first):
    m, d = x.shape
    g_pre = p["norm_mix_pre"][layer]
    has_vres = o > 0
    width = lambda rng: rng[1] - rng[0]
    w1p, w2p = _pad_lora(p["rw_w1"][o], p["rw_w2"][o], width(LORA_W))
    a1p, a2p = _pad_lora(p["rw_a1"][o], p["rw_a2"][o], width(LORA_A))
    g1p, g2p = _pad_lora(p["rw_g1"][o], p["rw_g2"][o], width(LORA_G))
    first, low2 = [w1p, a1p, g1p], [p["rw_w0"][o], w2p, p["rw_a0"][o], a2p, g2p]
    if has_vres:
        v1p, v2p = _pad_lora(p["rw_v1"][o - 1], p["rw_v2"][o - 1], width(LORA_V))
        first.append(v1p)
        low2 += [p["rw_v0"][o - 1], v2p]
    rkv, mid = rwkv_in(x, g_pre, shift0, batch, seqlen, p["rw_mu"][o], wget(("rkv", o)),
                       jnp.concatenate(first, axis=1), has_vres)
    yg, s_new = rwkv_core(rkv, mid, low2, p["rw_kk"][o], p["rw_ka"][o], p["rw_rk"][o],
                          p["rw_lnx_g"][o], p["rw_lnx_b"][o], batch, seqlen, o, s0,
                          v_first if has_vres else None)
    shift_new = rmsnorm_rows(x.reshape(batch, seqlen, d)[:, -1], g_pre)
    x = odd_out(yg, x, wget(("wo", o)), p["norm_mix_post"][layer])
    return x, s_new, shift_new, rkv


def _trunk(x3, st_hgrn, k_cache, v_cache, st_rwkv, st_shift, p, bank):
    batch, seqlen, d = x3.shape
    x = x3.reshape(batch * seqlen, d)
    has_state = st_hgrn is not None
    hgrn_out, k_out, v_out, rwkv_out, shift_out = [], [], [], [], []
    v_first = None
    src = _weight_sources(p)
    wget = functools.partial(_bf16_weight, bank, src)
    nsteps = ffn_steps(batch * seqlen, D_FF)
    for layer in range(DEPTH):
        if layer % 2 == 0:
            e = layer // 2
            x, s_new, k_new, v_new = _even_layer(x, batch, seqlen, e, layer, p, wget, st_hgrn, k_cache, v_cache)
            hgrn_out.append(s_new)
            k_out.append(k_new)
            v_out.append(v_new)
        else:
            o = layer // 2
            x, s_new, sh_new, rkv = _odd_layer(
                x, batch, seqlen, o, layer, p, wget,
                st_shift[o] if has_state else None,
                st_rwkv,
                v_first)
            if o == 0:
                v_first = rkv
            rwkv_out.append(s_new)
            shift_out.append(sh_new)
        nl = layer + 1
        wanted = []
        if nl < DEPTH:
            wanted = [("up", nl), ("down", nl)]
            wanted += [("in", nl // 2), ("out", nl // 2)] if nl % 2 == 0 else [("rkv", nl // 2), ("wo", nl // 2)]
        jobs = [k for k in wanted
                if k not in bank and can_cast_in(nsteps, *src[k][0][0].shape[-2:])]
        x, cast = ffn(x, p["norm_ffn_pre"][layer], wget(("up", layer)), wget(("down", layer)),
                      p["norm_ffn_post"][layer], [src[k] for k in jobs])
        bank.update(zip(jobs, cast))
    return (x.reshape(batch, seqlen, d), jnp.stack(hgrn_out), jnp.stack(k_out), jnp.stack(v_out),
            jnp.stack(rwkv_out), jnp.stack(shift_out))


def kernel(x_prompt, x_sample, state_hgrn, cache_swa_k, cache_swa_v, state_rwkv, state_shift,
           norm_mix_pre, norm_mix_post, norm_ffn_pre, norm_ffn_post,
           w_in_even, w_out_even, hgrn_lb_raw, hgrn_norm_g, rel_bias, attn_sinks,
           rw_mu, rw_wr, rw_wk, rw_wv, rw_wo, rw_w0, rw_w1, rw_w2, rw_a0, rw_a1, rw_a2,
           rw_v0, rw_v1, rw_v2, rw_g1, rw_g2, rw_kk, rw_ka, rw_rk, rw_lnx_g, rw_lnx_b,
           w_up, w_down):
    p = {
        "norm_mix_pre": norm_mix_pre, "norm_mix_post": norm_mix_post,
        "norm_ffn_pre": norm_ffn_pre, "norm_ffn_post": norm_ffn_post,
        "w_in_even": w_in_even, "w_out_even": w_out_even,
        "hgrn_lb_raw": hgrn_lb_raw, "hgrn_norm_g": hgrn_norm_g,
        "rel_bias": rel_bias, "attn_sinks": attn_sinks,
        "rw_mu": rw_mu, "rw_wr": rw_wr, "rw_wk": rw_wk, "rw_wv": rw_wv, "rw_wo": rw_wo,
        "rw_w0": rw_w0, "rw_w1": rw_w1, "rw_w2": rw_w2, "rw_a0": rw_a0, "rw_a1": rw_a1, "rw_a2": rw_a2,
        "rw_v0": rw_v0, "rw_v1": rw_v1, "rw_v2": rw_v2, "rw_g1": rw_g1, "rw_g2": rw_g2,
        "rw_kk": rw_kk, "rw_ka": rw_ka, "rw_rk": rw_rk, "rw_lnx_g": rw_lnx_g, "rw_lnx_b": rw_lnx_b,
        "w_up": w_up, "w_down": w_down,
    }
    bank = {}
    y_p, hgrn_p, k_p, v_p, rwkv_p, shift_p = _trunk(x_prompt, None, None, None, None, None, p, bank)
    y_s, hgrn_s, k_s, v_s, rwkv_s, shift_s = _trunk(
        x_sample, state_hgrn, cache_swa_k, cache_swa_v, state_rwkv, state_shift, p, bank)
    return (y_p, y_s, hgrn_p, hgrn_s, k_p, k_s, v_p, v_s, rwkv_p, rwkv_s, shift_p, shift_s)
```
